```python
import jax, jax.numpy as jnp
from jax import lax
import numpy as np

D_MODEL = 1024
BATCH = 2
SEQ = 16384
DEPTH = 4

ATTN_HEAD_DIM = 64
ATTN_HEADS = D_MODEL // 128
ATTN_KV_HEADS = ATTN_HEADS // 4
ROPE_DIM = ATTN_HEAD_DIM // 4
ROPE_THETA = 500000.0
WINDOW = 128
ATTN_BLOCK = 128
MLSTM_HEAD_DIM = 128
MLSTM_HEADS = D_MODEL // 256
MLSTM_CHUNK = 64
CONV_WIDTH = 4
N_EXPERTS = 16
N_GROUPS = 4
EXPERTS_PER_GROUP = N_EXPERTS // N_GROUPS
TOP_K = 2
D_FF_EXPERT = D_MODEL // 2
MOE_BLOCK = 256
EPS = 1e-6

ATTN_Q_W = ATTN_HEADS * ATTN_HEAD_DIM
ATTN_KV_W = ATTN_KV_HEADS * ATTN_HEAD_DIM
MLSTM_W = MLSTM_HEADS * MLSTM_HEAD_DIM
IN_WIDTHS = (ATTN_Q_W, ATTN_KV_W, ATTN_KV_W, 2 * MLSTM_W, MLSTM_W, MLSTM_HEADS, MLSTM_HEADS, MLSTM_W, D_MODEL, D_MODEL)
D_IN = ATTN_Q_W + 2 * ATTN_KV_W + 4 * MLSTM_W + 2 * MLSTM_HEADS + 2 * D_MODEL

kernel_name = "hybrid_swa_mlstm_grouped_moe_trunk"


def rms_norm(x, w):
    xf = x.astype(jnp.float32)
    y = xf * lax.rsqrt(jnp.mean(xf * xf, axis=-1, keepdims=True) + EPS)
    return (y * w.astype(jnp.float32)).astype(x.dtype)


def split_cols(p):
    outs = []
    start = 0
    for w in IN_WIDTHS:
        outs.append(p[..., start:start + w])
        start += w
    return outs


def partial_rope(t, cos, sin):
    half = ROPE_DIM // 2
    tf = t.astype(jnp.float32)
    t1 = tf[..., :half]
    t2 = tf[..., half:ROPE_DIM]
    out = jnp.concatenate([t1 * cos - t2 * sin, t2 * cos + t1 * sin, tf[..., ROPE_DIM:]], axis=-1)
    return out.astype(t.dtype)


def sliding_window_attention(q, k, v, sinks):
    B, S, Hq, hd = q.shape
    Hkv = k.shape[2]
    G = Hq // Hkv
    nb = S // ATTN_BLOCK
    f32 = jnp.float32
    qb = q.astype(f32).reshape(B, nb, ATTN_BLOCK, Hkv, G, hd)

    def band(t):
        tb = t.astype(f32).reshape(B, nb, ATTN_BLOCK, Hkv, hd)
        prev = jnp.concatenate([jnp.zeros_like(tb[:, :1]), tb[:, :-1]], axis=1)
        return jnp.concatenate([prev, tb], axis=2)

    kb, vb = band(k), band(v)
    s = jnp.einsum('bnqhgd,bnkhd->bhgnqk', qb, kb) * (hd ** -0.5)
    qi = jnp.arange(ATTN_BLOCK)[:, None]
    kj = jnp.arange(2 * ATTN_BLOCK)[None, :]
    dist = qi + ATTN_BLOCK - kj
    in_window = (dist >= 0) & (dist < WINDOW)
    has_prev = (jnp.arange(nb)[:, None, None] > 0) | (kj >= ATTN_BLOCK)[None]
    mask = in_window[None] & has_prev
    s = jnp.where(mask, s, -jnp.inf)
    sink = sinks.astype(f32).reshape(1, Hkv, G, 1, 1)
    m = jnp.maximum(jnp.max(s, axis=-1), sink)
    p = jnp.exp(s - m[..., None])
    den = jnp.sum(p, axis=-1) + jnp.exp(sink - m)
    o = jnp.einsum('bhgnqk,bnkhd->bnqhgd', p / den[..., None], vb)
    return o.reshape(B, S, Hq * hd).astype(q.dtype)


def causal_depthwise_conv(u, w, b):
    S = u.shape[1]
    K = w.shape[0]
    up = jnp.pad(u, ((0, 0), (K - 1, 0), (0, 0)))
    out = b
    for j in range(K):
        out = out + up[:, j:j + S] * w[j]
    return out


def mlstm_chunkwise(q, k, v, i_pre, f_pre):
    B, S, H, Dh = q.shape
    Dv = v.shape[-1]
    L = MLSTM_CHUNK
    nc = S // L
    f32 = jnp.float32

    def to_chunks(t):
        return t.astype(f32).reshape(B, nc, L, H, -1).transpose(0, 3, 1, 2, 4)

    qc, kc, vc = to_chunks(q), to_chunks(k), to_chunks(v)
    log_i = i_pre.astype(f32).reshape(B, nc, L, H).transpose(0, 3, 1, 2)
    log_f = jax.nn.log_sigmoid(f_pre.astype(f32)).reshape(B, nc, L, H).transpose(0, 3, 1, 2)
    b = jnp.cumsum(log_f, axis=-1)
    b_last = b[..., -1]

    w_state = b_last[..., None] - b + log_i
    m_loc = jnp.max(w_state, axis=-1)
    e_state = jnp.exp(w_state - m_loc[..., None])
    kv_chunk = jnp.einsum('bhcld,bhcle->bhcde', kc * e_state[..., None], vc)
    n_chunk = jnp.einsum('bhcl,bhcld->bhcd', e_state, kc)

    def step(carry, inp):
        C, n, m = carry
        kv_c, n_c, m_c, bl = inp
        m_new = jnp.maximum(bl + m, m_c)
        a = jnp.exp(bl + m - m_new)
        s = jnp.exp(m_c - m_new)
        C_new = a[..., None, None] * C + s[..., None, None] * kv_c
        n_new = a[..., None] * n + s[..., None] * n_c
        return (C_new, n_new, m_new), (C, n, m)

    init = (jnp.zeros((B, H, Dh, Dv), f32), jnp.zeros((B, H, Dh), f32), jnp.zeros((B, H), f32))
    xs = (jnp.moveaxis(kv_chunk, 2, 0), jnp.moveaxis(n_chunk, 2, 0),
          jnp.moveaxis(m_loc, 2, 0), jnp.moveaxis(b_last, 2, 0))
    _, (C_prev, n_prev, m_prev) = lax.scan(step, init, xs)
    C_prev = jnp.moveaxis(C_prev, 0, 2)
    n_prev = jnp.moveaxis(n_prev, 0, 2)
    m_prev = jnp.moveaxis(m_prev, 0, 2)

    causal = jnp.tril(jnp.ones((L, L), dtype=bool))
    d_mat = b[..., :, None] - b[..., None, :] + log_i[..., None, :]
    d_mat = jnp.where(causal, d_mat, -jnp.inf)
    g = b + m_prev[..., None]
    m_t = jnp.maximum(jnp.max(d_mat, axis=-1), g)
    p = jnp.exp(d_mat - m_t[..., None])
    qk = jnp.einsum('bhctd,bhcsd->bhcts', qc, kc) * p
    inter = jnp.exp(g - m_t)
    num = jnp.einsum('bhcts,bhcse->bhcte', qk, vc) + inter[..., None] * jnp.einsum('bhctd,bhcde->bhcte', qc, C_prev)
    den = jnp.sum(qk, axis=-1) + inter * jnp.einsum('bhctd,bhcd->bhct', qc, n_prev)
    den = jnp.maximum(jnp.abs(den), jnp.exp(-m_t))
    h = num / den[..., None]
    return h.transpose(0, 2, 3, 1, 4).reshape(B, S, H, Dv).astype(q.dtype)


def grouped_moe(h, router_w, router_bias, w_gate, w_up, w_down):
    B, S, D = h.shape
    T = B * S
    E = N_EXPERTS
    f32 = jnp.float32
    xt = h.reshape(T, D)
    scores = jax.nn.sigmoid(xt.astype(f32) @ router_w.astype(f32))
    sel = scores + router_bias.astype(f32)
    grp = sel.reshape(T, N_GROUPS, EXPERTS_PER_GROUP)
    grp_score = jnp.sum(lax.top_k(grp, TOP_K)[0], axis=-1)
    g_idx = jnp.argmax(grp_score, axis=-1)
    in_grp = jnp.take_along_axis(grp, g_idx[:, None, None], axis=1)[:, 0]
    _, local = lax.top_k(in_grp, TOP_K)
    expert_idx = (g_idx[:, None] * EXPERTS_PER_GROUP + local).astype(jnp.int32)
    gate = jnp.take_along_axis(scores, expert_idx, axis=1)
    gate = gate / jnp.sum(gate, axis=-1, keepdims=True)

    n_assign = T * TOP_K
    flat_e = expert_idx.reshape(-1)
    flat_tok = jnp.arange(n_assign, dtype=jnp.int32) // TOP_K
    order = jnp.argsort(flat_e)
    e_sorted = flat_e[order]
    tok_sorted = flat_tok[order]
    gate_sorted = gate.reshape(-1)[order]
    counts = jnp.zeros((E,), jnp.int32).at[flat_e].add(1)
    starts = jnp.cumsum(counts) - counts
    padded = (counts + MOE_BLOCK - 1) // MOE_BLOCK * MOE_BLOCK
    pad_ends = jnp.cumsum(padded)
    pad_starts = pad_ends - padded
    dest = pad_starts[e_sorted] + jnp.arange(n_assign, dtype=jnp.int32) - starts[e_sorted]
    n_blocks = (n_assign + E * (MOE_BLOCK - 1) + MOE_BLOCK - 1) // MOE_BLOCK
    x_pad = jnp.zeros((n_blocks * MOE_BLOCK, D), xt.dtype).at[dest].set(xt[tok_sorted])
    blk_start = jnp.arange(n_blocks, dtype=jnp.int32) * MOE_BLOCK
    blk_e = jnp.minimum(jnp.searchsorted(pad_ends, blk_start, side='right'), E - 1)

    def expert_block(args):
        xb, e = args
        return (jax.nn.silu(xb @ w_gate[e]) * (xb @ w_up[e])) @ w_down[e]

    y_pad = lax.map(expert_block, (x_pad.reshape(n_blocks, MOE_BLOCK, D), blk_e)).reshape(-1, D)
    y = jax.ops.segment_sum(y_pad[dest] * gate_sorted[:, None].astype(y_pad.dtype), tok_sorted, num_segments=T)
    return y.reshape(B, S, D)


def setup_inputs(seed: int = 0) -> dict:
    key = jax.random.key(seed)
    ks = jax.random.split(key, 26)
    nrm = jax.random.normal
    D = D_MODEL
    E = N_EXPERTS
    F = D_FF_EXPERT
    x = nrm(ks[0], (BATCH, SEQ, D), jnp.float32)
    c = nrm(ks[1], (BATCH, D), jnp.float32)
    offs = jax.random.randint(ks[2], (BATCH, 1), 0, 4096, dtype=jnp.int32)
    positions = (jnp.arange(SEQ, dtype=jnp.int32)[None, :] + offs).astype(jnp.int32)
    ada_w = nrm(ks[3], (DEPTH, D, 6 * D)) * (0.5 * D ** -0.5)
    ada_b = 0.02 * nrm(ks[4], (DEPTH, 6 * D))
    norm_mix_w = 1.0 + 0.05 * nrm(ks[5], (DEPTH, D))
    norm_ffn_w = 1.0 + 0.05 * nrm(ks[6], (DEPTH, D))
    w_in = nrm(ks[7], (DEPTH, D, D_IN)) * D ** -0.5
    b_igate = 0.1 * nrm(ks[8], (DEPTH, MLSTM_HEADS))
    b_fgate = jnp.linspace(3.0, 6.0, MLSTM_HEADS)[None, :] + 0.1 * nrm(ks[9], (DEPTH, MLSTM_HEADS))
    q_norm_w = 1.0 + 0.05 * nrm(ks[10], (DEPTH, ATTN_HEAD_DIM))
    k_norm_w = 1.0 + 0.05 * nrm(ks[11], (DEPTH, ATTN_HEAD_DIM))
    sinks = 0.5 * nrm(ks[12], (DEPTH, ATTN_HEADS))
    conv_w = nrm(ks[13], (DEPTH, CONV_WIDTH, 2 * MLSTM_W)) * CONV_WIDTH ** -0.5
    conv_b = 0.02 * nrm(ks[14], (DEPTH, 2 * MLSTM_W))
    mlstm_norm_w = 1.0 + 0.05 * nrm(ks[15], (DEPTH, MLSTM_W))
    w_attn_up = nrm(ks[16], (DEPTH, ATTN_Q_W, D)) * ATTN_Q_W ** -0.5
    w_mlstm_up = nrm(ks[17], (DEPTH, MLSTM_W, D)) * MLSTM_W ** -0.5
    w_out = nrm(ks[18], (DEPTH, D, D)) * D ** -0.5
    router_w = nrm(ks[19], (D, E)) * D ** -0.5
    router_bias = 0.01 * nrm(ks[20], (E,))
    w_gate = nrm(ks[21], (DEPTH, E, D, F)) * D ** -0.5
    w_up = nrm(ks[22], (DEPTH, E, D, F)) * D ** -0.5
    w_down = nrm(ks[23], (DEPTH, E, F, D)) * F ** -0.5
    return {"x": x, "c": c, "positions": positions, "ada_w": ada_w, "ada_b": ada_b,
            "norm_mix_w": norm_mix_w, "norm_ffn_w": norm_ffn_w, "w_in": w_in,
            "b_igate": b_igate, "b_fgate": b_fgate, "q_norm_w": q_norm_w, "k_norm_w": k_norm_w,
            "sinks": sinks, "conv_w": conv_w, "conv_b": conv_b, "mlstm_norm_w": mlstm_norm_w,
            "w_attn_up": w_attn_up, "w_mlstm_up": w_mlstm_up, "w_out": w_out,
            "router_w": router_w, "router_bias": router_bias,
            "w_gate": w_gate, "w_up": w_up, "w_down": w_down}


def reference(x, c, positions, ada_w, ada_b, norm_mix_w, norm_ffn_w, w_in, b_igate, b_fgate,
              q_norm_w, k_norm_w, sinks, conv_w, conv_b, mlstm_norm_w, w_attn_up, w_mlstm_up,
              w_out, router_w, router_bias, w_gate, w_up, w_down):
    B, S, D = x.shape
    inv_freq = ROPE_THETA ** (-(jnp.arange(0, ROPE_DIM, 2, dtype=jnp.float32) / ROPE_DIM))
    ang = positions.astype(jnp.float32)[..., None] * inv_freq
    cos = jnp.cos(ang)[:, :, None, :]
    sin = jnp.sin(ang)[:, :, None, :]
    c_act = jax.nn.silu(c)

    for l in range(DEPTH):
        mod = c_act @ ada_w[l] + ada_b[l]
        sh1, sc1, g1, sh2, sc2, g2 = [t[:, None, :] for t in jnp.split(mod, 6, axis=-1)]

        h = rms_norm(x, norm_mix_w[l]) * (1.0 + sc1) + sh1
        proj = h @ w_in[l]
        aq, ak, av, mqk, mv, mi, mf, mo, ga, gb = split_cols(proj)

        aq = rms_norm(aq.reshape(B, S, ATTN_HEADS, ATTN_HEAD_DIM), q_norm_w[l])
        ak = rms_norm(ak.reshape(B, S, ATTN_KV_HEADS, ATTN_HEAD_DIM), k_norm_w[l])
        aq = partial_rope(aq, cos, sin)
        ak = partial_rope(ak, cos, sin)
        av = av.reshape(B, S, ATTN_KV_HEADS, ATTN_HEAD_DIM)
        ya = sliding_window_attention(aq, ak, av, sinks[l]) @ w_attn_up[l]

        mqk = jax.nn.silu(causal_depthwise_conv(mqk, conv_w[l], conv_b[l]))
        mq = mqk[..., :MLSTM_W].reshape(B, S, MLSTM_HEADS, MLSTM_HEAD_DIM)
        mk = mqk[..., MLSTM_W:].reshape(B, S, MLSTM_HEADS, MLSTM_HEAD_DIM) * (MLSTM_HEAD_DIM ** -0.5)
        mv = mv.reshape(B, S, MLSTM_HEADS, MLSTM_HEAD_DIM)
        hm = mlstm_chunkwise(mq, mk, mv, mi + b_igate[l], mf + b_fgate[l])
        hm = rms_norm(hm, mlstm_norm_w[l].reshape(MLSTM_HEADS, MLSTM_HEAD_DIM)).reshape(B, S, MLSTM_W)
        yb = (jax.nn.sigmoid(mo) * hm) @ w_mlstm_up[l]

        merged = jax.nn.sigmoid(ga) * ya + jax.nn.sigmoid(gb) * yb
        x = x + g1 * (merged @ w_out[l])

        h2 = rms_norm(x, norm_ffn_w[l]) * (1.0 + sc2) + sh2
        x = x + g2 * grouped_moe(h2, router_w, router_bias, w_gate[l], w_up[l], w_down[l])
    return x
```

```python
import functools

import jax
import jax.numpy as jnp
from jax import lax
from jax.experimental import pallas as pl
from jax.experimental.pallas import tpu as pltpu

F32 = jnp.float32
BF16 = jnp.bfloat16
U32 = jnp.uint32
I32 = jnp.int32
HIGHEST = lax.Precision.HIGHEST

HEAD_DIM = 64
N_HEADS = 8
N_KV = 2
ROPE_DIM = 16
ROPE_THETA = 500000.0
ATTN_BLOCK = 128
M_HEADS = 4
M_DIM = 128
CONV_K = 4
N_EXPERTS = 16
N_GROUPS = 4
EPG = 4
EPS = 1e-6

LANES = 128
SUBLANES = 8

CHUNK = 128
N_PAIRS = 6
N_BUCKETS = N_GROUPS * N_PAIRS
BUCKET_ROWS = 32
EXPERT_BLOCK = 256
PAY_W = 640
VMEM_LIMIT = 56 * 1024 * 1024


def _dot(a, b, precision=None):
    return jnp.dot(a, b, preferred_element_type=F32, precision=precision)


def _dot_nt(a, b):
    return lax.dot_general(a, b, (((1,), (1,)), ((), ())), preferred_element_type=F32)


def _sigmoid(x):
    return 1.0 / (1.0 + jnp.exp(-x))


def _log_sigmoid(x):
    return jnp.minimum(x, 0.0) - jnp.log1p(jnp.exp(-jnp.abs(x)))


def _params(*sem):
    return pltpu.CompilerParams(dimension_semantics=sem, vmem_limit_bytes=VMEM_LIMIT)


def _ada_kernel(c_ref, w_ref, b_ref, o_ref):
    c = c_ref[...]
    ca = c * _sigmoid(c)
    o_ref[0] = _dot(ca, w_ref[0], HIGHEST) + b_ref[0]


def _ada_mod(c_pad, ada_w, ada_b):
    depth, d, n = ada_w.shape
    tn = 1536
    return pl.pallas_call(
        _ada_kernel,
        grid=(depth, n // tn),
        in_specs=[
            pl.BlockSpec((SUBLANES, d), lambda l, j: (0, 0)),
            pl.BlockSpec((1, d, tn), lambda l, j: (l, 0, j)),
            pl.BlockSpec((1, 1, tn), lambda l, j: (l, 0, j)),
        ],
        out_specs=pl.BlockSpec((1, SUBLANES, tn), lambda l, j: (l, 0, j)),
        out_shape=jax.ShapeDtypeStruct((depth, SUBLANES, n), F32),
        compiler_params=_params("arbitrary", "arbitrary"),
        name="ada_mod",
    )(c_pad, ada_w, ada_b.reshape(depth, 1, n))


def _inproj_kernel(x_ref, sc_ref, sh_ref, nw_ref, wa_ref, wm_ref, wg_ref, wmg_ref,
                   a_ref, m_ref, mg_ref, gc_ref, gr_ref):
    x = x_ref[...]
    ms = jnp.mean(x * x, axis=-1, keepdims=True)
    h = x * lax.rsqrt(ms + EPS) * nw_ref[...]
    h = h * (1.0 + sc_ref[...]) + sh_ref[...]
    hb = h.astype(BF16)
    a_ref[...] = _dot(hb, wa_ref[...]).astype(BF16)
    m_ref[...] = _dot(hb, wm_ref[...]).astype(BF16)
    mg_ref[...] = _sigmoid(_dot(hb, wmg_ref[...])).astype(BF16)
    g = _dot(hb, wg_ref[...])
    gc_ref[...] = g
    gr_ref[...] = g.T[:SUBLANES, :]


def _inproj(x, sc, sh, nw, wa, wm, wg, wmg, l, seq):
    t, d = x.shape
    tm = min(512, seq)
    tpb = seq // tm
    row = lambda i: (i, 0)
    bsel = lambda i: (i // tpb, 0, 0)
    wsel = lambda i: (l, 0, 0)
    na, nm, ng, nmg = wa.shape[2], wm.shape[2], wg.shape[2], wmg.shape[2]
    return pl.pallas_call(
        _inproj_kernel,
        grid=(t // tm,),
        in_specs=[
            pl.BlockSpec((tm, d), row),
            pl.BlockSpec((None, 1, d), bsel),
            pl.BlockSpec((None, 1, d), bsel),
            pl.BlockSpec((None, 1, d), wsel),
            pl.BlockSpec((None, d, na), wsel),
            pl.BlockSpec((None, d, nm), wsel),
            pl.BlockSpec((None, d, ng), wsel),
            pl.BlockSpec((None, d, nmg), wsel),
        ],
        out_specs=[
            pl.BlockSpec((tm, na), row),
            pl.BlockSpec((tm, nm), row),
            pl.BlockSpec((tm, nmg), row),
            pl.BlockSpec((tm, ng), row),
            pl.BlockSpec((SUBLANES, tm), lambda i: (0, i)),
        ],
        out_shape=[
            jax.ShapeDtypeStruct((t, na), BF16),
            jax.ShapeDtypeStruct((t, nm), BF16),
            jax.ShapeDtypeStruct((t, nmg), BF16),
            jax.ShapeDtypeStruct((t, ng), F32),
            jax.ShapeDtypeStruct((SUBLANES, t), F32),
        ],
        compiler_params=_params("arbitrary"),
        name="inproj",
    )(x, sc, sh, nw, wa, wm, wg, wmg)


def _rope(t, cos, sin):
    w = t.shape[1]
    reps = w // LANES
    cosw = jnp.concatenate([cos] * reps, axis=1) if reps > 1 else cos
    sinw = jnp.concatenate([sin] * reps, axis=1) if reps > 1 else sin
    lane = lax.broadcasted_iota(I32, t.shape, 1)
    half = ROPE_DIM // 2
    up = pltpu.roll(t, w - half, axis=1)
    dn = pltpu.roll(t, half, axis=1)
    partner = jnp.where((lane % ROPE_DIM) < half, up, dn)
    return t * cosw + partner * sinw


def _head_norm(t, bd, w):
    ms = _dot((t * t).astype(BF16), bd)
    return t * lax.rsqrt(ms + EPS) * w


def _attn_kernel(sink_ref, cur_ref, prev_ref, cos_ref, sin_ref, cosp_ref, sinp_ref,
                 qw_ref, kw_ref, bdq_ref, bdk_ref, o_ref):
    tq = cur_ref.shape[0]
    nj = tq // ATTN_BLOCK
    qw = N_HEADS * HEAD_DIM
    kw = 2 * N_KV * HEAD_DIM
    blk0 = pl.program_id(1) * nj

    cur = cur_ref[...]
    q = cur[:, :qw].astype(F32)
    kc = cur[:, qw:qw + kw].astype(F32)
    vc = cur[:, qw + kw:]
    prev = prev_ref[...]
    kp = prev[:, :kw].astype(F32)
    vp = prev[:, kw:]

    cos, sin = cos_ref[...], sin_ref[...]
    q = _rope(_head_norm(q, bdq_ref[...], qw_ref[...]), cos, sin) * (HEAD_DIM ** -0.5)
    kc = _rope(_head_norm(kc, bdk_ref[...], kw_ref[...]), cos, sin)
    kp = _rope(_head_norm(kp, bdk_ref[...], kw_ref[...]), cosp_ref[...], sinp_ref[...])
    qb = q.astype(BF16)
    k_all = jnp.concatenate([kp, kc], axis=0).astype(BF16)
    v_all = jnp.concatenate([vp, vc], axis=0)

    lane = lax.broadcasted_iota(I32, (ATTN_BLOCK, LANES), 1)
    lo = lane < HEAD_DIM
    zero = jnp.zeros((ATTN_BLOCK, LANES), BF16)
    g_heads = N_HEADS // N_KV
    rr = lax.broadcasted_iota(I32, (g_heads * ATTN_BLOCK, 2 * ATTN_BLOCK), 0)
    ri = rr % ATTN_BLOCK
    ci = lax.broadcasted_iota(I32, (g_heads * ATTN_BLOCK, 2 * ATTN_BLOCK), 1)
    dist = ri + ATTN_BLOCK - ci
    head_row = lax.broadcasted_iota(I32, (g_heads * ATTN_BLOCK, 1), 0) // ATTN_BLOCK

    for j in range(nj):
        rows = slice(j * ATTN_BLOCK, (j + 1) * ATTN_BLOCK)
        has_prev = jnp.minimum(blk0 + j, 1)
        mask = (dist >= 0) & (dist < jnp.minimum(ATTN_BLOCK, ri + 1 + has_prev * ATTN_BLOCK))
        for g in range(N_KV):
            kb = k_all[j * ATTN_BLOCK:(j + 2) * ATTN_BLOCK, g * LANES:(g + 1) * LANES]
            vb = v_all[j * ATTN_BLOCK:(j + 2) * ATTN_BLOCK, g * LANES:(g + 1) * LANES]
            qp0 = qb[rows, (2 * g) * LANES:(2 * g + 1) * LANES]
            qp1 = qb[rows, (2 * g + 1) * LANES:(2 * g + 2) * LANES]
            q4 = jnp.concatenate([jnp.where(lo, qp0, zero), jnp.where(lo, zero, qp0),
                                  jnp.where(lo, qp1, zero), jnp.where(lo, zero, qp1)], axis=0)
            s = _dot_nt(q4, kb)
            s = jnp.where(mask, s, -jnp.inf)
            sink = jnp.full((g_heads * ATTN_BLOCK, 1), sink_ref[g_heads * g], F32)
            for r in range(1, g_heads):
                sink = jnp.where(head_row == r, sink_ref[g_heads * g + r], sink)
            m = jnp.maximum(jnp.max(s, axis=-1, keepdims=True), sink)
            p = jnp.exp(s - m)
            den = jnp.sum(p, axis=-1, keepdims=True) + jnp.exp(sink - m)
            o4 = _dot(p.astype(BF16), vb) / den
            b = ATTN_BLOCK
            o_ref[rows, (2 * g) * LANES:(2 * g + 1) * LANES] = jnp.where(lo, o4[0:b], o4[b:2 * b]).astype(BF16)
            o_ref[rows, (2 * g + 1) * LANES:(2 * g + 2) * LANES] = jnp.where(
                lo, o4[2 * b:3 * b], o4[3 * b:4 * b]).astype(BF16)


def _attention(a_in, cos_t, sin_t, sinks_l, qw, kw, bdq, bdk, batch, seq):
    t = a_in.shape[0]
    tq = min(512, seq)
    nj = tq // ATTN_BLOCK
    tpb = seq // tq
    bpb = seq // ATTN_BLOCK
    qwid = N_HEADS * HEAD_DIM
    kvw = 4 * N_KV * HEAD_DIM
    cur = lambda b, i: (b * tpb + i, 0)
    prv = lambda b, i: (b * bpb + jnp.maximum(i * nj - 1, 0), 1)
    prv0 = lambda b, i: (b * bpb + jnp.maximum(i * nj - 1, 0), 0)
    const = lambda b, i: (0, 0)
    return pl.pallas_call(
        _attn_kernel,
        grid=(batch, tpb),
        in_specs=[
            pl.BlockSpec(memory_space=pltpu.SMEM),
            pl.BlockSpec((tq, qwid + kvw), cur),
            pl.BlockSpec((ATTN_BLOCK, kvw), prv),
            pl.BlockSpec((tq, LANES), cur),
            pl.BlockSpec((tq, LANES), cur),
            pl.BlockSpec((ATTN_BLOCK, LANES), prv0),
            pl.BlockSpec((ATTN_BLOCK, LANES), prv0),
            pl.BlockSpec((1, qwid), const),
            pl.BlockSpec((1, kvw // 2), const),
            pl.BlockSpec((qwid, qwid), const),
            pl.BlockSpec((kvw // 2, kvw // 2), const),
        ],
        out_specs=pl.BlockSpec((tq, qwid), cur),
        out_shape=jax.ShapeDtypeStruct((t, qwid), BF16),
        compiler_params=_params("arbitrary", "arbitrary"),
        name="swa_attention",
    )(sinks_l, a_in, a_in, cos_t, sin_t, cos_t, sin_t, qw, kw, bdq, bdk)


def _mlstm_kernel(min_ref, gc_ref, gr_ref, cw_ref, cb_ref, brow_ref, bcol_ref, nw_ref,
                  hm_ref, ext_ref, act_ref, st_ref, mx_ref):
    tt = min_ref.shape[0]
    mw = M_HEADS * M_DIM
    nchunks = tt // CHUNK

    @pl.when(pl.program_id(1) == 0)
    def _():
        ext_ref[0:SUBLANES, :] = jnp.zeros((SUBLANES, 2 * mw), F32)
        st_ref[...] = jnp.zeros(st_ref.shape, F32)
        mx_ref[...] = jnp.zeros(mx_ref.shape, F32)

    u = min_ref[:, :2 * mw].astype(F32)
    ext_ref[SUBLANES:SUBLANES + tt, :] = u
    acc = cb_ref[...] + cw_ref[CONV_K - 1:CONV_K, :] * u
    for jj in range(CONV_K - 1):
        off = SUBLANES - (CONV_K - 1) + jj
        acc = acc + cw_ref[jj:jj + 1, :] * ext_ref[off:off + tt, :]
    act_ref[...] = acc * _sigmoid(acc)
    ext_ref[0:SUBLANES, :] = u[tt - SUBLANES:tt, :]

    ri = lax.broadcasted_iota(I32, (CHUNK, CHUNK), 0)
    ci = lax.broadcasted_iota(I32, (CHUNK, CHUNK), 1)
    causal = ci <= ri
    tril = jnp.where(causal, 1.0, 0.0)
    triu = jnp.where(ri <= ci, 1.0, 0.0)
    ones_col = jnp.where(lax.broadcasted_iota(I32, (CHUNK, M_DIM), 1) == 0, 1.0, 0.0).astype(BF16)
    kscale = M_DIM ** -0.5

    def chunk_body(c, carry):
        r0 = pl.multiple_of(c * CHUNK, CHUNK)
        rs = pl.ds(r0, CHUNK)
        gcol = gc_ref[rs, :] + brow_ref[...]
        grow = gr_ref[:, rs] + bcol_ref[...]
        bc_all = _dot(tril, _log_sigmoid(gcol), HIGHEST)
        br_all = _dot(_log_sigmoid(grow), triu, HIGHEST)
        for h in range(M_HEADS):
            hs = slice(h * M_DIM, (h + 1) * M_DIM)
            q = act_ref[rs, hs].astype(BF16)
            kf = act_ref[rs, mw + h * M_DIM:mw + (h + 1) * M_DIM] * kscale
            v = min_ref[rs, 2 * mw + h * M_DIM:2 * mw + (h + 1) * M_DIM]
            og = min_ref[rs, 3 * mw + h * M_DIM:3 * mw + (h + 1) * M_DIM].astype(F32)
            v_ext = jnp.concatenate([v, ones_col], axis=1)
            li_c = gcol[:, h:h + 1]
            li_r = grow[h:h + 1, :]
            b_c = bc_all[:, M_HEADS + h:M_HEADS + h + 1]
            b_r = br_all[M_HEADS + h:M_HEADS + h + 1, :]
            b_last = b_r[:, CHUNK - 1:CHUNK]
            m_prev = mx_ref[h][0:1, 0:1]
            state = st_ref[h]

            d = jnp.where(causal, b_c - b_r + li_r, -jnp.inf)
            g_c = b_c + m_prev
            m_t = jnp.maximum(jnp.max(d, axis=-1, keepdims=True), g_c)
            p = jnp.exp(d - m_t)
            inter = jnp.exp(g_c - m_t)
            qk = (_dot_nt(q, kf.astype(BF16)) * p).astype(BF16)
            num = _dot(qk, v_ext) + inter * _dot(q, state.astype(BF16))
            den = jnp.maximum(jnp.abs(num[:, M_DIM:M_DIM + 1]), jnp.exp(-m_t))
            hh = num[:, :M_DIM] / den
            hn = hh * lax.rsqrt(jnp.mean(hh * hh, axis=-1, keepdims=True) + EPS) * nw_ref[:, hs]
            hm_ref[rs, hs] = (_sigmoid(og) * hn).astype(BF16)

            w_r = b_last - b_r + li_r
            m_loc = jnp.max(w_r, axis=-1, keepdims=True)
            e_c = jnp.exp(b_last - b_c + li_c - m_loc)
            kt = (kf * e_c).T.astype(BF16)
            kv = _dot(kt, v_ext)
            m_new = jnp.maximum(b_last + m_prev, m_loc)
            a = jnp.exp(b_last + m_prev - m_new)
            sc = jnp.exp(m_loc - m_new)
            st_ref[h] = a * state + sc * kv
            mx_ref[h] = jnp.broadcast_to(m_new, (SUBLANES, LANES))
        return carry

    lax.fori_loop(0, nchunks, chunk_body, 0)


def _mlstm(m_in, gcol, grow, conv_w, conv_b, brow, bcol, nw, batch, seq):
    t = m_in.shape[0]
    tt = min(1024, seq)
    tpb = seq // tt
    mw = M_HEADS * M_DIM
    cur = lambda b, i: (b * tpb + i, 0)
    const = lambda b, i: (0, 0)
    return pl.pallas_call(
        _mlstm_kernel,
        grid=(batch, tpb),
        in_specs=[
            pl.BlockSpec((tt, 4 * mw), cur),
            pl.BlockSpec((tt, LANES), cur),
            pl.BlockSpec((SUBLANES, tt), lambda b, i: (0, b * tpb + i)),
            pl.BlockSpec((CONV_K, 2 * mw), const),
            pl.BlockSpec((1, 2 * mw), const),
            pl.BlockSpec((1, LANES), const),
            pl.BlockSpec((SUBLANES, LANES), const),
            pl.BlockSpec((1, mw), const),
        ],
        out_specs=pl.BlockSpec((tt, mw), cur),
        out_shape=jax.ShapeDtypeStruct((t, mw), BF16),
        scratch_shapes=[
            pltpu.VMEM((tt + SUBLANES, 2 * mw), F32),
            pltpu.VMEM((tt, 2 * mw), F32),
            pltpu.VMEM((M_HEADS, M_DIM, 2 * M_DIM), F32),
            pltpu.VMEM((M_HEADS, SUBLANES, LANES), F32),
        ],
        compiler_params=_params("arbitrary", "arbitrary"),
        name="mlstm",
    )(m_in, gcol, grow, conv_w, conv_b, brow, bcol, nw)


def _merge_kernel(o_ref, hm_ref, mg_ref, x_ref, g1_ref, sc_ref, sh_ref, nw_ref,
                  wa_ref, wm_ref, wo_ref, rwh_ref, rwl_ref, rb_ref, tri_ref,
                  x1_ref, pay_ref, route_ref, cnt_ref, carry_ref):
    tm, d = x_ref.shape

    @pl.when(pl.program_id(0) == 0)
    def _():
        carry_ref[...] = jnp.zeros(carry_ref.shape, F32)

    ya = _dot(o_ref[...], wa_ref[...])
    yb = _dot(hm_ref[...], wm_ref[...])
    mg = mg_ref[...]
    merged = mg[:, :d].astype(F32) * ya + mg[:, d:].astype(F32) * yb
    x1 = x_ref[...] + g1_ref[...] * _dot(merged.astype(BF16), wo_ref[...])
    x1_ref[...] = x1

    ms = jnp.mean(x1 * x1, axis=-1, keepdims=True)
    h2 = x1 * lax.rsqrt(ms + EPS) * nw_ref[...]
    h2 = h2 * (1.0 + sc_ref[...]) + sh_ref[...]
    hi = h2.astype(BF16)
    hif = hi.astype(F32)
    lo = (h2 - hif).astype(BF16)
    rwh = rwh_ref[...]
    logits = _dot(hi, rwh) + _dot(lo, rwh) + _dot(hi, rwl_ref[...])
    sc_t = _sigmoid(logits).T[:N_EXPERTS, :]
    sel_t = sc_t + rb_ref[:, 0:1]

    def row(a, e):
        return a[e:e + 1, :]

    best = None
    gi = jnp.zeros((1, tm), I32)
    for g in range(N_GROUPS):
        r = [row(sel_t, EPG * g + i) for i in range(EPG)]
        gs = None
        for i in range(EPG):
            for j in range(i + 1, EPG):
                pr = r[i] + r[j]
                gs = pr if gs is None else jnp.maximum(gs, pr)
        if best is None:
            best = gs
        else:
            upd = gs > best
            gi = jnp.where(upd, g, gi)
            best = jnp.maximum(best, gs)

    def pick(a, i):
        out = row(a, i)
        for g in range(1, N_GROUPS):
            out = jnp.where(gi == g, row(a, EPG * g + i), out)
        return out

    v = [pick(sel_t, i) for i in range(EPG)]
    s = [pick(sc_t, i) for i in range(EPG)]

    def argmax4(vals):
        bv, bi = vals[0], jnp.zeros((1, tm), I32)
        for i in range(1, EPG):
            upd = vals[i] > bv
            bi = jnp.where(upd, i, bi)
            bv = jnp.maximum(bv, vals[i])
        return bi

    i1 = argmax4(v)
    i2 = argmax4([jnp.where(i1 == i, -jnp.inf, v[i]) for i in range(EPG)])
    ia = jnp.minimum(i1, i2)
    ib = jnp.maximum(i1, i2)
    pidx = jnp.where(ia == 0, ib - 1, jnp.where(ia == 1, ib + 1, N_PAIRS - 1))
    bucket = gi * N_PAIRS + pidx

    def by_index(vals, idx):
        out = vals[0]
        for i in range(1, EPG):
            out = jnp.where(idx == i, vals[i], out)
        return out

    s_a, s_b = by_index(s, ia), by_index(s, ib)
    gate_a = s_a / (s_a + s_b)
    gate_b = s_b / (s_a + s_b)

    brow = lax.broadcasted_iota(I32, (BUCKET_ROWS, tm), 0)
    onehot = brow == bucket
    cums = _dot(jnp.where(onehot, 1.0, 0.0).astype(BF16), tri_ref[...])
    carry = carry_ref[...]
    rank = jnp.sum(jnp.where(onehot, carry[:, 0:1] + cums, 0.0), axis=0, keepdims=True) - 1.0
    new_carry = carry + cums[:, tm - 1:tm]
    carry_ref[...] = new_carry
    cnt_ref[...] = new_carry

    route_ref[...] = jnp.concatenate(
        [bucket.astype(F32), gate_a, gate_b, rank, jnp.zeros((SUBLANES - 4, tm), F32)], axis=0)

    bits = lax.bitcast_convert_type(hif, U32)
    half = d // 2
    pay_ref[:, :half] = bits[:, :half] | (bits[:, half:] >> 16)
    gates_t = jnp.concatenate([gate_a, gate_b, jnp.zeros((LANES - 2, tm), F32)], axis=0)
    pay_ref[:, half:] = lax.bitcast_convert_type(gates_t.T, U32)


def _merge(o_attn, hm, mg, x, g1, sc2, sh2, nw, wa, wm, wo, rwh, rwl, rb, tri, l, seq):
    t, d = x.shape
    tm = tri.shape[0]
    tpb = seq // tm
    row = lambda i: (i, 0)
    bsel = lambda i: (i // tpb, 0, 0)
    wsel = lambda i: (l, 0, 0)
    const = lambda i: (0, 0)
    hw = o_attn.shape[1]
    return pl.pallas_call(
        _merge_kernel,
        grid=(t // tm,),
        in_specs=[
            pl.BlockSpec((tm, hw), row),
            pl.BlockSpec((tm, hw), row),
            pl.BlockSpec((tm, 2 * d), row),
            pl.BlockSpec((tm, d), row),
            pl.BlockSpec((None, 1, d), bsel),
            pl.BlockSpec((None, 1, d), bsel),
            pl.BlockSpec((None, 1, d), bsel),
            pl.BlockSpec((None, 1, d), wsel),
            pl.BlockSpec((None, hw, d), wsel),
            pl.BlockSpec((None, hw, d), wsel),
            pl.BlockSpec((None, d, d), wsel),
            pl.BlockSpec((d, LANES), const),
            pl.BlockSpec((d, LANES), const),
            pl.BlockSpec((N_EXPERTS, LANES), const),
            pl.BlockSpec((tm, tm), const),
        ],
        out_specs=[
            pl.BlockSpec((tm, d), row),
            pl.BlockSpec((tm, PAY_W), row),
            pl.BlockSpec((SUBLANES, tm), lambda i: (0, i)),
            pl.BlockSpec((BUCKET_ROWS, LANES), const),
        ],
        out_shape=[
            jax.ShapeDtypeStruct((t, d), F32),
            jax.ShapeDtypeStruct((t, PAY_W), U32),
            jax.ShapeDtypeStruct((SUBLANES, t), F32),
            jax.ShapeDtypeStruct((BUCKET_ROWS, LANES), F32),
        ],
        scratch_shapes=[pltpu.VMEM((BUCKET_ROWS, LANES), F32)],
        compiler_params=_params("arbitrary"),
        name="merge_router",
    )(o_attn, hm, mg, x, g1, sc2, sh2, nw, wa, wm, wo, rwh, rwl, rb, tri)


def _row_copy(src_ref, src_row, dst_ref, dst_row, sem):
    return pltpu.make_async_copy(src_ref.at[pl.ds(src_row, 1)], dst_ref.at[pl.ds(dst_row, 1)], sem)


def _scatter_kernel(dest_ref, pay_ref, init_ref, xs_ref, sem):
    del init_ref
    ts = dest_ref.shape[1]
    base = pl.program_id(0) * ts

    def start(r, c):
        _row_copy(pay_ref, base + r, xs_ref, dest_ref[0, r], sem).start()
        return c

    lax.fori_loop(0, ts, start, 0)

    def wait(r, c):
        _row_copy(pay_ref, 0, xs_ref, 0, sem).wait()
        return c

    lax.fori_loop(0, ts, wait, 0)


def _scatter_rows(dest, pay, n_rows):
    t = pay.shape[0]
    ts = min(1024, t)
    dest3 = dest.reshape(t // ts, 1, ts)
    init = jnp.zeros((n_rows, PAY_W), U32)
    return pl.pallas_call(
        _scatter_kernel,
        grid=(t // ts,),
        in_specs=[
            pl.BlockSpec((None, 1, ts), lambda i: (i, 0, 0), memory_space=pltpu.SMEM),
            pl.BlockSpec(memory_space=pl.ANY),
            pl.BlockSpec(memory_space=pl.ANY),
        ],
        out_specs=pl.BlockSpec(memory_space=pl.ANY),
        out_shape=jax.ShapeDtypeStruct((n_rows, PAY_W), U32),
        scratch_shapes=[pltpu.SemaphoreType.DMA(())],
        input_output_aliases={2: 0},
        compiler_params=_params("arbitrary"),
        name="scatter_rows",
    )(dest3, pay, init)


def _combine_kernel(dest_ref, ys_ref, x_ref, g2_ref, o_ref, buf_ref, sem):
    tm = x_ref.shape[0]

    def start(r, c):
        _row_copy(ys_ref, dest_ref[0, r], buf_ref, r, sem).start()
        return c

    lax.fori_loop(0, tm, start, 0)

    def wait(r, c):
        _row_copy(ys_ref, 0, buf_ref, 0, sem).wait()
        return c

    lax.fori_loop(0, tm, wait, 0)
    o_ref[...] = x_ref[...] + g2_ref[...] * buf_ref[...]


def _combine(dest, ys, x1, g2, seq):
    t, d = x1.shape
    tm = min(512, seq)
    tpb = seq // tm
    dest3 = dest.reshape(t // tm, 1, tm)
    return pl.pallas_call(
        _combine_kernel,
        grid=(t // tm,),
        in_specs=[
            pl.BlockSpec((None, 1, tm), lambda i: (i, 0, 0), memory_space=pltpu.SMEM),
            pl.BlockSpec(memory_space=pl.ANY),
            pl.BlockSpec((tm, d), lambda i: (i, 0)),
            pl.BlockSpec((None, 1, d), lambda i: (i // tpb, 0, 0)),
        ],
        out_specs=pl.BlockSpec((tm, d), lambda i: (i, 0)),
        out_shape=jax.ShapeDtypeStruct((t, d), F32),
        scratch_shapes=[pltpu.VMEM((tm, d), F32), pltpu.SemaphoreType.DMA(())],
        compiler_params=_params("arbitrary"),
        name="combine_rows",
    )(dest3, ys, x1, g2)


def _expert_kernel(ea_ref, eb_ref, nr_ref, xs_ref, wgua_ref, wda_ref, wgub_ref, wdb_ref, ys_ref):
    j = pl.program_id(0)
    half = (PAY_W - LANES)

    @pl.when(j < nr_ref[0])
    def _():
        w = xs_ref[...]
        pk = w[:, :half]
        xa = lax.bitcast_convert_type(pk & jnp.uint32(0xFFFF0000), F32)
        xb = lax.bitcast_convert_type(pk << 16, F32)
        x = jnp.concatenate([xa, xb], axis=1).astype(BF16)
        gl = lax.bitcast_convert_type(w[:, half:], F32)

        def ffn(wgu_ref, wd_ref):
            gu = _dot(x, wgu_ref[...])
            f = gu.shape[1] // 2
            gte, up = gu[:, :f], gu[:, f:]
            act = gte * _sigmoid(gte) * up
            return _dot(act.astype(BF16), wd_ref[...])

        ys_ref[...] = gl[:, 0:1] * ffn(wgua_ref, wda_ref) + gl[:, 1:2] * ffn(wgub_ref, wdb_ref)

    @pl.when(j >= nr_ref[0])
    def _():
        ys_ref[...] = jnp.zeros(ys_ref.shape, F32)


def _experts(blk_ea, blk_eb, n_real, xs, wgu, wd, d):
    n_rows = xs.shape[0]
    nblk = n_rows // EXPERT_BLOCK
    f2 = wgu.shape[2]
    grid_spec = pltpu.PrefetchScalarGridSpec(
        num_scalar_prefetch=3,
        grid=(nblk,),
        in_specs=[
            pl.BlockSpec((EXPERT_BLOCK, PAY_W), lambda j, ea, eb, nr: (j, 0)),
            pl.BlockSpec((None, d, f2), lambda j, ea, eb, nr: (ea[j], 0, 0)),
            pl.BlockSpec((None, f2 // 2, d), lambda j, ea, eb, nr: (ea[j], 0, 0)),
            pl.BlockSpec((None, d, f2), lambda j, ea, eb, nr: (eb[j], 0, 0)),
            pl.BlockSpec((None, f2 // 2, d), lambda j, ea, eb, nr: (eb[j], 0, 0)),
        ],
        out_specs=pl.BlockSpec((EXPERT_BLOCK, d), lambda j, ea, eb, nr: (j, 0)),
    )
    return pl.pallas_call(
        _expert_kernel,
        grid_spec=grid_spec,
        out_shape=jax.ShapeDtypeStruct((n_rows, d), F32),
        compiler_params=_params("arbitrary"),
        name="experts",
    )(blk_ea, blk_eb, n_real, xs, wgu, wd, wgu, wd)


_PAIR_A = (0, 0, 0, 1, 1, 2)
_PAIR_B = (1, 2, 3, 2, 3, 3)


def kernel(x, c, positions, ada_w, ada_b, norm_mix_w, norm_ffn_w, w_in, b_igate, b_fgate, q_norm_w, k_norm_w,
           sinks, conv_w, conv_b, mlstm_norm_w, w_attn_up, w_mlstm_up, w_out, router_w, router_bias,
           w_gate, w_up, w_down):
    batch, seq, d = x.shape
    depth = w_in.shape[0]
    t = batch * seq
    qw = N_HEADS * HEAD_DIM
    kvw = N_KV * HEAD_DIM
    mw = M_HEADS * M_DIM

    o = 0
    cols = {}
    for name, wdt in (("q", qw), ("k", kvw), ("v", kvw), ("mqk", 2 * mw), ("mv", mw), ("mi", M_HEADS),
                      ("mf", M_HEADS), ("mo", mw), ("ga", d), ("gb", d)):
        cols[name] = (o, o + wdt)
        o += wdt

    def wc(name, lo=0, hi=None):
        s, e = cols[name]
        return w_in[:, :, s + lo:(s + hi if hi is not None else e)]

    k0, k1 = wc("k", 0, HEAD_DIM), wc("k", HEAD_DIM, 2 * HEAD_DIM)
    v0, v1 = wc("v", 0, HEAD_DIM), wc("v", HEAD_DIM, 2 * HEAD_DIM)
    w_a = jnp.concatenate([wc("q"), k0, k0, k1, k1, v0, v0, v1, v1], axis=2).astype(BF16)
    w_m = jnp.concatenate([wc("mqk"), wc("mv"), wc("mo")], axis=2).astype(BF16)
    w_g = jnp.concatenate([wc("mi"), wc("mf"), jnp.zeros((depth, d, LANES - 2 * M_HEADS), F32)], axis=2).astype(BF16)
    w_mg = jnp.concatenate([wc("ga"), wc("gb")], axis=2).astype(BF16)
    w_au = w_attn_up.astype(BF16)
    w_mu = w_mlstm_up.astype(BF16)
    w_o = w_out.astype(BF16)
    n_e = w_gate.shape[1]
    w_gu = jnp.concatenate([w_gate, w_up], axis=3).astype(BF16).reshape(depth * n_e, d, -1)
    w_d = w_down.astype(BF16).reshape(depth * n_e, -1, d)

    rw_pad = jnp.concatenate([router_w, jnp.zeros((d, LANES - n_e), F32)], axis=1)
    rw_hi = rw_pad.astype(BF16)
    rw_lo = (rw_pad - rw_hi.astype(F32)).astype(BF16)
    rb = jnp.broadcast_to(router_bias.astype(F32)[:, None], (n_e, LANES))

    qn_w = jnp.tile(q_norm_w, (1, N_HEADS)).reshape(depth, 1, qw)
    kn_w = jnp.tile(k_norm_w, (1, 2 * N_KV)).reshape(depth, 1, 2 * kvw)
    seg = jnp.arange(qw) // HEAD_DIM
    bdq = jnp.where(seg[:, None] == seg[None, :], 1.0 / HEAD_DIM, 0.0).astype(BF16)
    bdk = bdq[:2 * kvw, :2 * kvw]

    half = ROPE_DIM // 2
    inv_freq = ROPE_THETA ** (-(jnp.arange(0, ROPE_DIM, 2, dtype=F32) / ROPE_DIM))
    ang = positions.astype(F32).reshape(t, 1) * inv_freq
    cos8, sin8 = jnp.cos(ang), jnp.sin(ang)
    pad1 = jnp.ones((t, HEAD_DIM - ROPE_DIM), F32)
    pad0 = jnp.zeros((t, HEAD_DIM - ROPE_DIM), F32)
    cos_t = jnp.tile(jnp.concatenate([cos8, cos8, pad1], axis=1), (1, LANES // HEAD_DIM))
    sin_t = jnp.tile(jnp.concatenate([-sin8, sin8, pad0], axis=1), (1, LANES // HEAD_DIM))

    gate_bias = jnp.concatenate([b_igate, b_fgate], axis=1).astype(F32)
    brow = jnp.concatenate([gate_bias, jnp.zeros((depth, LANES - 2 * M_HEADS), F32)], axis=1)
    bcol = jnp.broadcast_to(gate_bias[:, :, None], (depth, 2 * M_HEADS, LANES))

    tm_merge = min(512, seq)
    ii = jnp.arange(tm_merge)
    tri = (ii[:, None] <= ii[None, :]).astype(BF16)

    n_blk = (t + N_BUCKETS * (EXPERT_BLOCK - 1)) // EXPERT_BLOCK + 1
    n_rows = n_blk * EXPERT_BLOCK
    pair_a = jnp.asarray(_PAIR_A, I32)
    pair_b = jnp.asarray(_PAIR_B, I32)

    c_pad = jnp.zeros((SUBLANES, d), F32).at[:batch].set(c)
    mod = _ada_mod(c_pad, ada_w, ada_b)[:, :batch]

    xf = x.reshape(t, d)
    for l in range(depth):
        sh1, sc1, g1, sh2, sc2, g2 = [m.reshape(batch, 1, d) for m in jnp.split(mod[l], 6, axis=-1)]

        a_in, m_in, mg, gcol, grow = _inproj(xf, sc1, sh1, norm_mix_w.reshape(depth, 1, d),
                                             w_a, w_m, w_g, w_mg, l, seq)
        o_attn = _attention(a_in, cos_t, sin_t, sinks[l], qn_w[l], kn_w[l], bdq, bdk, batch, seq)
        hm = _mlstm(m_in, gcol, grow, conv_w[l], conv_b[l].reshape(1, -1), brow[l:l + 1], bcol[l],
                    mlstm_norm_w[l].reshape(1, mw), batch, seq)
        x1, pay, route, cnt = _merge(o_attn, hm, mg, xf, g1, sc2, sh2, norm_ffn_w.reshape(depth, 1, d),
                                     w_au, w_mu, w_o, rw_hi, rw_lo, rb, tri, l, seq)

        counts = cnt[:N_BUCKETS, 0].astype(I32)
        padded = (counts + EXPERT_BLOCK - 1) // EXPERT_BLOCK * EXPERT_BLOCK
        pad_ends = jnp.cumsum(padded)
        pad_starts = pad_ends - padded
        bucket = route[0].astype(I32)
        dest = pad_starts[bucket] + route[3].astype(I32)
        blk_bucket = jnp.minimum(
            jnp.searchsorted(pad_ends, jnp.arange(n_blk, dtype=I32) * EXPERT_BLOCK, side="right"), N_BUCKETS - 1)
        grp = blk_bucket // N_PAIRS
        blk_ea = (l * n_e + grp * EPG + pair_a[blk_bucket % N_PAIRS]).astype(I32)
        blk_eb = (l * n_e + grp * EPG + pair_b[blk_bucket % N_PAIRS]).astype(I32)
        n_real = (pad_ends[-1:] // EXPERT_BLOCK).astype(I32)

        xs = _scatter_rows(dest, pay, n_rows)
        ys = _experts(blk_ea, blk_eb, n_real, xs, w_gu, w_d, d)
        xf = _combine(dest, ys, x1, g2, seq)
    return xf.reshape(batch, seq, d)
```

```python
import functools

import jax
import jax.numpy as jnp
from jax import lax
from jax.experimental import pallas as pl
from jax.experimental.pallas import tpu as pltpu

F32 = jnp.float32
BF16 = jnp.bfloat16
U32 = jnp.uint32
I32 = jnp.int32
HIGHEST = lax.Precision.HIGHEST

HEAD_DIM = 64
N_HEADS = 8
N_KV = 2
ROPE_DIM = 16
ROPE_THETA = 500000.0
ATTN_BLOCK = 128
M_HEADS = 4
M_DIM = 128
CONV_K = 4
N_EXPERTS = 16
N_GROUPS = 4
EPG = 4
EPS = 1e-6

LANES = 128
SUBLANES = 8

CHUNK = 128
N_PAIRS = 6
N_BUCKETS = N_GROUPS * N_PAIRS
BUCKET_ROWS = 32
EXPERT_BLOCK = 256
PAY_W = 640
VMEM_LIMIT = 56 * 1024 * 1024


def _dot(a, b, precision=None):
    return jnp.dot(a, b, preferred_element_type=F32, precision=precision)


def _dot_nt(a, b):
    return lax.dot_general(a, b, (((1,), (1,)), ((), ())), preferred_element_type=F32)


def _sigmoid(x):
    return 1.0 / (1.0 + jnp.exp(-x))


def _log_sigmoid(x):
    return jnp.minimum(x, 0.0) - jnp.log1p(jnp.exp(-jnp.abs(x)))


def _params(*sem):
    return pltpu.CompilerParams(dimension_semantics=sem, vmem_limit_bytes=VMEM_LIMIT)


def _ada_kernel(c_ref, w_ref, b_ref, o_ref):
    c = c_ref[...]
    ca = c * _sigmoid(c)
    o_ref[0] = _dot(ca, w_ref[0], HIGHEST) + b_ref[0]


def _ada_mod(c_pad, ada_w, ada_b):
    depth, d, n = ada_w.shape
    tn = 1536
    return pl.pallas_call(
        _ada_kernel,
        grid=(depth, n // tn),
        in_specs=[
            pl.BlockSpec((SUBLANES, d), lambda l, j: (0, 0)),
            pl.BlockSpec((1, d, tn), lambda l, j: (l, 0, j)),
            pl.BlockSpec((1, 1, tn), lambda l, j: (l, 0, j)),
        ],
        out_specs=pl.BlockSpec((1, SUBLANES, tn), lambda l, j: (l, 0, j)),
        out_shape=jax.ShapeDtypeStruct((depth, SUBLANES, n), F32),
        compiler_params=_params("arbitrary", "arbitrary"),
        name="ada_mod",
    )(c_pad, ada_w, ada_b.reshape(depth, 1, n))


def _inproj_kernel(x_ref, sc_ref, sh_ref, nw_ref, wa_ref, wm_ref, wg_ref, wmg_ref,
                   a_ref, m_ref, mg_ref, gc_ref, gr_ref):
    x = x_ref[...]
    ms = jnp.mean(x * x, axis=-1, keepdims=True)
    h = x * lax.rsqrt(ms + EPS) * nw_ref[...]
    h = h * (1.0 + sc_ref[...]) + sh_ref[...]
    hb = h.astype(BF16)
    a_ref[...] = _dot(hb, wa_ref[...]).astype(BF16)
    m_ref[...] = _dot(hb, wm_ref[...]).astype(BF16)
    mg_ref[...] = _sigmoid(_dot(hb, wmg_ref[...])).astype(BF16)
    g = _dot(hb, wg_ref[...])
    gc_ref[...] = g
    gr_ref[...] = g.T[:SUBLANES, :]


def _inproj(x, sc, sh, nw, wa, wm, wg, wmg, l, seq):
    t, d = x.shape
    tm = min(512, seq)
    tpb = seq // tm
    row = lambda i: (i, 0)
    bsel = lambda i: (i // tpb, 0, 0)
    wsel = lambda i: (l, 0, 0)
    na, nm, ng, nmg = wa.shape[2], wm.shape[2], wg.shape[2], wmg.shape[2]
    return pl.pallas_call(
        _inproj_kernel,
        grid=(t // tm,),
        in_specs=[
            pl.BlockSpec((tm, d), row),
            pl.BlockSpec((None, 1, d), bsel),
            pl.BlockSpec((None, 1, d), bsel),
            pl.BlockSpec((None, 1, d), wsel),
            pl.BlockSpec((None, d, na), wsel),
            pl.BlockSpec((None, d, nm), wsel),
            pl.BlockSpec((None, d, ng), wsel),
            pl.BlockSpec((None, d, nmg), wsel),
        ],
        out_specs=[
            pl.BlockSpec((tm, na), row),
            pl.BlockSpec((tm, nm), row),
            pl.BlockSpec((tm, nmg), row),
            pl.BlockSpec((tm, ng), row),
            pl.BlockSpec((SUBLANES, tm), lambda i: (0, i)),
        ],
        out_shape=[
            jax.ShapeDtypeStruct((t, na), BF16),
            jax.ShapeDtypeStruct((t, nm), BF16),
            jax.ShapeDtypeStruct((t, nmg), BF16),
            jax.ShapeDtypeStruct((t, ng), F32),
            jax.ShapeDtypeStruct((SUBLANES, t), F32),
        ],
        compiler_params=_params("arbitrary"),
        name="inproj",
    )(x, sc, sh, nw, wa, wm, wg, wmg)


def _rope(t, cos, sin):
    w = t.shape[1]
    reps = w // LANES
    cosw = jnp.concatenate([cos] * reps, axis=1) if reps > 1 else cos
    sinw = jnp.concatenate([sin] * reps, axis=1) if reps > 1 else sin
    lane = lax.broadcasted_iota(I32, t.shape, 1)
    half = ROPE_DIM // 2
    up = pltpu.roll(t, w - half, axis=1)
    dn = pltpu.roll(t, half, axis=1)
    partner = jnp.where((lane % ROPE_DIM) < half, up, dn)
    return t * cosw + partner * sinw


def _head_norm(t, bd, w):
    ms = _dot((t * t).astype(BF16), bd)
    return t * lax.rsqrt(ms + EPS) * w


def _attn_kernel(sink_ref, cur_ref, prev_ref, cos_ref, sin_ref, cosp_ref, sinp_ref,
                 qw_ref, kw_ref, bdq_ref, bdk_ref, o_ref):
    tq = cur_ref.shape[0]
    nj = tq // ATTN_BLOCK
    qw = N_HEADS * HEAD_DIM
    kw = 2 * N_KV * HEAD_DIM
    blk0 = pl.program_id(1) * nj

    cur = cur_ref[...]
    q = cur[:, :qw].astype(F32)
    kc = cur[:, qw:qw + kw].astype(F32)
    vc = cur[:, qw + kw:]
    prev = prev_ref[...]
    kp = prev[:, :kw].astype(F32)
    vp = prev[:, kw:]

    cos, sin = cos_ref[...], sin_ref[...]
    q = _rope(_head_norm(q, bdq_ref[...], qw_ref[...]), cos, sin) * (HEAD_DIM ** -0.5)
    kc = _rope(_head_norm(kc, bdk_ref[...], kw_ref[...]), cos, sin)
    kp = _rope(_head_norm(kp, bdk_ref[...], kw_ref[...]), cosp_ref[...], sinp_ref[...])
    qb = q.astype(BF16)
    k_all = jnp.concatenate([kp, kc], axis=0).astype(BF16)
    v_all = jnp.concatenate([vp, vc], axis=0)

    lane = lax.broadcasted_iota(I32, (ATTN_BLOCK, LANES), 1)
    lo = lane < HEAD_DIM
    zero = jnp.zeros((ATTN_BLOCK, LANES), BF16)
    g_heads = N_HEADS // N_KV
    rr = lax.broadcasted_iota(I32, (g_heads * ATTN_BLOCK, 2 * ATTN_BLOCK), 0)
    ri = rr % ATTN_BLOCK
    ci = lax.broadcasted_iota(I32, (g_heads * ATTN_BLOCK, 2 * ATTN_BLOCK), 1)
    dist = ri + ATTN_BLOCK - ci
    head_row = lax.broadcasted_iota(I32, (g_heads * ATTN_BLOCK, 1), 0) // ATTN_BLOCK

    for j in range(nj):
        rows = slice(j * ATTN_BLOCK, (j + 1) * ATTN_BLOCK)
        has_prev = jnp.minimum(blk0 + j, 1)
        mask = (dist >= 0) & (dist < jnp.minimum(ATTN_BLOCK, ri + 1 + has_prev * ATTN_BLOCK))
        for g in range(N_KV):
            kb = k_all[j * ATTN_BLOCK:(j + 2) * ATTN_BLOCK, g * LANES:(g + 1) * LANES]
            vb = v_all[j * ATTN_BLOCK:(j + 2) * ATTN_BLOCK, g * LANES:(g + 1) * LANES]
            qp0 = qb[rows, (2 * g) * LANES:(2 * g + 1) * LANES]
            qp1 = qb[rows, (2 * g + 1) * LANES:(2 * g + 2) * LANES]
            q4 = jnp.concatenate([jnp.where(lo, qp0, zero), jnp.where(lo, zero, qp0),
                                  jnp.where(lo, qp1, zero), jnp.where(lo, zero, qp1)], axis=0)
            s = _dot_nt(q4, kb)
            s = jnp.where(mask, s, -jnp.inf)
            sink = jnp.full((g_heads * ATTN_BLOCK, 1), sink_ref[g_heads * g], F32)
            for r in range(1, g_heads):
                sink = jnp.where(head_row == r, sink_ref[g_heads * g + r], sink)
            m = jnp.maximum(jnp.max(s, axis=-1, keepdims=True), sink)
            p = jnp.exp(s - m)
            den = jnp.sum(p, axis=-1, keepdims=True) + jnp.exp(sink - m)
            o4 = _dot(p.astype(BF16), vb) / den
            b = ATTN_BLOCK
            o_ref[rows, (2 * g) * LANES:(2 * g + 1) * LANES] = jnp.where(lo, o4[0:b], o4[b:2 * b]).astype(BF16)
            o_ref[rows, (2 * g + 1) * LANES:(2 * g + 2) * LANES] = jnp.where(
                lo, o4[2 * b:3 * b], o4[3 * b:4 * b]).astype(BF16)


def _attention(a_in, cos_t, sin_t, sinks_l, qw, kw, bdq, bdk, batch, seq):
    t = a_in.shape[0]
    tq = min(512, seq)
    nj = tq // ATTN_BLOCK
    tpb = seq // tq
    bpb = seq // ATTN_BLOCK
    qwid = N_HEADS * HEAD_DIM
    kvw = 4 * N_KV * HEAD_DIM
    cur = lambda b, i: (b * tpb + i, 0)
    prv = lambda b, i: (b * bpb + jnp.maximum(i * nj - 1, 0), 1)
    prv0 = lambda b, i: (b * bpb + jnp.maximum(i * nj - 1, 0), 0)
    const = lambda b, i: (0, 0)
    return pl.pallas_call(
        _attn_kernel,
        grid=(batch, tpb),
        in_specs=[
            pl.BlockSpec(memory_space=pltpu.SMEM),
            pl.BlockSpec((tq, qwid + kvw), cur),
            pl.BlockSpec((ATTN_BLOCK, kvw), prv),
            pl.BlockSpec((tq, LANES), cur),
            pl.BlockSpec((tq, LANES), cur),
            pl.BlockSpec((ATTN_BLOCK, LANES), prv0),
            pl.BlockSpec((ATTN_BLOCK, LANES), prv0),
            pl.BlockSpec((1, qwid), const),
            pl.BlockSpec((1, kvw // 2), const),
            pl.BlockSpec((qwid, qwid), const),
            pl.BlockSpec((kvw // 2, kvw // 2), const),
        ],
        out_specs=pl.BlockSpec((tq, qwid), cur),
        out_shape=jax.ShapeDtypeStruct((t, qwid), BF16),
        compiler_params=_params("arbitrary", "arbitrary"),
        name="swa_attention",
    )(sinks_l, a_in, a_in, cos_t, sin_t, cos_t, sin_t, qw, kw, bdq, bdk)


def _mlstm_kernel(min_ref, gc_ref, gr_ref, cw_ref, cb_ref, brow_ref, bcol_ref, nw_ref,
                  hm_ref, ext_ref, act_ref, st_ref, mx_ref):
    tt = min_ref.shape[0]
    mw = M_HEADS * M_DIM
    nchunks = tt // CHUNK

    @pl.when(pl.program_id(1) == 0)
    def _():
        ext_ref[0:SUBLANES, :] = jnp.zeros((SUBLANES, 2 * mw), F32)
        st_ref[...] = jnp.zeros(st_ref.shape, F32)
        mx_ref[...] = jnp.zeros(mx_ref.shape, F32)

    u = min_ref[:, :2 * mw].astype(F32)
    ext_ref[SUBLANES:SUBLANES + tt, :] = u
    acc = cb_ref[...] + cw_ref[CONV_K - 1:CONV_K, :] * u
    for jj in range(CONV_K - 1):
        off = SUBLANES - (CONV_K - 1) + jj
        acc = acc + cw_ref[jj:jj + 1, :] * ext_ref[off:off + tt, :]
    act_ref[...] = acc * _sigmoid(acc)
    ext_ref[0:SUBLANES, :] = u[tt - SUBLANES:tt, :]

    ri = lax.broadcasted_iota(I32, (CHUNK, CHUNK), 0)
    ci = lax.broadcasted_iota(I32, (CHUNK, CHUNK), 1)
    causal = ci <= ri
    tril = jnp.where(causal, 1.0, 0.0)
    triu = jnp.where(ri <= ci, 1.0, 0.0)
    ones_col = jnp.where(lax.broadcasted_iota(I32, (CHUNK, M_DIM), 1) == 0, 1.0, 0.0).astype(BF16)
    kscale = M_DIM ** -0.5

    def chunk_body(c, carry):
        r0 = pl.multiple_of(c * CHUNK, CHUNK)
        rs = pl.ds(r0, CHUNK)
        gcol = gc_ref[rs, :] + brow_ref[...]
        grow = gr_ref[:, rs] + bcol_ref[...]
        bc_all = _dot(tril, _log_sigmoid(gcol), HIGHEST)
        br_all = _dot(_log_sigmoid(grow), triu, HIGHEST)
        for h in range(M_HEADS):
            hs = slice(h * M_DIM, (h + 1) * M_DIM)
            q = act_ref[rs, hs].astype(BF16)
            kf = act_ref[rs, mw + h * M_DIM:mw + (h + 1) * M_DIM] * kscale
            v = min_ref[rs, 2 * mw + h * M_DIM:2 * mw + (h + 1) * M_DIM]
            og = min_ref[rs, 3 * mw + h * M_DIM:3 * mw + (h + 1) * M_DIM].astype(F32)
            v_ext = jnp.concatenate([v, ones_col], axis=1)
            li_c = gcol[:, h:h + 1]
            li_r = grow[h:h + 1, :]
            b_c = bc_all[:, M_HEADS + h:M_HEADS + h + 1]
            b_r = br_all[M_HEADS + h:M_HEADS + h + 1, :]
            b_last = b_r[:, CHUNK - 1:CHUNK]
            m_prev = mx_ref[h][0:1, 0:1]
            state = st_ref[h]

            d = jnp.where(causal, b_c - b_r + li_r, -jnp.inf)
            g_c = b_c + m_prev
            m_t = jnp.maximum(jnp.max(d, axis=-1, keepdims=True), g_c)
            p = jnp.exp(d - m_t)
            inter = jnp.exp(g_c - m_t)
            qk = (_dot_nt(q, kf.astype(BF16)) * p).astype(BF16)
            num = _dot(qk, v_ext) + inter * _dot(q, state.astype(BF16))
            den = jnp.maximum(jnp.abs(num[:, M_DIM:M_DIM + 1]), jnp.exp(-m_t))
            hh = num[:, :M_DIM] / den
            hn = hh * lax.rsqrt(jnp.mean(hh * hh, axis=-1, keepdims=True) + EPS) * nw_ref[:, hs]
            hm_ref[rs, hs] = (_sigmoid(og) * hn).astype(BF16)

            w_r = b_last - b_r + li_r
            m_loc = jnp.max(w_r, axis=-1, keepdims=True)
            e_c = jnp.exp(b_last - b_c + li_c - m_loc)
            kt = (kf * e_c).T.astype(BF16)
            kv = _dot(kt, v_ext)
            m_new = jnp.maximum(b_last + m_prev, m_loc)
            a = jnp.exp(b_last + m_prev - m_new)
            sc = jnp.exp(m_loc - m_new)
            st_ref[h] = a * state + sc * kv
            mx_ref[h] = jnp.broadcast_to(m_new, (SUBLANES, LANES))
        return carry

    lax.fori_loop(0, nchunks, chunk_body, 0)


def _mlstm(m_in, gcol, grow, conv_w, conv_b, brow, bcol, nw, batch, seq):
    t = m_in.shape[0]
    tt = min(1024, seq)
    tpb = seq // tt
    mw = M_HEADS * M_DIM
    cur = lambda b, i: (b * tpb + i, 0)
    const = lambda b, i: (0, 0)
    return pl.pallas_call(
        _mlstm_kernel,
        grid=(batch, tpb),
        in_specs=[
            pl.BlockSpec((tt, 4 * mw), cur),
            pl.BlockSpec((tt, LANES), cur),
            pl.BlockSpec((SUBLANES, tt), lambda b, i: (0, b * tpb + i)),
            pl.BlockSpec((CONV_K, 2 * mw), const),
            pl.BlockSpec((1, 2 * mw), const),
            pl.BlockSpec((1, LANES), const),
            pl.BlockSpec((SUBLANES, LANES), const),
            pl.BlockSpec((1, mw), const),
        ],
        out_specs=pl.BlockSpec((tt, mw), cur),
        out_shape=jax.ShapeDtypeStruct((t, mw), BF16),
        scratch_shapes=[
            pltpu.VMEM((tt + SUBLANES, 2 * mw), F32),
            pltpu.VMEM((tt, 2 * mw), F32),
            pltpu.VMEM((M_HEADS, M_DIM, 2 * M_DIM), F32),
            pltpu.VMEM((M_HEADS, SUBLANES, LANES), F32),
        ],
        compiler_params=_params("arbitrary", "arbitrary"),
        name="mlstm",
    )(m_in, gcol, grow, conv_w, conv_b, brow, bcol, nw)


def _merge_kernel(o_ref, hm_ref, mg_ref, x_ref, g1_ref, sc_ref, sh_ref, nw_ref,
                  wa_ref, wm_ref, wo_ref, rwh_ref, rwl_ref, rb_ref, tri_ref,
                  x1_ref, pay_ref, route_ref, cnt_ref, carry_ref):
    tm, d = x_ref.shape

    @pl.when(pl.program_id(0) == 0)
    def _():
        carry_ref[...] = jnp.zeros(carry_ref.shape, F32)

    ya = _dot(o_ref[...], wa_ref[...])
    yb = _dot(hm_ref[...], wm_ref[...])
    mg = mg_ref[...]
    merged = mg[:, :d].astype(F32) * ya + mg[:, d:].astype(F32) * yb
    x1 = x_ref[...] + g1_ref[...] * _dot(merged.astype(BF16), wo_ref[...])
    x1_ref[...] = x1

    ms = jnp.mean(x1 * x1, axis=-1, keepdims=True)
    h2 = x1 * lax.rsqrt(ms + EPS) * nw_ref[...]
    h2 = h2 * (1.0 + sc_ref[...]) + sh_ref[...]
    hi = h2.astype(BF16)
    hif = hi.astype(F32)
    lo = (h2 - hif).astype(BF16)
    rwh = rwh_ref[...]
    logits = _dot(hi, rwh) + _dot(lo, rwh) + _dot(hi, rwl_ref[...])
    sc_t = _sigmoid(logits).T[:N_EXPERTS, :]
    sel_t = sc_t + rb_ref[:, 0:1]

    def row(a, e):
        return a[e:e + 1, :]

    best = None
    gi = jnp.zeros((1, tm), I32)
    for g in range(N_GROUPS):
        r = [row(sel_t, EPG * g + i) for i in range(EPG)]
        gs = None
        for i in range(EPG):
            for j in range(i + 1, EPG):
                pr = r[i] + r[j]
                gs = pr if gs is None else jnp.maximum(gs, pr)
        if best is None:
            best = gs
        else:
            upd = gs > best
            gi = jnp.where(upd, g, gi)
            best = jnp.maximum(best, gs)

    def pick(a, i):
        out = row(a, i)
        for g in range(1, N_GROUPS):
            out = jnp.where(gi == g, row(a, EPG * g + i), out)
        return out

    v = [pick(sel_t, i) for i in range(EPG)]
    s = [pick(sc_t, i) for i in range(EPG)]

    def argmax4(vals):
        bv, bi = vals[0], jnp.zeros((1, tm), I32)
        for i in range(1, EPG):
            upd = vals[i] > bv
            bi = jnp.where(upd, i, bi)
            bv = jnp.maximum(bv, vals[i])
        return bi

    i1 = argmax4(v)
    i2 = argmax4([jnp.where(i1 == i, -jnp.inf, v[i]) for i in range(EPG)])
    ia = jnp.minimum(i1, i2)
    ib = jnp.maximum(i1, i2)
    pidx = jnp.where(ia == 0, ib - 1, jnp.where(ia == 1, ib + 1, N_PAIRS - 1))
    bucket = gi * N_PAIRS + pidx

    def by_index(vals, idx):
        out = vals[0]
        for i in range(1, EPG):
            out = jnp.where(idx == i, vals[i], out)
        return out

    s_a, s_b = by_index(s, ia), by_index(s, ib)
    gate_a = s_a / (s_a + s_b)
    gate_b = s_b / (s_a + s_b)

    brow = lax.broadcasted_iota(I32, (BUCKET_ROWS, tm), 0)
    onehot = brow == bucket
    cums = _dot(jnp.where(onehot, 1.0, 0.0).astype(BF16), tri_ref[...])
    carry = carry_ref[...]
    rank = jnp.sum(jnp.where(onehot, carry[:, 0:1] + cums, 0.0), axis=0, keepdims=True) - 1.0
    new_carry = carry + cums[:, tm - 1:tm]
    carry_ref[...] = new_carry
    cnt_ref[...] = new_carry

    route_ref[...] = jnp.concatenate(
        [bucket.astype(F32), gate_a, gate_b, rank, jnp.zeros((SUBLANES - 4, tm), F32)], axis=0)

    bits = lax.bitcast_convert_type(hif, U32)
    half = d // 2
    pay_ref[:, :half] = bits[:, :half] | (bits[:, half:] >> 16)
    gates_t = jnp.concatenate([gate_a, gate_b, jnp.zeros((LANES - 2, tm), F32)], axis=0)
    pay_ref[:, half:] = lax.bitcast_convert_type(gates_t.T, U32)


def _merge(o_attn, hm, mg, x, g1, sc2, sh2, nw, wa, wm, wo, rwh, rwl, rb, tri, l, seq):
    t, d = x.shape
    tm = tri.shape[0]
    tpb = seq // tm
    row = lambda i: (i, 0)
    bsel = lambda i: (i // tpb, 0, 0)
    wsel = lambda i: (l, 0, 0)
    const = lambda i: (0, 0)
    hw = o_attn.shape[1]
    return pl.pallas_call(
        _merge_kernel,
        grid=(t // tm,),
        in_specs=[
            pl.BlockSpec((tm, hw), row),
            pl.BlockSpec((tm, hw), row),
            pl.BlockSpec((tm, 2 * d), row),
            pl.BlockSpec((tm, d), row),
            pl.BlockSpec((None, 1, d), bsel),
            pl.BlockSpec((None, 1, d), bsel),
            pl.BlockSpec((None, 1, d), bsel),
            pl.BlockSpec((None, 1, d), wsel),
            pl.BlockSpec((None, hw, d), wsel),
            pl.BlockSpec((None, hw, d), wsel),
            pl.BlockSpec((None, d, d), wsel),
            pl.BlockSpec((d, LANES), const),
            pl.BlockSpec((d, LANES), const),
            pl.BlockSpec((N_EXPERTS, LANES), const),
            pl.BlockSpec((tm, tm), const),
        ],
        out_specs=[
            pl.BlockSpec((tm, d), row),
            pl.BlockSpec((tm, PAY_W), row),
            pl.BlockSpec((SUBLANES, tm), lambda i: (0, i)),
            pl.BlockSpec((BUCKET_ROWS, LANES), const),
        ],
        out_shape=[
            jax.ShapeDtypeStruct((t, d), F32),
            jax.ShapeDtypeStruct((t, PAY_W), U32),
            jax.ShapeDtypeStruct((SUBLANES, t), F32),
            jax.ShapeDtypeStruct((BUCKET_ROWS, LANES), F32),
        ],
        scratch_shapes=[pltpu.VMEM((BUCKET_ROWS, LANES), F32)],
        compiler_params=_params("arbitrary"),
        name="merge_router",
    )(o_attn, hm, mg, x, g1, sc2, sh2, nw, wa, wm, wo, rwh, rwl, rb, tri)


def _start_row_gather(idx_ref, src_ref, buf_ref, slot, sem):
    for r in range(buf_ref.shape[1]):
        pltpu.make_async_copy(src_ref.at[pl.ds(idx_ref[0, r], 1)], buf_ref.at[slot, pl.ds(r, 1)],
                              sem.at[slot]).start()


def _wait_row_gather(src_ref, buf_ref, slot, sem):
    pltpu.make_async_copy(src_ref.at[pl.ds(0, buf_ref.shape[1])], buf_ref.at[slot], sem.at[slot]).wait()


def _invperm_kernel(dest_ref, src_ref):
    i = pl.program_id(0)
    ts = dest_ref.shape[0]

    @pl.when(i == 0)
    def _():
        def zero(r, c):
            src_ref[r] = 0
            return c

        lax.fori_loop(0, src_ref.shape[0], zero, 0, unroll=8)

    base = i * ts

    def body(r, c):
        src_ref[dest_ref[r]] = base + r
        return c

    lax.fori_loop(0, ts, body, 0, unroll=8)


def _invperm(dest, n_rows):
    t = dest.shape[0]
    ts = min(4096, t)
    return pl.pallas_call(
        _invperm_kernel,
        grid=(t // ts,),
        in_specs=[pl.BlockSpec((ts,), lambda i: (i,), memory_space=pltpu.SMEM)],
        out_specs=pl.BlockSpec((n_rows,), lambda i: (0,), memory_space=pltpu.SMEM),
        out_shape=jax.ShapeDtypeStruct((n_rows,), I32),
        compiler_params=_params("arbitrary"),
        name="invperm",
    )(dest)


def _combine_kernel(dcur_ref, dnext_ref, ys_ref, x_ref, g2_ref, o_ref, buf_ref, sem):
    i = pl.program_id(0)
    slot = i % 2

    @pl.when(i == 0)
    def _():
        _start_row_gather(dcur_ref, ys_ref, buf_ref, 0, sem)

    @pl.when(i + 1 < pl.num_programs(0))
    def _():
        _start_row_gather(dnext_ref, ys_ref, buf_ref, 1 - slot, sem)

    _wait_row_gather(ys_ref, buf_ref, slot, sem)
    o_ref[...] = x_ref[...] + g2_ref[...] * buf_ref[slot]


def _combine(dest, ys, x1, g2, seq):
    t, d = x1.shape
    tm = min(512, seq)
    tpb = seq // tm
    nt = t // tm
    dest3 = dest.reshape(nt, 1, tm)
    return pl.pallas_call(
        _combine_kernel,
        grid=(nt,),
        in_specs=[
            pl.BlockSpec((None, 1, tm), lambda i: (i, 0, 0), memory_space=pltpu.SMEM),
            pl.BlockSpec((None, 1, tm), lambda i: (jnp.minimum(i + 1, nt - 1), 0, 0), memory_space=pltpu.SMEM),
            pl.BlockSpec(memory_space=pl.ANY),
            pl.BlockSpec((tm, d), lambda i: (i, 0)),
            pl.BlockSpec((None, 1, d), lambda i: (i // tpb, 0, 0)),
        ],
        out_specs=pl.BlockSpec((tm, d), lambda i: (i, 0)),
        out_shape=jax.ShapeDtypeStruct((t, d), F32),
        scratch_shapes=[pltpu.VMEM((2, tm, d), F32), pltpu.SemaphoreType.DMA((2,))],
        compiler_params=_params("arbitrary"),
        name="combine_rows",
    )(dest3, dest3, ys, x1, g2)


def _expert_kernel(ea_ref, eb_ref, nr_ref, icur_ref, inext_ref, pay_ref, wgua_ref, wda_ref, wgub_ref, wdb_ref,
                   ys_ref, buf_ref, sem):
    j = pl.program_id(0)
    nr = nr_ref[0]
    slot = j % 2
    half = (PAY_W - LANES)

    @pl.when(j == 0)
    def _():
        _start_row_gather(icur_ref, pay_ref, buf_ref, 0, sem)

    @pl.when(j < nr)
    def _():
        _wait_row_gather(pay_ref, buf_ref, slot, sem)
        _start_row_gather(inext_ref, pay_ref, buf_ref, 1 - slot, sem)
        w = buf_ref[slot]
        pk = w[:, :half]
        xa = lax.bitcast_convert_type(pk & jnp.uint32(0xFFFF0000), F32)
        xb = lax.bitcast_convert_type(pk << 16, F32)
        x = jnp.concatenate([xa, xb], axis=1).astype(BF16)
        gl = lax.bitcast_convert_type(w[:, half:], F32)

        def ffn(wgu_ref, wd_ref):
            gu = _dot(x, wgu_ref[...])
            f = gu.shape[1] // 2
            gte, up = gu[:, :f], gu[:, f:]
            act = gte * _sigmoid(gte) * up
            return _dot(act.astype(BF16), wd_ref[...])

        ys_ref[...] = gl[:, 0:1] * ffn(wgua_ref, wda_ref) + gl[:, 1:2] * ffn(wgub_ref, wdb_ref)

    @pl.when(j >= nr)
    def _():
        ys_ref[...] = jnp.zeros(ys_ref.shape, F32)

    @pl.when(j == nr)
    def _():
        _wait_row_gather(pay_ref, buf_ref, slot, sem)


def _experts(blk_ea, blk_eb, n_real, src, pay, wgu, wd, d):
    n_rows = src.shape[0]
    nblk = n_rows // EXPERT_BLOCK
    f2 = wgu.shape[2]
    src3 = src.reshape(nblk, 1, EXPERT_BLOCK)
    grid_spec = pltpu.PrefetchScalarGridSpec(
        num_scalar_prefetch=3,
        grid=(nblk,),
        in_specs=[
            pl.BlockSpec((None, 1, EXPERT_BLOCK), lambda j, ea, eb, nr: (j, 0, 0), memory_space=pltpu.SMEM),
            pl.BlockSpec((None, 1, EXPERT_BLOCK), lambda j, ea, eb, nr: (jnp.minimum(j + 1, nblk - 1), 0, 0),
                         memory_space=pltpu.SMEM),
            pl.BlockSpec(memory_space=pl.ANY),
            pl.BlockSpec((None, d, f2), lambda j, ea, eb, nr: (ea[j], 0, 0)),
            pl.BlockSpec((None, f2 // 2, d), lambda j, ea, eb, nr: (ea[j], 0, 0)),
            pl.BlockSpec((None, d, f2), lambda j, ea, eb, nr: (eb[j], 0, 0)),
            pl.BlockSpec((None, f2 // 2, d), lambda j, ea, eb, nr: (eb[j], 0, 0)),
        ],
        out_specs=pl.BlockSpec((EXPERT_BLOCK, d), lambda j, ea, eb, nr: (j, 0)),
        scratch_shapes=[pltpu.VMEM((2, EXPERT_BLOCK, PAY_W), U32), pltpu.SemaphoreType.DMA((2,))],
    )
    return pl.pallas_call(
        _expert_kernel,
        grid_spec=grid_spec,
        out_shape=jax.ShapeDtypeStruct((n_rows, d), F32),
        compiler_params=_params("arbitrary"),
        name="experts",
    )(blk_ea, blk_eb, n_real, src3, src3, pay, wgu, wd, wgu, wd)


_PAIR_A = (0, 0, 0, 1, 1, 2)
_PAIR_B = (1, 2, 3, 2, 3, 3)


def kernel(x, c, positions, ada_w, ada_b, norm_mix_w, norm_ffn_w, w_in, b_igate, b_fgate, q_norm_w, k_norm_w,
           sinks, conv_w, conv_b, mlstm_norm_w, w_attn_up, w_mlstm_up, w_out, router_w, router_bias,
           w_gate, w_up, w_down):
    batch, seq, d = x.shape
    depth = w_in.shape[0]
    t = batch * seq
    qw = N_HEADS * HEAD_DIM
    kvw = N_KV * HEAD_DIM
    mw = M_HEADS * M_DIM

    o = 0
    cols = {}
    for name, wdt in (("q", qw), ("k", kvw), ("v", kvw), ("mqk", 2 * mw), ("mv", mw), ("mi", M_HEADS),
                      ("mf", M_HEADS), ("mo", mw), ("ga", d), ("gb", d)):
        cols[name] = (o, o + wdt)
        o += wdt

    def wc(name, lo=0, hi=None):
        s, e = cols[name]
        return w_in[:, :, s + lo:(s + hi if hi is not None else e)]

    k0, k1 = wc("k", 0, HEAD_DIM), wc("k", HEAD_DIM, 2 * HEAD_DIM)
    v0, v1 = wc("v", 0, HEAD_DIM), wc("v", HEAD_DIM, 2 * HEAD_DIM)
    w_a = jnp.concatenate([wc("q"), k0, k0, k1, k1, v0, v0, v1, v1], axis=2).astype(BF16)
    w_m = jnp.concatenate([wc("mqk"), wc("mv"), wc("mo")], axis=2).astype(BF16)
    w_g = jnp.concatenate([wc("mi"), wc("mf"), jnp.zeros((depth, d, LANES - 2 * M_HEADS), F32)], axis=2).astype(BF16)
    w_mg = jnp.concatenate([wc("ga"), wc("gb")], axis=2).astype(BF16)
    w_au = w_attn_up.astype(BF16)
    w_mu = w_mlstm_up.astype(BF16)
    w_o = w_out.astype(BF16)
    n_e = w_gate.shape[1]
    w_gu = jnp.concatenate([w_gate, w_up], axis=3).astype(BF16).reshape(depth * n_e, d, -1)
    w_d = w_down.astype(BF16).reshape(depth * n_e, -1, d)

    rw_pad = jnp.concatenate([router_w, jnp.zeros((d, LANES - n_e), F32)], axis=1)
    rw_hi = rw_pad.astype(BF16)
    rw_lo = (rw_pad - rw_hi.astype(F32)).astype(BF16)
    rb = jnp.broadcast_to(router_bias.astype(F32)[:, None], (n_e, LANES))

    qn_w = jnp.tile(q_norm_w, (1, N_HEADS)).reshape(depth, 1, qw)
    kn_w = jnp.tile(k_norm_w, (1, 2 * N_KV)).reshape(depth, 1, 2 * kvw)
    seg = jnp.arange(qw) // HEAD_DIM
    bdq = jnp.where(seg[:, None] == seg[None, :], 1.0 / HEAD_DIM, 0.0).astype(BF16)
    bdk = bdq[:2 * kvw, :2 * kvw]

    half = ROPE_DIM // 2
    inv_freq = ROPE_THETA ** (-(jnp.arange(0, ROPE_DIM, 2, dtype=F32) / ROPE_DIM))
    ang = positions.astype(F32).reshape(t, 1) * inv_freq
    cos8, sin8 = jnp.cos(ang), jnp.sin(ang)
    pad1 = jnp.ones((t, HEAD_DIM - ROPE_DIM), F32)
    pad0 = jnp.zeros((t, HEAD_DIM - ROPE_DIM), F32)
    cos_t = jnp.tile(jnp.concatenate([cos8, cos8, pad1], axis=1), (1, LANES // HEAD_DIM))
    sin_t = jnp.tile(jnp.concatenate([-sin8, sin8, pad0], axis=1), (1, LANES // HEAD_DIM))

    gate_bias = jnp.concatenate([b_igate, b_fgate], axis=1).astype(F32)
    brow = jnp.concatenate([gate_bias, jnp.zeros((depth, LANES - 2 * M_HEADS), F32)], axis=1)
    bcol = jnp.broadcast_to(gate_bias[:, :, None], (depth, 2 * M_HEADS, LANES))

    tm_merge = min(512, seq)
    ii = jnp.arange(tm_merge)
    tri = (ii[:, None] <= ii[None, :]).astype(BF16)

    n_blk = (t + N_BUCKETS * (EXPERT_BLOCK - 1)) // EXPERT_BLOCK + 1
    n_rows = n_blk * EXPERT_BLOCK
    pair_a = jnp.asarray(_PAIR_A, I32)
    pair_b = jnp.asarray(_PAIR_B, I32)

    c_pad = jnp.zeros((SUBLANES, d), F32).at[:batch].set(c)
    mod = _ada_mod(c_pad, ada_w, ada_b)[:, :batch]

    xf = x.reshape(t, d)
    for l in range(depth):
        sh1, sc1, g1, sh2, sc2, g2 = [m.reshape(batch, 1, d) for m in jnp.split(mod[l], 6, axis=-1)]

        a_in, m_in, mg, gcol, grow = _inproj(xf, sc1, sh1, norm_mix_w.reshape(depth, 1, d),
                                             w_a, w_m, w_g, w_mg, l, seq)
        o_attn = _attention(a_in, cos_t, sin_t, sinks[l], qn_w[l], kn_w[l], bdq, bdk, batch, seq)
        hm = _mlstm(m_in, gcol, grow, conv_w[l], conv_b[l].reshape(1, -1), brow[l:l + 1], bcol[l],
                    mlstm_norm_w[l].reshape(1, mw), batch, seq)
        x1, pay, route, cnt = _merge(o_attn, hm, mg, xf, g1, sc2, sh2, norm_ffn_w.reshape(depth, 1, d),
                                     w_au, w_mu, w_o, rw_hi, rw_lo, rb, tri, l, seq)

        counts = cnt[:N_BUCKETS, 0].astype(I32)
        padded = (counts + EXPERT_BLOCK - 1) // EXPERT_BLOCK * EXPERT_BLOCK
        pad_ends = jnp.cumsum(padded)
        pad_starts = pad_ends - padded
        bucket = route[0].astype(I32)
        dest = pad_starts[bucket] + route[3].astype(I32)
        blk_start = jnp.arange(n_blk, dtype=I32) * EXPERT_BLOCK
        blk_bucket = jnp.minimum(jnp.sum((pad_ends[None, :] <= blk_start[:, None]).astype(I32), axis=1), N_BUCKETS - 1)
        grp = blk_bucket // N_PAIRS
        blk_ea = (l * n_e + grp * EPG + pair_a[blk_bucket % N_PAIRS]).astype(I32)
        blk_eb = (l * n_e + grp * EPG + pair_b[blk_bucket % N_PAIRS]).astype(I32)
        n_real = (pad_ends[-1:] // EXPERT_BLOCK).astype(I32)

        src = _invperm(dest, n_rows)
        ys = _experts(blk_ea, blk_eb, n_real, src, pay, w_gu, w_d, d)
        xf = _combine(dest, ys, x1, g2, seq)
    return xf.reshape(batch, seq, d)
```

```python
import functools

import jax
import jax.numpy as jnp
from jax import lax
from jax.experimental import pallas as pl
from jax.experimental.pallas import tpu as pltpu

F32 = jnp.float32
BF16 = jnp.bfloat16
U32 = jnp.uint32
I32 = jnp.int32
HIGHEST = lax.Precision.HIGHEST

HEAD_DIM = 64
N_HEADS = 8
N_KV = 2
ROPE_DIM = 16
ROPE_THETA = 500000.0
ATTN_BLOCK = 128
M_HEADS = 4
M_DIM = 128
CONV_K = 4
N_EXPERTS = 16
N_GROUPS = 4
EPG = 4
EPS = 1e-6

LANES = 128
SUBLANES = 8

CHUNK = 128
CHUNKS_PER_STEP = 4
GATE_CHUNKS_PER_STEP = 4
N_PAIRS = 6
N_BUCKETS = N_GROUPS * N_PAIRS
BUCKET_ROWS = 32
EXPERT_BLOCK = 256
PAY_W = 640
VMEM_LIMIT = 56 * 1024 * 1024


def _dot(a, b, precision=None):
    return jnp.dot(a, b, preferred_element_type=F32, precision=precision)


def _dot_nt(a, b):
    return lax.dot_general(a, b, (((1,), (1,)), ((), ())), preferred_element_type=F32)


def _sigmoid(x):
    return 1.0 / (1.0 + jnp.exp(-x))


def _log_sigmoid(x):
    return jnp.minimum(x, 0.0) - jnp.log1p(jnp.exp(-jnp.abs(x)))


def _params(*sem):
    return pltpu.CompilerParams(dimension_semantics=sem, vmem_limit_bytes=VMEM_LIMIT)


def _ada_kernel(c_ref, w_ref, b_ref, o_ref):
    c = c_ref[...]
    ca = c * _sigmoid(c)
    o_ref[0] = _dot(ca, w_ref[0], HIGHEST) + b_ref[0]


def _ada_mod(c_pad, ada_w, ada_b):
    depth, d, n = ada_w.shape
    tn = 1536
    return pl.pallas_call(
        _ada_kernel,
        grid=(depth, n // tn),
        in_specs=[
            pl.BlockSpec((SUBLANES, d), lambda l, j: (0, 0)),
            pl.BlockSpec((1, d, tn), lambda l, j: (l, 0, j)),
            pl.BlockSpec((1, 1, tn), lambda l, j: (l, 0, j)),
        ],
        out_specs=pl.BlockSpec((1, SUBLANES, tn), lambda l, j: (l, 0, j)),
        out_shape=jax.ShapeDtypeStruct((depth, SUBLANES, n), F32),
        compiler_params=_params("arbitrary", "arbitrary"),
        name="ada_mod",
    )(c_pad, ada_w, ada_b.reshape(depth, 1, n))


def _inproj_kernel(x_ref, sc_ref, sh_ref, nw_ref, wa_ref, wm_ref, wg_ref, wmg_ref,
                   a_ref, m_ref, mg_ref, gr_ref, g_ref):
    x = x_ref[...]
    ms = jnp.mean(x * x, axis=-1, keepdims=True)
    h = x * lax.rsqrt(ms + EPS) * nw_ref[...]
    h = h * (1.0 + sc_ref[...]) + sh_ref[...]
    hb = h.astype(BF16)
    a_ref[...] = _dot(hb, wa_ref[...]).astype(BF16)
    m_ref[...] = _dot(hb, wm_ref[...]).astype(BF16)
    mg_ref[...] = _sigmoid(_dot(hb, wmg_ref[...])).astype(BF16)
    g_ref[...] = _dot(hb, wg_ref[...])
    gr_ref[...] = g_ref[...].T[:SUBLANES, :]


def _inproj(x, sc, sh, nw, wa, wm, wg, wmg, l, seq):
    t, d = x.shape
    tm = min(512, seq)
    tpb = seq // tm
    row = lambda i: (i, 0)
    bsel = lambda i: (i // tpb, 0, 0)
    wsel = lambda i: (l, 0, 0)
    na, nm, ng, nmg = wa.shape[2], wm.shape[2], wg.shape[2], wmg.shape[2]
    return pl.pallas_call(
        _inproj_kernel,
        grid=(t // tm,),
        in_specs=[
            pl.BlockSpec((tm, d), row),
            pl.BlockSpec((None, 1, d), bsel),
            pl.BlockSpec((None, 1, d), bsel),
            pl.BlockSpec((None, 1, d), wsel),
            pl.BlockSpec((None, d, na), wsel),
            pl.BlockSpec((None, d, nm), wsel),
            pl.BlockSpec((None, d, ng), wsel),
            pl.BlockSpec((None, d, nmg), wsel),
        ],
        out_specs=[
            pl.BlockSpec((tm, na), row),
            pl.BlockSpec((tm, nm), row),
            pl.BlockSpec((tm, nmg), row),
            pl.BlockSpec((SUBLANES, tm), lambda i: (0, i)),
        ],
        out_shape=[
            jax.ShapeDtypeStruct((t, na), BF16),
            jax.ShapeDtypeStruct((t, nm), BF16),
            jax.ShapeDtypeStruct((t, nmg), BF16),
            jax.ShapeDtypeStruct((SUBLANES, t), F32),
        ],
        scratch_shapes=[pltpu.VMEM((tm, ng), F32)],
        compiler_params=_params("arbitrary"),
        name="inproj",
    )(x, sc, sh, nw, wa, wm, wg, wmg)


def _rope(t, cos, sin):
    w = t.shape[1]
    reps = w // LANES
    cosw = jnp.concatenate([cos] * reps, axis=1) if reps > 1 else cos
    sinw = jnp.concatenate([sin] * reps, axis=1) if reps > 1 else sin
    lane = lax.broadcasted_iota(I32, t.shape, 1)
    half = ROPE_DIM // 2
    up = pltpu.roll(t, w - half, axis=1)
    dn = pltpu.roll(t, half, axis=1)
    partner = jnp.where((lane % ROPE_DIM) < half, up, dn)
    return t * cosw + partner * sinw


def _head_norm(t, bd, w):
    ms = _dot((t * t).astype(BF16), bd)
    return t * lax.rsqrt(ms + EPS) * w


def _attn_kernel(sink_ref, cur_ref, prev_ref, cos_ref, sin_ref, cosp_ref, sinp_ref,
                 qw_ref, kw_ref, bdq_ref, bdk_ref, o_ref):
    tq = cur_ref.shape[0]
    nj = tq // ATTN_BLOCK
    qw = N_HEADS * HEAD_DIM
    kw = 2 * N_KV * HEAD_DIM
    blk0 = pl.program_id(1) * nj

    cur = cur_ref[...]
    q = cur[:, :qw].astype(F32)
    kc = cur[:, qw:qw + kw].astype(F32)
    vc = cur[:, qw + kw:]
    prev = prev_ref[...]
    kp = prev[:, :kw].astype(F32)
    vp = prev[:, kw:]

    cos, sin = cos_ref[...], sin_ref[...]
    q = _rope(_head_norm(q, bdq_ref[...], qw_ref[...]), cos, sin) * (HEAD_DIM ** -0.5)
    kc = _rope(_head_norm(kc, bdk_ref[...], kw_ref[...]), cos, sin)
    kp = _rope(_head_norm(kp, bdk_ref[...], kw_ref[...]), cosp_ref[...], sinp_ref[...])
    qb = q.astype(BF16)
    k_all = jnp.concatenate([kp, kc], axis=0).astype(BF16)
    v_all = jnp.concatenate([vp, vc], axis=0)

    lane = lax.broadcasted_iota(I32, (ATTN_BLOCK, LANES), 1)
    lo = lane < HEAD_DIM
    zero = jnp.zeros((ATTN_BLOCK, LANES), BF16)
    g_heads = N_HEADS // N_KV
    rr = lax.broadcasted_iota(I32, (g_heads * ATTN_BLOCK, 2 * ATTN_BLOCK), 0)
    ri = rr % ATTN_BLOCK
    ci = lax.broadcasted_iota(I32, (g_heads * ATTN_BLOCK, 2 * ATTN_BLOCK), 1)
    dist = ri + ATTN_BLOCK - ci
    head_row = lax.broadcasted_iota(I32, (g_heads * ATTN_BLOCK, 1), 0) // ATTN_BLOCK

    for j in range(nj):
        rows = slice(j * ATTN_BLOCK, (j + 1) * ATTN_BLOCK)
        has_prev = jnp.minimum(blk0 + j, 1)
        mask = (dist >= 0) & (dist < jnp.minimum(ATTN_BLOCK, ri + 1 + has_prev * ATTN_BLOCK))
        for g in range(N_KV):
            kb = k_all[j * ATTN_BLOCK:(j + 2) * ATTN_BLOCK, g * LANES:(g + 1) * LANES]
            vb = v_all[j * ATTN_BLOCK:(j + 2) * ATTN_BLOCK, g * LANES:(g + 1) * LANES]
            qp0 = qb[rows, (2 * g) * LANES:(2 * g + 1) * LANES]
            qp1 = qb[rows, (2 * g + 1) * LANES:(2 * g + 2) * LANES]
            q4 = jnp.concatenate([jnp.where(lo, qp0, zero), jnp.where(lo, zero, qp0),
                                  jnp.where(lo, qp1, zero), jnp.where(lo, zero, qp1)], axis=0)
            s = _dot_nt(q4, kb)
            s = jnp.where(mask, s, -jnp.inf)
            sink = jnp.full((g_heads * ATTN_BLOCK, 1), sink_ref[g_heads * g], F32)
            for r in range(1, g_heads):
                sink = jnp.where(head_row == r, sink_ref[g_heads * g + r], sink)
            m = jnp.maximum(jnp.max(s, axis=-1, keepdims=True), sink)
            p = jnp.exp(s - m)
            den = jnp.sum(p, axis=-1, keepdims=True) + jnp.exp(sink - m)
            o4 = _dot(p.astype(BF16), vb) / den
            b = ATTN_BLOCK
            o_ref[rows, (2 * g) * LANES:(2 * g + 1) * LANES] = jnp.where(lo, o4[0:b], o4[b:2 * b]).astype(BF16)
            o_ref[rows, (2 * g + 1) * LANES:(2 * g + 2) * LANES] = jnp.where(
                lo, o4[2 * b:3 * b], o4[3 * b:4 * b]).astype(BF16)


def _attention(a_in, cos_t, sin_t, sinks_l, qw, kw, bdq, bdk, batch, seq):
    t = a_in.shape[0]
    tq = min(512, seq)
    nj = tq // ATTN_BLOCK
    tpb = seq // tq
    bpb = seq // ATTN_BLOCK
    qwid = N_HEADS * HEAD_DIM
    kvw = 4 * N_KV * HEAD_DIM
    cur = lambda b, i: (b * tpb + i, 0)
    prv = lambda b, i: (b * bpb + jnp.maximum(i * nj - 1, 0), 1)
    prv0 = lambda b, i: (b * bpb + jnp.maximum(i * nj - 1, 0), 0)
    const = lambda b, i: (0, 0)
    return pl.pallas_call(
        _attn_kernel,
        grid=(batch, tpb),
        in_specs=[
            pl.BlockSpec(memory_space=pltpu.SMEM),
            pl.BlockSpec((tq, qwid + kvw), cur),
            pl.BlockSpec((ATTN_BLOCK, kvw), prv),
            pl.BlockSpec((tq, LANES), cur),
            pl.BlockSpec((tq, LANES), cur),
            pl.BlockSpec((ATTN_BLOCK, LANES), prv0),
            pl.BlockSpec((ATTN_BLOCK, LANES), prv0),
            pl.BlockSpec((1, qwid), const),
            pl.BlockSpec((1, kvw // 2), const),
            pl.BlockSpec((qwid, qwid), const),
            pl.BlockSpec((kvw // 2, kvw // 2), const),
        ],
        out_specs=pl.BlockSpec((tq, qwid), cur),
        out_shape=jax.ShapeDtypeStruct((t, qwid), BF16),
        compiler_params=_params("arbitrary", "arbitrary"),
        name="swa_attention",
    )(sinks_l, a_in, a_in, cos_t, sin_t, cos_t, sin_t, qw, kw, bdq, bdk)


def _mlstm_kernel(min_ref, gr_ref, cw_ref, cb_ref, bcol_ref, nw_ref,
                  hm_ref, ext_ref, q_ref, kt_ref, st_ref, mx_ref, ab_ref, bc_ref):
    tt = min_ref.shape[0]
    mw = M_HEADS * M_DIM
    nchunks = tt // CHUNK

    @pl.when(pl.program_id(1) == 0)
    def _():
        ext_ref[0:SUBLANES, :] = jnp.zeros((SUBLANES, 2 * mw), F32)
        st_ref[...] = jnp.zeros(st_ref.shape, F32)
        mx_ref[...] = jnp.zeros(mx_ref.shape, F32)

    u = min_ref[:, :2 * mw].astype(F32)
    ext_ref[SUBLANES:SUBLANES + tt, :] = u
    acc = cb_ref[...] + cw_ref[CONV_K - 1:CONV_K, :] * u
    for jj in range(CONV_K - 1):
        off = SUBLANES - (CONV_K - 1) + jj
        acc = acc + cw_ref[jj:jj + 1, :] * ext_ref[off:off + tt, :]
    act = acc * _sigmoid(acc)
    ext_ref[0:SUBLANES, :] = u[tt - SUBLANES:tt, :]
    q_ref[...] = act[:, :mw].astype(BF16)
    for j in range(nchunks):
        for h in range(M_HEADS):
            kt_ref[h * M_DIM:(h + 1) * M_DIM, j * CHUNK:(j + 1) * CHUNK] = (
                act[j * CHUNK:(j + 1) * CHUNK, mw + h * M_DIM:mw + (h + 1) * M_DIM] * (M_DIM ** -0.5)).T

    ri = lax.broadcasted_iota(I32, (CHUNK, CHUNK), 0)
    ci = lax.broadcasted_iota(I32, (CHUNK, CHUNK), 1)
    causal = ci <= ri
    tril = jnp.where(causal, 1.0, 0.0).astype(BF16)
    triu = jnp.where(ri <= ci, 1.0, 0.0).astype(BF16)
    ones_half = jnp.ones((CHUNK, M_DIM), BF16)
    mean_mat = jnp.full((M_DIM, M_DIM), 1.0 / M_DIM, BF16)
    sub = lax.broadcasted_iota(I32, (SUBLANES, CHUNK), 0)
    heads = range(M_HEADS)

    def split3(v):
        v1 = v.astype(BF16)
        r1 = v - v1.astype(F32)
        v2 = r1.astype(BF16)
        v3 = (r1 - v2.astype(F32)).astype(BF16)
        return v1, v2, v3

    pad_rows = jnp.zeros((CHUNK - SUBLANES, CHUNK), F32)
    zero_rows = jnp.zeros((SUBLANES, CHUNK), F32)

    def gate_body(jg, carry):
        for u_ in range(GATE_CHUNKS_PER_STEP):
            rs = pl.ds(pl.multiple_of((jg * GATE_CHUNKS_PER_STEP + u_) * CHUNK, CHUNK), CHUNK)
            gr = gr_ref[:, rs] + bcol_ref[...]
            ls = _log_sigmoid(gr)
            ls1 = ls.astype(BF16).astype(F32)
            ls2 = (ls - ls1).astype(BF16).astype(F32)
            pieces = jnp.concatenate([ls1, ls2, ls - ls1 - ls2, zero_rows], axis=0).astype(BF16)
            sums = _dot(pieces, triu)
            br = sums[0:SUBLANES] + sums[SUBLANES:2 * SUBLANES] + sums[2 * SUBLANES:3 * SUBLANES]
            ab = jnp.where(sub < M_HEADS, gr - pltpu.roll(br, M_HEADS, axis=0), br)
            ab_ref[:, rs] = ab
            bc_ref[rs, :] = jnp.concatenate([ab, pad_rows], axis=0).T
        return carry

    lax.fori_loop(0, nchunks // GATE_CHUNKS_PER_STEP, gate_body, 0)

    def group_body(cg, carry):
        rows, ab = [], []
        for u_ in range(CHUNKS_PER_STEP):
            r0 = pl.multiple_of((cg * CHUNKS_PER_STEP + u_) * CHUNK, CHUNK)
            rows.append(pl.ds(r0, CHUNK))
            ab.append(ab_ref[:, rows[u_]])
        lanes = [(u_, h) for u_ in range(CHUNKS_PER_STEP) for h in heads]
        a_r = {(u_, h): ab[u_][h:h + 1, :] for u_, h in lanes}
        b_last = {(u_, h): ab[u_][M_HEADS + h:M_HEADS + h + 1, CHUNK - 1:CHUNK] for u_, h in lanes}

        m_prev, a_max, a_dec, s_in = {}, {}, {}, {}
        m_run = [mx_ref[h][0:1, 0:1] for h in heads]
        for k in lanes:
            u_, h = k
            m_prev[k] = m_run[h]
            a_max[k] = jnp.max(a_r[k], axis=-1, keepdims=True)
            m_loc = b_last[k] + a_max[k]
            m_new = jnp.maximum(b_last[k] + m_prev[k], m_loc)
            a_dec[k] = jnp.exp(b_last[k] + m_prev[k] - m_new)
            s_in[k] = jnp.exp(m_loc - m_new)
            m_run[h] = m_new
        for h in heads:
            mx_ref[h] = jnp.broadcast_to(m_run[h], (SUBLANES, LANES))

        q, v_ext, s_qk, kv = {}, {}, {}, {}
        for k in lanes:
            u_, h = k
            rs = rows[u_]
            q[k] = q_ref[rs, h * M_DIM:(h + 1) * M_DIM]
            kt = kt_ref[h * M_DIM:(h + 1) * M_DIM, rs]
            v = min_ref[rs, 2 * mw + h * M_DIM:2 * mw + (h + 1) * M_DIM]
            v_ext[k] = jnp.concatenate([v, ones_half], axis=1)
            s_qk[k] = _dot(q[k], kt.astype(BF16))
            e_r = jnp.exp(a_r[k] - a_max[k])
            kv[k] = _dot((kt * e_r).astype(BF16), v_ext[k])

        thr, qk, inter = {}, {}, {}
        for k in lanes:
            u_, h = k
            a_mat = jnp.where(causal, a_r[k], -jnp.inf)
            mu = jnp.maximum(jnp.max(a_mat, axis=-1, keepdims=True), m_prev[k])
            b_c = bc_ref[rows[u_], M_HEADS + h:M_HEADS + h + 1]
            thr[k] = jnp.broadcast_to(jnp.exp(-(b_c + mu)), (CHUNK, M_DIM))
            mu_b = jnp.broadcast_to(mu, (CHUNK, CHUNK))
            inter[k] = jnp.exp(m_prev[k] - mu_b)
            qk[k] = (s_qk[k] * jnp.exp(a_mat - mu_b)).astype(BF16)

        q_state = {}
        state = [st_ref[h] for h in heads]
        for k in lanes:
            u_, h = k
            q_state[k] = _dot(q[k], state[h].astype(BF16))
            state[h] = a_dec[k] * state[h] + s_in[k] * kv[k]
        for h in heads:
            st_ref[h] = state[h]

        for k in lanes:
            u_, h = k
            hs = slice(h * M_DIM, (h + 1) * M_DIM)
            num = _dot(qk[k], v_ext[k])
            den = jnp.maximum(jnp.abs(num[:, M_DIM:] + inter[k] * q_state[k][:, M_DIM:]), thr[k])
            hh = (num[:, :M_DIM] + inter[k] * q_state[k][:, :M_DIM]) / den
            msq = _dot((hh * hh).astype(BF16), mean_mat)
            hn = hh * lax.rsqrt(msq + EPS) * nw_ref[:, hs]
            og = min_ref[rows[u_], 3 * mw + h * M_DIM:3 * mw + (h + 1) * M_DIM].astype(F32)
            hm_ref[rows[u_], hs] = (_sigmoid(og) * hn).astype(BF16)
        return carry

    lax.fori_loop(0, nchunks // CHUNKS_PER_STEP, group_body, 0)


def _mlstm(m_in, grow, conv_w, conv_b, bcol, nw, batch, seq):
    t = m_in.shape[0]
    tt = min(1024, seq)
    tpb = seq // tt
    mw = M_HEADS * M_DIM
    cur = lambda b, i: (b * tpb + i, 0)
    const = lambda b, i: (0, 0)
    return pl.pallas_call(
        _mlstm_kernel,
        grid=(batch, tpb),
        in_specs=[
            pl.BlockSpec((tt, 4 * mw), cur),
            pl.BlockSpec((SUBLANES, tt), lambda b, i: (0, b * tpb + i)),
            pl.BlockSpec((CONV_K, 2 * mw), const),
            pl.BlockSpec((1, 2 * mw), const),
            pl.BlockSpec((SUBLANES, LANES), const),
            pl.BlockSpec((1, mw), const),
        ],
        out_specs=pl.BlockSpec((tt, mw), cur),
        out_shape=jax.ShapeDtypeStruct((t, mw), BF16),
        scratch_shapes=[
            pltpu.VMEM((tt + SUBLANES, 2 * mw), F32),
            pltpu.VMEM((tt, mw), BF16),
            pltpu.VMEM((mw, tt), F32),
            pltpu.VMEM((M_HEADS, M_DIM, 2 * M_DIM), F32),
            pltpu.VMEM((M_HEADS, SUBLANES, LANES), F32),
            pltpu.VMEM((SUBLANES, tt), F32),
            pltpu.VMEM((tt, LANES), F32),
        ],
        compiler_params=_params("arbitrary", "arbitrary"),
        name="mlstm",
    )(m_in, grow, conv_w, conv_b, bcol, nw)


def _merge_kernel(o_ref, hm_ref, mg_ref, x_ref, g1_ref, sc_ref, sh_ref, nw_ref,
                  wa_ref, wm_ref, wo_ref, rwh_ref, rwl_ref, rb_ref, tri_ref,
                  x1_ref, pay_ref, route_ref, cnt_ref, carry_ref):
    tm, d = x_ref.shape

    @pl.when(pl.program_id(0) == 0)
    def _():
        carry_ref[...] = jnp.zeros(carry_ref.shape, F32)

    ya = _dot(o_ref[...], wa_ref[...])
    yb = _dot(hm_ref[...], wm_ref[...])
    mg = mg_ref[...]
    merged = mg[:, :d].astype(F32) * ya + mg[:, d:].astype(F32) * yb
    x1 = x_ref[...] + g1_ref[...] * _dot(merged.astype(BF16), wo_ref[...])
    x1_ref[...] = x1

    ms = jnp.mean(x1 * x1, axis=-1, keepdims=True)
    h2 = x1 * lax.rsqrt(ms + EPS) * nw_ref[...]
    h2 = h2 * (1.0 + sc_ref[...]) + sh_ref[...]
    hi = h2.astype(BF16)
    hif = hi.astype(F32)
    lo = (h2 - hif).astype(BF16)
    rwh = rwh_ref[...]
    logits = _dot(hi, rwh) + _dot(lo, rwh) + _dot(hi, rwl_ref[...])
    sc_t = _sigmoid(logits).T[:N_EXPERTS, :]
    sel_t = sc_t + rb_ref[:, 0:1]

    def row(a, e):
        return a[e:e + 1, :]

    best = None
    gi = jnp.zeros((1, tm), I32)
    for g in range(N_GROUPS):
        r = [row(sel_t, EPG * g + i) for i in range(EPG)]
        gs = None
        for i in range(EPG):
            for j in range(i + 1, EPG):
                pr = r[i] + r[j]
                gs = pr if gs is None else jnp.maximum(gs, pr)
        if best is None:
            best = gs
        else:
            upd = gs > best
            gi = jnp.where(upd, g, gi)
            best = jnp.maximum(best, gs)

    def pick(a, i):
        out = row(a, i)
        for g in range(1, N_GROUPS):
            out = jnp.where(gi == g, row(a, EPG * g + i), out)
        return out

    v = [pick(sel_t, i) for i in range(EPG)]
    s = [pick(sc_t, i) for i in range(EPG)]

    def argmax4(vals):
        bv, bi = vals[0], jnp.zeros((1, tm), I32)
        for i in range(1, EPG):
            upd = vals[i] > bv
            bi = jnp.where(upd, i, bi)
            bv = jnp.maximum(bv, vals[i])
        return bi

    i1 = argmax4(v)
    i2 = argmax4([jnp.where(i1 == i, -jnp.inf, v[i]) for i in range(EPG)])
    ia = jnp.minimum(i1, i2)
    ib = jnp.maximum(i1, i2)
    pidx = jnp.where(ia == 0, ib - 1, jnp.where(ia == 1, ib + 1, N_PAIRS - 1))
    bucket = gi * N_PAIRS + pidx

    def by_index(vals, idx):
        out = vals[0]
        for i in range(1, EPG):
            out = jnp.where(idx == i, vals[i], out)
        return out

    s_a, s_b = by_index(s, ia), by_index(s, ib)
    gate_a = s_a / (s_a + s_b)
    gate_b = s_b / (s_a + s_b)

    brow = lax.broadcasted_iota(I32, (BUCKET_ROWS, tm), 0)
    onehot = brow == bucket
    cums = _dot(jnp.where(onehot, 1.0, 0.0).astype(BF16), tri_ref[...])
    carry = carry_ref[...]
    rank = jnp.sum(jnp.where(onehot, carry[:, 0:1] + cums, 0.0), axis=0, keepdims=True) - 1.0
    new_carry = carry + cums[:, tm - 1:tm]
    carry_ref[...] = new_carry
    cnt_ref[...] = new_carry

    route_ref[...] = jnp.concatenate(
        [bucket.astype(F32), gate_a, gate_b, rank, jnp.zeros((SUBLANES - 4, tm), F32)], axis=0)

    bits = lax.bitcast_convert_type(hif, U32)
    half = d // 2
    pay_ref[:, :half] = bits[:, :half] | (bits[:, half:] >> 16)
    gates_t = jnp.concatenate([gate_a, gate_b, jnp.zeros((LANES - 2, tm), F32)], axis=0)
    pay_ref[:, half:] = lax.bitcast_convert_type(gates_t.T, U32)


def _merge(o_attn, hm, mg, x, g1, sc2, sh2, nw, wa, wm, wo, rwh, rwl, rb, tri, l, seq):
    t, d = x.shape
    tm = tri.shape[0]
    tpb = seq // tm
    row = lambda i: (i, 0)
    bsel = lambda i: (i // tpb, 0, 0)
    wsel = lambda i: (l, 0, 0)
    const = lambda i: (0, 0)
    hw = o_attn.shape[1]
    return pl.pallas_call(
        _merge_kernel,
        grid=(t // tm,),
        in_specs=[
            pl.BlockSpec((tm, hw), row),
            pl.BlockSpec((tm, hw), row),
            pl.BlockSpec((tm, 2 * d), row),
            pl.BlockSpec((tm, d), row),
            pl.BlockSpec((None, 1, d), bsel),
            pl.BlockSpec((None, 1, d), bsel),
            pl.BlockSpec((None, 1, d), bsel),
            pl.BlockSpec((None, 1, d), wsel),
            pl.BlockSpec((None, hw, d), wsel),
            pl.BlockSpec((None, hw, d), wsel),
            pl.BlockSpec((None, d, d), wsel),
            pl.BlockSpec((d, LANES), const),
            pl.BlockSpec((d, LANES), const),
            pl.BlockSpec((N_EXPERTS, LANES), const),
            pl.BlockSpec((tm, tm), const),
        ],
        out_specs=[
            pl.BlockSpec((tm, d), row),
            pl.BlockSpec((tm, PAY_W), row),
            pl.BlockSpec((SUBLANES, tm), lambda i: (0, i)),
            pl.BlockSpec((BUCKET_ROWS, LANES), const),
        ],
        out_shape=[
            jax.ShapeDtypeStruct((t, d), F32),
            jax.ShapeDtypeStruct((t, PAY_W), U32),
            jax.ShapeDtypeStruct((SUBLANES, t), F32),
            jax.ShapeDtypeStruct((BUCKET_ROWS, LANES), F32),
        ],
        scratch_shapes=[pltpu.VMEM((BUCKET_ROWS, LANES), F32)],
        compiler_params=_params("arbitrary"),
        name="merge_router",
    )(o_attn, hm, mg, x, g1, sc2, sh2, nw, wa, wm, wo, rwh, rwl, rb, tri)


def _start_row_gather(idx_ref, src_ref, buf_ref, slot, sem):
    for r in range(buf_ref.shape[1]):
        pltpu.make_async_copy(src_ref.at[pl.ds(idx_ref[0, r], 1)], buf_ref.at[slot, pl.ds(r, 1)],
                              sem.at[slot]).start()


def _wait_row_gather(src_ref, buf_ref, slot, sem):
    pltpu.make_async_copy(src_ref.at[pl.ds(0, buf_ref.shape[1])], buf_ref.at[slot], sem.at[slot]).wait()


def _invperm_kernel(dest_ref, src_ref):
    i = pl.program_id(0)
    ts = dest_ref.shape[0]

    @pl.when(i == 0)
    def _():
        def zero(r, c):
            src_ref[r] = 0
            return c

        lax.fori_loop(0, src_ref.shape[0], zero, 0, unroll=8)

    base = i * ts

    def body(r, c):
        src_ref[dest_ref[r]] = base + r
        return c

    lax.fori_loop(0, ts, body, 0, unroll=8)


def _invperm(dest, n_rows):
    t = dest.shape[0]
    ts = min(4096, t)
    return pl.pallas_call(
        _invperm_kernel,
        grid=(t // ts,),
        in_specs=[pl.BlockSpec((ts,), lambda i: (i,), memory_space=pltpu.SMEM)],
        out_specs=pl.BlockSpec((n_rows,), lambda i: (0,), memory_space=pltpu.SMEM),
        out_shape=jax.ShapeDtypeStruct((n_rows,), I32),
        compiler_params=_params("arbitrary"),
        name="invperm",
    )(dest)


def _combine_kernel(dcur_ref, dnext_ref, ys_ref, x_ref, g2_ref, o_ref, buf_ref, sem):
    i = pl.program_id(0)
    slot = i % 2

    @pl.when(i == 0)
    def _():
        _start_row_gather(dcur_ref, ys_ref, buf_ref, 0, sem)

    @pl.when(i + 1 < pl.num_programs(0))
    def _():
        _start_row_gather(dnext_ref, ys_ref, buf_ref, 1 - slot, sem)

    _wait_row_gather(ys_ref, buf_ref, slot, sem)
    o_ref[...] = x_ref[...] + g2_ref[...] * buf_ref[slot]


def _combine(dest, ys, x1, g2, seq):
    t, d = x1.shape
    tm = min(512, seq)
    tpb = seq // tm
    nt = t // tm
    dest3 = dest.reshape(nt, 1, tm)
    return pl.pallas_call(
        _combine_kernel,
        grid=(nt,),
        in_specs=[
            pl.BlockSpec((None, 1, tm), lambda i: (i, 0, 0), memory_space=pltpu.SMEM),
            pl.BlockSpec((None, 1, tm), lambda i: (jnp.minimum(i + 1, nt - 1), 0, 0), memory_space=pltpu.SMEM),
            pl.BlockSpec(memory_space=pl.ANY),
            pl.BlockSpec((tm, d), lambda i: (i, 0)),
            pl.BlockSpec((None, 1, d), lambda i: (i // tpb, 0, 0)),
        ],
        out_specs=pl.BlockSpec((tm, d), lambda i: (i, 0)),
        out_shape=jax.ShapeDtypeStruct((t, d), F32),
        scratch_shapes=[pltpu.VMEM((2, tm, d), F32), pltpu.SemaphoreType.DMA((2,))],
        compiler_params=_params("arbitrary"),
        name="combine_rows",
    )(dest3, dest3, ys, x1, g2)


def _expert_kernel(ea_ref, eb_ref, nr_ref, icur_ref, inext_ref, pay_ref, wgua_ref, wda_ref, wgub_ref, wdb_ref,
                   ys_ref, buf_ref, sem):
    j = pl.program_id(0)
    nr = nr_ref[0]
    slot = j % 2
    half = (PAY_W - LANES)

    @pl.when(j == 0)
    def _():
        _start_row_gather(icur_ref, pay_ref, buf_ref, 0, sem)

    @pl.when(j < nr)
    def _():
        _wait_row_gather(pay_ref, buf_ref, slot, sem)
        _start_row_gather(inext_ref, pay_ref, buf_ref, 1 - slot, sem)
        w = buf_ref[slot]
        pk = w[:, :half]
        xa = lax.bitcast_convert_type(pk & jnp.uint32(0xFFFF0000), F32)
        xb = lax.bitcast_convert_type(pk << 16, F32)
        x = jnp.concatenate([xa, xb], axis=1).astype(BF16)
        gl = lax.bitcast_convert_type(w[:, half:], F32)

        def ffn(wgu_ref, wd_ref):
            gu = _dot(x, wgu_ref[...])
            f = gu.shape[1] // 2
            gte, up = gu[:, :f], gu[:, f:]
            act = gte * _sigmoid(gte) * up
            return _dot(act.astype(BF16), wd_ref[...])

        ys_ref[...] = gl[:, 0:1] * ffn(wgua_ref, wda_ref) + gl[:, 1:2] * ffn(wgub_ref, wdb_ref)

    @pl.when(j >= nr)
    def _():
        ys_ref[...] = jnp.zeros(ys_ref.shape, F32)

    @pl.when(j == nr)
    def _():
        _wait_row_gather(pay_ref, buf_ref, slot, sem)


def _experts(blk_ea, blk_eb, n_real, src, pay, wgu, wd, d):
    n_rows = src.shape[0]
    nblk = n_rows // EXPERT_BLOCK
    f2 = wgu.shape[2]
    src3 = src.reshape(nblk, 1, EXPERT_BLOCK)
    grid_spec = pltpu.PrefetchScalarGridSpec(
        num_scalar_prefetch=3,
        grid=(nblk,),
        in_specs=[
            pl.BlockSpec((None, 1, EXPERT_BLOCK), lambda j, ea, eb, nr: (j, 0, 0), memory_space=pltpu.SMEM),
            pl.BlockSpec((None, 1, EXPERT_BLOCK), lambda j, ea, eb, nr: (jnp.minimum(j + 1, nblk - 1), 0, 0),
                         memory_space=pltpu.SMEM),
            pl.BlockSpec(memory_space=pl.ANY),
            pl.BlockSpec((None, d, f2), lambda j, ea, eb, nr: (ea[j], 0, 0)),
            pl.BlockSpec((None, f2 // 2, d), lambda j, ea, eb, nr: (ea[j], 0, 0)),
            pl.BlockSpec((None, d, f2), lambda j, ea, eb, nr: (eb[j], 0, 0)),
            pl.BlockSpec((None, f2 // 2, d), lambda j, ea, eb, nr: (eb[j], 0, 0)),
        ],
        out_specs=pl.BlockSpec((EXPERT_BLOCK, d), lambda j, ea, eb, nr: (j, 0)),
        scratch_shapes=[pltpu.VMEM((2, EXPERT_BLOCK, PAY_W), U32), pltpu.SemaphoreType.DMA((2,))],
    )
    return pl.pallas_call(
        _expert_kernel,
        grid_spec=grid_spec,
        out_shape=jax.ShapeDtypeStruct((n_rows, d), F32),
        compiler_params=_params("arbitrary"),
        name="experts",
    )(blk_ea, blk_eb, n_real, src3, src3, pay, wgu, wd, wgu, wd)


_PAIR_A = (0, 0, 0, 1, 1, 2)
_PAIR_B = (1, 2, 3, 2, 3, 3)


def kernel(x, c, positions, ada_w, ada_b, norm_mix_w, norm_ffn_w, w_in, b_igate, b_fgate, q_norm_w, k_norm_w,
           sinks, conv_w, conv_b, mlstm_norm_w, w_attn_up, w_mlstm_up, w_out, router_w, router_bias,
           w_gate, w_up, w_down):
    batch, seq, d = x.shape
    depth = w_in.shape[0]
    t = batch * seq
    qw = N_HEADS * HEAD_DIM
    kvw = N_KV * HEAD_DIM
    mw = M_HEADS * M_DIM

    o = 0
    cols = {}
    for name, wdt in (("q", qw), ("k", kvw), ("v", kvw), ("mqk", 2 * mw), ("mv", mw), ("mi", M_HEADS),
                      ("mf", M_HEADS), ("mo", mw), ("ga", d), ("gb", d)):
        cols[name] = (o, o + wdt)
        o += wdt

    def wc(name, lo=0, hi=None):
        s, e = cols[name]
        return w_in[:, :, s + lo:(s + hi if hi is not None else e)]

    k0, k1 = wc("k", 0, HEAD_DIM), wc("k", HEAD_DIM, 2 * HEAD_DIM)
    v0, v1 = wc("v", 0, HEAD_DIM), wc("v", HEAD_DIM, 2 * HEAD_DIM)
    w_a = jnp.concatenate([wc("q"), k0, k0, k1, k1, v0, v0, v1, v1], axis=2).astype(BF16)
    w_m = jnp.concatenate([wc("mqk"), wc("mv"), wc("mo")], axis=2).astype(BF16)
    w_g = jnp.concatenate([wc("mi"), wc("mf"), jnp.zeros((depth, d, LANES - 2 * M_HEADS), F32)], axis=2).astype(BF16)
    w_mg = jnp.concatenate([wc("ga"), wc("gb")], axis=2).astype(BF16)
    w_au = w_attn_up.astype(BF16)
    w_mu = w_mlstm_up.astype(BF16)
    w_o = w_out.astype(BF16)
    n_e = w_gate.shape[1]
    w_gu = jnp.concatenate([w_gate, w_up], axis=3).astype(BF16).reshape(depth * n_e, d, -1)
    w_d = w_down.astype(BF16).reshape(depth * n_e, -1, d)

    rw_pad = jnp.concatenate([router_w, jnp.zeros((d, LANES - n_e), F32)], axis=1)
    rw_hi = rw_pad.astype(BF16)
    rw_lo = (rw_pad - rw_hi.astype(F32)).astype(BF16)
    rb = jnp.broadcast_to(router_bias.astype(F32)[:, None], (n_e, LANES))

    qn_w = jnp.tile(q_norm_w, (1, N_HEADS)).reshape(depth, 1, qw)
    kn_w = jnp.tile(k_norm_w, (1, 2 * N_KV)).reshape(depth, 1, 2 * kvw)
    seg = jnp.arange(qw) // HEAD_DIM
    bdq = jnp.where(seg[:, None] == seg[None, :], 1.0 / HEAD_DIM, 0.0).astype(BF16)
    bdk = bdq[:2 * kvw, :2 * kvw]

    half = ROPE_DIM // 2
    inv_freq = ROPE_THETA ** (-(jnp.arange(0, ROPE_DIM, 2, dtype=F32) / ROPE_DIM))
    ang = positions.astype(F32).reshape(t, 1) * inv_freq
    cos8, sin8 = jnp.cos(ang), jnp.sin(ang)
    pad1 = jnp.ones((t, HEAD_DIM - ROPE_DIM), F32)
    pad0 = jnp.zeros((t, HEAD_DIM - ROPE_DIM), F32)
    cos_t = jnp.tile(jnp.concatenate([cos8, cos8, pad1], axis=1), (1, LANES // HEAD_DIM))
    sin_t = jnp.tile(jnp.concatenate([-sin8, sin8, pad0], axis=1), (1, LANES // HEAD_DIM))

    gate_bias = jnp.concatenate([b_igate, b_fgate], axis=1).astype(F32)
    bcol = jnp.broadcast_to(gate_bias[:, :, None], (depth, 2 * M_HEADS, LANES))

    tm_merge = min(512, seq)
    ii = jnp.arange(tm_merge)
    tri = (ii[:, None] <= ii[None, :]).astype(BF16)

    n_blk = (t + N_BUCKETS * (EXPERT_BLOCK - 1)) // EXPERT_BLOCK + 1
    n_rows = n_blk * EXPERT_BLOCK
    pair_a = jnp.asarray(_PAIR_A, I32)
    pair_b = jnp.asarray(_PAIR_B, I32)

    c_pad = jnp.zeros((SUBLANES, d), F32).at[:batch].set(c)
    mod = _ada_mod(c_pad, ada_w, ada_b)[:, :batch]

    xf = x.reshape(t, d)
    for l in range(depth):
        sh1, sc1, g1, sh2, sc2, g2 = [m.reshape(batch, 1, d) for m in jnp.split(mod[l], 6, axis=-1)]

        a_in, m_in, mg, grow = _inproj(xf, sc1, sh1, norm_mix_w.reshape(depth, 1, d), w_a, w_m, w_g, w_mg, l, seq)
        o_attn = _attention(a_in, cos_t, sin_t, sinks[l], qn_w[l], kn_w[l], bdq, bdk, batch, seq)
        hm = _mlstm(m_in, grow, conv_w[l], conv_b[l].reshape(1, -1), bcol[l], mlstm_norm_w[l].reshape(1, mw),
                    batch, seq)
        x1, pay, route, cnt = _merge(o_attn, hm, mg, xf, g1, sc2, sh2, norm_ffn_w.reshape(depth, 1, d),
                                     w_au, w_mu, w_o, rw_hi, rw_lo, rb, tri, l, seq)

        counts = cnt[:N_BUCKETS, 0].astype(I32)
        padded = (counts + EXPERT_BLOCK - 1) // EXPERT_BLOCK * EXPERT_BLOCK
        pad_ends = jnp.cumsum(padded)
        pad_starts = pad_ends - padded
        bucket = route[0].astype(I32)
        dest = pad_starts[bucket] + route[3].astype(I32)
        blk_start = jnp.arange(n_blk, dtype=I32) * EXPERT_BLOCK
        blk_bucket = jnp.minimum(jnp.sum((pad_ends[None, :] <= blk_start[:, None]).astype(I32), axis=1), N_BUCKETS - 1)
        grp = blk_bucket // N_PAIRS
        blk_ea = (l * n_e + grp * EPG + pair_a[blk_bucket % N_PAIRS]).astype(I32)
        blk_eb = (l * n_e + grp * EPG + pair_b[blk_bucket % N_PAIRS]).astype(I32)
        n_real = (pad_ends[-1:] // EXPERT_BLOCK).astype(I32)

        src = _invperm(dest, n_rows)
        ys = _experts(blk_ea, blk_eb, n_real, src, pay, w_gu, w_d, d)
        xf = _combine(dest, ys, x1, g2, seq)
    return xf.reshape(batch, seq, d)
```

```python
import jax
import jax.numpy as jnp
from jax import lax
from jax.experimental import pallas as pl
from jax.experimental.pallas import tpu as pltpu
from jax.experimental.pallas import tpu_sc as plsc

F32 = jnp.float32
BF16 = jnp.bfloat16
U32 = jnp.uint32
I32 = jnp.int32
HIGHEST = lax.Precision.HIGHEST

HEAD_DIM = 64
N_HEADS = 8
N_KV = 2
ROPE_DIM = 16
ROPE_THETA = 500000.0
ATTN_BLOCK = 128
M_HEADS = 4
M_DIM = 128
CONV_K = 4
N_EXPERTS = 16
N_GROUPS = 4
EPG = 4
EPS = 1e-6

LANES = 128
SUBLANES = 8

CHUNK = 128
CHUNKS_PER_STEP = 4
GATE_CHUNKS_PER_STEP = 4
N_PAIRS = 6
N_BUCKETS = N_GROUPS * N_PAIRS
BUCKET_ROWS = 32
EXPERT_BLOCK = 256
PAY_PARTS = 5
OUT_PARTS = 8
SC_WINDOW = 128
VMEM_LIMIT = 56 * 1024 * 1024


def _dot(a, b, precision=None):
    return jnp.dot(a, b, preferred_element_type=F32, precision=precision)


def _dot_nt(a, b):
    return lax.dot_general(a, b, (((1,), (1,)), ((), ())), preferred_element_type=F32)


def _sigmoid(x):
    return 1.0 / (1.0 + jnp.exp(-x))


def _log_sigmoid(x):
    return jnp.minimum(x, 0.0) - jnp.log1p(jnp.exp(-jnp.abs(x)))


def _params(*sem):
    return pltpu.CompilerParams(dimension_semantics=sem, vmem_limit_bytes=VMEM_LIMIT)


def _ada_kernel(c_ref, w_ref, b_ref, o_ref):
    c = c_ref[...]
    ca = c * _sigmoid(c)
    o_ref[0] = _dot(ca, w_ref[0], HIGHEST) + b_ref[0]


def _ada_mod(c_pad, ada_w, ada_b):
    depth, d, n = ada_w.shape
    tn = 1536
    return pl.pallas_call(
        _ada_kernel,
        grid=(depth, n // tn),
        in_specs=[
            pl.BlockSpec((SUBLANES, d), lambda l, j: (0, 0)),
            pl.BlockSpec((1, d, tn), lambda l, j: (l, 0, j)),
            pl.BlockSpec((1, 1, tn), lambda l, j: (l, 0, j)),
        ],
        out_specs=pl.BlockSpec((1, SUBLANES, tn), lambda l, j: (l, 0, j)),
        out_shape=jax.ShapeDtypeStruct((depth, SUBLANES, n), F32),
        compiler_params=_params("arbitrary", "arbitrary"),
        name="ada_mod",
    )(c_pad, ada_w, ada_b.reshape(depth, 1, n))


def _inproj_kernel(x_ref, sc_ref, sh_ref, nw_ref, wa_ref, wm_ref, wg_ref, wmg_ref,
                   a_ref, m_ref, mg_ref, gr_ref, g_ref):
    x = x_ref[...]
    ms = jnp.mean(x * x, axis=-1, keepdims=True)
    h = x * lax.rsqrt(ms + EPS) * nw_ref[...]
    h = h * (1.0 + sc_ref[...]) + sh_ref[...]
    hb = h.astype(BF16)
    a_ref[...] = _dot(hb, wa_ref[...]).astype(BF16)
    m_ref[...] = _dot(hb, wm_ref[...]).astype(BF16)
    mg_ref[...] = _sigmoid(_dot(hb, wmg_ref[...])).astype(BF16)
    g_ref[...] = _dot(hb, wg_ref[...])
    gr_ref[...] = g_ref[...].T[:SUBLANES, :]


def _inproj(x, sc, sh, nw, wa, wm, wg, wmg, l, seq):
    t, d = x.shape
    tm = min(512, seq)
    tpb = seq // tm
    row = lambda i: (i, 0)
    bsel = lambda i: (i // tpb, 0, 0)
    wsel = lambda i: (l, 0, 0)
    na, nm, ng, nmg = wa.shape[2], wm.shape[2], wg.shape[2], wmg.shape[2]
    return pl.pallas_call(
        _inproj_kernel,
        grid=(t // tm,),
        in_specs=[
            pl.BlockSpec((tm, d), row),
            pl.BlockSpec((None, 1, d), bsel),
            pl.BlockSpec((None, 1, d), bsel),
            pl.BlockSpec((None, 1, d), wsel),
            pl.BlockSpec((None, d, na), wsel),
            pl.BlockSpec((None, d, nm), wsel),
            pl.BlockSpec((None, d, ng), wsel),
            pl.BlockSpec((None, d, nmg), wsel),
        ],
        out_specs=[
            pl.BlockSpec((tm, na), row),
            pl.BlockSpec((tm, nm), row),
            pl.BlockSpec((tm, nmg), row),
            pl.BlockSpec((SUBLANES, tm), lambda i: (0, i)),
        ],
        out_shape=[
            jax.ShapeDtypeStruct((t, na), BF16),
            jax.ShapeDtypeStruct((t, nm), BF16),
            jax.ShapeDtypeStruct((t, nmg), BF16),
            jax.ShapeDtypeStruct((SUBLANES, t), F32),
        ],
        scratch_shapes=[pltpu.VMEM((tm, ng), F32)],
        compiler_params=_params("arbitrary"),
        name="inproj",
    )(x, sc, sh, nw, wa, wm, wg, wmg)


def _rope(t, cos, sin):
    w = t.shape[1]
    reps = w // LANES
    cosw = jnp.concatenate([cos] * reps, axis=1) if reps > 1 else cos
    sinw = jnp.concatenate([sin] * reps, axis=1) if reps > 1 else sin
    lane = lax.broadcasted_iota(I32, t.shape, 1)
    half = ROPE_DIM // 2
    up = pltpu.roll(t, w - half, axis=1)
    dn = pltpu.roll(t, half, axis=1)
    partner = jnp.where((lane % ROPE_DIM) < half, up, dn)
    return t * cosw + partner * sinw


def _head_norm(t, bd, w):
    ms = _dot((t * t).astype(BF16), bd)
    return t * lax.rsqrt(ms + EPS) * w


def _attn_kernel(sink_ref, cur_ref, prev_ref, cos_ref, sin_ref, cosp_ref, sinp_ref,
                 qw_ref, kw_ref, bdq_ref, bdk_ref, o_ref):
    tq = cur_ref.shape[0]
    nj = tq // ATTN_BLOCK
    qw = N_HEADS * HEAD_DIM
    kw = 2 * N_KV * HEAD_DIM
    blk0 = pl.program_id(1) * nj

    cur = cur_ref[...]
    q = cur[:, :qw].astype(F32)
    kc = cur[:, qw:qw + kw].astype(F32)
    vc = cur[:, qw + kw:]
    prev = prev_ref[...]
    kp = prev[:, :kw].astype(F32)
    vp = prev[:, kw:]

    cos, sin = cos_ref[...], sin_ref[...]
    q = _rope(_head_norm(q, bdq_ref[...], qw_ref[...]), cos, sin) * (HEAD_DIM ** -0.5)
    kc = _rope(_head_norm(kc, bdk_ref[...], kw_ref[...]), cos, sin)
    kp = _rope(_head_norm(kp, bdk_ref[...], kw_ref[...]), cosp_ref[...], sinp_ref[...])
    qb = q.astype(BF16)
    k_all = jnp.concatenate([kp, kc], axis=0).astype(BF16)
    v_all = jnp.concatenate([vp, vc], axis=0)

    lane = lax.broadcasted_iota(I32, (ATTN_BLOCK, LANES), 1)
    lo = lane < HEAD_DIM
    zero = jnp.zeros((ATTN_BLOCK, LANES), BF16)
    g_heads = N_HEADS // N_KV
    rr = lax.broadcasted_iota(I32, (g_heads * ATTN_BLOCK, 2 * ATTN_BLOCK), 0)
    ri = rr % ATTN_BLOCK
    ci = lax.broadcasted_iota(I32, (g_heads * ATTN_BLOCK, 2 * ATTN_BLOCK), 1)
    dist = ri + ATTN_BLOCK - ci
    head_row = lax.broadcasted_iota(I32, (g_heads * ATTN_BLOCK, 1), 0) // ATTN_BLOCK

    for j in range(nj):
        rows = slice(j * ATTN_BLOCK, (j + 1) * ATTN_BLOCK)
        has_prev = jnp.minimum(blk0 + j, 1)
        mask = (dist >= 0) & (dist < jnp.minimum(ATTN_BLOCK, ri + 1 + has_prev * ATTN_BLOCK))
        for g in range(N_KV):
            kb = k_all[j * ATTN_BLOCK:(j + 2) * ATTN_BLOCK, g * LANES:(g + 1) * LANES]
            vb = v_all[j * ATTN_BLOCK:(j + 2) * ATTN_BLOCK, g * LANES:(g + 1) * LANES]
            qp0 = qb[rows, (2 * g) * LANES:(2 * g + 1) * LANES]
            qp1 = qb[rows, (2 * g + 1) * LANES:(2 * g + 2) * LANES]
            q4 = jnp.concatenate([jnp.where(lo, qp0, zero), jnp.where(lo, zero, qp0),
                                  jnp.where(lo, qp1, zero), jnp.where(lo, zero, qp1)], axis=0)
            s = _dot_nt(q4, kb)
            s = jnp.where(mask, s, -jnp.inf)
            sink = jnp.full((g_heads * ATTN_BLOCK, 1), sink_ref[g_heads * g], F32)
            for r in range(1, g_heads):
                sink = jnp.where(head_row == r, sink_ref[g_heads * g + r], sink)
            m = jnp.maximum(jnp.max(s, axis=-1, keepdims=True), sink)
            p = jnp.exp(s - m)
            den = jnp.sum(p, axis=-1, keepdims=True) + jnp.exp(sink - m)
            o4 = _dot(p.astype(BF16), vb) / den
            b = ATTN_BLOCK
            o_ref[rows, (2 * g) * LANES:(2 * g + 1) * LANES] = jnp.where(lo, o4[0:b], o4[b:2 * b]).astype(BF16)
            o_ref[rows, (2 * g + 1) * LANES:(2 * g + 2) * LANES] = jnp.where(
                lo, o4[2 * b:3 * b], o4[3 * b:4 * b]).astype(BF16)


def _attention(a_in, cos_t, sin_t, sinks_l, qw, kw, bdq, bdk, batch, seq):
    t = a_in.shape[0]
    tq = min(512, seq)
    nj = tq // ATTN_BLOCK
    tpb = seq // tq
    bpb = seq // ATTN_BLOCK
    qwid = N_HEADS * HEAD_DIM
    kvw = 4 * N_KV * HEAD_DIM
    cur = lambda b, i: (b * tpb + i, 0)
    prv = lambda b, i: (b * bpb + jnp.maximum(i * nj - 1, 0), 1)
    prv0 = lambda b, i: (b * bpb + jnp.maximum(i * nj - 1, 0), 0)
    const = lambda b, i: (0, 0)
    return pl.pallas_call(
        _attn_kernel,
        grid=(batch, tpb),
        in_specs=[
            pl.BlockSpec(memory_space=pltpu.SMEM),
            pl.BlockSpec((tq, qwid + kvw), cur),
            pl.BlockSpec((ATTN_BLOCK, kvw), prv),
            pl.BlockSpec((tq, LANES), cur),
            pl.BlockSpec((tq, LANES), cur),
            pl.BlockSpec((ATTN_BLOCK, LANES), prv0),
            pl.BlockSpec((ATTN_BLOCK, LANES), prv0),
            pl.BlockSpec((1, qwid), const),
            pl.BlockSpec((1, kvw // 2), const),
            pl.BlockSpec((qwid, qwid), const),
            pl.BlockSpec((kvw // 2, kvw // 2), const),
        ],
        out_specs=pl.BlockSpec((tq, qwid), cur),
        out_shape=jax.ShapeDtypeStruct((t, qwid), BF16),
        compiler_params=_params("arbitrary", "arbitrary"),
        name="swa_attention",
    )(sinks_l, a_in, a_in, cos_t, sin_t, cos_t, sin_t, qw, kw, bdq, bdk)


def _mlstm_kernel(min_ref, gr_ref, cw_ref, cb_ref, bcol_ref, nw_ref,
                  hm_ref, ext_ref, q_ref, kt_ref, st_ref, mx_ref, ab_ref, bc_ref):
    tt = min_ref.shape[0]
    mw = M_HEADS * M_DIM
    nchunks = tt // CHUNK

    @pl.when(pl.program_id(1) == 0)
    def _():
        ext_ref[0:SUBLANES, :] = jnp.zeros((SUBLANES, 2 * mw), F32)
        st_ref[...] = jnp.zeros(st_ref.shape, F32)
        mx_ref[...] = jnp.zeros(mx_ref.shape, F32)

    u = min_ref[:, :2 * mw].astype(F32)
    ext_ref[SUBLANES:SUBLANES + tt, :] = u
    acc = cb_ref[...] + cw_ref[CONV_K - 1:CONV_K, :] * u
    for jj in range(CONV_K - 1):
        off = SUBLANES - (CONV_K - 1) + jj
        acc = acc + cw_ref[jj:jj + 1, :] * ext_ref[off:off + tt, :]
    act = acc * _sigmoid(acc)
    ext_ref[0:SUBLANES, :] = u[tt - SUBLANES:tt, :]
    q_ref[...] = act[:, :mw].astype(BF16)
    for j in range(nchunks):
        for h in range(M_HEADS):
            kt_ref[h * M_DIM:(h + 1) * M_DIM, j * CHUNK:(j + 1) * CHUNK] = (
                act[j * CHUNK:(j + 1) * CHUNK, mw + h * M_DIM:mw + (h + 1) * M_DIM] * (M_DIM ** -0.5)).T

    ri = lax.broadcasted_iota(I32, (CHUNK, CHUNK), 0)
    ci = lax.broadcasted_iota(I32, (CHUNK, CHUNK), 1)
    causal = ci <= ri
    triu = jnp.where(ri <= ci, 1.0, 0.0).astype(BF16)
    ones_half = jnp.ones((CHUNK, M_DIM), BF16)
    mean_mat = jnp.full((M_DIM, M_DIM), 1.0 / M_DIM, BF16)
    sub = lax.broadcasted_iota(I32, (SUBLANES, CHUNK), 0)
    heads = range(M_HEADS)

    pad_rows = jnp.zeros((CHUNK - SUBLANES, CHUNK), F32)
    zero_rows = jnp.zeros((SUBLANES, CHUNK), F32)

    def gate_body(jg, carry):
        for u_ in range(GATE_CHUNKS_PER_STEP):
            rs = pl.ds(pl.multiple_of((jg * GATE_CHUNKS_PER_STEP + u_) * CHUNK, CHUNK), CHUNK)
            gr = gr_ref[:, rs] + bcol_ref[...]
            ls = _log_sigmoid(gr)
            ls1 = ls.astype(BF16).astype(F32)
            ls2 = (ls - ls1).astype(BF16).astype(F32)
            pieces = jnp.concatenate([ls1, ls2, ls - ls1 - ls2, zero_rows], axis=0).astype(BF16)
            sums = _dot(pieces, triu)
            br = sums[0:SUBLANES] + sums[SUBLANES:2 * SUBLANES] + sums[2 * SUBLANES:3 * SUBLANES]
            ab = jnp.where(sub < M_HEADS, gr - pltpu.roll(br, M_HEADS, axis=0), br)
            ab_ref[:, rs] = ab
            bc_ref[rs, :] = jnp.concatenate([ab, pad_rows], axis=0).T
        return carry

    lax.fori_loop(0, nchunks // GATE_CHUNKS_PER_STEP, gate_body, 0)

    def group_body(cg, carry):
        rows, ab = [], []
        for u_ in range(CHUNKS_PER_STEP):
            r0 = pl.multiple_of((cg * CHUNKS_PER_STEP + u_) * CHUNK, CHUNK)
            rows.append(pl.ds(r0, CHUNK))
            ab.append(ab_ref[:, rows[u_]])
        lanes = [(u_, h) for u_ in range(CHUNKS_PER_STEP) for h in heads]
        a_r = {(u_, h): ab[u_][h:h + 1, :] for u_, h in lanes}
        b_last = {(u_, h): ab[u_][M_HEADS + h:M_HEADS + h + 1, CHUNK - 1:CHUNK] for u_, h in lanes}

        m_prev, a_max, a_dec, s_in = {}, {}, {}, {}
        m_run = [mx_ref[h][0:1, 0:1] for h in heads]
        for k in lanes:
            u_, h = k
            m_prev[k] = m_run[h]
            a_max[k] = jnp.max(a_r[k], axis=-1, keepdims=True)
            m_loc = b_last[k] + a_max[k]
            m_new = jnp.maximum(b_last[k] + m_prev[k], m_loc)
            a_dec[k] = jnp.exp(b_last[k] + m_prev[k] - m_new)
            s_in[k] = jnp.exp(m_loc - m_new)
            m_run[h] = m_new
        for h in heads:
            mx_ref[h] = jnp.broadcast_to(m_run[h], (SUBLANES, LANES))

        q, v_ext, s_qk, kv = {}, {}, {}, {}
        for k in lanes:
            u_, h = k
            rs = rows[u_]
            q[k] = q_ref[rs, h * M_DIM:(h + 1) * M_DIM]
            kt = kt_ref[h * M_DIM:(h + 1) * M_DIM, rs]
            v = min_ref[rs, 2 * mw + h * M_DIM:2 * mw + (h + 1) * M_DIM]
            v_ext[k] = jnp.concatenate([v, ones_half], axis=1)
            s_qk[k] = _dot(q[k], kt.astype(BF16))
            e_r = jnp.exp(a_r[k] - a_max[k])
            kv[k] = _dot((kt * e_r).astype(BF16), v_ext[k])

        thr, qk, inter = {}, {}, {}
        for k in lanes:
            u_, h = k
            a_mat = jnp.where(causal, a_r[k], -jnp.inf)
            mu = jnp.maximum(jnp.max(a_mat, axis=-1, keepdims=True), m_prev[k])
            b_c = bc_ref[rows[u_], M_HEADS + h:M_HEADS + h + 1]
            thr[k] = jnp.broadcast_to(jnp.exp(-(b_c + mu)), (CHUNK, M_DIM))
            mu_b = jnp.broadcast_to(mu, (CHUNK, CHUNK))
            inter[k] = jnp.exp(m_prev[k] - mu_b)
            qk[k] = (s_qk[k] * jnp.exp(a_mat - mu_b)).astype(BF16)

        q_state = {}
        state = [st_ref[h] for h in heads]
        for k in lanes:
            u_, h = k
            q_state[k] = _dot(q[k], state[h].astype(BF16))
            state[h] = a_dec[k] * state[h] + s_in[k] * kv[k]
        for h in heads:
            st_ref[h] = state[h]

        for k in lanes:
            u_, h = k
            hs = slice(h * M_DIM, (h + 1) * M_DIM)
            num = _dot(qk[k], v_ext[k])
            den = jnp.maximum(jnp.abs(num[:, M_DIM:] + inter[k] * q_state[k][:, M_DIM:]), thr[k])
            hh = (num[:, :M_DIM] + inter[k] * q_state[k][:, :M_DIM]) / den
            msq = _dot((hh * hh).astype(BF16), mean_mat)
            hn = hh * lax.rsqrt(msq + EPS) * nw_ref[:, hs]
            og = min_ref[rows[u_], 3 * mw + h * M_DIM:3 * mw + (h + 1) * M_DIM].astype(F32)
            hm_ref[rows[u_], hs] = (_sigmoid(og) * hn).astype(BF16)
        return carry

    lax.fori_loop(0, nchunks // CHUNKS_PER_STEP, group_body, 0)


def _mlstm(m_in, grow, conv_w, conv_b, bcol, nw, batch, seq):
    t = m_in.shape[0]
    tt = min(1024, seq)
    tpb = seq // tt
    mw = M_HEADS * M_DIM
    cur = lambda b, i: (b * tpb + i, 0)
    const = lambda b, i: (0, 0)
    return pl.pallas_call(
        _mlstm_kernel,
        grid=(batch, tpb),
        in_specs=[
            pl.BlockSpec((tt, 4 * mw), cur),
            pl.BlockSpec((SUBLANES, tt), lambda b, i: (0, b * tpb + i)),
            pl.BlockSpec((CONV_K, 2 * mw), const),
            pl.BlockSpec((1, 2 * mw), const),
            pl.BlockSpec((SUBLANES, LANES), const),
            pl.BlockSpec((1, mw), const),
        ],
        out_specs=pl.BlockSpec((tt, mw), cur),
        out_shape=jax.ShapeDtypeStruct((t, mw), BF16),
        scratch_shapes=[
            pltpu.VMEM((tt + SUBLANES, 2 * mw), F32),
            pltpu.VMEM((tt, mw), BF16),
            pltpu.VMEM((mw, tt), F32),
            pltpu.VMEM((M_HEADS, M_DIM, 2 * M_DIM), F32),
            pltpu.VMEM((M_HEADS, SUBLANES, LANES), F32),
            pltpu.VMEM((SUBLANES, tt), F32),
            pltpu.VMEM((tt, LANES), F32),
        ],
        compiler_params=_params("arbitrary", "arbitrary"),
        name="mlstm",
    )(m_in, grow, conv_w, conv_b, bcol, nw)


def _merge_kernel(o_ref, hm_ref, mg_ref, x_ref, g1_ref, sc_ref, sh_ref, nw_ref,
                  wa_ref, wm_ref, wo_ref, rwh_ref, rwl_ref, rb_ref, tri_ref,
                  x1_ref, pay_ref, route_ref, cnt_ref, carry_ref):
    tm, d = x_ref.shape

    @pl.when(pl.program_id(0) == 0)
    def _():
        carry_ref[...] = jnp.zeros(carry_ref.shape, F32)

    ya = _dot(o_ref[...], wa_ref[...])
    yb = _dot(hm_ref[...], wm_ref[...])
    mg = mg_ref[...]
    merged = mg[:, :d].astype(F32) * ya + mg[:, d:].astype(F32) * yb
    x1 = x_ref[...] + g1_ref[...] * _dot(merged.astype(BF16), wo_ref[...])
    x1_ref[...] = x1

    ms = jnp.mean(x1 * x1, axis=-1, keepdims=True)
    h2 = x1 * lax.rsqrt(ms + EPS) * nw_ref[...]
    h2 = h2 * (1.0 + sc_ref[...]) + sh_ref[...]
    hi = h2.astype(BF16)
    hif = hi.astype(F32)
    lo = (h2 - hif).astype(BF16)
    rwh = rwh_ref[...]
    logits = _dot(hi, rwh) + _dot(lo, rwh) + _dot(hi, rwl_ref[...])
    sc_t = _sigmoid(logits).T[:N_EXPERTS, :]
    sel_t = sc_t + rb_ref[:, 0:1]

    def row(a, e):
        return a[e:e + 1, :]

    best = None
    gi = jnp.zeros((1, tm), I32)
    for g in range(N_GROUPS):
        r = [row(sel_t, EPG * g + i) for i in range(EPG)]
        gs = None
        for i in range(EPG):
            for j in range(i + 1, EPG):
                pr = r[i] + r[j]
                gs = pr if gs is None else jnp.maximum(gs, pr)
        if best is None:
            best = gs
        else:
            upd = gs > best
            gi = jnp.where(upd, g, gi)
            best = jnp.maximum(best, gs)

    def pick(a, i):
        out = row(a, i)
        for g in range(1, N_GROUPS):
            out = jnp.where(gi == g, row(a, EPG * g + i), out)
        return out

    v = [pick(sel_t, i) for i in range(EPG)]
    s = [pick(sc_t, i) for i in range(EPG)]

    def argmax4(vals):
        bv, bi = vals[0], jnp.zeros((1, tm), I32)
        for i in range(1, EPG):
            upd = vals[i] > bv
            bi = jnp.where(upd, i, bi)
            bv = jnp.maximum(bv, vals[i])
        return bi

    i1 = argmax4(v)
    i2 = argmax4([jnp.where(i1 == i, -jnp.inf, v[i]) for i in range(EPG)])
    ia = jnp.minimum(i1, i2)
    ib = jnp.maximum(i1, i2)
    pidx = jnp.where(ia == 0, ib - 1, jnp.where(ia == 1, ib + 1, N_PAIRS - 1))
    bucket = gi * N_PAIRS + pidx

    def by_index(vals, idx):
        out = vals[0]
        for i in range(1, EPG):
            out = jnp.where(idx == i, vals[i], out)
        return out

    s_a, s_b = by_index(s, ia), by_index(s, ib)
    gate_a = s_a / (s_a + s_b)
    gate_b = s_b / (s_a + s_b)

    brow = lax.broadcasted_iota(I32, (BUCKET_ROWS, tm), 0)
    onehot = brow == bucket
    cums = _dot(jnp.where(onehot, 1.0, 0.0).astype(BF16), tri_ref[...])
    carry = carry_ref[...]
    rank = jnp.sum(jnp.where(onehot, carry[:, 0:1] + cums, 0.0), axis=0, keepdims=True) - 1.0
    new_carry = carry + cums[:, tm - 1:tm]
    carry_ref[...] = new_carry
    cnt_ref[...] = new_carry

    route_ref[...] = jnp.concatenate(
        [bucket.astype(F32), gate_a, gate_b, rank, jnp.zeros((SUBLANES - 4, tm), F32)], axis=0)

    bits = lax.bitcast_convert_type(hif, U32)
    half = d // 2
    packed = bits[:, :half] | (bits[:, half:] >> 16)
    for cpart in range(half // LANES):
        pay_ref[cpart] = packed[:, cpart * LANES:(cpart + 1) * LANES]
    gates_t = jnp.concatenate([gate_a, gate_b, jnp.zeros((LANES - 2, tm), F32)], axis=0)
    pay_ref[half // LANES] = lax.bitcast_convert_type(gates_t.T, U32)


def _merge(o_attn, hm, mg, x, g1, sc2, sh2, nw, wa, wm, wo, rwh, rwl, rb, tri, l, seq):
    t, d = x.shape
    tm = tri.shape[0]
    tpb = seq // tm
    row = lambda i: (i, 0)
    bsel = lambda i: (i // tpb, 0, 0)
    wsel = lambda i: (l, 0, 0)
    const = lambda i: (0, 0)
    hw = o_attn.shape[1]
    return pl.pallas_call(
        _merge_kernel,
        grid=(t // tm,),
        in_specs=[
            pl.BlockSpec((tm, hw), row),
            pl.BlockSpec((tm, hw), row),
            pl.BlockSpec((tm, 2 * d), row),
            pl.BlockSpec((tm, d), row),
            pl.BlockSpec((None, 1, d), bsel),
            pl.BlockSpec((None, 1, d), bsel),
            pl.BlockSpec((None, 1, d), bsel),
            pl.BlockSpec((None, 1, d), wsel),
            pl.BlockSpec((None, hw, d), wsel),
            pl.BlockSpec((None, hw, d), wsel),
            pl.BlockSpec((None, d, d), wsel),
            pl.BlockSpec((d, LANES), const),
            pl.BlockSpec((d, LANES), const),
            pl.BlockSpec((N_EXPERTS, LANES), const),
            pl.BlockSpec((tm, tm), const),
        ],
        out_specs=[
            pl.BlockSpec((tm, d), row),
            pl.BlockSpec((PAY_PARTS, tm, LANES), lambda i: (0, i, 0)),
            pl.BlockSpec((SUBLANES, tm), lambda i: (0, i)),
            pl.BlockSpec((BUCKET_ROWS, LANES), const),
        ],
        out_shape=[
            jax.ShapeDtypeStruct((t, d), F32),
            jax.ShapeDtypeStruct((PAY_PARTS, t, LANES), U32),
            jax.ShapeDtypeStruct((SUBLANES, t), F32),
            jax.ShapeDtypeStruct((BUCKET_ROWS, LANES), F32),
        ],
        scratch_shapes=[pltpu.VMEM((BUCKET_ROWS, LANES), F32)],
        compiler_params=_params("arbitrary"),
        name="merge_router",
    )(o_attn, hm, mg, x, g1, sc2, sh2, nw, wa, wm, wo, rwh, rwl, rb, tri)


def _sc_mesh():
    return plsc.VectorSubcoreMesh(core_axis_name="core", subcore_axis_name="subcore")


def _sc_scatter_rows(rows, dest, n_out):
    n, w = rows.shape

    @pl.kernel(out_type=jax.ShapeDtypeStruct((n_out, w), rows.dtype), mesh=_sc_mesh(), scratch_types=[])
    def scatter(x_hbm, i_hbm, o_hbm):
        def body(x_vmem, i_vmem):
            pltpu.sync_copy(x_vmem, o_hbm.at[i_vmem.at[0]])

        pltpu.emit_pipeline(
            body,
            grid=(n // SC_WINDOW,),
            in_specs=[pl.BlockSpec((SC_WINDOW, w), lambda i: (i, 0)),
                      pl.BlockSpec((1, SC_WINDOW), lambda i: (0, i))],
            out_specs=[],
            core_axis_name=("core", "subcore"),
            dimension_semantics=(pltpu.PARALLEL,),
        )(x_hbm, i_hbm)

    return scatter(rows, dest.reshape(1, n))


def _sc_gather_rows(src, idx):
    n = idx.shape[0]
    w = src.shape[1]

    @pl.kernel(out_type=jax.ShapeDtypeStruct((n, w), src.dtype), mesh=_sc_mesh(), scratch_types=[])
    def gather(x_hbm, i_hbm, o_hbm):
        def body(i_vmem, o_vmem):
            pltpu.sync_copy(x_hbm.at[i_vmem.at[0]], o_vmem)

        pltpu.emit_pipeline(
            body,
            grid=(n // SC_WINDOW,),
            in_specs=[pl.BlockSpec((1, SC_WINDOW), lambda i: (0, i))],
            out_specs=[pl.BlockSpec((SC_WINDOW, w), lambda i: (i, 0))],
            core_axis_name=("core", "subcore"),
            dimension_semantics=(pltpu.PARALLEL,),
        )(i_hbm, o_hbm)

    return gather(src, idx.reshape(1, n))


def _part_index(dest, parts, n_rows):
    return (jnp.arange(parts, dtype=I32)[:, None] * n_rows + dest[None, :]).reshape(-1)


def _residual_kernel(x_ref, y_ref, g2_ref, o_ref):
    y = jnp.concatenate([y_ref[c] for c in range(OUT_PARTS)], axis=1)
    o_ref[...] = x_ref[...] + g2_ref[...] * y


def _residual(x1, ytok, g2, seq):
    t, d = x1.shape
    tm = min(512, seq)
    tpb = seq // tm
    return pl.pallas_call(
        _residual_kernel,
        grid=(t // tm,),
        in_specs=[
            pl.BlockSpec((tm, d), lambda i: (i, 0)),
            pl.BlockSpec((OUT_PARTS, tm, LANES), lambda i: (0, i, 0)),
            pl.BlockSpec((None, 1, d), lambda i: (i // tpb, 0, 0)),
        ],
        out_specs=pl.BlockSpec((tm, d), lambda i: (i, 0)),
        out_shape=jax.ShapeDtypeStruct((t, d), F32),
        compiler_params=_params("arbitrary"),
        name="residual",
    )(x1, ytok, g2)


def _expert_kernel(ea_ref, eb_ref, nr_ref, xs_ref, wgua_ref, wda_ref, wgub_ref, wdb_ref, ys_ref):
    j = pl.program_id(0)
    nr = nr_ref[0]

    @pl.when(j < nr)
    def _():
        pk = jnp.concatenate([xs_ref[c] for c in range(PAY_PARTS - 1)], axis=1)
        xa = lax.bitcast_convert_type(pk & jnp.uint32(0xFFFF0000), F32)
        xb = lax.bitcast_convert_type(pk << 16, F32)
        x = jnp.concatenate([xa, xb], axis=1).astype(BF16)
        gl = lax.bitcast_convert_type(xs_ref[PAY_PARTS - 1], F32)

        def ffn(wgu_ref, wd_ref):
            gu = _dot(x, wgu_ref[...])
            f = gu.shape[1] // 2
            gte, up = gu[:, :f], gu[:, f:]
            act = gte * _sigmoid(gte) * up
            return _dot(act.astype(BF16), wd_ref[...])

        y = gl[:, 0:1] * ffn(wgua_ref, wda_ref) + gl[:, 1:2] * ffn(wgub_ref, wdb_ref)
        for c in range(OUT_PARTS):
            ys_ref[c] = y[:, c * LANES:(c + 1) * LANES]

    @pl.when(j >= nr)
    def _():
        ys_ref[...] = jnp.zeros(ys_ref.shape, F32)


def _experts(blk_ea, blk_eb, n_real, xs, wgu, wd, d):
    n_rows = xs.shape[1]
    nblk = n_rows // EXPERT_BLOCK
    f2 = wgu.shape[2]
    grid_spec = pltpu.PrefetchScalarGridSpec(
        num_scalar_prefetch=3,
        grid=(nblk,),
        in_specs=[
            pl.BlockSpec((PAY_PARTS, EXPERT_BLOCK, LANES), lambda j, ea, eb, nr: (0, j, 0)),
            pl.BlockSpec((None, d, f2), lambda j, ea, eb, nr: (ea[j], 0, 0)),
            pl.BlockSpec((None, f2 // 2, d), lambda j, ea, eb, nr: (ea[j], 0, 0)),
            pl.BlockSpec((None, d, f2), lambda j, ea, eb, nr: (eb[j], 0, 0)),
            pl.BlockSpec((None, f2 // 2, d), lambda j, ea, eb, nr: (eb[j], 0, 0)),
        ],
        out_specs=pl.BlockSpec((OUT_PARTS, EXPERT_BLOCK, LANES), lambda j, ea, eb, nr: (0, j, 0)),
    )
    return pl.pallas_call(
        _expert_kernel,
        grid_spec=grid_spec,
        out_shape=jax.ShapeDtypeStruct((OUT_PARTS, n_rows, LANES), F32),
        compiler_params=_params("arbitrary"),
        name="experts",
    )(blk_ea, blk_eb, n_real, xs, wgu, wd, wgu, wd)


_PAIR_A = (0, 0, 0, 1, 1, 2)
_PAIR_B = (1, 2, 3, 2, 3, 3)


def kernel(x, c, positions, ada_w, ada_b, norm_mix_w, norm_ffn_w, w_in, b_igate, b_fgate, q_norm_w, k_norm_w,
           sinks, conv_w, conv_b, mlstm_norm_w, w_attn_up, w_mlstm_up, w_out, router_w, router_bias,
           w_gate, w_up, w_down):
    batch, seq, d = x.shape
    depth = w_in.shape[0]
    t = batch * seq
    qw = N_HEADS * HEAD_DIM
    kvw = N_KV * HEAD_DIM
    mw = M_HEADS * M_DIM

    o = 0
    cols = {}
    for name, wdt in (("q", qw), ("k", kvw), ("v", kvw), ("mqk", 2 * mw), ("mv", mw), ("mi", M_HEADS),
                      ("mf", M_HEADS), ("mo", mw), ("ga", d), ("gb", d)):
        cols[name] = (o, o + wdt)
        o += wdt

    def wc(name, lo=0, hi=None):
        s, e = cols[name]
        return w_in[:, :, s + lo:(s + hi if hi is not None else e)]

    k0, k1 = wc("k", 0, HEAD_DIM), wc("k", HEAD_DIM, 2 * HEAD_DIM)
    v0, v1 = wc("v", 0, HEAD_DIM), wc("v", HEAD_DIM, 2 * HEAD_DIM)
    w_a = jnp.concatenate([wc("q"), k0, k0, k1, k1, v0, v0, v1, v1], axis=2).astype(BF16)
    w_m = jnp.concatenate([wc("mqk"), wc("mv"), wc("mo")], axis=2).astype(BF16)
    w_g = jnp.concatenate([wc("mi"), wc("mf"), jnp.zeros((depth, d, LANES - 2 * M_HEADS), F32)], axis=2).astype(BF16)
    w_mg = jnp.concatenate([wc("ga"), wc("gb")], axis=2).astype(BF16)
    w_au = w_attn_up.astype(BF16)
    w_mu = w_mlstm_up.astype(BF16)
    w_o = w_out.astype(BF16)
    n_e = w_gate.shape[1]
    w_gu = jnp.concatenate([w_gate, w_up], axis=3).astype(BF16).reshape(depth * n_e, d, -1)
    w_d = w_down.astype(BF16).reshape(depth * n_e, -1, d)

    rw_pad = jnp.concatenate([router_w, jnp.zeros((d, LANES - n_e), F32)], axis=1)
    rw_hi = rw_pad.astype(BF16)
    rw_lo = (rw_pad - rw_hi.astype(F32)).astype(BF16)
    rb = jnp.broadcast_to(router_bias.astype(F32)[:, None], (n_e, LANES))

    qn_w = jnp.tile(q_norm_w, (1, N_HEADS)).reshape(depth, 1, qw)
    kn_w = jnp.tile(k_norm_w, (1, 2 * N_KV)).reshape(depth, 1, 2 * kvw)
    seg = jnp.arange(qw) // HEAD_DIM
    bdq = jnp.where(seg[:, None] == seg[None, :], 1.0 / HEAD_DIM, 0.0).astype(BF16)
    bdk = bdq[:2 * kvw, :2 * kvw]

    inv_freq = ROPE_THETA ** (-(jnp.arange(0, ROPE_DIM, 2, dtype=F32) / ROPE_DIM))
    ang = positions.astype(F32).reshape(t, 1) * inv_freq
    cos8, sin8 = jnp.cos(ang), jnp.sin(ang)
    pad1 = jnp.ones((t, HEAD_DIM - ROPE_DIM), F32)
    pad0 = jnp.zeros((t, HEAD_DIM - ROPE_DIM), F32)
    cos_t = jnp.tile(jnp.concatenate([cos8, cos8, pad1], axis=1), (1, LANES // HEAD_DIM))
    sin_t = jnp.tile(jnp.concatenate([-sin8, sin8, pad0], axis=1), (1, LANES // HEAD_DIM))

    gate_bias = jnp.concatenate([b_igate, b_fgate], axis=1).astype(F32)
    bcol = jnp.broadcast_to(gate_bias[:, :, None], (depth, 2 * M_HEADS, LANES))

    tm_merge = min(512, seq)
    ii = jnp.arange(tm_merge)
    tri = (ii[:, None] <= ii[None, :]).astype(BF16)

    n_blk = (t + N_BUCKETS * (EXPERT_BLOCK - 1)) // EXPERT_BLOCK + 1
    n_rows = n_blk * EXPERT_BLOCK
    pair_a = jnp.asarray(_PAIR_A, I32)
    pair_b = jnp.asarray(_PAIR_B, I32)

    c_pad = jnp.zeros((SUBLANES, d), F32).at[:batch].set(c)
    mod = _ada_mod(c_pad, ada_w, ada_b)[:, :batch]

    xf = x.reshape(t, d)
    for l in range(depth):
        sh1, sc1, g1, sh2, sc2, g2 = [m.reshape(batch, 1, d) for m in jnp.split(mod[l], 6, axis=-1)]

        a_in, m_in, mg, grow = _inproj(xf, sc1, sh1, norm_mix_w.reshape(depth, 1, d), w_a, w_m, w_g, w_mg, l, seq)
        o_attn = _attention(a_in, cos_t, sin_t, sinks[l], qn_w[l], kn_w[l], bdq, bdk, batch, seq)
        hm = _mlstm(m_in, grow, conv_w[l], conv_b[l].reshape(1, -1), bcol[l], mlstm_norm_w[l].reshape(1, mw),
                    batch, seq)
        x1, pay, route, cnt = _merge(o_attn, hm, mg, xf, g1, sc2, sh2, norm_ffn_w.reshape(depth, 1, d),
                                     w_au, w_mu, w_o, rw_hi, rw_lo, rb, tri, l, seq)

        counts = cnt[:N_BUCKETS, 0].astype(I32)
        padded = (counts + EXPERT_BLOCK - 1) // EXPERT_BLOCK * EXPERT_BLOCK
        pad_ends = jnp.cumsum(padded)
        pad_starts = pad_ends - padded
        bucket = route[0].astype(I32)
        dest = pad_starts[bucket] + route[3].astype(I32)
        blk_start = jnp.arange(n_blk, dtype=I32) * EXPERT_BLOCK
        blk_bucket = jnp.minimum(jnp.sum((pad_ends[None, :] <= blk_start[:, None]).astype(I32), axis=1), N_BUCKETS - 1)
        grp = blk_bucket // N_PAIRS
        blk_ea = (l * n_e + grp * EPG + pair_a[blk_bucket % N_PAIRS]).astype(I32)
        blk_eb = (l * n_e + grp * EPG + pair_b[blk_bucket % N_PAIRS]).astype(I32)
        n_real = (pad_ends[-1:] // EXPERT_BLOCK).astype(I32)

        xs = _sc_scatter_rows(pay.reshape(PAY_PARTS * t, LANES), _part_index(dest, PAY_PARTS, n_rows),
                              PAY_PARTS * n_rows).reshape(PAY_PARTS, n_rows, LANES)
        ys = _experts(blk_ea, blk_eb, n_real, xs, w_gu, w_d, d)
        ytok = _sc_gather_rows(ys.reshape(OUT_PARTS * n_rows, LANES),
                               _part_index(dest, OUT_PARTS, n_rows)).reshape(OUT_PARTS, t, LANES)
        xf = _residual(x1, ytok, g2, seq)
    return xf.reshape(batch, seq, d)
```

```python
import functools

import jax
import jax.numpy as jnp
from jax import lax
from jax.experimental import pallas as pl
from jax.experimental.pallas import tpu as pltpu
from jax.experimental.pallas import tpu_sc as plsc

F32 = jnp.float32
BF16 = jnp.bfloat16
U32 = jnp.uint32
I32 = jnp.int32
HIGHEST = lax.Precision.HIGHEST

HEAD_DIM = 64
N_HEADS = 8
N_KV = 2
ROPE_DIM = 16
ROPE_THETA = 500000.0
ATTN_BLOCK = 128
M_HEADS = 4
M_DIM = 128
CONV_K = 4
N_EXPERTS = 16
N_GROUPS = 4
EPG = 4
EPS = 1e-6

LANES = 128
SUBLANES = 8

CHUNK = 128
CHUNKS_PER_STEP = 4
GATE_CHUNKS_PER_STEP = 4
N_PAIRS = 6
N_BUCKETS = N_GROUPS * N_PAIRS
BUCKET_ROWS = 32
EXPERT_BLOCK = 256
PAY_PARTS = 5
OUT_PARTS = 8
SC_WINDOW = 128
VMEM_LIMIT = 56 * 1024 * 1024


def _dot(a, b, precision=None):
    return jnp.dot(a, b, preferred_element_type=F32, precision=precision)


def _dot_nt(a, b):
    return lax.dot_general(a, b, (((1,), (1,)), ((), ())), preferred_element_type=F32)


def _sigmoid(x):
    return 1.0 / (1.0 + jnp.exp(-x))


def _log_sigmoid(x):
    return jnp.minimum(x, 0.0) - jnp.log1p(jnp.exp(-jnp.abs(x)))


def _params(*sem):
    return pltpu.CompilerParams(dimension_semantics=sem, vmem_limit_bytes=VMEM_LIMIT)


def _ada_kernel(c_ref, w_ref, b_ref, o_ref):
    c = c_ref[...]
    ca = c * _sigmoid(c)
    o_ref[0] = _dot(ca, w_ref[0], HIGHEST) + b_ref[0]


def _ada_mod(c_pad, ada_w, ada_b):
    depth, d, n = ada_w.shape
    tn = 1536
    return pl.pallas_call(
        _ada_kernel,
        grid=(depth, n // tn),
        in_specs=[
            pl.BlockSpec((SUBLANES, d), lambda l, j: (0, 0)),
            pl.BlockSpec((1, d, tn), lambda l, j: (l, 0, j)),
            pl.BlockSpec((1, 1, tn), lambda l, j: (l, 0, j)),
        ],
        out_specs=pl.BlockSpec((1, SUBLANES, tn), lambda l, j: (l, 0, j)),
        out_shape=jax.ShapeDtypeStruct((depth, SUBLANES, n), F32),
        compiler_params=_params("arbitrary", "arbitrary"),
        name="ada_mod",
    )(c_pad, ada_w, ada_b.reshape(depth, 1, n))


def _inproj_kernel(*refs, fuse_residual):
    if fuse_residual:
        (x_ref, y_ref, g2_ref, sc_ref, sh_ref, nw_ref, wa_ref, wm_ref, wg_ref, wmg_ref,
         a_ref, m_ref, mg_ref, gr_ref, xo_ref, g_ref) = refs
        y = jnp.concatenate([y_ref[c] for c in range(OUT_PARTS)], axis=1)
        x = x_ref[...] + g2_ref[...] * y
        xo_ref[...] = x
    else:
        (x_ref, sc_ref, sh_ref, nw_ref, wa_ref, wm_ref, wg_ref, wmg_ref,
         a_ref, m_ref, mg_ref, gr_ref, g_ref) = refs
        x = x_ref[...]
    ms = jnp.mean(x * x, axis=-1, keepdims=True)
    h = x * lax.rsqrt(ms + EPS) * nw_ref[...]
    h = h * (1.0 + sc_ref[...]) + sh_ref[...]
    hb = h.astype(BF16)
    a_ref[...] = _dot(hb, wa_ref[...]).astype(BF16)
    m_ref[...] = _dot(hb, wm_ref[...]).astype(BF16)
    mg_ref[...] = _sigmoid(_dot(hb, wmg_ref[...])).astype(BF16)
    g_ref[...] = _dot(hb, wg_ref[...])
    gr_ref[...] = g_ref[...].T[:SUBLANES, :]


def _inproj(x, moe, sc, sh, nw, wa, wm, wg, wmg, l, seq):
    t, d = x.shape
    tm = min(512, seq)
    tpb = seq // tm
    row = lambda i: (i, 0)
    bsel = lambda i: (i // tpb, 0, 0)
    wsel = lambda i: (l, 0, 0)
    na, nm, ng, nmg = wa.shape[2], wm.shape[2], wg.shape[2], wmg.shape[2]
    fuse = moe is not None
    moe_specs = [pl.BlockSpec((OUT_PARTS, tm, LANES), lambda i: (0, i, 0)), pl.BlockSpec((None, 1, d), bsel)]
    return pl.pallas_call(
        functools.partial(_inproj_kernel, fuse_residual=fuse),
        grid=(t // tm,),
        in_specs=[pl.BlockSpec((tm, d), row)] + (moe_specs if fuse else []) + [
            pl.BlockSpec((None, 1, d), bsel),
            pl.BlockSpec((None, 1, d), bsel),
            pl.BlockSpec((None, 1, d), wsel),
            pl.BlockSpec((None, d, na), wsel),
            pl.BlockSpec((None, d, nm), wsel),
            pl.BlockSpec((None, d, ng), wsel),
            pl.BlockSpec((None, d, nmg), wsel),
        ],
        out_specs=[
            pl.BlockSpec((tm, na), row),
            pl.BlockSpec((tm, nm), row),
            pl.BlockSpec((tm, nmg), row),
            pl.BlockSpec((SUBLANES, tm), lambda i: (0, i)),
        ] + ([pl.BlockSpec((tm, d), row)] if fuse else []),
        out_shape=[
            jax.ShapeDtypeStruct((t, na), BF16),
            jax.ShapeDtypeStruct((t, nm), BF16),
            jax.ShapeDtypeStruct((t, nmg), BF16),
            jax.ShapeDtypeStruct((SUBLANES, t), F32),
        ] + ([jax.ShapeDtypeStruct((t, d), F32)] if fuse else []),
        scratch_shapes=[pltpu.VMEM((tm, ng), F32)],
        compiler_params=_params("arbitrary"),
        name="inproj",
    )(x, *(moe if fuse else ()), sc, sh, nw, wa, wm, wg, wmg)


def _rope(t, cos, sin):
    w = t.shape[1]
    reps = w // LANES
    cosw = jnp.concatenate([cos] * reps, axis=1) if reps > 1 else cos
    sinw = jnp.concatenate([sin] * reps, axis=1) if reps > 1 else sin
    lane = lax.broadcasted_iota(I32, t.shape, 1)
    half = ROPE_DIM // 2
    up = pltpu.roll(t, w - half, axis=1)
    dn = pltpu.roll(t, half, axis=1)
    partner = jnp.where((lane % ROPE_DIM) < half, up, dn)
    return t * cosw + partner * sinw


def _head_norm(t, bd, w):
    ms = _dot((t * t).astype(BF16), bd)
    return t * lax.rsqrt(ms + EPS) * w


def _attn_kernel(sink_ref, cur_ref, prev_ref, cos_ref, sin_ref, cosp_ref, sinp_ref,
                 qw_ref, kw_ref, bdq_ref, bdk_ref, o_ref):
    tq = cur_ref.shape[0]
    nj = tq // ATTN_BLOCK
    qw = N_HEADS * HEAD_DIM
    kw = 2 * N_KV * HEAD_DIM
    blk0 = pl.program_id(1) * nj

    cur = cur_ref[...]
    q = cur[:, :qw].astype(F32)
    kc = cur[:, qw:qw + kw].astype(F32)
    vc = cur[:, qw + kw:]
    prev = prev_ref[...]
    kp = prev[:, :kw].astype(F32)
    vp = prev[:, kw:]

    cos, sin = cos_ref[...], sin_ref[...]
    q = _rope(_head_norm(q, bdq_ref[...], qw_ref[...]), cos, sin) * (HEAD_DIM ** -0.5)
    kc = _rope(_head_norm(kc, bdk_ref[...], kw_ref[...]), cos, sin)
    kp = _rope(_head_norm(kp, bdk_ref[...], kw_ref[...]), cosp_ref[...], sinp_ref[...])
    qb = q.astype(BF16)
    k_all = jnp.concatenate([kp, kc], axis=0).astype(BF16)
    v_all = jnp.concatenate([vp, vc], axis=0)

    lane = lax.broadcasted_iota(I32, (ATTN_BLOCK, LANES), 1)
    lo = lane < HEAD_DIM
    zero = jnp.zeros((ATTN_BLOCK, LANES), BF16)
    g_heads = N_HEADS // N_KV
    ri = lax.broadcasted_iota(I32, (g_heads * ATTN_BLOCK, ATTN_BLOCK), 0) % ATTN_BLOCK
    ci = lax.broadcasted_iota(I32, (g_heads * ATTN_BLOCK, ATTN_BLOCK), 1)
    from_prev = ci > ri
    head_row = lax.broadcasted_iota(I32, (g_heads * ATTN_BLOCK, 1), 0) // ATTN_BLOCK
    ones_v = jnp.ones((2 * ATTN_BLOCK, LANES), BF16)

    tiles = [(j, g) for j in range(nj) for g in range(N_KV)]
    scores = {}
    for j, g in tiles:
        rows = slice(j * ATTN_BLOCK, (j + 1) * ATTN_BLOCK)
        band = slice(j * ATTN_BLOCK, (j + 2) * ATTN_BLOCK)
        qp0 = qb[rows, (2 * g) * LANES:(2 * g + 1) * LANES]
        qp1 = qb[rows, (2 * g + 1) * LANES:(2 * g + 2) * LANES]
        q4 = jnp.concatenate([jnp.where(lo, qp0, zero), jnp.where(lo, zero, qp0),
                              jnp.where(lo, qp1, zero), jnp.where(lo, zero, qp1)], axis=0)
        scores[j, g] = _dot_nt(q4, k_all[band, g * LANES:(g + 1) * LANES])

    probs, sink_term = {}, {}
    for j, g in tiles:
        prev_ok = ci > ri + (1 - jnp.minimum(blk0 + j, 1)) * ATTN_BLOCK
        s2 = scores[j, g]
        s = jnp.where(prev_ok, s2[:, :ATTN_BLOCK], jnp.where(from_prev, -jnp.inf, s2[:, ATTN_BLOCK:]))
        sink = jnp.full((g_heads * ATTN_BLOCK, 1), sink_ref[g_heads * g], F32)
        for r in range(1, g_heads):
            sink = jnp.where(head_row == r, sink_ref[g_heads * g + r], sink)
        m = jnp.maximum(jnp.max(s, axis=-1, keepdims=True), sink)
        p = jnp.exp(s - m)
        probs[j, g] = jnp.concatenate([jnp.where(from_prev, p, 0.0), jnp.where(from_prev, 0.0, p)],
                                      axis=1).astype(BF16)
        sink_term[j, g] = jnp.exp(sink - m)

    for j, g in tiles:
        rows = slice(j * ATTN_BLOCK, (j + 1) * ATTN_BLOCK)
        band = slice(j * ATTN_BLOCK, (j + 2) * ATTN_BLOCK)
        o8 = _dot(probs[j, g], jnp.concatenate([v_all[band, g * LANES:(g + 1) * LANES], ones_v], axis=1))
        o4 = o8[:, :LANES] / (o8[:, LANES:] + sink_term[j, g])
        b = ATTN_BLOCK
        o_ref[rows, (2 * g) * LANES:(2 * g + 1) * LANES] = jnp.where(lo, o4[0:b], o4[b:2 * b]).astype(BF16)
        o_ref[rows, (2 * g + 1) * LANES:(2 * g + 2) * LANES] = jnp.where(
            lo, o4[2 * b:3 * b], o4[3 * b:4 * b]).astype(BF16)


def _attention(a_in, cos_t, sin_t, sinks_l, qw, kw, bdq, bdk, batch, seq):
    t = a_in.shape[0]
    tq = min(512, seq)
    nj = tq // ATTN_BLOCK
    tpb = seq // tq
    bpb = seq // ATTN_BLOCK
    qwid = N_HEADS * HEAD_DIM
    kvw = 4 * N_KV * HEAD_DIM
    cur = lambda b, i: (b * tpb + i, 0)
    prv = lambda b, i: (b * bpb + jnp.maximum(i * nj - 1, 0), 1)
    prv0 = lambda b, i: (b * bpb + jnp.maximum(i * nj - 1, 0), 0)
    const = lambda b, i: (0, 0)
    return pl.pallas_call(
        _attn_kernel,
        grid=(batch, tpb),
        in_specs=[
            pl.BlockSpec(memory_space=pltpu.SMEM),
            pl.BlockSpec((tq, qwid + kvw), cur),
            pl.BlockSpec((ATTN_BLOCK, kvw), prv),
            pl.BlockSpec((tq, LANES), cur),
            pl.BlockSpec((tq, LANES), cur),
            pl.BlockSpec((ATTN_BLOCK, LANES), prv0),
            pl.BlockSpec((ATTN_BLOCK, LANES), prv0),
            pl.BlockSpec((1, qwid), const),
            pl.BlockSpec((1, kvw // 2), const),
            pl.BlockSpec((qwid, qwid), const),
            pl.BlockSpec((kvw // 2, kvw // 2), const),
        ],
        out_specs=pl.BlockSpec((tq, qwid), cur),
        out_shape=jax.ShapeDtypeStruct((t, qwid), BF16),
        compiler_params=_params("arbitrary", "arbitrary"),
        name="swa_attention",
    )(sinks_l, a_in, a_in, cos_t, sin_t, cos_t, sin_t, qw, kw, bdq, bdk)


def _mlstm_kernel(min_ref, gr_ref, cw_ref, cb_ref, bcol_ref, nw_ref,
                  hm_ref, ext_ref, q_ref, kt_ref, st_ref, mx_ref, ab_ref, bc_ref):
    tt = min_ref.shape[0]
    mw = M_HEADS * M_DIM
    nchunks = tt // CHUNK

    @pl.when(pl.program_id(1) == 0)
    def _():
        ext_ref[0:SUBLANES, :] = jnp.zeros((SUBLANES, 2 * mw), F32)
        st_ref[...] = jnp.zeros(st_ref.shape, F32)
        mx_ref[...] = jnp.zeros(mx_ref.shape, F32)

    u = min_ref[:, :2 * mw].astype(F32)
    ext_ref[SUBLANES:SUBLANES + tt, :] = u
    acc = cb_ref[...] + cw_ref[CONV_K - 1:CONV_K, :] * u
    for jj in range(CONV_K - 1):
        off = SUBLANES - (CONV_K - 1) + jj
        acc = acc + cw_ref[jj:jj + 1, :] * ext_ref[off:off + tt, :]
    act = acc * _sigmoid(acc)
    ext_ref[0:SUBLANES, :] = u[tt - SUBLANES:tt, :]
    q_ref[...] = act[:, :mw].astype(BF16)
    for j in range(nchunks):
        for h in range(M_HEADS):
            kt_ref[h * M_DIM:(h + 1) * M_DIM, j * CHUNK:(j + 1) * CHUNK] = (
                act[j * CHUNK:(j + 1) * CHUNK, mw + h * M_DIM:mw + (h + 1) * M_DIM] * (M_DIM ** -0.5)).T

    ri = lax.broadcasted_iota(I32, (CHUNK, CHUNK), 0)
    ci = lax.broadcasted_iota(I32, (CHUNK, CHUNK), 1)
    causal = ci <= ri
    triu = jnp.where(ri <= ci, 1.0, 0.0).astype(BF16)
    ones_half = jnp.ones((CHUNK, M_DIM), BF16)
    mean_mat = jnp.full((M_DIM, M_DIM), 1.0 / M_DIM, BF16)
    sub = lax.broadcasted_iota(I32, (SUBLANES, CHUNK), 0)
    heads = range(M_HEADS)

    pad_rows = jnp.zeros((CHUNK - SUBLANES, CHUNK), F32)
    zero_rows = jnp.zeros((SUBLANES, CHUNK), F32)

    def gate_body(jg, carry):
        for u_ in range(GATE_CHUNKS_PER_STEP):
            rs = pl.ds(pl.multiple_of((jg * GATE_CHUNKS_PER_STEP + u_) * CHUNK, CHUNK), CHUNK)
            gr = gr_ref[:, rs] + bcol_ref[...]
            ls = _log_sigmoid(gr)
            ls1 = ls.astype(BF16).astype(F32)
            ls2 = (ls - ls1).astype(BF16).astype(F32)
            pieces = jnp.concatenate([ls1, ls2, ls - ls1 - ls2, zero_rows], axis=0).astype(BF16)
            sums = _dot(pieces, triu)
            br = sums[0:SUBLANES] + sums[SUBLANES:2 * SUBLANES] + sums[2 * SUBLANES:3 * SUBLANES]
            ab = jnp.where(sub < M_HEADS, gr - pltpu.roll(br, M_HEADS, axis=0), br)
            ab_ref[:, rs] = ab
            bc_ref[rs, :] = jnp.concatenate([ab, pad_rows], axis=0).T
        return carry

    lax.fori_loop(0, nchunks // GATE_CHUNKS_PER_STEP, gate_body, 0)

    def group_body(cg, carry):
        rows, ab = [], []
        for u_ in range(CHUNKS_PER_STEP):
            r0 = pl.multiple_of((cg * CHUNKS_PER_STEP + u_) * CHUNK, CHUNK)
            rows.append(pl.ds(r0, CHUNK))
            ab.append(ab_ref[:, rows[u_]])
        lanes = [(u_, h) for u_ in range(CHUNKS_PER_STEP) for h in heads]
        a_r = {(u_, h): ab[u_][h:h + 1, :] for u_, h in lanes}
        b_last = {(u_, h): ab[u_][M_HEADS + h:M_HEADS + h + 1, CHUNK - 1:CHUNK] for u_, h in lanes}

        m_prev, a_max, a_dec, s_in = {}, {}, {}, {}
        m_run = [mx_ref[h][0:1, 0:1] for h in heads]
        for k in lanes:
            u_, h = k
            m_prev[k] = m_run[h]
            a_max[k] = jnp.max(a_r[k], axis=-1, keepdims=True)
            m_loc = b_last[k] + a_max[k]
            m_new = jnp.maximum(b_last[k] + m_prev[k], m_loc)
            a_dec[k] = jnp.exp(b_last[k] + m_prev[k] - m_new)
            s_in[k] = jnp.exp(m_loc - m_new)
            m_run[h] = m_new
        for h in heads:
            mx_ref[h] = jnp.broadcast_to(m_run[h], (SUBLANES, LANES))

        q, v_ext, s_qk, kv = {}, {}, {}, {}
        for k in lanes:
            u_, h = k
            rs = rows[u_]
            q[k] = q_ref[rs, h * M_DIM:(h + 1) * M_DIM]
            kt = kt_ref[h * M_DIM:(h + 1) * M_DIM, rs]
            v = min_ref[rs, 2 * mw + h * M_DIM:2 * mw + (h + 1) * M_DIM]
            v_ext[k] = jnp.concatenate([v, ones_half], axis=1)
            s_qk[k] = _dot(q[k], kt.astype(BF16))
            e_r = jnp.exp(a_r[k] - a_max[k])
            kv[k] = _dot((kt * e_r).astype(BF16), v_ext[k])

        thr, qk, inter = {}, {}, {}
        for k in lanes:
            u_, h = k
            a_mat = jnp.where(causal, a_r[k], -jnp.inf)
            mu = jnp.maximum(jnp.max(a_mat, axis=-1, keepdims=True), m_prev[k])
            b_c = bc_ref[rows[u_], M_HEADS + h:M_HEADS + h + 1]
            thr[k] = jnp.broadcast_to(jnp.exp(-(b_c + mu)), (CHUNK, M_DIM))
            mu_b = jnp.broadcast_to(mu, (CHUNK, CHUNK))
            inter[k] = jnp.exp(m_prev[k] - mu_b)
            qk[k] = (s_qk[k] * jnp.exp(a_mat - mu_b)).astype(BF16)

        q_state = {}
        state = [st_ref[h] for h in heads]
        for k in lanes:
            u_, h = k
            q_state[k] = _dot(q[k], state[h].astype(BF16))
            state[h] = a_dec[k] * state[h] + s_in[k] * kv[k]
        for h in heads:
            st_ref[h] = state[h]

        for k in lanes:
            u_, h = k
            hs = slice(h * M_DIM, (h + 1) * M_DIM)
            num = _dot(qk[k], v_ext[k])
            den = jnp.maximum(jnp.abs(num[:, M_DIM:] + inter[k] * q_state[k][:, M_DIM:]), thr[k])
            hh = (num[:, :M_DIM] + inter[k] * q_state[k][:, :M_DIM]) / den
            msq = _dot((hh * hh).astype(BF16), mean_mat)
            hn = hh * lax.rsqrt(msq + EPS) * nw_ref[:, hs]
            og = min_ref[rows[u_], 3 * mw + h * M_DIM:3 * mw + (h + 1) * M_DIM].astype(F32)
            hm_ref[rows[u_], hs] = (_sigmoid(og) * hn).astype(BF16)
        return carry

    lax.fori_loop(0, nchunks // CHUNKS_PER_STEP, group_body, 0)


def _mlstm(m_in, grow, conv_w, conv_b, bcol, nw, batch, seq):
    t = m_in.shape[0]
    tt = min(1024, seq)
    tpb = seq // tt
    mw = M_HEADS * M_DIM
    cur = lambda b, i: (b * tpb + i, 0)
    const = lambda b, i: (0, 0)
    return pl.pallas_call(
        _mlstm_kernel,
        grid=(batch, tpb),
        in_specs=[
            pl.BlockSpec((tt, 4 * mw), cur),
            pl.BlockSpec((SUBLANES, tt), lambda b, i: (0, b * tpb + i)),
            pl.BlockSpec((CONV_K, 2 * mw), const),
            pl.BlockSpec((1, 2 * mw), const),
            pl.BlockSpec((SUBLANES, LANES), const),
            pl.BlockSpec((1, mw), const),
        ],
        out_specs=pl.BlockSpec((tt, mw), cur),
        out_shape=jax.ShapeDtypeStruct((t, mw), BF16),
        scratch_shapes=[
            pltpu.VMEM((tt + SUBLANES, 2 * mw), F32),
            pltpu.VMEM((tt, mw), BF16),
            pltpu.VMEM((mw, tt), F32),
            pltpu.VMEM((M_HEADS, M_DIM, 2 * M_DIM), F32),
            pltpu.VMEM((M_HEADS, SUBLANES, LANES), F32),
            pltpu.VMEM((SUBLANES, tt), F32),
            pltpu.VMEM((tt, LANES), F32),
        ],
        compiler_params=_params("arbitrary", "arbitrary"),
        name="mlstm",
    )(m_in, grow, conv_w, conv_b, bcol, nw)


def _merge_kernel(o_ref, hm_ref, mg_ref, x_ref, g1_ref, sc_ref, sh_ref, nw_ref,
                  wa_ref, wm_ref, wo_ref, rwh_ref, rwl_ref, rb_ref, tri_ref,
                  x1_ref, pay_ref, route_ref, cnt_ref, carry_ref):
    tm, d = x_ref.shape

    @pl.when(pl.program_id(0) == 0)
    def _():
        carry_ref[...] = jnp.zeros(carry_ref.shape, F32)

    ya = _dot(o_ref[...], wa_ref[...])
    yb = _dot(hm_ref[...], wm_ref[...])
    mg = mg_ref[...]
    merged = mg[:, :d].astype(F32) * ya + mg[:, d:].astype(F32) * yb
    x1 = x_ref[...] + g1_ref[...] * _dot(merged.astype(BF16), wo_ref[...])
    x1_ref[...] = x1

    ms = jnp.mean(x1 * x1, axis=-1, keepdims=True)
    h2 = x1 * lax.rsqrt(ms + EPS) * nw_ref[...]
    h2 = h2 * (1.0 + sc_ref[...]) + sh_ref[...]
    hi = h2.astype(BF16)
    hif = hi.astype(F32)
    lo = (h2 - hif).astype(BF16)
    r_hi = _dot_nt(rwh_ref[...], hi)
    r_lo = _dot_nt(rwl_ref[...], lo)
    sc_t = _sigmoid(r_hi[:N_EXPERTS] + r_hi[N_EXPERTS:] + r_lo)
    sel_t = sc_t + rb_ref[:, 0:1]

    def row(a, e):
        return a[e:e + 1, :]

    best = None
    gi = jnp.zeros((1, tm), I32)
    for g in range(N_GROUPS):
        r = [row(sel_t, EPG * g + i) for i in range(EPG)]
        gs = None
        for i in range(EPG):
            for j in range(i + 1, EPG):
                pr = r[i] + r[j]
                gs = pr if gs is None else jnp.maximum(gs, pr)
        if best is None:
            best = gs
        else:
            upd = gs > best
            gi = jnp.where(upd, g, gi)
            best = jnp.maximum(best, gs)

    def pick(a, i):
        out = row(a, i)
        for g in range(1, N_GROUPS):
            out = jnp.where(gi == g, row(a, EPG * g + i), out)
        return out

    v = [pick(sel_t, i) for i in range(EPG)]
    s = [pick(sc_t, i) for i in range(EPG)]

    def argmax4(vals):
        bv, bi = vals[0], jnp.zeros((1, tm), I32)
        for i in range(1, EPG):
            upd = vals[i] > bv
            bi = jnp.where(upd, i, bi)
            bv = jnp.maximum(bv, vals[i])
        return bi

    i1 = argmax4(v)
    i2 = argmax4([jnp.where(i1 == i, -jnp.inf, v[i]) for i in range(EPG)])
    ia = jnp.minimum(i1, i2)
    ib = jnp.maximum(i1, i2)
    pidx = jnp.where(ia == 0, ib - 1, jnp.where(ia == 1, ib + 1, N_PAIRS - 1))
    bucket = gi * N_PAIRS + pidx

    def by_index(vals, idx):
        out = vals[0]
        for i in range(1, EPG):
            out = jnp.where(idx == i, vals[i], out)
        return out

    s_a, s_b = by_index(s, ia), by_index(s, ib)
    gate_a = s_a / (s_a + s_b)
    gate_b = s_b / (s_a + s_b)

    brow = lax.broadcasted_iota(I32, (BUCKET_ROWS, tm), 0)
    onehot = brow == bucket
    cums = _dot(jnp.where(onehot, 1.0, 0.0).astype(BF16), tri_ref[...])
    carry = carry_ref[...]
    rank = jnp.sum(jnp.where(onehot, carry[:, 0:1] + cums, 0.0), axis=0, keepdims=True) - 1.0
    new_carry = carry + cums[:, tm - 1:tm]
    carry_ref[...] = new_carry
    cnt_ref[...] = new_carry

    route_ref[...] = jnp.concatenate(
        [bucket.astype(F32), gate_a, gate_b, rank, jnp.zeros((SUBLANES - 4, tm), F32)], axis=0)

    bits = lax.bitcast_convert_type(hif, U32)
    half = d // 2
    packed = bits[:, :half] | (bits[:, half:] >> 16)
    for cpart in range(half // LANES):
        pay_ref[cpart] = packed[:, cpart * LANES:(cpart + 1) * LANES]
    gates_t = jnp.concatenate([gate_a, gate_b, jnp.zeros((LANES - 2, tm), F32)], axis=0)
    pay_ref[half // LANES] = lax.bitcast_convert_type(gates_t.T, U32)


def _merge(o_attn, hm, mg, x, g1, sc2, sh2, nw, wa, wm, wo, rwh, rwl, rb, tri, l, seq):
    t, d = x.shape
    tm = tri.shape[0]
    tpb = seq // tm
    row = lambda i: (i, 0)
    bsel = lambda i: (i // tpb, 0, 0)
    wsel = lambda i: (l, 0, 0)
    const = lambda i: (0, 0)
    hw = o_attn.shape[1]
    return pl.pallas_call(
        _merge_kernel,
        grid=(t // tm,),
        in_specs=[
            pl.BlockSpec((tm, hw), row),
            pl.BlockSpec((tm, hw), row),
            pl.BlockSpec((tm, 2 * d), row),
            pl.BlockSpec((tm, d), row),
            pl.BlockSpec((None, 1, d), bsel),
            pl.BlockSpec((None, 1, d), bsel),
            pl.BlockSpec((None, 1, d), bsel),
            pl.BlockSpec((None, 1, d), wsel),
            pl.BlockSpec((None, hw, d), wsel),
            pl.BlockSpec((None, hw, d), wsel),
            pl.BlockSpec((None, d, d), wsel),
            pl.BlockSpec((2 * N_EXPERTS, d), const),
            pl.BlockSpec((N_EXPERTS, d), const),
            pl.BlockSpec((N_EXPERTS, LANES), const),
            pl.BlockSpec((tm, tm), const),
        ],
        out_specs=[
            pl.BlockSpec((tm, d), row),
            pl.BlockSpec((PAY_PARTS, tm, LANES), lambda i: (0, i, 0)),
            pl.BlockSpec((SUBLANES, tm), lambda i: (0, i)),
            pl.BlockSpec((BUCKET_ROWS, LANES), const),
        ],
        out_shape=[
            jax.ShapeDtypeStruct((t, d), F32),
            jax.ShapeDtypeStruct((PAY_PARTS, t, LANES), U32),
            jax.ShapeDtypeStruct((SUBLANES, t), F32),
            jax.ShapeDtypeStruct((BUCKET_ROWS, LANES), F32),
        ],
        scratch_shapes=[pltpu.VMEM((BUCKET_ROWS, LANES), F32)],
        compiler_params=_params("arbitrary"),
        name="merge_router",
    )(o_attn, hm, mg, x, g1, sc2, sh2, nw, wa, wm, wo, rwh, rwl, rb, tri)


def _sc_mesh():
    return plsc.VectorSubcoreMesh(core_axis_name="core", subcore_axis_name="subcore")


def _sc_scatter_rows(rows, dest, n_out):
    n, w = rows.shape

    @pl.kernel(out_type=jax.ShapeDtypeStruct((n_out, w), rows.dtype), mesh=_sc_mesh(), scratch_types=[])
    def scatter(x_hbm, i_hbm, o_hbm):
        def body(x_vmem, i_vmem):
            pltpu.sync_copy(x_vmem, o_hbm.at[i_vmem.at[0]])

        pltpu.emit_pipeline(
            body,
            grid=(n // SC_WINDOW,),
            in_specs=[pl.BlockSpec((SC_WINDOW, w), lambda i: (i, 0)),
                      pl.BlockSpec((1, SC_WINDOW), lambda i: (0, i))],
            out_specs=[],
            core_axis_name=("core", "subcore"),
            dimension_semantics=(pltpu.PARALLEL,),
        )(x_hbm, i_hbm)

    return scatter(rows, dest.reshape(1, n))


def _sc_gather_rows(src, idx):
    n = idx.shape[0]
    w = src.shape[1]

    @pl.kernel(out_type=jax.ShapeDtypeStruct((n, w), src.dtype), mesh=_sc_mesh(), scratch_types=[])
    def gather(x_hbm, i_hbm, o_hbm):
        def body(i_vmem, o_vmem):
            pltpu.sync_copy(x_hbm.at[i_vmem.at[0]], o_vmem)

        pltpu.emit_pipeline(
            body,
            grid=(n // SC_WINDOW,),
            in_specs=[pl.BlockSpec((1, SC_WINDOW), lambda i: (0, i))],
            out_specs=[pl.BlockSpec((SC_WINDOW, w), lambda i: (i, 0))],
            core_axis_name=("core", "subcore"),
            dimension_semantics=(pltpu.PARALLEL,),
        )(i_hbm, o_hbm)

    return gather(src, idx.reshape(1, n))


def _part_index(dest, parts, n_rows):
    return (jnp.arange(parts, dtype=I32)[:, None] * n_rows + dest[None, :]).reshape(-1)


def _residual_kernel(x_ref, y_ref, g2_ref, o_ref):
    y = jnp.concatenate([y_ref[c] for c in range(OUT_PARTS)], axis=1)
    o_ref[...] = x_ref[...] + g2_ref[...] * y


def _residual(x1, ytok, g2, seq):
    t, d = x1.shape
    tm = min(512, seq)
    tpb = seq // tm
    return pl.pallas_call(
        _residual_kernel,
        grid=(t // tm,),
        in_specs=[
            pl.BlockSpec((tm, d), lambda i: (i, 0)),
            pl.BlockSpec((OUT_PARTS, tm, LANES), lambda i: (0, i, 0)),
            pl.BlockSpec((None, 1, d), lambda i: (i // tpb, 0, 0)),
        ],
        out_specs=pl.BlockSpec((tm, d), lambda i: (i, 0)),
        out_shape=jax.ShapeDtypeStruct((t, d), F32),
        compiler_params=_params("arbitrary"),
        name="residual",
    )(x1, ytok, g2)


def _expert_kernel(ea_ref, eb_ref, nr_ref, xs_ref, wgua_ref, wda_ref, wgub_ref, wdb_ref, ys_ref):
    j = pl.program_id(0)
    nr = nr_ref[0]

    @pl.when(j < nr)
    def _():
        pk = jnp.concatenate([xs_ref[c] for c in range(PAY_PARTS - 1)], axis=1)
        xa = lax.bitcast_convert_type(pk & jnp.uint32(0xFFFF0000), F32)
        xb = lax.bitcast_convert_type(pk << 16, F32)
        x = jnp.concatenate([xa, xb], axis=1).astype(BF16)
        gl = lax.bitcast_convert_type(xs_ref[PAY_PARTS - 1], F32)

        def ffn(wgu_ref, wd_ref):
            gu = _dot(x, wgu_ref[...])
            f = gu.shape[1] // 2
            gte, up = gu[:, :f], gu[:, f:]
            act = gte * _sigmoid(gte) * up
            return _dot(act.astype(BF16), wd_ref[...])

        y = gl[:, 0:1] * ffn(wgua_ref, wda_ref) + gl[:, 1:2] * ffn(wgub_ref, wdb_ref)
        for c in range(OUT_PARTS):
            ys_ref[c] = y[:, c * LANES:(c + 1) * LANES]

    @pl.when(j >= nr)
    def _():
        ys_ref[...] = jnp.zeros(ys_ref.shape, F32)


def _experts(blk_ea, blk_eb, n_real, xs, wgu, wd, d):
    n_rows = xs.shape[1]
    nblk = n_rows // EXPERT_BLOCK
    f2 = wgu.shape[2]
    grid_spec = pltpu.PrefetchScalarGridSpec(
        num_scalar_prefetch=3,
        grid=(nblk,),
        in_specs=[
            pl.BlockSpec((PAY_PARTS, EXPERT_BLOCK, LANES), lambda j, ea, eb, nr: (0, j, 0)),
            pl.BlockSpec((None, d, f2), lambda j, ea, eb, nr: (ea[j], 0, 0)),
            pl.BlockSpec((None, f2 // 2, d), lambda j, ea, eb, nr: (ea[j], 0, 0)),
            pl.BlockSpec((None, d, f2), lambda j, ea, eb, nr: (eb[j], 0, 0)),
            pl.BlockSpec((None, f2 // 2, d), lambda j, ea, eb, nr: (eb[j], 0, 0)),
        ],
        out_specs=pl.BlockSpec((OUT_PARTS, EXPERT_BLOCK, LANES), lambda j, ea, eb, nr: (0, j, 0)),
    )
    return pl.pallas_call(
        _expert_kernel,
        grid_spec=grid_spec,
        out_shape=jax.ShapeDtypeStruct((OUT_PARTS, n_rows, LANES), F32),
        compiler_params=_params("arbitrary"),
        name="experts",
    )(blk_ea, blk_eb, n_real, xs, wgu, wd, wgu, wd)


_PAIR_A = (0, 0, 0, 1, 1, 2)
_PAIR_B = (1, 2, 3, 2, 3, 3)


def kernel(x, c, positions, ada_w, ada_b, norm_mix_w, norm_ffn_w, w_in, b_igate, b_fgate, q_norm_w, k_norm_w,
           sinks, conv_w, conv_b, mlstm_norm_w, w_attn_up, w_mlstm_up, w_out, router_w, router_bias,
           w_gate, w_up, w_down):
    batch, seq, d = x.shape
    depth = w_in.shape[0]
    t = batch * seq
    qw = N_HEADS * HEAD_DIM
    kvw = N_KV * HEAD_DIM
    mw = M_HEADS * M_DIM

    o = 0
    cols = {}
    for name, wdt in (("q", qw), ("k", kvw), ("v", kvw), ("mqk", 2 * mw), ("mv", mw), ("mi", M_HEADS),
                      ("mf", M_HEADS), ("mo", mw), ("ga", d), ("gb", d)):
        cols[name] = (o, o + wdt)
        o += wdt

    def wc(name, lo=0, hi=None):
        s, e = cols[name]
        return w_in[:, :, s + lo:(s + hi if hi is not None else e)]

    k0, k1 = wc("k", 0, HEAD_DIM), wc("k", HEAD_DIM, 2 * HEAD_DIM)
    v0, v1 = wc("v", 0, HEAD_DIM), wc("v", HEAD_DIM, 2 * HEAD_DIM)
    w_a = jnp.concatenate([wc("q"), k0, k0, k1, k1, v0, v0, v1, v1], axis=2).astype(BF16)
    w_m = jnp.concatenate([wc("mqk"), wc("mv"), wc("mo")], axis=2).astype(BF16)
    w_g = jnp.concatenate([wc("mi"), wc("mf"), jnp.zeros((depth, d, LANES - 2 * M_HEADS), F32)], axis=2).astype(BF16)
    w_mg = jnp.concatenate([wc("ga"), wc("gb")], axis=2).astype(BF16)
    w_au = w_attn_up.astype(BF16)
    w_mu = w_mlstm_up.astype(BF16)
    w_o = w_out.astype(BF16)
    n_e = w_gate.shape[1]
    w_gu = jnp.concatenate([w_gate, w_up], axis=3).astype(BF16).reshape(depth * n_e, d, -1)
    w_d = w_down.astype(BF16).reshape(depth * n_e, -1, d)

    rw_t = router_w.astype(F32).T
    rw_top = rw_t.astype(BF16)
    rw_hi = jnp.concatenate([rw_top, (rw_t - rw_top.astype(F32)).astype(BF16)], axis=0)
    rw_lo = rw_top
    rb = jnp.broadcast_to(router_bias.astype(F32)[:, None], (n_e, LANES))

    qn_w = jnp.tile(q_norm_w, (1, N_HEADS)).reshape(depth, 1, qw)
    kn_w = jnp.tile(k_norm_w, (1, 2 * N_KV)).reshape(depth, 1, 2 * kvw)
    seg = jnp.arange(qw) // HEAD_DIM
    bdq = jnp.where(seg[:, None] == seg[None, :], 1.0 / HEAD_DIM, 0.0).astype(BF16)
    bdk = bdq[:2 * kvw, :2 * kvw]

    inv_freq = ROPE_THETA ** (-(jnp.arange(0, ROPE_DIM, 2, dtype=F32) / ROPE_DIM))
    ang = positions.astype(F32).reshape(t, 1) * inv_freq
    cos8, sin8 = jnp.cos(ang), jnp.sin(ang)
    pad1 = jnp.ones((t, HEAD_DIM - ROPE_DIM), F32)
    pad0 = jnp.zeros((t, HEAD_DIM - ROPE_DIM), F32)
    cos_t = jnp.tile(jnp.concatenate([cos8, cos8, pad1], axis=1), (1, LANES // HEAD_DIM))
    sin_t = jnp.tile(jnp.concatenate([-sin8, sin8, pad0], axis=1), (1, LANES // HEAD_DIM))

    gate_bias = jnp.concatenate([b_igate, b_fgate], axis=1).astype(F32)
    bcol = jnp.broadcast_to(gate_bias[:, :, None], (depth, 2 * M_HEADS, LANES))

    tm_merge = min(512, seq)
    ii = jnp.arange(tm_merge)
    tri = (ii[:, None] <= ii[None, :]).astype(BF16)

    n_blk = (t + N_BUCKETS * (EXPERT_BLOCK - 1)) // EXPERT_BLOCK + 1
    n_rows = n_blk * EXPERT_BLOCK
    pair_a = jnp.asarray(_PAIR_A, I32)
    pair_b = jnp.asarray(_PAIR_B, I32)

    c_pad = jnp.zeros((SUBLANES, d), F32).at[:batch].set(c)
    mod = _ada_mod(c_pad, ada_w, ada_b)[:, :batch]

    xf = x.reshape(t, d)
    moe = None
    for l in range(depth):
        sh1, sc1, g1, sh2, sc2, g2 = [m.reshape(batch, 1, d) for m in jnp.split(mod[l], 6, axis=-1)]

        outs = _inproj(xf, moe, sc1, sh1, norm_mix_w.reshape(depth, 1, d), w_a, w_m, w_g, w_mg, l, seq)
        a_in, m_in, mg, grow = outs[:4]
        if moe is not None:
            xf = outs[4]
        o_attn = _attention(a_in, cos_t, sin_t, sinks[l], qn_w[l], kn_w[l], bdq, bdk, batch, seq)
        hm = _mlstm(m_in, grow, conv_w[l], conv_b[l].reshape(1, -1), bcol[l], mlstm_norm_w[l].reshape(1, mw),
                    batch, seq)
        x1, pay, route, cnt = _merge(o_attn, hm, mg, xf, g1, sc2, sh2, norm_ffn_w.reshape(depth, 1, d),
                                     w_au, w_mu, w_o, rw_hi, rw_lo, rb, tri, l, seq)

        counts = cnt[:N_BUCKETS, 0].astype(I32)
        padded = (counts + EXPERT_BLOCK - 1) // EXPERT_BLOCK * EXPERT_BLOCK
        pad_ends = jnp.cumsum(padded)
        pad_starts = pad_ends - padded
        bucket = route[0].astype(I32)
        dest = pad_starts[bucket] + route[3].astype(I32)
        blk_start = jnp.arange(n_blk, dtype=I32) * EXPERT_BLOCK
        blk_bucket = jnp.minimum(jnp.sum((pad_ends[None, :] <= blk_start[:, None]).astype(I32), axis=1), N_BUCKETS - 1)
        grp = blk_bucket // N_PAIRS
        blk_ea = (l * n_e + grp * EPG + pair_a[blk_bucket % N_PAIRS]).astype(I32)
        blk_eb = (l * n_e + grp * EPG + pair_b[blk_bucket % N_PAIRS]).astype(I32)
        n_real = (pad_ends[-1:] // EXPERT_BLOCK).astype(I32)

        xs = _sc_scatter_rows(pay.reshape(PAY_PARTS * t, LANES), _part_index(dest, PAY_PARTS, n_rows),
                              PAY_PARTS * n_rows).reshape(PAY_PARTS, n_rows, LANES)
        ys = _experts(blk_ea, blk_eb, n_real, xs, w_gu, w_d, d)
        ytok = _sc_gather_rows(ys.reshape(OUT_PARTS * n_rows, LANES),
                               _part_index(dest, OUT_PARTS, n_rows)).reshape(OUT_PARTS, t, LANES)
        xf, moe = x1, (ytok, g2)
    return _residual(xf, moe[0], moe[1], seq).reshape(batch, seq, d)
```

```python
import functools

import jax
import jax.numpy as jnp
from jax import lax
from jax.experimental import pallas as pl
from jax.experimental.pallas import tpu as pltpu
from jax.experimental.pallas import tpu_sc as plsc

F32 = jnp.float32
BF16 = jnp.bfloat16
U32 = jnp.uint32
I32 = jnp.int32
HIGHEST = lax.Precision.HIGHEST

HEAD_DIM = 64
N_HEADS = 8
N_KV = 2
ROPE_DIM = 16
ROPE_THETA = 500000.0
ATTN_BLOCK = 128
M_HEADS = 4
M_DIM = 128
CONV_K = 4
N_EXPERTS = 16
N_GROUPS = 4
EPG = 4
EPS = 1e-6

LANES = 128
SUBLANES = 8

CHUNK = 128
CHUNKS_PER_STEP = 4
GATE_CHUNKS_PER_STEP = 4
N_PAIRS = 6
N_BUCKETS = N_GROUPS * N_PAIRS
BUCKET_ROWS = 32
EXPERT_BLOCK = 256
PAY_PARTS = 5
OUT_PARTS = 4
SC_WINDOW = 128
VMEM_LIMIT = 56 * 1024 * 1024


def _dot(a, b, precision=None):
    return jnp.dot(a, b, preferred_element_type=F32, precision=precision)


def _dot_nt(a, b):
    return lax.dot_general(a, b, (((1,), (1,)), ((), ())), preferred_element_type=F32)


def _sigmoid(x):
    return 1.0 / (1.0 + jnp.exp(-x))


def _log_sigmoid(x):
    return jnp.minimum(x, 0.0) - jnp.log1p(jnp.exp(-jnp.abs(x)))


def _pack_bf16_pairs(v):
    n = v.shape[1] // 2
    bits = lax.bitcast_convert_type(v.astype(BF16).astype(F32), U32)
    return bits[:, :n] | (bits[:, n:] >> 16)


def _unpack_bf16_pairs(w):
    hi = lax.bitcast_convert_type(w & jnp.uint32(0xFFFF0000), F32)
    lo = lax.bitcast_convert_type(w << 16, F32)
    return jnp.concatenate([hi, lo], axis=1)


def _params(*sem):
    return pltpu.CompilerParams(dimension_semantics=sem, vmem_limit_bytes=VMEM_LIMIT)


def _ada_kernel(c_ref, w_ref, b_ref, o_ref):
    c = c_ref[...]
    ca = c * _sigmoid(c)
    o_ref[0] = _dot(ca, w_ref[0], HIGHEST) + b_ref[0]


def _ada_mod(c_pad, ada_w, ada_b):
    depth, d, n = ada_w.shape
    tn = 1536
    return pl.pallas_call(
        _ada_kernel,
        grid=(depth, n // tn),
        in_specs=[
            pl.BlockSpec((SUBLANES, d), lambda l, j: (0, 0)),
            pl.BlockSpec((1, d, tn), lambda l, j: (l, 0, j)),
            pl.BlockSpec((1, 1, tn), lambda l, j: (l, 0, j)),
        ],
        out_specs=pl.BlockSpec((1, SUBLANES, tn), lambda l, j: (l, 0, j)),
        out_shape=jax.ShapeDtypeStruct((depth, SUBLANES, n), F32),
        compiler_params=_params("arbitrary", "arbitrary"),
        name="ada_mod",
    )(c_pad, ada_w, ada_b.reshape(depth, 1, n))


def _inproj_kernel(*refs, fuse_residual):
    if fuse_residual:
        (x_ref, y_ref, g2_ref, sc_ref, sh_ref, nw_ref, wa_ref, wm_ref, wg_ref, wmg_ref,
         a_ref, m_ref, mg_ref, gr_ref, xo_ref, g_ref) = refs
        y = _unpack_bf16_pairs(jnp.concatenate([y_ref[c] for c in range(OUT_PARTS)], axis=1))
        x = x_ref[...] + g2_ref[...] * y
        xo_ref[...] = x
    else:
        (x_ref, sc_ref, sh_ref, nw_ref, wa_ref, wm_ref, wg_ref, wmg_ref,
         a_ref, m_ref, mg_ref, gr_ref, g_ref) = refs
        x = x_ref[...]
    ms = jnp.mean(x * x, axis=-1, keepdims=True)
    h = x * lax.rsqrt(ms + EPS) * nw_ref[...]
    h = h * (1.0 + sc_ref[...]) + sh_ref[...]
    hb = h.astype(BF16)
    a_ref[...] = _dot(hb, wa_ref[...]).astype(BF16)
    m_ref[...] = _dot(hb, wm_ref[...]).astype(BF16)
    mg_ref[...] = _sigmoid(_dot(hb, wmg_ref[...])).astype(BF16)
    g_ref[...] = _dot(hb, wg_ref[...])
    gr_ref[...] = g_ref[...].T[:SUBLANES, :]


def _inproj(x, moe, sc, sh, nw, wa, wm, wg, wmg, l, seq):
    t, d = x.shape
    tm = min(512, seq)
    tpb = seq // tm
    row = lambda i: (i, 0)
    bsel = lambda i: (i // tpb, 0, 0)
    wsel = lambda i: (l, 0, 0)
    na, nm, ng, nmg = wa.shape[2], wm.shape[2], wg.shape[2], wmg.shape[2]
    fuse = moe is not None
    moe_specs = [pl.BlockSpec((OUT_PARTS, tm, LANES), lambda i: (0, i, 0)), pl.BlockSpec((None, 1, d), bsel)]
    return pl.pallas_call(
        functools.partial(_inproj_kernel, fuse_residual=fuse),
        grid=(t // tm,),
        in_specs=[pl.BlockSpec((tm, d), row)] + (moe_specs if fuse else []) + [
            pl.BlockSpec((None, 1, d), bsel),
            pl.BlockSpec((None, 1, d), bsel),
            pl.BlockSpec((None, 1, d), wsel),
            pl.BlockSpec((None, d, na), wsel),
            pl.BlockSpec((None, d, nm), wsel),
            pl.BlockSpec((None, d, ng), wsel),
            pl.BlockSpec((None, d, nmg), wsel),
        ],
        out_specs=[
            pl.BlockSpec((tm, na), row),
            pl.BlockSpec((tm, nm), row),
            pl.BlockSpec((tm, nmg), row),
            pl.BlockSpec((SUBLANES, tm), lambda i: (0, i)),
        ] + ([pl.BlockSpec((tm, d), row)] if fuse else []),
        out_shape=[
            jax.ShapeDtypeStruct((t, na), BF16),
            jax.ShapeDtypeStruct((t, nm), BF16),
            jax.ShapeDtypeStruct((t, nmg), BF16),
            jax.ShapeDtypeStruct((SUBLANES, t), F32),
        ] + ([jax.ShapeDtypeStruct((t, d), F32)] if fuse else []),
        scratch_shapes=[pltpu.VMEM((tm, ng), F32)],
        compiler_params=_params("arbitrary"),
        name="inproj",
    )(x, *(moe if fuse else ()), sc, sh, nw, wa, wm, wg, wmg)


def _rope(t, cos, sin):
    w = t.shape[1]
    reps = w // LANES
    cosw = jnp.concatenate([cos] * reps, axis=1) if reps > 1 else cos
    sinw = jnp.concatenate([sin] * reps, axis=1) if reps > 1 else sin
    lane = lax.broadcasted_iota(I32, t.shape, 1)
    half = ROPE_DIM // 2
    up = pltpu.roll(t, w - half, axis=1)
    dn = pltpu.roll(t, half, axis=1)
    partner = jnp.where((lane % ROPE_DIM) < half, up, dn)
    return t * cosw + partner * sinw


def _head_norm(t, bd, w):
    ms = _dot((t * t).astype(BF16), bd)
    return t * lax.rsqrt(ms + EPS) * w


def _attn_kernel(sink_ref, cur_ref, prev_ref, cos_ref, sin_ref, cosp_ref, sinp_ref,
                 qw_ref, kw_ref, bdq_ref, bdk_ref, o_ref):
    tq = cur_ref.shape[0]
    nj = tq // ATTN_BLOCK
    qw = N_HEADS * HEAD_DIM
    kw = N_KV * HEAD_DIM
    blk0 = pl.program_id(1) * nj

    cur = cur_ref[...]
    q = cur[:, :qw].astype(F32)
    kc = cur[:, qw:qw + kw].astype(F32)
    vc = cur[:, qw + kw:].astype(F32)
    prev = prev_ref[...]
    kp = prev[:, :kw].astype(F32)
    vp = prev[:, kw:].astype(F32)

    cos, sin = cos_ref[...], sin_ref[...]
    q = _rope(_head_norm(q, bdq_ref[...], qw_ref[...]), cos, sin) * (HEAD_DIM ** -0.5)
    kc = _rope(_head_norm(kc, bdk_ref[...], kw_ref[...]), cos, sin)
    kp = _rope(_head_norm(kp, bdk_ref[...], kw_ref[...]), cosp_ref[...], sinp_ref[...])
    qb = q.astype(BF16)

    def both_halves(x2):
        swapped = pltpu.roll(x2, HEAD_DIM, axis=1)
        first = lax.broadcasted_iota(I32, x2.shape, 1) < HEAD_DIM
        return jnp.concatenate([jnp.where(first, x2, swapped), jnp.where(first, swapped, x2)], axis=1).astype(BF16)

    k_all = both_halves(jnp.concatenate([kp, kc], axis=0))
    v_all = both_halves(jnp.concatenate([vp, vc], axis=0))

    lane = lax.broadcasted_iota(I32, (ATTN_BLOCK, LANES), 1)
    lo = lane < HEAD_DIM
    zero = jnp.zeros((ATTN_BLOCK, LANES), BF16)
    g_heads = N_HEADS // N_KV
    ri = lax.broadcasted_iota(I32, (g_heads * ATTN_BLOCK, ATTN_BLOCK), 0) % ATTN_BLOCK
    ci = lax.broadcasted_iota(I32, (g_heads * ATTN_BLOCK, ATTN_BLOCK), 1)
    from_prev = ci > ri
    head_row = lax.broadcasted_iota(I32, (g_heads * ATTN_BLOCK, 1), 0) // ATTN_BLOCK
    ones_v = jnp.ones((2 * ATTN_BLOCK, LANES), BF16)

    tiles = [(j, g) for j in range(nj) for g in range(N_KV)]
    scores = {}
    for j, g in tiles:
        rows = slice(j * ATTN_BLOCK, (j + 1) * ATTN_BLOCK)
        band = slice(j * ATTN_BLOCK, (j + 2) * ATTN_BLOCK)
        qp0 = qb[rows, (2 * g) * LANES:(2 * g + 1) * LANES]
        qp1 = qb[rows, (2 * g + 1) * LANES:(2 * g + 2) * LANES]
        q4 = jnp.concatenate([jnp.where(lo, qp0, zero), jnp.where(lo, zero, qp0),
                              jnp.where(lo, qp1, zero), jnp.where(lo, zero, qp1)], axis=0)
        scores[j, g] = _dot_nt(q4, k_all[band, g * LANES:(g + 1) * LANES])

    probs, sink_term = {}, {}
    for j, g in tiles:
        prev_ok = ci > ri + (1 - jnp.minimum(blk0 + j, 1)) * ATTN_BLOCK
        s2 = scores[j, g]
        s = jnp.where(prev_ok, s2[:, :ATTN_BLOCK], jnp.where(from_prev, -jnp.inf, s2[:, ATTN_BLOCK:]))
        sink = jnp.full((g_heads * ATTN_BLOCK, 1), sink_ref[g_heads * g], F32)
        for r in range(1, g_heads):
            sink = jnp.where(head_row == r, sink_ref[g_heads * g + r], sink)
        m = jnp.maximum(jnp.max(s, axis=-1, keepdims=True), sink)
        p = jnp.exp(s - m)
        probs[j, g] = jnp.concatenate([jnp.where(from_prev, p, 0.0), jnp.where(from_prev, 0.0, p)],
                                      axis=1).astype(BF16)
        sink_term[j, g] = jnp.exp(sink - m)

    for j, g in tiles:
        rows = slice(j * ATTN_BLOCK, (j + 1) * ATTN_BLOCK)
        band = slice(j * ATTN_BLOCK, (j + 2) * ATTN_BLOCK)
        o8 = _dot(probs[j, g], jnp.concatenate([v_all[band, g * LANES:(g + 1) * LANES], ones_v], axis=1))
        o4 = o8[:, :LANES] / (o8[:, LANES:] + sink_term[j, g])
        b = ATTN_BLOCK
        o_ref[rows, (2 * g) * LANES:(2 * g + 1) * LANES] = jnp.where(lo, o4[0:b], o4[b:2 * b]).astype(BF16)
        o_ref[rows, (2 * g + 1) * LANES:(2 * g + 2) * LANES] = jnp.where(
            lo, o4[2 * b:3 * b], o4[3 * b:4 * b]).astype(BF16)


def _attention(a_in, cos_t, sin_t, sinks_l, qw, kw, bdq, bdk, batch, seq):
    t = a_in.shape[0]
    tq = min(512, seq)
    nj = tq // ATTN_BLOCK
    tpb = seq // tq
    bpb = seq // ATTN_BLOCK
    qwid = N_HEADS * HEAD_DIM
    kvw = 2 * N_KV * HEAD_DIM
    cur = lambda b, i: (b * tpb + i, 0)
    prv = lambda b, i: (b * bpb + jnp.maximum(i * nj - 1, 0), qwid // kvw)
    prv0 = lambda b, i: (b * bpb + jnp.maximum(i * nj - 1, 0), 0)
    const = lambda b, i: (0, 0)
    return pl.pallas_call(
        _attn_kernel,
        grid=(batch, tpb),
        in_specs=[
            pl.BlockSpec(memory_space=pltpu.SMEM),
            pl.BlockSpec((tq, qwid + kvw), cur),
            pl.BlockSpec((ATTN_BLOCK, kvw), prv),
            pl.BlockSpec((tq, LANES), cur),
            pl.BlockSpec((tq, LANES), cur),
            pl.BlockSpec((ATTN_BLOCK, LANES), prv0),
            pl.BlockSpec((ATTN_BLOCK, LANES), prv0),
            pl.BlockSpec((1, qwid), const),
            pl.BlockSpec((1, kvw // 2), const),
            pl.BlockSpec((qwid, qwid), const),
            pl.BlockSpec((kvw // 2, kvw // 2), const),
        ],
        out_specs=pl.BlockSpec((tq, qwid), cur),
        out_shape=jax.ShapeDtypeStruct((t, qwid), BF16),
        compiler_params=_params("arbitrary", "arbitrary"),
        name="swa_attention",
    )(sinks_l, a_in, a_in, cos_t, sin_t, cos_t, sin_t, qw, kw, bdq, bdk)


def _mlstm_kernel(min_ref, gr_ref, cw_ref, cb_ref, bcol_ref, nw_ref,
                  hm_ref, ext_ref, q_ref, kt_ref, st_ref, mx_ref, ab_ref, bc_ref):
    tt = min_ref.shape[0]
    mw = M_HEADS * M_DIM
    nchunks = tt // CHUNK

    @pl.when(pl.program_id(1) == 0)
    def _():
        ext_ref[0:SUBLANES, :] = jnp.zeros((SUBLANES, 2 * mw), F32)
        st_ref[...] = jnp.zeros(st_ref.shape, F32)
        mx_ref[...] = jnp.zeros(mx_ref.shape, F32)

    def conv_block(cols):
        u = min_ref[:, cols].astype(F32)
        ext_ref[SUBLANES:SUBLANES + tt, cols] = u
        acc = cb_ref[:, cols] + cw_ref[CONV_K - 1:CONV_K, cols] * u
        for jj in range(CONV_K - 1):
            off = SUBLANES - (CONV_K - 1) + jj
            acc = acc + cw_ref[jj:jj + 1, cols] * ext_ref[off:off + tt, cols]
        ext_ref[0:SUBLANES, cols] = u[tt - SUBLANES:tt, :]
        return acc * _sigmoid(acc)

    def q_body(h, carry):
        cols = pl.ds(pl.multiple_of(h * M_DIM, M_DIM), M_DIM)
        q_ref[:, cols] = conv_block(cols).astype(BF16)
        return carry

    def k_body(h, carry):
        off = pl.multiple_of(h * M_DIM, M_DIM)
        act = conv_block(pl.ds(mw + off, M_DIM)) * (M_DIM ** -0.5)
        for j in range(nchunks):
            kt_ref[pl.ds(off, M_DIM), j * CHUNK:(j + 1) * CHUNK] = act[j * CHUNK:(j + 1) * CHUNK, :].T
        return carry

    lax.fori_loop(0, M_HEADS, q_body, 0)
    lax.fori_loop(0, M_HEADS, k_body, 0)

    ri = lax.broadcasted_iota(I32, (CHUNK, CHUNK), 0)
    ci = lax.broadcasted_iota(I32, (CHUNK, CHUNK), 1)
    causal = ci <= ri
    triu = jnp.where(ri <= ci, 1.0, 0.0).astype(BF16)
    ones_half = jnp.ones((CHUNK, M_DIM), BF16)
    mean_mat = jnp.full((M_DIM, M_DIM), 1.0 / M_DIM, BF16)
    sub = lax.broadcasted_iota(I32, (SUBLANES, CHUNK), 0)
    heads = range(M_HEADS)

    pad_rows = jnp.zeros((CHUNK - SUBLANES, CHUNK), F32)
    zero_rows = jnp.zeros((SUBLANES, CHUNK), F32)

    def gate_body(jg, carry):
        for u_ in range(GATE_CHUNKS_PER_STEP):
            rs = pl.ds(pl.multiple_of((jg * GATE_CHUNKS_PER_STEP + u_) * CHUNK, CHUNK), CHUNK)
            gr = gr_ref[:, rs] + bcol_ref[...]
            ls = _log_sigmoid(gr)
            ls1 = ls.astype(BF16).astype(F32)
            ls2 = (ls - ls1).astype(BF16).astype(F32)
            pieces = jnp.concatenate([ls1, ls2, ls - ls1 - ls2, zero_rows], axis=0).astype(BF16)
            sums = _dot(pieces, triu)
            br = sums[0:SUBLANES] + sums[SUBLANES:2 * SUBLANES] + sums[2 * SUBLANES:3 * SUBLANES]
            ab = jnp.where(sub < M_HEADS, gr - pltpu.roll(br, M_HEADS, axis=0), br)
            ab_ref[:, rs] = ab
            bc_ref[rs, :] = jnp.concatenate([ab, pad_rows], axis=0).T
        return carry

    lax.fori_loop(0, nchunks // GATE_CHUNKS_PER_STEP, gate_body, 0)

    def group_body(cg, carry):
        rows, ab = [], []
        for u_ in range(CHUNKS_PER_STEP):
            r0 = pl.multiple_of((cg * CHUNKS_PER_STEP + u_) * CHUNK, CHUNK)
            rows.append(pl.ds(r0, CHUNK))
            ab.append(ab_ref[:, rows[u_]])
        lanes = [(u_, h) for u_ in range(CHUNKS_PER_STEP) for h in heads]
        a_r = {(u_, h): ab[u_][h:h + 1, :] for u_, h in lanes}
        b_last = {(u_, h): ab[u_][M_HEADS + h:M_HEADS + h + 1, CHUNK - 1:CHUNK] for u_, h in lanes}

        m_prev, a_max, a_dec, s_in = {}, {}, {}, {}
        m_run = [mx_ref[h][0:1, 0:1] for h in heads]
        for k in lanes:
            u_, h = k
            m_prev[k] = m_run[h]
            a_max[k] = jnp.max(a_r[k], axis=-1, keepdims=True)
            m_loc = b_last[k] + a_max[k]
            m_new = jnp.maximum(b_last[k] + m_prev[k], m_loc)
            a_dec[k] = jnp.exp(b_last[k] + m_prev[k] - m_new)
            s_in[k] = jnp.exp(m_loc - m_new)
            m_run[h] = m_new
        for h in heads:
            mx_ref[h] = jnp.broadcast_to(m_run[h], (SUBLANES, LANES))

        q, v_ext, s_qk, kv = {}, {}, {}, {}
        for k in lanes:
            u_, h = k
            rs = rows[u_]
            q[k] = q_ref[rs, h * M_DIM:(h + 1) * M_DIM]
            kt = kt_ref[h * M_DIM:(h + 1) * M_DIM, rs]
            v = min_ref[rs, 2 * mw + h * M_DIM:2 * mw + (h + 1) * M_DIM]
            v_ext[k] = jnp.concatenate([v, ones_half], axis=1)
            s_qk[k] = _dot(q[k], kt.astype(BF16))
            e_r = jnp.exp(a_r[k] - a_max[k])
            kv[k] = _dot((kt * e_r).astype(BF16), v_ext[k])

        thr, qk, inter = {}, {}, {}
        for k in lanes:
            u_, h = k
            a_mat = jnp.where(causal, a_r[k], -jnp.inf)
            mu = jnp.maximum(jnp.max(a_mat, axis=-1, keepdims=True), m_prev[k])
            b_c = bc_ref[rows[u_], M_HEADS + h:M_HEADS + h + 1]
            thr[k] = jnp.broadcast_to(jnp.exp(-(b_c + mu)), (CHUNK, M_DIM))
            mu_b = jnp.broadcast_to(mu, (CHUNK, CHUNK))
            inter[k] = jnp.exp(m_prev[k] - mu_b)
            qk[k] = (s_qk[k] * jnp.exp(a_mat - mu_b)).astype(BF16)

        q_state = {}
        state = [st_ref[h] for h in heads]
        for k in lanes:
            u_, h = k
            q_state[k] = _dot(q[k], state[h].astype(BF16))
            state[h] = a_dec[k] * state[h] + s_in[k] * kv[k]
        for h in heads:
            st_ref[h] = state[h]

        for k in lanes:
            u_, h = k
            hs = slice(h * M_DIM, (h + 1) * M_DIM)
            num = _dot(qk[k], v_ext[k])
            den = jnp.maximum(jnp.abs(num[:, M_DIM:] + inter[k] * q_state[k][:, M_DIM:]), thr[k])
            hh = (num[:, :M_DIM] + inter[k] * q_state[k][:, :M_DIM]) / den
            msq = _dot((hh * hh).astype(BF16), mean_mat)
            hn = hh * lax.rsqrt(msq + EPS) * nw_ref[:, hs]
            og = min_ref[rows[u_], 3 * mw + h * M_DIM:3 * mw + (h + 1) * M_DIM].astype(F32)
            hm_ref[rows[u_], hs] = (_sigmoid(og) * hn).astype(BF16)
        return carry

    lax.fori_loop(0, nchunks // CHUNKS_PER_STEP, group_body, 0)


def _mlstm(m_in, grow, conv_w, conv_b, bcol, nw, batch, seq):
    t = m_in.shape[0]
    tt = min(1024, seq)
    tpb = seq // tt
    mw = M_HEADS * M_DIM
    cur = lambda b, i: (b * tpb + i, 0)
    const = lambda b, i: (0, 0)
    return pl.pallas_call(
        _mlstm_kernel,
        grid=(batch, tpb),
        in_specs=[
            pl.BlockSpec((tt, 4 * mw), cur),
            pl.BlockSpec((SUBLANES, tt), lambda b, i: (0, b * tpb + i)),
            pl.BlockSpec((CONV_K, 2 * mw), const),
            pl.BlockSpec((1, 2 * mw), const),
            pl.BlockSpec((SUBLANES, LANES), const),
            pl.BlockSpec((1, mw), const),
        ],
        out_specs=pl.BlockSpec((tt, mw), cur),
        out_shape=jax.ShapeDtypeStruct((t, mw), BF16),
        scratch_shapes=[
            pltpu.VMEM((tt + SUBLANES, 2 * mw), F32),
            pltpu.VMEM((tt, mw), BF16),
            pltpu.VMEM((mw, tt), F32),
            pltpu.VMEM((M_HEADS, M_DIM, 2 * M_DIM), F32),
            pltpu.VMEM((M_HEADS, SUBLANES, LANES), F32),
            pltpu.VMEM((SUBLANES, tt), F32),
            pltpu.VMEM((tt, LANES), F32),
        ],
        compiler_params=_params("arbitrary", "arbitrary"),
        name="mlstm",
    )(m_in, grow, conv_w, conv_b, bcol, nw)


def _merge_kernel(o_ref, hm_ref, mg_ref, x_ref, g1_ref, sc_ref, sh_ref, nw_ref,
                  wa_ref, wm_ref, wo_ref, rwh_ref, rwl_ref, rb_ref, tri_ref,
                  x1_ref, pay_ref, route_ref, cnt_ref, carry_ref):
    tm, d = x_ref.shape

    @pl.when(pl.program_id(0) == 0)
    def _():
        carry_ref[...] = jnp.zeros(carry_ref.shape, F32)

    ya = _dot(o_ref[...], wa_ref[...])
    yb = _dot(hm_ref[...], wm_ref[...])
    mg = mg_ref[...]
    merged = mg[:, :d].astype(F32) * ya + mg[:, d:].astype(F32) * yb
    x1 = x_ref[...] + g1_ref[...] * _dot(merged.astype(BF16), wo_ref[...])
    x1_ref[...] = x1

    ms = jnp.mean(x1 * x1, axis=-1, keepdims=True)
    h2 = x1 * lax.rsqrt(ms + EPS) * nw_ref[...]
    h2 = h2 * (1.0 + sc_ref[...]) + sh_ref[...]
    hi = h2.astype(BF16)
    hif = hi.astype(F32)
    lo = (h2 - hif).astype(BF16)
    r_hi = _dot_nt(rwh_ref[...], hi)
    r_lo = _dot_nt(rwl_ref[...], lo)
    sc_t = _sigmoid(r_hi[:N_EXPERTS] + r_hi[N_EXPERTS:] + r_lo)
    sel_t = sc_t + rb_ref[:, 0:1]

    def row(a, e):
        return a[e:e + 1, :]

    best = None
    gi = jnp.zeros((1, tm), I32)
    for g in range(N_GROUPS):
        r = [row(sel_t, EPG * g + i) for i in range(EPG)]
        gs = None
        for i in range(EPG):
            for j in range(i + 1, EPG):
                pr = r[i] + r[j]
                gs = pr if gs is None else jnp.maximum(gs, pr)
        if best is None:
            best = gs
        else:
            upd = gs > best
            gi = jnp.where(upd, g, gi)
            best = jnp.maximum(best, gs)

    def pick(a, i):
        out = row(a, i)
        for g in range(1, N_GROUPS):
            out = jnp.where(gi == g, row(a, EPG * g + i), out)
        return out

    v = [pick(sel_t, i) for i in range(EPG)]
    s = [pick(sc_t, i) for i in range(EPG)]

    def argmax4(vals):
        bv, bi = vals[0], jnp.zeros((1, tm), I32)
        for i in range(1, EPG):
            upd = vals[i] > bv
            bi = jnp.where(upd, i, bi)
            bv = jnp.maximum(bv, vals[i])
        return bi

    i1 = argmax4(v)
    i2 = argmax4([jnp.where(i1 == i, -jnp.inf, v[i]) for i in range(EPG)])
    ia = jnp.minimum(i1, i2)
    ib = jnp.maximum(i1, i2)
    pidx = jnp.where(ia == 0, ib - 1, jnp.where(ia == 1, ib + 1, N_PAIRS - 1))
    bucket = gi * N_PAIRS + pidx

    def by_index(vals, idx):
        out = vals[0]
        for i in range(1, EPG):
            out = jnp.where(idx == i, vals[i], out)
        return out

    s_a, s_b = by_index(s, ia), by_index(s, ib)
    gate_a = s_a / (s_a + s_b)
    gate_b = s_b / (s_a + s_b)

    brow = lax.broadcasted_iota(I32, (BUCKET_ROWS, tm), 0)
    onehot = brow == bucket
    cums = _dot(jnp.where(onehot, 1.0, 0.0).astype(BF16), tri_ref[...])
    carry = carry_ref[...]
    rank = jnp.sum(jnp.where(onehot, carry[:, 0:1] + cums, 0.0), axis=0, keepdims=True) - 1.0
    new_carry = carry + cums[:, tm - 1:tm]
    carry_ref[...] = new_carry
    cnt_ref[...] = new_carry

    route_ref[...] = jnp.concatenate(
        [bucket.astype(F32), gate_a, gate_b, rank, jnp.zeros((SUBLANES - 4, tm), F32)], axis=0)

    half = d // 2
    packed = _pack_bf16_pairs(hif)
    for cpart in range(half // LANES):
        pay_ref[cpart] = packed[:, cpart * LANES:(cpart + 1) * LANES]
    gates_t = jnp.concatenate([gate_a, gate_b, jnp.zeros((LANES - 2, tm), F32)], axis=0)
    pay_ref[half // LANES] = lax.bitcast_convert_type(gates_t.T, U32)


def _merge(o_attn, hm, mg, x, g1, sc2, sh2, nw, wa, wm, wo, rwh, rwl, rb, tri, l, seq):
    t, d = x.shape
    tm = tri.shape[0]
    tpb = seq // tm
    row = lambda i: (i, 0)
    bsel = lambda i: (i // tpb, 0, 0)
    wsel = lambda i: (l, 0, 0)
    const = lambda i: (0, 0)
    hw = o_attn.shape[1]
    return pl.pallas_call(
        _merge_kernel,
        grid=(t // tm,),
        in_specs=[
            pl.BlockSpec((tm, hw), row),
            pl.BlockSpec((tm, hw), row),
            pl.BlockSpec((tm, 2 * d), row),
            pl.BlockSpec((tm, d), row),
            pl.BlockSpec((None, 1, d), bsel),
            pl.BlockSpec((None, 1, d), bsel),
            pl.BlockSpec((None, 1, d), bsel),
            pl.BlockSpec((None, 1, d), wsel),
            pl.BlockSpec((None, hw, d), wsel),
            pl.BlockSpec((None, hw, d), wsel),
            pl.BlockSpec((None, d, d), wsel),
            pl.BlockSpec((2 * N_EXPERTS, d), const),
            pl.BlockSpec((N_EXPERTS, d), const),
            pl.BlockSpec((N_EXPERTS, LANES), const),
            pl.BlockSpec((tm, tm), const),
        ],
        out_specs=[
            pl.BlockSpec((tm, d), row),
            pl.BlockSpec((PAY_PARTS, tm, LANES), lambda i: (0, i, 0)),
            pl.BlockSpec((SUBLANES, tm), lambda i: (0, i)),
            pl.BlockSpec((BUCKET_ROWS, LANES), const),
        ],
        out_shape=[
            jax.ShapeDtypeStruct((t, d), F32),
            jax.ShapeDtypeStruct((PAY_PARTS, t, LANES), U32),
            jax.ShapeDtypeStruct((SUBLANES, t), F32),
            jax.ShapeDtypeStruct((BUCKET_ROWS, LANES), F32),
        ],
        scratch_shapes=[pltpu.VMEM((BUCKET_ROWS, LANES), F32)],
        compiler_params=_params("arbitrary"),
        name="merge_router",
    )(o_attn, hm, mg, x, g1, sc2, sh2, nw, wa, wm, wo, rwh, rwl, rb, tri)


def _sc_mesh():
    return plsc.VectorSubcoreMesh(core_axis_name="core", subcore_axis_name="subcore")


def _sc_scatter_rows(rows, dest, n_out):
    n, w = rows.shape

    @pl.kernel(out_type=jax.ShapeDtypeStruct((n_out, w), rows.dtype), mesh=_sc_mesh(), scratch_types=[])
    def scatter(x_hbm, i_hbm, o_hbm):
        def body(x_vmem, i_vmem):
            pltpu.sync_copy(x_vmem, o_hbm.at[i_vmem.at[0]])

        pltpu.emit_pipeline(
            body,
            grid=(n // SC_WINDOW,),
            in_specs=[pl.BlockSpec((SC_WINDOW, w), lambda i: (i, 0)),
                      pl.BlockSpec((1, SC_WINDOW), lambda i: (0, i))],
            out_specs=[],
            core_axis_name=("core", "subcore"),
            dimension_semantics=(pltpu.PARALLEL,),
        )(x_hbm, i_hbm)

    return scatter(rows, dest.reshape(1, n))


def _sc_gather_rows(src, idx):
    n = idx.shape[0]
    w = src.shape[1]

    @pl.kernel(out_type=jax.ShapeDtypeStruct((n, w), src.dtype), mesh=_sc_mesh(), scratch_types=[])
    def gather(x_hbm, i_hbm, o_hbm):
        def body(i_vmem, o_vmem):
            pltpu.sync_copy(x_hbm.at[i_vmem.at[0]], o_vmem)

        pltpu.emit_pipeline(
            body,
            grid=(n // SC_WINDOW,),
            in_specs=[pl.BlockSpec((1, SC_WINDOW), lambda i: (0, i))],
            out_specs=[pl.BlockSpec((SC_WINDOW, w), lambda i: (i, 0))],
            core_axis_name=("core", "subcore"),
            dimension_semantics=(pltpu.PARALLEL,),
        )(i_hbm, o_hbm)

    return gather(src, idx.reshape(1, n))


def _part_index(dest, parts, n_rows):
    return (jnp.arange(parts, dtype=I32)[:, None] * n_rows + dest[None, :]).reshape(-1)


def _residual_kernel(x_ref, y_ref, g2_ref, o_ref):
    y = _unpack_bf16_pairs(jnp.concatenate([y_ref[c] for c in range(OUT_PARTS)], axis=1))
    o_ref[...] = x_ref[...] + g2_ref[...] * y


def _residual(x1, ytok, g2, seq):
    t, d = x1.shape
    tm = min(512, seq)
    tpb = seq // tm
    return pl.pallas_call(
        _residual_kernel,
        grid=(t // tm,),
        in_specs=[
            pl.BlockSpec((tm, d), lambda i: (i, 0)),
            pl.BlockSpec((OUT_PARTS, tm, LANES), lambda i: (0, i, 0)),
            pl.BlockSpec((None, 1, d), lambda i: (i // tpb, 0, 0)),
        ],
        out_specs=pl.BlockSpec((tm, d), lambda i: (i, 0)),
        out_shape=jax.ShapeDtypeStruct((t, d), F32),
        compiler_params=_params("arbitrary"),
        name="residual",
    )(x1, ytok, g2)


def _expert_kernel(ea_ref, eb_ref, nr_ref, xs_ref, wgua_ref, wda_ref, wgub_ref, wdb_ref, ys_ref):
    j = pl.program_id(0)
    nr = nr_ref[0]

    @pl.when(j < nr)
    def _():
        x = _unpack_bf16_pairs(jnp.concatenate([xs_ref[c] for c in range(PAY_PARTS - 1)], axis=1)).astype(BF16)
        gl = lax.bitcast_convert_type(xs_ref[PAY_PARTS - 1], F32)

        def ffn(wgu_ref, wd_ref):
            gu = _dot(x, wgu_ref[...])
            f = gu.shape[1] // 2
            gte, up = gu[:, :f], gu[:, f:]
            act = gte * _sigmoid(gte) * up
            return _dot(act.astype(BF16), wd_ref[...])

        y = _pack_bf16_pairs(gl[:, 0:1] * ffn(wgua_ref, wda_ref) + gl[:, 1:2] * ffn(wgub_ref, wdb_ref))
        for c in range(OUT_PARTS):
            ys_ref[c] = y[:, c * LANES:(c + 1) * LANES]

    @pl.when(j >= nr)
    def _():
        ys_ref[...] = jnp.zeros(ys_ref.shape, U32)


def _experts(blk_ea, blk_eb, n_real, xs, wgu, wd, d):
    n_rows = xs.shape[1]
    nblk = n_rows // EXPERT_BLOCK
    f2 = wgu.shape[2]
    grid_spec = pltpu.PrefetchScalarGridSpec(
        num_scalar_prefetch=3,
        grid=(nblk,),
        in_specs=[
            pl.BlockSpec((PAY_PARTS, EXPERT_BLOCK, LANES), lambda j, ea, eb, nr: (0, j, 0)),
            pl.BlockSpec((None, d, f2), lambda j, ea, eb, nr: (ea[j], 0, 0)),
            pl.BlockSpec((None, f2 // 2, d), lambda j, ea, eb, nr: (ea[j], 0, 0)),
            pl.BlockSpec((None, d, f2), lambda j, ea, eb, nr: (eb[j], 0, 0)),
            pl.BlockSpec((None, f2 // 2, d), lambda j, ea, eb, nr: (eb[j], 0, 0)),
        ],
        out_specs=pl.BlockSpec((OUT_PARTS, EXPERT_BLOCK, LANES), lambda j, ea, eb, nr: (0, j, 0)),
    )
    return pl.pallas_call(
        _expert_kernel,
        grid_spec=grid_spec,
        out_shape=jax.ShapeDtypeStruct((OUT_PARTS, n_rows, LANES), U32),
        compiler_params=_params("arbitrary"),
        name="experts",
    )(blk_ea, blk_eb, n_real, xs, wgu, wd, wgu, wd)


_PAIR_A = (0, 0, 0, 1, 1, 2)
_PAIR_B = (1, 2, 3, 2, 3, 3)


def kernel(x, c, positions, ada_w, ada_b, norm_mix_w, norm_ffn_w, w_in, b_igate, b_fgate, q_norm_w, k_norm_w,
           sinks, conv_w, conv_b, mlstm_norm_w, w_attn_up, w_mlstm_up, w_out, router_w, router_bias,
           w_gate, w_up, w_down):
    batch, seq, d = x.shape
    depth = w_in.shape[0]
    t = batch * seq
    qw = N_HEADS * HEAD_DIM
    kvw = N_KV * HEAD_DIM
    mw = M_HEADS * M_DIM

    o = 0
    cols = {}
    for name, wdt in (("q", qw), ("k", kvw), ("v", kvw), ("mqk", 2 * mw), ("mv", mw), ("mi", M_HEADS),
                      ("mf", M_HEADS), ("mo", mw), ("ga", d), ("gb", d)):
        cols[name] = (o, o + wdt)
        o += wdt

    def wc(name, lo=0, hi=None):
        s, e = cols[name]
        return w_in[:, :, s + lo:(s + hi if hi is not None else e)]

    w_a = jnp.concatenate([wc("q"), wc("k"), wc("v")], axis=2).astype(BF16)
    w_m = jnp.concatenate([wc("mqk"), wc("mv"), wc("mo")], axis=2).astype(BF16)
    w_g = jnp.concatenate([wc("mi"), wc("mf"), jnp.zeros((depth, d, LANES - 2 * M_HEADS), F32)], axis=2).astype(BF16)
    w_mg = jnp.concatenate([wc("ga"), wc("gb")], axis=2).astype(BF16)
    w_au = w_attn_up.astype(BF16)
    w_mu = w_mlstm_up.astype(BF16)
    w_o = w_out.astype(BF16)
    n_e = w_gate.shape[1]
    w_gu = jnp.concatenate([w_gate, w_up], axis=3).astype(BF16).reshape(depth * n_e, d, -1)
    w_d = w_down.astype(BF16).reshape(depth * n_e, -1, d)

    rw_t = router_w.astype(F32).T
    rw_top = rw_t.astype(BF16)
    rw_hi = jnp.concatenate([rw_top, (rw_t - rw_top.astype(F32)).astype(BF16)], axis=0)
    rw_lo = rw_top
    rb = jnp.broadcast_to(router_bias.astype(F32)[:, None], (n_e, LANES))

    qn_w = jnp.tile(q_norm_w, (1, N_HEADS)).reshape(depth, 1, qw)
    kn_w = jnp.tile(k_norm_w, (1, N_KV)).reshape(depth, 1, kvw)
    seg = jnp.arange(qw) // HEAD_DIM
    bdq = jnp.where(seg[:, None] == seg[None, :], 1.0 / HEAD_DIM, 0.0).astype(BF16)
    bdk = bdq[:kvw, :kvw]

    inv_freq = ROPE_THETA ** (-(jnp.arange(0, ROPE_DIM, 2, dtype=F32) / ROPE_DIM))
    ang = positions.astype(F32).reshape(1, t) * inv_freq[:, None]
    cos8, sin8 = jnp.cos(ang).T, jnp.sin(ang).T
    pad1 = jnp.ones((t, HEAD_DIM - ROPE_DIM), F32)
    pad0 = jnp.zeros((t, HEAD_DIM - ROPE_DIM), F32)
    cos_t = jnp.tile(jnp.concatenate([cos8, cos8, pad1], axis=1), (1, LANES // HEAD_DIM))
    sin_t = jnp.tile(jnp.concatenate([-sin8, sin8, pad0], axis=1), (1, LANES // HEAD_DIM))

    gate_bias = jnp.concatenate([b_igate, b_fgate], axis=1).astype(F32)
    bcol = jnp.broadcast_to(gate_bias[:, :, None], (depth, 2 * M_HEADS, LANES))

    tm_merge = min(512, seq)
    ii = jnp.arange(tm_merge)
    tri = (ii[:, None] <= ii[None, :]).astype(BF16)

    n_blk = (t + N_BUCKETS * (EXPERT_BLOCK - 1)) // EXPERT_BLOCK + 1
    n_rows = n_blk * EXPERT_BLOCK
    pair_a = jnp.asarray(_PAIR_A, I32)
    pair_b = jnp.asarray(_PAIR_B, I32)

    c_pad = jnp.zeros((SUBLANES, d), F32).at[:batch].set(c)
    mod = _ada_mod(c_pad, ada_w, ada_b)[:, :batch]

    xf = x.reshape(t, d)
    moe = None
    for l in range(depth):
        sh1, sc1, g1, sh2, sc2, g2 = [m.reshape(batch, 1, d) for m in jnp.split(mod[l], 6, axis=-1)]

        outs = _inproj(xf, moe, sc1, sh1, norm_mix_w.reshape(depth, 1, d), w_a, w_m, w_g, w_mg, l, seq)
        a_in, m_in, mg, grow = outs[:4]
        if moe is not None:
            xf = outs[4]
        o_attn = _attention(a_in, cos_t, sin_t, sinks[l], qn_w[l], kn_w[l], bdq, bdk, batch, seq)
        hm = _mlstm(m_in, grow, conv_w[l], conv_b[l].reshape(1, -1), bcol[l], mlstm_norm_w[l].reshape(1, mw),
                    batch, seq)
        x1, pay, route, cnt = _merge(o_attn, hm, mg, xf, g1, sc2, sh2, norm_ffn_w.reshape(depth, 1, d),
                                     w_au, w_mu, w_o, rw_hi, rw_lo, rb, tri, l, seq)

        counts = cnt[:N_BUCKETS, 0].astype(I32)
        padded = (counts + EXPERT_BLOCK - 1) // EXPERT_BLOCK * EXPERT_BLOCK
        pad_ends = jnp.cumsum(padded)
        pad_starts = pad_ends - padded
        bucket = route[0].astype(I32)
        dest = pad_starts[bucket] + route[3].astype(I32)
        blk_start = jnp.arange(n_blk, dtype=I32) * EXPERT_BLOCK
        blk_bucket = jnp.minimum(jnp.sum((pad_ends[None, :] <= blk_start[:, None]).astype(I32), axis=1), N_BUCKETS - 1)
        grp = blk_bucket // N_PAIRS
        blk_ea = (l * n_e + grp * EPG + pair_a[blk_bucket % N_PAIRS]).astype(I32)
        blk_eb = (l * n_e + grp * EPG + pair_b[blk_bucket % N_PAIRS]).astype(I32)
        n_real = (pad_ends[-1:] // EXPERT_BLOCK).astype(I32)

        xs = _sc_scatter_rows(pay.reshape(PAY_PARTS * t, LANES), _part_index(dest, PAY_PARTS, n_rows),
                              PAY_PARTS * n_rows).reshape(PAY_PARTS, n_rows, LANES)
        ys = _experts(blk_ea, blk_eb, n_real, xs, w_gu, w_d, d)
        ytok = _sc_gather_rows(ys.reshape(OUT_PARTS * n_rows, LANES),
                               _part_index(dest, OUT_PARTS, n_rows)).reshape(OUT_PARTS, t, LANES)
        xf, moe = x1, (ytok, g2)
    return _residual(xf, moe[0], moe[1], seq).reshape(batch, seq, d)
```

```python
import functools

import jax
import jax.numpy as jnp
from jax import lax
from jax.experimental import pallas as pl
from jax.experimental.pallas import tpu as pltpu
from jax.experimental.pallas import tpu_sc as plsc

F32 = jnp.float32
BF16 = jnp.bfloat16
U32 = jnp.uint32
I32 = jnp.int32
HIGHEST = lax.Precision.HIGHEST

HEAD_DIM = 64
N_HEADS = 8
N_KV = 2
ROPE_DIM = 16
ROPE_THETA = 500000.0
ATTN_BLOCK = 128
M_HEADS = 4
M_DIM = 128
CONV_K = 4
N_EXPERTS = 16
N_GROUPS = 4
EPG = 4
EPS = 1e-6

LANES = 128
SUBLANES = 8

CHUNK = 128
CHUNKS_PER_STEP = 4
GATE_CHUNKS_PER_STEP = 4
N_PAIRS = 6
N_BUCKETS = N_GROUPS * N_PAIRS
BUCKET_ROWS = 32
EXPERT_BLOCK = 256
PAY_PARTS = 5
OUT_PARTS = 4
SC_WINDOW = 128
VMEM_LIMIT = 56 * 1024 * 1024


def _dot(a, b, precision=None):
    return jnp.dot(a, b, preferred_element_type=F32, precision=precision)


def _dot_nt(a, b):
    return lax.dot_general(a, b, (((1,), (1,)), ((), ())), preferred_element_type=F32)


def _sigmoid(x):
    return 1.0 / (1.0 + jnp.exp(-x))


def _log_sigmoid(x):
    return jnp.minimum(x, 0.0) - jnp.log1p(jnp.exp(-jnp.abs(x)))


def _pack_bf16_pairs(v):
    n = v.shape[1] // 2
    bits = lax.bitcast_convert_type(v.astype(BF16).astype(F32), U32)
    return bits[:, :n] | (bits[:, n:] >> 16)


def _unpack_bf16_pairs(w):
    hi = lax.bitcast_convert_type(w & jnp.uint32(0xFFFF0000), F32)
    lo = lax.bitcast_convert_type(w << 16, F32)
    return jnp.concatenate([hi, lo], axis=1)


def _params(*sem):
    return pltpu.CompilerParams(dimension_semantics=sem, vmem_limit_bytes=VMEM_LIMIT)


def _ada_kernel(c_ref, w_ref, b_ref, o_ref):
    c = c_ref[...]
    ca = c * _sigmoid(c)
    o_ref[0] = _dot(ca, w_ref[0], HIGHEST) + b_ref[0]


def _ada_mod(c_pad, ada_w, ada_b):
    depth, d, n = ada_w.shape
    tn = 1536
    return pl.pallas_call(
        _ada_kernel,
        grid=(depth, n // tn),
        in_specs=[
            pl.BlockSpec((SUBLANES, d), lambda l, j: (0, 0)),
            pl.BlockSpec((1, d, tn), lambda l, j: (l, 0, j)),
            pl.BlockSpec((1, 1, tn), lambda l, j: (l, 0, j)),
        ],
        out_specs=pl.BlockSpec((1, SUBLANES, tn), lambda l, j: (l, 0, j)),
        out_shape=jax.ShapeDtypeStruct((depth, SUBLANES, n), F32),
        compiler_params=_params("arbitrary", "arbitrary"),
        name="ada_mod",
    )(c_pad, ada_w, ada_b.reshape(depth, 1, n))


def _inproj_kernel(*refs, fuse_residual):
    if fuse_residual:
        (x_ref, y_ref, g2_ref, sc_ref, sh_ref, nw_ref, wa_ref, wm_ref, wg_ref, wmg_ref,
         a_ref, m_ref, mg_ref, gr_ref, xo_ref, g_ref) = refs
        y = _unpack_bf16_pairs(jnp.concatenate([y_ref[c] for c in range(OUT_PARTS)], axis=1))
        x = x_ref[...] + g2_ref[...] * y
        xo_ref[...] = x
    else:
        (x_ref, sc_ref, sh_ref, nw_ref, wa_ref, wm_ref, wg_ref, wmg_ref,
         a_ref, m_ref, mg_ref, gr_ref, g_ref) = refs
        x = x_ref[...]
    ms = jnp.mean(x * x, axis=-1, keepdims=True)
    h = x * lax.rsqrt(ms + EPS) * nw_ref[...]
    h = h * (1.0 + sc_ref[...]) + sh_ref[...]
    hb = h.astype(BF16)
    a_ref[...] = _dot(hb, wa_ref[...]).astype(BF16)
    m_ref[...] = _dot(hb, wm_ref[...]).astype(BF16)
    mg_ref[...] = _sigmoid(_dot(hb, wmg_ref[...])).astype(BF16)
    g_ref[...] = _dot(hb, wg_ref[...])
    gr_ref[...] = g_ref[...].T[:SUBLANES, :]


def _inproj(x, moe, sc, sh, nw, wa, wm, wg, wmg, l, seq):
    t, d = x.shape
    tm = min(512, seq)
    tpb = seq // tm
    row = lambda i: (i, 0)
    bsel = lambda i: (i // tpb, 0, 0)
    wsel = lambda i: (l, 0, 0)
    na, nm, ng, nmg = wa.shape[2], wm.shape[2], wg.shape[2], wmg.shape[2]
    fuse = moe is not None
    moe_specs = [pl.BlockSpec((OUT_PARTS, tm, LANES), lambda i: (0, i, 0)), pl.BlockSpec((None, 1, d), bsel)]
    return pl.pallas_call(
        functools.partial(_inproj_kernel, fuse_residual=fuse),
        grid=(t // tm,),
        in_specs=[pl.BlockSpec((tm, d), row)] + (moe_specs if fuse else []) + [
            pl.BlockSpec((None, 1, d), bsel),
            pl.BlockSpec((None, 1, d), bsel),
            pl.BlockSpec((None, 1, d), wsel),
            pl.BlockSpec((None, d, na), wsel),
            pl.BlockSpec((None, d, nm), wsel),
            pl.BlockSpec((None, d, ng), wsel),
            pl.BlockSpec((None, d, nmg), wsel),
        ],
        out_specs=[
            pl.BlockSpec((tm, na), row),
            pl.BlockSpec((tm, nm), row),
            pl.BlockSpec((tm, nmg), row),
            pl.BlockSpec((SUBLANES, tm), lambda i: (0, i)),
        ] + ([pl.BlockSpec((tm, d), row)] if fuse else []),
        out_shape=[
            jax.ShapeDtypeStruct((t, na), BF16),
            jax.ShapeDtypeStruct((t, nm), BF16),
            jax.ShapeDtypeStruct((t, nmg), BF16),
            jax.ShapeDtypeStruct((SUBLANES, t), F32),
        ] + ([jax.ShapeDtypeStruct((t, d), F32)] if fuse else []),
        scratch_shapes=[pltpu.VMEM((tm, ng), F32)],
        compiler_params=_params("arbitrary"),
        name="inproj",
    )(x, *(moe if fuse else ()), sc, sh, nw, wa, wm, wg, wmg)


def _rope(t, cos, sin):
    w = t.shape[1]
    reps = w // LANES
    cosw = jnp.concatenate([cos] * reps, axis=1) if reps > 1 else cos
    sinw = jnp.concatenate([sin] * reps, axis=1) if reps > 1 else sin
    lane = lax.broadcasted_iota(I32, t.shape, 1)
    half = ROPE_DIM // 2
    up = pltpu.roll(t, w - half, axis=1)
    dn = pltpu.roll(t, half, axis=1)
    partner = jnp.where((lane % ROPE_DIM) < half, up, dn)
    return t * cosw + partner * sinw


def _head_norm(t, bd, w):
    ms = _dot((t * t).astype(BF16), bd)
    return t * lax.rsqrt(ms + EPS) * w


def _attn_kernel(sink_ref, cur_ref, prev_ref, cos_ref, sin_ref, cosp_ref, sinp_ref,
                 qw_ref, kw_ref, bdq_ref, bdk_ref, o_ref):
    tq = cur_ref.shape[0]
    nj = tq // ATTN_BLOCK
    qw = N_HEADS * HEAD_DIM
    kw = N_KV * HEAD_DIM
    blk0 = pl.program_id(1) * nj

    cur = cur_ref[...]
    q = cur[:, :qw].astype(F32)
    kc = cur[:, qw:qw + kw].astype(F32)
    vc = cur[:, qw + kw:].astype(F32)
    prev = prev_ref[...]
    kp = prev[:, :kw].astype(F32)
    vp = prev[:, kw:].astype(F32)

    cos, sin = cos_ref[...], sin_ref[...]
    q = _rope(_head_norm(q, bdq_ref[...], qw_ref[...]), cos, sin) * (HEAD_DIM ** -0.5)
    kc = _rope(_head_norm(kc, bdk_ref[...], kw_ref[...]), cos, sin)
    kp = _rope(_head_norm(kp, bdk_ref[...], kw_ref[...]), cosp_ref[...], sinp_ref[...])
    qb = q.astype(BF16)

    def both_halves(x2):
        swapped = pltpu.roll(x2, HEAD_DIM, axis=1)
        first = lax.broadcasted_iota(I32, x2.shape, 1) < HEAD_DIM
        return jnp.concatenate([jnp.where(first, x2, swapped), jnp.where(first, swapped, x2)], axis=1).astype(BF16)

    k_all = both_halves(jnp.concatenate([kp, kc], axis=0))
    v_all = both_halves(jnp.concatenate([vp, vc], axis=0))

    lane = lax.broadcasted_iota(I32, (ATTN_BLOCK, LANES), 1)
    lo = lane < HEAD_DIM
    zero = jnp.zeros((ATTN_BLOCK, LANES), BF16)
    g_heads = N_HEADS // N_KV
    ri = lax.broadcasted_iota(I32, (g_heads * ATTN_BLOCK, ATTN_BLOCK), 0) % ATTN_BLOCK
    ci = lax.broadcasted_iota(I32, (g_heads * ATTN_BLOCK, ATTN_BLOCK), 1)
    from_prev = ci > ri
    head_row = lax.broadcasted_iota(I32, (g_heads * ATTN_BLOCK, 1), 0) // ATTN_BLOCK
    ones_v = jnp.ones((2 * ATTN_BLOCK, LANES), BF16)

    tiles = [(j, g) for j in range(nj) for g in range(N_KV)]
    scores = {}
    for j, g in tiles:
        rows = slice(j * ATTN_BLOCK, (j + 1) * ATTN_BLOCK)
        band = slice(j * ATTN_BLOCK, (j + 2) * ATTN_BLOCK)
        qp0 = qb[rows, (2 * g) * LANES:(2 * g + 1) * LANES]
        qp1 = qb[rows, (2 * g + 1) * LANES:(2 * g + 2) * LANES]
        q4 = jnp.concatenate([jnp.where(lo, qp0, zero), jnp.where(lo, zero, qp0),
                              jnp.where(lo, qp1, zero), jnp.where(lo, zero, qp1)], axis=0)
        scores[j, g] = _dot_nt(q4, k_all[band, g * LANES:(g + 1) * LANES])

    probs, sink_term = {}, {}
    for j, g in tiles:
        prev_ok = ci > ri + (1 - jnp.minimum(blk0 + j, 1)) * ATTN_BLOCK
        s2 = scores[j, g]
        s = jnp.where(prev_ok, s2[:, :ATTN_BLOCK], jnp.where(from_prev, -jnp.inf, s2[:, ATTN_BLOCK:]))
        sink = jnp.full((g_heads * ATTN_BLOCK, 1), sink_ref[g_heads * g], F32)
        for r in range(1, g_heads):
            sink = jnp.where(head_row == r, sink_ref[g_heads * g + r], sink)
        m = jnp.maximum(jnp.max(s, axis=-1, keepdims=True), sink)
        p = jnp.exp(s - m)
        probs[j, g] = jnp.concatenate([jnp.where(from_prev, p, 0.0), jnp.where(from_prev, 0.0, p)],
                                      axis=1).astype(BF16)
        sink_term[j, g] = jnp.exp(sink - m)

    for j, g in tiles:
        rows = slice(j * ATTN_BLOCK, (j + 1) * ATTN_BLOCK)
        band = slice(j * ATTN_BLOCK, (j + 2) * ATTN_BLOCK)
        o8 = _dot(probs[j, g], jnp.concatenate([v_all[band, g * LANES:(g + 1) * LANES], ones_v], axis=1))
        o4 = o8[:, :LANES] / (o8[:, LANES:] + sink_term[j, g])
        b = ATTN_BLOCK
        o_ref[rows, (2 * g) * LANES:(2 * g + 1) * LANES] = jnp.where(lo, o4[0:b], o4[b:2 * b]).astype(BF16)
        o_ref[rows, (2 * g + 1) * LANES:(2 * g + 2) * LANES] = jnp.where(
            lo, o4[2 * b:3 * b], o4[3 * b:4 * b]).astype(BF16)


def _attention(a_in, cos_t, sin_t, sinks_l, qw, kw, bdq, bdk, batch, seq):
    t = a_in.shape[0]
    tq = min(512, seq)
    nj = tq // ATTN_BLOCK
    tpb = seq // tq
    bpb = seq // ATTN_BLOCK
    qwid = N_HEADS * HEAD_DIM
    kvw = 2 * N_KV * HEAD_DIM
    cur = lambda b, i: (b * tpb + i, 0)
    prv = lambda b, i: (b * bpb + jnp.maximum(i * nj - 1, 0), qwid // kvw)
    prv0 = lambda b, i: (b * bpb + jnp.maximum(i * nj - 1, 0), 0)
    const = lambda b, i: (0, 0)
    return pl.pallas_call(
        _attn_kernel,
        grid=(batch, tpb),
        in_specs=[
            pl.BlockSpec(memory_space=pltpu.SMEM),
            pl.BlockSpec((tq, qwid + kvw), cur),
            pl.BlockSpec((ATTN_BLOCK, kvw), prv),
            pl.BlockSpec((tq, LANES), cur),
            pl.BlockSpec((tq, LANES), cur),
            pl.BlockSpec((ATTN_BLOCK, LANES), prv0),
            pl.BlockSpec((ATTN_BLOCK, LANES), prv0),
            pl.BlockSpec((1, qwid), const),
            pl.BlockSpec((1, kvw // 2), const),
            pl.BlockSpec((qwid, qwid), const),
            pl.BlockSpec((kvw // 2, kvw // 2), const),
        ],
        out_specs=pl.BlockSpec((tq, qwid), cur),
        out_shape=jax.ShapeDtypeStruct((t, qwid), BF16),
        compiler_params=_params("arbitrary", "arbitrary"),
        name="swa_attention",
    )(sinks_l, a_in, a_in, cos_t, sin_t, cos_t, sin_t, qw, kw, bdq, bdk)


def _mlstm_kernel(min_ref, gr_ref, cw_ref, cb_ref, bcol_ref, nw_ref,
                  hm_ref, ext_ref, q_ref, kt_ref, st_ref, mx_ref, ab_ref, bc_ref):
    tt = min_ref.shape[0]
    mw = M_HEADS * M_DIM
    nchunks = tt // CHUNK

    @pl.when(pl.program_id(1) == 0)
    def _():
        ext_ref[0:SUBLANES, :] = jnp.zeros((SUBLANES, 2 * mw), F32)
        st_ref[...] = jnp.zeros(st_ref.shape, F32)
        mx_ref[...] = jnp.zeros(mx_ref.shape, F32)

    def conv_block(cols):
        u = min_ref[:, cols].astype(F32)
        ext_ref[SUBLANES:SUBLANES + tt, cols] = u
        acc = cb_ref[:, cols] + cw_ref[CONV_K - 1:CONV_K, cols] * u
        for jj in range(CONV_K - 1):
            off = SUBLANES - (CONV_K - 1) + jj
            acc = acc + cw_ref[jj:jj + 1, cols] * ext_ref[off:off + tt, cols]
        ext_ref[0:SUBLANES, cols] = u[tt - SUBLANES:tt, :]
        return acc * _sigmoid(acc)

    def q_body(h, carry):
        cols = pl.ds(pl.multiple_of(h * M_DIM, M_DIM), M_DIM)
        q_ref[:, cols] = conv_block(cols).astype(BF16)
        return carry

    def k_body(h, carry):
        off = pl.multiple_of(h * M_DIM, M_DIM)
        act = conv_block(pl.ds(mw + off, M_DIM)) * (M_DIM ** -0.5)
        for j in range(nchunks):
            kt_ref[pl.ds(off, M_DIM), j * CHUNK:(j + 1) * CHUNK] = act[j * CHUNK:(j + 1) * CHUNK, :].T
        return carry

    lax.fori_loop(0, M_HEADS, q_body, 0)
    lax.fori_loop(0, M_HEADS, k_body, 0)

    ri = lax.broadcasted_iota(I32, (CHUNK, CHUNK), 0)
    ci = lax.broadcasted_iota(I32, (CHUNK, CHUNK), 1)
    causal = ci <= ri
    triu = jnp.where(ri <= ci, 1.0, 0.0).astype(BF16)
    ones_half = jnp.ones((CHUNK, M_DIM), BF16)
    mean_mat = jnp.full((M_DIM, M_DIM), 1.0 / M_DIM, BF16)
    sub = lax.broadcasted_iota(I32, (SUBLANES, CHUNK), 0)
    heads = range(M_HEADS)

    pad_rows = jnp.zeros((CHUNK - SUBLANES, CHUNK), F32)
    zero_rows = jnp.zeros((SUBLANES, CHUNK), F32)

    def gate_body(jg, carry):
        for u_ in range(GATE_CHUNKS_PER_STEP):
            rs = pl.ds(pl.multiple_of((jg * GATE_CHUNKS_PER_STEP + u_) * CHUNK, CHUNK), CHUNK)
            gr = gr_ref[:, rs] + bcol_ref[...]
            ls = _log_sigmoid(gr)
            ls1 = ls.astype(BF16).astype(F32)
            ls2 = (ls - ls1).astype(BF16).astype(F32)
            pieces = jnp.concatenate([ls1, ls2, ls - ls1 - ls2, zero_rows], axis=0).astype(BF16)
            sums = _dot(pieces, triu)
            br = sums[0:SUBLANES] + sums[SUBLANES:2 * SUBLANES] + sums[2 * SUBLANES:3 * SUBLANES]
            ab = jnp.where(sub < M_HEADS, gr - pltpu.roll(br, M_HEADS, axis=0), br)
            ab_ref[:, rs] = ab
            bc_ref[rs, :] = jnp.concatenate([ab, pad_rows], axis=0).T
        return carry

    lax.fori_loop(0, nchunks // GATE_CHUNKS_PER_STEP, gate_body, 0)

    def group_body(cg, carry):
        rows, ab = [], []
        for u_ in range(CHUNKS_PER_STEP):
            r0 = pl.multiple_of((cg * CHUNKS_PER_STEP + u_) * CHUNK, CHUNK)
            rows.append(pl.ds(r0, CHUNK))
            ab.append(ab_ref[:, rows[u_]])
        lanes = [(u_, h) for u_ in range(CHUNKS_PER_STEP) for h in heads]
        a_r = {(u_, h): ab[u_][h:h + 1, :] for u_, h in lanes}
        b_last = {(u_, h): ab[u_][M_HEADS + h:M_HEADS + h + 1, CHUNK - 1:CHUNK] for u_, h in lanes}

        m_prev, a_max, a_dec, s_in = {}, {}, {}, {}
        m_run = [mx_ref[h][0:1, 0:1] for h in heads]
        for k in lanes:
            u_, h = k
            m_prev[k] = m_run[h]
            a_max[k] = jnp.max(a_r[k], axis=-1, keepdims=True)
            m_loc = b_last[k] + a_max[k]
            m_new = jnp.maximum(b_last[k] + m_prev[k], m_loc)
            a_dec[k] = jnp.exp(b_last[k] + m_prev[k] - m_new)
            s_in[k] = jnp.exp(m_loc - m_new)
            m_run[h] = m_new
        for h in heads:
            mx_ref[h] = jnp.broadcast_to(m_run[h], (SUBLANES, LANES))

        q, v_ext, s_qk, kv = {}, {}, {}, {}
        for k in lanes:
            u_, h = k
            rs = rows[u_]
            q[k] = q_ref[rs, h * M_DIM:(h + 1) * M_DIM]
            kt = kt_ref[h * M_DIM:(h + 1) * M_DIM, rs]
            v = min_ref[rs, 2 * mw + h * M_DIM:2 * mw + (h + 1) * M_DIM]
            v_ext[k] = jnp.concatenate([v, ones_half], axis=1)
            s_qk[k] = _dot(q[k], kt.astype(BF16))
            e_r = jnp.exp(a_r[k] - a_max[k])
            kv[k] = _dot((kt * e_r).astype(BF16), v_ext[k])

        thr, qk, inter = {}, {}, {}
        for k in lanes:
            u_, h = k
            a_mat = jnp.where(causal, a_r[k], -jnp.inf)
            mu = jnp.maximum(jnp.max(a_mat, axis=-1, keepdims=True), m_prev[k])
            b_c = bc_ref[rows[u_], M_HEADS + h:M_HEADS + h + 1]
            thr[k] = jnp.broadcast_to(jnp.exp(-(b_c + mu)), (CHUNK, M_DIM))
            mu_b = jnp.broadcast_to(mu, (CHUNK, CHUNK))
            inter[k] = jnp.exp(m_prev[k] - mu_b)
            qk[k] = (s_qk[k] * jnp.exp(a_mat - mu_b)).astype(BF16)

        q_state = {}
        state = [st_ref[h] for h in heads]
        for k in lanes:
            u_, h = k
            q_state[k] = _dot(q[k], state[h].astype(BF16))
            state[h] = a_dec[k] * state[h] + s_in[k] * kv[k]
        for h in heads:
            st_ref[h] = state[h]

        for k in lanes:
            u_, h = k
            hs = slice(h * M_DIM, (h + 1) * M_DIM)
            num = _dot(qk[k], v_ext[k])
            den = jnp.maximum(jnp.abs(num[:, M_DIM:] + inter[k] * q_state[k][:, M_DIM:]), thr[k])
            hh = (num[:, :M_DIM] + inter[k] * q_state[k][:, :M_DIM]) / den
            msq = _dot((hh * hh).astype(BF16), mean_mat)
            hn = hh * lax.rsqrt(msq + EPS) * nw_ref[:, hs]
            og = min_ref[rows[u_], 3 * mw + h * M_DIM:3 * mw + (h + 1) * M_DIM].astype(F32)
            hm_ref[rows[u_], hs] = (_sigmoid(og) * hn).astype(BF16)
        return carry

    lax.fori_loop(0, nchunks // CHUNKS_PER_STEP, group_body, 0)


def _mlstm(m_in, grow, conv_w, conv_b, bcol, nw, batch, seq):
    t = m_in.shape[0]
    tt = min(1024, seq)
    tpb = seq // tt
    mw = M_HEADS * M_DIM
    cur = lambda b, i: (b * tpb + i, 0)
    const = lambda b, i: (0, 0)
    return pl.pallas_call(
        _mlstm_kernel,
        grid=(batch, tpb),
        in_specs=[
            pl.BlockSpec((tt, 4 * mw), cur),
            pl.BlockSpec((SUBLANES, tt), lambda b, i: (0, b * tpb + i)),
            pl.BlockSpec((CONV_K, 2 * mw), const),
            pl.BlockSpec((1, 2 * mw), const),
            pl.BlockSpec((SUBLANES, LANES), const),
            pl.BlockSpec((1, mw), const),
        ],
        out_specs=pl.BlockSpec((tt, mw), cur),
        out_shape=jax.ShapeDtypeStruct((t, mw), BF16),
        scratch_shapes=[
            pltpu.VMEM((tt + SUBLANES, 2 * mw), F32),
            pltpu.VMEM((tt, mw), BF16),
            pltpu.VMEM((mw, tt), F32),
            pltpu.VMEM((M_HEADS, M_DIM, 2 * M_DIM), F32),
            pltpu.VMEM((M_HEADS, SUBLANES, LANES), F32),
            pltpu.VMEM((SUBLANES, tt), F32),
            pltpu.VMEM((tt, LANES), F32),
        ],
        compiler_params=_params("arbitrary", "arbitrary"),
        name="mlstm",
    )(m_in, grow, conv_w, conv_b, bcol, nw)


def _merge_kernel(o_ref, hm_ref, mg_ref, x_ref, g1_ref, sc_ref, sh_ref, nw_ref,
                  wa_ref, wm_ref, wo_ref, rwh_ref, rwl_ref, rb_ref, tri_ref,
                  x1_ref, pay_ref, route_ref, cnt_ref, carry_ref):
    tm, d = x_ref.shape

    @pl.when(pl.program_id(0) == 0)
    def _():
        carry_ref[...] = jnp.zeros(carry_ref.shape, F32)

    ya = _dot(o_ref[...], wa_ref[...])
    yb = _dot(hm_ref[...], wm_ref[...])
    mg = mg_ref[...]
    merged = mg[:, :d].astype(F32) * ya + mg[:, d:].astype(F32) * yb
    x1 = x_ref[...] + g1_ref[...] * _dot(merged.astype(BF16), wo_ref[...])
    x1_ref[...] = x1

    ms = jnp.mean(x1 * x1, axis=-1, keepdims=True)
    h2 = x1 * lax.rsqrt(ms + EPS) * nw_ref[...]
    h2 = h2 * (1.0 + sc_ref[...]) + sh_ref[...]
    hi = h2.astype(BF16)
    hif = hi.astype(F32)
    lo = (h2 - hif).astype(BF16)
    r_hi = _dot_nt(rwh_ref[...], hi)
    r_lo = _dot_nt(rwl_ref[...], lo)
    sc_t = _sigmoid(r_hi[:N_EXPERTS] + r_hi[N_EXPERTS:] + r_lo)
    sel_t = sc_t + rb_ref[:, 0:1]

    def row(a, e):
        return a[e:e + 1, :]

    best = None
    gi = jnp.zeros((1, tm), I32)
    for g in range(N_GROUPS):
        r = [row(sel_t, EPG * g + i) for i in range(EPG)]
        gs = None
        for i in range(EPG):
            for j in range(i + 1, EPG):
                pr = r[i] + r[j]
                gs = pr if gs is None else jnp.maximum(gs, pr)
        if best is None:
            best = gs
        else:
            upd = gs > best
            gi = jnp.where(upd, g, gi)
            best = jnp.maximum(best, gs)

    def pick(a, i):
        out = row(a, i)
        for g in range(1, N_GROUPS):
            out = jnp.where(gi == g, row(a, EPG * g + i), out)
        return out

    v = [pick(sel_t, i) for i in range(EPG)]
    s = [pick(sc_t, i) for i in range(EPG)]

    def argmax4(vals):
        bv, bi = vals[0], jnp.zeros((1, tm), I32)
        for i in range(1, EPG):
            upd = vals[i] > bv
            bi = jnp.where(upd, i, bi)
            bv = jnp.maximum(bv, vals[i])
        return bi

    i1 = argmax4(v)
    i2 = argmax4([jnp.where(i1 == i, -jnp.inf, v[i]) for i in range(EPG)])
    ia = jnp.minimum(i1, i2)
    ib = jnp.maximum(i1, i2)
    pidx = jnp.where(ia == 0, ib - 1, jnp.where(ia == 1, ib + 1, N_PAIRS - 1))
    bucket = gi * N_PAIRS + pidx

    def by_index(vals, idx):
        out = vals[0]
        for i in range(1, EPG):
            out = jnp.where(idx == i, vals[i], out)
        return out

    s_a, s_b = by_index(s, ia), by_index(s, ib)
    gate_a = s_a / (s_a + s_b)
    gate_b = s_b / (s_a + s_b)

    brow = lax.broadcasted_iota(I32, (BUCKET_ROWS, tm), 0)
    onehot = brow == bucket
    cums = _dot(jnp.where(onehot, 1.0, 0.0).astype(BF16), tri_ref[...])
    carry = carry_ref[...]
    rank = jnp.sum(jnp.where(onehot, carry[:, 0:1] + cums, 0.0), axis=0, keepdims=True) - 1.0
    new_carry = carry + cums[:, tm - 1:tm]
    carry_ref[...] = new_carry
    cnt_ref[...] = new_carry

    route_ref[...] = jnp.concatenate(
        [bucket.astype(F32), gate_a, gate_b, rank, jnp.zeros((SUBLANES - 4, tm), F32)], axis=0)

    half = d // 2
    packed = _pack_bf16_pairs(hif)
    for cpart in range(half // LANES):
        pay_ref[cpart] = packed[:, cpart * LANES:(cpart + 1) * LANES]
    gates_t = jnp.concatenate([gate_a, gate_b, jnp.zeros((LANES - 2, tm), F32)], axis=0)
    pay_ref[half // LANES] = lax.bitcast_convert_type(gates_t.T, U32)


def _merge(o_attn, hm, mg, x, g1, sc2, sh2, nw, wa, wm, wo, rwh, rwl, rb, tri, l, seq):
    t, d = x.shape
    tm = tri.shape[0]
    tpb = seq // tm
    row = lambda i: (i, 0)
    bsel = lambda i: (i // tpb, 0, 0)
    wsel = lambda i: (l, 0, 0)
    const = lambda i: (0, 0)
    hw = o_attn.shape[1]
    return pl.pallas_call(
        _merge_kernel,
        grid=(t // tm,),
        in_specs=[
            pl.BlockSpec((tm, hw), row),
            pl.BlockSpec((tm, hw), row),
            pl.BlockSpec((tm, 2 * d), row),
            pl.BlockSpec((tm, d), row),
            pl.BlockSpec((None, 1, d), bsel),
            pl.BlockSpec((None, 1, d), bsel),
            pl.BlockSpec((None, 1, d), bsel),
            pl.BlockSpec((None, 1, d), wsel),
            pl.BlockSpec((None, hw, d), wsel),
            pl.BlockSpec((None, hw, d), wsel),
            pl.BlockSpec((None, d, d), wsel),
            pl.BlockSpec((2 * N_EXPERTS, d), const),
            pl.BlockSpec((N_EXPERTS, d), const),
            pl.BlockSpec((N_EXPERTS, LANES), const),
            pl.BlockSpec((tm, tm), const),
        ],
        out_specs=[
            pl.BlockSpec((tm, d), row),
            pl.BlockSpec((PAY_PARTS, tm, LANES), lambda i: (0, i, 0)),
            pl.BlockSpec((SUBLANES, tm), lambda i: (0, i)),
            pl.BlockSpec((BUCKET_ROWS, LANES), const),
        ],
        out_shape=[
            jax.ShapeDtypeStruct((t, d), F32),
            jax.ShapeDtypeStruct((PAY_PARTS, t, LANES), U32),
            jax.ShapeDtypeStruct((SUBLANES, t), F32),
            jax.ShapeDtypeStruct((BUCKET_ROWS, LANES), F32),
        ],
        scratch_shapes=[pltpu.VMEM((BUCKET_ROWS, LANES), F32)],
        compiler_params=_params("arbitrary"),
        name="merge_router",
    )(o_attn, hm, mg, x, g1, sc2, sh2, nw, wa, wm, wo, rwh, rwl, rb, tri)


def _sc_mesh():
    return plsc.VectorSubcoreMesh(core_axis_name="core", subcore_axis_name="subcore")


def _sc_scatter_rows(rows, dest, n_out):
    n, w = rows.shape

    @pl.kernel(out_type=jax.ShapeDtypeStruct((n_out, w), rows.dtype), mesh=_sc_mesh(), scratch_types=[])
    def scatter(x_hbm, i_hbm, o_hbm):
        def body(x_vmem, i_vmem):
            pltpu.sync_copy(x_vmem, o_hbm.at[i_vmem.at[0]])

        pltpu.emit_pipeline(
            body,
            grid=(n // SC_WINDOW,),
            in_specs=[pl.BlockSpec((SC_WINDOW, w), lambda i: (i, 0)),
                      pl.BlockSpec((1, SC_WINDOW), lambda i: (0, i))],
            out_specs=[],
            core_axis_name=("core", "subcore"),
            dimension_semantics=(pltpu.PARALLEL,),
        )(x_hbm, i_hbm)

    return scatter(rows, dest.reshape(1, n))


def _sc_gather_rows(src, idx):
    n = idx.shape[0]
    w = src.shape[1]

    @pl.kernel(out_type=jax.ShapeDtypeStruct((n, w), src.dtype), mesh=_sc_mesh(), scratch_types=[])
    def gather(x_hbm, i_hbm, o_hbm):
        def body(i_vmem, o_vmem):
            pltpu.sync_copy(x_hbm.at[i_vmem.at[0]], o_vmem)

        pltpu.emit_pipeline(
            body,
            grid=(n // SC_WINDOW,),
            in_specs=[pl.BlockSpec((1, SC_WINDOW), lambda i: (0, i))],
            out_specs=[pl.BlockSpec((SC_WINDOW, w), lambda i: (i, 0))],
            core_axis_name=("core", "subcore"),
            dimension_semantics=(pltpu.PARALLEL,),
        )(i_hbm, o_hbm)

    return gather(src, idx.reshape(1, n))


def _part_index(dest, parts, n_rows):
    return (jnp.arange(parts, dtype=I32)[:, None] * n_rows + dest[None, :]).reshape(-1)


def _residual_kernel(x_ref, y_ref, g2_ref, o_ref):
    y = _unpack_bf16_pairs(jnp.concatenate([y_ref[c] for c in range(OUT_PARTS)], axis=1))
    o_ref[...] = x_ref[...] + g2_ref[...] * y


def _residual(x1, ytok, g2, seq):
    t, d = x1.shape
    tm = min(512, seq)
    tpb = seq // tm
    return pl.pallas_call(
        _residual_kernel,
        grid=(t // tm,),
        in_specs=[
            pl.BlockSpec((tm, d), lambda i: (i, 0)),
            pl.BlockSpec((OUT_PARTS, tm, LANES), lambda i: (0, i, 0)),
            pl.BlockSpec((None, 1, d), lambda i: (i // tpb, 0, 0)),
        ],
        out_specs=pl.BlockSpec((tm, d), lambda i: (i, 0)),
        out_shape=jax.ShapeDtypeStruct((t, d), F32),
        compiler_params=_params("arbitrary"),
        name="residual",
    )(x1, ytok, g2)


def _expert_kernel(ea_ref, eb_ref, nr_ref, xs_ref, wga_ref, wua_ref, wda_ref, wgb_ref, wub_ref, wdb_ref, ys_ref):
    j = pl.program_id(0)
    nr = nr_ref[0]

    @pl.when(j < nr)
    def _():
        x = _unpack_bf16_pairs(jnp.concatenate([xs_ref[c] for c in range(PAY_PARTS - 1)], axis=1)).astype(BF16)
        gl = lax.bitcast_convert_type(xs_ref[PAY_PARTS - 1], F32)

        def ffn(wg_ref, wu_ref, wd_ref):
            gte = _dot(x, wg_ref[...])
            act = gte * _sigmoid(gte) * _dot(x, wu_ref[...])
            return _dot(act.astype(BF16), wd_ref[...])

        y = _pack_bf16_pairs(gl[:, 0:1] * ffn(wga_ref, wua_ref, wda_ref) + gl[:, 1:2] * ffn(wgb_ref, wub_ref, wdb_ref))
        for c in range(OUT_PARTS):
            ys_ref[c] = y[:, c * LANES:(c + 1) * LANES]

    @pl.when(j >= nr)
    def _():
        ys_ref[...] = jnp.zeros(ys_ref.shape, U32)


def _experts(blk_ea, blk_eb, n_real, xs, wg, wu, wd, d):
    n_rows = xs.shape[1]
    nblk = n_rows // EXPERT_BLOCK
    f = wg.shape[2]
    grid_spec = pltpu.PrefetchScalarGridSpec(
        num_scalar_prefetch=3,
        grid=(nblk,),
        in_specs=[
            pl.BlockSpec((PAY_PARTS, EXPERT_BLOCK, LANES), lambda j, ea, eb, nr: (0, j, 0)),
            pl.BlockSpec((None, d, f), lambda j, ea, eb, nr: (ea[j], 0, 0)),
            pl.BlockSpec((None, d, f), lambda j, ea, eb, nr: (ea[j], 0, 0)),
            pl.BlockSpec((None, f, d), lambda j, ea, eb, nr: (ea[j], 0, 0)),
            pl.BlockSpec((None, d, f), lambda j, ea, eb, nr: (eb[j], 0, 0)),
            pl.BlockSpec((None, d, f), lambda j, ea, eb, nr: (eb[j], 0, 0)),
            pl.BlockSpec((None, f, d), lambda j, ea, eb, nr: (eb[j], 0, 0)),
        ],
        out_specs=pl.BlockSpec((OUT_PARTS, EXPERT_BLOCK, LANES), lambda j, ea, eb, nr: (0, j, 0)),
    )
    return pl.pallas_call(
        _expert_kernel,
        grid_spec=grid_spec,
        out_shape=jax.ShapeDtypeStruct((OUT_PARTS, n_rows, LANES), U32),
        compiler_params=_params("arbitrary"),
        name="experts",
    )(blk_ea, blk_eb, n_real, xs, wg, wu, wd, wg, wu, wd)


_PAIR_A = (0, 0, 0, 1, 1, 2)
_PAIR_B = (1, 2, 3, 2, 3, 3)


def kernel(x, c, positions, ada_w, ada_b, norm_mix_w, norm_ffn_w, w_in, b_igate, b_fgate, q_norm_w, k_norm_w,
           sinks, conv_w, conv_b, mlstm_norm_w, w_attn_up, w_mlstm_up, w_out, router_w, router_bias,
           w_gate, w_up, w_down):
    batch, seq, d = x.shape
    depth = w_in.shape[0]
    t = batch * seq
    qw = N_HEADS * HEAD_DIM
    kvw = N_KV * HEAD_DIM
    mw = M_HEADS * M_DIM

    o = 0
    cols = {}
    for name, wdt in (("q", qw), ("k", kvw), ("v", kvw), ("mqk", 2 * mw), ("mv", mw), ("mi", M_HEADS),
                      ("mf", M_HEADS), ("mo", mw), ("ga", d), ("gb", d)):
        cols[name] = (o, o + wdt)
        o += wdt

    def wc(name, lo=0, hi=None):
        s, e = cols[name]
        return w_in[:, :, s + lo:(s + hi if hi is not None else e)]

    w_a = jnp.concatenate([wc("q"), wc("k"), wc("v")], axis=2).astype(BF16)
    w_m = jnp.concatenate([wc("mqk"), wc("mv"), wc("mo")], axis=2).astype(BF16)
    w_g = jnp.concatenate([wc("mi"), wc("mf"), jnp.zeros((depth, d, LANES - 2 * M_HEADS), F32)], axis=2).astype(BF16)
    w_mg = jnp.concatenate([wc("ga"), wc("gb")], axis=2).astype(BF16)
    w_au = w_attn_up.astype(BF16)
    w_mu = w_mlstm_up.astype(BF16)
    w_o = w_out.astype(BF16)
    n_e = w_gate.shape[1]
    w_g8 = w_gate.astype(BF16).reshape(depth * n_e, d, -1)
    w_u8 = w_up.astype(BF16).reshape(depth * n_e, d, -1)
    w_d = w_down.astype(BF16).reshape(depth * n_e, -1, d)

    rw_t = router_w.astype(F32).T
    rw_top = rw_t.astype(BF16)
    rw_hi = jnp.concatenate([rw_top, (rw_t - rw_top.astype(F32)).astype(BF16)], axis=0)
    rw_lo = rw_top
    rb = jnp.broadcast_to(router_bias.astype(F32)[:, None], (n_e, LANES))

    qn_w = jnp.tile(q_norm_w, (1, N_HEADS)).reshape(depth, 1, qw)
    kn_w = jnp.tile(k_norm_w, (1, N_KV)).reshape(depth, 1, kvw)
    seg = jnp.arange(qw) // HEAD_DIM
    bdq = jnp.where(seg[:, None] == seg[None, :], 1.0 / HEAD_DIM, 0.0).astype(BF16)
    bdk = bdq[:kvw, :kvw]

    inv_freq = ROPE_THETA ** (-(jnp.arange(0, ROPE_DIM, 2, dtype=F32) / ROPE_DIM))
    ang = positions.astype(F32).reshape(1, t) * inv_freq[:, None]
    cos8, sin8 = jnp.cos(ang).T, jnp.sin(ang).T
    pad1 = jnp.ones((t, HEAD_DIM - ROPE_DIM), F32)
    pad0 = jnp.zeros((t, HEAD_DIM - ROPE_DIM), F32)
    cos_t = jnp.tile(jnp.concatenate([cos8, cos8, pad1], axis=1), (1, LANES // HEAD_DIM))
    sin_t = jnp.tile(jnp.concatenate([-sin8, sin8, pad0], axis=1), (1, LANES // HEAD_DIM))

    gate_bias = jnp.concatenate([b_igate, b_fgate], axis=1).astype(F32)
    bcol = jnp.broadcast_to(gate_bias[:, :, None], (depth, 2 * M_HEADS, LANES))

    tm_merge = min(1024, seq)
    ii = jnp.arange(tm_merge)
    tri = (ii[:, None] <= ii[None, :]).astype(BF16)

    n_blk = (t + N_BUCKETS * (EXPERT_BLOCK - 1)) // EXPERT_BLOCK + 1
    n_rows = n_blk * EXPERT_BLOCK
    pair_a = jnp.asarray(_PAIR_A, I32)
    pair_b = jnp.asarray(_PAIR_B, I32)

    c_pad = jnp.zeros((SUBLANES, d), F32).at[:batch].set(c)
    mod = _ada_mod(c_pad, ada_w, ada_b)[:, :batch]

    xf = x.reshape(t, d)
    moe = None
    for l in range(depth):
        sh1, sc1, g1, sh2, sc2, g2 = [m.reshape(batch, 1, d) for m in jnp.split(mod[l], 6, axis=-1)]

        outs = _inproj(xf, moe, sc1, sh1, norm_mix_w.reshape(depth, 1, d), w_a, w_m, w_g, w_mg, l, seq)
        a_in, m_in, mg, grow = outs[:4]
        if moe is not None:
            xf = outs[4]
        o_attn = _attention(a_in, cos_t, sin_t, sinks[l], qn_w[l], kn_w[l], bdq, bdk, batch, seq)
        hm = _mlstm(m_in, grow, conv_w[l], conv_b[l].reshape(1, -1), bcol[l], mlstm_norm_w[l].reshape(1, mw),
                    batch, seq)
        x1, pay, route, cnt = _merge(o_attn, hm, mg, xf, g1, sc2, sh2, norm_ffn_w.reshape(depth, 1, d),
                                     w_au, w_mu, w_o, rw_hi, rw_lo, rb, tri, l, seq)

        counts = cnt[:N_BUCKETS, 0].astype(I32)
        padded = (counts + EXPERT_BLOCK - 1) // EXPERT_BLOCK * EXPERT_BLOCK
        pad_ends = jnp.cumsum(padded)
        pad_starts = pad_ends - padded
        bucket = route[0].astype(I32)
        dest = pad_starts[bucket] + route[3].astype(I32)
        blk_start = jnp.arange(n_blk, dtype=I32) * EXPERT_BLOCK
        blk_bucket = jnp.minimum(jnp.sum((pad_ends[None, :] <= blk_start[:, None]).astype(I32), axis=1), N_BUCKETS - 1)
        grp = blk_bucket // N_PAIRS
        blk_ea = (l * n_e + grp * EPG + pair_a[blk_bucket % N_PAIRS]).astype(I32)
        blk_eb = (l * n_e + grp * EPG + pair_b[blk_bucket % N_PAIRS]).astype(I32)
        n_real = (pad_ends[-1:] // EXPERT_BLOCK).astype(I32)

        xs = _sc_scatter_rows(pay.reshape(PAY_PARTS * t, LANES), _part_index(dest, PAY_PARTS, n_rows),
                              PAY_PARTS * n_rows).reshape(PAY_PARTS, n_rows, LANES)
        ys = _experts(blk_ea, blk_eb, n_real, xs, w_g8, w_u8, w_d, d)
        ytok = _sc_gather_rows(ys.reshape(OUT_PARTS * n_rows, LANES),
                               _part_index(dest, OUT_PARTS, n_rows)).reshape(OUT_PARTS, t, LANES)
        xf, moe = x1, (ytok, g2)
    return _residual(xf, moe[0], moe[1], seq).reshape(batch, seq, d)
```

```python
import functools

import jax
import jax.numpy as jnp
from jax import lax
from jax.experimental import pallas as pl
from jax.experimental.pallas import tpu as pltpu
from jax.experimental.pallas import tpu_sc as plsc

F32 = jnp.float32
BF16 = jnp.bfloat16
U32 = jnp.uint32
I32 = jnp.int32
HIGHEST = lax.Precision.HIGHEST

HEAD_DIM = 64
N_HEADS = 8
N_KV = 2
ROPE_DIM = 16
ROPE_THETA = 500000.0
ATTN_BLOCK = 128
M_HEADS = 4
M_DIM = 128
CONV_K = 4
N_EXPERTS = 16
N_GROUPS = 4
EPG = 4
EPS = 1e-6

LANES = 128
SUBLANES = 8

CHUNK = 128
CHUNKS_PER_STEP = 4
GATE_CHUNKS_PER_STEP = 4
N_PAIRS = 6
N_BUCKETS = N_GROUPS * N_PAIRS
BUCKET_ROWS = 32
EXPERT_BLOCK = 256
PAY_PARTS = 5
OUT_PARTS = 4
SC_WINDOW = 128
VMEM_LIMIT = 56 * 1024 * 1024


def _dot(a, b, precision=None):
    return jnp.dot(a, b, preferred_element_type=F32, precision=precision)


def _dot_nt(a, b):
    return lax.dot_general(a, b, (((1,), (1,)), ((), ())), preferred_element_type=F32)


def _sigmoid(x):
    return 1.0 / (1.0 + jnp.exp(-x))


def _log_sigmoid(x):
    return jnp.minimum(x, 0.0) - jnp.log1p(jnp.exp(-jnp.abs(x)))


def _pack_bf16_pairs(v):
    n = v.shape[1] // 2
    bits = lax.bitcast_convert_type(v.astype(BF16).astype(F32), U32)
    return bits[:, :n] | (bits[:, n:] >> 16)


def _unpack_bf16_pairs(w):
    hi = lax.bitcast_convert_type(w & jnp.uint32(0xFFFF0000), F32)
    lo = lax.bitcast_convert_type(w << 16, F32)
    return jnp.concatenate([hi, lo], axis=1)


def _params(*sem):
    return pltpu.CompilerParams(dimension_semantics=sem, vmem_limit_bytes=VMEM_LIMIT)


def _ada_kernel(c_ref, w_ref, b_ref, o_ref):
    c = c_ref[...]
    ca = c * _sigmoid(c)
    o_ref[0] = _dot(ca, w_ref[0], HIGHEST) + b_ref[0]


def _ada_mod(c_pad, ada_w, ada_b):
    depth, d, n = ada_w.shape
    tn = 1536
    return pl.pallas_call(
        _ada_kernel,
        grid=(depth, n // tn),
        in_specs=[
            pl.BlockSpec((SUBLANES, d), lambda l, j: (0, 0)),
            pl.BlockSpec((1, d, tn), lambda l, j: (l, 0, j)),
            pl.BlockSpec((1, 1, tn), lambda l, j: (l, 0, j)),
        ],
        out_specs=pl.BlockSpec((1, SUBLANES, tn), lambda l, j: (l, 0, j)),
        out_shape=jax.ShapeDtypeStruct((depth, SUBLANES, n), F32),
        compiler_params=_params("arbitrary", "arbitrary"),
        name="ada_mod",
    )(c_pad, ada_w, ada_b.reshape(depth, 1, n))


def _inproj_kernel(*refs, fuse_residual):
    if fuse_residual:
        (x_ref, y_ref, g2_ref, sc_ref, sh_ref, nw_ref, wa_ref, wm_ref, wg_ref, wmg_ref,
         a_ref, m_ref, mg_ref, gr_ref, xo_ref, g_ref) = refs
        y = _unpack_bf16_pairs(jnp.concatenate([y_ref[c] for c in range(OUT_PARTS)], axis=1))
        x = x_ref[...] + g2_ref[...] * y
        xo_ref[...] = x
    else:
        (x_ref, sc_ref, sh_ref, nw_ref, wa_ref, wm_ref, wg_ref, wmg_ref,
         a_ref, m_ref, mg_ref, gr_ref, g_ref) = refs
        x = x_ref[...]
    ms = jnp.mean(x * x, axis=-1, keepdims=True)
    h = x * lax.rsqrt(ms + EPS) * nw_ref[...]
    h = h * (1.0 + sc_ref[...]) + sh_ref[...]
    hb = h.astype(BF16)
    a_ref[...] = _dot(hb, wa_ref[...]).astype(BF16)
    m_ref[...] = _dot(hb, wm_ref[...]).astype(BF16)
    mg_ref[...] = _sigmoid(_dot(hb, wmg_ref[...])).astype(BF16)
    g_ref[...] = _dot(hb, wg_ref[...])
    gr_ref[...] = g_ref[...].T[:SUBLANES, :]


def _inproj(x, moe, sc, sh, nw, wa, wm, wg, wmg, l, seq):
    t, d = x.shape
    tm = min(512, seq)
    tpb = seq // tm
    row = lambda i: (i, 0)
    bsel = lambda i: (i // tpb, 0, 0)
    wsel = lambda i: (l, 0, 0)
    na, nm, ng, nmg = wa.shape[2], wm.shape[2], wg.shape[2], wmg.shape[2]
    fuse = moe is not None
    moe_specs = [pl.BlockSpec((OUT_PARTS, tm, LANES), lambda i: (0, i, 0)), pl.BlockSpec((None, 1, d), bsel)]
    return pl.pallas_call(
        functools.partial(_inproj_kernel, fuse_residual=fuse),
        grid=(t // tm,),
        in_specs=[pl.BlockSpec((tm, d), row)] + (moe_specs if fuse else []) + [
            pl.BlockSpec((None, 1, d), bsel),
            pl.BlockSpec((None, 1, d), bsel),
            pl.BlockSpec((None, 1, d), wsel),
            pl.BlockSpec((None, d, na), wsel),
            pl.BlockSpec((None, d, nm), wsel),
            pl.BlockSpec((None, d, ng), wsel),
            pl.BlockSpec((None, d, nmg), wsel),
        ],
        out_specs=[
            pl.BlockSpec((tm, na), row),
            pl.BlockSpec((tm, nm), row),
            pl.BlockSpec((tm, nmg), row),
            pl.BlockSpec((SUBLANES, tm), lambda i: (0, i)),
        ] + ([pl.BlockSpec((tm, d), row)] if fuse else []),
        out_shape=[
            jax.ShapeDtypeStruct((t, na), BF16),
            jax.ShapeDtypeStruct((t, nm), BF16),
            jax.ShapeDtypeStruct((t, nmg), BF16),
            jax.ShapeDtypeStruct((SUBLANES, t), F32),
        ] + ([jax.ShapeDtypeStruct((t, d), F32)] if fuse else []),
        scratch_shapes=[pltpu.VMEM((tm, ng), F32)],
        compiler_params=_params("arbitrary"),
        name="inproj",
    )(x, *(moe if fuse else ()), sc, sh, nw, wa, wm, wg, wmg)


def _rope(t, cos, sin):
    w = t.shape[1]
    reps = w // LANES
    cosw = jnp.concatenate([cos] * reps, axis=1) if reps > 1 else cos
    sinw = jnp.concatenate([sin] * reps, axis=1) if reps > 1 else sin
    lane = lax.broadcasted_iota(I32, t.shape, 1)
    half = ROPE_DIM // 2
    up = pltpu.roll(t, w - half, axis=1)
    dn = pltpu.roll(t, half, axis=1)
    partner = jnp.where((lane % ROPE_DIM) < half, up, dn)
    return t * cosw + partner * sinw


def _head_norm(t, bd, w):
    ms = _dot((t * t).astype(BF16), bd)
    return t * lax.rsqrt(ms + EPS) * w


def _attn_kernel(sink_ref, cur_ref, prev_ref, cos_ref, sin_ref, cosp_ref, sinp_ref,
                 qw_ref, kw_ref, bdq_ref, bdk_ref, o_ref):
    tq = cur_ref.shape[0]
    nj = tq // ATTN_BLOCK
    qw = N_HEADS * HEAD_DIM
    kw = N_KV * HEAD_DIM
    blk0 = pl.program_id(1) * nj

    cur = cur_ref[...]
    q = cur[:, :qw].astype(F32)
    kc = cur[:, qw:qw + kw].astype(F32)
    vc = cur[:, qw + kw:].astype(F32)
    prev = prev_ref[...]
    kp = prev[:, :kw].astype(F32)
    vp = prev[:, kw:].astype(F32)

    cos, sin = cos_ref[...], sin_ref[...]
    q = _rope(_head_norm(q, bdq_ref[...], qw_ref[...]), cos, sin) * (HEAD_DIM ** -0.5)
    kc = _rope(_head_norm(kc, bdk_ref[...], kw_ref[...]), cos, sin)
    kp = _rope(_head_norm(kp, bdk_ref[...], kw_ref[...]), cosp_ref[...], sinp_ref[...])
    qb = q.astype(BF16)

    def both_halves(x2):
        swapped = pltpu.roll(x2, HEAD_DIM, axis=1)
        first = lax.broadcasted_iota(I32, x2.shape, 1) < HEAD_DIM
        return jnp.concatenate([jnp.where(first, x2, swapped), jnp.where(first, swapped, x2)], axis=1).astype(BF16)

    k_all = both_halves(jnp.concatenate([kp, kc], axis=0))
    v_all = both_halves(jnp.concatenate([vp, vc], axis=0))

    lane = lax.broadcasted_iota(I32, (ATTN_BLOCK, LANES), 1)
    lo = lane < HEAD_DIM
    zero = jnp.zeros((ATTN_BLOCK, LANES), BF16)
    g_heads = N_HEADS // N_KV
    ri = lax.broadcasted_iota(I32, (g_heads * ATTN_BLOCK, ATTN_BLOCK), 0) % ATTN_BLOCK
    ci = lax.broadcasted_iota(I32, (g_heads * ATTN_BLOCK, ATTN_BLOCK), 1)
    from_prev = ci > ri
    head_row = lax.broadcasted_iota(I32, (g_heads * ATTN_BLOCK, 1), 0) // ATTN_BLOCK
    ones_v = jnp.ones((2 * ATTN_BLOCK, LANES), BF16)

    tiles = [(j, g) for j in range(nj) for g in range(N_KV)]
    scores = {}
    for j, g in tiles:
        rows = slice(j * ATTN_BLOCK, (j + 1) * ATTN_BLOCK)
        band = slice(j * ATTN_BLOCK, (j + 2) * ATTN_BLOCK)
        qp0 = qb[rows, (2 * g) * LANES:(2 * g + 1) * LANES]
        qp1 = qb[rows, (2 * g + 1) * LANES:(2 * g + 2) * LANES]
        q4 = jnp.concatenate([jnp.where(lo, qp0, zero), jnp.where(lo, zero, qp0),
                              jnp.where(lo, qp1, zero), jnp.where(lo, zero, qp1)], axis=0)
        scores[j, g] = _dot_nt(q4, k_all[band, g * LANES:(g + 1) * LANES])

    probs, sink_term = {}, {}
    for j, g in tiles:
        prev_ok = ci > ri + (1 - jnp.minimum(blk0 + j, 1)) * ATTN_BLOCK
        s2 = scores[j, g]
        s = jnp.where(prev_ok, s2[:, :ATTN_BLOCK], jnp.where(from_prev, -jnp.inf, s2[:, ATTN_BLOCK:]))
        sink = jnp.full((g_heads * ATTN_BLOCK, 1), sink_ref[g_heads * g], F32)
        for r in range(1, g_heads):
            sink = jnp.where(head_row == r, sink_ref[g_heads * g + r], sink)
        m = jnp.maximum(jnp.max(s, axis=-1, keepdims=True), sink)
        p = jnp.exp(s - m)
        probs[j, g] = jnp.concatenate([jnp.where(from_prev, p, 0.0), jnp.where(from_prev, 0.0, p)],
                                      axis=1).astype(BF16)
        sink_term[j, g] = jnp.exp(sink - m)

    for j, g in tiles:
        rows = slice(j * ATTN_BLOCK, (j + 1) * ATTN_BLOCK)
        band = slice(j * ATTN_BLOCK, (j + 2) * ATTN_BLOCK)
        o8 = _dot(probs[j, g], jnp.concatenate([v_all[band, g * LANES:(g + 1) * LANES], ones_v], axis=1))
        o4 = o8[:, :LANES] / (o8[:, LANES:] + sink_term[j, g])
        b = ATTN_BLOCK
        o_ref[rows, (2 * g) * LANES:(2 * g + 1) * LANES] = jnp.where(lo, o4[0:b], o4[b:2 * b]).astype(BF16)
        o_ref[rows, (2 * g + 1) * LANES:(2 * g + 2) * LANES] = jnp.where(
            lo, o4[2 * b:3 * b], o4[3 * b:4 * b]).astype(BF16)


def _attention(a_in, cos_t, sin_t, sinks_l, qw, kw, bdq, bdk, batch, seq):
    t = a_in.shape[0]
    tq = min(512, seq)
    nj = tq // ATTN_BLOCK
    tpb = seq // tq
    bpb = seq // ATTN_BLOCK
    qwid = N_HEADS * HEAD_DIM
    kvw = 2 * N_KV * HEAD_DIM
    cur = lambda b, i: (b * tpb + i, 0)
    prv = lambda b, i: (b * bpb + jnp.maximum(i * nj - 1, 0), qwid // kvw)
    prv0 = lambda b, i: (b * bpb + jnp.maximum(i * nj - 1, 0), 0)
    const = lambda b, i: (0, 0)
    return pl.pallas_call(
        _attn_kernel,
        grid=(batch, tpb),
        in_specs=[
            pl.BlockSpec(memory_space=pltpu.SMEM),
            pl.BlockSpec((tq, qwid + kvw), cur),
            pl.BlockSpec((ATTN_BLOCK, kvw), prv),
            pl.BlockSpec((tq, LANES), cur),
            pl.BlockSpec((tq, LANES), cur),
            pl.BlockSpec((ATTN_BLOCK, LANES), prv0),
            pl.BlockSpec((ATTN_BLOCK, LANES), prv0),
            pl.BlockSpec((1, qwid), const),
            pl.BlockSpec((1, kvw // 2), const),
            pl.BlockSpec((qwid, qwid), const),
            pl.BlockSpec((kvw // 2, kvw // 2), const),
        ],
        out_specs=pl.BlockSpec((tq, qwid), cur),
        out_shape=jax.ShapeDtypeStruct((t, qwid), BF16),
        compiler_params=_params("arbitrary", "arbitrary"),
        name="swa_attention",
    )(sinks_l, a_in, a_in, cos_t, sin_t, cos_t, sin_t, qw, kw, bdq, bdk)


def _mlstm_kernel(min_ref, gr_ref, cw_ref, cb_ref, bcol_ref, nw_ref,
                  hm_ref, ext_ref, q_ref, kt_ref, st_ref, mx_ref, ab_ref, bc_ref):
    tt = min_ref.shape[0]
    mw = M_HEADS * M_DIM
    nchunks = tt // CHUNK

    @pl.when(pl.program_id(1) == 0)
    def _():
        ext_ref[0:SUBLANES, :] = jnp.zeros((SUBLANES, 2 * mw), F32)
        st_ref[...] = jnp.zeros(st_ref.shape, F32)
        mx_ref[...] = jnp.zeros(mx_ref.shape, F32)

    def conv_block(cols):
        u = min_ref[:, cols].astype(F32)
        ext_ref[SUBLANES:SUBLANES + tt, cols] = u
        acc = cb_ref[:, cols] + cw_ref[CONV_K - 1:CONV_K, cols] * u
        for jj in range(CONV_K - 1):
            off = SUBLANES - (CONV_K - 1) + jj
            acc = acc + cw_ref[jj:jj + 1, cols] * ext_ref[off:off + tt, cols]
        ext_ref[0:SUBLANES, cols] = u[tt - SUBLANES:tt, :]
        return acc * _sigmoid(acc)

    def q_body(h, carry):
        cols = pl.ds(pl.multiple_of(h * M_DIM, M_DIM), M_DIM)
        q_ref[:, cols] = conv_block(cols).astype(BF16)
        return carry

    def k_body(h, carry):
        off = pl.multiple_of(h * M_DIM, M_DIM)
        act = conv_block(pl.ds(mw + off, M_DIM)) * (M_DIM ** -0.5)
        for j in range(nchunks):
            kt_ref[pl.ds(off, M_DIM), j * CHUNK:(j + 1) * CHUNK] = act[j * CHUNK:(j + 1) * CHUNK, :].T
        return carry

    lax.fori_loop(0, M_HEADS, q_body, 0)
    lax.fori_loop(0, M_HEADS, k_body, 0)

    ri = lax.broadcasted_iota(I32, (CHUNK, CHUNK), 0)
    ci = lax.broadcasted_iota(I32, (CHUNK, CHUNK), 1)
    causal = ci <= ri
    triu = jnp.where(ri <= ci, 1.0, 0.0).astype(BF16)
    ones_half = jnp.ones((CHUNK, M_DIM), BF16)
    mean_mat = jnp.full((M_DIM, M_DIM), 1.0 / M_DIM, BF16)
    sub = lax.broadcasted_iota(I32, (SUBLANES, CHUNK), 0)
    heads = range(M_HEADS)

    pad_rows = jnp.zeros((CHUNK - SUBLANES, CHUNK), F32)
    zero_rows = jnp.zeros((SUBLANES, CHUNK), F32)

    def gate_body(jg, carry):
        for u_ in range(GATE_CHUNKS_PER_STEP):
            rs = pl.ds(pl.multiple_of((jg * GATE_CHUNKS_PER_STEP + u_) * CHUNK, CHUNK), CHUNK)
            gr = gr_ref[:, rs] + bcol_ref[...]
            ls = _log_sigmoid(gr)
            ls1 = ls.astype(BF16).astype(F32)
            ls2 = (ls - ls1).astype(BF16).astype(F32)
            pieces = jnp.concatenate([ls1, ls2, ls - ls1 - ls2, zero_rows], axis=0).astype(BF16)
            sums = _dot(pieces, triu)
            br = sums[0:SUBLANES] + sums[SUBLANES:2 * SUBLANES] + sums[2 * SUBLANES:3 * SUBLANES]
            ab = jnp.where(sub < M_HEADS, gr - pltpu.roll(br, M_HEADS, axis=0), br)
            ab_ref[:, rs] = ab
            bc_ref[rs, :] = jnp.concatenate([ab, pad_rows], axis=0).T
        return carry

    lax.fori_loop(0, nchunks // GATE_CHUNKS_PER_STEP, gate_body, 0)

    def group_body(cg, carry):
        rows, ab = [], []
        for u_ in range(CHUNKS_PER_STEP):
            r0 = pl.multiple_of((cg * CHUNKS_PER_STEP + u_) * CHUNK, CHUNK)
            rows.append(pl.ds(r0, CHUNK))
            ab.append(ab_ref[:, rows[u_]])
        lanes = [(u_, h) for u_ in range(CHUNKS_PER_STEP) for h in heads]
        a_r = {(u_, h): ab[u_][h:h + 1, :] for u_, h in lanes}
        b_last = {(u_, h): ab[u_][M_HEADS + h:M_HEADS + h + 1, CHUNK - 1:CHUNK] for u_, h in lanes}

        m_prev, a_max, a_dec, s_in = {}, {}, {}, {}
        m_run = [mx_ref[h][0:1, 0:1] for h in heads]
        for k in lanes:
            u_, h = k
            m_prev[k] = m_run[h]
            a_max[k] = jnp.max(a_r[k], axis=-1, keepdims=True)
            m_loc = b_last[k] + a_max[k]
            m_new = jnp.maximum(b_last[k] + m_prev[k], m_loc)
            a_dec[k] = jnp.exp(b_last[k] + m_prev[k] - m_new)
            s_in[k] = jnp.exp(m_loc - m_new)
            m_run[h] = m_new
        for h in heads:
            mx_ref[h] = jnp.broadcast_to(m_run[h], (SUBLANES, LANES))

        q, v_ext, s_qk, kv = {}, {}, {}, {}
        for k in lanes:
            u_, h = k
            rs = rows[u_]
            q[k] = q_ref[rs, h * M_DIM:(h + 1) * M_DIM]
            kt = kt_ref[h * M_DIM:(h + 1) * M_DIM, rs]
            v = min_ref[rs, 2 * mw + h * M_DIM:2 * mw + (h + 1) * M_DIM]
            v_ext[k] = jnp.concatenate([v, ones_half], axis=1)
            s_qk[k] = _dot(q[k], kt.astype(BF16))
            e_r = jnp.exp(a_r[k] - a_max[k])
            kv[k] = _dot((kt * e_r).astype(BF16), v_ext[k])

        thr, qk, inter = {}, {}, {}
        for k in lanes:
            u_, h = k
            a_mat = jnp.where(causal, a_r[k], -jnp.inf)
            mu = jnp.maximum(jnp.max(a_mat, axis=-1, keepdims=True), m_prev[k])
            b_c = bc_ref[rows[u_], M_HEADS + h:M_HEADS + h + 1]
            thr[k] = jnp.broadcast_to(jnp.exp(-(b_c + mu)), (CHUNK, M_DIM))
            mu_b = jnp.broadcast_to(mu, (CHUNK, CHUNK))
            inter[k] = jnp.exp(m_prev[k] - mu_b)
            qk[k] = (s_qk[k] * jnp.exp(a_mat - mu_b)).astype(BF16)

        q_state = {}
        state = [st_ref[h] for h in heads]
        for k in lanes:
            u_, h = k
            q_state[k] = _dot(q[k], state[h].astype(BF16))
            state[h] = a_dec[k] * state[h] + s_in[k] * kv[k]
        for h in heads:
            st_ref[h] = state[h]

        for k in lanes:
            u_, h = k
            hs = slice(h * M_DIM, (h + 1) * M_DIM)
            num = _dot(qk[k], v_ext[k])
            den = jnp.maximum(jnp.abs(num[:, M_DIM:] + inter[k] * q_state[k][:, M_DIM:]), thr[k])
            hh = (num[:, :M_DIM] + inter[k] * q_state[k][:, :M_DIM]) / den
            msq = _dot((hh * hh).astype(BF16), mean_mat)
            hn = hh * lax.rsqrt(msq + EPS) * nw_ref[:, hs]
            og = min_ref[rows[u_], 3 * mw + h * M_DIM:3 * mw + (h + 1) * M_DIM].astype(F32)
            hm_ref[rows[u_], hs] = (_sigmoid(og) * hn).astype(BF16)
        return carry

    lax.fori_loop(0, nchunks // CHUNKS_PER_STEP, group_body, 0)


def _mlstm(m_in, grow, conv_w, conv_b, bcol, nw, batch, seq):
    t = m_in.shape[0]
    tt = min(1024, seq)
    tpb = seq // tt
    mw = M_HEADS * M_DIM
    cur = lambda b, i: (b * tpb + i, 0)
    const = lambda b, i: (0, 0)
    return pl.pallas_call(
        _mlstm_kernel,
        grid=(batch, tpb),
        in_specs=[
            pl.BlockSpec((tt, 4 * mw), cur),
            pl.BlockSpec((SUBLANES, tt), lambda b, i: (0, b * tpb + i)),
            pl.BlockSpec((CONV_K, 2 * mw), const),
            pl.BlockSpec((1, 2 * mw), const),
            pl.BlockSpec((SUBLANES, LANES), const),
            pl.BlockSpec((1, mw), const),
        ],
        out_specs=pl.BlockSpec((tt, mw), cur),
        out_shape=jax.ShapeDtypeStruct((t, mw), BF16),
        scratch_shapes=[
            pltpu.VMEM((tt + SUBLANES, 2 * mw), F32),
            pltpu.VMEM((tt, mw), BF16),
            pltpu.VMEM((mw, tt), F32),
            pltpu.VMEM((M_HEADS, M_DIM, 2 * M_DIM), F32),
            pltpu.VMEM((M_HEADS, SUBLANES, LANES), F32),
            pltpu.VMEM((SUBLANES, tt), F32),
            pltpu.VMEM((tt, LANES), F32),
        ],
        compiler_params=_params("arbitrary", "arbitrary"),
        name="mlstm",
    )(m_in, grow, conv_w, conv_b, bcol, nw)


def _merge_kernel(o_ref, hm_ref, mg_ref, x_ref, g1_ref, sc_ref, sh_ref, nw_ref,
                  wa_ref, wm_ref, wo_ref, rwh_ref, rwl_ref, rb_ref, tri_ref,
                  x1_ref, pay_ref, route_ref, cnt_ref, carry_ref):
    tm, d = x_ref.shape

    @pl.when(pl.program_id(0) == 0)
    def _():
        carry_ref[...] = jnp.zeros(carry_ref.shape, F32)

    ya = _dot(o_ref[...], wa_ref[...])
    yb = _dot(hm_ref[...], wm_ref[...])
    mg = mg_ref[...]
    merged = mg[:, :d].astype(F32) * ya + mg[:, d:].astype(F32) * yb
    x1 = x_ref[...] + g1_ref[...] * _dot(merged.astype(BF16), wo_ref[...])
    x1_ref[...] = x1

    ms = jnp.mean(x1 * x1, axis=-1, keepdims=True)
    h2 = x1 * lax.rsqrt(ms + EPS) * nw_ref[...]
    h2 = h2 * (1.0 + sc_ref[...]) + sh_ref[...]
    hi = h2.astype(BF16)
    hif = hi.astype(F32)
    lo = (h2 - hif).astype(BF16)
    r_hi = _dot_nt(rwh_ref[...], hi)
    r_lo = _dot_nt(rwl_ref[...], lo)
    sc_t = _sigmoid(r_hi[:N_EXPERTS] + r_hi[N_EXPERTS:] + r_lo)
    sel_t = sc_t + rb_ref[:, 0:1]

    def row(a, e):
        return a[e:e + 1, :]

    best = None
    gi = jnp.zeros((1, tm), I32)
    for g in range(N_GROUPS):
        r = [row(sel_t, EPG * g + i) for i in range(EPG)]
        gs = None
        for i in range(EPG):
            for j in range(i + 1, EPG):
                pr = r[i] + r[j]
                gs = pr if gs is None else jnp.maximum(gs, pr)
        if best is None:
            best = gs
        else:
            upd = gs > best
            gi = jnp.where(upd, g, gi)
            best = jnp.maximum(best, gs)

    def pick(a, i):
        out = row(a, i)
        for g in range(1, N_GROUPS):
            out = jnp.where(gi == g, row(a, EPG * g + i), out)
        return out

    v = [pick(sel_t, i) for i in range(EPG)]
    s = [pick(sc_t, i) for i in range(EPG)]

    def argmax4(vals):
        bv, bi = vals[0], jnp.zeros((1, tm), I32)
        for i in range(1, EPG):
            upd = vals[i] > bv
            bi = jnp.where(upd, i, bi)
            bv = jnp.maximum(bv, vals[i])
        return bi

    i1 = argmax4(v)
    i2 = argmax4([jnp.where(i1 == i, -jnp.inf, v[i]) for i in range(EPG)])
    ia = jnp.minimum(i1, i2)
    ib = jnp.maximum(i1, i2)
    pidx = jnp.where(ia == 0, ib - 1, jnp.where(ia == 1, jnp.where(ib == 3, 3, 5), 4))
    bucket = gi * N_PAIRS + pidx

    def by_index(vals, idx):
        out = vals[0]
        for i in range(1, EPG):
            out = jnp.where(idx == i, vals[i], out)
        return out

    swap = pidx == N_PAIRS - 1
    s_lo, s_hi = by_index(s, ia), by_index(s, ib)
    s_a, s_b = jnp.where(swap, s_hi, s_lo), jnp.where(swap, s_lo, s_hi)
    gate_a = s_a / (s_a + s_b)
    gate_b = s_b / (s_a + s_b)

    brow = lax.broadcasted_iota(I32, (BUCKET_ROWS, tm), 0)
    onehot = brow == bucket
    cums = _dot(jnp.where(onehot, 1.0, 0.0).astype(BF16), tri_ref[...])
    carry = carry_ref[...]
    rank = jnp.sum(jnp.where(onehot, carry[:, 0:1] + cums, 0.0), axis=0, keepdims=True) - 1.0
    new_carry = carry + cums[:, tm - 1:tm]
    carry_ref[...] = new_carry
    cnt_ref[...] = new_carry

    route_ref[...] = jnp.concatenate(
        [bucket.astype(F32), gate_a, gate_b, rank, jnp.zeros((SUBLANES - 4, tm), F32)], axis=0)

    half = d // 2
    packed = _pack_bf16_pairs(hif)
    for cpart in range(half // LANES):
        pay_ref[cpart] = packed[:, cpart * LANES:(cpart + 1) * LANES]
    gates_t = jnp.concatenate([gate_a, gate_b, jnp.zeros((LANES - 2, tm), F32)], axis=0)
    pay_ref[half // LANES] = lax.bitcast_convert_type(gates_t.T, U32)


def _merge(o_attn, hm, mg, x, g1, sc2, sh2, nw, wa, wm, wo, rwh, rwl, rb, tri, l, seq):
    t, d = x.shape
    tm = tri.shape[0]
    tpb = seq // tm
    row = lambda i: (i, 0)
    bsel = lambda i: (i // tpb, 0, 0)
    wsel = lambda i: (l, 0, 0)
    const = lambda i: (0, 0)
    hw = o_attn.shape[1]
    return pl.pallas_call(
        _merge_kernel,
        grid=(t // tm,),
        in_specs=[
            pl.BlockSpec((tm, hw), row),
            pl.BlockSpec((tm, hw), row),
            pl.BlockSpec((tm, 2 * d), row),
            pl.BlockSpec((tm, d), row),
            pl.BlockSpec((None, 1, d), bsel),
            pl.BlockSpec((None, 1, d), bsel),
            pl.BlockSpec((None, 1, d), bsel),
            pl.BlockSpec((None, 1, d), wsel),
            pl.BlockSpec((None, hw, d), wsel),
            pl.BlockSpec((None, hw, d), wsel),
            pl.BlockSpec((None, d, d), wsel),
            pl.BlockSpec((2 * N_EXPERTS, d), const),
            pl.BlockSpec((N_EXPERTS, d), const),
            pl.BlockSpec((N_EXPERTS, LANES), const),
            pl.BlockSpec((tm, tm), const),
        ],
        out_specs=[
            pl.BlockSpec((tm, d), row),
            pl.BlockSpec((PAY_PARTS, tm, LANES), lambda i: (0, i, 0)),
            pl.BlockSpec((SUBLANES, tm), lambda i: (0, i)),
            pl.BlockSpec((BUCKET_ROWS, LANES), const),
        ],
        out_shape=[
            jax.ShapeDtypeStruct((t, d), F32),
            jax.ShapeDtypeStruct((PAY_PARTS, t, LANES), U32),
            jax.ShapeDtypeStruct((SUBLANES, t), F32),
            jax.ShapeDtypeStruct((BUCKET_ROWS, LANES), F32),
        ],
        scratch_shapes=[pltpu.VMEM((BUCKET_ROWS, LANES), F32)],
        compiler_params=_params("arbitrary"),
        name="merge_router",
    )(o_attn, hm, mg, x, g1, sc2, sh2, nw, wa, wm, wo, rwh, rwl, rb, tri)


def _sc_mesh():
    return plsc.VectorSubcoreMesh(core_axis_name="core", subcore_axis_name="subcore")


def _sc_scatter_rows(rows, dest, n_out):
    n, w = rows.shape

    @pl.kernel(out_type=jax.ShapeDtypeStruct((n_out, w), rows.dtype), mesh=_sc_mesh(), scratch_types=[])
    def scatter(x_hbm, i_hbm, o_hbm):
        def body(x_vmem, i_vmem):
            pltpu.sync_copy(x_vmem, o_hbm.at[i_vmem.at[0]])

        pltpu.emit_pipeline(
            body,
            grid=(n // SC_WINDOW,),
            in_specs=[pl.BlockSpec((SC_WINDOW, w), lambda i: (i, 0)),
                      pl.BlockSpec((1, SC_WINDOW), lambda i: (0, i))],
            out_specs=[],
            core_axis_name=("core", "subcore"),
            dimension_semantics=(pltpu.PARALLEL,),
        )(x_hbm, i_hbm)

    return scatter(rows, dest.reshape(1, n))


def _sc_gather_rows(src, idx):
    n = idx.shape[0]
    w = src.shape[1]

    @pl.kernel(out_type=jax.ShapeDtypeStruct((n, w), src.dtype), mesh=_sc_mesh(), scratch_types=[])
    def gather(x_hbm, i_hbm, o_hbm):
        def body(i_vmem, o_vmem):
            pltpu.sync_copy(x_hbm.at[i_vmem.at[0]], o_vmem)

        pltpu.emit_pipeline(
            body,
            grid=(n // SC_WINDOW,),
            in_specs=[pl.BlockSpec((1, SC_WINDOW), lambda i: (0, i))],
            out_specs=[pl.BlockSpec((SC_WINDOW, w), lambda i: (i, 0))],
            core_axis_name=("core", "subcore"),
            dimension_semantics=(pltpu.PARALLEL,),
        )(i_hbm, o_hbm)

    return gather(src, idx.reshape(1, n))


def _part_index(dest, parts, n_rows):
    return (jnp.arange(parts, dtype=I32)[:, None] * n_rows + dest[None, :]).reshape(-1)


def _residual_kernel(x_ref, y_ref, g2_ref, o_ref):
    y = _unpack_bf16_pairs(jnp.concatenate([y_ref[c] for c in range(OUT_PARTS)], axis=1))
    o_ref[...] = x_ref[...] + g2_ref[...] * y


def _residual(x1, ytok, g2, seq):
    t, d = x1.shape
    tm = min(512, seq)
    tpb = seq // tm
    return pl.pallas_call(
        _residual_kernel,
        grid=(t // tm,),
        in_specs=[
            pl.BlockSpec((tm, d), lambda i: (i, 0)),
            pl.BlockSpec((OUT_PARTS, tm, LANES), lambda i: (0, i, 0)),
            pl.BlockSpec((None, 1, d), lambda i: (i // tpb, 0, 0)),
        ],
        out_specs=pl.BlockSpec((tm, d), lambda i: (i, 0)),
        out_shape=jax.ShapeDtypeStruct((t, d), F32),
        compiler_params=_params("arbitrary"),
        name="residual",
    )(x1, ytok, g2)


def _expert_kernel(ea_ref, eb_ref, nr_ref, xs_ref, wga_ref, wua_ref, wda_ref, wgb_ref, wub_ref, wdb_ref, ys_ref,
                   ga_ref, ua_ref, da_ref, gb_ref, ub_ref, db_ref):
    j = pl.program_id(0)
    nr = nr_ref[0]
    prev = jnp.maximum(j - 1, 0)

    @pl.when((j == 0) | (ea_ref[j] != ea_ref[prev]))
    def _():
        ga_ref[...] = wga_ref[...].astype(BF16)
        ua_ref[...] = wua_ref[...].astype(BF16)
        da_ref[...] = wda_ref[...].astype(BF16)

    @pl.when((j == 0) | (eb_ref[j] != eb_ref[prev]))
    def _():
        gb_ref[...] = wgb_ref[...].astype(BF16)
        ub_ref[...] = wub_ref[...].astype(BF16)
        db_ref[...] = wdb_ref[...].astype(BF16)

    @pl.when(j < nr)
    def _():
        x = _unpack_bf16_pairs(jnp.concatenate([xs_ref[c] for c in range(PAY_PARTS - 1)], axis=1)).astype(BF16)
        gl = lax.bitcast_convert_type(xs_ref[PAY_PARTS - 1], F32)

        def ffn(wg_ref, wu_ref, wd_ref):
            gte = _dot(x, wg_ref[...])
            act = gte * _sigmoid(gte) * _dot(x, wu_ref[...])
            return _dot(act.astype(BF16), wd_ref[...])

        y = _pack_bf16_pairs(gl[:, 0:1] * ffn(ga_ref, ua_ref, da_ref) + gl[:, 1:2] * ffn(gb_ref, ub_ref, db_ref))
        for c in range(OUT_PARTS):
            ys_ref[c] = y[:, c * LANES:(c + 1) * LANES]

    @pl.when(j >= nr)
    def _():
        ys_ref[...] = jnp.zeros(ys_ref.shape, U32)


def _experts(blk_ea, blk_eb, n_real, xs, wg, wu, wd, d):
    n_rows = xs.shape[1]
    nblk = n_rows // EXPERT_BLOCK
    f = wg.shape[2]
    grid_spec = pltpu.PrefetchScalarGridSpec(
        num_scalar_prefetch=3,
        grid=(nblk,),
        in_specs=[
            pl.BlockSpec((PAY_PARTS, EXPERT_BLOCK, LANES), lambda j, ea, eb, nr: (0, j, 0)),
            pl.BlockSpec((None, d, f), lambda j, ea, eb, nr: (ea[j], 0, 0)),
            pl.BlockSpec((None, d, f), lambda j, ea, eb, nr: (ea[j], 0, 0)),
            pl.BlockSpec((None, f, d), lambda j, ea, eb, nr: (ea[j], 0, 0)),
            pl.BlockSpec((None, d, f), lambda j, ea, eb, nr: (eb[j], 0, 0)),
            pl.BlockSpec((None, d, f), lambda j, ea, eb, nr: (eb[j], 0, 0)),
            pl.BlockSpec((None, f, d), lambda j, ea, eb, nr: (eb[j], 0, 0)),
        ],
        out_specs=pl.BlockSpec((OUT_PARTS, EXPERT_BLOCK, LANES), lambda j, ea, eb, nr: (0, j, 0)),
        scratch_shapes=[pltpu.VMEM((d, f), BF16), pltpu.VMEM((d, f), BF16), pltpu.VMEM((f, d), BF16)] * 2,
    )
    return pl.pallas_call(
        _expert_kernel,
        grid_spec=grid_spec,
        out_shape=jax.ShapeDtypeStruct((OUT_PARTS, n_rows, LANES), U32),
        compiler_params=_params("arbitrary"),
        name="experts",
    )(blk_ea, blk_eb, n_real, xs, wg, wu, wd, wg, wu, wd)


_PAIR_A = (0, 0, 0, 1, 2, 2)
_PAIR_B = (1, 2, 3, 3, 3, 1)


def kernel(x, c, positions, ada_w, ada_b, norm_mix_w, norm_ffn_w, w_in, b_igate, b_fgate, q_norm_w, k_norm_w,
           sinks, conv_w, conv_b, mlstm_norm_w, w_attn_up, w_mlstm_up, w_out, router_w, router_bias,
           w_gate, w_up, w_down):
    batch, seq, d = x.shape
    depth = w_in.shape[0]
    t = batch * seq
    qw = N_HEADS * HEAD_DIM
    kvw = N_KV * HEAD_DIM
    mw = M_HEADS * M_DIM

    o = 0
    cols = {}
    for name, wdt in (("q", qw), ("k", kvw), ("v", kvw), ("mqk", 2 * mw), ("mv", mw), ("mi", M_HEADS),
                      ("mf", M_HEADS), ("mo", mw), ("ga", d), ("gb", d)):
        cols[name] = (o, o + wdt)
        o += wdt

    def wc(name, lo=0, hi=None):
        s, e = cols[name]
        return w_in[:, :, s + lo:(s + hi if hi is not None else e)]

    w_a = jnp.concatenate([wc("q"), wc("k"), wc("v")], axis=2).astype(BF16)
    w_m = jnp.concatenate([wc("mqk"), wc("mv"), wc("mo")], axis=2).astype(BF16)
    w_g = jnp.concatenate([wc("mi"), wc("mf"), jnp.zeros((depth, d, LANES - 2 * M_HEADS), F32)], axis=2).astype(BF16)
    w_mg = jnp.concatenate([wc("ga"), wc("gb")], axis=2).astype(BF16)
    w_au = w_attn_up.astype(BF16)
    w_mu = w_mlstm_up.astype(BF16)
    w_o = w_out.astype(BF16)
    n_e = w_gate.shape[1]
    w_g8 = w_gate.reshape(depth * n_e, d, -1)
    w_u8 = w_up.reshape(depth * n_e, d, -1)
    w_d = w_down.reshape(depth * n_e, -1, d)

    rw_t = router_w.astype(F32).T
    rw_top = rw_t.astype(BF16)
    rw_hi = jnp.concatenate([rw_top, (rw_t - rw_top.astype(F32)).astype(BF16)], axis=0)
    rw_lo = rw_top
    rb = jnp.broadcast_to(router_bias.astype(F32)[:, None], (n_e, LANES))

    qn_w = jnp.tile(q_norm_w, (1, N_HEADS)).reshape(depth, 1, qw)
    kn_w = jnp.tile(k_norm_w, (1, N_KV)).reshape(depth, 1, kvw)
    seg = jnp.arange(qw) // HEAD_DIM
    bdq = jnp.where(seg[:, None] == seg[None, :], 1.0 / HEAD_DIM, 0.0).astype(BF16)
    bdk = bdq[:kvw, :kvw]

    inv_freq = ROPE_THETA ** (-(jnp.arange(0, ROPE_DIM, 2, dtype=F32) / ROPE_DIM))
    ang = positions.astype(F32).reshape(1, t) * inv_freq[:, None]
    cos8, sin8 = jnp.cos(ang).T, jnp.sin(ang).T
    pad1 = jnp.ones((t, HEAD_DIM - ROPE_DIM), F32)
    pad0 = jnp.zeros((t, HEAD_DIM - ROPE_DIM), F32)
    cos_t = jnp.tile(jnp.concatenate([cos8, cos8, pad1], axis=1), (1, LANES // HEAD_DIM))
    sin_t = jnp.tile(jnp.concatenate([-sin8, sin8, pad0], axis=1), (1, LANES // HEAD_DIM))

    gate_bias = jnp.concatenate([b_igate, b_fgate], axis=1).astype(F32)
    bcol = jnp.broadcast_to(gate_bias[:, :, None], (depth, 2 * M_HEADS, LANES))

    tm_merge = min(1024, seq)
    ii = jnp.arange(tm_merge)
    tri = (ii[:, None] <= ii[None, :]).astype(BF16)

    n_blk = (t + N_BUCKETS * (EXPERT_BLOCK - 1)) // EXPERT_BLOCK + 1
    n_rows = n_blk * EXPERT_BLOCK
    pair_a = jnp.asarray(_PAIR_A, I32)
    pair_b = jnp.asarray(_PAIR_B, I32)

    c_pad = jnp.zeros((SUBLANES, d), F32).at[:batch].set(c)
    mod = _ada_mod(c_pad, ada_w, ada_b)[:, :batch]

    xf = x.reshape(t, d)
    moe = None
    for l in range(depth):
        sh1, sc1, g1, sh2, sc2, g2 = [m.reshape(batch, 1, d) for m in jnp.split(mod[l], 6, axis=-1)]

        outs = _inproj(xf, moe, sc1, sh1, norm_mix_w.reshape(depth, 1, d), w_a, w_m, w_g, w_mg, l, seq)
        a_in, m_in, mg, grow = outs[:4]
        if moe is not None:
            xf = outs[4]
        o_attn = _attention(a_in, cos_t, sin_t, sinks[l], qn_w[l], kn_w[l], bdq, bdk, batch, seq)
        hm = _mlstm(m_in, grow, conv_w[l], conv_b[l].reshape(1, -1), bcol[l], mlstm_norm_w[l].reshape(1, mw),
                    batch, seq)
        x1, pay, route, cnt = _merge(o_attn, hm, mg, xf, g1, sc2, sh2, norm_ffn_w.reshape(depth, 1, d),
                                     w_au, w_mu, w_o, rw_hi, rw_lo, rb, tri, l, seq)

        counts = cnt[:N_BUCKETS, 0].astype(I32)
        padded = (counts + EXPERT_BLOCK - 1) // EXPERT_BLOCK * EXPERT_BLOCK
        pad_ends = jnp.cumsum(padded)
        pad_starts = pad_ends - padded
        bucket = route[0].astype(I32)
        dest = pad_starts[bucket] + route[3].astype(I32)
        blk_start = jnp.arange(n_blk, dtype=I32) * EXPERT_BLOCK
        blk_bucket = jnp.minimum(jnp.sum((pad_ends[None, :] <= blk_start[:, None]).astype(I32), axis=1), N_BUCKETS - 1)
        grp = blk_bucket // N_PAIRS
        blk_ea = (l * n_e + grp * EPG + pair_a[blk_bucket % N_PAIRS]).astype(I32)
        blk_eb = (l * n_e + grp * EPG + pair_b[blk_bucket % N_PAIRS]).astype(I32)
        n_real = (pad_ends[-1:] // EXPERT_BLOCK).astype(I32)

        xs = _sc_scatter_rows(pay.reshape(PAY_PARTS * t, LANES), _part_index(dest, PAY_PARTS, n_rows),
                              PAY_PARTS * n_rows).reshape(PAY_PARTS, n_rows, LANES)
        ys = _experts(blk_ea, blk_eb, n_real, xs, w_g8, w_u8, w_d, d)
        ytok = _sc_gather_rows(ys.reshape(OUT_PARTS * n_rows, LANES),
                               _part_index(dest, OUT_PARTS, n_rows)).reshape(OUT_PARTS, t, LANES)
        xf, moe = x1, (ytok, g2)
    return _residual(xf, moe[0], moe[1], seq).reshape(batch, seq, d)
```

```python
import functools

import jax
import jax.numpy as jnp
from jax import lax
from jax.experimental import pallas as pl
from jax.experimental.pallas import tpu as pltpu
from jax.experimental.pallas import tpu_sc as plsc

F32 = jnp.float32
BF16 = jnp.bfloat16
U32 = jnp.uint32
I32 = jnp.int32
HIGHEST = lax.Precision.HIGHEST

HEAD_DIM = 64
N_HEADS = 8
N_KV = 2
ROPE_DIM = 16
ROPE_THETA = 500000.0
ATTN_BLOCK = 128
M_HEADS = 4
M_DIM = 128
CONV_K = 4
N_EXPERTS = 16
N_GROUPS = 4
EPG = 4
EPS = 1e-6

LANES = 128
SUBLANES = 8

CHUNK = 128
CHUNKS_PER_STEP = 4
GATE_CHUNKS_PER_STEP = 4
N_PAIRS = 6
N_BUCKETS = N_GROUPS * N_PAIRS
BUCKET_ROWS = 32
EXPERT_BLOCK = 256
PAY_PARTS = 5
OUT_PARTS = 4
SC_WINDOW = 128
VMEM_LIMIT = 56 * 1024 * 1024


def _dot(a, b, precision=None):
    return jnp.dot(a, b, preferred_element_type=F32, precision=precision)


def _dot_nt(a, b):
    return lax.dot_general(a, b, (((1,), (1,)), ((), ())), preferred_element_type=F32)


def _sigmoid(x):
    return 1.0 / (1.0 + jnp.exp(-x))


def _log_sigmoid(x):
    return jnp.minimum(x, 0.0) - jnp.log1p(jnp.exp(-jnp.abs(x)))


def _pack_bf16_pairs(v):
    n = v.shape[1] // 2
    bits = lax.bitcast_convert_type(v.astype(BF16).astype(F32), U32)
    return bits[:, :n] | (bits[:, n:] >> 16)


def _unpack_bf16_pairs(w):
    hi = lax.bitcast_convert_type(w & jnp.uint32(0xFFFF0000), F32)
    lo = lax.bitcast_convert_type(w << 16, F32)
    return jnp.concatenate([hi, lo], axis=1)


def _params(*sem):
    return pltpu.CompilerParams(dimension_semantics=sem, vmem_limit_bytes=VMEM_LIMIT)


def _ada_kernel(c_ref, w_ref, b_ref, o_ref):
    c = c_ref[...]
    ca = c * _sigmoid(c)
    o_ref[0] = _dot(ca, w_ref[0], HIGHEST) + b_ref[0]


def _ada_mod(c_pad, ada_w, ada_b):
    depth, d, n = ada_w.shape
    tn = 1536
    return pl.pallas_call(
        _ada_kernel,
        grid=(depth, n // tn),
        in_specs=[
            pl.BlockSpec((SUBLANES, d), lambda l, j: (0, 0)),
            pl.BlockSpec((1, d, tn), lambda l, j: (l, 0, j)),
            pl.BlockSpec((1, 1, tn), lambda l, j: (l, 0, j)),
        ],
        out_specs=pl.BlockSpec((1, SUBLANES, tn), lambda l, j: (l, 0, j)),
        out_shape=jax.ShapeDtypeStruct((depth, SUBLANES, n), F32),
        compiler_params=_params("arbitrary", "arbitrary"),
        name="ada_mod",
    )(c_pad, ada_w, ada_b.reshape(depth, 1, n))


def _inproj_kernel(*refs, fuse_residual):
    if fuse_residual:
        (x_ref, y_ref, g2_ref, sc_ref, sh_ref, nw_ref, wa_ref, wm_ref, wg_ref, wmg_ref,
         a_ref, m_ref, mg_ref, gr_ref, xo_ref, g_ref) = refs
        y = _unpack_bf16_pairs(jnp.concatenate([y_ref[c] for c in range(OUT_PARTS)], axis=1))
        x = x_ref[...] + g2_ref[...] * y
        xo_ref[...] = x
    else:
        (x_ref, sc_ref, sh_ref, nw_ref, wa_ref, wm_ref, wg_ref, wmg_ref,
         a_ref, m_ref, mg_ref, gr_ref, g_ref) = refs
        x = x_ref[...]
    ms = jnp.mean(x * x, axis=-1, keepdims=True)
    h = x * lax.rsqrt(ms + EPS) * nw_ref[...]
    h = h * (1.0 + sc_ref[...]) + sh_ref[...]
    hb = h.astype(BF16)
    a_ref[...] = _dot(hb, wa_ref[...]).astype(BF16)
    m_ref[...] = _dot(hb, wm_ref[...]).astype(BF16)
    mg_ref[...] = _sigmoid(_dot(hb, wmg_ref[...])).astype(BF16)
    g_ref[...] = _dot(hb, wg_ref[...])
    gr_ref[...] = g_ref[...].T[:SUBLANES, :]


def _inproj(x, moe, sc, sh, nw, wa, wm, wg, wmg, l, seq):
    t, d = x.shape
    tm = min(1024, seq)
    tpb = seq // tm
    row = lambda i: (i, 0)
    bsel = lambda i: (i // tpb, 0, 0)
    wsel = lambda i: (l, 0, 0)
    once = pl.Buffered(1)
    na, nm, ng, nmg = wa.shape[2], wm.shape[2], wg.shape[2], wmg.shape[2]
    fuse = moe is not None
    moe_specs = [pl.BlockSpec((OUT_PARTS, tm, LANES), lambda i: (0, i, 0)), pl.BlockSpec((None, 1, d), bsel)]
    return pl.pallas_call(
        functools.partial(_inproj_kernel, fuse_residual=fuse),
        grid=(t // tm,),
        in_specs=[pl.BlockSpec((tm, d), row)] + (moe_specs if fuse else []) + [
            pl.BlockSpec((None, 1, d), bsel),
            pl.BlockSpec((None, 1, d), bsel),
            pl.BlockSpec((None, 1, d), wsel),
            pl.BlockSpec((None, d, na), wsel, pipeline_mode=once),
            pl.BlockSpec((None, d, nm), wsel, pipeline_mode=once),
            pl.BlockSpec((None, d, ng), wsel, pipeline_mode=once),
            pl.BlockSpec((None, d, nmg), wsel, pipeline_mode=once),
        ],
        out_specs=[
            pl.BlockSpec((tm, na), row),
            pl.BlockSpec((tm, nm), row),
            pl.BlockSpec((tm, nmg), row),
            pl.BlockSpec((SUBLANES, tm), lambda i: (0, i)),
        ] + ([pl.BlockSpec((tm, d), row)] if fuse else []),
        out_shape=[
            jax.ShapeDtypeStruct((t, na), BF16),
            jax.ShapeDtypeStruct((t, nm), BF16),
            jax.ShapeDtypeStruct((t, nmg), BF16),
            jax.ShapeDtypeStruct((SUBLANES, t), F32),
        ] + ([jax.ShapeDtypeStruct((t, d), F32)] if fuse else []),
        scratch_shapes=[pltpu.VMEM((tm, ng), F32)],
        compiler_params=_params("arbitrary"),
        name="inproj",
    )(x, *(moe if fuse else ()), sc, sh, nw, wa, wm, wg, wmg)


def _rope(t, cos, sin):
    w = t.shape[1]
    reps = w // LANES
    cosw = jnp.concatenate([cos] * reps, axis=1) if reps > 1 else cos
    sinw = jnp.concatenate([sin] * reps, axis=1) if reps > 1 else sin
    lane = lax.broadcasted_iota(I32, t.shape, 1)
    half = ROPE_DIM // 2
    up = pltpu.roll(t, w - half, axis=1)
    dn = pltpu.roll(t, half, axis=1)
    partner = jnp.where((lane % ROPE_DIM) < half, up, dn)
    return t * cosw + partner * sinw


def _head_norm(t, bd, w):
    ms = _dot((t * t).astype(BF16), bd)
    return t * lax.rsqrt(ms + EPS) * w


def _attn_kernel(sink_ref, cur_ref, prev_ref, cos_ref, sin_ref, cosp_ref, sinp_ref,
                 qw_ref, kw_ref, bdq_ref, bdk_ref, o_ref):
    tq = cur_ref.shape[0]
    nj = tq // ATTN_BLOCK
    qw = N_HEADS * HEAD_DIM
    kw = N_KV * HEAD_DIM
    blk0 = pl.program_id(1) * nj

    cur = cur_ref[...]
    q = cur[:, :qw].astype(F32)
    kc = cur[:, qw:qw + kw].astype(F32)
    vc = cur[:, qw + kw:].astype(F32)
    prev = prev_ref[...]
    kp = prev[:, :kw].astype(F32)
    vp = prev[:, kw:].astype(F32)

    cos, sin = cos_ref[...], sin_ref[...]
    q = _rope(_head_norm(q, bdq_ref[...], qw_ref[...]), cos, sin) * (HEAD_DIM ** -0.5)
    kc = _rope(_head_norm(kc, bdk_ref[...], kw_ref[...]), cos, sin)
    kp = _rope(_head_norm(kp, bdk_ref[...], kw_ref[...]), cosp_ref[...], sinp_ref[...])
    qb = q.astype(BF16)

    def both_halves(x2):
        swapped = pltpu.roll(x2, HEAD_DIM, axis=1)
        first = lax.broadcasted_iota(I32, x2.shape, 1) < HEAD_DIM
        return jnp.concatenate([jnp.where(first, x2, swapped), jnp.where(first, swapped, x2)], axis=1).astype(BF16)

    k_all = both_halves(jnp.concatenate([kp, kc], axis=0))
    v_all = both_halves(jnp.concatenate([vp, vc], axis=0))

    lane = lax.broadcasted_iota(I32, (ATTN_BLOCK, LANES), 1)
    lo = lane < HEAD_DIM
    zero = jnp.zeros((ATTN_BLOCK, LANES), BF16)
    g_heads = N_HEADS // N_KV
    ri = lax.broadcasted_iota(I32, (g_heads * ATTN_BLOCK, ATTN_BLOCK), 0) % ATTN_BLOCK
    ci = lax.broadcasted_iota(I32, (g_heads * ATTN_BLOCK, ATTN_BLOCK), 1)
    from_prev = ci > ri
    head_row = lax.broadcasted_iota(I32, (g_heads * ATTN_BLOCK, 1), 0) // ATTN_BLOCK
    ones_v = jnp.ones((2 * ATTN_BLOCK, LANES), BF16)

    tiles = [(j, g) for j in range(nj) for g in range(N_KV)]
    scores = {}
    for j, g in tiles:
        rows = slice(j * ATTN_BLOCK, (j + 1) * ATTN_BLOCK)
        band = slice(j * ATTN_BLOCK, (j + 2) * ATTN_BLOCK)
        qp0 = qb[rows, (2 * g) * LANES:(2 * g + 1) * LANES]
        qp1 = qb[rows, (2 * g + 1) * LANES:(2 * g + 2) * LANES]
        q4 = jnp.concatenate([jnp.where(lo, qp0, zero), jnp.where(lo, zero, qp0),
                              jnp.where(lo, qp1, zero), jnp.where(lo, zero, qp1)], axis=0)
        scores[j, g] = _dot_nt(q4, k_all[band, g * LANES:(g + 1) * LANES])

    probs, sink_term = {}, {}
    for j, g in tiles:
        prev_ok = ci > ri + (1 - jnp.minimum(blk0 + j, 1)) * ATTN_BLOCK
        s2 = scores[j, g]
        s = jnp.where(prev_ok, s2[:, :ATTN_BLOCK], jnp.where(from_prev, -jnp.inf, s2[:, ATTN_BLOCK:]))
        sink = jnp.full((g_heads * ATTN_BLOCK, 1), sink_ref[g_heads * g], F32)
        for r in range(1, g_heads):
            sink = jnp.where(head_row == r, sink_ref[g_heads * g + r], sink)
        m = jnp.maximum(jnp.max(s, axis=-1, keepdims=True), sink)
        p = jnp.exp(s - m)
        probs[j, g] = jnp.concatenate([jnp.where(from_prev, p, 0.0), jnp.where(from_prev, 0.0, p)],
                                      axis=1).astype(BF16)
        sink_term[j, g] = jnp.exp(sink - m)

    for j, g in tiles:
        rows = slice(j * ATTN_BLOCK, (j + 1) * ATTN_BLOCK)
        band = slice(j * ATTN_BLOCK, (j + 2) * ATTN_BLOCK)
        o8 = _dot(probs[j, g], jnp.concatenate([v_all[band, g * LANES:(g + 1) * LANES], ones_v], axis=1))
        o4 = o8[:, :LANES] / (o8[:, LANES:] + sink_term[j, g])
        b = ATTN_BLOCK
        o_ref[rows, (2 * g) * LANES:(2 * g + 1) * LANES] = jnp.where(lo, o4[0:b], o4[b:2 * b]).astype(BF16)
        o_ref[rows, (2 * g + 1) * LANES:(2 * g + 2) * LANES] = jnp.where(
            lo, o4[2 * b:3 * b], o4[3 * b:4 * b]).astype(BF16)


def _attention(a_in, cos_t, sin_t, sinks_l, qw, kw, bdq, bdk, batch, seq):
    t = a_in.shape[0]
    tq = min(512, seq)
    nj = tq // ATTN_BLOCK
    tpb = seq // tq
    bpb = seq // ATTN_BLOCK
    qwid = N_HEADS * HEAD_DIM
    kvw = 2 * N_KV * HEAD_DIM
    cur = lambda b, i: (b * tpb + i, 0)
    prv = lambda b, i: (b * bpb + jnp.maximum(i * nj - 1, 0), qwid // kvw)
    prv0 = lambda b, i: (b * bpb + jnp.maximum(i * nj - 1, 0), 0)
    const = lambda b, i: (0, 0)
    return pl.pallas_call(
        _attn_kernel,
        grid=(batch, tpb),
        in_specs=[
            pl.BlockSpec(memory_space=pltpu.SMEM),
            pl.BlockSpec((tq, qwid + kvw), cur),
            pl.BlockSpec((ATTN_BLOCK, kvw), prv),
            pl.BlockSpec((tq, LANES), cur),
            pl.BlockSpec((tq, LANES), cur),
            pl.BlockSpec((ATTN_BLOCK, LANES), prv0),
            pl.BlockSpec((ATTN_BLOCK, LANES), prv0),
            pl.BlockSpec((1, qwid), const),
            pl.BlockSpec((1, kvw // 2), const),
            pl.BlockSpec((qwid, qwid), const),
            pl.BlockSpec((kvw // 2, kvw // 2), const),
        ],
        out_specs=pl.BlockSpec((tq, qwid), cur),
        out_shape=jax.ShapeDtypeStruct((t, qwid), BF16),
        compiler_params=_params("arbitrary", "arbitrary"),
        name="swa_attention",
    )(sinks_l, a_in, a_in, cos_t, sin_t, cos_t, sin_t, qw, kw, bdq, bdk)


def _mlstm_kernel(min_ref, gr_ref, cw_ref, cb_ref, bcol_ref, nw_ref,
                  hm_ref, ext_ref, q_ref, kt_ref, st_ref, mx_ref, ab_ref, bc_ref):
    tt = min_ref.shape[0]
    mw = M_HEADS * M_DIM
    nchunks = tt // CHUNK

    @pl.when(pl.program_id(1) == 0)
    def _():
        ext_ref[0:SUBLANES, :] = jnp.zeros((SUBLANES, 2 * mw), F32)
        st_ref[...] = jnp.zeros(st_ref.shape, F32)
        mx_ref[...] = jnp.zeros(mx_ref.shape, F32)

    def conv_block(cols):
        u = min_ref[:, cols].astype(F32)
        ext_ref[SUBLANES:SUBLANES + tt, cols] = u
        acc = cb_ref[:, cols] + cw_ref[CONV_K - 1:CONV_K, cols] * u
        for jj in range(CONV_K - 1):
            off = SUBLANES - (CONV_K - 1) + jj
            acc = acc + cw_ref[jj:jj + 1, cols] * ext_ref[off:off + tt, cols]
        ext_ref[0:SUBLANES, cols] = u[tt - SUBLANES:tt, :]
        return acc * _sigmoid(acc)

    def q_body(h, carry):
        cols = pl.ds(pl.multiple_of(h * M_DIM, M_DIM), M_DIM)
        q_ref[:, cols] = conv_block(cols).astype(BF16)
        return carry

    def k_body(h, carry):
        off = pl.multiple_of(h * M_DIM, M_DIM)
        act = conv_block(pl.ds(mw + off, M_DIM)) * (M_DIM ** -0.5)
        for j in range(nchunks):
            kt_ref[pl.ds(off, M_DIM), j * CHUNK:(j + 1) * CHUNK] = act[j * CHUNK:(j + 1) * CHUNK, :].T
        return carry

    lax.fori_loop(0, M_HEADS, q_body, 0)
    lax.fori_loop(0, M_HEADS, k_body, 0)

    ri = lax.broadcasted_iota(I32, (CHUNK, CHUNK), 0)
    ci = lax.broadcasted_iota(I32, (CHUNK, CHUNK), 1)
    causal = ci <= ri
    triu = jnp.where(ri <= ci, 1.0, 0.0).astype(BF16)
    ones_half = jnp.ones((CHUNK, M_DIM), BF16)
    mean_mat = jnp.full((M_DIM, M_DIM), 1.0 / M_DIM, BF16)
    sub = lax.broadcasted_iota(I32, (SUBLANES, CHUNK), 0)
    heads = range(M_HEADS)

    pad_rows = jnp.zeros((CHUNK - SUBLANES, CHUNK), F32)
    zero_rows = jnp.zeros((SUBLANES, CHUNK), F32)

    def gate_body(jg, carry):
        for u_ in range(GATE_CHUNKS_PER_STEP):
            rs = pl.ds(pl.multiple_of((jg * GATE_CHUNKS_PER_STEP + u_) * CHUNK, CHUNK), CHUNK)
            gr = gr_ref[:, rs] + bcol_ref[...]
            ls = _log_sigmoid(gr)
            ls1 = ls.astype(BF16).astype(F32)
            ls2 = (ls - ls1).astype(BF16).astype(F32)
            pieces = jnp.concatenate([ls1, ls2, ls - ls1 - ls2, zero_rows], axis=0).astype(BF16)
            sums = _dot(pieces, triu)
            br = sums[0:SUBLANES] + sums[SUBLANES:2 * SUBLANES] + sums[2 * SUBLANES:3 * SUBLANES]
            ab = jnp.where(sub < M_HEADS, gr - pltpu.roll(br, M_HEADS, axis=0), br)
            ab_ref[:, rs] = ab
            bc_ref[rs, :] = jnp.concatenate([ab, pad_rows], axis=0).T
        return carry

    lax.fori_loop(0, nchunks // GATE_CHUNKS_PER_STEP, gate_body, 0)

    def group_body(cg, carry):
        rows, ab = [], []
        for u_ in range(CHUNKS_PER_STEP):
            r0 = pl.multiple_of((cg * CHUNKS_PER_STEP + u_) * CHUNK, CHUNK)
            rows.append(pl.ds(r0, CHUNK))
            ab.append(ab_ref[:, rows[u_]])
        lanes = [(u_, h) for u_ in range(CHUNKS_PER_STEP) for h in heads]
        a_r = {(u_, h): ab[u_][h:h + 1, :] for u_, h in lanes}
        b_last = {(u_, h): ab[u_][M_HEADS + h:M_HEADS + h + 1, CHUNK - 1:CHUNK] for u_, h in lanes}

        m_prev, a_max, a_dec, s_in = {}, {}, {}, {}
        m_run = [mx_ref[h][0:1, 0:1] for h in heads]
        for k in lanes:
            u_, h = k
            m_prev[k] = m_run[h]
            a_max[k] = jnp.max(a_r[k], axis=-1, keepdims=True)
            m_loc = b_last[k] + a_max[k]
            m_new = jnp.maximum(b_last[k] + m_prev[k], m_loc)
            a_dec[k] = jnp.exp(b_last[k] + m_prev[k] - m_new)
            s_in[k] = jnp.exp(m_loc - m_new)
            m_run[h] = m_new
        for h in heads:
            mx_ref[h] = jnp.broadcast_to(m_run[h], (SUBLANES, LANES))

        q, v_ext, s_qk, kv, thr, qk, inter, q_state = {}, {}, {}, {}, {}, {}, {}, {}
        state = [st_ref[h] for h in heads]

        def stage_scores(k):
            u_, h = k
            rs = rows[u_]
            q[k] = q_ref[rs, h * M_DIM:(h + 1) * M_DIM]
            kt = kt_ref[h * M_DIM:(h + 1) * M_DIM, rs]
            v = min_ref[rs, 2 * mw + h * M_DIM:2 * mw + (h + 1) * M_DIM]
            v_ext[k] = jnp.concatenate([v, ones_half], axis=1)
            s_qk[k] = _dot(q[k], kt.astype(BF16))
            e_r = jnp.exp(a_r[k] - a_max[k])
            kv[k] = _dot((kt * e_r).astype(BF16), v_ext[k])

        def stage_weights(k):
            u_, h = k
            a_mat = jnp.where(causal, a_r[k], -jnp.inf)
            mu = jnp.maximum(jnp.max(a_mat, axis=-1, keepdims=True), m_prev[k])
            b_c = bc_ref[rows[u_], M_HEADS + h:M_HEADS + h + 1]
            thr[k] = jnp.broadcast_to(jnp.exp(-(b_c + mu)), (CHUNK, M_DIM))
            mu_b = jnp.broadcast_to(mu, (CHUNK, CHUNK))
            inter[k] = jnp.exp(m_prev[k] - mu_b)
            qk[k] = (s_qk.pop(k) * jnp.exp(a_mat - mu_b)).astype(BF16)

        def stage_state(k):
            u_, h = k
            q_state[k] = _dot(q[k], state[h].astype(BF16))
            state[h] = a_dec[k] * state[h] + s_in[k] * kv.pop(k)

        def stage_output(k):
            u_, h = k
            hs = slice(h * M_DIM, (h + 1) * M_DIM)
            num = _dot(qk.pop(k), v_ext.pop(k))
            qs = q_state.pop(k)
            den = jnp.maximum(jnp.abs(num[:, M_DIM:] + inter[k] * qs[:, M_DIM:]), thr.pop(k))
            hh = (num[:, :M_DIM] + inter.pop(k) * qs[:, :M_DIM]) / den
            msq = _dot((hh * hh).astype(BF16), mean_mat)
            hn = hh * lax.rsqrt(msq + EPS) * nw_ref[:, hs]
            og = min_ref[rows[u_], 3 * mw + h * M_DIM:3 * mw + (h + 1) * M_DIM].astype(F32)
            hm_ref[rows[u_], hs] = (_sigmoid(og) * hn).astype(BF16)

        stages = (stage_scores, stage_weights, stage_state, stage_output)
        for n in range(len(lanes) + len(stages) - 1):
            for depth, stage in enumerate(stages):
                if 0 <= n - depth < len(lanes):
                    stage(lanes[n - depth])
        for h in heads:
            st_ref[h] = state[h]
        return carry

    lax.fori_loop(0, nchunks // CHUNKS_PER_STEP, group_body, 0)


def _mlstm(m_in, grow, conv_w, conv_b, bcol, nw, batch, seq):
    t = m_in.shape[0]
    tt = min(1024, seq)
    tpb = seq // tt
    mw = M_HEADS * M_DIM
    cur = lambda b, i: (b * tpb + i, 0)
    const = lambda b, i: (0, 0)
    return pl.pallas_call(
        _mlstm_kernel,
        grid=(batch, tpb),
        in_specs=[
            pl.BlockSpec((tt, 4 * mw), cur),
            pl.BlockSpec((SUBLANES, tt), lambda b, i: (0, b * tpb + i)),
            pl.BlockSpec((CONV_K, 2 * mw), const),
            pl.BlockSpec((1, 2 * mw), const),
            pl.BlockSpec((SUBLANES, LANES), const),
            pl.BlockSpec((1, mw), const),
        ],
        out_specs=pl.BlockSpec((tt, mw), cur),
        out_shape=jax.ShapeDtypeStruct((t, mw), BF16),
        scratch_shapes=[
            pltpu.VMEM((tt + SUBLANES, 2 * mw), F32),
            pltpu.VMEM((tt, mw), BF16),
            pltpu.VMEM((mw, tt), F32),
            pltpu.VMEM((M_HEADS, M_DIM, 2 * M_DIM), F32),
            pltpu.VMEM((M_HEADS, SUBLANES, LANES), F32),
            pltpu.VMEM((SUBLANES, tt), F32),
            pltpu.VMEM((tt, LANES), F32),
        ],
        compiler_params=_params("arbitrary", "arbitrary"),
        name="mlstm",
    )(m_in, grow, conv_w, conv_b, bcol, nw)


def _merge_kernel(o_ref, hm_ref, mg_ref, x_ref, g1_ref, sc_ref, sh_ref, nw_ref,
                  wa_ref, wm_ref, wo_ref, rwh_ref, rwl_ref, rb_ref, tri_ref,
                  x1_ref, pay_ref, route_ref, cnt_ref, carry_ref):
    tm, d = x_ref.shape

    @pl.when(pl.program_id(0) == 0)
    def _():
        carry_ref[...] = jnp.zeros(carry_ref.shape, F32)

    ya = _dot(o_ref[...], wa_ref[...])
    yb = _dot(hm_ref[...], wm_ref[...])
    mg = mg_ref[...]
    merged = mg[:, :d].astype(F32) * ya + mg[:, d:].astype(F32) * yb
    x1 = x_ref[...] + g1_ref[...] * _dot(merged.astype(BF16), wo_ref[...])
    x1_ref[...] = x1

    ms = jnp.mean(x1 * x1, axis=-1, keepdims=True)
    h2 = x1 * lax.rsqrt(ms + EPS) * nw_ref[...]
    h2 = h2 * (1.0 + sc_ref[...]) + sh_ref[...]
    hi = h2.astype(BF16)
    hif = hi.astype(F32)
    lo = (h2 - hif).astype(BF16)
    r_hi = _dot_nt(rwh_ref[...], hi)
    r_lo = _dot_nt(rwl_ref[...], lo)
    sc_t = _sigmoid(r_hi[:N_EXPERTS] + r_hi[N_EXPERTS:] + r_lo)
    sel_t = sc_t + rb_ref[:, 0:1]

    def row(a, e):
        return a[e:e + 1, :]

    best = None
    gi = jnp.zeros((1, tm), I32)
    for g in range(N_GROUPS):
        r = [row(sel_t, EPG * g + i) for i in range(EPG)]
        gs = None
        for i in range(EPG):
            for j in range(i + 1, EPG):
                pr = r[i] + r[j]
                gs = pr if gs is None else jnp.maximum(gs, pr)
        if best is None:
            best = gs
        else:
            upd = gs > best
            gi = jnp.where(upd, g, gi)
            best = jnp.maximum(best, gs)

    def pick(a, i):
        out = row(a, i)
        for g in range(1, N_GROUPS):
            out = jnp.where(gi == g, row(a, EPG * g + i), out)
        return out

    v = [pick(sel_t, i) for i in range(EPG)]
    s = [pick(sc_t, i) for i in range(EPG)]

    def argmax4(vals):
        bv, bi = vals[0], jnp.zeros((1, tm), I32)
        for i in range(1, EPG):
            upd = vals[i] > bv
            bi = jnp.where(upd, i, bi)
            bv = jnp.maximum(bv, vals[i])
        return bi

    i1 = argmax4(v)
    i2 = argmax4([jnp.where(i1 == i, -jnp.inf, v[i]) for i in range(EPG)])
    ia = jnp.minimum(i1, i2)
    ib = jnp.maximum(i1, i2)
    pidx = jnp.where(ia == 0, ib - 1, jnp.where(ia == 1, jnp.where(ib == 3, 3, 5), 4))
    bucket = gi * N_PAIRS + pidx

    def by_index(vals, idx):
        out = vals[0]
        for i in range(1, EPG):
            out = jnp.where(idx == i, vals[i], out)
        return out

    swap = pidx == N_PAIRS - 1
    s_lo, s_hi = by_index(s, ia), by_index(s, ib)
    s_a, s_b = jnp.where(swap, s_hi, s_lo), jnp.where(swap, s_lo, s_hi)
    gate_a = s_a / (s_a + s_b)
    gate_b = s_b / (s_a + s_b)

    brow = lax.broadcasted_iota(I32, (BUCKET_ROWS, tm), 0)
    onehot = brow == bucket
    cums = _dot(jnp.where(onehot, 1.0, 0.0).astype(BF16), tri_ref[...])
    carry = carry_ref[...]
    rank = jnp.sum(jnp.where(onehot, carry[:, 0:1] + cums, 0.0), axis=0, keepdims=True) - 1.0
    new_carry = carry + cums[:, tm - 1:tm]
    carry_ref[...] = new_carry
    cnt_ref[...] = new_carry

    route_ref[...] = jnp.concatenate(
        [bucket.astype(F32), gate_a, gate_b, rank, jnp.zeros((SUBLANES - 4, tm), F32)], axis=0)

    half = d // 2
    packed = _pack_bf16_pairs(hif)
    for cpart in range(half // LANES):
        pay_ref[cpart] = packed[:, cpart * LANES:(cpart + 1) * LANES]
    gates_t = jnp.concatenate([gate_a, gate_b, jnp.zeros((LANES - 2, tm), F32)], axis=0)
    pay_ref[half // LANES] = lax.bitcast_convert_type(gates_t.T, U32)


def _merge(o_attn, hm, mg, x, g1, sc2, sh2, nw, wa, wm, wo, rwh, rwl, rb, tri, l, seq):
    t, d = x.shape
    tm = tri.shape[0]
    tpb = seq // tm
    row = lambda i: (i, 0)
    bsel = lambda i: (i // tpb, 0, 0)
    wsel = lambda i: (l, 0, 0)
    const = lambda i: (0, 0)
    hw = o_attn.shape[1]
    return pl.pallas_call(
        _merge_kernel,
        grid=(t // tm,),
        in_specs=[
            pl.BlockSpec((tm, hw), row),
            pl.BlockSpec((tm, hw), row),
            pl.BlockSpec((tm, 2 * d), row),
            pl.BlockSpec((tm, d), row),
            pl.BlockSpec((None, 1, d), bsel),
            pl.BlockSpec((None, 1, d), bsel),
            pl.BlockSpec((None, 1, d), bsel),
            pl.BlockSpec((None, 1, d), wsel),
            pl.BlockSpec((None, hw, d), wsel),
            pl.BlockSpec((None, hw, d), wsel),
            pl.BlockSpec((None, d, d), wsel),
            pl.BlockSpec((2 * N_EXPERTS, d), const),
            pl.BlockSpec((N_EXPERTS, d), const),
            pl.BlockSpec((N_EXPERTS, LANES), const),
            pl.BlockSpec((tm, tm), const),
        ],
        out_specs=[
            pl.BlockSpec((tm, d), row),
            pl.BlockSpec((PAY_PARTS, tm, LANES), lambda i: (0, i, 0)),
            pl.BlockSpec((SUBLANES, tm), lambda i: (0, i)),
            pl.BlockSpec((BUCKET_ROWS, LANES), const),
        ],
        out_shape=[
            jax.ShapeDtypeStruct((t, d), F32),
            jax.ShapeDtypeStruct((PAY_PARTS, t, LANES), U32),
            jax.ShapeDtypeStruct((SUBLANES, t), F32),
            jax.ShapeDtypeStruct((BUCKET_ROWS, LANES), F32),
        ],
        scratch_shapes=[pltpu.VMEM((BUCKET_ROWS, LANES), F32)],
        compiler_params=_params("arbitrary"),
        name="merge_router",
    )(o_attn, hm, mg, x, g1, sc2, sh2, nw, wa, wm, wo, rwh, rwl, rb, tri)


def _sc_mesh():
    return plsc.VectorSubcoreMesh(core_axis_name="core", subcore_axis_name="subcore")


def _sc_scatter_rows(rows, dest, n_out):
    n, w = rows.shape

    @pl.kernel(out_type=jax.ShapeDtypeStruct((n_out, w), rows.dtype), mesh=_sc_mesh(), scratch_types=[])
    def scatter(x_hbm, i_hbm, o_hbm):
        def body(x_vmem, i_vmem):
            pltpu.sync_copy(x_vmem, o_hbm.at[i_vmem.at[0]])

        pltpu.emit_pipeline(
            body,
            grid=(n // SC_WINDOW,),
            in_specs=[pl.BlockSpec((SC_WINDOW, w), lambda i: (i, 0)),
                      pl.BlockSpec((1, SC_WINDOW), lambda i: (0, i))],
            out_specs=[],
            core_axis_name=("core", "subcore"),
            dimension_semantics=(pltpu.PARALLEL,),
        )(x_hbm, i_hbm)

    return scatter(rows, dest.reshape(1, n))


def _sc_gather_rows(src, idx):
    n = idx.shape[0]
    w = src.shape[1]

    @pl.kernel(out_type=jax.ShapeDtypeStruct((n, w), src.dtype), mesh=_sc_mesh(), scratch_types=[])
    def gather(x_hbm, i_hbm, o_hbm):
        def body(i_vmem, o_vmem):
            pltpu.sync_copy(x_hbm.at[i_vmem.at[0]], o_vmem)

        pltpu.emit_pipeline(
            body,
            grid=(n // SC_WINDOW,),
            in_specs=[pl.BlockSpec((1, SC_WINDOW), lambda i: (0, i))],
            out_specs=[pl.BlockSpec((SC_WINDOW, w), lambda i: (i, 0))],
            core_axis_name=("core", "subcore"),
            dimension_semantics=(pltpu.PARALLEL,),
        )(i_hbm, o_hbm)

    return gather(src, idx.reshape(1, n))


def _part_index(dest, parts, n_rows):
    return (jnp.arange(parts, dtype=I32)[:, None] * n_rows + dest[None, :]).reshape(-1)


def _residual_kernel(x_ref, y_ref, g2_ref, o_ref):
    y = _unpack_bf16_pairs(jnp.concatenate([y_ref[c] for c in range(OUT_PARTS)], axis=1))
    o_ref[...] = x_ref[...] + g2_ref[...] * y


def _residual(x1, ytok, g2, seq):
    t, d = x1.shape
    tm = min(512, seq)
    tpb = seq // tm
    return pl.pallas_call(
        _residual_kernel,
        grid=(t // tm,),
        in_specs=[
            pl.BlockSpec((tm, d), lambda i: (i, 0)),
            pl.BlockSpec((OUT_PARTS, tm, LANES), lambda i: (0, i, 0)),
            pl.BlockSpec((None, 1, d), lambda i: (i // tpb, 0, 0)),
        ],
        out_specs=pl.BlockSpec((tm, d), lambda i: (i, 0)),
        out_shape=jax.ShapeDtypeStruct((t, d), F32),
        compiler_params=_params("arbitrary"),
        name="residual",
    )(x1, ytok, g2)


def _expert_kernel(ea_ref, eb_ref, nr_ref, xs_ref, wga_ref, wua_ref, wda_ref, wgb_ref, wub_ref, wdb_ref, ys_ref,
                   ga_ref, ua_ref, da_ref, gb_ref, ub_ref, db_ref):
    j = pl.program_id(0)
    nr = nr_ref[0]
    prev = jnp.maximum(j - 1, 0)

    @pl.when((j == 0) | (ea_ref[j] != ea_ref[prev]))
    def _():
        ga_ref[...] = wga_ref[...].astype(BF16)
        ua_ref[...] = wua_ref[...].astype(BF16)
        da_ref[...] = wda_ref[...].astype(BF16)

    @pl.when((j == 0) | (eb_ref[j] != eb_ref[prev]))
    def _():
        gb_ref[...] = wgb_ref[...].astype(BF16)
        ub_ref[...] = wub_ref[...].astype(BF16)
        db_ref[...] = wdb_ref[...].astype(BF16)

    @pl.when(j < nr)
    def _():
        x = _unpack_bf16_pairs(jnp.concatenate([xs_ref[c] for c in range(PAY_PARTS - 1)], axis=1)).astype(BF16)
        gl = lax.bitcast_convert_type(xs_ref[PAY_PARTS - 1], F32)

        def ffn(wg_ref, wu_ref, wd_ref):
            gte = _dot(x, wg_ref[...])
            act = gte * _sigmoid(gte) * _dot(x, wu_ref[...])
            return _dot(act.astype(BF16), wd_ref[...])

        y = _pack_bf16_pairs(gl[:, 0:1] * ffn(ga_ref, ua_ref, da_ref) + gl[:, 1:2] * ffn(gb_ref, ub_ref, db_ref))
        for c in range(OUT_PARTS):
            ys_ref[c] = y[:, c * LANES:(c + 1) * LANES]

    @pl.when(j >= nr)
    def _():
        ys_ref[...] = jnp.zeros(ys_ref.shape, U32)


def _experts(blk_ea, blk_eb, n_real, xs, wg, wu, wd, d):
    n_rows = xs.shape[1]
    nblk = n_rows // EXPERT_BLOCK
    f = wg.shape[2]
    grid_spec = pltpu.PrefetchScalarGridSpec(
        num_scalar_prefetch=3,
        grid=(nblk,),
        in_specs=[
            pl.BlockSpec((PAY_PARTS, EXPERT_BLOCK, LANES), lambda j, ea, eb, nr: (0, j, 0)),
            pl.BlockSpec((None, d, f), lambda j, ea, eb, nr: (ea[j], 0, 0)),
            pl.BlockSpec((None, d, f), lambda j, ea, eb, nr: (ea[j], 0, 0)),
            pl.BlockSpec((None, f, d), lambda j, ea, eb, nr: (ea[j], 0, 0)),
            pl.BlockSpec((None, d, f), lambda j, ea, eb, nr: (eb[j], 0, 0)),
            pl.BlockSpec((None, d, f), lambda j, ea, eb, nr: (eb[j], 0, 0)),
            pl.BlockSpec((None, f, d), lambda j, ea, eb, nr: (eb[j], 0, 0)),
        ],
        out_specs=pl.BlockSpec((OUT_PARTS, EXPERT_BLOCK, LANES), lambda j, ea, eb, nr: (0, j, 0)),
        scratch_shapes=[pltpu.VMEM((d, f), BF16), pltpu.VMEM((d, f), BF16), pltpu.VMEM((f, d), BF16)] * 2,
    )
    return pl.pallas_call(
        _expert_kernel,
        grid_spec=grid_spec,
        out_shape=jax.ShapeDtypeStruct((OUT_PARTS, n_rows, LANES), U32),
        compiler_params=_params("arbitrary"),
        name="experts",
    )(blk_ea, blk_eb, n_real, xs, wg, wu, wd, wg, wu, wd)


_PAIR_A = (0, 0, 0, 1, 2, 2)
_PAIR_B = (1, 2, 3, 3, 3, 1)


def kernel(x, c, positions, ada_w, ada_b, norm_mix_w, norm_ffn_w, w_in, b_igate, b_fgate, q_norm_w, k_norm_w,
           sinks, conv_w, conv_b, mlstm_norm_w, w_attn_up, w_mlstm_up, w_out, router_w, router_bias,
           w_gate, w_up, w_down):
    batch, seq, d = x.shape
    depth = w_in.shape[0]
    t = batch * seq
    qw = N_HEADS * HEAD_DIM
    kvw = N_KV * HEAD_DIM
    mw = M_HEADS * M_DIM

    o = 0
    cols = {}
    for name, wdt in (("q", qw), ("k", kvw), ("v", kvw), ("mqk", 2 * mw), ("mv", mw), ("mi", M_HEADS),
                      ("mf", M_HEADS), ("mo", mw), ("ga", d), ("gb", d)):
        cols[name] = (o, o + wdt)
        o += wdt

    def wc(name, lo=0, hi=None):
        s, e = cols[name]
        return w_in[:, :, s + lo:(s + hi if hi is not None else e)]

    w_a = jnp.concatenate([wc("q"), wc("k"), wc("v")], axis=2).astype(BF16)
    w_m = jnp.concatenate([wc("mqk"), wc("mv"), wc("mo")], axis=2).astype(BF16)
    w_g = jnp.concatenate([wc("mi"), wc("mf"), jnp.zeros((depth, d, LANES - 2 * M_HEADS), F32)], axis=2).astype(BF16)
    w_mg = jnp.concatenate([wc("ga"), wc("gb")], axis=2).astype(BF16)
    w_au = w_attn_up.astype(BF16)
    w_mu = w_mlstm_up.astype(BF16)
    w_o = w_out.astype(BF16)
    n_e = w_gate.shape[1]
    w_g8 = w_gate.reshape(depth * n_e, d, -1)
    w_u8 = w_up.reshape(depth * n_e, d, -1)
    w_d = w_down.reshape(depth * n_e, -1, d)

    rw_t = router_w.astype(F32).T
    rw_top = rw_t.astype(BF16)
    rw_hi = jnp.concatenate([rw_top, (rw_t - rw_top.astype(F32)).astype(BF16)], axis=0)
    rw_lo = rw_top
    rb = jnp.broadcast_to(router_bias.astype(F32)[:, None], (n_e, LANES))

    qn_w = jnp.tile(q_norm_w, (1, N_HEADS)).reshape(depth, 1, qw)
    kn_w = jnp.tile(k_norm_w, (1, N_KV)).reshape(depth, 1, kvw)
    seg = jnp.arange(qw) // HEAD_DIM
    bdq = jnp.where(seg[:, None] == seg[None, :], 1.0 / HEAD_DIM, 0.0).astype(BF16)
    bdk = bdq[:kvw, :kvw]

    inv_freq = ROPE_THETA ** (-(jnp.arange(0, ROPE_DIM, 2, dtype=F32) / ROPE_DIM))
    ang = positions.astype(F32).reshape(1, t) * inv_freq[:, None]
    cos8, sin8 = jnp.cos(ang).T, jnp.sin(ang).T
    pad1 = jnp.ones((t, HEAD_DIM - ROPE_DIM), F32)
    pad0 = jnp.zeros((t, HEAD_DIM - ROPE_DIM), F32)
    cos_t = jnp.tile(jnp.concatenate([cos8, cos8, pad1], axis=1), (1, LANES // HEAD_DIM))
    sin_t = jnp.tile(jnp.concatenate([-sin8, sin8, pad0], axis=1), (1, LANES // HEAD_DIM))

    gate_bias = jnp.concatenate([b_igate, b_fgate], axis=1).astype(F32)
    bcol = jnp.broadcast_to(gate_bias[:, :, None], (depth, 2 * M_HEADS, LANES))

    tm_merge = min(1024, seq)
    ii = jnp.arange(tm_merge)
    tri = (ii[:, None] <= ii[None, :]).astype(BF16)

    n_blk = (t + N_BUCKETS * (EXPERT_BLOCK - 1)) // EXPERT_BLOCK + 1
    n_rows = n_blk * EXPERT_BLOCK
    pair_a = jnp.asarray(_PAIR_A, I32)
    pair_b = jnp.asarray(_PAIR_B, I32)

    c_pad = jnp.zeros((SUBLANES, d), F32).at[:batch].set(c)
    mod = _ada_mod(c_pad, ada_w, ada_b)[:, :batch]

    xf = x.reshape(t, d)
    moe = None
    for l in range(depth):
        sh1, sc1, g1, sh2, sc2, g2 = [m.reshape(batch, 1, d) for m in jnp.split(mod[l], 6, axis=-1)]

        outs = _inproj(xf, moe, sc1, sh1, norm_mix_w.reshape(depth, 1, d), w_a, w_m, w_g, w_mg, l, seq)
        a_in, m_in, mg, grow = outs[:4]
        if moe is not None:
            xf = outs[4]
        o_attn = _attention(a_in, cos_t, sin_t, sinks[l], qn_w[l], kn_w[l], bdq, bdk, batch, seq)
        hm = _mlstm(m_in, grow, conv_w[l], conv_b[l].reshape(1, -1), bcol[l], mlstm_norm_w[l].reshape(1, mw),
                    batch, seq)
        x1, pay, route, cnt = _merge(o_attn, hm, mg, xf, g1, sc2, sh2, norm_ffn_w.reshape(depth, 1, d),
                                     w_au, w_mu, w_o, rw_hi, rw_lo, rb, tri, l, seq)

        counts = cnt[:N_BUCKETS, 0].astype(I32)
        padded = (counts + EXPERT_BLOCK - 1) // EXPERT_BLOCK * EXPERT_BLOCK
        pad_ends = jnp.cumsum(padded)
        pad_starts = pad_ends - padded
        bucket = route[0].astype(I32)
        dest = pad_starts[bucket] + route[3].astype(I32)
        blk_start = jnp.arange(n_blk, dtype=I32) * EXPERT_BLOCK
        blk_bucket = jnp.minimum(jnp.sum((pad_ends[None, :] <= blk_start[:, None]).astype(I32), axis=1), N_BUCKETS - 1)
        grp = blk_bucket // N_PAIRS
        blk_ea = (l * n_e + grp * EPG + pair_a[blk_bucket % N_PAIRS]).astype(I32)
        blk_eb = (l * n_e + grp * EPG + pair_b[blk_bucket % N_PAIRS]).astype(I32)
        n_real = (pad_ends[-1:] // EXPERT_BLOCK).astype(I32)

        xs = _sc_scatter_rows(pay.reshape(PAY_PARTS * t, LANES), _part_index(dest, PAY_PARTS, n_rows),
                              PAY_PARTS * n_rows).reshape(PAY_PARTS, n_rows, LANES)
        ys = _experts(blk_ea, blk_eb, n_real, xs, w_g8, w_u8, w_d, d)
        ytok = _sc_gather_rows(ys.reshape(OUT_PARTS * n_rows, LANES),
                               _part_index(dest, OUT_PARTS, n_rows)).reshape(OUT_PARTS, t, LANES)
        xf, moe = x1, (ytok, g2)
    return _residual(xf, moe[0], moe[1], seq).reshape(batch, seq, d)
```

```python
import functools

import jax
import jax.numpy as jnp
from jax import lax
from jax.experimental import pallas as pl
from jax.experimental.pallas import tpu as pltpu
from jax.experimental.pallas import tpu_sc as plsc

F32 = jnp.float32
BF16 = jnp.bfloat16
U32 = jnp.uint32
I32 = jnp.int32
HIGHEST = lax.Precision.HIGHEST

HEAD_DIM = 64
N_HEADS = 8
N_KV = 2
ROPE_DIM = 16
ROPE_THETA = 500000.0
ATTN_BLOCK = 128
M_HEADS = 4
M_DIM = 128
CONV_K = 4
N_EXPERTS = 16
N_GROUPS = 4
EPG = 4
EPS = 1e-6

LANES = 128
SUBLANES = 8

CHUNK = 128
CHUNKS_PER_STEP = 4
GATE_CHUNKS_PER_STEP = 4
N_PAIRS = 6
N_BUCKETS = N_GROUPS * N_PAIRS
BUCKET_ROWS = 32
EXPERT_BLOCK = 256
PAY_PARTS = 5
OUT_PARTS = 4
SC_WINDOW = 128
VMEM_LIMIT = 56 * 1024 * 1024


def _dot(a, b, precision=None):
    return jnp.dot(a, b, preferred_element_type=F32, precision=precision)


def _dot_nt(a, b):
    return lax.dot_general(a, b, (((1,), (1,)), ((), ())), preferred_element_type=F32)


def _sigmoid(x):
    return 1.0 / (1.0 + jnp.exp(-x))


def _log_sigmoid(x):
    return jnp.minimum(x, 0.0) - jnp.log1p(jnp.exp(-jnp.abs(x)))


def _pack_bf16_pairs(v, rounded=False):
    n = v.shape[1] // 2
    bits = lax.bitcast_convert_type(v if rounded else v.astype(BF16).astype(F32), U32)
    return bits[:, :n] | (bits[:, n:] >> 16)


def _unpack_bf16_pairs(w):
    hi = lax.bitcast_convert_type(w & jnp.uint32(0xFFFF0000), F32)
    lo = lax.bitcast_convert_type(w << 16, F32)
    return jnp.concatenate([hi, lo], axis=1)


def _params(*sem):
    return pltpu.CompilerParams(dimension_semantics=sem, vmem_limit_bytes=VMEM_LIMIT)


def _ada_kernel(c_ref, w_ref, b_ref, o_ref):
    c = c_ref[...]
    ca = c * _sigmoid(c)
    o_ref[0] = _dot(ca, w_ref[0], HIGHEST) + b_ref[0]


def _ada_mod(c_pad, ada_w, ada_b):
    depth, d, n = ada_w.shape
    tn = 1536
    return pl.pallas_call(
        _ada_kernel,
        grid=(depth, n // tn),
        in_specs=[
            pl.BlockSpec((SUBLANES, d), lambda l, j: (0, 0)),
            pl.BlockSpec((1, d, tn), lambda l, j: (l, 0, j)),
            pl.BlockSpec((1, 1, tn), lambda l, j: (l, 0, j)),
        ],
        out_specs=pl.BlockSpec((1, SUBLANES, tn), lambda l, j: (l, 0, j)),
        out_shape=jax.ShapeDtypeStruct((depth, SUBLANES, n), F32),
        compiler_params=_params("arbitrary", "arbitrary"),
        name="ada_mod",
    )(c_pad, ada_w, ada_b.reshape(depth, 1, n))


def _inproj_kernel(*refs, fuse_residual):
    if fuse_residual:
        (x_ref, y_ref, g2_ref, sc_ref, sh_ref, nw_ref, wa_ref, wm_ref, wg_ref, wmg_ref,
         a_ref, m_ref, mg_ref, gr_ref, xo_ref, g_ref) = refs
        y = _unpack_bf16_pairs(jnp.concatenate([y_ref[c] for c in range(OUT_PARTS)], axis=1))
        x = x_ref[...] + g2_ref[...] * y
        xo_ref[...] = x
    else:
        (x_ref, sc_ref, sh_ref, nw_ref, wa_ref, wm_ref, wg_ref, wmg_ref,
         a_ref, m_ref, mg_ref, gr_ref, g_ref) = refs
        x = x_ref[...]
    ms = jnp.mean(x * x, axis=-1, keepdims=True)
    h = x * lax.rsqrt(ms + EPS) * nw_ref[...]
    h = h * (1.0 + sc_ref[...]) + sh_ref[...]
    hb = h.astype(BF16)
    a_ref[...] = _dot(hb, wa_ref[...]).astype(BF16)
    m_ref[...] = _dot(hb, wm_ref[...]).astype(BF16)
    mg_ref[...] = _sigmoid(_dot(hb, wmg_ref[...])).astype(BF16)
    g_ref[...] = _dot(hb, wg_ref[...])
    gr_ref[...] = g_ref[...].T[:SUBLANES, :]


def _inproj(x, moe, sc, sh, nw, wa, wm, wg, wmg, l, seq):
    t, d = x.shape
    tm = min(1024, seq)
    tpb = seq // tm
    row = lambda i: (i, 0)
    bsel = lambda i: (i // tpb, 0, 0)
    wsel = lambda i: (l, 0, 0)
    once = pl.Buffered(1)
    na, nm, ng, nmg = wa.shape[2], wm.shape[2], wg.shape[2], wmg.shape[2]
    fuse = moe is not None
    moe_specs = [pl.BlockSpec((OUT_PARTS, tm, LANES), lambda i: (0, i, 0)), pl.BlockSpec((None, 1, d), bsel)]
    return pl.pallas_call(
        functools.partial(_inproj_kernel, fuse_residual=fuse),
        grid=(t // tm,),
        in_specs=[pl.BlockSpec((tm, d), row)] + (moe_specs if fuse else []) + [
            pl.BlockSpec((None, 1, d), bsel),
            pl.BlockSpec((None, 1, d), bsel),
            pl.BlockSpec((None, 1, d), wsel),
            pl.BlockSpec((None, d, na), wsel, pipeline_mode=once),
            pl.BlockSpec((None, d, nm), wsel, pipeline_mode=once),
            pl.BlockSpec((None, d, ng), wsel, pipeline_mode=once),
            pl.BlockSpec((None, d, nmg), wsel, pipeline_mode=once),
        ],
        out_specs=[
            pl.BlockSpec((tm, na), row),
            pl.BlockSpec((tm, nm), row),
            pl.BlockSpec((tm, nmg), row),
            pl.BlockSpec((SUBLANES, tm), lambda i: (0, i)),
        ] + ([pl.BlockSpec((tm, d), row)] if fuse else []),
        out_shape=[
            jax.ShapeDtypeStruct((t, na), BF16),
            jax.ShapeDtypeStruct((t, nm), BF16),
            jax.ShapeDtypeStruct((t, nmg), BF16),
            jax.ShapeDtypeStruct((SUBLANES, t), F32),
        ] + ([jax.ShapeDtypeStruct((t, d), F32)] if fuse else []),
        scratch_shapes=[pltpu.VMEM((tm, ng), F32)],
        compiler_params=_params("arbitrary"),
        name="inproj",
    )(x, *(moe if fuse else ()), sc, sh, nw, wa, wm, wg, wmg)


def _rope(t, cos, sin):
    w = t.shape[1]
    reps = w // LANES
    cosw = jnp.concatenate([cos] * reps, axis=1) if reps > 1 else cos
    sinw = jnp.concatenate([sin] * reps, axis=1) if reps > 1 else sin
    lane = lax.broadcasted_iota(I32, t.shape, 1)
    half = ROPE_DIM // 2
    up = pltpu.roll(t, w - half, axis=1)
    dn = pltpu.roll(t, half, axis=1)
    partner = jnp.where((lane % ROPE_DIM) < half, up, dn)
    return t * cosw + partner * sinw


def _head_norm(t, bd, w):
    ms = _dot((t * t).astype(BF16), bd)
    return t * lax.rsqrt(ms + EPS) * w


def _attn_kernel(sink_ref, cur_ref, prev_ref, cos_ref, sin_ref, cosp_ref, sinp_ref,
                 qw_ref, kw_ref, bdq_ref, bdk_ref, o_ref):
    tq = cur_ref.shape[0]
    nj = tq // ATTN_BLOCK
    qw = N_HEADS * HEAD_DIM
    kw = N_KV * HEAD_DIM
    blk0 = pl.program_id(1) * nj

    cur = cur_ref[...]
    q = cur[:, :qw].astype(F32)
    kc = cur[:, qw:qw + kw].astype(F32)
    vc = cur[:, qw + kw:].astype(F32)
    prev = prev_ref[...]
    kp = prev[:, :kw].astype(F32)
    vp = prev[:, kw:].astype(F32)

    cos, sin = cos_ref[...], sin_ref[...]
    q = _rope(_head_norm(q, bdq_ref[...], qw_ref[...]), cos, sin)
    kc = _rope(_head_norm(kc, bdk_ref[...], kw_ref[...]), cos, sin)
    kp = _rope(_head_norm(kp, bdk_ref[...], kw_ref[...]), cosp_ref[...], sinp_ref[...])
    qb = q.astype(BF16)

    def both_halves(x2):
        swapped = pltpu.roll(x2, HEAD_DIM, axis=1)
        first = lax.broadcasted_iota(I32, x2.shape, 1) < HEAD_DIM
        return jnp.concatenate([jnp.where(first, x2, swapped), jnp.where(first, swapped, x2)], axis=1).astype(BF16)

    k_all = both_halves(jnp.concatenate([kp, kc], axis=0))
    v_all = both_halves(jnp.concatenate([vp, vc], axis=0))

    lane = lax.broadcasted_iota(I32, (ATTN_BLOCK, LANES), 1)
    lo = lane < HEAD_DIM
    zero = jnp.zeros((ATTN_BLOCK, LANES), BF16)
    g_heads = N_HEADS // N_KV
    ri = lax.broadcasted_iota(I32, (g_heads * ATTN_BLOCK, ATTN_BLOCK), 0) % ATTN_BLOCK
    ci = lax.broadcasted_iota(I32, (g_heads * ATTN_BLOCK, ATTN_BLOCK), 1)
    from_prev = ci > ri
    head_row = lax.broadcasted_iota(I32, (g_heads * ATTN_BLOCK, 1), 0) // ATTN_BLOCK
    ones_v = jnp.ones((2 * ATTN_BLOCK, LANES), BF16)

    tiles = [(j, g) for j in range(nj) for g in range(N_KV)]
    scores = {}
    for j, g in tiles:
        rows = slice(j * ATTN_BLOCK, (j + 1) * ATTN_BLOCK)
        band = slice(j * ATTN_BLOCK, (j + 2) * ATTN_BLOCK)
        qp0 = qb[rows, (2 * g) * LANES:(2 * g + 1) * LANES]
        qp1 = qb[rows, (2 * g + 1) * LANES:(2 * g + 2) * LANES]
        q4 = jnp.concatenate([jnp.where(lo, qp0, zero), jnp.where(lo, zero, qp0),
                              jnp.where(lo, qp1, zero), jnp.where(lo, zero, qp1)], axis=0)
        scores[j, g] = _dot_nt(q4, k_all[band, g * LANES:(g + 1) * LANES])

    probs, sink_term = {}, {}
    for j, g in tiles:
        prev_ok = ci > ri + (1 - jnp.minimum(blk0 + j, 1)) * ATTN_BLOCK
        s2 = scores[j, g]
        s = jnp.where(prev_ok, s2[:, :ATTN_BLOCK], jnp.where(from_prev, -jnp.inf, s2[:, ATTN_BLOCK:]))
        sink = jnp.full((g_heads * ATTN_BLOCK, 1), sink_ref[g_heads * g], F32)
        for r in range(1, g_heads):
            sink = jnp.where(head_row == r, sink_ref[g_heads * g + r], sink)
        m = jnp.maximum(jnp.max(s, axis=-1, keepdims=True), sink)
        p = jnp.exp(s - m)
        probs[j, g] = jnp.concatenate([jnp.where(from_prev, p, 0.0), jnp.where(from_prev, 0.0, p)],
                                      axis=1).astype(BF16)
        sink_term[j, g] = jnp.exp(sink - m)

    for j, g in tiles:
        rows = slice(j * ATTN_BLOCK, (j + 1) * ATTN_BLOCK)
        band = slice(j * ATTN_BLOCK, (j + 2) * ATTN_BLOCK)
        o8 = _dot(probs[j, g], jnp.concatenate([v_all[band, g * LANES:(g + 1) * LANES], ones_v], axis=1))
        o4 = o8[:, :LANES] / (o8[:, LANES:] + sink_term[j, g])
        b = ATTN_BLOCK
        o_ref[rows, (2 * g) * LANES:(2 * g + 1) * LANES] = jnp.where(lo, o4[0:b], o4[b:2 * b]).astype(BF16)
        o_ref[rows, (2 * g + 1) * LANES:(2 * g + 2) * LANES] = jnp.where(
            lo, o4[2 * b:3 * b], o4[3 * b:4 * b]).astype(BF16)


def _attention(a_in, cos_t, sin_t, sinks_l, qw, kw, bdq, bdk, batch, seq):
    t = a_in.shape[0]
    tq = min(512, seq)
    nj = tq // ATTN_BLOCK
    tpb = seq // tq
    bpb = seq // ATTN_BLOCK
    qwid = N_HEADS * HEAD_DIM
    kvw = 2 * N_KV * HEAD_DIM
    cur = lambda b, i: (b * tpb + i, 0)
    prv = lambda b, i: (b * bpb + jnp.maximum(i * nj - 1, 0), qwid // kvw)
    prv0 = lambda b, i: (b * bpb + jnp.maximum(i * nj - 1, 0), 0)
    const = lambda b, i: (0, 0)
    return pl.pallas_call(
        _attn_kernel,
        grid=(batch, tpb),
        in_specs=[
            pl.BlockSpec(memory_space=pltpu.SMEM),
            pl.BlockSpec((tq, qwid + kvw), cur),
            pl.BlockSpec((ATTN_BLOCK, kvw), prv),
            pl.BlockSpec((tq, LANES), cur),
            pl.BlockSpec((tq, LANES), cur),
            pl.BlockSpec((ATTN_BLOCK, LANES), prv0),
            pl.BlockSpec((ATTN_BLOCK, LANES), prv0),
            pl.BlockSpec((1, qwid), const),
            pl.BlockSpec((1, kvw // 2), const),
            pl.BlockSpec((qwid, qwid), const),
            pl.BlockSpec((kvw // 2, kvw // 2), const),
        ],
        out_specs=pl.BlockSpec((tq, qwid), cur),
        out_shape=jax.ShapeDtypeStruct((t, qwid), BF16),
        compiler_params=_params("arbitrary", "arbitrary"),
        name="swa_attention",
    )(sinks_l, a_in, a_in, cos_t, sin_t, cos_t, sin_t, qw, kw, bdq, bdk)


def _mlstm_kernel(min_ref, gr_ref, cw_ref, cb_ref, bcol_ref, nw_ref,
                  hm_ref, ext_ref, q_ref, kt_ref, st_ref, mx_ref, ab_ref, bc_ref):
    tt = min_ref.shape[0]
    mw = M_HEADS * M_DIM
    nchunks = tt // CHUNK

    @pl.when(pl.program_id(1) == 0)
    def _():
        ext_ref[0:SUBLANES, :] = jnp.zeros((SUBLANES, 2 * mw), F32)
        st_ref[...] = jnp.zeros(st_ref.shape, F32)
        mx_ref[...] = jnp.zeros(mx_ref.shape, F32)

    def conv_block(cols):
        u = min_ref[:, cols].astype(F32)
        ext_ref[SUBLANES:SUBLANES + tt, cols] = u
        acc = cb_ref[:, cols] + cw_ref[CONV_K - 1:CONV_K, cols] * u
        for jj in range(CONV_K - 1):
            off = SUBLANES - (CONV_K - 1) + jj
            acc = acc + cw_ref[jj:jj + 1, cols] * ext_ref[off:off + tt, cols]
        ext_ref[0:SUBLANES, cols] = u[tt - SUBLANES:tt, :]
        return acc * _sigmoid(acc)

    def q_body(h, carry):
        cols = pl.ds(pl.multiple_of(h * M_DIM, M_DIM), M_DIM)
        q_ref[:, cols] = conv_block(cols).astype(BF16)
        return carry

    def k_body(h, carry):
        off = pl.multiple_of(h * M_DIM, M_DIM)
        act = conv_block(pl.ds(mw + off, M_DIM)) * (M_DIM ** -0.5)
        for j in range(nchunks):
            kt_ref[pl.ds(off, M_DIM), j * CHUNK:(j + 1) * CHUNK] = act[j * CHUNK:(j + 1) * CHUNK, :].T
        return carry

    lax.fori_loop(0, M_HEADS, q_body, 0)
    lax.fori_loop(0, M_HEADS, k_body, 0)

    ri = lax.broadcasted_iota(I32, (CHUNK, CHUNK), 0)
    ci = lax.broadcasted_iota(I32, (CHUNK, CHUNK), 1)
    causal = ci <= ri
    triu = jnp.where(ri <= ci, 1.0, 0.0).astype(BF16)
    ones_half = jnp.ones((CHUNK, M_DIM), BF16)
    mean_mat = jnp.full((M_DIM, M_DIM), 1.0 / M_DIM, BF16)
    sub = lax.broadcasted_iota(I32, (SUBLANES, CHUNK), 0)
    heads = range(M_HEADS)

    pad_rows = jnp.zeros((CHUNK - SUBLANES, CHUNK), F32)
    zero_rows = jnp.zeros((SUBLANES, CHUNK), F32)

    def gate_body(jg, carry):
        for u_ in range(GATE_CHUNKS_PER_STEP):
            rs = pl.ds(pl.multiple_of((jg * GATE_CHUNKS_PER_STEP + u_) * CHUNK, CHUNK), CHUNK)
            gr = gr_ref[:, rs] + bcol_ref[...]
            ls = _log_sigmoid(gr)
            ls1 = ls.astype(BF16).astype(F32)
            ls2 = (ls - ls1).astype(BF16).astype(F32)
            pieces = jnp.concatenate([ls1, ls2, ls - ls1 - ls2, zero_rows], axis=0).astype(BF16)
            sums = _dot(pieces, triu)
            br = sums[0:SUBLANES] + sums[SUBLANES:2 * SUBLANES] + sums[2 * SUBLANES:3 * SUBLANES]
            ab = jnp.where(sub < M_HEADS, gr - pltpu.roll(br, M_HEADS, axis=0), br)
            ab_ref[:, rs] = ab
            bc_ref[rs, :] = jnp.concatenate([ab, pad_rows], axis=0).T
        return carry

    lax.fori_loop(0, nchunks // GATE_CHUNKS_PER_STEP, gate_body, 0)

    def group_body(cg, carry):
        rows, ab = [], []
        for u_ in range(CHUNKS_PER_STEP):
            r0 = pl.multiple_of((cg * CHUNKS_PER_STEP + u_) * CHUNK, CHUNK)
            rows.append(pl.ds(r0, CHUNK))
            ab.append(ab_ref[:, rows[u_]])
        lanes = [(u_, h) for u_ in range(CHUNKS_PER_STEP) for h in heads]
        a_r = {(u_, h): ab[u_][h:h + 1, :] for u_, h in lanes}
        b_last = {(u_, h): ab[u_][M_HEADS + h:M_HEADS + h + 1, CHUNK - 1:CHUNK] for u_, h in lanes}

        m_prev, a_max, a_dec, s_in = {}, {}, {}, {}
        m_run = [mx_ref[h][0:1, 0:1] for h in heads]
        for k in lanes:
            u_, h = k
            m_prev[k] = m_run[h]
            a_max[k] = jnp.max(a_r[k], axis=-1, keepdims=True)
            m_loc = b_last[k] + a_max[k]
            m_new = jnp.maximum(b_last[k] + m_prev[k], m_loc)
            a_dec[k] = jnp.exp(b_last[k] + m_prev[k] - m_new)
            s_in[k] = jnp.exp(m_loc - m_new)
            m_run[h] = m_new
        for h in heads:
            mx_ref[h] = jnp.broadcast_to(m_run[h], (SUBLANES, LANES))

        q, v_ext, s_qk, kv = {}, {}, {}, {}
        for k in lanes:
            u_, h = k
            rs = rows[u_]
            q[k] = q_ref[rs, h * M_DIM:(h + 1) * M_DIM]
            kt = kt_ref[h * M_DIM:(h + 1) * M_DIM, rs]
            v = min_ref[rs, 2 * mw + h * M_DIM:2 * mw + (h + 1) * M_DIM]
            v_ext[k] = jnp.concatenate([v, ones_half], axis=1)
            s_qk[k] = _dot(q[k], kt.astype(BF16))
            e_r = jnp.exp(a_r[k] - a_max[k])
            kv[k] = _dot((kt * e_r).astype(BF16), v_ext[k])

        thr, qk, inter = {}, {}, {}
        for k in lanes:
            u_, h = k
            a_mat = jnp.where(causal, a_r[k], -jnp.inf)
            mu = jnp.maximum(jnp.max(a_mat, axis=-1, keepdims=True), m_prev[k])
            b_c = bc_ref[rows[u_], M_HEADS + h:M_HEADS + h + 1]
            thr[k] = jnp.broadcast_to(jnp.exp(-(b_c + mu)), (CHUNK, M_DIM))
            mu_b = jnp.broadcast_to(mu, (CHUNK, CHUNK))
            inter[k] = jnp.exp(m_prev[k] - mu_b)
            qk[k] = (s_qk[k] * jnp.exp(a_mat - mu_b)).astype(BF16)

        q_state = {}
        state = [st_ref[h] for h in heads]
        for k in lanes:
            u_, h = k
            q_state[k] = _dot(q[k], state[h].astype(BF16))
            state[h] = a_dec[k] * state[h] + s_in[k] * kv[k]
        for h in heads:
            st_ref[h] = state[h]

        for k in lanes:
            u_, h = k
            hs = slice(h * M_DIM, (h + 1) * M_DIM)
            num = _dot(qk[k], v_ext[k])
            den = jnp.maximum(jnp.abs(num[:, M_DIM:] + inter[k] * q_state[k][:, M_DIM:]), thr[k])
            hh = (num[:, :M_DIM] + inter[k] * q_state[k][:, :M_DIM]) / den
            msq = _dot((hh * hh).astype(BF16), mean_mat)
            hn = hh * lax.rsqrt(msq + EPS) * nw_ref[:, hs]
            og = min_ref[rows[u_], 3 * mw + h * M_DIM:3 * mw + (h + 1) * M_DIM].astype(F32)
            hm_ref[rows[u_], hs] = (_sigmoid(og) * hn).astype(BF16)
        return carry

    lax.fori_loop(0, nchunks // CHUNKS_PER_STEP, group_body, 0)


def _mlstm(m_in, grow, conv_w, conv_b, bcol, nw, batch, seq):
    t = m_in.shape[0]
    tt = min(1024, seq)
    tpb = seq // tt
    mw = M_HEADS * M_DIM
    cur = lambda b, i: (b * tpb + i, 0)
    const = lambda b, i: (0, 0)
    return pl.pallas_call(
        _mlstm_kernel,
        grid=(batch, tpb),
        in_specs=[
            pl.BlockSpec((tt, 4 * mw), cur),
            pl.BlockSpec((SUBLANES, tt), lambda b, i: (0, b * tpb + i)),
            pl.BlockSpec((CONV_K, 2 * mw), const),
            pl.BlockSpec((1, 2 * mw), const),
            pl.BlockSpec((SUBLANES, LANES), const),
            pl.BlockSpec((1, mw), const),
        ],
        out_specs=pl.BlockSpec((tt, mw), cur),
        out_shape=jax.ShapeDtypeStruct((t, mw), BF16),
        scratch_shapes=[
            pltpu.VMEM((tt + SUBLANES, 2 * mw), F32),
            pltpu.VMEM((tt, mw), BF16),
            pltpu.VMEM((mw, tt), F32),
            pltpu.VMEM((M_HEADS, M_DIM, 2 * M_DIM), F32),
            pltpu.VMEM((M_HEADS, SUBLANES, LANES), F32),
            pltpu.VMEM((SUBLANES, tt), F32),
            pltpu.VMEM((tt, LANES), F32),
        ],
        compiler_params=_params("arbitrary", "arbitrary"),
        name="mlstm",
    )(m_in, grow, conv_w, conv_b, bcol, nw)


def _merge_kernel(o_ref, hm_ref, mg_ref, x_ref, g1_ref, sc_ref, sh_ref, nw_ref,
                  wa_ref, wm_ref, wo_ref, rwh_ref, rwl_ref, rb_ref, tri_ref,
                  x1_ref, pay_ref, route_ref, cnt_ref, carry_ref):
    tm, d = x_ref.shape

    @pl.when(pl.program_id(0) == 0)
    def _():
        carry_ref[...] = jnp.zeros(carry_ref.shape, F32)

    ya = _dot(o_ref[...], wa_ref[...])
    yb = _dot(hm_ref[...], wm_ref[...])
    mg = mg_ref[...]
    merged = mg[:, :d].astype(F32) * ya + mg[:, d:].astype(F32) * yb
    x1 = x_ref[...] + g1_ref[...] * _dot(merged.astype(BF16), wo_ref[...])
    x1_ref[...] = x1

    ms = jnp.mean(x1 * x1, axis=-1, keepdims=True)
    h2 = x1 * lax.rsqrt(ms + EPS) * nw_ref[...]
    h2 = h2 * (1.0 + sc_ref[...]) + sh_ref[...]
    hi = h2.astype(BF16)
    hif = hi.astype(F32)
    lo = (h2 - hif).astype(BF16)
    r_hi = _dot_nt(rwh_ref[...], hi)
    r_lo = _dot_nt(rwl_ref[...], lo)
    sc_t = _sigmoid(r_hi[:N_EXPERTS] + r_hi[N_EXPERTS:] + r_lo)
    sel_t = sc_t + rb_ref[:, 0:1]

    def row(a, e):
        return a[e:e + 1, :]

    best = None
    gi = jnp.zeros((1, tm), I32)
    for g in range(N_GROUPS):
        r = [row(sel_t, EPG * g + i) for i in range(EPG)]
        gs = None
        for i in range(EPG):
            for j in range(i + 1, EPG):
                pr = r[i] + r[j]
                gs = pr if gs is None else jnp.maximum(gs, pr)
        if best is None:
            best = gs
        else:
            upd = gs > best
            gi = jnp.where(upd, g, gi)
            best = jnp.maximum(best, gs)

    def pick(a, i):
        out = row(a, i)
        for g in range(1, N_GROUPS):
            out = jnp.where(gi == g, row(a, EPG * g + i), out)
        return out

    v = [pick(sel_t, i) for i in range(EPG)]
    s = [pick(sc_t, i) for i in range(EPG)]

    def argmax4(vals):
        bv, bi = vals[0], jnp.zeros((1, tm), I32)
        for i in range(1, EPG):
            upd = vals[i] > bv
            bi = jnp.where(upd, i, bi)
            bv = jnp.maximum(bv, vals[i])
        return bi

    i1 = argmax4(v)
    i2 = argmax4([jnp.where(i1 == i, -jnp.inf, v[i]) for i in range(EPG)])
    ia = jnp.minimum(i1, i2)
    ib = jnp.maximum(i1, i2)
    pidx = jnp.where(ia == 0, ib - 1, jnp.where(ia == 1, jnp.where(ib == 3, 3, 5), 4))
    bucket = gi * N_PAIRS + pidx

    def by_index(vals, idx):
        out = vals[0]
        for i in range(1, EPG):
            out = jnp.where(idx == i, vals[i], out)
        return out

    swap = pidx == N_PAIRS - 1
    s_lo, s_hi = by_index(s, ia), by_index(s, ib)
    s_a, s_b = jnp.where(swap, s_hi, s_lo), jnp.where(swap, s_lo, s_hi)
    gate_a = s_a / (s_a + s_b)
    gate_b = s_b / (s_a + s_b)

    brow = lax.broadcasted_iota(I32, (BUCKET_ROWS, tm), 0)
    onehot = brow == bucket
    cums = _dot(jnp.where(onehot, 1.0, 0.0).astype(BF16), tri_ref[...])
    carry = carry_ref[...]
    rank = jnp.sum(jnp.where(onehot, carry[:, 0:1] + cums, 0.0), axis=0, keepdims=True) - 1.0
    new_carry = carry + cums[:, tm - 1:tm]
    carry_ref[...] = new_carry
    cnt_ref[...] = new_carry

    route_ref[...] = jnp.concatenate(
        [bucket.astype(F32), gate_a, gate_b, rank, jnp.zeros((SUBLANES - 4, tm), F32)], axis=0)

    half = d // 2
    packed = _pack_bf16_pairs(hif, rounded=True)
    for cpart in range(half // LANES):
        pay_ref[cpart] = packed[:, cpart * LANES:(cpart + 1) * LANES]
    gates_t = jnp.concatenate([gate_a, gate_b, jnp.zeros((LANES - 2, tm), F32)], axis=0)
    pay_ref[half // LANES] = lax.bitcast_convert_type(gates_t.T, U32)


def _merge(o_attn, hm, mg, x, g1, sc2, sh2, nw, wa, wm, wo, rwh, rwl, rb, tri, l, seq):
    t, d = x.shape
    tm = tri.shape[0]
    tpb = seq // tm
    row = lambda i: (i, 0)
    bsel = lambda i: (i // tpb, 0, 0)
    wsel = lambda i: (l, 0, 0)
    const = lambda i: (0, 0)
    hw = o_attn.shape[1]
    return pl.pallas_call(
        _merge_kernel,
        grid=(t // tm,),
        in_specs=[
            pl.BlockSpec((tm, hw), row),
            pl.BlockSpec((tm, hw), row),
            pl.BlockSpec((tm, 2 * d), row),
            pl.BlockSpec((tm, d), row),
            pl.BlockSpec((None, 1, d), bsel),
            pl.BlockSpec((None, 1, d), bsel),
            pl.BlockSpec((None, 1, d), bsel),
            pl.BlockSpec((None, 1, d), wsel),
            pl.BlockSpec((None, hw, d), wsel),
            pl.BlockSpec((None, hw, d), wsel),
            pl.BlockSpec((None, d, d), wsel),
            pl.BlockSpec((2 * N_EXPERTS, d), const),
            pl.BlockSpec((N_EXPERTS, d), const),
            pl.BlockSpec((N_EXPERTS, LANES), const),
            pl.BlockSpec((tm, tm), const),
        ],
        out_specs=[
            pl.BlockSpec((tm, d), row),
            pl.BlockSpec((PAY_PARTS, tm, LANES), lambda i: (0, i, 0)),
            pl.BlockSpec((SUBLANES, tm), lambda i: (0, i)),
            pl.BlockSpec((BUCKET_ROWS, LANES), const),
        ],
        out_shape=[
            jax.ShapeDtypeStruct((t, d), F32),
            jax.ShapeDtypeStruct((PAY_PARTS, t, LANES), U32),
            jax.ShapeDtypeStruct((SUBLANES, t), F32),
            jax.ShapeDtypeStruct((BUCKET_ROWS, LANES), F32),
        ],
        scratch_shapes=[pltpu.VMEM((BUCKET_ROWS, LANES), F32)],
        compiler_params=_params("arbitrary"),
        name="merge_router",
    )(o_attn, hm, mg, x, g1, sc2, sh2, nw, wa, wm, wo, rwh, rwl, rb, tri)


def _sc_mesh():
    return plsc.VectorSubcoreMesh(core_axis_name="core", subcore_axis_name="subcore")


def _sc_scatter_rows(rows, dest, n_out):
    n, w = rows.shape

    @pl.kernel(out_type=jax.ShapeDtypeStruct((n_out, w), rows.dtype), mesh=_sc_mesh(), scratch_types=[])
    def scatter(x_hbm, i_hbm, o_hbm):
        def body(x_vmem, i_vmem):
            pltpu.sync_copy(x_vmem, o_hbm.at[i_vmem.at[0]])

        pltpu.emit_pipeline(
            body,
            grid=(n // SC_WINDOW,),
            in_specs=[pl.BlockSpec((SC_WINDOW, w), lambda i: (i, 0)),
                      pl.BlockSpec((1, SC_WINDOW), lambda i: (0, i))],
            out_specs=[],
            core_axis_name=("core", "subcore"),
            dimension_semantics=(pltpu.PARALLEL,),
        )(x_hbm, i_hbm)

    return scatter(rows, dest.reshape(1, n))


def _sc_gather_rows(src, idx):
    n = idx.shape[0]
    w = src.shape[1]

    @pl.kernel(out_type=jax.ShapeDtypeStruct((n, w), src.dtype), mesh=_sc_mesh(), scratch_types=[])
    def gather(x_hbm, i_hbm, o_hbm):
        def body(i_vmem, o_vmem):
            pltpu.sync_copy(x_hbm.at[i_vmem.at[0]], o_vmem)

        pltpu.emit_pipeline(
            body,
            grid=(n // SC_WINDOW,),
            in_specs=[pl.BlockSpec((1, SC_WINDOW), lambda i: (0, i))],
            out_specs=[pl.BlockSpec((SC_WINDOW, w), lambda i: (i, 0))],
            core_axis_name=("core", "subcore"),
            dimension_semantics=(pltpu.PARALLEL,),
        )(i_hbm, o_hbm)

    return gather(src, idx.reshape(1, n))


def _row_index_kernel(ps_ref, route_ref, o_ref, *, n_rows):
    bucket = route_ref[0:1, :].astype(I32)
    start = jnp.zeros(bucket.shape, I32)
    for b in range(N_BUCKETS):
        start = jnp.where(bucket == b, ps_ref[b], start)
    dest = start + route_ref[3:4, :].astype(I32)
    part = lax.broadcasted_iota(I32, o_ref.shape, 0)
    o_ref[...] = part * n_rows + dest


def _row_index(pad_starts, route, n_rows):
    t = route.shape[1]
    tm = min(2048, t)
    return pl.pallas_call(
        functools.partial(_row_index_kernel, n_rows=n_rows),
        grid=(t // tm,),
        in_specs=[pl.BlockSpec(memory_space=pltpu.SMEM), pl.BlockSpec((SUBLANES, tm), lambda i: (0, i))],
        out_specs=pl.BlockSpec((SUBLANES, tm), lambda i: (0, i)),
        out_shape=jax.ShapeDtypeStruct((SUBLANES, t), I32),
        compiler_params=_params("arbitrary"),
        name="row_index",
    )(pad_starts, route)


def _residual_kernel(x_ref, y_ref, g2_ref, o_ref):
    y = _unpack_bf16_pairs(jnp.concatenate([y_ref[c] for c in range(OUT_PARTS)], axis=1))
    o_ref[...] = x_ref[...] + g2_ref[...] * y


def _residual(x1, ytok, g2, seq):
    t, d = x1.shape
    tm = min(512, seq)
    tpb = seq // tm
    return pl.pallas_call(
        _residual_kernel,
        grid=(t // tm,),
        in_specs=[
            pl.BlockSpec((tm, d), lambda i: (i, 0)),
            pl.BlockSpec((OUT_PARTS, tm, LANES), lambda i: (0, i, 0)),
            pl.BlockSpec((None, 1, d), lambda i: (i // tpb, 0, 0)),
        ],
        out_specs=pl.BlockSpec((tm, d), lambda i: (i, 0)),
        out_shape=jax.ShapeDtypeStruct((t, d), F32),
        compiler_params=_params("arbitrary"),
        name="residual",
    )(x1, ytok, g2)


def _expert_kernel(ea_ref, eb_ref, nr_ref, xs_ref, wga_ref, wua_ref, wda_ref, wgb_ref, wub_ref, wdb_ref, ys_ref,
                   ga_ref, ua_ref, da_ref, gb_ref, ub_ref, db_ref):
    j = pl.program_id(0)
    nr = nr_ref[0]
    prev = jnp.maximum(j - 1, 0)

    @pl.when((j == 0) | (ea_ref[j] != ea_ref[prev]))
    def _():
        ga_ref[...] = wga_ref[...].astype(BF16)
        ua_ref[...] = wua_ref[...].astype(BF16)
        da_ref[...] = wda_ref[...].astype(BF16)

    @pl.when((j == 0) | (eb_ref[j] != eb_ref[prev]))
    def _():
        gb_ref[...] = wgb_ref[...].astype(BF16)
        ub_ref[...] = wub_ref[...].astype(BF16)
        db_ref[...] = wdb_ref[...].astype(BF16)

    @pl.when(j < nr)
    def _():
        x = _unpack_bf16_pairs(jnp.concatenate([xs_ref[c] for c in range(PAY_PARTS - 1)], axis=1)).astype(BF16)
        gl = lax.bitcast_convert_type(xs_ref[PAY_PARTS - 1], F32)

        def ffn(wg_ref, wu_ref, wd_ref):
            gte = _dot(x, wg_ref[...])
            act = gte * _sigmoid(gte) * _dot(x, wu_ref[...])
            return _dot(act.astype(BF16), wd_ref[...])

        y = _pack_bf16_pairs(gl[:, 0:1] * ffn(ga_ref, ua_ref, da_ref) + gl[:, 1:2] * ffn(gb_ref, ub_ref, db_ref))
        for c in range(OUT_PARTS):
            ys_ref[c] = y[:, c * LANES:(c + 1) * LANES]

    @pl.when(j >= nr)
    def _():
        ys_ref[...] = jnp.zeros(ys_ref.shape, U32)


def _experts(blk_ea, blk_eb, n_real, xs, wg, wu, wd, d):
    n_rows = xs.shape[1]
    nblk = n_rows // EXPERT_BLOCK
    f = wg.shape[2]
    grid_spec = pltpu.PrefetchScalarGridSpec(
        num_scalar_prefetch=3,
        grid=(nblk,),
        in_specs=[
            pl.BlockSpec((PAY_PARTS, EXPERT_BLOCK, LANES), lambda j, ea, eb, nr: (0, j, 0)),
            pl.BlockSpec((None, d, f), lambda j, ea, eb, nr: (ea[j], 0, 0)),
            pl.BlockSpec((None, d, f), lambda j, ea, eb, nr: (ea[j], 0, 0)),
            pl.BlockSpec((None, f, d), lambda j, ea, eb, nr: (ea[j], 0, 0)),
            pl.BlockSpec((None, d, f), lambda j, ea, eb, nr: (eb[j], 0, 0)),
            pl.BlockSpec((None, d, f), lambda j, ea, eb, nr: (eb[j], 0, 0)),
            pl.BlockSpec((None, f, d), lambda j, ea, eb, nr: (eb[j], 0, 0)),
        ],
        out_specs=pl.BlockSpec((OUT_PARTS, EXPERT_BLOCK, LANES), lambda j, ea, eb, nr: (0, j, 0)),
        scratch_shapes=[pltpu.VMEM((d, f), BF16), pltpu.VMEM((d, f), BF16), pltpu.VMEM((f, d), BF16)] * 2,
    )
    return pl.pallas_call(
        _expert_kernel,
        grid_spec=grid_spec,
        out_shape=jax.ShapeDtypeStruct((OUT_PARTS, n_rows, LANES), U32),
        compiler_params=_params("arbitrary"),
        name="experts",
    )(blk_ea, blk_eb, n_real, xs, wg, wu, wd, wg, wu, wd)


_PAIR_A = (0, 0, 0, 1, 2, 2)
_PAIR_B = (1, 2, 3, 3, 3, 1)


def kernel(x, c, positions, ada_w, ada_b, norm_mix_w, norm_ffn_w, w_in, b_igate, b_fgate, q_norm_w, k_norm_w,
           sinks, conv_w, conv_b, mlstm_norm_w, w_attn_up, w_mlstm_up, w_out, router_w, router_bias,
           w_gate, w_up, w_down):
    batch, seq, d = x.shape
    depth = w_in.shape[0]
    t = batch * seq
    qw = N_HEADS * HEAD_DIM
    kvw = N_KV * HEAD_DIM
    mw = M_HEADS * M_DIM

    o = 0
    cols = {}
    for name, wdt in (("q", qw), ("k", kvw), ("v", kvw), ("mqk", 2 * mw), ("mv", mw), ("mi", M_HEADS),
                      ("mf", M_HEADS), ("mo", mw), ("ga", d), ("gb", d)):
        cols[name] = (o, o + wdt)
        o += wdt

    def wc(name, lo=0, hi=None):
        s, e = cols[name]
        return w_in[:, :, s + lo:(s + hi if hi is not None else e)]

    w_a = jnp.concatenate([wc("q"), wc("k"), wc("v")], axis=2).astype(BF16)
    w_m = jnp.concatenate([wc("mqk"), wc("mv"), wc("mo")], axis=2).astype(BF16)
    w_g = jnp.concatenate([wc("mi"), wc("mf"), jnp.zeros((depth, d, LANES - 2 * M_HEADS), F32)], axis=2).astype(BF16)
    w_mg = jnp.concatenate([wc("ga"), wc("gb")], axis=2).astype(BF16)
    w_au = w_attn_up.astype(BF16)
    w_mu = w_mlstm_up.astype(BF16)
    w_o = w_out.astype(BF16)
    n_e = w_gate.shape[1]
    w_g8 = w_gate.reshape(depth * n_e, d, -1)
    w_u8 = w_up.reshape(depth * n_e, d, -1)
    w_d = w_down.reshape(depth * n_e, -1, d)

    rw_t = router_w.astype(F32).T
    rw_top = rw_t.astype(BF16)
    rw_hi = jnp.concatenate([rw_top, (rw_t - rw_top.astype(F32)).astype(BF16)], axis=0)
    rw_lo = rw_top
    rb = jnp.broadcast_to(router_bias.astype(F32)[:, None], (n_e, LANES))

    qn_w = jnp.tile(q_norm_w * (HEAD_DIM ** -0.5), (1, N_HEADS)).reshape(depth, 1, qw)
    kn_w = jnp.tile(k_norm_w, (1, N_KV)).reshape(depth, 1, kvw)
    seg = jnp.arange(qw) // HEAD_DIM
    bdq = jnp.where(seg[:, None] == seg[None, :], 1.0 / HEAD_DIM, 0.0).astype(BF16)
    bdk = bdq[:kvw, :kvw]

    inv_freq = ROPE_THETA ** (-(jnp.arange(0, ROPE_DIM, 2, dtype=F32) / ROPE_DIM))
    ang = positions.astype(F32).reshape(1, t) * inv_freq[:, None]
    cos8, sin8 = jnp.cos(ang).T, jnp.sin(ang).T
    pad1 = jnp.ones((t, HEAD_DIM - ROPE_DIM), F32)
    pad0 = jnp.zeros((t, HEAD_DIM - ROPE_DIM), F32)
    cos_t = jnp.tile(jnp.concatenate([cos8, cos8, pad1], axis=1), (1, LANES // HEAD_DIM))
    sin_t = jnp.tile(jnp.concatenate([-sin8, sin8, pad0], axis=1), (1, LANES // HEAD_DIM))

    gate_bias = jnp.concatenate([b_igate, b_fgate], axis=1).astype(F32)
    bcol = jnp.broadcast_to(gate_bias[:, :, None], (depth, 2 * M_HEADS, LANES))

    tm_merge = min(1024, seq)
    ii = jnp.arange(tm_merge)
    tri = (ii[:, None] <= ii[None, :]).astype(BF16)

    n_blk = (t + N_BUCKETS * (EXPERT_BLOCK - 1)) // EXPERT_BLOCK + 1
    n_rows = n_blk * EXPERT_BLOCK
    pair_a = jnp.asarray(_PAIR_A, I32)
    pair_b = jnp.asarray(_PAIR_B, I32)

    c_pad = jnp.zeros((SUBLANES, d), F32).at[:batch].set(c)
    mod = _ada_mod(c_pad, ada_w, ada_b)[:, :batch]

    xf = x.reshape(t, d)
    moe = None
    for l in range(depth):
        sh1, sc1, g1, sh2, sc2, g2 = [m.reshape(batch, 1, d) for m in jnp.split(mod[l], 6, axis=-1)]

        outs = _inproj(xf, moe, sc1, sh1, norm_mix_w.reshape(depth, 1, d), w_a, w_m, w_g, w_mg, l, seq)
        a_in, m_in, mg, grow = outs[:4]
        if moe is not None:
            xf = outs[4]
        o_attn = _attention(a_in, cos_t, sin_t, sinks[l], qn_w[l], kn_w[l], bdq, bdk, batch, seq)
        hm = _mlstm(m_in, grow, conv_w[l], conv_b[l].reshape(1, -1), bcol[l], mlstm_norm_w[l].reshape(1, mw),
                    batch, seq)
        x1, pay, route, cnt = _merge(o_attn, hm, mg, xf, g1, sc2, sh2, norm_ffn_w.reshape(depth, 1, d),
                                     w_au, w_mu, w_o, rw_hi, rw_lo, rb, tri, l, seq)

        counts = cnt[:N_BUCKETS, 0].astype(I32)
        padded = (counts + EXPERT_BLOCK - 1) // EXPERT_BLOCK * EXPERT_BLOCK
        pad_ends = jnp.cumsum(padded)
        pad_starts = pad_ends - padded
        row_idx = _row_index(jnp.concatenate([pad_starts, jnp.zeros((BUCKET_ROWS - N_BUCKETS,), I32)]), route, n_rows)
        blk_start = jnp.arange(n_blk, dtype=I32) * EXPERT_BLOCK
        blk_bucket = jnp.minimum(jnp.sum((pad_ends[None, :] <= blk_start[:, None]).astype(I32), axis=1), N_BUCKETS - 1)
        grp = blk_bucket // N_PAIRS
        blk_ea = (l * n_e + grp * EPG + pair_a[blk_bucket % N_PAIRS]).astype(I32)
        blk_eb = (l * n_e + grp * EPG + pair_b[blk_bucket % N_PAIRS]).astype(I32)
        n_real = (pad_ends[-1:] // EXPERT_BLOCK).astype(I32)

        xs = _sc_scatter_rows(pay.reshape(PAY_PARTS * t, LANES), row_idx[:PAY_PARTS].reshape(-1),
                              PAY_PARTS * n_rows).reshape(PAY_PARTS, n_rows, LANES)
        ys = _experts(blk_ea, blk_eb, n_real, xs, w_g8, w_u8, w_d, d)
        ytok = _sc_gather_rows(ys.reshape(OUT_PARTS * n_rows, LANES),
                               row_idx[:OUT_PARTS].reshape(-1)).reshape(OUT_PARTS, t, LANES)
        xf, moe = x1, (ytok, g2)
    return _residual(xf, moe[0], moe[1], seq).reshape(batch, seq, d)
```

```python
import functools

import jax
import jax.numpy as jnp
from jax import lax
from jax.experimental import pallas as pl
from jax.experimental.pallas import tpu as pltpu
from jax.experimental.pallas import tpu_sc as plsc

F32 = jnp.float32
BF16 = jnp.bfloat16
U32 = jnp.uint32
I32 = jnp.int32
HIGHEST = lax.Precision.HIGHEST

HEAD_DIM = 64
N_HEADS = 8
N_KV = 2
ROPE_DIM = 16
ROPE_THETA = 500000.0
ATTN_BLOCK = 128
M_HEADS = 4
M_DIM = 128
CONV_K = 4
N_EXPERTS = 16
N_GROUPS = 4
EPG = 4
EPS = 1e-6

LANES = 128
SUBLANES = 8

TILE_INPROJ = 1024
TILE_ATTN = 512
TILE_MLSTM = 1024
TILE_MERGE = 1024
TILE_RESIDUAL = 1024
TILE_ROW_INDEX = 8192
ADA_COLS = 3072
CHUNK = 128
CHUNKS_PER_STEP = 4
GATE_CHUNKS_PER_STEP = 4
N_PAIRS = 6
N_BUCKETS = N_GROUPS * N_PAIRS
BUCKET_ROWS = 32
EXPERT_BLOCK = 256
PAY_PARTS = 5
OUT_PARTS = 4
SC_WINDOW = 128
VMEM_LIMIT = 56 * 1024 * 1024


def _dot(a, b, precision=None):
    return jnp.dot(a, b, preferred_element_type=F32, precision=precision)


def _dot_nt(a, b):
    return lax.dot_general(a, b, (((1,), (1,)), ((), ())), preferred_element_type=F32)


def _sigmoid(x):
    return 1.0 / (1.0 + jnp.exp(-x))


def _log_sigmoid(x):
    return jnp.minimum(x, 0.0) - jnp.log1p(jnp.exp(-jnp.abs(x)))


def _pack_bf16_pairs(v, rounded=False):
    n = v.shape[1] // 2
    bits = lax.bitcast_convert_type(v if rounded else v.astype(BF16).astype(F32), U32)
    return bits[:, :n] | (bits[:, n:] >> 16)


def _unpack_bf16_pairs(w):
    hi = lax.bitcast_convert_type(w & jnp.uint32(0xFFFF0000), F32)
    lo = lax.bitcast_convert_type(w << 16, F32)
    return jnp.concatenate([hi, lo], axis=1)


def _params(*sem):
    return pltpu.CompilerParams(dimension_semantics=sem, vmem_limit_bytes=VMEM_LIMIT)


def _ada_kernel(c_ref, w_ref, b_ref, o_ref):
    c = c_ref[...]
    ca = c * _sigmoid(c)
    o_ref[0] = _dot(ca, w_ref[0], HIGHEST) + b_ref[0]


def _ada_mod(c_pad, ada_w, ada_b):
    depth, d, n = ada_w.shape
    tn = ADA_COLS
    return pl.pallas_call(
        _ada_kernel,
        grid=(depth, n // tn),
        in_specs=[
            pl.BlockSpec((SUBLANES, d), lambda l, j: (0, 0)),
            pl.BlockSpec((1, d, tn), lambda l, j: (l, 0, j)),
            pl.BlockSpec((1, 1, tn), lambda l, j: (l, 0, j)),
        ],
        out_specs=pl.BlockSpec((1, SUBLANES, tn), lambda l, j: (l, 0, j)),
        out_shape=jax.ShapeDtypeStruct((depth, SUBLANES, n), F32),
        compiler_params=_params("arbitrary", "arbitrary"),
        name="ada_mod",
    )(c_pad, ada_w, ada_b.reshape(depth, 1, n))


def _inproj_kernel(*refs, fuse_residual):
    if fuse_residual:
        (x_ref, y_ref, g2_ref, sc_ref, sh_ref, nw_ref, wa_ref, wm_ref, wg_ref, wmg_ref,
         a_ref, m_ref, mg_ref, gr_ref, xo_ref, g_ref) = refs
        y = _unpack_bf16_pairs(jnp.concatenate([y_ref[c] for c in range(OUT_PARTS)], axis=1))
        x = x_ref[...] + g2_ref[...] * y
        xo_ref[...] = x
    else:
        (x_ref, sc_ref, sh_ref, nw_ref, wa_ref, wm_ref, wg_ref, wmg_ref,
         a_ref, m_ref, mg_ref, gr_ref, g_ref) = refs
        x = x_ref[...]
    ms = jnp.mean(x * x, axis=-1, keepdims=True)
    h = x * lax.rsqrt(ms + EPS) * nw_ref[...]
    h = h * (1.0 + sc_ref[...]) + sh_ref[...]
    hb = h.astype(BF16)
    a_ref[...] = _dot(hb, wa_ref[...]).astype(BF16)
    m_ref[...] = _dot(hb, wm_ref[...]).astype(BF16)
    mg_ref[...] = _sigmoid(_dot(hb, wmg_ref[...])).astype(BF16)
    g_ref[...] = _dot(hb, wg_ref[...])
    gr_ref[...] = g_ref[...].T[:SUBLANES, :]


def _inproj(x, moe, sc, sh, nw, wa, wm, wg, wmg, l, seq):
    t, d = x.shape
    tm = min(TILE_INPROJ, seq)
    tpb = seq // tm
    row = lambda i: (i, 0)
    bsel = lambda i: (i // tpb, 0, 0)
    wsel = lambda i: (l, 0, 0)
    once = pl.Buffered(1)
    na, nm, ng, nmg = wa.shape[2], wm.shape[2], wg.shape[2], wmg.shape[2]
    fuse = moe is not None
    moe_specs = [pl.BlockSpec((OUT_PARTS, tm, LANES), lambda i: (0, i, 0)), pl.BlockSpec((None, 1, d), bsel)]
    return pl.pallas_call(
        functools.partial(_inproj_kernel, fuse_residual=fuse),
        grid=(t // tm,),
        in_specs=[pl.BlockSpec((tm, d), row)] + (moe_specs if fuse else []) + [
            pl.BlockSpec((None, 1, d), bsel),
            pl.BlockSpec((None, 1, d), bsel),
            pl.BlockSpec((None, 1, d), wsel),
            pl.BlockSpec((None, d, na), wsel, pipeline_mode=once),
            pl.BlockSpec((None, d, nm), wsel, pipeline_mode=once),
            pl.BlockSpec((None, d, ng), wsel, pipeline_mode=once),
            pl.BlockSpec((None, d, nmg), wsel, pipeline_mode=once),
        ],
        out_specs=[
            pl.BlockSpec((tm, na), row),
            pl.BlockSpec((tm, nm), row),
            pl.BlockSpec((tm, nmg), row),
            pl.BlockSpec((SUBLANES, tm), lambda i: (0, i)),
        ] + ([pl.BlockSpec((tm, d), row)] if fuse else []),
        out_shape=[
            jax.ShapeDtypeStruct((t, na), BF16),
            jax.ShapeDtypeStruct((t, nm), BF16),
            jax.ShapeDtypeStruct((t, nmg), BF16),
            jax.ShapeDtypeStruct((SUBLANES, t), F32),
        ] + ([jax.ShapeDtypeStruct((t, d), F32)] if fuse else []),
        scratch_shapes=[pltpu.VMEM((tm, ng), F32)],
        compiler_params=_params("arbitrary"),
        name="inproj",
    )(x, *(moe if fuse else ()), sc, sh, nw, wa, wm, wg, wmg)


def _rope(t, cos, sin):
    w = t.shape[1]
    reps = w // LANES
    cosw = jnp.concatenate([cos] * reps, axis=1) if reps > 1 else cos
    sinw = jnp.concatenate([sin] * reps, axis=1) if reps > 1 else sin
    lane = lax.broadcasted_iota(I32, t.shape, 1)
    half = ROPE_DIM // 2
    up = pltpu.roll(t, w - half, axis=1)
    dn = pltpu.roll(t, half, axis=1)
    partner = jnp.where((lane % ROPE_DIM) < half, up, dn)
    return t * cosw + partner * sinw


def _head_norm(t, bd, w):
    ms = _dot((t * t).astype(BF16), bd)
    return t * lax.rsqrt(ms + EPS) * w


def _attn_kernel(sink_ref, cur_ref, prev_ref, cos_ref, sin_ref, cosp_ref, sinp_ref,
                 qw_ref, kw_ref, bdq_ref, bdk_ref, o_ref):
    tq = cur_ref.shape[0]
    nj = tq // ATTN_BLOCK
    qw = N_HEADS * HEAD_DIM
    kw = N_KV * HEAD_DIM
    blk0 = pl.program_id(1) * nj

    cur = cur_ref[...]
    q = cur[:, :qw].astype(F32)
    kc = cur[:, qw:qw + kw].astype(F32)
    vc = cur[:, qw + kw:].astype(F32)
    prev = prev_ref[...]
    kp = prev[:, :kw].astype(F32)
    vp = prev[:, kw:].astype(F32)

    cos, sin = cos_ref[...], sin_ref[...]
    q = _rope(_head_norm(q, bdq_ref[...], qw_ref[...]), cos, sin)
    kc = _rope(_head_norm(kc, bdk_ref[...], kw_ref[...]), cos, sin)
    kp = _rope(_head_norm(kp, bdk_ref[...], kw_ref[...]), cosp_ref[...], sinp_ref[...])
    qb = q.astype(BF16)

    def both_halves(x2):
        swapped = pltpu.roll(x2, HEAD_DIM, axis=1)
        first = lax.broadcasted_iota(I32, x2.shape, 1) < HEAD_DIM
        return jnp.concatenate([jnp.where(first, x2, swapped), jnp.where(first, swapped, x2)], axis=1).astype(BF16)

    k_all = both_halves(jnp.concatenate([kp, kc], axis=0))
    v_all = both_halves(jnp.concatenate([vp, vc], axis=0))

    lane = lax.broadcasted_iota(I32, (ATTN_BLOCK, LANES), 1)
    lo = lane < HEAD_DIM
    zero = jnp.zeros((ATTN_BLOCK, LANES), BF16)
    g_heads = N_HEADS // N_KV
    ri = lax.broadcasted_iota(I32, (g_heads * ATTN_BLOCK, ATTN_BLOCK), 0) % ATTN_BLOCK
    ci = lax.broadcasted_iota(I32, (g_heads * ATTN_BLOCK, ATTN_BLOCK), 1)
    from_prev = ci > ri
    head_row = lax.broadcasted_iota(I32, (g_heads * ATTN_BLOCK, 1), 0) // ATTN_BLOCK
    ones_v = jnp.ones((2 * ATTN_BLOCK, LANES), BF16)

    tiles = [(j, g) for j in range(nj) for g in range(N_KV)]
    scores = {}
    for j, g in tiles:
        rows = slice(j * ATTN_BLOCK, (j + 1) * ATTN_BLOCK)
        band = slice(j * ATTN_BLOCK, (j + 2) * ATTN_BLOCK)
        qp0 = qb[rows, (2 * g) * LANES:(2 * g + 1) * LANES]
        qp1 = qb[rows, (2 * g + 1) * LANES:(2 * g + 2) * LANES]
        q4 = jnp.concatenate([jnp.where(lo, qp0, zero), jnp.where(lo, zero, qp0),
                              jnp.where(lo, qp1, zero), jnp.where(lo, zero, qp1)], axis=0)
        scores[j, g] = _dot_nt(q4, k_all[band, g * LANES:(g + 1) * LANES])

    probs, sink_term = {}, {}
    for j, g in tiles:
        prev_ok = ci > ri + (1 - jnp.minimum(blk0 + j, 1)) * ATTN_BLOCK
        s2 = scores[j, g]
        s = jnp.where(prev_ok, s2[:, :ATTN_BLOCK], jnp.where(from_prev, -jnp.inf, s2[:, ATTN_BLOCK:]))
        sink = jnp.full((g_heads * ATTN_BLOCK, 1), sink_ref[g_heads * g], F32)
        for r in range(1, g_heads):
            sink = jnp.where(head_row == r, sink_ref[g_heads * g + r], sink)
        m = jnp.maximum(jnp.max(s, axis=-1, keepdims=True), sink)
        p = jnp.exp(s - m)
        probs[j, g] = jnp.concatenate([jnp.where(from_prev, p, 0.0), jnp.where(from_prev, 0.0, p)],
                                      axis=1).astype(BF16)
        sink_term[j, g] = jnp.exp(sink - m)

    for j, g in tiles:
        rows = slice(j * ATTN_BLOCK, (j + 1) * ATTN_BLOCK)
        band = slice(j * ATTN_BLOCK, (j + 2) * ATTN_BLOCK)
        o8 = _dot(probs[j, g], jnp.concatenate([v_all[band, g * LANES:(g + 1) * LANES], ones_v], axis=1))
        o4 = o8[:, :LANES] / (o8[:, LANES:] + sink_term[j, g])
        b = ATTN_BLOCK
        o_ref[rows, (2 * g) * LANES:(2 * g + 1) * LANES] = jnp.where(lo, o4[0:b], o4[b:2 * b]).astype(BF16)
        o_ref[rows, (2 * g + 1) * LANES:(2 * g + 2) * LANES] = jnp.where(
            lo, o4[2 * b:3 * b], o4[3 * b:4 * b]).astype(BF16)


def _attention(a_in, cos_t, sin_t, sinks_l, qw, kw, bdq, bdk, batch, seq):
    t = a_in.shape[0]
    tq = min(TILE_ATTN, seq)
    nj = tq // ATTN_BLOCK
    tpb = seq // tq
    bpb = seq // ATTN_BLOCK
    qwid = N_HEADS * HEAD_DIM
    kvw = 2 * N_KV * HEAD_DIM
    cur = lambda b, i: (b * tpb + i, 0)
    prv = lambda b, i: (b * bpb + jnp.maximum(i * nj - 1, 0), qwid // kvw)
    prv0 = lambda b, i: (b * bpb + jnp.maximum(i * nj - 1, 0), 0)
    const = lambda b, i: (0, 0)
    return pl.pallas_call(
        _attn_kernel,
        grid=(batch, tpb),
        in_specs=[
            pl.BlockSpec(memory_space=pltpu.SMEM),
            pl.BlockSpec((tq, qwid + kvw), cur),
            pl.BlockSpec((ATTN_BLOCK, kvw), prv),
            pl.BlockSpec((tq, LANES), cur),
            pl.BlockSpec((tq, LANES), cur),
            pl.BlockSpec((ATTN_BLOCK, LANES), prv0),
            pl.BlockSpec((ATTN_BLOCK, LANES), prv0),
            pl.BlockSpec((1, qwid), const),
            pl.BlockSpec((1, kvw // 2), const),
            pl.BlockSpec((qwid, qwid), const),
            pl.BlockSpec((kvw // 2, kvw // 2), const),
        ],
        out_specs=pl.BlockSpec((tq, qwid), cur),
        out_shape=jax.ShapeDtypeStruct((t, qwid), BF16),
        compiler_params=_params("arbitrary", "arbitrary"),
        name="swa_attention",
    )(sinks_l, a_in, a_in, cos_t, sin_t, cos_t, sin_t, qw, kw, bdq, bdk)


def _mlstm_kernel(min_ref, gr_ref, cw_ref, cb_ref, bcol_ref, nw_ref,
                  hm_ref, ext_ref, q_ref, kt_ref, st_ref, mx_ref, ab_ref, bc_ref):
    tt = min_ref.shape[0]
    mw = M_HEADS * M_DIM
    nchunks = tt // CHUNK

    @pl.when(pl.program_id(1) == 0)
    def _():
        ext_ref[0:SUBLANES, :] = jnp.zeros((SUBLANES, 2 * mw), F32)
        st_ref[...] = jnp.zeros(st_ref.shape, F32)
        mx_ref[...] = jnp.zeros(mx_ref.shape, F32)

    def conv_block(cols):
        u = min_ref[:, cols].astype(F32)
        ext_ref[SUBLANES:SUBLANES + tt, cols] = u
        acc = cb_ref[:, cols] + cw_ref[CONV_K - 1:CONV_K, cols] * u
        for jj in range(CONV_K - 1):
            off = SUBLANES - (CONV_K - 1) + jj
            acc = acc + cw_ref[jj:jj + 1, cols] * ext_ref[off:off + tt, cols]
        ext_ref[0:SUBLANES, cols] = u[tt - SUBLANES:tt, :]
        return acc * _sigmoid(acc)

    def q_body(h, carry):
        cols = pl.ds(pl.multiple_of(h * M_DIM, M_DIM), M_DIM)
        q_ref[:, cols] = conv_block(cols).astype(BF16)
        return carry

    def k_body(h, carry):
        off = pl.multiple_of(h * M_DIM, M_DIM)
        act = conv_block(pl.ds(mw + off, M_DIM)) * (M_DIM ** -0.5)
        for j in range(nchunks):
            kt_ref[pl.ds(off, M_DIM), j * CHUNK:(j + 1) * CHUNK] = act[j * CHUNK:(j + 1) * CHUNK, :].T
        return carry

    lax.fori_loop(0, M_HEADS, q_body, 0)
    lax.fori_loop(0, M_HEADS, k_body, 0)

    ri = lax.broadcasted_iota(I32, (CHUNK, CHUNK), 0)
    ci = lax.broadcasted_iota(I32, (CHUNK, CHUNK), 1)
    causal = ci <= ri
    triu = jnp.where(ri <= ci, 1.0, 0.0).astype(BF16)
    ones_half = jnp.ones((CHUNK, M_DIM), BF16)
    mean_mat = jnp.full((M_DIM, M_DIM), 1.0 / M_DIM, BF16)
    sub = lax.broadcasted_iota(I32, (SUBLANES, CHUNK), 0)
    heads = range(M_HEADS)

    pad_rows = jnp.zeros((CHUNK - SUBLANES, CHUNK), F32)
    zero_rows = jnp.zeros((SUBLANES, CHUNK), F32)

    def gate_body(jg, carry):
        for u_ in range(GATE_CHUNKS_PER_STEP):
            rs = pl.ds(pl.multiple_of((jg * GATE_CHUNKS_PER_STEP + u_) * CHUNK, CHUNK), CHUNK)
            gr = gr_ref[:, rs] + bcol_ref[...]
            ls = _log_sigmoid(gr)
            ls1 = ls.astype(BF16).astype(F32)
            ls2 = (ls - ls1).astype(BF16).astype(F32)
            pieces = jnp.concatenate([ls1, ls2, ls - ls1 - ls2, zero_rows], axis=0).astype(BF16)
            sums = _dot(pieces, triu)
            br = sums[0:SUBLANES] + sums[SUBLANES:2 * SUBLANES] + sums[2 * SUBLANES:3 * SUBLANES]
            ab = jnp.where(sub < M_HEADS, gr - pltpu.roll(br, M_HEADS, axis=0), br)
            ab_ref[:, rs] = ab
            bc_ref[rs, :] = jnp.concatenate([ab, pad_rows], axis=0).T
        return carry

    lax.fori_loop(0, nchunks // GATE_CHUNKS_PER_STEP, gate_body, 0)

    def group_body(cg, carry):
        rows, ab = [], []
        for u_ in range(CHUNKS_PER_STEP):
            r0 = pl.multiple_of((cg * CHUNKS_PER_STEP + u_) * CHUNK, CHUNK)
            rows.append(pl.ds(r0, CHUNK))
            ab.append(ab_ref[:, rows[u_]])
        lanes = [(u_, h) for u_ in range(CHUNKS_PER_STEP) for h in heads]
        a_r = {(u_, h): ab[u_][h:h + 1, :] for u_, h in lanes}
        b_last = {(u_, h): ab[u_][M_HEADS + h:M_HEADS + h + 1, CHUNK - 1:CHUNK] for u_, h in lanes}

        m_prev, a_max, a_dec, s_in = {}, {}, {}, {}
        m_run = [mx_ref[h][0:1, 0:1] for h in heads]
        for k in lanes:
            u_, h = k
            m_prev[k] = m_run[h]
            a_max[k] = jnp.max(a_r[k], axis=-1, keepdims=True)
            m_loc = b_last[k] + a_max[k]
            m_new = jnp.maximum(b_last[k] + m_prev[k], m_loc)
            a_dec[k] = jnp.exp(b_last[k] + m_prev[k] - m_new)
            s_in[k] = jnp.exp(m_loc - m_new)
            m_run[h] = m_new
        for h in heads:
            mx_ref[h] = jnp.broadcast_to(m_run[h], (SUBLANES, LANES))

        q, v_ext, s_qk, kv = {}, {}, {}, {}
        for k in lanes:
            u_, h = k
            rs = rows[u_]
            q[k] = q_ref[rs, h * M_DIM:(h + 1) * M_DIM]
            kt = kt_ref[h * M_DIM:(h + 1) * M_DIM, rs]
            v = min_ref[rs, 2 * mw + h * M_DIM:2 * mw + (h + 1) * M_DIM]
            v_ext[k] = jnp.concatenate([v, ones_half], axis=1)
            s_qk[k] = _dot(q[k], kt.astype(BF16))
            e_r = jnp.exp(a_r[k] - a_max[k])
            kv[k] = _dot((kt * e_r).astype(BF16), v_ext[k])

        thr, qk, inter = {}, {}, {}
        for k in lanes:
            u_, h = k
            a_mat = jnp.where(causal, a_r[k], -jnp.inf)
            mu = jnp.maximum(jnp.max(a_mat, axis=-1, keepdims=True), m_prev[k])
            b_c = bc_ref[rows[u_], M_HEADS + h:M_HEADS + h + 1]
            thr[k] = jnp.broadcast_to(jnp.exp(-(b_c + mu)), (CHUNK, M_DIM))
            mu_b = jnp.broadcast_to(mu, (CHUNK, CHUNK))
            inter[k] = jnp.exp(m_prev[k] - mu_b)
            qk[k] = (s_qk[k] * jnp.exp(a_mat - mu_b)).astype(BF16)

        q_state = {}
        state = [st_ref[h] for h in heads]
        for k in lanes:
            u_, h = k
            q_state[k] = _dot(q[k], state[h].astype(BF16))
            state[h] = a_dec[k] * state[h] + s_in[k] * kv[k]
        for h in heads:
            st_ref[h] = state[h]

        for k in lanes:
            u_, h = k
            hs = slice(h * M_DIM, (h + 1) * M_DIM)
            num = _dot(qk[k], v_ext[k])
            den = jnp.maximum(jnp.abs(num[:, M_DIM:] + inter[k] * q_state[k][:, M_DIM:]), thr[k])
            hh = (num[:, :M_DIM] + inter[k] * q_state[k][:, :M_DIM]) / den
            msq = _dot((hh * hh).astype(BF16), mean_mat)
            hn = hh * lax.rsqrt(msq + EPS) * nw_ref[:, hs]
            og = min_ref[rows[u_], 3 * mw + h * M_DIM:3 * mw + (h + 1) * M_DIM].astype(F32)
            hm_ref[rows[u_], hs] = (_sigmoid(og) * hn).astype(BF16)
        return carry

    lax.fori_loop(0, nchunks // CHUNKS_PER_STEP, group_body, 0)


def _mlstm(m_in, grow, conv_w, conv_b, bcol, nw, batch, seq):
    t = m_in.shape[0]
    tt = min(TILE_MLSTM, seq)
    tpb = seq // tt
    mw = M_HEADS * M_DIM
    cur = lambda b, i: (b * tpb + i, 0)
    const = lambda b, i: (0, 0)
    return pl.pallas_call(
        _mlstm_kernel,
        grid=(batch, tpb),
        in_specs=[
            pl.BlockSpec((tt, 4 * mw), cur),
            pl.BlockSpec((SUBLANES, tt), lambda b, i: (0, b * tpb + i)),
            pl.BlockSpec((CONV_K, 2 * mw), const),
            pl.BlockSpec((1, 2 * mw), const),
            pl.BlockSpec((SUBLANES, LANES), const),
            pl.BlockSpec((1, mw), const),
        ],
        out_specs=pl.BlockSpec((tt, mw), cur),
        out_shape=jax.ShapeDtypeStruct((t, mw), BF16),
        scratch_shapes=[
            pltpu.VMEM((tt + SUBLANES, 2 * mw), F32),
            pltpu.VMEM((tt, mw), BF16),
            pltpu.VMEM((mw, tt), F32),
            pltpu.VMEM((M_HEADS, M_DIM, 2 * M_DIM), F32),
            pltpu.VMEM((M_HEADS, SUBLANES, LANES), F32),
            pltpu.VMEM((SUBLANES, tt), F32),
            pltpu.VMEM((tt, LANES), F32),
        ],
        compiler_params=_params("arbitrary", "arbitrary"),
        name="mlstm",
    )(m_in, grow, conv_w, conv_b, bcol, nw)


def _merge_kernel(o_ref, hm_ref, mg_ref, x_ref, g1_ref, sc_ref, sh_ref, nw_ref,
                  wa_ref, wm_ref, wo_ref, rwh_ref, rwl_ref, rb_ref, tri_ref,
                  x1_ref, pay_ref, route_ref, cnt_ref, carry_ref):
    tm, d = x_ref.shape

    @pl.when(pl.program_id(0) == 0)
    def _():
        carry_ref[...] = jnp.zeros(carry_ref.shape, F32)

    ya = _dot(o_ref[...], wa_ref[...])
    yb = _dot(hm_ref[...], wm_ref[...])
    mg = mg_ref[...]
    merged = mg[:, :d].astype(F32) * ya + mg[:, d:].astype(F32) * yb
    x1 = x_ref[...] + g1_ref[...] * _dot(merged.astype(BF16), wo_ref[...])
    x1_ref[...] = x1

    ms = jnp.mean(x1 * x1, axis=-1, keepdims=True)
    h2 = x1 * lax.rsqrt(ms + EPS) * nw_ref[...]
    h2 = h2 * (1.0 + sc_ref[...]) + sh_ref[...]
    hi = h2.astype(BF16)
    hif = hi.astype(F32)
    lo = (h2 - hif).astype(BF16)
    r_hi = _dot_nt(rwh_ref[...], hi)
    r_lo = _dot_nt(rwl_ref[...], lo)
    sc_t = _sigmoid(r_hi[:N_EXPERTS] + r_hi[N_EXPERTS:] + r_lo)
    sel_t = sc_t + rb_ref[:, 0:1]

    def row(a, e):
        return a[e:e + 1, :]

    best = None
    gi = jnp.zeros((1, tm), I32)
    for g in range(N_GROUPS):
        r = [row(sel_t, EPG * g + i) for i in range(EPG)]
        gs = None
        for i in range(EPG):
            for j in range(i + 1, EPG):
                pr = r[i] + r[j]
                gs = pr if gs is None else jnp.maximum(gs, pr)
        if best is None:
            best = gs
        else:
            upd = gs > best
            gi = jnp.where(upd, g, gi)
            best = jnp.maximum(best, gs)

    def pick(a, i):
        out = row(a, i)
        for g in range(1, N_GROUPS):
            out = jnp.where(gi == g, row(a, EPG * g + i), out)
        return out

    v = [pick(sel_t, i) for i in range(EPG)]
    s = [pick(sc_t, i) for i in range(EPG)]

    def argmax4(vals):
        bv, bi = vals[0], jnp.zeros((1, tm), I32)
        for i in range(1, EPG):
            upd = vals[i] > bv
            bi = jnp.where(upd, i, bi)
            bv = jnp.maximum(bv, vals[i])
        return bi

    i1 = argmax4(v)
    i2 = argmax4([jnp.where(i1 == i, -jnp.inf, v[i]) for i in range(EPG)])
    ia = jnp.minimum(i1, i2)
    ib = jnp.maximum(i1, i2)
    pidx = jnp.where(ia == 0, ib - 1, jnp.where(ia == 1, jnp.where(ib == 3, 3, 5), 4))
    bucket = gi * N_PAIRS + pidx

    def by_index(vals, idx):
        out = vals[0]
        for i in range(1, EPG):
            out = jnp.where(idx == i, vals[i], out)
        return out

    swap = pidx == N_PAIRS - 1
    s_lo, s_hi = by_index(s, ia), by_index(s, ib)
    s_a, s_b = jnp.where(swap, s_hi, s_lo), jnp.where(swap, s_lo, s_hi)
    gate_a = s_a / (s_a + s_b)
    gate_b = s_b / (s_a + s_b)

    brow = lax.broadcasted_iota(I32, (BUCKET_ROWS, tm), 0)
    onehot = brow == bucket
    cums = _dot(jnp.where(onehot, 1.0, 0.0).astype(BF16), tri_ref[...])
    carry = carry_ref[...]
    rank = jnp.sum(jnp.where(onehot, carry[:, 0:1] + cums, 0.0), axis=0, keepdims=True) - 1.0
    new_carry = carry + cums[:, tm - 1:tm]
    carry_ref[...] = new_carry
    cnt_ref[...] = new_carry

    route_ref[...] = jnp.concatenate(
        [bucket.astype(F32), gate_a, gate_b, rank, jnp.zeros((SUBLANES - 4, tm), F32)], axis=0)

    half = d // 2
    packed = _pack_bf16_pairs(hif, rounded=True)
    for cpart in range(half // LANES):
        pay_ref[cpart] = packed[:, cpart * LANES:(cpart + 1) * LANES]
    gates_t = jnp.concatenate([gate_a, gate_b, jnp.zeros((LANES - 2, tm), F32)], axis=0)
    pay_ref[half // LANES] = lax.bitcast_convert_type(gates_t.T, U32)


def _merge(o_attn, hm, mg, x, g1, sc2, sh2, nw, wa, wm, wo, rwh, rwl, rb, tri, l, seq):
    t, d = x.shape
    tm = tri.shape[0]
    tpb = seq // tm
    row = lambda i: (i, 0)
    bsel = lambda i: (i // tpb, 0, 0)
    wsel = lambda i: (l, 0, 0)
    const = lambda i: (0, 0)
    hw = o_attn.shape[1]
    return pl.pallas_call(
        _merge_kernel,
        grid=(t // tm,),
        in_specs=[
            pl.BlockSpec((tm, hw), row),
            pl.BlockSpec((tm, hw), row),
            pl.BlockSpec((tm, 2 * d), row),
            pl.BlockSpec((tm, d), row),
            pl.BlockSpec((None, 1, d), bsel),
            pl.BlockSpec((None, 1, d), bsel),
            pl.BlockSpec((None, 1, d), bsel),
            pl.BlockSpec((None, 1, d), wsel),
            pl.BlockSpec((None, hw, d), wsel),
            pl.BlockSpec((None, hw, d), wsel),
            pl.BlockSpec((None, d, d), wsel),
            pl.BlockSpec((2 * N_EXPERTS, d), const),
            pl.BlockSpec((N_EXPERTS, d), const),
            pl.BlockSpec((N_EXPERTS, LANES), const),
            pl.BlockSpec((tm, tm), const),
        ],
        out_specs=[
            pl.BlockSpec((tm, d), row),
            pl.BlockSpec((PAY_PARTS, tm, LANES), lambda i: (0, i, 0)),
            pl.BlockSpec((SUBLANES, tm), lambda i: (0, i)),
            pl.BlockSpec((BUCKET_ROWS, LANES), const),
        ],
        out_shape=[
            jax.ShapeDtypeStruct((t, d), F32),
            jax.ShapeDtypeStruct((PAY_PARTS, t, LANES), U32),
            jax.ShapeDtypeStruct((SUBLANES, t), F32),
            jax.ShapeDtypeStruct((BUCKET_ROWS, LANES), F32),
        ],
        scratch_shapes=[pltpu.VMEM((BUCKET_ROWS, LANES), F32)],
        compiler_params=_params("arbitrary"),
        name="merge_router",
    )(o_attn, hm, mg, x, g1, sc2, sh2, nw, wa, wm, wo, rwh, rwl, rb, tri)


def _sc_mesh():
    return plsc.VectorSubcoreMesh(core_axis_name="core", subcore_axis_name="subcore")


def _sc_scatter_rows(rows, dest, n_out):
    n, w = rows.shape

    @pl.kernel(out_type=jax.ShapeDtypeStruct((n_out, w), rows.dtype), mesh=_sc_mesh(), scratch_types=[])
    def scatter(x_hbm, i_hbm, o_hbm):
        def body(x_vmem, i_vmem):
            pltpu.sync_copy(x_vmem, o_hbm.at[i_vmem.at[0]])

        pltpu.emit_pipeline(
            body,
            grid=(n // SC_WINDOW,),
            in_specs=[pl.BlockSpec((SC_WINDOW, w), lambda i: (i, 0)),
                      pl.BlockSpec((1, SC_WINDOW), lambda i: (0, i))],
            out_specs=[],
            core_axis_name=("core", "subcore"),
            dimension_semantics=(pltpu.PARALLEL,),
        )(x_hbm, i_hbm)

    return scatter(rows, dest.reshape(1, n))


def _sc_gather_rows(src, idx):
    n = idx.shape[0]
    w = src.shape[1]

    @pl.kernel(out_type=jax.ShapeDtypeStruct((n, w), src.dtype), mesh=_sc_mesh(), scratch_types=[])
    def gather(x_hbm, i_hbm, o_hbm):
        def body(i_vmem, o_vmem):
            pltpu.sync_copy(x_hbm.at[i_vmem.at[0]], o_vmem)

        pltpu.emit_pipeline(
            body,
            grid=(n // SC_WINDOW,),
            in_specs=[pl.BlockSpec((1, SC_WINDOW), lambda i: (0, i))],
            out_specs=[pl.BlockSpec((SC_WINDOW, w), lambda i: (i, 0))],
            core_axis_name=("core", "subcore"),
            dimension_semantics=(pltpu.PARALLEL,),
        )(i_hbm, o_hbm)

    return gather(src, idx.reshape(1, n))


def _row_index_kernel(ps_ref, route_ref, o_ref, *, n_rows):
    bucket = route_ref[0:1, :].astype(I32)
    start = jnp.zeros(bucket.shape, I32)
    for b in range(N_BUCKETS):
        start = jnp.where(bucket == b, ps_ref[b], start)
    dest = start + route_ref[3:4, :].astype(I32)
    part = lax.broadcasted_iota(I32, o_ref.shape, 0)
    o_ref[...] = part * n_rows + dest


def _row_index(pad_starts, route, n_rows):
    t = route.shape[1]
    tm = min(TILE_ROW_INDEX, t)
    return pl.pallas_call(
        functools.partial(_row_index_kernel, n_rows=n_rows),
        grid=(t // tm,),
        in_specs=[pl.BlockSpec(memory_space=pltpu.SMEM), pl.BlockSpec((SUBLANES, tm), lambda i: (0, i))],
        out_specs=pl.BlockSpec((SUBLANES, tm), lambda i: (0, i)),
        out_shape=jax.ShapeDtypeStruct((SUBLANES, t), I32),
        compiler_params=_params("arbitrary"),
        name="row_index",
    )(pad_starts, route)


def _residual_kernel(x_ref, y_ref, g2_ref, o_ref):
    y = _unpack_bf16_pairs(jnp.concatenate([y_ref[c] for c in range(OUT_PARTS)], axis=1))
    o_ref[...] = x_ref[...] + g2_ref[...] * y


def _residual(x1, ytok, g2, seq):
    t, d = x1.shape
    tm = min(TILE_RESIDUAL, seq)
    tpb = seq // tm
    return pl.pallas_call(
        _residual_kernel,
        grid=(t // tm,),
        in_specs=[
            pl.BlockSpec((tm, d), lambda i: (i, 0)),
            pl.BlockSpec((OUT_PARTS, tm, LANES), lambda i: (0, i, 0)),
            pl.BlockSpec((None, 1, d), lambda i: (i // tpb, 0, 0)),
        ],
        out_specs=pl.BlockSpec((tm, d), lambda i: (i, 0)),
        out_shape=jax.ShapeDtypeStruct((t, d), F32),
        compiler_params=_params("arbitrary"),
        name="residual",
    )(x1, ytok, g2)


def _expert_kernel(ea_ref, eb_ref, nr_ref, xs_ref, wga_ref, wua_ref, wda_ref, wgb_ref, wub_ref, wdb_ref, ys_ref,
                   ga_ref, ua_ref, da_ref, gb_ref, ub_ref, db_ref):
    j = pl.program_id(0)
    nr = nr_ref[0]
    prev = jnp.maximum(j - 1, 0)

    @pl.when((j == 0) | (ea_ref[j] != ea_ref[prev]))
    def _():
        ga_ref[...] = wga_ref[...].astype(BF16)
        ua_ref[...] = wua_ref[...].astype(BF16)
        da_ref[...] = wda_ref[...].astype(BF16)

    @pl.when((j == 0) | (eb_ref[j] != eb_ref[prev]))
    def _():
        gb_ref[...] = wgb_ref[...].astype(BF16)
        ub_ref[...] = wub_ref[...].astype(BF16)
        db_ref[...] = wdb_ref[...].astype(BF16)

    @pl.when(j < nr)
    def _():
        x = _unpack_bf16_pairs(jnp.concatenate([xs_ref[c] for c in range(PAY_PARTS - 1)], axis=1)).astype(BF16)
        gl = lax.bitcast_convert_type(xs_ref[PAY_PARTS - 1], F32)

        def ffn(wg_ref, wu_ref, wd_ref):
            gte = _dot(x, wg_ref[...])
            act = gte * _sigmoid(gte) * _dot(x, wu_ref[...])
            return _dot(act.astype(BF16), wd_ref[...])

        y = _pack_bf16_pairs(gl[:, 0:1] * ffn(ga_ref, ua_ref, da_ref) + gl[:, 1:2] * ffn(gb_ref, ub_ref, db_ref))
        for c in range(OUT_PARTS):
            ys_ref[c] = y[:, c * LANES:(c + 1) * LANES]

    @pl.when(j >= nr)
    def _():
        ys_ref[...] = jnp.zeros(ys_ref.shape, U32)


def _experts(blk_ea, blk_eb, n_real, xs, wg, wu, wd, d):
    n_rows = xs.shape[1]
    nblk = n_rows // EXPERT_BLOCK
    f = wg.shape[2]
    grid_spec = pltpu.PrefetchScalarGridSpec(
        num_scalar_prefetch=3,
        grid=(nblk,),
        in_specs=[
            pl.BlockSpec((PAY_PARTS, EXPERT_BLOCK, LANES), lambda j, ea, eb, nr: (0, j, 0)),
            pl.BlockSpec((None, d, f), lambda j, ea, eb, nr: (ea[j], 0, 0)),
            pl.BlockSpec((None, d, f), lambda j, ea, eb, nr: (ea[j], 0, 0)),
            pl.BlockSpec((None, f, d), lambda j, ea, eb, nr: (ea[j], 0, 0)),
            pl.BlockSpec((None, d, f), lambda j, ea, eb, nr: (eb[j], 0, 0)),
            pl.BlockSpec((None, d, f), lambda j, ea, eb, nr: (eb[j], 0, 0)),
            pl.BlockSpec((None, f, d), lambda j, ea, eb, nr: (eb[j], 0, 0)),
        ],
        out_specs=pl.BlockSpec((OUT_PARTS, EXPERT_BLOCK, LANES), lambda j, ea, eb, nr: (0, j, 0)),
        scratch_shapes=[pltpu.VMEM((d, f), BF16), pltpu.VMEM((d, f), BF16), pltpu.VMEM((f, d), BF16)] * 2,
    )
    return pl.pallas_call(
        _expert_kernel,
        grid_spec=grid_spec,
        out_shape=jax.ShapeDtypeStruct((OUT_PARTS, n_rows, LANES), U32),
        compiler_params=_params("arbitrary"),
        name="experts",
    )(blk_ea, blk_eb, n_real, xs, wg, wu, wd, wg, wu, wd)


_PAIR_A = (0, 0, 0, 1, 2, 2)
_PAIR_B = (1, 2, 3, 3, 3, 1)


def kernel(x, c, positions, ada_w, ada_b, norm_mix_w, norm_ffn_w, w_in, b_igate, b_fgate, q_norm_w, k_norm_w,
           sinks, conv_w, conv_b, mlstm_norm_w, w_attn_up, w_mlstm_up, w_out, router_w, router_bias,
           w_gate, w_up, w_down):
    batch, seq, d = x.shape
    depth = w_in.shape[0]
    t = batch * seq
    qw = N_HEADS * HEAD_DIM
    kvw = N_KV * HEAD_DIM
    mw = M_HEADS * M_DIM

    o = 0
    cols = {}
    for name, wdt in (("q", qw), ("k", kvw), ("v", kvw), ("mqk", 2 * mw), ("mv", mw), ("mi", M_HEADS),
                      ("mf", M_HEADS), ("mo", mw), ("ga", d), ("gb", d)):
        cols[name] = (o, o + wdt)
        o += wdt

    def wc(name, lo=0, hi=None):
        s, e = cols[name]
        return w_in[:, :, s + lo:(s + hi if hi is not None else e)]

    w_a = jnp.concatenate([wc("q"), wc("k"), wc("v")], axis=2).astype(BF16)
    w_m = jnp.concatenate([wc("mqk"), wc("mv"), wc("mo")], axis=2).astype(BF16)
    w_g = jnp.concatenate([wc("mi"), wc("mf"), jnp.zeros((depth, d, LANES - 2 * M_HEADS), F32)], axis=2).astype(BF16)
    w_mg = jnp.concatenate([wc("ga"), wc("gb")], axis=2).astype(BF16)
    w_au = w_attn_up.astype(BF16)
    w_mu = w_mlstm_up.astype(BF16)
    w_o = w_out.astype(BF16)
    n_e = w_gate.shape[1]
    w_g8 = w_gate.reshape(depth * n_e, d, -1)
    w_u8 = w_up.reshape(depth * n_e, d, -1)
    w_d = w_down.reshape(depth * n_e, -1, d)

    rw_t = router_w.astype(F32).T
    rw_top = rw_t.astype(BF16)
    rw_hi = jnp.concatenate([rw_top, (rw_t - rw_top.astype(F32)).astype(BF16)], axis=0)
    rw_lo = rw_top
    rb = jnp.broadcast_to(router_bias.astype(F32)[:, None], (n_e, LANES))

    qn_w = jnp.tile(q_norm_w * (HEAD_DIM ** -0.5), (1, N_HEADS)).reshape(depth, 1, qw)
    kn_w = jnp.tile(k_norm_w, (1, N_KV)).reshape(depth, 1, kvw)
    seg = jnp.arange(qw) // HEAD_DIM
    bdq = jnp.where(seg[:, None] == seg[None, :], 1.0 / HEAD_DIM, 0.0).astype(BF16)
    bdk = bdq[:kvw, :kvw]

    inv_freq = ROPE_THETA ** (-(jnp.arange(0, ROPE_DIM, 2, dtype=F32) / ROPE_DIM))
    ang = positions.astype(F32).reshape(1, t) * inv_freq[:, None]
    cos8, sin8 = jnp.cos(ang).T, jnp.sin(ang).T
    pad1 = jnp.ones((t, HEAD_DIM - ROPE_DIM), F32)
    pad0 = jnp.zeros((t, HEAD_DIM - ROPE_DIM), F32)
    cos_t = jnp.tile(jnp.concatenate([cos8, cos8, pad1], axis=1), (1, LANES // HEAD_DIM))
    sin_t = jnp.tile(jnp.concatenate([-sin8, sin8, pad0], axis=1), (1, LANES // HEAD_DIM))

    gate_bias = jnp.concatenate([b_igate, b_fgate], axis=1).astype(F32)
    bcol = jnp.broadcast_to(gate_bias[:, :, None], (depth, 2 * M_HEADS, LANES))

    tm_merge = min(TILE_MERGE, seq)
    ii = jnp.arange(tm_merge)
    tri = (ii[:, None] <= ii[None, :]).astype(BF16)

    n_blk = (t + N_BUCKETS * (EXPERT_BLOCK - 1)) // EXPERT_BLOCK + 1
    n_rows = n_blk * EXPERT_BLOCK
    pair_a = jnp.asarray(_PAIR_A, I32)
    pair_b = jnp.asarray(_PAIR_B, I32)

    c_pad = jnp.zeros((SUBLANES, d), F32).at[:batch].set(c)
    mod = _ada_mod(c_pad, ada_w, ada_b)[:, :batch]

    xf = x.reshape(t, d)
    moe = None
    for l in range(depth):
        sh1, sc1, g1, sh2, sc2, g2 = [m.reshape(batch, 1, d) for m in jnp.split(mod[l], 6, axis=-1)]

        outs = _inproj(xf, moe, sc1, sh1, norm_mix_w.reshape(depth, 1, d), w_a, w_m, w_g, w_mg, l, seq)
        a_in, m_in, mg, grow = outs[:4]
        if moe is not None:
            xf = outs[4]
        o_attn = _attention(a_in, cos_t, sin_t, sinks[l], qn_w[l], kn_w[l], bdq, bdk, batch, seq)
        hm = _mlstm(m_in, grow, conv_w[l], conv_b[l].reshape(1, -1), bcol[l], mlstm_norm_w[l].reshape(1, mw),
                    batch, seq)
        x1, pay, route, cnt = _merge(o_attn, hm, mg, xf, g1, sc2, sh2, norm_ffn_w.reshape(depth, 1, d),
                                     w_au, w_mu, w_o, rw_hi, rw_lo, rb, tri, l, seq)

        counts = cnt[:N_BUCKETS, 0].astype(I32)
        padded = (counts + EXPERT_BLOCK - 1) // EXPERT_BLOCK * EXPERT_BLOCK
        pad_ends = jnp.cumsum(padded)
        pad_starts = pad_ends - padded
        row_idx = _row_index(jnp.concatenate([pad_starts, jnp.zeros((BUCKET_ROWS - N_BUCKETS,), I32)]), route, n_rows)
        blk_start = jnp.arange(n_blk, dtype=I32) * EXPERT_BLOCK
        blk_bucket = jnp.minimum(jnp.sum((pad_ends[None, :] <= blk_start[:, None]).astype(I32), axis=1), N_BUCKETS - 1)
        grp = blk_bucket // N_PAIRS
        blk_ea = (l * n_e + grp * EPG + pair_a[blk_bucket % N_PAIRS]).astype(I32)
        blk_eb = (l * n_e + grp * EPG + pair_b[blk_bucket % N_PAIRS]).astype(I32)
        n_real = (pad_ends[-1:] // EXPERT_BLOCK).astype(I32)

        xs = _sc_scatter_rows(pay.reshape(PAY_PARTS * t, LANES), row_idx[:PAY_PARTS].reshape(-1),
                              PAY_PARTS * n_rows).reshape(PAY_PARTS, n_rows, LANES)
        ys = _experts(blk_ea, blk_eb, n_real, xs, w_g8, w_u8, w_d, d)
        ytok = _sc_gather_rows(ys.reshape(OUT_PARTS * n_rows, LANES),
                               row_idx[:OUT_PARTS].reshape(-1)).reshape(OUT_PARTS, t, LANES)
        xf, moe = x1, (ytok, g2)
    return _residual(xf, moe[0], moe[1], seq).reshape(batch, seq, d)
```

```python
import functools

import jax
import jax.numpy as jnp
from jax import lax
from jax.experimental import pallas as pl
from jax.experimental.pallas import tpu as pltpu
from jax.experimental.pallas import tpu_sc as plsc

F32 = jnp.float32
BF16 = jnp.bfloat16
U32 = jnp.uint32
I32 = jnp.int32
HIGHEST = lax.Precision.HIGHEST

HEAD_DIM = 64
N_HEADS = 8
N_KV = 2
ROPE_DIM = 16
ROPE_THETA = 500000.0
ATTN_BLOCK = 128
M_HEADS = 4
M_DIM = 128
CONV_K = 4
N_EXPERTS = 16
N_GROUPS = 4
EPG = 4
EPS = 1e-6

LANES = 128
SUBLANES = 8

TILE_INPROJ = 1024
TILE_ATTN = 1024
TILE_MLSTM = 2048
TILE_MERGE = 1024
TILE_RESIDUAL = 1024
TILE_ROW_INDEX = 8192
ADA_COLS = 3072
CHUNK = 128
CHUNKS_PER_STEP = 4
GATE_CHUNKS_PER_STEP = 4
N_PAIRS = 6
N_BUCKETS = N_GROUPS * N_PAIRS
BUCKET_ROWS = 32
EXPERT_BLOCK = 256
PAY_PARTS = 5
OUT_PARTS = 4
SC_WINDOW = 128
VMEM_LIMIT = 56 * 1024 * 1024


def _dot(a, b, precision=None):
    return jnp.dot(a, b, preferred_element_type=F32, precision=precision)


def _dot_nt(a, b):
    return lax.dot_general(a, b, (((1,), (1,)), ((), ())), preferred_element_type=F32)


def _sigmoid(x):
    return 1.0 / (1.0 + jnp.exp(-x))


def _log_sigmoid(x):
    return jnp.minimum(x, 0.0) - jnp.log1p(jnp.exp(-jnp.abs(x)))


def _pack_bf16_pairs(v, rounded=False):
    n = v.shape[1] // 2
    bits = lax.bitcast_convert_type(v if rounded else v.astype(BF16).astype(F32), U32)
    return bits[:, :n] | (bits[:, n:] >> 16)


def _unpack_bf16_pairs(w):
    hi = lax.bitcast_convert_type(w & jnp.uint32(0xFFFF0000), F32)
    lo = lax.bitcast_convert_type(w << 16, F32)
    return jnp.concatenate([hi, lo], axis=1)


def _params(*sem):
    return pltpu.CompilerParams(dimension_semantics=sem, vmem_limit_bytes=VMEM_LIMIT)


def _ada_kernel(c_ref, w_ref, b_ref, o_ref):
    c = c_ref[...]
    ca = c * _sigmoid(c)
    o_ref[0] = _dot(ca, w_ref[0], HIGHEST) + b_ref[0]


def _ada_mod(c_pad, ada_w, ada_b):
    depth, d, n = ada_w.shape
    tn = ADA_COLS
    return pl.pallas_call(
        _ada_kernel,
        grid=(depth, n // tn),
        in_specs=[
            pl.BlockSpec((SUBLANES, d), lambda l, j: (0, 0)),
            pl.BlockSpec((1, d, tn), lambda l, j: (l, 0, j)),
            pl.BlockSpec((1, 1, tn), lambda l, j: (l, 0, j)),
        ],
        out_specs=pl.BlockSpec((1, SUBLANES, tn), lambda l, j: (l, 0, j)),
        out_shape=jax.ShapeDtypeStruct((depth, SUBLANES, n), F32),
        compiler_params=_params("arbitrary", "arbitrary"),
        name="ada_mod",
    )(c_pad, ada_w, ada_b.reshape(depth, 1, n))


def _inproj_kernel(*refs, fuse_residual):
    if fuse_residual:
        (x_ref, y_ref, g2_ref, sc_ref, sh_ref, nw_ref, wa_ref, wm_ref, wg_ref, wmg_ref,
         a_ref, m_ref, mg_ref, gr_ref, xo_ref, g_ref) = refs
        y = _unpack_bf16_pairs(jnp.concatenate([y_ref[c] for c in range(OUT_PARTS)], axis=1))
        x = x_ref[...] + g2_ref[...] * y
        xo_ref[...] = x
    else:
        (x_ref, sc_ref, sh_ref, nw_ref, wa_ref, wm_ref, wg_ref, wmg_ref,
         a_ref, m_ref, mg_ref, gr_ref, g_ref) = refs
        x = x_ref[...]
    ms = jnp.mean(x * x, axis=-1, keepdims=True)
    h = x * lax.rsqrt(ms + EPS) * nw_ref[...]
    h = h * (1.0 + sc_ref[...]) + sh_ref[...]
    hb = h.astype(BF16)
    a_ref[...] = _dot(hb, wa_ref[...]).astype(BF16)
    m_ref[...] = _dot(hb, wm_ref[...]).astype(BF16)
    mg_ref[...] = _sigmoid(_dot(hb, wmg_ref[...])).astype(BF16)
    g_ref[...] = _dot(hb, wg_ref[...])
    gr_ref[...] = g_ref[...].T[:SUBLANES, :]


def _inproj(x, moe, sc, sh, nw, wa, wm, wg, wmg, l, seq):
    t, d = x.shape
    tm = min(TILE_INPROJ, seq)
    tpb = seq // tm
    row = lambda i: (i, 0)
    bsel = lambda i: (i // tpb, 0, 0)
    wsel = lambda i: (l, 0, 0)
    once = pl.Buffered(1)
    na, nm, ng, nmg = wa.shape[2], wm.shape[2], wg.shape[2], wmg.shape[2]
    fuse = moe is not None
    moe_specs = [pl.BlockSpec((OUT_PARTS, tm, LANES), lambda i: (0, i, 0)), pl.BlockSpec((None, 1, d), bsel)]
    return pl.pallas_call(
        functools.partial(_inproj_kernel, fuse_residual=fuse),
        grid=(t // tm,),
        in_specs=[pl.BlockSpec((tm, d), row)] + (moe_specs if fuse else []) + [
            pl.BlockSpec((None, 1, d), bsel),
            pl.BlockSpec((None, 1, d), bsel),
            pl.BlockSpec((None, 1, d), wsel),
            pl.BlockSpec((None, d, na), wsel, pipeline_mode=once),
            pl.BlockSpec((None, d, nm), wsel, pipeline_mode=once),
            pl.BlockSpec((None, d, ng), wsel, pipeline_mode=once),
            pl.BlockSpec((None, d, nmg), wsel, pipeline_mode=once),
        ],
        out_specs=[
            pl.BlockSpec((tm, na), row),
            pl.BlockSpec((tm, nm), row),
            pl.BlockSpec((tm, nmg), row),
            pl.BlockSpec((SUBLANES, tm), lambda i: (0, i)),
        ] + ([pl.BlockSpec((tm, d), row)] if fuse else []),
        out_shape=[
            jax.ShapeDtypeStruct((t, na), BF16),
            jax.ShapeDtypeStruct((t, nm), BF16),
            jax.ShapeDtypeStruct((t, nmg), BF16),
            jax.ShapeDtypeStruct((SUBLANES, t), F32),
        ] + ([jax.ShapeDtypeStruct((t, d), F32)] if fuse else []),
        scratch_shapes=[pltpu.VMEM((tm, ng), F32)],
        compiler_params=_params("arbitrary"),
        name="inproj",
    )(x, *(moe if fuse else ()), sc, sh, nw, wa, wm, wg, wmg)


def _rope(t, cos, sin):
    w = t.shape[1]
    reps = w // LANES
    cosw = jnp.concatenate([cos] * reps, axis=1) if reps > 1 else cos
    sinw = jnp.concatenate([sin] * reps, axis=1) if reps > 1 else sin
    lane = lax.broadcasted_iota(I32, t.shape, 1)
    half = ROPE_DIM // 2
    up = pltpu.roll(t, w - half, axis=1)
    dn = pltpu.roll(t, half, axis=1)
    partner = jnp.where((lane % ROPE_DIM) < half, up, dn)
    return t * cosw + partner * sinw


def _head_norm(t, bd, w):
    ms = _dot((t * t).astype(BF16), bd)
    return t * lax.rsqrt(ms + EPS) * w


def _attn_kernel(sink_ref, cur_ref, prev_ref, cos_ref, sin_ref, cosp_ref, sinp_ref,
                 qw_ref, kw_ref, bdq_ref, bdk_ref, o_ref):
    tq = cur_ref.shape[0]
    nj = tq // ATTN_BLOCK
    qw = N_HEADS * HEAD_DIM
    kw = N_KV * HEAD_DIM
    blk0 = pl.program_id(1) * nj

    cur = cur_ref[...]
    q = cur[:, :qw].astype(F32)
    kc = cur[:, qw:qw + kw].astype(F32)
    vc = cur[:, qw + kw:].astype(F32)
    prev = prev_ref[...]
    kp = prev[:, :kw].astype(F32)
    vp = prev[:, kw:].astype(F32)

    cos, sin = cos_ref[...], sin_ref[...]
    q = _rope(_head_norm(q, bdq_ref[...], qw_ref[...]), cos, sin)
    kc = _rope(_head_norm(kc, bdk_ref[...], kw_ref[...]), cos, sin)
    kp = _rope(_head_norm(kp, bdk_ref[...], kw_ref[...]), cosp_ref[...], sinp_ref[...])
    qb = q.astype(BF16)

    def both_halves(x2):
        swapped = pltpu.roll(x2, HEAD_DIM, axis=1)
        first = lax.broadcasted_iota(I32, x2.shape, 1) < HEAD_DIM
        return jnp.concatenate([jnp.where(first, x2, swapped), jnp.where(first, swapped, x2)], axis=1).astype(BF16)

    k_all = both_halves(jnp.concatenate([kp, kc], axis=0))
    v_all = both_halves(jnp.concatenate([vp, vc], axis=0))

    lane = lax.broadcasted_iota(I32, (ATTN_BLOCK, LANES), 1)
    lo = lane < HEAD_DIM
    zero = jnp.zeros((ATTN_BLOCK, LANES), BF16)
    g_heads = N_HEADS // N_KV
    ri = lax.broadcasted_iota(I32, (g_heads * ATTN_BLOCK, ATTN_BLOCK), 0) % ATTN_BLOCK
    ci = lax.broadcasted_iota(I32, (g_heads * ATTN_BLOCK, ATTN_BLOCK), 1)
    from_prev = ci > ri
    head_row = lax.broadcasted_iota(I32, (g_heads * ATTN_BLOCK, 1), 0) // ATTN_BLOCK
    ones_v = jnp.ones((2 * ATTN_BLOCK, LANES), BF16)

    tiles = [(j, g) for j in range(nj) for g in range(N_KV)]
    scores = {}
    for j, g in tiles:
        rows = slice(j * ATTN_BLOCK, (j + 1) * ATTN_BLOCK)
        band = slice(j * ATTN_BLOCK, (j + 2) * ATTN_BLOCK)
        qp0 = qb[rows, (2 * g) * LANES:(2 * g + 1) * LANES]
        qp1 = qb[rows, (2 * g + 1) * LANES:(2 * g + 2) * LANES]
        q4 = jnp.concatenate([jnp.where(lo, qp0, zero), jnp.where(lo, zero, qp0),
                              jnp.where(lo, qp1, zero), jnp.where(lo, zero, qp1)], axis=0)
        scores[j, g] = _dot_nt(q4, k_all[band, g * LANES:(g + 1) * LANES])

    probs, sink_term = {}, {}
    for j, g in tiles:
        prev_ok = ci > ri + (1 - jnp.minimum(blk0 + j, 1)) * ATTN_BLOCK
        s2 = scores[j, g]
        s = jnp.where(prev_ok, s2[:, :ATTN_BLOCK], jnp.where(from_prev, -jnp.inf, s2[:, ATTN_BLOCK:]))
        sink = jnp.full((g_heads * ATTN_BLOCK, 1), sink_ref[g_heads * g], F32)
        for r in range(1, g_heads):
            sink = jnp.where(head_row == r, sink_ref[g_heads * g + r], sink)
        m = jnp.maximum(jnp.max(s, axis=-1, keepdims=True), sink)
        p = jnp.exp(s - m)
        probs[j, g] = jnp.concatenate([jnp.where(from_prev, p, 0.0), jnp.where(from_prev, 0.0, p)],
                                      axis=1).astype(BF16)
        sink_term[j, g] = jnp.exp(sink - m)

    for j, g in tiles:
        rows = slice(j * ATTN_BLOCK, (j + 1) * ATTN_BLOCK)
        band = slice(j * ATTN_BLOCK, (j + 2) * ATTN_BLOCK)
        o8 = _dot(probs[j, g], jnp.concatenate([v_all[band, g * LANES:(g + 1) * LANES], ones_v], axis=1))
        o4 = o8[:, :LANES] / (o8[:, LANES:] + sink_term[j, g])
        b = ATTN_BLOCK
        o_ref[rows, (2 * g) * LANES:(2 * g + 1) * LANES] = jnp.where(lo, o4[0:b], o4[b:2 * b]).astype(BF16)
        o_ref[rows, (2 * g + 1) * LANES:(2 * g + 2) * LANES] = jnp.where(
            lo, o4[2 * b:3 * b], o4[3 * b:4 * b]).astype(BF16)


def _attention(a_in, cos_t, sin_t, sinks_l, qw, kw, bdq, bdk, batch, seq):
    t = a_in.shape[0]
    tq = min(TILE_ATTN, seq)
    nj = tq // ATTN_BLOCK
    tpb = seq // tq
    bpb = seq // ATTN_BLOCK
    qwid = N_HEADS * HEAD_DIM
    kvw = 2 * N_KV * HEAD_DIM
    cur = lambda b, i: (b * tpb + i, 0)
    prv = lambda b, i: (b * bpb + jnp.maximum(i * nj - 1, 0), qwid // kvw)
    prv0 = lambda b, i: (b * bpb + jnp.maximum(i * nj - 1, 0), 0)
    const = lambda b, i: (0, 0)
    return pl.pallas_call(
        _attn_kernel,
        grid=(batch, tpb),
        in_specs=[
            pl.BlockSpec(memory_space=pltpu.SMEM),
            pl.BlockSpec((tq, qwid + kvw), cur),
            pl.BlockSpec((ATTN_BLOCK, kvw), prv),
            pl.BlockSpec((tq, LANES), cur),
            pl.BlockSpec((tq, LANES), cur),
            pl.BlockSpec((ATTN_BLOCK, LANES), prv0),
            pl.BlockSpec((ATTN_BLOCK, LANES), prv0),
            pl.BlockSpec((1, qwid), const),
            pl.BlockSpec((1, kvw // 2), const),
            pl.BlockSpec((qwid, qwid), const),
            pl.BlockSpec((kvw // 2, kvw // 2), const),
        ],
        out_specs=pl.BlockSpec((tq, qwid), cur),
        out_shape=jax.ShapeDtypeStruct((t, qwid), BF16),
        compiler_params=_params("arbitrary", "arbitrary"),
        name="swa_attention",
    )(sinks_l, a_in, a_in, cos_t, sin_t, cos_t, sin_t, qw, kw, bdq, bdk)


def _mlstm_kernel(min_ref, gr_ref, cw_ref, cb_ref, bcol_ref, nw_ref,
                  hm_ref, ext_ref, q_ref, kt_ref, st_ref, mx_ref, ab_ref, bc_ref):
    tt = min_ref.shape[0]
    mw = M_HEADS * M_DIM
    nchunks = tt // CHUNK

    @pl.when(pl.program_id(1) == 0)
    def _():
        ext_ref[0:SUBLANES, :] = jnp.zeros((SUBLANES, 2 * mw), F32)
        st_ref[...] = jnp.zeros(st_ref.shape, F32)
        mx_ref[...] = jnp.zeros(mx_ref.shape, F32)

    def conv_block(cols):
        u = min_ref[:, cols].astype(F32)
        ext_ref[SUBLANES:SUBLANES + tt, cols] = u
        acc = cb_ref[:, cols] + cw_ref[CONV_K - 1:CONV_K, cols] * u
        for jj in range(CONV_K - 1):
            off = SUBLANES - (CONV_K - 1) + jj
            acc = acc + cw_ref[jj:jj + 1, cols] * ext_ref[off:off + tt, cols]
        ext_ref[0:SUBLANES, cols] = u[tt - SUBLANES:tt, :]
        return acc * _sigmoid(acc)

    def q_body(h, carry):
        cols = pl.ds(pl.multiple_of(h * M_DIM, M_DIM), M_DIM)
        q_ref[:, cols] = conv_block(cols).astype(BF16)
        return carry

    def k_body(h, carry):
        off = pl.multiple_of(h * M_DIM, M_DIM)
        act = conv_block(pl.ds(mw + off, M_DIM)) * (M_DIM ** -0.5)
        for j in range(nchunks):
            kt_ref[pl.ds(off, M_DIM), j * CHUNK:(j + 1) * CHUNK] = act[j * CHUNK:(j + 1) * CHUNK, :].T
        return carry

    lax.fori_loop(0, M_HEADS, q_body, 0)
    lax.fori_loop(0, M_HEADS, k_body, 0)

    ri = lax.broadcasted_iota(I32, (CHUNK, CHUNK), 0)
    ci = lax.broadcasted_iota(I32, (CHUNK, CHUNK), 1)
    causal = ci <= ri
    triu = jnp.where(ri <= ci, 1.0, 0.0).astype(BF16)
    ones_half = jnp.ones((CHUNK, M_DIM), BF16)
    mean_mat = jnp.full((M_DIM, M_DIM), 1.0 / M_DIM, BF16)
    sub = lax.broadcasted_iota(I32, (SUBLANES, CHUNK), 0)
    heads = range(M_HEADS)

    pad_rows = jnp.zeros((CHUNK - SUBLANES, CHUNK), F32)
    zero_rows = jnp.zeros((SUBLANES, CHUNK), F32)

    def gate_body(jg, carry):
        for u_ in range(GATE_CHUNKS_PER_STEP):
            rs = pl.ds(pl.multiple_of((jg * GATE_CHUNKS_PER_STEP + u_) * CHUNK, CHUNK), CHUNK)
            gr = gr_ref[:, rs] + bcol_ref[...]
            ls = _log_sigmoid(gr)
            ls1 = ls.astype(BF16).astype(F32)
            ls2 = (ls - ls1).astype(BF16).astype(F32)
            pieces = jnp.concatenate([ls1, ls2, ls - ls1 - ls2, zero_rows], axis=0).astype(BF16)
            sums = _dot(pieces, triu)
            br = sums[0:SUBLANES] + sums[SUBLANES:2 * SUBLANES] + sums[2 * SUBLANES:3 * SUBLANES]
            ab = jnp.where(sub < M_HEADS, gr - pltpu.roll(br, M_HEADS, axis=0), br)
            ab_ref[:, rs] = ab
            bc_ref[rs, :] = jnp.concatenate([ab, pad_rows], axis=0).T
        return carry

    lax.fori_loop(0, nchunks // GATE_CHUNKS_PER_STEP, gate_body, 0)

    def group_body(cg, carry):
        rows, ab = [], []
        for u_ in range(CHUNKS_PER_STEP):
            r0 = pl.multiple_of((cg * CHUNKS_PER_STEP + u_) * CHUNK, CHUNK)
            rows.append(pl.ds(r0, CHUNK))
            ab.append(ab_ref[:, rows[u_]])
        lanes = [(u_, h) for u_ in range(CHUNKS_PER_STEP) for h in heads]
        a_r = {(u_, h): ab[u_][h:h + 1, :] for u_, h in lanes}
        b_last = {(u_, h): ab[u_][M_HEADS + h:M_HEADS + h + 1, CHUNK - 1:CHUNK] for u_, h in lanes}

        m_prev, a_max, a_dec, s_in = {}, {}, {}, {}
        m_run = [mx_ref[h][0:1, 0:1] for h in heads]
        for k in lanes:
            u_, h = k
            m_prev[k] = m_run[h]
            a_max[k] = jnp.max(a_r[k], axis=-1, keepdims=True)
            m_loc = b_last[k] + a_max[k]
            m_new = jnp.maximum(b_last[k] + m_prev[k], m_loc)
            a_dec[k] = jnp.exp(b_last[k] + m_prev[k] - m_new)
            s_in[k] = jnp.exp(m_loc - m_new)
            m_run[h] = m_new
        for h in heads:
            mx_ref[h] = jnp.broadcast_to(m_run[h], (SUBLANES, LANES))

        q, v_ext, s_qk, kv = {}, {}, {}, {}
        for k in lanes:
            u_, h = k
            rs = rows[u_]
            q[k] = q_ref[rs, h * M_DIM:(h + 1) * M_DIM]
            kt = kt_ref[h * M_DIM:(h + 1) * M_DIM, rs]
            v = min_ref[rs, 2 * mw + h * M_DIM:2 * mw + (h + 1) * M_DIM]
            v_ext[k] = jnp.concatenate([v, ones_half], axis=1)
            s_qk[k] = _dot(q[k], kt.astype(BF16))
            e_r = jnp.exp(a_r[k] - a_max[k])
            kv[k] = _dot((kt * e_r).astype(BF16), v_ext[k])

        thr, qk, inter = {}, {}, {}
        for k in lanes:
            u_, h = k
            a_mat = jnp.where(causal, a_r[k], -jnp.inf)
            mu = jnp.maximum(jnp.max(a_mat, axis=-1, keepdims=True), m_prev[k])
            b_c = bc_ref[rows[u_], M_HEADS + h:M_HEADS + h + 1]
            thr[k] = jnp.broadcast_to(jnp.exp(-(b_c + mu)), (CHUNK, M_DIM))
            mu_b = jnp.broadcast_to(mu, (CHUNK, CHUNK))
            inter[k] = jnp.exp(m_prev[k] - mu_b)
            qk[k] = (s_qk[k] * jnp.exp(a_mat - mu_b)).astype(BF16)

        q_state = {}
        state = [st_ref[h] for h in heads]
        for k in lanes:
            u_, h = k
            q_state[k] = _dot(q[k], state[h].astype(BF16))
            state[h] = a_dec[k] * state[h] + s_in[k] * kv[k]
        for h in heads:
            st_ref[h] = state[h]

        for k in lanes:
            u_, h = k
            hs = slice(h * M_DIM, (h + 1) * M_DIM)
            num = _dot(qk[k], v_ext[k])
            den = jnp.maximum(jnp.abs(num[:, M_DIM:] + inter[k] * q_state[k][:, M_DIM:]), thr[k])
            hh = (num[:, :M_DIM] + inter[k] * q_state[k][:, :M_DIM]) / den
            msq = _dot((hh * hh).astype(BF16), mean_mat)
            hn = hh * lax.rsqrt(msq + EPS) * nw_ref[:, hs]
            og = min_ref[rows[u_], 3 * mw + h * M_DIM:3 * mw + (h + 1) * M_DIM].astype(F32)
            hm_ref[rows[u_], hs] = (_sigmoid(og) * hn).astype(BF16)
        return carry

    lax.fori_loop(0, nchunks // CHUNKS_PER_STEP, group_body, 0)


def _mlstm(m_in, grow, conv_w, conv_b, bcol, nw, batch, seq):
    t = m_in.shape[0]
    tt = min(TILE_MLSTM, seq)
    tpb = seq // tt
    mw = M_HEADS * M_DIM
    cur = lambda b, i: (b * tpb + i, 0)
    const = lambda b, i: (0, 0)
    return pl.pallas_call(
        _mlstm_kernel,
        grid=(batch, tpb),
        in_specs=[
            pl.BlockSpec((tt, 4 * mw), cur),
            pl.BlockSpec((SUBLANES, tt), lambda b, i: (0, b * tpb + i)),
            pl.BlockSpec((CONV_K, 2 * mw), const),
            pl.BlockSpec((1, 2 * mw), const),
            pl.BlockSpec((SUBLANES, LANES), const),
            pl.BlockSpec((1, mw), const),
        ],
        out_specs=pl.BlockSpec((tt, mw), cur),
        out_shape=jax.ShapeDtypeStruct((t, mw), BF16),
        scratch_shapes=[
            pltpu.VMEM((tt + SUBLANES, 2 * mw), F32),
            pltpu.VMEM((tt, mw), BF16),
            pltpu.VMEM((mw, tt), F32),
            pltpu.VMEM((M_HEADS, M_DIM, 2 * M_DIM), F32),
            pltpu.VMEM((M_HEADS, SUBLANES, LANES), F32),
            pltpu.VMEM((SUBLANES, tt), F32),
            pltpu.VMEM((tt, LANES), F32),
        ],
        compiler_params=_params("arbitrary", "arbitrary"),
        name="mlstm",
    )(m_in, grow, conv_w, conv_b, bcol, nw)


def _merge_kernel(o_ref, hm_ref, mg_ref, x_ref, g1_ref, sc_ref, sh_ref, nw_ref,
                  wa_ref, wm_ref, wo_ref, rwh_ref, rwl_ref, rb_ref, tri_ref,
                  x1_ref, pay_ref, route_ref, cnt_ref, carry_ref):
    tm, d = x_ref.shape

    @pl.when(pl.program_id(0) == 0)
    def _():
        carry_ref[...] = jnp.zeros(carry_ref.shape, F32)

    ya = _dot(o_ref[...], wa_ref[...])
    yb = _dot(hm_ref[...], wm_ref[...])
    mg = mg_ref[...]
    merged = mg[:, :d].astype(F32) * ya + mg[:, d:].astype(F32) * yb
    x1 = x_ref[...] + g1_ref[...] * _dot(merged.astype(BF16), wo_ref[...])
    x1_ref[...] = x1

    ms = jnp.mean(x1 * x1, axis=-1, keepdims=True)
    h2 = x1 * lax.rsqrt(ms + EPS) * nw_ref[...]
    h2 = h2 * (1.0 + sc_ref[...]) + sh_ref[...]
    hi = h2.astype(BF16)
    hif = hi.astype(F32)
    lo = (h2 - hif).astype(BF16)
    r_hi = _dot_nt(rwh_ref[...], hi)
    r_lo = _dot_nt(rwl_ref[...], lo)
    sc_t = _sigmoid(r_hi[:N_EXPERTS] + r_hi[N_EXPERTS:] + r_lo)
    sel_t = sc_t + rb_ref[:, 0:1]

    def row(a, e):
        return a[e:e + 1, :]

    best = None
    gi = jnp.zeros((1, tm), I32)
    for g in range(N_GROUPS):
        r = [row(sel_t, EPG * g + i) for i in range(EPG)]
        gs = None
        for i in range(EPG):
            for j in range(i + 1, EPG):
                pr = r[i] + r[j]
                gs = pr if gs is None else jnp.maximum(gs, pr)
        if best is None:
            best = gs
        else:
            upd = gs > best
            gi = jnp.where(upd, g, gi)
            best = jnp.maximum(best, gs)

    def pick(a, i):
        out = row(a, i)
        for g in range(1, N_GROUPS):
            out = jnp.where(gi == g, row(a, EPG * g + i), out)
        return out

    v = [pick(sel_t, i) for i in range(EPG)]
    s = [pick(sc_t, i) for i in range(EPG)]

    def argmax4(vals):
        bv, bi = vals[0], jnp.zeros((1, tm), I32)
        for i in range(1, EPG):
            upd = vals[i] > bv
            bi = jnp.where(upd, i, bi)
            bv = jnp.maximum(bv, vals[i])
        return bi

    i1 = argmax4(v)
    i2 = argmax4([jnp.where(i1 == i, -jnp.inf, v[i]) for i in range(EPG)])
    ia = jnp.minimum(i1, i2)
    ib = jnp.maximum(i1, i2)
    pidx = jnp.where(ia == 0, ib - 1, jnp.where(ia == 1, jnp.where(ib == 3, 3, 5), 4))
    bucket = gi * N_PAIRS + pidx

    def by_index(vals, idx):
        out = vals[0]
        for i in range(1, EPG):
            out = jnp.where(idx == i, vals[i], out)
        return out

    swap = pidx == N_PAIRS - 1
    s_lo, s_hi = by_index(s, ia), by_index(s, ib)
    s_a, s_b = jnp.where(swap, s_hi, s_lo), jnp.where(swap, s_lo, s_hi)
    gate_a = s_a / (s_a + s_b)
    gate_b = s_b / (s_a + s_b)

    brow = lax.broadcasted_iota(I32, (BUCKET_ROWS, tm), 0)
    onehot = brow == bucket
    cums = _dot(jnp.where(onehot, 1.0, 0.0).astype(BF16), tri_ref[...])
    carry = carry_ref[...]
    rank = jnp.sum(jnp.where(onehot, carry[:, 0:1] + cums, 0.0), axis=0, keepdims=True) - 1.0
    new_carry = carry + cums[:, tm - 1:tm]
    carry_ref[...] = new_carry
    cnt_ref[...] = new_carry

    route_ref[...] = jnp.concatenate(
        [bucket.astype(F32), gate_a, gate_b, rank, jnp.zeros((SUBLANES - 4, tm), F32)], axis=0)

    half = d // 2
    packed = _pack_bf16_pairs(hif, rounded=True)
    for cpart in range(half // LANES):
        pay_ref[cpart] = packed[:, cpart * LANES:(cpart + 1) * LANES]
    gates_t = jnp.concatenate([gate_a, gate_b, jnp.zeros((LANES - 2, tm), F32)], axis=0)
    pay_ref[half // LANES] = lax.bitcast_convert_type(gates_t.T, U32)


def _merge(o_attn, hm, mg, x, g1, sc2, sh2, nw, wa, wm, wo, rwh, rwl, rb, tri, l, seq):
    t, d = x.shape
    tm = tri.shape[0]
    tpb = seq // tm
    row = lambda i: (i, 0)
    bsel = lambda i: (i // tpb, 0, 0)
    wsel = lambda i: (l, 0, 0)
    const = lambda i: (0, 0)
    hw = o_attn.shape[1]
    return pl.pallas_call(
        _merge_kernel,
        grid=(t // tm,),
        in_specs=[
            pl.BlockSpec((tm, hw), row),
            pl.BlockSpec((tm, hw), row),
            pl.BlockSpec((tm, 2 * d), row),
            pl.BlockSpec((tm, d), row),
            pl.BlockSpec((None, 1, d), bsel),
            pl.BlockSpec((None, 1, d), bsel),
            pl.BlockSpec((None, 1, d), bsel),
            pl.BlockSpec((None, 1, d), wsel),
            pl.BlockSpec((None, hw, d), wsel),
            pl.BlockSpec((None, hw, d), wsel),
            pl.BlockSpec((None, d, d), wsel),
            pl.BlockSpec((2 * N_EXPERTS, d), const),
            pl.BlockSpec((N_EXPERTS, d), const),
            pl.BlockSpec((N_EXPERTS, LANES), const),
            pl.BlockSpec((tm, tm), const),
        ],
        out_specs=[
            pl.BlockSpec((tm, d), row),
            pl.BlockSpec((PAY_PARTS, tm, LANES), lambda i: (0, i, 0)),
            pl.BlockSpec((SUBLANES, tm), lambda i: (0, i)),
            pl.BlockSpec((BUCKET_ROWS, LANES), const),
        ],
        out_shape=[
            jax.ShapeDtypeStruct((t, d), F32),
            jax.ShapeDtypeStruct((PAY_PARTS, t, LANES), U32),
            jax.ShapeDtypeStruct((SUBLANES, t), F32),
            jax.ShapeDtypeStruct((BUCKET_ROWS, LANES), F32),
        ],
        scratch_shapes=[pltpu.VMEM((BUCKET_ROWS, LANES), F32)],
        compiler_params=_params("arbitrary"),
        name="merge_router",
    )(o_attn, hm, mg, x, g1, sc2, sh2, nw, wa, wm, wo, rwh, rwl, rb, tri)


def _sc_mesh():
    return plsc.VectorSubcoreMesh(core_axis_name="core", subcore_axis_name="subcore")


def _sc_scatter_rows(rows, dest, n_out):
    n, w = rows.shape

    @pl.kernel(out_type=jax.ShapeDtypeStruct((n_out, w), rows.dtype), mesh=_sc_mesh(), scratch_types=[])
    def scatter(x_hbm, i_hbm, o_hbm):
        def body(x_vmem, i_vmem):
            pltpu.sync_copy(x_vmem, o_hbm.at[i_vmem.at[0]])

        pltpu.emit_pipeline(
            body,
            grid=(n // SC_WINDOW,),
            in_specs=[pl.BlockSpec((SC_WINDOW, w), lambda i: (i, 0)),
                      pl.BlockSpec((1, SC_WINDOW), lambda i: (0, i))],
            out_specs=[],
            core_axis_name=("core", "subcore"),
            dimension_semantics=(pltpu.PARALLEL,),
        )(x_hbm, i_hbm)

    return scatter(rows, dest.reshape(1, n))


def _sc_gather_rows(src, idx):
    n = idx.shape[0]
    w = src.shape[1]

    @pl.kernel(out_type=jax.ShapeDtypeStruct((n, w), src.dtype), mesh=_sc_mesh(), scratch_types=[])
    def gather(x_hbm, i_hbm, o_hbm):
        def body(i_vmem, o_vmem):
            pltpu.sync_copy(x_hbm.at[i_vmem.at[0]], o_vmem)

        pltpu.emit_pipeline(
            body,
            grid=(n // SC_WINDOW,),
            in_specs=[pl.BlockSpec((1, SC_WINDOW), lambda i: (0, i))],
            out_specs=[pl.BlockSpec((SC_WINDOW, w), lambda i: (i, 0))],
            core_axis_name=("core", "subcore"),
            dimension_semantics=(pltpu.PARALLEL,),
        )(i_hbm, o_hbm)

    return gather(src, idx.reshape(1, n))


def _row_index_kernel(ps_ref, route_ref, o_ref, *, n_rows):
    bucket = route_ref[0:1, :].astype(I32)
    start = jnp.zeros(bucket.shape, I32)
    for b in range(N_BUCKETS):
        start = jnp.where(bucket == b, ps_ref[b], start)
    dest = start + route_ref[3:4, :].astype(I32)
    part = lax.broadcasted_iota(I32, o_ref.shape, 0)
    o_ref[...] = part * n_rows + dest


def _row_index(pad_starts, route, n_rows):
    t = route.shape[1]
    tm = min(TILE_ROW_INDEX, t)
    return pl.pallas_call(
        functools.partial(_row_index_kernel, n_rows=n_rows),
        grid=(t // tm,),
        in_specs=[pl.BlockSpec(memory_space=pltpu.SMEM), pl.BlockSpec((SUBLANES, tm), lambda i: (0, i))],
        out_specs=pl.BlockSpec((SUBLANES, tm), lambda i: (0, i)),
        out_shape=jax.ShapeDtypeStruct((SUBLANES, t), I32),
        compiler_params=_params("arbitrary"),
        name="row_index",
    )(pad_starts, route)


def _residual_kernel(x_ref, y_ref, g2_ref, o_ref):
    y = _unpack_bf16_pairs(jnp.concatenate([y_ref[c] for c in range(OUT_PARTS)], axis=1))
    o_ref[...] = x_ref[...] + g2_ref[...] * y


def _residual(x1, ytok, g2, seq):
    t, d = x1.shape
    tm = min(TILE_RESIDUAL, seq)
    tpb = seq // tm
    return pl.pallas_call(
        _residual_kernel,
        grid=(t // tm,),
        in_specs=[
            pl.BlockSpec((tm, d), lambda i: (i, 0)),
            pl.BlockSpec((OUT_PARTS, tm, LANES), lambda i: (0, i, 0)),
            pl.BlockSpec((None, 1, d), lambda i: (i // tpb, 0, 0)),
        ],
        out_specs=pl.BlockSpec((tm, d), lambda i: (i, 0)),
        out_shape=jax.ShapeDtypeStruct((t, d), F32),
        compiler_params=_params("arbitrary"),
        name="residual",
    )(x1, ytok, g2)


def _expert_kernel(ea_ref, eb_ref, nr_ref, xs_ref, wga_ref, wua_ref, wda_ref, wgb_ref, wub_ref, wdb_ref, ys_ref,
                   ga_ref, ua_ref, da_ref, gb_ref, ub_ref, db_ref):
    j = pl.program_id(0)
    nr = nr_ref[0]
    prev = jnp.maximum(j - 1, 0)

    @pl.when((j == 0) | (ea_ref[j] != ea_ref[prev]))
    def _():
        ga_ref[...] = wga_ref[...].astype(BF16)
        ua_ref[...] = wua_ref[...].astype(BF16)
        da_ref[...] = wda_ref[...].astype(BF16)

    @pl.when((j == 0) | (eb_ref[j] != eb_ref[prev]))
    def _():
        gb_ref[...] = wgb_ref[...].astype(BF16)
        ub_ref[...] = wub_ref[...].astype(BF16)
        db_ref[...] = wdb_ref[...].astype(BF16)

    @pl.when(j < nr)
    def _():
        x = _unpack_bf16_pairs(jnp.concatenate([xs_ref[c] for c in range(PAY_PARTS - 1)], axis=1)).astype(BF16)
        gl = lax.bitcast_convert_type(xs_ref[PAY_PARTS - 1], F32)

        def ffn(wg_ref, wu_ref, wd_ref):
            gte = _dot(x, wg_ref[...])
            act = gte * _sigmoid(gte) * _dot(x, wu_ref[...])
            return _dot(act.astype(BF16), wd_ref[...])

        y = _pack_bf16_pairs(gl[:, 0:1] * ffn(ga_ref, ua_ref, da_ref) + gl[:, 1:2] * ffn(gb_ref, ub_ref, db_ref))
        for c in range(OUT_PARTS):
            ys_ref[c] = y[:, c * LANES:(c + 1) * LANES]

    @pl.when(j >= nr)
    def _():
        ys_ref[...] = jnp.zeros(ys_ref.shape, U32)


def _experts(blk_ea, blk_eb, n_real, xs, wg, wu, wd, d):
    n_rows = xs.shape[1]
    nblk = n_rows // EXPERT_BLOCK
    f = wg.shape[2]
    grid_spec = pltpu.PrefetchScalarGridSpec(
        num_scalar_prefetch=3,
        grid=(nblk,),
        in_specs=[
            pl.BlockSpec((PAY_PARTS, EXPERT_BLOCK, LANES), lambda j, ea, eb, nr: (0, j, 0)),
            pl.BlockSpec((None, d, f), lambda j, ea, eb, nr: (ea[j], 0, 0)),
            pl.BlockSpec((None, d, f), lambda j, ea, eb, nr: (ea[j], 0, 0)),
            pl.BlockSpec((None, f, d), lambda j, ea, eb, nr: (ea[j], 0, 0)),
            pl.BlockSpec((None, d, f), lambda j, ea, eb, nr: (eb[j], 0, 0)),
            pl.BlockSpec((None, d, f), lambda j, ea, eb, nr: (eb[j], 0, 0)),
            pl.BlockSpec((None, f, d), lambda j, ea, eb, nr: (eb[j], 0, 0)),
        ],
        out_specs=pl.BlockSpec((OUT_PARTS, EXPERT_BLOCK, LANES), lambda j, ea, eb, nr: (0, j, 0)),
        scratch_shapes=[pltpu.VMEM((d, f), BF16), pltpu.VMEM((d, f), BF16), pltpu.VMEM((f, d), BF16)] * 2,
    )
    return pl.pallas_call(
        _expert_kernel,
        grid_spec=grid_spec,
        out_shape=jax.ShapeDtypeStruct((OUT_PARTS, n_rows, LANES), U32),
        compiler_params=_params("arbitrary"),
        name="experts",
    )(blk_ea, blk_eb, n_real, xs, wg, wu, wd, wg, wu, wd)


_PAIR_A = (0, 0, 0, 1, 2, 2)
_PAIR_B = (1, 2, 3, 3, 3, 1)


def kernel(x, c, positions, ada_w, ada_b, norm_mix_w, norm_ffn_w, w_in, b_igate, b_fgate, q_norm_w, k_norm_w,
           sinks, conv_w, conv_b, mlstm_norm_w, w_attn_up, w_mlstm_up, w_out, router_w, router_bias,
           w_gate, w_up, w_down):
    batch, seq, d = x.shape
    depth = w_in.shape[0]
    t = batch * seq
    qw = N_HEADS * HEAD_DIM
    kvw = N_KV * HEAD_DIM
    mw = M_HEADS * M_DIM

    o = 0
    cols = {}
    for name, wdt in (("q", qw), ("k", kvw), ("v", kvw), ("mqk", 2 * mw), ("mv", mw), ("mi", M_HEADS),
                      ("mf", M_HEADS), ("mo", mw), ("ga", d), ("gb", d)):
        cols[name] = (o, o + wdt)
        o += wdt

    def wc(name, lo=0, hi=None):
        s, e = cols[name]
        return w_in[:, :, s + lo:(s + hi if hi is not None else e)]

    w_a = jnp.concatenate([wc("q"), wc("k"), wc("v")], axis=2).astype(BF16)
    w_m = jnp.concatenate([wc("mqk"), wc("mv"), wc("mo")], axis=2).astype(BF16)
    w_g = jnp.concatenate([wc("mi"), wc("mf"), jnp.zeros((depth, d, LANES - 2 * M_HEADS), F32)], axis=2).astype(BF16)
    w_mg = jnp.concatenate([wc("ga"), wc("gb")], axis=2).astype(BF16)
    w_au = w_attn_up.astype(BF16)
    w_mu = w_mlstm_up.astype(BF16)
    w_o = w_out.astype(BF16)
    n_e = w_gate.shape[1]
    w_g8 = w_gate.reshape(depth * n_e, d, -1)
    w_u8 = w_up.reshape(depth * n_e, d, -1)
    w_d = w_down.reshape(depth * n_e, -1, d)

    rw_t = router_w.astype(F32).T
    rw_top = rw_t.astype(BF16)
    rw_hi = jnp.concatenate([rw_top, (rw_t - rw_top.astype(F32)).astype(BF16)], axis=0)
    rw_lo = rw_top
    rb = jnp.broadcast_to(router_bias.astype(F32)[:, None], (n_e, LANES))

    qn_w = jnp.tile(q_norm_w * (HEAD_DIM ** -0.5), (1, N_HEADS)).reshape(depth, 1, qw)
    kn_w = jnp.tile(k_norm_w, (1, N_KV)).reshape(depth, 1, kvw)
    seg = jnp.arange(qw) // HEAD_DIM
    bdq = jnp.where(seg[:, None] == seg[None, :], 1.0 / HEAD_DIM, 0.0).astype(BF16)
    bdk = bdq[:kvw, :kvw]

    inv_freq = ROPE_THETA ** (-(jnp.arange(0, ROPE_DIM, 2, dtype=F32) / ROPE_DIM))
    ang = positions.astype(F32).reshape(1, t) * inv_freq[:, None]
    cos8, sin8 = jnp.cos(ang).T, jnp.sin(ang).T
    pad1 = jnp.ones((t, HEAD_DIM - ROPE_DIM), F32)
    pad0 = jnp.zeros((t, HEAD_DIM - ROPE_DIM), F32)
    cos_t = jnp.tile(jnp.concatenate([cos8, cos8, pad1], axis=1), (1, LANES // HEAD_DIM))
    sin_t = jnp.tile(jnp.concatenate([-sin8, sin8, pad0], axis=1), (1, LANES // HEAD_DIM))

    gate_bias = jnp.concatenate([b_igate, b_fgate], axis=1).astype(F32)
    bcol = jnp.broadcast_to(gate_bias[:, :, None], (depth, 2 * M_HEADS, LANES))

    tm_merge = min(TILE_MERGE, seq)
    ii = jnp.arange(tm_merge)
    tri = (ii[:, None] <= ii[None, :]).astype(BF16)

    n_blk = (t + N_BUCKETS * (EXPERT_BLOCK - 1)) // EXPERT_BLOCK + 1
    n_rows = n_blk * EXPERT_BLOCK
    pair_a = jnp.asarray(_PAIR_A, I32)
    pair_b = jnp.asarray(_PAIR_B, I32)

    c_pad = jnp.zeros((SUBLANES, d), F32).at[:batch].set(c)
    mod = _ada_mod(c_pad, ada_w, ada_b)[:, :batch]

    xf = x.reshape(t, d)
    moe = None
    for l in range(depth):
        sh1, sc1, g1, sh2, sc2, g2 = [m.reshape(batch, 1, d) for m in jnp.split(mod[l], 6, axis=-1)]

        outs = _inproj(xf, moe, sc1, sh1, norm_mix_w.reshape(depth, 1, d), w_a, w_m, w_g, w_mg, l, seq)
        a_in, m_in, mg, grow = outs[:4]
        if moe is not None:
            xf = outs[4]
        o_attn = _attention(a_in, cos_t, sin_t, sinks[l], qn_w[l], kn_w[l], bdq, bdk, batch, seq)
        hm = _mlstm(m_in, grow, conv_w[l], conv_b[l].reshape(1, -1), bcol[l], mlstm_norm_w[l].reshape(1, mw),
                    batch, seq)
        x1, pay, route, cnt = _merge(o_attn, hm, mg, xf, g1, sc2, sh2, norm_ffn_w.reshape(depth, 1, d),
                                     w_au, w_mu, w_o, rw_hi, rw_lo, rb, tri, l, seq)

        counts = cnt[:N_BUCKETS, 0].astype(I32)
        padded = (counts + EXPERT_BLOCK - 1) // EXPERT_BLOCK * EXPERT_BLOCK
        pad_ends = jnp.cumsum(padded)
        pad_starts = pad_ends - padded
        row_idx = _row_index(jnp.concatenate([pad_starts, jnp.zeros((BUCKET_ROWS - N_BUCKETS,), I32)]), route, n_rows)
        blk_start = jnp.arange(n_blk, dtype=I32) * EXPERT_BLOCK
        blk_bucket = jnp.minimum(jnp.sum((pad_ends[None, :] <= blk_start[:, None]).astype(I32), axis=1), N_BUCKETS - 1)
        grp = blk_bucket // N_PAIRS
        blk_ea = (l * n_e + grp * EPG + pair_a[blk_bucket % N_PAIRS]).astype(I32)
        blk_eb = (l * n_e + grp * EPG + pair_b[blk_bucket % N_PAIRS]).astype(I32)
        n_real = (pad_ends[-1:] // EXPERT_BLOCK).astype(I32)

        xs = _sc_scatter_rows(pay.reshape(PAY_PARTS * t, LANES), row_idx[:PAY_PARTS].reshape(-1),
                              PAY_PARTS * n_rows).reshape(PAY_PARTS, n_rows, LANES)
        ys = _experts(blk_ea, blk_eb, n_real, xs, w_g8, w_u8, w_d, d)
        ytok = _sc_gather_rows(ys.reshape(OUT_PARTS * n_rows, LANES),
                               row_idx[:OUT_PARTS].reshape(-1)).reshape(OUT_PARTS, t, LANES)
        xf, moe = x1, (ytok, g2)
    return _residual(xf, moe[0], moe[1], seq).reshape(batch, seq, d)
```

```python
import functools

import jax
import jax.numpy as jnp
from jax import lax
from jax.experimental import pallas as pl
from jax.experimental.pallas import tpu as pltpu
from jax.experimental.pallas import tpu_sc as plsc

F32 = jnp.float32
BF16 = jnp.bfloat16
U32 = jnp.uint32
I32 = jnp.int32
HIGHEST = lax.Precision.HIGHEST

HEAD_DIM = 64
N_HEADS = 8
N_KV = 2
ROPE_DIM = 16
ROPE_THETA = 500000.0
ATTN_BLOCK = 128
M_HEADS = 4
M_DIM = 128
CONV_K = 4
N_EXPERTS = 16
N_GROUPS = 4
EPG = 4
EPS = 1e-6

LANES = 128
SUBLANES = 8

TILE_INPROJ = 1024
TILE_ATTN = 1024
TILE_MLSTM = 2048
TILE_MERGE = 1024
TILE_RESIDUAL = 1024
TILE_ROW_INDEX = 8192
ADA_COLS = 3072
CHUNK = 128
CHUNKS_PER_STEP = 4
GATE_CHUNKS_PER_STEP = 4
N_PAIRS = 6
N_BUCKETS = N_GROUPS * N_PAIRS
BUCKET_ROWS = 32
EXPERT_BLOCK = 256
PAY_PARTS = 5
OUT_PARTS = 4
SC_WINDOW = 128
VMEM_LIMIT = 56 * 1024 * 1024


def _dot(a, b, precision=None):
    return jnp.dot(a, b, preferred_element_type=F32, precision=precision)


def _dot_nt(a, b):
    return lax.dot_general(a, b, (((1,), (1,)), ((), ())), preferred_element_type=F32)


def _sigmoid(x):
    return 1.0 / (1.0 + jnp.exp(-x))


def _log_sigmoid(x):
    return jnp.minimum(x, 0.0) - jnp.log1p(jnp.exp(-jnp.abs(x)))


def _pack_bf16_pairs(v, rounded=False):
    n = v.shape[1] // 2
    bits = lax.bitcast_convert_type(v if rounded else v.astype(BF16).astype(F32), U32)
    return bits[:, :n] | (bits[:, n:] >> 16)


def _unpack_bf16_pairs(w):
    hi = lax.bitcast_convert_type(w & jnp.uint32(0xFFFF0000), F32)
    lo = lax.bitcast_convert_type(w << 16, F32)
    return jnp.concatenate([hi, lo], axis=1)


def _params(*sem):
    return pltpu.CompilerParams(dimension_semantics=sem, vmem_limit_bytes=VMEM_LIMIT)


def _ada_kernel(c_ref, w_ref, b_ref, o_ref):
    c = c_ref[...]
    ca = c * _sigmoid(c)
    o_ref[0] = _dot(ca, w_ref[0], HIGHEST) + b_ref[0]


def _ada_mod(c_pad, ada_w, ada_b):
    depth, d, n = ada_w.shape
    tn = ADA_COLS
    return pl.pallas_call(
        _ada_kernel,
        grid=(depth, n // tn),
        in_specs=[
            pl.BlockSpec((SUBLANES, d), lambda l, j: (0, 0)),
            pl.BlockSpec((1, d, tn), lambda l, j: (l, 0, j)),
            pl.BlockSpec((1, 1, tn), lambda l, j: (l, 0, j)),
        ],
        out_specs=pl.BlockSpec((1, SUBLANES, tn), lambda l, j: (l, 0, j)),
        out_shape=jax.ShapeDtypeStruct((depth, SUBLANES, n), F32),
        compiler_params=_params("arbitrary", "arbitrary"),
        name="ada_mod",
    )(c_pad, ada_w, ada_b.reshape(depth, 1, n))


def _inproj_kernel(*refs, fuse_residual):
    if fuse_residual:
        (x_ref, y_ref, g2_ref, sc_ref, sh_ref, nw_ref, wa_ref, wm_ref, wg_ref, wmg_ref,
         a_ref, m_ref, mg_ref, gr_ref, xo_ref, g_ref) = refs
        y = _unpack_bf16_pairs(jnp.concatenate([y_ref[c] for c in range(OUT_PARTS)], axis=1))
        x = x_ref[...] + g2_ref[...] * y
        xo_ref[...] = x
    else:
        (x_ref, sc_ref, sh_ref, nw_ref, wa_ref, wm_ref, wg_ref, wmg_ref,
         a_ref, m_ref, mg_ref, gr_ref, g_ref) = refs
        x = x_ref[...]
    ms = jnp.mean(x * x, axis=-1, keepdims=True)
    h = x * lax.rsqrt(ms + EPS) * (nw_ref[...] * (1.0 + sc_ref[...])) + sh_ref[...]
    hb = h.astype(BF16)
    a_ref[...] = _dot(hb, wa_ref[...]).astype(BF16)
    m_ref[...] = _dot(hb, wm_ref[...]).astype(BF16)
    mg_ref[...] = _sigmoid(_dot(hb, wmg_ref[...])).astype(BF16)
    g_ref[...] = _dot(hb, wg_ref[...])
    gr_ref[...] = g_ref[...].T[:SUBLANES, :]


def _inproj(x, moe, sc, sh, nw, wa, wm, wg, wmg, l, seq):
    t, d = x.shape
    tm = min(TILE_INPROJ, seq)
    tpb = seq // tm
    row = lambda i: (i, 0)
    bsel = lambda i: (i // tpb, 0, 0)
    wsel = lambda i: (l, 0, 0)
    once = pl.Buffered(1)
    na, nm, ng, nmg = wa.shape[2], wm.shape[2], wg.shape[2], wmg.shape[2]
    fuse = moe is not None
    moe_specs = [pl.BlockSpec((OUT_PARTS, tm, LANES), lambda i: (0, i, 0)), pl.BlockSpec((None, 1, d), bsel)]
    return pl.pallas_call(
        functools.partial(_inproj_kernel, fuse_residual=fuse),
        grid=(t // tm,),
        in_specs=[pl.BlockSpec((tm, d), row)] + (moe_specs if fuse else []) + [
            pl.BlockSpec((None, 1, d), bsel),
            pl.BlockSpec((None, 1, d), bsel),
            pl.BlockSpec((None, 1, d), wsel),
            pl.BlockSpec((None, d, na), wsel, pipeline_mode=once),
            pl.BlockSpec((None, d, nm), wsel, pipeline_mode=once),
            pl.BlockSpec((None, d, ng), wsel, pipeline_mode=once),
            pl.BlockSpec((None, d, nmg), wsel, pipeline_mode=once),
        ],
        out_specs=[
            pl.BlockSpec((tm, na), row),
            pl.BlockSpec((tm, nm), row),
            pl.BlockSpec((tm, nmg), row),
            pl.BlockSpec((SUBLANES, tm), lambda i: (0, i)),
        ] + ([pl.BlockSpec((tm, d), row)] if fuse else []),
        out_shape=[
            jax.ShapeDtypeStruct((t, na), BF16),
            jax.ShapeDtypeStruct((t, nm), BF16),
            jax.ShapeDtypeStruct((t, nmg), BF16),
            jax.ShapeDtypeStruct((SUBLANES, t), F32),
        ] + ([jax.ShapeDtypeStruct((t, d), F32)] if fuse else []),
        scratch_shapes=[pltpu.VMEM((tm, ng), F32)],
        compiler_params=_params("arbitrary"),
        name="inproj",
    )(x, *(moe if fuse else ()), sc, sh, nw, wa, wm, wg, wmg)


def _rope(t, cos, sin):
    w = t.shape[1]
    reps = w // LANES
    cosw = jnp.concatenate([cos] * reps, axis=1) if reps > 1 else cos
    sinw = jnp.concatenate([sin] * reps, axis=1) if reps > 1 else sin
    lane = lax.broadcasted_iota(I32, t.shape, 1)
    half = ROPE_DIM // 2
    up = pltpu.roll(t, w - half, axis=1)
    dn = pltpu.roll(t, half, axis=1)
    partner = jnp.where((lane % ROPE_DIM) < half, up, dn)
    return t * cosw + partner * sinw


def _head_norm(t, bd, w):
    ms = _dot((t * t).astype(BF16), bd)
    return t * lax.rsqrt(ms + EPS) * w


def _attn_kernel(sink_ref, cur_ref, prev_ref, cos_ref, sin_ref, cosp_ref, sinp_ref,
                 qw_ref, kw_ref, bdq_ref, bdk_ref, o_ref):
    tq = cur_ref.shape[0]
    nj = tq // ATTN_BLOCK
    qw = N_HEADS * HEAD_DIM
    kw = N_KV * HEAD_DIM
    blk0 = pl.program_id(1) * nj

    cur = cur_ref[...]
    q = cur[:, :qw].astype(F32)
    kc = cur[:, qw:qw + kw].astype(F32)
    vc = cur[:, qw + kw:].astype(F32)
    prev = prev_ref[...]
    kp = prev[:, :kw].astype(F32)
    vp = prev[:, kw:].astype(F32)

    cos, sin = cos_ref[...], sin_ref[...]
    q = _rope(_head_norm(q, bdq_ref[...], qw_ref[...]), cos, sin)
    kc = _rope(_head_norm(kc, bdk_ref[...], kw_ref[...]), cos, sin)
    kp = _rope(_head_norm(kp, bdk_ref[...], kw_ref[...]), cosp_ref[...], sinp_ref[...])
    qb = q.astype(BF16)

    def both_halves(x2):
        swapped = pltpu.roll(x2, HEAD_DIM, axis=1)
        first = lax.broadcasted_iota(I32, x2.shape, 1) < HEAD_DIM
        return jnp.concatenate([jnp.where(first, x2, swapped), jnp.where(first, swapped, x2)], axis=1).astype(BF16)

    k_all = both_halves(jnp.concatenate([kp, kc], axis=0))
    v_all = both_halves(jnp.concatenate([vp, vc], axis=0))

    lane = lax.broadcasted_iota(I32, (ATTN_BLOCK, LANES), 1)
    lo = lane < HEAD_DIM
    zero = jnp.zeros((ATTN_BLOCK, LANES), BF16)
    g_heads = N_HEADS // N_KV
    ri = lax.broadcasted_iota(I32, (g_heads * ATTN_BLOCK, ATTN_BLOCK), 0) % ATTN_BLOCK
    ci = lax.broadcasted_iota(I32, (g_heads * ATTN_BLOCK, ATTN_BLOCK), 1)
    from_prev = ci > ri
    head_row = lax.broadcasted_iota(I32, (g_heads * ATTN_BLOCK, 1), 0) // ATTN_BLOCK
    ones_v = jnp.ones((2 * ATTN_BLOCK, LANES), BF16)

    tiles = [(j, g) for j in range(nj) for g in range(N_KV)]
    scores = {}
    for j, g in tiles:
        rows = slice(j * ATTN_BLOCK, (j + 1) * ATTN_BLOCK)
        band = slice(j * ATTN_BLOCK, (j + 2) * ATTN_BLOCK)
        qp0 = qb[rows, (2 * g) * LANES:(2 * g + 1) * LANES]
        qp1 = qb[rows, (2 * g + 1) * LANES:(2 * g + 2) * LANES]
        q4 = jnp.concatenate([jnp.where(lo, qp0, zero), jnp.where(lo, zero, qp0),
                              jnp.where(lo, qp1, zero), jnp.where(lo, zero, qp1)], axis=0)
        scores[j, g] = _dot_nt(q4, k_all[band, g * LANES:(g + 1) * LANES])

    probs, sink_term = {}, {}
    for j, g in tiles:
        prev_ok = ci > ri + (1 - jnp.minimum(blk0 + j, 1)) * ATTN_BLOCK
        s2 = scores[j, g]
        s = jnp.where(prev_ok, s2[:, :ATTN_BLOCK], jnp.where(from_prev, -jnp.inf, s2[:, ATTN_BLOCK:]))
        sink = jnp.full((g_heads * ATTN_BLOCK, 1), sink_ref[g_heads * g], F32)
        for r in range(1, g_heads):
            sink = jnp.where(head_row == r, sink_ref[g_heads * g + r], sink)
        m = jnp.maximum(jnp.max(s, axis=-1, keepdims=True), sink)
        p = jnp.exp(s - m)
        probs[j, g] = jnp.concatenate([jnp.where(from_prev, p, 0.0), jnp.where(from_prev, 0.0, p)],
                                      axis=1).astype(BF16)
        sink_term[j, g] = jnp.exp(sink - m)

    for j, g in tiles:
        rows = slice(j * ATTN_BLOCK, (j + 1) * ATTN_BLOCK)
        band = slice(j * ATTN_BLOCK, (j + 2) * ATTN_BLOCK)
        o8 = _dot(probs[j, g], jnp.concatenate([v_all[band, g * LANES:(g + 1) * LANES], ones_v], axis=1))
        o4 = o8[:, :LANES] / (o8[:, LANES:] + sink_term[j, g])
        b = ATTN_BLOCK
        o_ref[rows, (2 * g) * LANES:(2 * g + 1) * LANES] = jnp.where(lo, o4[0:b], o4[b:2 * b]).astype(BF16)
        o_ref[rows, (2 * g + 1) * LANES:(2 * g + 2) * LANES] = jnp.where(
            lo, o4[2 * b:3 * b], o4[3 * b:4 * b]).astype(BF16)


def _attention(a_in, cos_t, sin_t, sinks_l, qw, kw, bdq, bdk, batch, seq):
    t = a_in.shape[0]
    tq = min(TILE_ATTN, seq)
    nj = tq // ATTN_BLOCK
    tpb = seq // tq
    bpb = seq // ATTN_BLOCK
    qwid = N_HEADS * HEAD_DIM
    kvw = 2 * N_KV * HEAD_DIM
    cur = lambda b, i: (b * tpb + i, 0)
    prv = lambda b, i: (b * bpb + jnp.maximum(i * nj - 1, 0), qwid // kvw)
    prv0 = lambda b, i: (b * bpb + jnp.maximum(i * nj - 1, 0), 0)
    const = lambda b, i: (0, 0)
    return pl.pallas_call(
        _attn_kernel,
        grid=(batch, tpb),
        in_specs=[
            pl.BlockSpec(memory_space=pltpu.SMEM),
            pl.BlockSpec((tq, qwid + kvw), cur),
            pl.BlockSpec((ATTN_BLOCK, kvw), prv),
            pl.BlockSpec((tq, LANES), cur),
            pl.BlockSpec((tq, LANES), cur),
            pl.BlockSpec((ATTN_BLOCK, LANES), prv0),
            pl.BlockSpec((ATTN_BLOCK, LANES), prv0),
            pl.BlockSpec((1, qwid), const),
            pl.BlockSpec((1, kvw // 2), const),
            pl.BlockSpec((qwid, qwid), const),
            pl.BlockSpec((kvw // 2, kvw // 2), const),
        ],
        out_specs=pl.BlockSpec((tq, qwid), cur),
        out_shape=jax.ShapeDtypeStruct((t, qwid), BF16),
        compiler_params=_params("arbitrary", "arbitrary"),
        name="swa_attention",
    )(sinks_l, a_in, a_in, cos_t, sin_t, cos_t, sin_t, qw, kw, bdq, bdk)


def _mlstm_kernel(min_ref, gr_ref, cw_ref, cb_ref, bcol_ref, nw_ref,
                  hm_ref, ext_ref, q_ref, kt_ref, st_ref, mx_ref, ab_ref, bc_ref):
    tt = min_ref.shape[0]
    mw = M_HEADS * M_DIM
    nchunks = tt // CHUNK

    @pl.when(pl.program_id(1) == 0)
    def _():
        ext_ref[0:SUBLANES, :] = jnp.zeros((SUBLANES, 2 * mw), F32)
        st_ref[...] = jnp.zeros(st_ref.shape, F32)
        mx_ref[...] = jnp.zeros(mx_ref.shape, F32)

    def conv_block(cols):
        u = min_ref[:, cols].astype(F32)
        ext_ref[SUBLANES:SUBLANES + tt, cols] = u
        acc = cb_ref[:, cols] + cw_ref[CONV_K - 1:CONV_K, cols] * u
        for jj in range(CONV_K - 1):
            off = SUBLANES - (CONV_K - 1) + jj
            acc = acc + cw_ref[jj:jj + 1, cols] * ext_ref[off:off + tt, cols]
        ext_ref[0:SUBLANES, cols] = u[tt - SUBLANES:tt, :]
        return acc * _sigmoid(acc)

    def q_body(h, carry):
        cols = pl.ds(pl.multiple_of(h * M_DIM, M_DIM), M_DIM)
        q_ref[:, cols] = conv_block(cols).astype(BF16)
        return carry

    def k_body(h, carry):
        off = pl.multiple_of(h * M_DIM, M_DIM)
        act = conv_block(pl.ds(mw + off, M_DIM)) * (M_DIM ** -0.5)
        for j in range(nchunks):
            kt_ref[pl.ds(off, M_DIM), j * CHUNK:(j + 1) * CHUNK] = act[j * CHUNK:(j + 1) * CHUNK, :].T
        return carry

    lax.fori_loop(0, M_HEADS, q_body, 0)
    lax.fori_loop(0, M_HEADS, k_body, 0)

    ri = lax.broadcasted_iota(I32, (CHUNK, CHUNK), 0)
    ci = lax.broadcasted_iota(I32, (CHUNK, CHUNK), 1)
    causal = ci <= ri
    triu = jnp.where(ri <= ci, 1.0, 0.0).astype(BF16)
    ones_half = jnp.ones((CHUNK, M_DIM), BF16)
    mean_mat = jnp.full((M_DIM, M_DIM), 1.0 / M_DIM, BF16)
    sub = lax.broadcasted_iota(I32, (SUBLANES, CHUNK), 0)
    heads = range(M_HEADS)

    pad_rows = jnp.zeros((CHUNK - SUBLANES, CHUNK), F32)
    zero_rows = jnp.zeros((SUBLANES, CHUNK), F32)

    def gate_body(jg, carry):
        for u_ in range(GATE_CHUNKS_PER_STEP):
            rs = pl.ds(pl.multiple_of((jg * GATE_CHUNKS_PER_STEP + u_) * CHUNK, CHUNK), CHUNK)
            gr = gr_ref[:, rs] + bcol_ref[...]
            ls = _log_sigmoid(gr)
            ls1 = ls.astype(BF16).astype(F32)
            ls2 = (ls - ls1).astype(BF16).astype(F32)
            pieces = jnp.concatenate([ls1, ls2, ls - ls1 - ls2, zero_rows], axis=0).astype(BF16)
            sums = _dot(pieces, triu)
            br = sums[0:SUBLANES] + sums[SUBLANES:2 * SUBLANES] + sums[2 * SUBLANES:3 * SUBLANES]
            ab = jnp.where(sub < M_HEADS, gr - pltpu.roll(br, M_HEADS, axis=0), br)
            ab_ref[:, rs] = ab
            bc_ref[rs, :] = jnp.concatenate([ab, pad_rows], axis=0).T
        return carry

    lax.fori_loop(0, nchunks // GATE_CHUNKS_PER_STEP, gate_body, 0)

    def group_body(cg, carry):
        rows, ab = [], []
        for u_ in range(CHUNKS_PER_STEP):
            r0 = pl.multiple_of((cg * CHUNKS_PER_STEP + u_) * CHUNK, CHUNK)
            rows.append(pl.ds(r0, CHUNK))
            ab.append(ab_ref[:, rows[u_]])
        lanes = [(u_, h) for u_ in range(CHUNKS_PER_STEP) for h in heads]
        a_r = {(u_, h): ab[u_][h:h + 1, :] for u_, h in lanes}
        b_last = {(u_, h): ab[u_][M_HEADS + h:M_HEADS + h + 1, CHUNK - 1:CHUNK] for u_, h in lanes}

        m_prev, a_max, a_dec, s_in = {}, {}, {}, {}
        m_run = [mx_ref[h][0:1, 0:1] for h in heads]
        for k in lanes:
            u_, h = k
            m_prev[k] = m_run[h]
            a_max[k] = jnp.max(a_r[k], axis=-1, keepdims=True)
            m_loc = b_last[k] + a_max[k]
            m_new = jnp.maximum(b_last[k] + m_prev[k], m_loc)
            a_dec[k] = jnp.exp(b_last[k] + m_prev[k] - m_new)
            s_in[k] = jnp.exp(m_loc - m_new)
            m_run[h] = m_new
        for h in heads:
            mx_ref[h] = jnp.broadcast_to(m_run[h], (SUBLANES, LANES))

        q, v_ext, s_qk, kv = {}, {}, {}, {}
        for k in lanes:
            u_, h = k
            rs = rows[u_]
            q[k] = q_ref[rs, h * M_DIM:(h + 1) * M_DIM]
            kt = kt_ref[h * M_DIM:(h + 1) * M_DIM, rs]
            v = min_ref[rs, 2 * mw + h * M_DIM:2 * mw + (h + 1) * M_DIM]
            v_ext[k] = jnp.concatenate([v, ones_half], axis=1)
            s_qk[k] = _dot(q[k], kt.astype(BF16))
            e_r = jnp.exp(a_r[k] - a_max[k])
            kv[k] = _dot((kt * e_r).astype(BF16), v_ext[k])

        thr, qk, inter = {}, {}, {}
        for k in lanes:
            u_, h = k
            a_mat = jnp.where(causal, a_r[k], -jnp.inf)
            mu = jnp.maximum(jnp.max(a_mat, axis=-1, keepdims=True), m_prev[k])
            b_c = bc_ref[rows[u_], M_HEADS + h:M_HEADS + h + 1]
            thr[k] = jnp.broadcast_to(jnp.exp(-(b_c + mu)), (CHUNK, M_DIM))
            mu_b = jnp.broadcast_to(mu, (CHUNK, CHUNK))
            inter[k] = jnp.exp(m_prev[k] - mu_b)
            qk[k] = (s_qk[k] * jnp.exp(a_mat - mu_b)).astype(BF16)

        q_state = {}
        state = [st_ref[h] for h in heads]
        for k in lanes:
            u_, h = k
            q_state[k] = _dot(q[k], state[h].astype(BF16))
            state[h] = a_dec[k] * state[h] + s_in[k] * kv[k]
        for h in heads:
            st_ref[h] = state[h]

        for k in lanes:
            u_, h = k
            hs = slice(h * M_DIM, (h + 1) * M_DIM)
            num = _dot(qk[k], v_ext[k])
            den = jnp.maximum(jnp.abs(num[:, M_DIM:] + inter[k] * q_state[k][:, M_DIM:]), thr[k])
            hh = (num[:, :M_DIM] + inter[k] * q_state[k][:, :M_DIM]) / den
            msq = _dot((hh * hh).astype(BF16), mean_mat)
            hn = hh * lax.rsqrt(msq + EPS) * nw_ref[:, hs]
            og = min_ref[rows[u_], 3 * mw + h * M_DIM:3 * mw + (h + 1) * M_DIM].astype(F32)
            hm_ref[rows[u_], hs] = (_sigmoid(og) * hn).astype(BF16)
        return carry

    lax.fori_loop(0, nchunks // CHUNKS_PER_STEP, group_body, 0)


def _mlstm(m_in, grow, conv_w, conv_b, bcol, nw, batch, seq):
    t = m_in.shape[0]
    tt = min(TILE_MLSTM, seq)
    tpb = seq // tt
    mw = M_HEADS * M_DIM
    cur = lambda b, i: (b * tpb + i, 0)
    const = lambda b, i: (0, 0)
    return pl.pallas_call(
        _mlstm_kernel,
        grid=(batch, tpb),
        in_specs=[
            pl.BlockSpec((tt, 4 * mw), cur),
            pl.BlockSpec((SUBLANES, tt), lambda b, i: (0, b * tpb + i)),
            pl.BlockSpec((CONV_K, 2 * mw), const),
            pl.BlockSpec((1, 2 * mw), const),
            pl.BlockSpec((SUBLANES, LANES), const),
            pl.BlockSpec((1, mw), const),
        ],
        out_specs=pl.BlockSpec((tt, mw), cur),
        out_shape=jax.ShapeDtypeStruct((t, mw), BF16),
        scratch_shapes=[
            pltpu.VMEM((tt + SUBLANES, 2 * mw), F32),
            pltpu.VMEM((tt, mw), BF16),
            pltpu.VMEM((mw, tt), F32),
            pltpu.VMEM((M_HEADS, M_DIM, 2 * M_DIM), F32),
            pltpu.VMEM((M_HEADS, SUBLANES, LANES), F32),
            pltpu.VMEM((SUBLANES, tt), F32),
            pltpu.VMEM((tt, LANES), F32),
        ],
        compiler_params=_params("arbitrary", "arbitrary"),
        name="mlstm",
    )(m_in, grow, conv_w, conv_b, bcol, nw)


def _merge_kernel(o_ref, hm_ref, mg_ref, x_ref, g1_ref, sc_ref, sh_ref, nw_ref,
                  wa_ref, wm_ref, wo_ref, rwh_ref, rwl_ref, rb_ref, tri_ref,
                  x1_ref, pay_ref, route_ref, cnt_ref, carry_ref):
    tm, d = x_ref.shape

    @pl.when(pl.program_id(0) == 0)
    def _():
        carry_ref[...] = jnp.zeros(carry_ref.shape, F32)

    ya = _dot(o_ref[...], wa_ref[...])
    yb = _dot(hm_ref[...], wm_ref[...])
    mg = mg_ref[...]
    merged = mg[:, :d].astype(F32) * ya + mg[:, d:].astype(F32) * yb
    x1 = x_ref[...] + g1_ref[...] * _dot(merged.astype(BF16), wo_ref[...])
    x1_ref[...] = x1

    ms = jnp.mean(x1 * x1, axis=-1, keepdims=True)
    h2 = x1 * lax.rsqrt(ms + EPS) * (nw_ref[...] * (1.0 + sc_ref[...])) + sh_ref[...]
    hi = h2.astype(BF16)
    hif = hi.astype(F32)
    lo = (h2 - hif).astype(BF16)
    r_hi = _dot_nt(rwh_ref[...], hi)
    r_lo = _dot_nt(rwl_ref[...], lo)
    sc_t = _sigmoid(r_hi[:N_EXPERTS] + r_hi[N_EXPERTS:] + r_lo)
    sel_t = sc_t + rb_ref[:, 0:1]

    def row(a, e):
        return a[e:e + 1, :]

    best = None
    gi = jnp.zeros((1, tm), I32)
    for g in range(N_GROUPS):
        r = [row(sel_t, EPG * g + i) for i in range(EPG)]
        gs = None
        for i in range(EPG):
            for j in range(i + 1, EPG):
                pr = r[i] + r[j]
                gs = pr if gs is None else jnp.maximum(gs, pr)
        if best is None:
            best = gs
        else:
            upd = gs > best
            gi = jnp.where(upd, g, gi)
            best = jnp.maximum(best, gs)

    def pick(a, i):
        out = row(a, i)
        for g in range(1, N_GROUPS):
            out = jnp.where(gi == g, row(a, EPG * g + i), out)
        return out

    v = [pick(sel_t, i) for i in range(EPG)]
    s = [pick(sc_t, i) for i in range(EPG)]

    def argmax4(vals):
        bv, bi = vals[0], jnp.zeros((1, tm), I32)
        for i in range(1, EPG):
            upd = vals[i] > bv
            bi = jnp.where(upd, i, bi)
            bv = jnp.maximum(bv, vals[i])
        return bi

    i1 = argmax4(v)
    i2 = argmax4([jnp.where(i1 == i, -jnp.inf, v[i]) for i in range(EPG)])
    ia = jnp.minimum(i1, i2)
    ib = jnp.maximum(i1, i2)
    pidx = jnp.where(ia == 0, ib - 1, jnp.where(ia == 1, jnp.where(ib == 3, 3, 5), 4))
    bucket = gi * N_PAIRS + pidx

    def by_index(vals, idx):
        out = vals[0]
        for i in range(1, EPG):
            out = jnp.where(idx == i, vals[i], out)
        return out

    swap = pidx == N_PAIRS - 1
    s_lo, s_hi = by_index(s, ia), by_index(s, ib)
    s_a, s_b = jnp.where(swap, s_hi, s_lo), jnp.where(swap, s_lo, s_hi)
    gate_a = s_a / (s_a + s_b)
    gate_b = s_b / (s_a + s_b)

    brow = lax.broadcasted_iota(I32, (BUCKET_ROWS, tm), 0)
    onehot = brow == bucket
    cums = _dot(jnp.where(onehot, 1.0, 0.0).astype(BF16), tri_ref[...])
    carry = carry_ref[...]
    rank = jnp.sum(jnp.where(onehot, carry[:, 0:1] + cums, 0.0), axis=0, keepdims=True) - 1.0
    new_carry = carry + cums[:, tm - 1:tm]
    carry_ref[...] = new_carry
    cnt_ref[...] = new_carry

    route_ref[...] = jnp.concatenate(
        [bucket.astype(F32), gate_a, gate_b, rank, jnp.zeros((SUBLANES - 4, tm), F32)], axis=0)

    half = d // 2
    packed = _pack_bf16_pairs(hif, rounded=True)
    for cpart in range(half // LANES):
        pay_ref[cpart] = packed[:, cpart * LANES:(cpart + 1) * LANES]
    gates_t = jnp.concatenate([gate_a, gate_b, jnp.zeros((LANES - 2, tm), F32)], axis=0)
    pay_ref[half // LANES] = lax.bitcast_convert_type(gates_t.T, U32)


def _merge(o_attn, hm, mg, x, g1, sc2, sh2, nw, wa, wm, wo, rwh, rwl, rb, tri, l, seq):
    t, d = x.shape
    tm = tri.shape[0]
    tpb = seq // tm
    row = lambda i: (i, 0)
    bsel = lambda i: (i // tpb, 0, 0)
    wsel = lambda i: (l, 0, 0)
    const = lambda i: (0, 0)
    hw = o_attn.shape[1]
    return pl.pallas_call(
        _merge_kernel,
        grid=(t // tm,),
        in_specs=[
            pl.BlockSpec((tm, hw), row),
            pl.BlockSpec((tm, hw), row),
            pl.BlockSpec((tm, 2 * d), row),
            pl.BlockSpec((tm, d), row),
            pl.BlockSpec((None, 1, d), bsel),
            pl.BlockSpec((None, 1, d), bsel),
            pl.BlockSpec((None, 1, d), bsel),
            pl.BlockSpec((None, 1, d), wsel),
            pl.BlockSpec((None, hw, d), wsel),
            pl.BlockSpec((None, hw, d), wsel),
            pl.BlockSpec((None, d, d), wsel),
            pl.BlockSpec((2 * N_EXPERTS, d), const),
            pl.BlockSpec((N_EXPERTS, d), const),
            pl.BlockSpec((N_EXPERTS, LANES), const),
            pl.BlockSpec((tm, tm), const),
        ],
        out_specs=[
            pl.BlockSpec((tm, d), row),
            pl.BlockSpec((PAY_PARTS, tm, LANES), lambda i: (0, i, 0)),
            pl.BlockSpec((SUBLANES, tm), lambda i: (0, i)),
            pl.BlockSpec((BUCKET_ROWS, LANES), const),
        ],
        out_shape=[
            jax.ShapeDtypeStruct((t, d), F32),
            jax.ShapeDtypeStruct((PAY_PARTS, t, LANES), U32),
            jax.ShapeDtypeStruct((SUBLANES, t), F32),
            jax.ShapeDtypeStruct((BUCKET_ROWS, LANES), F32),
        ],
        scratch_shapes=[pltpu.VMEM((BUCKET_ROWS, LANES), F32)],
        compiler_params=_params("arbitrary"),
        name="merge_router",
    )(o_attn, hm, mg, x, g1, sc2, sh2, nw, wa, wm, wo, rwh, rwl, rb, tri)


def _sc_mesh():
    return plsc.VectorSubcoreMesh(core_axis_name="core", subcore_axis_name="subcore")


def _sc_scatter_rows(rows, dest, n_out):
    n, w = rows.shape

    @pl.kernel(out_type=jax.ShapeDtypeStruct((n_out, w), rows.dtype), mesh=_sc_mesh(), scratch_types=[])
    def scatter(x_hbm, i_hbm, o_hbm):
        def body(x_vmem, i_vmem):
            pltpu.sync_copy(x_vmem, o_hbm.at[i_vmem.at[0]])

        pltpu.emit_pipeline(
            body,
            grid=(n // SC_WINDOW,),
            in_specs=[pl.BlockSpec((SC_WINDOW, w), lambda i: (i, 0)),
                      pl.BlockSpec((1, SC_WINDOW), lambda i: (0, i))],
            out_specs=[],
            core_axis_name=("core", "subcore"),
            dimension_semantics=(pltpu.PARALLEL,),
        )(x_hbm, i_hbm)

    return scatter(rows, dest.reshape(1, n))


def _sc_gather_rows(src, idx):
    n = idx.shape[0]
    w = src.shape[1]

    @pl.kernel(out_type=jax.ShapeDtypeStruct((n, w), src.dtype), mesh=_sc_mesh(), scratch_types=[])
    def gather(x_hbm, i_hbm, o_hbm):
        def body(i_vmem, o_vmem):
            pltpu.sync_copy(x_hbm.at[i_vmem.at[0]], o_vmem)

        pltpu.emit_pipeline(
            body,
            grid=(n // SC_WINDOW,),
            in_specs=[pl.BlockSpec((1, SC_WINDOW), lambda i: (0, i))],
            out_specs=[pl.BlockSpec((SC_WINDOW, w), lambda i: (i, 0))],
            core_axis_name=("core", "subcore"),
            dimension_semantics=(pltpu.PARALLEL,),
        )(i_hbm, o_hbm)

    return gather(src, idx.reshape(1, n))


def _row_index_kernel(ps_ref, route_ref, o_ref, *, n_rows):
    bucket = route_ref[0:1, :].astype(I32)
    start = jnp.zeros(bucket.shape, I32)
    for b in range(N_BUCKETS):
        start = jnp.where(bucket == b, ps_ref[b], start)
    dest = start + route_ref[3:4, :].astype(I32)
    part = lax.broadcasted_iota(I32, o_ref.shape, 0)
    o_ref[...] = part * n_rows + dest


def _row_index(pad_starts, route, n_rows):
    t = route.shape[1]
    tm = min(TILE_ROW_INDEX, t)
    return pl.pallas_call(
        functools.partial(_row_index_kernel, n_rows=n_rows),
        grid=(t // tm,),
        in_specs=[pl.BlockSpec(memory_space=pltpu.SMEM), pl.BlockSpec((SUBLANES, tm), lambda i: (0, i))],
        out_specs=pl.BlockSpec((SUBLANES, tm), lambda i: (0, i)),
        out_shape=jax.ShapeDtypeStruct((SUBLANES, t), I32),
        compiler_params=_params("arbitrary"),
        name="row_index",
    )(pad_starts, route)


def _residual_kernel(x_ref, y_ref, g2_ref, o_ref):
    y = _unpack_bf16_pairs(jnp.concatenate([y_ref[c] for c in range(OUT_PARTS)], axis=1))
    o_ref[...] = x_ref[...] + g2_ref[...] * y


def _residual(x1, ytok, g2, seq):
    t, d = x1.shape
    tm = min(TILE_RESIDUAL, seq)
    tpb = seq // tm
    return pl.pallas_call(
        _residual_kernel,
        grid=(t // tm,),
        in_specs=[
            pl.BlockSpec((tm, d), lambda i: (i, 0)),
            pl.BlockSpec((OUT_PARTS, tm, LANES), lambda i: (0, i, 0)),
            pl.BlockSpec((None, 1, d), lambda i: (i // tpb, 0, 0)),
        ],
        out_specs=pl.BlockSpec((tm, d), lambda i: (i, 0)),
        out_shape=jax.ShapeDtypeStruct((t, d), F32),
        compiler_params=_params("arbitrary"),
        name="residual",
    )(x1, ytok, g2)


def _expert_kernel(ea_ref, eb_ref, nr_ref, xs_ref, wga_ref, wua_ref, wda_ref, wgb_ref, wub_ref, wdb_ref, ys_ref,
                   ga_ref, ua_ref, da_ref, gb_ref, ub_ref, db_ref):
    j = pl.program_id(0)
    nr = nr_ref[0]
    prev = jnp.maximum(j - 1, 0)

    @pl.when((j == 0) | (ea_ref[j] != ea_ref[prev]))
    def _():
        ga_ref[...] = wga_ref[...].astype(BF16)
        ua_ref[...] = wua_ref[...].astype(BF16)
        da_ref[...] = wda_ref[...].astype(BF16)

    @pl.when((j == 0) | (eb_ref[j] != eb_ref[prev]))
    def _():
        gb_ref[...] = wgb_ref[...].astype(BF16)
        ub_ref[...] = wub_ref[...].astype(BF16)
        db_ref[...] = wdb_ref[...].astype(BF16)

    @pl.when(j < nr)
    def _():
        x = _unpack_bf16_pairs(jnp.concatenate([xs_ref[c] for c in range(PAY_PARTS - 1)], axis=1)).astype(BF16)
        gl = lax.bitcast_convert_type(xs_ref[PAY_PARTS - 1], F32)

        def ffn(wg_ref, wu_ref, wd_ref):
            gte = _dot(x, wg_ref[...])
            act = gte * _sigmoid(gte) * _dot(x, wu_ref[...])
            return _dot(act.astype(BF16), wd_ref[...])

        y = _pack_bf16_pairs(gl[:, 0:1] * ffn(ga_ref, ua_ref, da_ref) + gl[:, 1:2] * ffn(gb_ref, ub_ref, db_ref))
        for c in range(OUT_PARTS):
            ys_ref[c] = y[:, c * LANES:(c + 1) * LANES]

    @pl.when(j >= nr)
    def _():
        ys_ref[...] = jnp.zeros(ys_ref.shape, U32)


def _experts(blk_ea, blk_eb, n_real, xs, wg, wu, wd, d):
    n_rows = xs.shape[1]
    nblk = n_rows // EXPERT_BLOCK
    f = wg.shape[2]
    grid_spec = pltpu.PrefetchScalarGridSpec(
        num_scalar_prefetch=3,
        grid=(nblk,),
        in_specs=[
            pl.BlockSpec((PAY_PARTS, EXPERT_BLOCK, LANES), lambda j, ea, eb, nr: (0, j, 0)),
            pl.BlockSpec((None, d, f), lambda j, ea, eb, nr: (ea[j], 0, 0)),
            pl.BlockSpec((None, d, f), lambda j, ea, eb, nr: (ea[j], 0, 0)),
            pl.BlockSpec((None, f, d), lambda j, ea, eb, nr: (ea[j], 0, 0)),
            pl.BlockSpec((None, d, f), lambda j, ea, eb, nr: (eb[j], 0, 0)),
            pl.BlockSpec((None, d, f), lambda j, ea, eb, nr: (eb[j], 0, 0)),
            pl.BlockSpec((None, f, d), lambda j, ea, eb, nr: (eb[j], 0, 0)),
        ],
        out_specs=pl.BlockSpec((OUT_PARTS, EXPERT_BLOCK, LANES), lambda j, ea, eb, nr: (0, j, 0)),
        scratch_shapes=[pltpu.VMEM((d, f), BF16), pltpu.VMEM((d, f), BF16), pltpu.VMEM((f, d), BF16)] * 2,
    )
    return pl.pallas_call(
        _expert_kernel,
        grid_spec=grid_spec,
        out_shape=jax.ShapeDtypeStruct((OUT_PARTS, n_rows, LANES), U32),
        compiler_params=_params("arbitrary"),
        name="experts",
    )(blk_ea, blk_eb, n_real, xs, wg, wu, wd, wg, wu, wd)


_PAIR_A = (0, 0, 0, 1, 2, 2)
_PAIR_B = (1, 2, 3, 3, 3, 1)


def kernel(x, c, positions, ada_w, ada_b, norm_mix_w, norm_ffn_w, w_in, b_igate, b_fgate, q_norm_w, k_norm_w,
           sinks, conv_w, conv_b, mlstm_norm_w, w_attn_up, w_mlstm_up, w_out, router_w, router_bias,
           w_gate, w_up, w_down):
    batch, seq, d = x.shape
    depth = w_in.shape[0]
    t = batch * seq
    qw = N_HEADS * HEAD_DIM
    kvw = N_KV * HEAD_DIM
    mw = M_HEADS * M_DIM

    o = 0
    cols = {}
    for name, wdt in (("q", qw), ("k", kvw), ("v", kvw), ("mqk", 2 * mw), ("mv", mw), ("mi", M_HEADS),
                      ("mf", M_HEADS), ("mo", mw), ("ga", d), ("gb", d)):
        cols[name] = (o, o + wdt)
        o += wdt

    def wc(name, lo=0, hi=None):
        s, e = cols[name]
        return w_in[:, :, s + lo:(s + hi if hi is not None else e)]

    w_a = jnp.concatenate([wc("q"), wc("k"), wc("v")], axis=2).astype(BF16)
    w_m = jnp.concatenate([wc("mqk"), wc("mv"), wc("mo")], axis=2).astype(BF16)
    w_g = jnp.concatenate([wc("mi"), wc("mf"), jnp.zeros((depth, d, LANES - 2 * M_HEADS), F32)], axis=2).astype(BF16)
    w_mg = jnp.concatenate([wc("ga"), wc("gb")], axis=2).astype(BF16)
    w_au = w_attn_up.astype(BF16)
    w_mu = w_mlstm_up.astype(BF16)
    w_o = w_out.astype(BF16)
    n_e = w_gate.shape[1]
    w_g8 = w_gate.reshape(depth * n_e, d, -1)
    w_u8 = w_up.reshape(depth * n_e, d, -1)
    w_d = w_down.reshape(depth * n_e, -1, d)

    rw_t = router_w.astype(F32).T
    rw_top = rw_t.astype(BF16)
    rw_hi = jnp.concatenate([rw_top, (rw_t - rw_top.astype(F32)).astype(BF16)], axis=0)
    rw_lo = rw_top
    rb = jnp.broadcast_to(router_bias.astype(F32)[:, None], (n_e, LANES))

    qn_w = jnp.tile(q_norm_w * (HEAD_DIM ** -0.5), (1, N_HEADS)).reshape(depth, 1, qw)
    kn_w = jnp.tile(k_norm_w, (1, N_KV)).reshape(depth, 1, kvw)
    seg = jnp.arange(qw) // HEAD_DIM
    bdq = jnp.where(seg[:, None] == seg[None, :], 1.0 / HEAD_DIM, 0.0).astype(BF16)
    bdk = bdq[:kvw, :kvw]

    inv_freq = ROPE_THETA ** (-(jnp.arange(0, ROPE_DIM, 2, dtype=F32) / ROPE_DIM))
    ang = positions.astype(F32).reshape(1, t) * inv_freq[:, None]
    cos8, sin8 = jnp.cos(ang).T, jnp.sin(ang).T
    pad1 = jnp.ones((t, HEAD_DIM - ROPE_DIM), F32)
    pad0 = jnp.zeros((t, HEAD_DIM - ROPE_DIM), F32)
    cos_t = jnp.tile(jnp.concatenate([cos8, cos8, pad1], axis=1), (1, LANES // HEAD_DIM))
    sin_t = jnp.tile(jnp.concatenate([-sin8, sin8, pad0], axis=1), (1, LANES // HEAD_DIM))

    gate_bias = jnp.concatenate([b_igate, b_fgate], axis=1).astype(F32)
    bcol = jnp.broadcast_to(gate_bias[:, :, None], (depth, 2 * M_HEADS, LANES))

    tm_merge = min(TILE_MERGE, seq)
    ii = jnp.arange(tm_merge)
    tri = (ii[:, None] <= ii[None, :]).astype(BF16)

    n_blk = (t + N_BUCKETS * (EXPERT_BLOCK - 1)) // EXPERT_BLOCK + 1
    n_rows = n_blk * EXPERT_BLOCK
    pair_a = jnp.asarray(_PAIR_A, I32)
    pair_b = jnp.asarray(_PAIR_B, I32)

    c_pad = jnp.zeros((SUBLANES, d), F32).at[:batch].set(c)
    mod = _ada_mod(c_pad, ada_w, ada_b)[:, :batch]

    xf = x.reshape(t, d)
    moe = None
    for l in range(depth):
        sh1, sc1, g1, sh2, sc2, g2 = [m.reshape(batch, 1, d) for m in jnp.split(mod[l], 6, axis=-1)]

        outs = _inproj(xf, moe, sc1, sh1, norm_mix_w.reshape(depth, 1, d), w_a, w_m, w_g, w_mg, l, seq)
        a_in, m_in, mg, grow = outs[:4]
        if moe is not None:
            xf = outs[4]
        o_attn = _attention(a_in, cos_t, sin_t, sinks[l], qn_w[l], kn_w[l], bdq, bdk, batch, seq)
        hm = _mlstm(m_in, grow, conv_w[l], conv_b[l].reshape(1, -1), bcol[l], mlstm_norm_w[l].reshape(1, mw),
                    batch, seq)
        x1, pay, route, cnt = _merge(o_attn, hm, mg, xf, g1, sc2, sh2, norm_ffn_w.reshape(depth, 1, d),
                                     w_au, w_mu, w_o, rw_hi, rw_lo, rb, tri, l, seq)

        counts = cnt[:N_BUCKETS, 0].astype(I32)
        padded = (counts + EXPERT_BLOCK - 1) // EXPERT_BLOCK * EXPERT_BLOCK
        pad_ends = jnp.cumsum(padded)
        pad_starts = pad_ends - padded
        row_idx = _row_index(jnp.concatenate([pad_starts, jnp.zeros((BUCKET_ROWS - N_BUCKETS,), I32)]), route, n_rows)
        blk_start = jnp.arange(n_blk, dtype=I32) * EXPERT_BLOCK
        blk_bucket = jnp.minimum(jnp.sum((pad_ends[None, :] <= blk_start[:, None]).astype(I32), axis=1), N_BUCKETS - 1)
        grp = blk_bucket // N_PAIRS
        blk_ea = (l * n_e + grp * EPG + pair_a[blk_bucket % N_PAIRS]).astype(I32)
        blk_eb = (l * n_e + grp * EPG + pair_b[blk_bucket % N_PAIRS]).astype(I32)
        n_real = (pad_ends[-1:] // EXPERT_BLOCK).astype(I32)

        xs = _sc_scatter_rows(pay.reshape(PAY_PARTS * t, LANES), row_idx[:PAY_PARTS].reshape(-1),
                              PAY_PARTS * n_rows).reshape(PAY_PARTS, n_rows, LANES)
        ys = _experts(blk_ea, blk_eb, n_real, xs, w_g8, w_u8, w_d, d)
        ytok = _sc_gather_rows(ys.reshape(OUT_PARTS * n_rows, LANES),
                               row_idx[:OUT_PARTS].reshape(-1)).reshape(OUT_PARTS, t, LANES)
        xf, moe = x1, (ytok, g2)
    return _residual(xf, moe[0], moe[1], seq).reshape(batch, seq, d)
```

```python
import functools

import jax
import jax.numpy as jnp
from jax import lax
from jax.experimental import pallas as pl
from jax.experimental.pallas import tpu as pltpu
from jax.experimental.pallas import tpu_sc as plsc

F32 = jnp.float32
BF16 = jnp.bfloat16
U32 = jnp.uint32
I32 = jnp.int32
HIGHEST = lax.Precision.HIGHEST

HEAD_DIM = 64
N_HEADS = 8
N_KV = 2
ROPE_DIM = 16
ROPE_THETA = 500000.0
ATTN_BLOCK = 128
M_HEADS = 4
M_DIM = 128
CONV_K = 4
N_EXPERTS = 16
N_GROUPS = 4
EPG = 4
EPS = 1e-6

LANES = 128
SUBLANES = 8

TILE_INPROJ = 1024
TILE_ATTN = 1024
TILE_MLSTM = 2048
TILE_MERGE = 1024
TILE_RESIDUAL = 1024
TILE_ROW_INDEX = 8192
ADA_COLS = 3072
CHUNK = 128
CHUNKS_PER_STEP = 4
GATE_CHUNKS_PER_STEP = 4
N_PAIRS = 6
N_BUCKETS = N_GROUPS * N_PAIRS
BUCKET_ROWS = 32
EXPERT_BLOCK = 256
PAY_PARTS = 5
OUT_PARTS = 4
SC_WINDOW = 128
VMEM_LIMIT = 56 * 1024 * 1024


def _dot(a, b, precision=None):
    return jnp.dot(a, b, preferred_element_type=F32, precision=precision)


def _dot_nt(a, b):
    return lax.dot_general(a, b, (((1,), (1,)), ((), ())), preferred_element_type=F32)


def _sigmoid(x):
    return 1.0 / (1.0 + jnp.exp(-x))


def _log_sigmoid(x):
    return jnp.minimum(x, 0.0) - jnp.log1p(jnp.exp(-jnp.abs(x)))


def _pack_bf16_pairs(v, rounded=False):
    n = v.shape[1] // 2
    bits = lax.bitcast_convert_type(v if rounded else v.astype(BF16).astype(F32), U32)
    return bits[:, :n] | (bits[:, n:] >> 16)


def _unpack_bf16_pairs(w):
    hi = lax.bitcast_convert_type(w & jnp.uint32(0xFFFF0000), F32)
    lo = lax.bitcast_convert_type(w << 16, F32)
    return jnp.concatenate([hi, lo], axis=1)


def _params(*sem):
    return pltpu.CompilerParams(dimension_semantics=sem, vmem_limit_bytes=VMEM_LIMIT)


def _ada_kernel(c_ref, w_ref, b_ref, o_ref):
    c = c_ref[...]
    ca = c * _sigmoid(c)
    o_ref[0] = _dot(ca, w_ref[0], HIGHEST) + b_ref[0]


def _ada_mod(c_pad, ada_w, ada_b):
    depth, d, n = ada_w.shape
    tn = ADA_COLS
    return pl.pallas_call(
        _ada_kernel,
        grid=(depth, n // tn),
        in_specs=[
            pl.BlockSpec((SUBLANES, d), lambda l, j: (0, 0)),
            pl.BlockSpec((1, d, tn), lambda l, j: (l, 0, j)),
            pl.BlockSpec((1, 1, tn), lambda l, j: (l, 0, j)),
        ],
        out_specs=pl.BlockSpec((1, SUBLANES, tn), lambda l, j: (l, 0, j)),
        out_shape=jax.ShapeDtypeStruct((depth, SUBLANES, n), F32),
        compiler_params=_params("arbitrary", "arbitrary"),
        name="ada_mod",
    )(c_pad, ada_w, ada_b.reshape(depth, 1, n))


def _inproj_kernel(*refs, fuse_residual):
    if fuse_residual:
        (x_ref, y_ref, g2_ref, sc_ref, sh_ref, nw_ref, wa_ref, wm_ref, wg_ref, wmg_ref,
         a_ref, m_ref, mg_ref, gr_ref, xo_ref, g_ref) = refs
        y = _unpack_bf16_pairs(jnp.concatenate([y_ref[c] for c in range(OUT_PARTS)], axis=1))
        x = x_ref[...] + g2_ref[...] * y
        xo_ref[...] = x
    else:
        (x_ref, sc_ref, sh_ref, nw_ref, wa_ref, wm_ref, wg_ref, wmg_ref,
         a_ref, m_ref, mg_ref, gr_ref, g_ref) = refs
        x = x_ref[...]
    ms = jnp.mean(x * x, axis=-1, keepdims=True)
    h = x * lax.rsqrt(ms + EPS) * (nw_ref[...] * (1.0 + sc_ref[...])) + sh_ref[...]
    hb = h.astype(BF16)
    a_ref[...] = _dot(hb, wa_ref[...]).astype(BF16)
    m_ref[...] = _dot(hb, wm_ref[...]).astype(BF16)
    mg_ref[...] = _sigmoid(_dot(hb, wmg_ref[...])).astype(BF16)
    g_ref[...] = _dot(hb, wg_ref[...])
    gr_ref[...] = g_ref[...].T[:SUBLANES, :]


def _inproj(x, moe, sc, sh, nw, wa, wm, wg, wmg, l, seq):
    t, d = x.shape
    tm = min(TILE_INPROJ, seq)
    tpb = seq // tm
    row = lambda i: (i, 0)
    bsel = lambda i: (i // tpb, 0, 0)
    wsel = lambda i: (l, 0, 0)
    once = pl.Buffered(1)
    na, nm, ng, nmg = wa.shape[2], wm.shape[2], wg.shape[2], wmg.shape[2]
    fuse = moe is not None
    moe_specs = [pl.BlockSpec((OUT_PARTS, tm, LANES), lambda i: (0, i, 0)), pl.BlockSpec((None, 1, d), bsel)]
    return pl.pallas_call(
        functools.partial(_inproj_kernel, fuse_residual=fuse),
        grid=(t // tm,),
        in_specs=[pl.BlockSpec((tm, d), row)] + (moe_specs if fuse else []) + [
            pl.BlockSpec((None, 1, d), bsel),
            pl.BlockSpec((None, 1, d), bsel),
            pl.BlockSpec((None, 1, d), wsel),
            pl.BlockSpec((None, d, na), wsel, pipeline_mode=once),
            pl.BlockSpec((None, d, nm), wsel, pipeline_mode=once),
            pl.BlockSpec((None, d, ng), wsel, pipeline_mode=once),
            pl.BlockSpec((None, d, nmg), wsel, pipeline_mode=once),
        ],
        out_specs=[
            pl.BlockSpec((tm, na), row),
            pl.BlockSpec((tm, nm), row),
            pl.BlockSpec((tm, nmg), row),
            pl.BlockSpec((SUBLANES, tm), lambda i: (0, i)),
        ] + ([pl.BlockSpec((tm, d), row)] if fuse else []),
        out_shape=[
            jax.ShapeDtypeStruct((t, na), BF16),
            jax.ShapeDtypeStruct((t, nm), BF16),
            jax.ShapeDtypeStruct((t, nmg), BF16),
            jax.ShapeDtypeStruct((SUBLANES, t), F32),
        ] + ([jax.ShapeDtypeStruct((t, d), F32)] if fuse else []),
        scratch_shapes=[pltpu.VMEM((tm, ng), F32)],
        compiler_params=_params("arbitrary"),
        name="inproj",
    )(x, *(moe if fuse else ()), sc, sh, nw, wa, wm, wg, wmg)


def _rope(t, cos, sin):
    w = t.shape[1]
    reps = w // LANES
    cosw = jnp.concatenate([cos] * reps, axis=1) if reps > 1 else cos
    sinw = jnp.concatenate([sin] * reps, axis=1) if reps > 1 else sin
    lane = lax.broadcasted_iota(I32, t.shape, 1)
    half = ROPE_DIM // 2
    up = pltpu.roll(t, w - half, axis=1)
    dn = pltpu.roll(t, half, axis=1)
    partner = jnp.where((lane % ROPE_DIM) < half, up, dn)
    return t * cosw + partner * sinw


def _head_norm(t, bd, w):
    ms = _dot((t * t).astype(BF16), bd)
    return t * lax.rsqrt(ms + EPS) * w


def _attn_kernel(sink_ref, cur_ref, prev_ref, cos_ref, sin_ref, cosp_ref, sinp_ref,
                 qw_ref, kw_ref, bdq_ref, bdk_ref, o_ref):
    tq = cur_ref.shape[0]
    nj = tq // ATTN_BLOCK
    qw = N_HEADS * HEAD_DIM
    kw = N_KV * HEAD_DIM
    blk0 = pl.program_id(1) * nj

    cur = cur_ref[...]
    q = cur[:, :qw].astype(F32)
    kc = cur[:, qw:qw + kw].astype(F32)
    vc = cur[:, qw + kw:].astype(F32)
    prev = prev_ref[...]
    kp = prev[:, :kw].astype(F32)
    vp = prev[:, kw:].astype(F32)

    cos, sin = cos_ref[...], sin_ref[...]
    q = _rope(_head_norm(q, bdq_ref[...], qw_ref[...]), cos, sin)
    kc = _rope(_head_norm(kc, bdk_ref[...], kw_ref[...]), cos, sin)
    kp = _rope(_head_norm(kp, bdk_ref[...], kw_ref[...]), cosp_ref[...], sinp_ref[...])
    qb = q.astype(BF16)

    def both_halves(x2):
        swapped = pltpu.roll(x2, HEAD_DIM, axis=1)
        first = lax.broadcasted_iota(I32, x2.shape, 1) < HEAD_DIM
        return jnp.concatenate([jnp.where(first, x2, swapped), jnp.where(first, swapped, x2)], axis=1).astype(BF16)

    k_all = both_halves(jnp.concatenate([kp, kc], axis=0))
    v_all = both_halves(jnp.concatenate([vp, vc], axis=0))

    lane = lax.broadcasted_iota(I32, (ATTN_BLOCK, LANES), 1)
    lo = lane < HEAD_DIM
    zero = jnp.zeros((ATTN_BLOCK, LANES), BF16)
    g_heads = N_HEADS // N_KV
    ri = lax.broadcasted_iota(I32, (g_heads * ATTN_BLOCK, ATTN_BLOCK), 0) % ATTN_BLOCK
    ci = lax.broadcasted_iota(I32, (g_heads * ATTN_BLOCK, ATTN_BLOCK), 1)
    from_prev = ci > ri
    head_row = lax.broadcasted_iota(I32, (g_heads * ATTN_BLOCK, 1), 0) // ATTN_BLOCK
    ones_v = jnp.ones((2 * ATTN_BLOCK, LANES), BF16)

    tiles = [(j, g) for j in range(nj) for g in range(N_KV)]
    scores = {}
    for j, g in tiles:
        rows = slice(j * ATTN_BLOCK, (j + 1) * ATTN_BLOCK)
        band = slice(j * ATTN_BLOCK, (j + 2) * ATTN_BLOCK)
        qp0 = qb[rows, (2 * g) * LANES:(2 * g + 1) * LANES]
        qp1 = qb[rows, (2 * g + 1) * LANES:(2 * g + 2) * LANES]
        q4 = jnp.concatenate([jnp.where(lo, qp0, zero), jnp.where(lo, zero, qp0),
                              jnp.where(lo, qp1, zero), jnp.where(lo, zero, qp1)], axis=0)
        scores[j, g] = _dot_nt(q4, k_all[band, g * LANES:(g + 1) * LANES])

    probs, sink_term = {}, {}
    for j, g in tiles:
        s2 = scores[j, g]
        if j == 0:
            prev_ok = ci > ri + (1 - jnp.minimum(blk0, 1)) * ATTN_BLOCK
            s = jnp.where(prev_ok, s2[:, :ATTN_BLOCK], jnp.where(from_prev, -jnp.inf, s2[:, ATTN_BLOCK:]))
        else:
            s = jnp.where(from_prev, s2[:, :ATTN_BLOCK], s2[:, ATTN_BLOCK:])
        sink = jnp.full((g_heads * ATTN_BLOCK, 1), sink_ref[g_heads * g], F32)
        for r in range(1, g_heads):
            sink = jnp.where(head_row == r, sink_ref[g_heads * g + r], sink)
        m = jnp.maximum(jnp.max(s, axis=-1, keepdims=True), sink)
        p = jnp.exp(s - m)
        probs[j, g] = jnp.concatenate([jnp.where(from_prev, p, 0.0), jnp.where(from_prev, 0.0, p)],
                                      axis=1).astype(BF16)
        sink_term[j, g] = jnp.exp(sink - m)

    for j, g in tiles:
        rows = slice(j * ATTN_BLOCK, (j + 1) * ATTN_BLOCK)
        band = slice(j * ATTN_BLOCK, (j + 2) * ATTN_BLOCK)
        o8 = _dot(probs[j, g], jnp.concatenate([v_all[band, g * LANES:(g + 1) * LANES], ones_v], axis=1))
        o4 = o8[:, :LANES] / (o8[:, LANES:] + sink_term[j, g])
        b = ATTN_BLOCK
        o_ref[rows, (2 * g) * LANES:(2 * g + 1) * LANES] = jnp.where(lo, o4[0:b], o4[b:2 * b]).astype(BF16)
        o_ref[rows, (2 * g + 1) * LANES:(2 * g + 2) * LANES] = jnp.where(
            lo, o4[2 * b:3 * b], o4[3 * b:4 * b]).astype(BF16)


def _attention(a_in, cos_t, sin_t, sinks_l, qw, kw, bdq, bdk, batch, seq):
    t = a_in.shape[0]
    tq = min(TILE_ATTN, seq)
    nj = tq // ATTN_BLOCK
    tpb = seq // tq
    bpb = seq // ATTN_BLOCK
    qwid = N_HEADS * HEAD_DIM
    kvw = 2 * N_KV * HEAD_DIM
    cur = lambda b, i: (b * tpb + i, 0)
    prv = lambda b, i: (b * bpb + jnp.maximum(i * nj - 1, 0), qwid // kvw)
    prv0 = lambda b, i: (b * bpb + jnp.maximum(i * nj - 1, 0), 0)
    const = lambda b, i: (0, 0)
    return pl.pallas_call(
        _attn_kernel,
        grid=(batch, tpb),
        in_specs=[
            pl.BlockSpec(memory_space=pltpu.SMEM),
            pl.BlockSpec((tq, qwid + kvw), cur),
            pl.BlockSpec((ATTN_BLOCK, kvw), prv),
            pl.BlockSpec((tq, LANES), cur),
            pl.BlockSpec((tq, LANES), cur),
            pl.BlockSpec((ATTN_BLOCK, LANES), prv0),
            pl.BlockSpec((ATTN_BLOCK, LANES), prv0),
            pl.BlockSpec((1, qwid), const),
            pl.BlockSpec((1, kvw // 2), const),
            pl.BlockSpec((qwid, qwid), const),
            pl.BlockSpec((kvw // 2, kvw // 2), const),
        ],
        out_specs=pl.BlockSpec((tq, qwid), cur),
        out_shape=jax.ShapeDtypeStruct((t, qwid), BF16),
        compiler_params=_params("arbitrary", "arbitrary"),
        name="swa_attention",
    )(sinks_l, a_in, a_in, cos_t, sin_t, cos_t, sin_t, qw, kw, bdq, bdk)


def _mlstm_kernel(min_ref, gr_ref, cw_ref, cb_ref, bcol_ref, nw_ref,
                  hm_ref, ext_ref, q_ref, kt_ref, st_ref, mx_ref, ab_ref, bc_ref):
    tt = min_ref.shape[0]
    mw = M_HEADS * M_DIM
    nchunks = tt // CHUNK

    @pl.when(pl.program_id(1) == 0)
    def _():
        ext_ref[0:SUBLANES, :] = jnp.zeros((SUBLANES, 2 * mw), F32)
        st_ref[...] = jnp.zeros(st_ref.shape, F32)
        mx_ref[...] = jnp.zeros(mx_ref.shape, F32)

    def conv_block(cols):
        u = min_ref[:, cols].astype(F32)
        ext_ref[SUBLANES:SUBLANES + tt, cols] = u
        acc = cb_ref[:, cols] + cw_ref[CONV_K - 1:CONV_K, cols] * u
        for jj in range(CONV_K - 1):
            off = SUBLANES - (CONV_K - 1) + jj
            acc = acc + cw_ref[jj:jj + 1, cols] * ext_ref[off:off + tt, cols]
        ext_ref[0:SUBLANES, cols] = u[tt - SUBLANES:tt, :]
        return acc * _sigmoid(acc)

    def q_body(h, carry):
        cols = pl.ds(pl.multiple_of(h * M_DIM, M_DIM), M_DIM)
        q_ref[:, cols] = conv_block(cols).astype(BF16)
        return carry

    def k_body(h, carry):
        off = pl.multiple_of(h * M_DIM, M_DIM)
        act = conv_block(pl.ds(mw + off, M_DIM)) * (M_DIM ** -0.5)
        for j in range(nchunks):
            kt_ref[pl.ds(off, M_DIM), j * CHUNK:(j + 1) * CHUNK] = act[j * CHUNK:(j + 1) * CHUNK, :].T
        return carry

    lax.fori_loop(0, M_HEADS, q_body, 0)
    lax.fori_loop(0, M_HEADS, k_body, 0)

    ri = lax.broadcasted_iota(I32, (CHUNK, CHUNK), 0)
    ci = lax.broadcasted_iota(I32, (CHUNK, CHUNK), 1)
    causal = ci <= ri
    triu = jnp.where(ri <= ci, 1.0, 0.0).astype(BF16)
    ones_half = jnp.ones((CHUNK, M_DIM), BF16)
    mean_mat = jnp.full((M_DIM, M_DIM), 1.0 / M_DIM, BF16)
    sub = lax.broadcasted_iota(I32, (SUBLANES, CHUNK), 0)
    heads = range(M_HEADS)

    pad_rows = jnp.zeros((CHUNK - SUBLANES, CHUNK), F32)
    zero_rows = jnp.zeros((SUBLANES, CHUNK), F32)

    def gate_body(jg, carry):
        for u_ in range(GATE_CHUNKS_PER_STEP):
            rs = pl.ds(pl.multiple_of((jg * GATE_CHUNKS_PER_STEP + u_) * CHUNK, CHUNK), CHUNK)
            gr = gr_ref[:, rs] + bcol_ref[...]
            ls = _log_sigmoid(gr)
            ls1 = ls.astype(BF16).astype(F32)
            ls2 = (ls - ls1).astype(BF16).astype(F32)
            pieces = jnp.concatenate([ls1, ls2, ls - ls1 - ls2, zero_rows], axis=0).astype(BF16)
            sums = _dot(pieces, triu)
            br = sums[0:SUBLANES] + sums[SUBLANES:2 * SUBLANES] + sums[2 * SUBLANES:3 * SUBLANES]
            ab = jnp.where(sub < M_HEADS, gr - pltpu.roll(br, M_HEADS, axis=0), br)
            ab_ref[:, rs] = ab
            bc_ref[rs, :] = jnp.concatenate([ab, pad_rows], axis=0).T
        return carry

    lax.fori_loop(0, nchunks // GATE_CHUNKS_PER_STEP, gate_body, 0)

    def group_body(cg, carry):
        rows, ab = [], []
        for u_ in range(CHUNKS_PER_STEP):
            r0 = pl.multiple_of((cg * CHUNKS_PER_STEP + u_) * CHUNK, CHUNK)
            rows.append(pl.ds(r0, CHUNK))
            ab.append(ab_ref[:, rows[u_]])
        lanes = [(u_, h) for u_ in range(CHUNKS_PER_STEP) for h in heads]
        a_r = {(u_, h): ab[u_][h:h + 1, :] for u_, h in lanes}
        b_last = {(u_, h): ab[u_][M_HEADS + h:M_HEADS + h + 1, CHUNK - 1:CHUNK] for u_, h in lanes}

        m_prev, a_max, a_dec, s_in = {}, {}, {}, {}
        m_run = [mx_ref[h][0:1, 0:1] for h in heads]
        for k in lanes:
            u_, h = k
            m_prev[k] = m_run[h]
            a_max[k] = jnp.max(a_r[k], axis=-1, keepdims=True)
            m_loc = b_last[k] + a_max[k]
            m_new = jnp.maximum(b_last[k] + m_prev[k], m_loc)
            a_dec[k] = jnp.exp(b_last[k] + m_prev[k] - m_new)
            s_in[k] = jnp.exp(m_loc - m_new)
            m_run[h] = m_new
        for h in heads:
            mx_ref[h] = jnp.broadcast_to(m_run[h], (SUBLANES, LANES))

        q, v_ext, s_qk, kv = {}, {}, {}, {}
        for k in lanes:
            u_, h = k
            rs = rows[u_]
            q[k] = q_ref[rs, h * M_DIM:(h + 1) * M_DIM]
            kt = kt_ref[h * M_DIM:(h + 1) * M_DIM, rs]
            v = min_ref[rs, 2 * mw + h * M_DIM:2 * mw + (h + 1) * M_DIM]
            v_ext[k] = jnp.concatenate([v, ones_half], axis=1)
            s_qk[k] = _dot(q[k], kt.astype(BF16))
            e_r = jnp.exp(a_r[k] - a_max[k])
            kv[k] = _dot((kt * e_r).astype(BF16), v_ext[k])

        thr, qk, inter = {}, {}, {}
        for k in lanes:
            u_, h = k
            a_mat = jnp.where(causal, a_r[k], -jnp.inf)
            mu = jnp.maximum(jnp.max(a_mat, axis=-1, keepdims=True), m_prev[k])
            b_c = bc_ref[rows[u_], M_HEADS + h:M_HEADS + h + 1]
            thr[k] = jnp.broadcast_to(jnp.exp(-(b_c + mu)), (CHUNK, M_DIM))
            mu_b = jnp.broadcast_to(mu, (CHUNK, CHUNK))
            inter[k] = jnp.exp(m_prev[k] - mu_b)
            qk[k] = (s_qk[k] * jnp.exp(a_mat - mu_b)).astype(BF16)

        q_state = {}
        state = [st_ref[h] for h in heads]
        for k in lanes:
            u_, h = k
            q_state[k] = _dot(q[k], state[h].astype(BF16))
            state[h] = a_dec[k] * state[h] + s_in[k] * kv[k]
        for h in heads:
            st_ref[h] = state[h]

        for k in lanes:
            u_, h = k
            hs = slice(h * M_DIM, (h + 1) * M_DIM)
            num = _dot(qk[k], v_ext[k])
            den = jnp.maximum(jnp.abs(num[:, M_DIM:] + inter[k] * q_state[k][:, M_DIM:]), thr[k])
            hh = (num[:, :M_DIM] + inter[k] * q_state[k][:, :M_DIM]) / den
            msq = _dot((hh * hh).astype(BF16), mean_mat)
            hn = hh * lax.rsqrt(msq + EPS) * nw_ref[:, hs]
            og = min_ref[rows[u_], 3 * mw + h * M_DIM:3 * mw + (h + 1) * M_DIM].astype(F32)
            hm_ref[rows[u_], hs] = (_sigmoid(og) * hn).astype(BF16)
        return carry

    lax.fori_loop(0, nchunks // CHUNKS_PER_STEP, group_body, 0)


def _mlstm(m_in, grow, conv_w, conv_b, bcol, nw, batch, seq):
    t = m_in.shape[0]
    tt = min(TILE_MLSTM, seq)
    tpb = seq // tt
    mw = M_HEADS * M_DIM
    cur = lambda b, i: (b * tpb + i, 0)
    const = lambda b, i: (0, 0)
    return pl.pallas_call(
        _mlstm_kernel,
        grid=(batch, tpb),
        in_specs=[
            pl.BlockSpec((tt, 4 * mw), cur),
            pl.BlockSpec((SUBLANES, tt), lambda b, i: (0, b * tpb + i)),
            pl.BlockSpec((CONV_K, 2 * mw), const),
            pl.BlockSpec((1, 2 * mw), const),
            pl.BlockSpec((SUBLANES, LANES), const),
            pl.BlockSpec((1, mw), const),
        ],
        out_specs=pl.BlockSpec((tt, mw), cur),
        out_shape=jax.ShapeDtypeStruct((t, mw), BF16),
        scratch_shapes=[
            pltpu.VMEM((tt + SUBLANES, 2 * mw), F32),
            pltpu.VMEM((tt, mw), BF16),
            pltpu.VMEM((mw, tt), F32),
            pltpu.VMEM((M_HEADS, M_DIM, 2 * M_DIM), F32),
            pltpu.VMEM((M_HEADS, SUBLANES, LANES), F32),
            pltpu.VMEM((SUBLANES, tt), F32),
            pltpu.VMEM((tt, LANES), F32),
        ],
        compiler_params=_params("arbitrary", "arbitrary"),
        name="mlstm",
    )(m_in, grow, conv_w, conv_b, bcol, nw)


def _merge_kernel(o_ref, hm_ref, mg_ref, x_ref, g1_ref, sc_ref, sh_ref, nw_ref,
                  wa_ref, wm_ref, wo_ref, rwh_ref, rwl_ref, rb_ref, tri_ref,
                  x1_ref, pay_ref, route_ref, cnt_ref, carry_ref):
    tm, d = x_ref.shape

    @pl.when(pl.program_id(0) == 0)
    def _():
        carry_ref[...] = jnp.zeros(carry_ref.shape, F32)

    ya = _dot(o_ref[...], wa_ref[...])
    yb = _dot(hm_ref[...], wm_ref[...])
    mg = mg_ref[...]
    merged = mg[:, :d].astype(F32) * ya + mg[:, d:].astype(F32) * yb
    x1 = x_ref[...] + g1_ref[...] * _dot(merged.astype(BF16), wo_ref[...])
    x1_ref[...] = x1

    ms = jnp.mean(x1 * x1, axis=-1, keepdims=True)
    h2 = x1 * lax.rsqrt(ms + EPS) * (nw_ref[...] * (1.0 + sc_ref[...])) + sh_ref[...]
    hi = h2.astype(BF16)
    hif = hi.astype(F32)
    lo = (h2 - hif).astype(BF16)
    r_hi = _dot_nt(rwh_ref[...], hi)
    r_lo = _dot_nt(rwl_ref[...], lo)
    sc_t = _sigmoid(r_hi[:N_EXPERTS] + r_hi[N_EXPERTS:] + r_lo)
    sel_t = sc_t + rb_ref[:, 0:1]

    def row(a, e):
        return a[e:e + 1, :]

    best = None
    gi = jnp.zeros((1, tm), I32)
    for g in range(N_GROUPS):
        r = [row(sel_t, EPG * g + i) for i in range(EPG)]
        gs = None
        for i in range(EPG):
            for j in range(i + 1, EPG):
                pr = r[i] + r[j]
                gs = pr if gs is None else jnp.maximum(gs, pr)
        if best is None:
            best = gs
        else:
            upd = gs > best
            gi = jnp.where(upd, g, gi)
            best = jnp.maximum(best, gs)

    def pick(a, i):
        out = row(a, i)
        for g in range(1, N_GROUPS):
            out = jnp.where(gi == g, row(a, EPG * g + i), out)
        return out

    v = [pick(sel_t, i) for i in range(EPG)]
    s = [pick(sc_t, i) for i in range(EPG)]

    def argmax4(vals):
        bv, bi = vals[0], jnp.zeros((1, tm), I32)
        for i in range(1, EPG):
            upd = vals[i] > bv
            bi = jnp.where(upd, i, bi)
            bv = jnp.maximum(bv, vals[i])
        return bi

    i1 = argmax4(v)
    i2 = argmax4([jnp.where(i1 == i, -jnp.inf, v[i]) for i in range(EPG)])
    ia = jnp.minimum(i1, i2)
    ib = jnp.maximum(i1, i2)
    pidx = jnp.where(ia == 0, ib - 1, jnp.where(ia == 1, jnp.where(ib == 3, 3, 5), 4))
    bucket = gi * N_PAIRS + pidx

    def by_index(vals, idx):
        out = vals[0]
        for i in range(1, EPG):
            out = jnp.where(idx == i, vals[i], out)
        return out

    swap = pidx == N_PAIRS - 1
    s_lo, s_hi = by_index(s, ia), by_index(s, ib)
    s_a, s_b = jnp.where(swap, s_hi, s_lo), jnp.where(swap, s_lo, s_hi)
    gate_a = s_a / (s_a + s_b)
    gate_b = s_b / (s_a + s_b)

    brow = lax.broadcasted_iota(I32, (BUCKET_ROWS, tm), 0)
    onehot = brow == bucket
    cums = _dot(jnp.where(onehot, 1.0, 0.0).astype(BF16), tri_ref[...])
    carry = carry_ref[...]
    rank = jnp.sum(jnp.where(onehot, carry[:, 0:1] + cums, 0.0), axis=0, keepdims=True) - 1.0
    new_carry = carry + cums[:, tm - 1:tm]
    carry_ref[...] = new_carry
    cnt_ref[...] = new_carry

    route_ref[...] = jnp.concatenate(
        [bucket.astype(F32), gate_a, gate_b, rank, jnp.zeros((SUBLANES - 4, tm), F32)], axis=0)

    half = d // 2
    packed = _pack_bf16_pairs(hif, rounded=True)
    for cpart in range(half // LANES):
        pay_ref[cpart] = packed[:, cpart * LANES:(cpart + 1) * LANES]
    gates_t = jnp.concatenate([gate_a, gate_b, jnp.zeros((LANES - 2, tm), F32)], axis=0)
    pay_ref[half // LANES] = lax.bitcast_convert_type(gates_t.T, U32)


def _merge(o_attn, hm, mg, x, g1, sc2, sh2, nw, wa, wm, wo, rwh, rwl, rb, tri, l, seq):
    t, d = x.shape
    tm = tri.shape[0]
    tpb = seq // tm
    row = lambda i: (i, 0)
    bsel = lambda i: (i // tpb, 0, 0)
    wsel = lambda i: (l, 0, 0)
    const = lambda i: (0, 0)
    hw = o_attn.shape[1]
    return pl.pallas_call(
        _merge_kernel,
        grid=(t // tm,),
        in_specs=[
            pl.BlockSpec((tm, hw), row),
            pl.BlockSpec((tm, hw), row),
            pl.BlockSpec((tm, 2 * d), row),
            pl.BlockSpec((tm, d), row),
            pl.BlockSpec((None, 1, d), bsel),
            pl.BlockSpec((None, 1, d), bsel),
            pl.BlockSpec((None, 1, d), bsel),
            pl.BlockSpec((None, 1, d), wsel),
            pl.BlockSpec((None, hw, d), wsel),
            pl.BlockSpec((None, hw, d), wsel),
            pl.BlockSpec((None, d, d), wsel),
            pl.BlockSpec((2 * N_EXPERTS, d), const),
            pl.BlockSpec((N_EXPERTS, d), const),
            pl.BlockSpec((N_EXPERTS, LANES), const),
            pl.BlockSpec((tm, tm), const),
        ],
        out_specs=[
            pl.BlockSpec((tm, d), row),
            pl.BlockSpec((PAY_PARTS, tm, LANES), lambda i: (0, i, 0)),
            pl.BlockSpec((SUBLANES, tm), lambda i: (0, i)),
            pl.BlockSpec((BUCKET_ROWS, LANES), const),
        ],
        out_shape=[
            jax.ShapeDtypeStruct((t, d), F32),
            jax.ShapeDtypeStruct((PAY_PARTS, t, LANES), U32),
            jax.ShapeDtypeStruct((SUBLANES, t), F32),
            jax.ShapeDtypeStruct((BUCKET_ROWS, LANES), F32),
        ],
        scratch_shapes=[pltpu.VMEM((BUCKET_ROWS, LANES), F32)],
        compiler_params=_params("arbitrary"),
        name="merge_router",
    )(o_attn, hm, mg, x, g1, sc2, sh2, nw, wa, wm, wo, rwh, rwl, rb, tri)


def _sc_mesh():
    return plsc.VectorSubcoreMesh(core_axis_name="core", subcore_axis_name="subcore")


def _sc_scatter_rows(rows, dest, n_out):
    n, w = rows.shape

    @pl.kernel(out_type=jax.ShapeDtypeStruct((n_out, w), rows.dtype), mesh=_sc_mesh(), scratch_types=[])
    def scatter(x_hbm, i_hbm, o_hbm):
        def body(x_vmem, i_vmem):
            pltpu.sync_copy(x_vmem, o_hbm.at[i_vmem.at[0]])

        pltpu.emit_pipeline(
            body,
            grid=(n // SC_WINDOW,),
            in_specs=[pl.BlockSpec((SC_WINDOW, w), lambda i: (i, 0)),
                      pl.BlockSpec((1, SC_WINDOW), lambda i: (0, i))],
            out_specs=[],
            core_axis_name=("core", "subcore"),
            dimension_semantics=(pltpu.PARALLEL,),
        )(x_hbm, i_hbm)

    return scatter(rows, dest.reshape(1, n))


def _sc_gather_rows(src, idx):
    n = idx.shape[0]
    w = src.shape[1]

    @pl.kernel(out_type=jax.ShapeDtypeStruct((n, w), src.dtype), mesh=_sc_mesh(), scratch_types=[])
    def gather(x_hbm, i_hbm, o_hbm):
        def body(i_vmem, o_vmem):
            pltpu.sync_copy(x_hbm.at[i_vmem.at[0]], o_vmem)

        pltpu.emit_pipeline(
            body,
            grid=(n // SC_WINDOW,),
            in_specs=[pl.BlockSpec((1, SC_WINDOW), lambda i: (0, i))],
            out_specs=[pl.BlockSpec((SC_WINDOW, w), lambda i: (i, 0))],
            core_axis_name=("core", "subcore"),
            dimension_semantics=(pltpu.PARALLEL,),
        )(i_hbm, o_hbm)

    return gather(src, idx.reshape(1, n))


def _row_index_kernel(ps_ref, route_ref, o_ref, *, n_rows):
    bucket = route_ref[0:1, :].astype(I32)
    start = jnp.zeros(bucket.shape, I32)
    for b in range(N_BUCKETS):
        start = jnp.where(bucket == b, ps_ref[b], start)
    dest = start + route_ref[3:4, :].astype(I32)
    part = lax.broadcasted_iota(I32, o_ref.shape, 0)
    o_ref[...] = part * n_rows + dest


def _row_index(pad_starts, route, n_rows):
    t = route.shape[1]
    tm = min(TILE_ROW_INDEX, t)
    return pl.pallas_call(
        functools.partial(_row_index_kernel, n_rows=n_rows),
        grid=(t // tm,),
        in_specs=[pl.BlockSpec(memory_space=pltpu.SMEM), pl.BlockSpec((SUBLANES, tm), lambda i: (0, i))],
        out_specs=pl.BlockSpec((SUBLANES, tm), lambda i: (0, i)),
        out_shape=jax.ShapeDtypeStruct((SUBLANES, t), I32),
        compiler_params=_params("arbitrary"),
        name="row_index",
    )(pad_starts, route)


def _residual_kernel(x_ref, y_ref, g2_ref, o_ref):
    y = _unpack_bf16_pairs(jnp.concatenate([y_ref[c] for c in range(OUT_PARTS)], axis=1))
    o_ref[...] = x_ref[...] + g2_ref[...] * y


def _residual(x1, ytok, g2, seq):
    t, d = x1.shape
    tm = min(TILE_RESIDUAL, seq)
    tpb = seq // tm
    return pl.pallas_call(
        _residual_kernel,
        grid=(t // tm,),
        in_specs=[
            pl.BlockSpec((tm, d), lambda i: (i, 0)),
            pl.BlockSpec((OUT_PARTS, tm, LANES), lambda i: (0, i, 0)),
            pl.BlockSpec((None, 1, d), lambda i: (i // tpb, 0, 0)),
        ],
        out_specs=pl.BlockSpec((tm, d), lambda i: (i, 0)),
        out_shape=jax.ShapeDtypeStruct((t, d), F32),
        compiler_params=_params("arbitrary"),
        name="residual",
    )(x1, ytok, g2)


def _expert_kernel(ea_ref, eb_ref, nr_ref, xs_ref, wga_ref, wua_ref, wda_ref, wgb_ref, wub_ref, wdb_ref, ys_ref,
                   ga_ref, ua_ref, da_ref, gb_ref, ub_ref, db_ref):
    j = pl.program_id(0)
    nr = nr_ref[0]
    prev = jnp.maximum(j - 1, 0)

    @pl.when((j == 0) | (ea_ref[j] != ea_ref[prev]))
    def _():
        ga_ref[...] = wga_ref[...].astype(BF16)
        ua_ref[...] = wua_ref[...].astype(BF16)
        da_ref[...] = wda_ref[...].astype(BF16)

    @pl.when((j == 0) | (eb_ref[j] != eb_ref[prev]))
    def _():
        gb_ref[...] = wgb_ref[...].astype(BF16)
        ub_ref[...] = wub_ref[...].astype(BF16)
        db_ref[...] = wdb_ref[...].astype(BF16)

    @pl.when(j < nr)
    def _():
        x = _unpack_bf16_pairs(jnp.concatenate([xs_ref[c] for c in range(PAY_PARTS - 1)], axis=1)).astype(BF16)
        gl = lax.bitcast_convert_type(xs_ref[PAY_PARTS - 1], F32)

        def ffn(wg_ref, wu_ref, wd_ref):
            gte = _dot(x, wg_ref[...])
            act = gte * _sigmoid(gte) * _dot(x, wu_ref[...])
            return _dot(act.astype(BF16), wd_ref[...])

        y = _pack_bf16_pairs(gl[:, 0:1] * ffn(ga_ref, ua_ref, da_ref) + gl[:, 1:2] * ffn(gb_ref, ub_ref, db_ref))
        for c in range(OUT_PARTS):
            ys_ref[c] = y[:, c * LANES:(c + 1) * LANES]

    @pl.when(j >= nr)
    def _():
        ys_ref[...] = jnp.zeros(ys_ref.shape, U32)


def _experts(blk_ea, blk_eb, n_real, xs, wg, wu, wd, d):
    n_rows = xs.shape[1]
    nblk = n_rows // EXPERT_BLOCK
    f = wg.shape[2]
    grid_spec = pltpu.PrefetchScalarGridSpec(
        num_scalar_prefetch=3,
        grid=(nblk,),
        in_specs=[
            pl.BlockSpec((PAY_PARTS, EXPERT_BLOCK, LANES), lambda j, ea, eb, nr: (0, j, 0)),
            pl.BlockSpec((None, d, f), lambda j, ea, eb, nr: (ea[j], 0, 0)),
            pl.BlockSpec((None, d, f), lambda j, ea, eb, nr: (ea[j], 0, 0)),
            pl.BlockSpec((None, f, d), lambda j, ea, eb, nr: (ea[j], 0, 0)),
            pl.BlockSpec((None, d, f), lambda j, ea, eb, nr: (eb[j], 0, 0)),
            pl.BlockSpec((None, d, f), lambda j, ea, eb, nr: (eb[j], 0, 0)),
            pl.BlockSpec((None, f, d), lambda j, ea, eb, nr: (eb[j], 0, 0)),
        ],
        out_specs=pl.BlockSpec((OUT_PARTS, EXPERT_BLOCK, LANES), lambda j, ea, eb, nr: (0, j, 0)),
        scratch_shapes=[pltpu.VMEM((d, f), BF16), pltpu.VMEM((d, f), BF16), pltpu.VMEM((f, d), BF16)] * 2,
    )
    return pl.pallas_call(
        _expert_kernel,
        grid_spec=grid_spec,
        out_shape=jax.ShapeDtypeStruct((OUT_PARTS, n_rows, LANES), U32),
        compiler_params=_params("arbitrary"),
        name="experts",
    )(blk_ea, blk_eb, n_real, xs, wg, wu, wd, wg, wu, wd)


_PAIR_A = (0, 0, 0, 1, 2, 2)
_PAIR_B = (1, 2, 3, 3, 3, 1)


def kernel(x, c, positions, ada_w, ada_b, norm_mix_w, norm_ffn_w, w_in, b_igate, b_fgate, q_norm_w, k_norm_w,
           sinks, conv_w, conv_b, mlstm_norm_w, w_attn_up, w_mlstm_up, w_out, router_w, router_bias,
           w_gate, w_up, w_down):
    batch, seq, d = x.shape
    depth = w_in.shape[0]
    t = batch * seq
    qw = N_HEADS * HEAD_DIM
    kvw = N_KV * HEAD_DIM
    mw = M_HEADS * M_DIM

    o = 0
    cols = {}
    for name, wdt in (("q", qw), ("k", kvw), ("v", kvw), ("mqk", 2 * mw), ("mv", mw), ("mi", M_HEADS),
                      ("mf", M_HEADS), ("mo", mw), ("ga", d), ("gb", d)):
        cols[name] = (o, o + wdt)
        o += wdt

    def wc(name, lo=0, hi=None):
        s, e = cols[name]
        return w_in[:, :, s + lo:(s + hi if hi is not None else e)]

    w_a = jnp.concatenate([wc("q"), wc("k"), wc("v")], axis=2).astype(BF16)
    w_m = jnp.concatenate([wc("mqk"), wc("mv"), wc("mo")], axis=2).astype(BF16)
    w_g = jnp.concatenate([wc("mi"), wc("mf"), jnp.zeros((depth, d, LANES - 2 * M_HEADS), F32)], axis=2).astype(BF16)
    w_mg = jnp.concatenate([wc("ga"), wc("gb")], axis=2).astype(BF16)
    w_au = w_attn_up.astype(BF16)
    w_mu = w_mlstm_up.astype(BF16)
    w_o = w_out.astype(BF16)
    n_e = w_gate.shape[1]
    w_g8 = w_gate.reshape(depth * n_e, d, -1)
    w_u8 = w_up.reshape(depth * n_e, d, -1)
    w_d = w_down.reshape(depth * n_e, -1, d)

    rw_t = router_w.astype(F32).T
    rw_top = rw_t.astype(BF16)
    rw_hi = jnp.concatenate([rw_top, (rw_t - rw_top.astype(F32)).astype(BF16)], axis=0)
    rw_lo = rw_top
    rb = jnp.broadcast_to(router_bias.astype(F32)[:, None], (n_e, LANES))

    qn_w = jnp.tile(q_norm_w * (HEAD_DIM ** -0.5), (1, N_HEADS)).reshape(depth, 1, qw)
    kn_w = jnp.tile(k_norm_w, (1, N_KV)).reshape(depth, 1, kvw)
    seg = jnp.arange(qw) // HEAD_DIM
    bdq = jnp.where(seg[:, None] == seg[None, :], 1.0 / HEAD_DIM, 0.0).astype(BF16)
    bdk = bdq[:kvw, :kvw]

    inv_freq = ROPE_THETA ** (-(jnp.arange(0, ROPE_DIM, 2, dtype=F32) / ROPE_DIM))
    ang = positions.astype(F32).reshape(1, t) * inv_freq[:, None]
    cos8, sin8 = jnp.cos(ang).T, jnp.sin(ang).T
    pad1 = jnp.ones((t, HEAD_DIM - ROPE_DIM), F32)
    pad0 = jnp.zeros((t, HEAD_DIM - ROPE_DIM), F32)
    cos_t = jnp.tile(jnp.concatenate([cos8, cos8, pad1], axis=1), (1, LANES // HEAD_DIM))
    sin_t = jnp.tile(jnp.concatenate([-sin8, sin8, pad0], axis=1), (1, LANES // HEAD_DIM))

    gate_bias = jnp.concatenate([b_igate, b_fgate], axis=1).astype(F32)
    bcol = jnp.broadcast_to(gate_bias[:, :, None], (depth, 2 * M_HEADS, LANES))

    tm_merge = min(TILE_MERGE, seq)
    ii = jnp.arange(tm_merge)
    tri = (ii[:, None] <= ii[None, :]).astype(BF16)

    n_blk = (t + N_BUCKETS * (EXPERT_BLOCK - 1)) // EXPERT_BLOCK + 1
    n_rows = n_blk * EXPERT_BLOCK
    pair_a = jnp.asarray(_PAIR_A, I32)
    pair_b = jnp.asarray(_PAIR_B, I32)

    c_pad = jnp.zeros((SUBLANES, d), F32).at[:batch].set(c)
    mod = _ada_mod(c_pad, ada_w, ada_b)[:, :batch]

    xf = x.reshape(t, d)
    moe = None
    for l in range(depth):
        sh1, sc1, g1, sh2, sc2, g2 = [m.reshape(batch, 1, d) for m in jnp.split(mod[l], 6, axis=-1)]

        outs = _inproj(xf, moe, sc1, sh1, norm_mix_w.reshape(depth, 1, d), w_a, w_m, w_g, w_mg, l, seq)
        a_in, m_in, mg, grow = outs[:4]
        if moe is not None:
            xf = outs[4]
        o_attn = _attention(a_in, cos_t, sin_t, sinks[l], qn_w[l], kn_w[l], bdq, bdk, batch, seq)
        hm = _mlstm(m_in, grow, conv_w[l], conv_b[l].reshape(1, -1), bcol[l], mlstm_norm_w[l].reshape(1, mw),
                    batch, seq)
        x1, pay, route, cnt = _merge(o_attn, hm, mg, xf, g1, sc2, sh2, norm_ffn_w.reshape(depth, 1, d),
                                     w_au, w_mu, w_o, rw_hi, rw_lo, rb, tri, l, seq)

        counts = cnt[:N_BUCKETS, 0].astype(I32)
        padded = (counts + EXPERT_BLOCK - 1) // EXPERT_BLOCK * EXPERT_BLOCK
        pad_ends = jnp.cumsum(padded)
        pad_starts = pad_ends - padded
        row_idx = _row_index(jnp.concatenate([pad_starts, jnp.zeros((BUCKET_ROWS - N_BUCKETS,), I32)]), route, n_rows)
        blk_start = jnp.arange(n_blk, dtype=I32) * EXPERT_BLOCK
        blk_bucket = jnp.minimum(jnp.sum((pad_ends[None, :] <= blk_start[:, None]).astype(I32), axis=1), N_BUCKETS - 1)
        grp = blk_bucket // N_PAIRS
        blk_ea = (l * n_e + grp * EPG + pair_a[blk_bucket % N_PAIRS]).astype(I32)
        blk_eb = (l * n_e + grp * EPG + pair_b[blk_bucket % N_PAIRS]).astype(I32)
        n_real = (pad_ends[-1:] // EXPERT_BLOCK).astype(I32)

        xs = _sc_scatter_rows(pay.reshape(PAY_PARTS * t, LANES), row_idx[:PAY_PARTS].reshape(-1),
                              PAY_PARTS * n_rows).reshape(PAY_PARTS, n_rows, LANES)
        ys = _experts(blk_ea, blk_eb, n_real, xs, w_g8, w_u8, w_d, d)
        ytok = _sc_gather_rows(ys.reshape(OUT_PARTS * n_rows, LANES),
                               row_idx[:OUT_PARTS].reshape(-1)).reshape(OUT_PARTS, t, LANES)
        xf, moe = x1, (ytok, g2)
    return _residual(xf, moe[0], moe[1], seq).reshape(batch, seq, d)
```

```python
import functools

import jax
import jax.numpy as jnp
from jax import lax
from jax.experimental import pallas as pl
from jax.experimental.pallas import tpu as pltpu
from jax.experimental.pallas import tpu_sc as plsc

F32 = jnp.float32
BF16 = jnp.bfloat16
U32 = jnp.uint32
I32 = jnp.int32
HIGHEST = lax.Precision.HIGHEST

HEAD_DIM = 64
N_HEADS = 8
N_KV = 2
ROPE_DIM = 16
ROPE_THETA = 500000.0
ATTN_BLOCK = 128
M_HEADS = 4
M_DIM = 128
CONV_K = 4
N_EXPERTS = 16
N_GROUPS = 4
EPG = 4
EPS = 1e-6

LANES = 128
SUBLANES = 8

TILE_INPROJ = 1024
TILE_ATTN = 1024
TILE_MLSTM = 2048
TILE_MERGE = 1024
TILE_RESIDUAL = 1024
TILE_ROW_INDEX = 8192
ADA_COLS = 3072
CHUNK = 128
CHUNKS_PER_STEP = 4
GATE_CHUNKS_PER_STEP = 4
N_PAIRS = 6
N_BUCKETS = N_GROUPS * N_PAIRS
BUCKET_ROWS = 32
EXPERT_BLOCK = 256
PAY_PARTS = 5
OUT_PARTS = 4
SC_WINDOW = 128
VMEM_LIMIT = 56 * 1024 * 1024


def _dot(a, b, precision=None):
    return jnp.dot(a, b, preferred_element_type=F32, precision=precision)


def _dot_nt(a, b):
    return lax.dot_general(a, b, (((1,), (1,)), ((), ())), preferred_element_type=F32)


def _sigmoid(x):
    return 1.0 / (1.0 + jnp.exp(-x))


def _log_sigmoid(x):
    return jnp.minimum(x, 0.0) - jnp.log1p(jnp.exp(-jnp.abs(x)))


def _pack_bf16_pairs(v, rounded=False):
    n = v.shape[1] // 2
    bits = lax.bitcast_convert_type(v if rounded else v.astype(BF16).astype(F32), U32)
    return bits[:, :n] | (bits[:, n:] >> 16)


def _unpack_bf16_pairs(w):
    hi = lax.bitcast_convert_type(w & jnp.uint32(0xFFFF0000), F32)
    lo = lax.bitcast_convert_type(w << 16, F32)
    return jnp.concatenate([hi, lo], axis=1)


def _params(*sem):
    return pltpu.CompilerParams(dimension_semantics=sem, vmem_limit_bytes=VMEM_LIMIT)


def _ada_kernel(c_ref, w_ref, b_ref, o_ref):
    c = c_ref[...]
    ca = c * _sigmoid(c)
    o_ref[0] = _dot(ca, w_ref[0], HIGHEST) + b_ref[0]


def _ada_mod(c_pad, ada_w, ada_b):
    depth, d, n = ada_w.shape
    tn = ADA_COLS
    return pl.pallas_call(
        _ada_kernel,
        grid=(depth, n // tn),
        in_specs=[
            pl.BlockSpec((SUBLANES, d), lambda l, j: (0, 0)),
            pl.BlockSpec((1, d, tn), lambda l, j: (l, 0, j)),
            pl.BlockSpec((1, 1, tn), lambda l, j: (l, 0, j)),
        ],
        out_specs=pl.BlockSpec((1, SUBLANES, tn), lambda l, j: (l, 0, j)),
        out_shape=jax.ShapeDtypeStruct((depth, SUBLANES, n), F32),
        compiler_params=_params("arbitrary", "arbitrary"),
        name="ada_mod",
    )(c_pad, ada_w, ada_b.reshape(depth, 1, n))


def _inproj_kernel(*refs, fuse_residual):
    if fuse_residual:
        (x_ref, y_ref, g2_ref, sc_ref, sh_ref, nw_ref, wa_ref, wm_ref, wg_ref, wmg_ref,
         a_ref, m_ref, mg_ref, gr_ref, xo_ref, g_ref) = refs
        y = _unpack_bf16_pairs(jnp.concatenate([y_ref[c] for c in range(OUT_PARTS)], axis=1))
        x = x_ref[...] + g2_ref[...] * y
        xo_ref[...] = x
    else:
        (x_ref, sc_ref, sh_ref, nw_ref, wa_ref, wm_ref, wg_ref, wmg_ref,
         a_ref, m_ref, mg_ref, gr_ref, g_ref) = refs
        x = x_ref[...]
    ms = jnp.mean(x * x, axis=-1, keepdims=True)
    h = x * lax.rsqrt(ms + EPS) * (nw_ref[...] * (1.0 + sc_ref[...])) + sh_ref[...]
    hb = h.astype(BF16)
    a_ref[...] = _dot(hb, wa_ref[...]).astype(BF16)
    m_ref[...] = _dot(hb, wm_ref[...]).astype(BF16)
    mg_ref[...] = _sigmoid(_dot(hb, wmg_ref[...])).astype(BF16)
    g_ref[...] = _dot(hb, wg_ref[...])
    gr_ref[...] = g_ref[...].T[:SUBLANES, :]


def _inproj(x, moe, sc, sh, nw, wa, wm, wg, wmg, l, seq):
    t, d = x.shape
    tm = min(TILE_INPROJ, seq)
    tpb = seq // tm
    row = lambda i: (i, 0)
    bsel = lambda i: (i // tpb, 0, 0)
    wsel = lambda i: (l, 0, 0)
    once = pl.Buffered(1)
    na, nm, ng, nmg = wa.shape[2], wm.shape[2], wg.shape[2], wmg.shape[2]
    fuse = moe is not None
    moe_specs = [pl.BlockSpec((OUT_PARTS, tm, LANES), lambda i: (0, i, 0)), pl.BlockSpec((None, 1, d), bsel)]
    return pl.pallas_call(
        functools.partial(_inproj_kernel, fuse_residual=fuse),
        grid=(t // tm,),
        in_specs=[pl.BlockSpec((tm, d), row)] + (moe_specs if fuse else []) + [
            pl.BlockSpec((None, 1, d), bsel),
            pl.BlockSpec((None, 1, d), bsel),
            pl.BlockSpec((None, 1, d), wsel),
            pl.BlockSpec((None, d, na), wsel, pipeline_mode=once),
            pl.BlockSpec((None, d, nm), wsel, pipeline_mode=once),
            pl.BlockSpec((None, d, ng), wsel, pipeline_mode=once),
            pl.BlockSpec((None, d, nmg), wsel, pipeline_mode=once),
        ],
        out_specs=[
            pl.BlockSpec((tm, na), row),
            pl.BlockSpec((tm, nm), row),
            pl.BlockSpec((tm, nmg), row),
            pl.BlockSpec((SUBLANES, tm), lambda i: (0, i)),
        ] + ([pl.BlockSpec((tm, d), row)] if fuse else []),
        out_shape=[
            jax.ShapeDtypeStruct((t, na), BF16),
            jax.ShapeDtypeStruct((t, nm), BF16),
            jax.ShapeDtypeStruct((t, nmg), BF16),
            jax.ShapeDtypeStruct((SUBLANES, t), F32),
        ] + ([jax.ShapeDtypeStruct((t, d), F32)] if fuse else []),
        scratch_shapes=[pltpu.VMEM((tm, ng), F32)],
        compiler_params=_params("arbitrary"),
        name="inproj",
    )(x, *(moe if fuse else ()), sc, sh, nw, wa, wm, wg, wmg)


def _rope(t, cos, sin):
    w = t.shape[1]
    reps = w // LANES
    cosw = jnp.concatenate([cos] * reps, axis=1) if reps > 1 else cos
    sinw = jnp.concatenate([sin] * reps, axis=1) if reps > 1 else sin
    lane = lax.broadcasted_iota(I32, t.shape, 1)
    half = ROPE_DIM // 2
    up = pltpu.roll(t, w - half, axis=1)
    dn = pltpu.roll(t, half, axis=1)
    partner = jnp.where((lane % ROPE_DIM) < half, up, dn)
    return t * cosw + partner * sinw


def _head_norm(t, bd, w):
    ms = _dot((t * t).astype(BF16), bd)
    return t * lax.rsqrt(ms + EPS) * w


def _attn_kernel(sink_ref, cur_ref, prev_ref, cos_ref, sin_ref, cosp_ref, sinp_ref,
                 qw_ref, kw_ref, bdq_ref, bdk_ref, o_ref):
    tq = cur_ref.shape[0]
    nj = tq // ATTN_BLOCK
    qw = N_HEADS * HEAD_DIM
    kw = N_KV * HEAD_DIM
    blk0 = pl.program_id(1) * nj

    cur = cur_ref[...]
    q = cur[:, :qw].astype(F32)
    kc = cur[:, qw:qw + kw].astype(F32)
    vc = cur[:, qw + kw:].astype(F32)
    prev = prev_ref[...]
    kp = prev[:, :kw].astype(F32)
    vp = prev[:, kw:].astype(F32)

    cos, sin = cos_ref[...], sin_ref[...]
    q = _rope(_head_norm(q, bdq_ref[...], qw_ref[...]), cos, sin)
    kc = _rope(_head_norm(kc, bdk_ref[...], kw_ref[...]), cos, sin)
    kp = _rope(_head_norm(kp, bdk_ref[...], kw_ref[...]), cosp_ref[...], sinp_ref[...])
    qb = q.astype(BF16)

    def both_halves(x2):
        swapped = pltpu.roll(x2, HEAD_DIM, axis=1)
        first = lax.broadcasted_iota(I32, x2.shape, 1) < HEAD_DIM
        return jnp.concatenate([jnp.where(first, x2, swapped), jnp.where(first, swapped, x2)], axis=1).astype(BF16)

    k_all = both_halves(jnp.concatenate([kp, kc], axis=0))
    v_all = both_halves(jnp.concatenate([vp, vc], axis=0))

    lane = lax.broadcasted_iota(I32, (ATTN_BLOCK, LANES), 1)
    lo = lane < HEAD_DIM
    zero = jnp.zeros((ATTN_BLOCK, LANES), BF16)
    g_heads = N_HEADS // N_KV
    ri = lax.broadcasted_iota(I32, (g_heads * ATTN_BLOCK, ATTN_BLOCK), 0) % ATTN_BLOCK
    ci = lax.broadcasted_iota(I32, (g_heads * ATTN_BLOCK, ATTN_BLOCK), 1)
    from_prev = ci > ri
    head_row = lax.broadcasted_iota(I32, (g_heads * ATTN_BLOCK, 1), 0) // ATTN_BLOCK
    ones_v = jnp.ones((2 * ATTN_BLOCK, LANES), BF16)
    zero_p = jnp.zeros((g_heads * ATTN_BLOCK, ATTN_BLOCK), BF16)

    tiles = [(j, g) for j in range(nj) for g in range(N_KV)]
    scores = {}
    for j, g in tiles:
        rows = slice(j * ATTN_BLOCK, (j + 1) * ATTN_BLOCK)
        band = slice(j * ATTN_BLOCK, (j + 2) * ATTN_BLOCK)
        qp0 = qb[rows, (2 * g) * LANES:(2 * g + 1) * LANES]
        qp1 = qb[rows, (2 * g + 1) * LANES:(2 * g + 2) * LANES]
        q4 = jnp.concatenate([jnp.where(lo, qp0, zero), jnp.where(lo, zero, qp0),
                              jnp.where(lo, qp1, zero), jnp.where(lo, zero, qp1)], axis=0)
        scores[j, g] = _dot_nt(q4, k_all[band, g * LANES:(g + 1) * LANES])

    probs, sink_term = {}, {}
    for j, g in tiles:
        s2 = scores[j, g]
        if j == 0:
            prev_ok = ci > ri + (1 - jnp.minimum(blk0, 1)) * ATTN_BLOCK
            s = jnp.where(prev_ok, s2[:, :ATTN_BLOCK], jnp.where(from_prev, -jnp.inf, s2[:, ATTN_BLOCK:]))
        else:
            s = jnp.where(from_prev, s2[:, :ATTN_BLOCK], s2[:, ATTN_BLOCK:])
        sink = jnp.full((g_heads * ATTN_BLOCK, 1), sink_ref[g_heads * g], F32)
        for r in range(1, g_heads):
            sink = jnp.where(head_row == r, sink_ref[g_heads * g + r], sink)
        m = jnp.maximum(jnp.max(s, axis=-1, keepdims=True), sink)
        p = jnp.exp(s - m)
        pb = p.astype(BF16)
        probs[j, g] = jnp.concatenate([jnp.where(from_prev, pb, zero_p), jnp.where(from_prev, zero_p, pb)], axis=1)
        sink_term[j, g] = jnp.exp(sink - m)

    for j, g in tiles:
        rows = slice(j * ATTN_BLOCK, (j + 1) * ATTN_BLOCK)
        band = slice(j * ATTN_BLOCK, (j + 2) * ATTN_BLOCK)
        o8 = _dot(probs[j, g], jnp.concatenate([v_all[band, g * LANES:(g + 1) * LANES], ones_v], axis=1))
        o4 = o8[:, :LANES] / (o8[:, LANES:] + sink_term[j, g])
        b = ATTN_BLOCK
        o_ref[rows, (2 * g) * LANES:(2 * g + 1) * LANES] = jnp.where(lo, o4[0:b], o4[b:2 * b]).astype(BF16)
        o_ref[rows, (2 * g + 1) * LANES:(2 * g + 2) * LANES] = jnp.where(
            lo, o4[2 * b:3 * b], o4[3 * b:4 * b]).astype(BF16)


def _attention(a_in, cos_t, sin_t, sinks_l, qw, kw, bdq, bdk, batch, seq):
    t = a_in.shape[0]
    tq = min(TILE_ATTN, seq)
    nj = tq // ATTN_BLOCK
    tpb = seq // tq
    bpb = seq // ATTN_BLOCK
    qwid = N_HEADS * HEAD_DIM
    kvw = 2 * N_KV * HEAD_DIM
    cur = lambda b, i: (b * tpb + i, 0)
    prv = lambda b, i: (b * bpb + jnp.maximum(i * nj - 1, 0), qwid // kvw)
    prv0 = lambda b, i: (b * bpb + jnp.maximum(i * nj - 1, 0), 0)
    const = lambda b, i: (0, 0)
    return pl.pallas_call(
        _attn_kernel,
        grid=(batch, tpb),
        in_specs=[
            pl.BlockSpec(memory_space=pltpu.SMEM),
            pl.BlockSpec((tq, qwid + kvw), cur),
            pl.BlockSpec((ATTN_BLOCK, kvw), prv),
            pl.BlockSpec((tq, LANES), cur),
            pl.BlockSpec((tq, LANES), cur),
            pl.BlockSpec((ATTN_BLOCK, LANES), prv0),
            pl.BlockSpec((ATTN_BLOCK, LANES), prv0),
            pl.BlockSpec((1, qwid), const),
            pl.BlockSpec((1, kvw // 2), const),
            pl.BlockSpec((qwid, qwid), const),
            pl.BlockSpec((kvw // 2, kvw // 2), const),
        ],
        out_specs=pl.BlockSpec((tq, qwid), cur),
        out_shape=jax.ShapeDtypeStruct((t, qwid), BF16),
        compiler_params=_params("arbitrary", "arbitrary"),
        name="swa_attention",
    )(sinks_l, a_in, a_in, cos_t, sin_t, cos_t, sin_t, qw, kw, bdq, bdk)


def _mlstm_kernel(min_ref, gr_ref, cw_ref, cb_ref, bcol_ref, nw_ref,
                  hm_ref, ext_ref, q_ref, kt_ref, st_ref, mx_ref, ab_ref, bc_ref):
    tt = min_ref.shape[0]
    mw = M_HEADS * M_DIM
    nchunks = tt // CHUNK

    @pl.when(pl.program_id(1) == 0)
    def _():
        ext_ref[0:SUBLANES, :] = jnp.zeros((SUBLANES, 2 * mw), F32)
        st_ref[...] = jnp.zeros(st_ref.shape, F32)
        mx_ref[...] = jnp.zeros(mx_ref.shape, F32)

    def conv_block(cols):
        u = min_ref[:, cols].astype(F32)
        ext_ref[SUBLANES:SUBLANES + tt, cols] = u
        acc = cb_ref[:, cols] + cw_ref[CONV_K - 1:CONV_K, cols] * u
        for jj in range(CONV_K - 1):
            off = SUBLANES - (CONV_K - 1) + jj
            acc = acc + cw_ref[jj:jj + 1, cols] * ext_ref[off:off + tt, cols]
        ext_ref[0:SUBLANES, cols] = u[tt - SUBLANES:tt, :]
        return acc * _sigmoid(acc)

    def q_body(h, carry):
        cols = pl.ds(pl.multiple_of(h * M_DIM, M_DIM), M_DIM)
        q_ref[:, cols] = conv_block(cols).astype(BF16)
        return carry

    def k_body(h, carry):
        off = pl.multiple_of(h * M_DIM, M_DIM)
        act = conv_block(pl.ds(mw + off, M_DIM)) * (M_DIM ** -0.5)
        for j in range(nchunks):
            kt_ref[pl.ds(off, M_DIM), j * CHUNK:(j + 1) * CHUNK] = act[j * CHUNK:(j + 1) * CHUNK, :].T
        return carry

    lax.fori_loop(0, M_HEADS, q_body, 0)
    lax.fori_loop(0, M_HEADS, k_body, 0)

    ri = lax.broadcasted_iota(I32, (CHUNK, CHUNK), 0)
    ci = lax.broadcasted_iota(I32, (CHUNK, CHUNK), 1)
    causal = ci <= ri
    triu = jnp.where(ri <= ci, 1.0, 0.0).astype(BF16)
    ones_half = jnp.ones((CHUNK, M_DIM), BF16)
    mean_mat = jnp.full((M_DIM, M_DIM), 1.0 / M_DIM, BF16)
    sub = lax.broadcasted_iota(I32, (SUBLANES, CHUNK), 0)
    heads = range(M_HEADS)

    pad_rows = jnp.zeros((CHUNK - SUBLANES, CHUNK), F32)
    zero_rows = jnp.zeros((SUBLANES, CHUNK), F32)

    def gate_body(jg, carry):
        for u_ in range(GATE_CHUNKS_PER_STEP):
            rs = pl.ds(pl.multiple_of((jg * GATE_CHUNKS_PER_STEP + u_) * CHUNK, CHUNK), CHUNK)
            gr = gr_ref[:, rs] + bcol_ref[...]
            ls = _log_sigmoid(gr)
            ls1 = ls.astype(BF16).astype(F32)
            ls2 = (ls - ls1).astype(BF16).astype(F32)
            pieces = jnp.concatenate([ls1, ls2, ls - ls1 - ls2, zero_rows], axis=0).astype(BF16)
            sums = _dot(pieces, triu)
            br = sums[0:SUBLANES] + sums[SUBLANES:2 * SUBLANES] + sums[2 * SUBLANES:3 * SUBLANES]
            ab = jnp.where(sub < M_HEADS, gr - pltpu.roll(br, M_HEADS, axis=0), br)
            ab_ref[:, rs] = ab
            bc_ref[rs, :] = jnp.concatenate([ab, pad_rows], axis=0).T
        return carry

    lax.fori_loop(0, nchunks // GATE_CHUNKS_PER_STEP, gate_body, 0)

    def group_body(cg, carry):
        rows, ab = [], []
        for u_ in range(CHUNKS_PER_STEP):
            r0 = pl.multiple_of((cg * CHUNKS_PER_STEP + u_) * CHUNK, CHUNK)
            rows.append(pl.ds(r0, CHUNK))
            ab.append(ab_ref[:, rows[u_]])
        lanes = [(u_, h) for u_ in range(CHUNKS_PER_STEP) for h in heads]
        a_r = {(u_, h): ab[u_][h:h + 1, :] for u_, h in lanes}
        b_last = {(u_, h): ab[u_][M_HEADS + h:M_HEADS + h + 1, CHUNK - 1:CHUNK] for u_, h in lanes}

        m_prev, a_max, a_dec, s_in = {}, {}, {}, {}
        m_run = [mx_ref[h][0:1, 0:1] for h in heads]
        for k in lanes:
            u_, h = k
            m_prev[k] = m_run[h]
            a_max[k] = jnp.max(a_r[k], axis=-1, keepdims=True)
            m_loc = b_last[k] + a_max[k]
            m_new = jnp.maximum(b_last[k] + m_prev[k], m_loc)
            a_dec[k] = jnp.exp(b_last[k] + m_prev[k] - m_new)
            s_in[k] = jnp.exp(m_loc - m_new)
            m_run[h] = m_new
        for h in heads:
            mx_ref[h] = jnp.broadcast_to(m_run[h], (SUBLANES, LANES))

        q, v_ext, s_qk, kv = {}, {}, {}, {}
        for k in lanes:
            u_, h = k
            rs = rows[u_]
            q[k] = q_ref[rs, h * M_DIM:(h + 1) * M_DIM]
            kt = kt_ref[h * M_DIM:(h + 1) * M_DIM, rs]
            v = min_ref[rs, 2 * mw + h * M_DIM:2 * mw + (h + 1) * M_DIM]
            v_ext[k] = jnp.concatenate([v, ones_half], axis=1)
            s_qk[k] = _dot(q[k], kt.astype(BF16))
            e_r = jnp.exp(a_r[k] - a_max[k])
            kv[k] = _dot((kt * e_r).astype(BF16), v_ext[k])

        thr, qk, inter = {}, {}, {}
        for k in lanes:
            u_, h = k
            a_mat = jnp.where(causal, a_r[k], -jnp.inf)
            mu = jnp.maximum(jnp.max(a_mat, axis=-1, keepdims=True), m_prev[k])
            b_c = bc_ref[rows[u_], M_HEADS + h:M_HEADS + h + 1]
            thr[k] = jnp.broadcast_to(jnp.exp(-(b_c + mu)), (CHUNK, M_DIM))
            mu_b = jnp.broadcast_to(mu, (CHUNK, CHUNK))
            inter[k] = jnp.exp(m_prev[k] - mu_b)
            qk[k] = (s_qk[k] * jnp.exp(a_mat - mu_b)).astype(BF16)

        q_state = {}
        state = [st_ref[h] for h in heads]
        for k in lanes:
            u_, h = k
            q_state[k] = _dot(q[k], state[h].astype(BF16))
            state[h] = a_dec[k] * state[h] + s_in[k] * kv[k]
        for h in heads:
            st_ref[h] = state[h]

        for k in lanes:
            u_, h = k
            hs = slice(h * M_DIM, (h + 1) * M_DIM)
            num = _dot(qk[k], v_ext[k])
            den = jnp.maximum(jnp.abs(num[:, M_DIM:] + inter[k] * q_state[k][:, M_DIM:]), thr[k])
            hh = (num[:, :M_DIM] + inter[k] * q_state[k][:, :M_DIM]) / den
            msq = _dot((hh * hh).astype(BF16), mean_mat)
            hn = hh * lax.rsqrt(msq + EPS) * nw_ref[:, hs]
            og = min_ref[rows[u_], 3 * mw + h * M_DIM:3 * mw + (h + 1) * M_DIM].astype(F32)
            hm_ref[rows[u_], hs] = (_sigmoid(og) * hn).astype(BF16)
        return carry

    lax.fori_loop(0, nchunks // CHUNKS_PER_STEP, group_body, 0)


def _mlstm(m_in, grow, conv_w, conv_b, bcol, nw, batch, seq):
    t = m_in.shape[0]
    tt = min(TILE_MLSTM, seq)
    tpb = seq // tt
    mw = M_HEADS * M_DIM
    cur = lambda b, i: (b * tpb + i, 0)
    const = lambda b, i: (0, 0)
    return pl.pallas_call(
        _mlstm_kernel,
        grid=(batch, tpb),
        in_specs=[
            pl.BlockSpec((tt, 4 * mw), cur),
            pl.BlockSpec((SUBLANES, tt), lambda b, i: (0, b * tpb + i)),
            pl.BlockSpec((CONV_K, 2 * mw), const),
            pl.BlockSpec((1, 2 * mw), const),
            pl.BlockSpec((SUBLANES, LANES), const),
            pl.BlockSpec((1, mw), const),
        ],
        out_specs=pl.BlockSpec((tt, mw), cur),
        out_shape=jax.ShapeDtypeStruct((t, mw), BF16),
        scratch_shapes=[
            pltpu.VMEM((tt + SUBLANES, 2 * mw), F32),
            pltpu.VMEM((tt, mw), BF16),
            pltpu.VMEM((mw, tt), F32),
            pltpu.VMEM((M_HEADS, M_DIM, 2 * M_DIM), F32),
            pltpu.VMEM((M_HEADS, SUBLANES, LANES), F32),
            pltpu.VMEM((SUBLANES, tt), F32),
            pltpu.VMEM((tt, LANES), F32),
        ],
        compiler_params=_params("arbitrary", "arbitrary"),
        name="mlstm",
    )(m_in, grow, conv_w, conv_b, bcol, nw)


def _merge_kernel(o_ref, hm_ref, mg_ref, x_ref, g1_ref, sc_ref, sh_ref, nw_ref,
                  wa_ref, wm_ref, wo_ref, rwh_ref, rwl_ref, rb_ref, tri_ref,
                  x1_ref, pay_ref, route_ref, cnt_ref, carry_ref):
    tm, d = x_ref.shape

    @pl.when(pl.program_id(0) == 0)
    def _():
        carry_ref[...] = jnp.zeros(carry_ref.shape, F32)

    ya = _dot(o_ref[...], wa_ref[...])
    yb = _dot(hm_ref[...], wm_ref[...])
    mg = mg_ref[...]
    merged = mg[:, :d].astype(F32) * ya + mg[:, d:].astype(F32) * yb
    x1 = x_ref[...] + g1_ref[...] * _dot(merged.astype(BF16), wo_ref[...])
    x1_ref[...] = x1

    ms = jnp.mean(x1 * x1, axis=-1, keepdims=True)
    h2 = x1 * lax.rsqrt(ms + EPS) * (nw_ref[...] * (1.0 + sc_ref[...])) + sh_ref[...]
    hi = h2.astype(BF16)
    hif = hi.astype(F32)
    lo = (h2 - hif).astype(BF16)
    r_hi = _dot_nt(rwh_ref[...], hi)
    r_lo = _dot_nt(rwl_ref[...], lo)
    sc_t = _sigmoid(r_hi[:N_EXPERTS] + r_hi[N_EXPERTS:] + r_lo)
    sel_t = sc_t + rb_ref[:, 0:1]

    def row(a, e):
        return a[e:e + 1, :]

    best = None
    gi = jnp.zeros((1, tm), I32)
    for g in range(N_GROUPS):
        r = [row(sel_t, EPG * g + i) for i in range(EPG)]
        gs = None
        for i in range(EPG):
            for j in range(i + 1, EPG):
                pr = r[i] + r[j]
                gs = pr if gs is None else jnp.maximum(gs, pr)
        if best is None:
            best = gs
        else:
            upd = gs > best
            gi = jnp.where(upd, g, gi)
            best = jnp.maximum(best, gs)

    def pick(a, i):
        out = row(a, i)
        for g in range(1, N_GROUPS):
            out = jnp.where(gi == g, row(a, EPG * g + i), out)
        return out

    v = [pick(sel_t, i) for i in range(EPG)]
    s = [pick(sc_t, i) for i in range(EPG)]

    def argmax4(vals):
        bv, bi = vals[0], jnp.zeros((1, tm), I32)
        for i in range(1, EPG):
            upd = vals[i] > bv
            bi = jnp.where(upd, i, bi)
            bv = jnp.maximum(bv, vals[i])
        return bi

    i1 = argmax4(v)
    i2 = argmax4([jnp.where(i1 == i, -jnp.inf, v[i]) for i in range(EPG)])
    ia = jnp.minimum(i1, i2)
    ib = jnp.maximum(i1, i2)
    pidx = jnp.where(ia == 0, ib - 1, jnp.where(ia == 1, jnp.where(ib == 3, 3, 5), 4))
    bucket = gi * N_PAIRS + pidx

    def by_index(vals, idx):
        out = vals[0]
        for i in range(1, EPG):
            out = jnp.where(idx == i, vals[i], out)
        return out

    swap = pidx == N_PAIRS - 1
    s_lo, s_hi = by_index(s, ia), by_index(s, ib)
    s_a, s_b = jnp.where(swap, s_hi, s_lo), jnp.where(swap, s_lo, s_hi)
    gate_a = s_a / (s_a + s_b)
    gate_b = s_b / (s_a + s_b)

    brow = lax.broadcasted_iota(I32, (BUCKET_ROWS, tm), 0)
    onehot = brow == bucket
    cums = _dot(jnp.where(onehot, 1.0, 0.0).astype(BF16), tri_ref[...])
    carry = carry_ref[...]
    rank = jnp.sum(jnp.where(onehot, carry[:, 0:1] + cums, 0.0), axis=0, keepdims=True) - 1.0
    new_carry = carry + cums[:, tm - 1:tm]
    carry_ref[...] = new_carry
    cnt_ref[...] = new_carry

    route_ref[...] = jnp.concatenate(
        [bucket.astype(F32), gate_a, gate_b, rank, jnp.zeros((SUBLANES - 4, tm), F32)], axis=0)

    half = d // 2
    packed = _pack_bf16_pairs(hif, rounded=True)
    for cpart in range(half // LANES):
        pay_ref[cpart] = packed[:, cpart * LANES:(cpart + 1) * LANES]
    gates_t = jnp.concatenate([gate_a, gate_b, jnp.zeros((LANES - 2, tm), F32)], axis=0)
    pay_ref[half // LANES] = lax.bitcast_convert_type(gates_t.T, U32)


def _merge(o_attn, hm, mg, x, g1, sc2, sh2, nw, wa, wm, wo, rwh, rwl, rb, tri, l, seq):
    t, d = x.shape
    tm = tri.shape[0]
    tpb = seq // tm
    row = lambda i: (i, 0)
    bsel = lambda i: (i // tpb, 0, 0)
    wsel = lambda i: (l, 0, 0)
    const = lambda i: (0, 0)
    hw = o_attn.shape[1]
    return pl.pallas_call(
        _merge_kernel,
        grid=(t // tm,),
        in_specs=[
            pl.BlockSpec((tm, hw), row),
            pl.BlockSpec((tm, hw), row),
            pl.BlockSpec((tm, 2 * d), row),
            pl.BlockSpec((tm, d), row),
            pl.BlockSpec((None, 1, d), bsel),
            pl.BlockSpec((None, 1, d), bsel),
            pl.BlockSpec((None, 1, d), bsel),
            pl.BlockSpec((None, 1, d), wsel),
            pl.BlockSpec((None, hw, d), wsel),
            pl.BlockSpec((None, hw, d), wsel),
            pl.BlockSpec((None, d, d), wsel),
            pl.BlockSpec((2 * N_EXPERTS, d), const),
            pl.BlockSpec((N_EXPERTS, d), const),
            pl.BlockSpec((N_EXPERTS, LANES), const),
            pl.BlockSpec((tm, tm), const),
        ],
        out_specs=[
            pl.BlockSpec((tm, d), row),
            pl.BlockSpec((PAY_PARTS, tm, LANES), lambda i: (0, i, 0)),
            pl.BlockSpec((SUBLANES, tm), lambda i: (0, i)),
            pl.BlockSpec((BUCKET_ROWS, LANES), const),
        ],
        out_shape=[
            jax.ShapeDtypeStruct((t, d), F32),
            jax.ShapeDtypeStruct((PAY_PARTS, t, LANES), U32),
            jax.ShapeDtypeStruct((SUBLANES, t), F32),
            jax.ShapeDtypeStruct((BUCKET_ROWS, LANES), F32),
        ],
        scratch_shapes=[pltpu.VMEM((BUCKET_ROWS, LANES), F32)],
        compiler_params=_params("arbitrary"),
        name="merge_router",
    )(o_attn, hm, mg, x, g1, sc2, sh2, nw, wa, wm, wo, rwh, rwl, rb, tri)


def _sc_mesh():
    return plsc.VectorSubcoreMesh(core_axis_name="core", subcore_axis_name="subcore")


def _sc_scatter_rows(rows, dest, n_out):
    n, w = rows.shape

    @pl.kernel(out_type=jax.ShapeDtypeStruct((n_out, w), rows.dtype), mesh=_sc_mesh(), scratch_types=[])
    def scatter(x_hbm, i_hbm, o_hbm):
        def body(x_vmem, i_vmem):
            pltpu.sync_copy(x_vmem, o_hbm.at[i_vmem.at[0]])

        pltpu.emit_pipeline(
            body,
            grid=(n // SC_WINDOW,),
            in_specs=[pl.BlockSpec((SC_WINDOW, w), lambda i: (i, 0)),
                      pl.BlockSpec((1, SC_WINDOW), lambda i: (0, i))],
            out_specs=[],
            core_axis_name=("core", "subcore"),
            dimension_semantics=(pltpu.PARALLEL,),
        )(x_hbm, i_hbm)

    return scatter(rows, dest.reshape(1, n))


def _sc_gather_rows(src, idx):
    n = idx.shape[0]
    w = src.shape[1]

    @pl.kernel(out_type=jax.ShapeDtypeStruct((n, w), src.dtype), mesh=_sc_mesh(), scratch_types=[])
    def gather(x_hbm, i_hbm, o_hbm):
        def body(i_vmem, o_vmem):
            pltpu.sync_copy(x_hbm.at[i_vmem.at[0]], o_vmem)

        pltpu.emit_pipeline(
            body,
            grid=(n // SC_WINDOW,),
            in_specs=[pl.BlockSpec((1, SC_WINDOW), lambda i: (0, i))],
            out_specs=[pl.BlockSpec((SC_WINDOW, w), lambda i: (i, 0))],
            core_axis_name=("core", "subcore"),
            dimension_semantics=(pltpu.PARALLEL,),
        )(i_hbm, o_hbm)

    return gather(src, idx.reshape(1, n))


def _row_index_kernel(ps_ref, route_ref, o_ref, *, n_rows):
    bucket = route_ref[0:1, :].astype(I32)
    start = jnp.zeros(bucket.shape, I32)
    for b in range(N_BUCKETS):
        start = jnp.where(bucket == b, ps_ref[b], start)
    dest = start + route_ref[3:4, :].astype(I32)
    part = lax.broadcasted_iota(I32, o_ref.shape, 0)
    o_ref[...] = part * n_rows + dest


def _row_index(pad_starts, route, n_rows):
    t = route.shape[1]
    tm = min(TILE_ROW_INDEX, t)
    return pl.pallas_call(
        functools.partial(_row_index_kernel, n_rows=n_rows),
        grid=(t // tm,),
        in_specs=[pl.BlockSpec(memory_space=pltpu.SMEM), pl.BlockSpec((SUBLANES, tm), lambda i: (0, i))],
        out_specs=pl.BlockSpec((SUBLANES, tm), lambda i: (0, i)),
        out_shape=jax.ShapeDtypeStruct((SUBLANES, t), I32),
        compiler_params=_params("arbitrary"),
        name="row_index",
    )(pad_starts, route)


def _residual_kernel(x_ref, y_ref, g2_ref, o_ref):
    y = _unpack_bf16_pairs(jnp.concatenate([y_ref[c] for c in range(OUT_PARTS)], axis=1))
    o_ref[...] = x_ref[...] + g2_ref[...] * y


def _residual(x1, ytok, g2, seq):
    t, d = x1.shape
    tm = min(TILE_RESIDUAL, seq)
    tpb = seq // tm
    return pl.pallas_call(
        _residual_kernel,
        grid=(t // tm,),
        in_specs=[
            pl.BlockSpec((tm, d), lambda i: (i, 0)),
            pl.BlockSpec((OUT_PARTS, tm, LANES), lambda i: (0, i, 0)),
            pl.BlockSpec((None, 1, d), lambda i: (i // tpb, 0, 0)),
        ],
        out_specs=pl.BlockSpec((tm, d), lambda i: (i, 0)),
        out_shape=jax.ShapeDtypeStruct((t, d), F32),
        compiler_params=_params("arbitrary"),
        name="residual",
    )(x1, ytok, g2)


def _expert_kernel(ea_ref, eb_ref, nr_ref, xs_ref, wga_ref, wua_ref, wda_ref, wgb_ref, wub_ref, wdb_ref, ys_ref,
                   ga_ref, ua_ref, da_ref, gb_ref, ub_ref, db_ref):
    j = pl.program_id(0)
    nr = nr_ref[0]
    prev = jnp.maximum(j - 1, 0)

    @pl.when((j == 0) | (ea_ref[j] != ea_ref[prev]))
    def _():
        ga_ref[...] = wga_ref[...].astype(BF16)
        ua_ref[...] = wua_ref[...].astype(BF16)
        da_ref[...] = wda_ref[...].astype(BF16)

    @pl.when((j == 0) | (eb_ref[j] != eb_ref[prev]))
    def _():
        gb_ref[...] = wgb_ref[...].astype(BF16)
        ub_ref[...] = wub_ref[...].astype(BF16)
        db_ref[...] = wdb_ref[...].astype(BF16)

    @pl.when(j < nr)
    def _():
        x = _unpack_bf16_pairs(jnp.concatenate([xs_ref[c] for c in range(PAY_PARTS - 1)], axis=1)).astype(BF16)
        gl = lax.bitcast_convert_type(xs_ref[PAY_PARTS - 1], F32)

        def ffn(wg_ref, wu_ref, wd_ref):
            gte = _dot(x, wg_ref[...])
            act = gte * _sigmoid(gte) * _dot(x, wu_ref[...])
            return _dot(act.astype(BF16), wd_ref[...])

        y = _pack_bf16_pairs(gl[:, 0:1] * ffn(ga_ref, ua_ref, da_ref) + gl[:, 1:2] * ffn(gb_ref, ub_ref, db_ref))
        for c in range(OUT_PARTS):
            ys_ref[c] = y[:, c * LANES:(c + 1) * LANES]

    @pl.when(j >= nr)
    def _():
        ys_ref[...] = jnp.zeros(ys_ref.shape, U32)


def _experts(blk_ea, blk_eb, n_real, xs, wg, wu, wd, d):
    n_rows = xs.shape[1]
    nblk = n_rows // EXPERT_BLOCK
    f = wg.shape[2]
    grid_spec = pltpu.PrefetchScalarGridSpec(
        num_scalar_prefetch=3,
        grid=(nblk,),
        in_specs=[
            pl.BlockSpec((PAY_PARTS, EXPERT_BLOCK, LANES), lambda j, ea, eb, nr: (0, j, 0)),
            pl.BlockSpec((None, d, f), lambda j, ea, eb, nr: (ea[j], 0, 0)),
            pl.BlockSpec((None, d, f), lambda j, ea, eb, nr: (ea[j], 0, 0)),
            pl.BlockSpec((None, f, d), lambda j, ea, eb, nr: (ea[j], 0, 0)),
            pl.BlockSpec((None, d, f), lambda j, ea, eb, nr: (eb[j], 0, 0)),
            pl.BlockSpec((None, d, f), lambda j, ea, eb, nr: (eb[j], 0, 0)),
            pl.BlockSpec((None, f, d), lambda j, ea, eb, nr: (eb[j], 0, 0)),
        ],
        out_specs=pl.BlockSpec((OUT_PARTS, EXPERT_BLOCK, LANES), lambda j, ea, eb, nr: (0, j, 0)),
        scratch_shapes=[pltpu.VMEM((d, f), BF16), pltpu.VMEM((d, f), BF16), pltpu.VMEM((f, d), BF16)] * 2,
    )
    return pl.pallas_call(
        _expert_kernel,
        grid_spec=grid_spec,
        out_shape=jax.ShapeDtypeStruct((OUT_PARTS, n_rows, LANES), U32),
        compiler_params=_params("arbitrary"),
        name="experts",
    )(blk_ea, blk_eb, n_real, xs, wg, wu, wd, wg, wu, wd)


_PAIR_A = (0, 0, 0, 1, 2, 2)
_PAIR_B = (1, 2, 3, 3, 3, 1)


def kernel(x, c, positions, ada_w, ada_b, norm_mix_w, norm_ffn_w, w_in, b_igate, b_fgate, q_norm_w, k_norm_w,
           sinks, conv_w, conv_b, mlstm_norm_w, w_attn_up, w_mlstm_up, w_out, router_w, router_bias,
           w_gate, w_up, w_down):
    batch, seq, d = x.shape
    depth = w_in.shape[0]
    t = batch * seq
    qw = N_HEADS * HEAD_DIM
    kvw = N_KV * HEAD_DIM
    mw = M_HEADS * M_DIM

    o = 0
    cols = {}
    for name, wdt in (("q", qw), ("k", kvw), ("v", kvw), ("mqk", 2 * mw), ("mv", mw), ("mi", M_HEADS),
                      ("mf", M_HEADS), ("mo", mw), ("ga", d), ("gb", d)):
        cols[name] = (o, o + wdt)
        o += wdt

    def wc(name, lo=0, hi=None):
        s, e = cols[name]
        return w_in[:, :, s + lo:(s + hi if hi is not None else e)]

    w_a = jnp.concatenate([wc("q"), wc("k"), wc("v")], axis=2).astype(BF16)
    w_m = jnp.concatenate([wc("mqk"), wc("mv"), wc("mo")], axis=2).astype(BF16)
    w_g = jnp.concatenate([wc("mi"), wc("mf"), jnp.zeros((depth, d, LANES - 2 * M_HEADS), F32)], axis=2).astype(BF16)
    w_mg = jnp.concatenate([wc("ga"), wc("gb")], axis=2).astype(BF16)
    w_au = w_attn_up.astype(BF16)
    w_mu = w_mlstm_up.astype(BF16)
    w_o = w_out.astype(BF16)
    n_e = w_gate.shape[1]
    w_g8 = w_gate.reshape(depth * n_e, d, -1)
    w_u8 = w_up.reshape(depth * n_e, d, -1)
    w_d = w_down.reshape(depth * n_e, -1, d)

    rw_t = router_w.astype(F32).T
    rw_top = rw_t.astype(BF16)
    rw_hi = jnp.concatenate([rw_top, (rw_t - rw_top.astype(F32)).astype(BF16)], axis=0)
    rw_lo = rw_top
    rb = jnp.broadcast_to(router_bias.astype(F32)[:, None], (n_e, LANES))

    qn_w = jnp.tile(q_norm_w * (HEAD_DIM ** -0.5), (1, N_HEADS)).reshape(depth, 1, qw)
    kn_w = jnp.tile(k_norm_w, (1, N_KV)).reshape(depth, 1, kvw)
    seg = jnp.arange(qw) // HEAD_DIM
    bdq = jnp.where(seg[:, None] == seg[None, :], 1.0 / HEAD_DIM, 0.0).astype(BF16)
    bdk = bdq[:kvw, :kvw]

    inv_freq = ROPE_THETA ** (-(jnp.arange(0, ROPE_DIM, 2, dtype=F32) / ROPE_DIM))
    ang = positions.astype(F32).reshape(1, t) * inv_freq[:, None]
    cos8, sin8 = jnp.cos(ang).T, jnp.sin(ang).T
    pad1 = jnp.ones((t, HEAD_DIM - ROPE_DIM), F32)
    pad0 = jnp.zeros((t, HEAD_DIM - ROPE_DIM), F32)
    cos_t = jnp.tile(jnp.concatenate([cos8, cos8, pad1], axis=1), (1, LANES // HEAD_DIM))
    sin_t = jnp.tile(jnp.concatenate([-sin8, sin8, pad0], axis=1), (1, LANES // HEAD_DIM))

    gate_bias = jnp.concatenate([b_igate, b_fgate], axis=1).astype(F32)
    bcol = jnp.broadcast_to(gate_bias[:, :, None], (depth, 2 * M_HEADS, LANES))

    tm_merge = min(TILE_MERGE, seq)
    ii = jnp.arange(tm_merge)
    tri = (ii[:, None] <= ii[None, :]).astype(BF16)

    n_blk = (t + N_BUCKETS * (EXPERT_BLOCK - 1)) // EXPERT_BLOCK + 1
    n_rows = n_blk * EXPERT_BLOCK
    pair_a = jnp.asarray(_PAIR_A, I32)
    pair_b = jnp.asarray(_PAIR_B, I32)

    c_pad = jnp.zeros((SUBLANES, d), F32).at[:batch].set(c)
    mod = _ada_mod(c_pad, ada_w, ada_b)[:, :batch]

    xf = x.reshape(t, d)
    moe = None
    for l in range(depth):
        sh1, sc1, g1, sh2, sc2, g2 = [m.reshape(batch, 1, d) for m in jnp.split(mod[l], 6, axis=-1)]

        outs = _inproj(xf, moe, sc1, sh1, norm_mix_w.reshape(depth, 1, d), w_a, w_m, w_g, w_mg, l, seq)
        a_in, m_in, mg, grow = outs[:4]
        if moe is not None:
            xf = outs[4]
        o_attn = _attention(a_in, cos_t, sin_t, sinks[l], qn_w[l], kn_w[l], bdq, bdk, batch, seq)
        hm = _mlstm(m_in, grow, conv_w[l], conv_b[l].reshape(1, -1), bcol[l], mlstm_norm_w[l].reshape(1, mw),
                    batch, seq)
        x1, pay, route, cnt = _merge(o_attn, hm, mg, xf, g1, sc2, sh2, norm_ffn_w.reshape(depth, 1, d),
                                     w_au, w_mu, w_o, rw_hi, rw_lo, rb, tri, l, seq)

        counts = cnt[:N_BUCKETS, 0].astype(I32)
        padded = (counts + EXPERT_BLOCK - 1) // EXPERT_BLOCK * EXPERT_BLOCK
        pad_ends = jnp.cumsum(padded)
        pad_starts = pad_ends - padded
        row_idx = _row_index(jnp.concatenate([pad_starts, jnp.zeros((BUCKET_ROWS - N_BUCKETS,), I32)]), route, n_rows)
        blk_start = jnp.arange(n_blk, dtype=I32) * EXPERT_BLOCK
        blk_bucket = jnp.minimum(jnp.sum((pad_ends[None, :] <= blk_start[:, None]).astype(I32), axis=1), N_BUCKETS - 1)
        grp = blk_bucket // N_PAIRS
        blk_ea = (l * n_e + grp * EPG + pair_a[blk_bucket % N_PAIRS]).astype(I32)
        blk_eb = (l * n_e + grp * EPG + pair_b[blk_bucket % N_PAIRS]).astype(I32)
        n_real = (pad_ends[-1:] // EXPERT_BLOCK).astype(I32)

        xs = _sc_scatter_rows(pay.reshape(PAY_PARTS * t, LANES), row_idx[:PAY_PARTS].reshape(-1),
                              PAY_PARTS * n_rows).reshape(PAY_PARTS, n_rows, LANES)
        ys = _experts(blk_ea, blk_eb, n_real, xs, w_g8, w_u8, w_d, d)
        ytok = _sc_gather_rows(ys.reshape(OUT_PARTS * n_rows, LANES),
                               row_idx[:OUT_PARTS].reshape(-1)).reshape(OUT_PARTS, t, LANES)
        xf, moe = x1, (ytok, g2)
    return _residual(xf, moe[0], moe[1], seq).reshape(batch, seq, d)
```

```python
import functools

import jax
import jax.numpy as jnp
from jax import lax
from jax.experimental import pallas as pl
from jax.experimental.pallas import tpu as pltpu
from jax.experimental.pallas import tpu_sc as plsc

F32 = jnp.float32
BF16 = jnp.bfloat16
U32 = jnp.uint32
I32 = jnp.int32
HIGHEST = lax.Precision.HIGHEST

HEAD_DIM = 64
N_HEADS = 8
N_KV = 2
ROPE_DIM = 16
ROPE_THETA = 500000.0
ATTN_BLOCK = 128
M_HEADS = 4
M_DIM = 128
CONV_K = 4
N_EXPERTS = 16
N_GROUPS = 4
EPG = 4
EPS = 1e-6

LANES = 128
SUBLANES = 8

TILE_INPROJ = 1024
TILE_ATTN = 1024
TILE_MLSTM = 2048
TILE_MERGE = 1024
TILE_RESIDUAL = 1024
TILE_ROW_INDEX = 8192
ADA_COLS = 3072
CHUNK = 128
CHUNKS_PER_STEP = 4
GATE_CHUNKS_PER_STEP = 4
N_PAIRS = 6
N_BUCKETS = N_GROUPS * N_PAIRS
BUCKET_ROWS = 32
EXPERT_BLOCK = 512
PAY_PARTS = 5
OUT_PARTS = 4
SC_WINDOW = 128
VMEM_LIMIT = 56 * 1024 * 1024


def _dot(a, b, precision=None):
    return jnp.dot(a, b, preferred_element_type=F32, precision=precision)


def _dot_nt(a, b):
    return lax.dot_general(a, b, (((1,), (1,)), ((), ())), preferred_element_type=F32)


def _sigmoid(x):
    return 1.0 / (1.0 + jnp.exp(-x))


def _log_sigmoid(x):
    return jnp.minimum(x, 0.0) - jnp.log1p(jnp.exp(-jnp.abs(x)))


def _pack_bf16_pairs(v, rounded=False):
    n = v.shape[1] // 2
    bits = lax.bitcast_convert_type(v if rounded else v.astype(BF16).astype(F32), U32)
    return bits[:, :n] | (bits[:, n:] >> 16)


def _unpack_bf16_pairs(w):
    hi = lax.bitcast_convert_type(w & jnp.uint32(0xFFFF0000), F32)
    lo = lax.bitcast_convert_type(w << 16, F32)
    return jnp.concatenate([hi, lo], axis=1)


def _params(*sem):
    return pltpu.CompilerParams(dimension_semantics=sem, vmem_limit_bytes=VMEM_LIMIT)


def _ada_kernel(c_ref, w_ref, b_ref, o_ref):
    c = c_ref[...]
    ca = c * _sigmoid(c)
    o_ref[0] = _dot(ca, w_ref[0], HIGHEST) + b_ref[0]


def _ada_mod(c_pad, ada_w, ada_b):
    depth, d, n = ada_w.shape
    tn = ADA_COLS
    return pl.pallas_call(
        _ada_kernel,
        grid=(depth, n // tn),
        in_specs=[
            pl.BlockSpec((SUBLANES, d), lambda l, j: (0, 0)),
            pl.BlockSpec((1, d, tn), lambda l, j: (l, 0, j)),
            pl.BlockSpec((1, 1, tn), lambda l, j: (l, 0, j)),
        ],
        out_specs=pl.BlockSpec((1, SUBLANES, tn), lambda l, j: (l, 0, j)),
        out_shape=jax.ShapeDtypeStruct((depth, SUBLANES, n), F32),
        compiler_params=_params("arbitrary", "arbitrary"),
        name="ada_mod",
    )(c_pad, ada_w, ada_b.reshape(depth, 1, n))


def _inproj_kernel(*refs, fuse_residual):
    if fuse_residual:
        (x_ref, y_ref, g2_ref, sc_ref, sh_ref, nw_ref, wa_ref, wm_ref, wg_ref, wmg_ref,
         a_ref, m_ref, mg_ref, gr_ref, xo_ref, g_ref) = refs
        y = _unpack_bf16_pairs(jnp.concatenate([y_ref[c] for c in range(OUT_PARTS)], axis=1))
        x = x_ref[...] + g2_ref[...] * y
        xo_ref[...] = x
    else:
        (x_ref, sc_ref, sh_ref, nw_ref, wa_ref, wm_ref, wg_ref, wmg_ref,
         a_ref, m_ref, mg_ref, gr_ref, g_ref) = refs
        x = x_ref[...]
    ms = jnp.mean(x * x, axis=-1, keepdims=True)
    h = x * lax.rsqrt(ms + EPS) * (nw_ref[...] * (1.0 + sc_ref[...])) + sh_ref[...]
    hb = h.astype(BF16)
    a_ref[...] = _dot(hb, wa_ref[...]).astype(BF16)
    m_ref[...] = _dot(hb, wm_ref[...]).astype(BF16)
    mg_ref[...] = _sigmoid(_dot(hb, wmg_ref[...])).astype(BF16)
    g_ref[...] = _dot(hb, wg_ref[...])
    gr_ref[...] = g_ref[...].T[:SUBLANES, :]


def _inproj(x, moe, sc, sh, nw, wa, wm, wg, wmg, l, seq):
    t, d = x.shape
    tm = min(TILE_INPROJ, seq)
    tpb = seq // tm
    row = lambda i: (i, 0)
    bsel = lambda i: (i // tpb, 0, 0)
    wsel = lambda i: (l, 0, 0)
    once = pl.Buffered(1)
    na, nm, ng, nmg = wa.shape[2], wm.shape[2], wg.shape[2], wmg.shape[2]
    fuse = moe is not None
    moe_specs = [pl.BlockSpec((OUT_PARTS, tm, LANES), lambda i: (0, i, 0)), pl.BlockSpec((None, 1, d), bsel)]
    return pl.pallas_call(
        functools.partial(_inproj_kernel, fuse_residual=fuse),
        grid=(t // tm,),
        in_specs=[pl.BlockSpec((tm, d), row)] + (moe_specs if fuse else []) + [
            pl.BlockSpec((None, 1, d), bsel),
            pl.BlockSpec((None, 1, d), bsel),
            pl.BlockSpec((None, 1, d), wsel),
            pl.BlockSpec((None, d, na), wsel, pipeline_mode=once),
            pl.BlockSpec((None, d, nm), wsel, pipeline_mode=once),
            pl.BlockSpec((None, d, ng), wsel, pipeline_mode=once),
            pl.BlockSpec((None, d, nmg), wsel, pipeline_mode=once),
        ],
        out_specs=[
            pl.BlockSpec((tm, na), row),
            pl.BlockSpec((tm, nm), row),
            pl.BlockSpec((tm, nmg), row),
            pl.BlockSpec((SUBLANES, tm), lambda i: (0, i)),
        ] + ([pl.BlockSpec((tm, d), row)] if fuse else []),
        out_shape=[
            jax.ShapeDtypeStruct((t, na), BF16),
            jax.ShapeDtypeStruct((t, nm), BF16),
            jax.ShapeDtypeStruct((t, nmg), BF16),
            jax.ShapeDtypeStruct((SUBLANES, t), F32),
        ] + ([jax.ShapeDtypeStruct((t, d), F32)] if fuse else []),
        scratch_shapes=[pltpu.VMEM((tm, ng), F32)],
        compiler_params=_params("arbitrary"),
        name="inproj",
    )(x, *(moe if fuse else ()), sc, sh, nw, wa, wm, wg, wmg)


def _rope(t, cos, sin):
    w = t.shape[1]
    reps = w // LANES
    cosw = jnp.concatenate([cos] * reps, axis=1) if reps > 1 else cos
    sinw = jnp.concatenate([sin] * reps, axis=1) if reps > 1 else sin
    lane = lax.broadcasted_iota(I32, t.shape, 1)
    half = ROPE_DIM // 2
    up = pltpu.roll(t, w - half, axis=1)
    dn = pltpu.roll(t, half, axis=1)
    partner = jnp.where((lane % ROPE_DIM) < half, up, dn)
    return t * cosw + partner * sinw


def _head_norm(t, bd, w):
    ms = _dot((t * t).astype(BF16), bd)
    return t * lax.rsqrt(ms + EPS) * w


def _attn_kernel(sink_ref, cur_ref, prev_ref, cos_ref, sin_ref, cosp_ref, sinp_ref,
                 qw_ref, kw_ref, bdq_ref, bdk_ref, o_ref):
    tq = cur_ref.shape[0]
    nj = tq // ATTN_BLOCK
    qw = N_HEADS * HEAD_DIM
    kw = N_KV * HEAD_DIM
    blk0 = pl.program_id(1) * nj

    cur = cur_ref[...]
    q = cur[:, :qw].astype(F32)
    kc = cur[:, qw:qw + kw].astype(F32)
    vc = cur[:, qw + kw:].astype(F32)
    prev = prev_ref[...]
    kp = prev[:, :kw].astype(F32)
    vp = prev[:, kw:].astype(F32)

    cos, sin = cos_ref[...], sin_ref[...]
    q = _rope(_head_norm(q, bdq_ref[...], qw_ref[...]), cos, sin)
    kc = _rope(_head_norm(kc, bdk_ref[...], kw_ref[...]), cos, sin)
    kp = _rope(_head_norm(kp, bdk_ref[...], kw_ref[...]), cosp_ref[...], sinp_ref[...])
    qb = q.astype(BF16)

    def both_halves(x2):
        swapped = pltpu.roll(x2, HEAD_DIM, axis=1)
        first = lax.broadcasted_iota(I32, x2.shape, 1) < HEAD_DIM
        return jnp.concatenate([jnp.where(first, x2, swapped), jnp.where(first, swapped, x2)], axis=1).astype(BF16)

    k_all = both_halves(jnp.concatenate([kp, kc], axis=0))
    v_all = both_halves(jnp.concatenate([vp, vc], axis=0))

    lane = lax.broadcasted_iota(I32, (ATTN_BLOCK, LANES), 1)
    lo = lane < HEAD_DIM
    zero = jnp.zeros((ATTN_BLOCK, LANES), BF16)
    g_heads = N_HEADS // N_KV
    ri = lax.broadcasted_iota(I32, (g_heads * ATTN_BLOCK, ATTN_BLOCK), 0) % ATTN_BLOCK
    ci = lax.broadcasted_iota(I32, (g_heads * ATTN_BLOCK, ATTN_BLOCK), 1)
    from_prev = ci > ri
    head_row = lax.broadcasted_iota(I32, (g_heads * ATTN_BLOCK, 1), 0) // ATTN_BLOCK
    ones_v = jnp.ones((2 * ATTN_BLOCK, LANES), BF16)

    tiles = [(j, g) for j in range(nj) for g in range(N_KV)]
    scores = {}
    for j, g in tiles:
        rows = slice(j * ATTN_BLOCK, (j + 1) * ATTN_BLOCK)
        band = slice(j * ATTN_BLOCK, (j + 2) * ATTN_BLOCK)
        qp0 = qb[rows, (2 * g) * LANES:(2 * g + 1) * LANES]
        qp1 = qb[rows, (2 * g + 1) * LANES:(2 * g + 2) * LANES]
        q4 = jnp.concatenate([jnp.where(lo, qp0, zero), jnp.where(lo, zero, qp0),
                              jnp.where(lo, qp1, zero), jnp.where(lo, zero, qp1)], axis=0)
        scores[j, g] = _dot_nt(q4, k_all[band, g * LANES:(g + 1) * LANES])

    probs, sink_term = {}, {}
    for j, g in tiles:
        s2 = scores[j, g]
        if j == 0:
            prev_ok = ci > ri + (1 - jnp.minimum(blk0, 1)) * ATTN_BLOCK
            s = jnp.where(prev_ok, s2[:, :ATTN_BLOCK], jnp.where(from_prev, -jnp.inf, s2[:, ATTN_BLOCK:]))
        else:
            s = jnp.where(from_prev, s2[:, :ATTN_BLOCK], s2[:, ATTN_BLOCK:])
        sink = jnp.full((g_heads * ATTN_BLOCK, 1), sink_ref[g_heads * g], F32)
        for r in range(1, g_heads):
            sink = jnp.where(head_row == r, sink_ref[g_heads * g + r], sink)
        m = jnp.maximum(jnp.max(s, axis=-1, keepdims=True), sink)
        p = jnp.exp(s - m)
        probs[j, g] = jnp.concatenate([jnp.where(from_prev, p, 0.0), jnp.where(from_prev, 0.0, p)],
                                      axis=1).astype(BF16)
        sink_term[j, g] = jnp.exp(sink - m)

    for j, g in tiles:
        rows = slice(j * ATTN_BLOCK, (j + 1) * ATTN_BLOCK)
        band = slice(j * ATTN_BLOCK, (j + 2) * ATTN_BLOCK)
        o8 = _dot(probs[j, g], jnp.concatenate([v_all[band, g * LANES:(g + 1) * LANES], ones_v], axis=1))
        o4 = o8[:, :LANES] / (o8[:, LANES:] + sink_term[j, g])
        b = ATTN_BLOCK
        o_ref[rows, (2 * g) * LANES:(2 * g + 1) * LANES] = jnp.where(lo, o4[0:b], o4[b:2 * b]).astype(BF16)
        o_ref[rows, (2 * g + 1) * LANES:(2 * g + 2) * LANES] = jnp.where(
            lo, o4[2 * b:3 * b], o4[3 * b:4 * b]).astype(BF16)


def _attention(a_in, cos_t, sin_t, sinks_l, qw, kw, bdq, bdk, batch, seq):
    t = a_in.shape[0]
    tq = min(TILE_ATTN, seq)
    nj = tq // ATTN_BLOCK
    tpb = seq // tq
    bpb = seq // ATTN_BLOCK
    qwid = N_HEADS * HEAD_DIM
    kvw = 2 * N_KV * HEAD_DIM
    cur = lambda b, i: (b * tpb + i, 0)
    prv = lambda b, i: (b * bpb + jnp.maximum(i * nj - 1, 0), qwid // kvw)
    prv0 = lambda b, i: (b * bpb + jnp.maximum(i * nj - 1, 0), 0)
    const = lambda b, i: (0, 0)
    return pl.pallas_call(
        _attn_kernel,
        grid=(batch, tpb),
        in_specs=[
            pl.BlockSpec(memory_space=pltpu.SMEM),
            pl.BlockSpec((tq, qwid + kvw), cur),
            pl.BlockSpec((ATTN_BLOCK, kvw), prv),
            pl.BlockSpec((tq, LANES), cur),
            pl.BlockSpec((tq, LANES), cur),
            pl.BlockSpec((ATTN_BLOCK, LANES), prv0),
            pl.BlockSpec((ATTN_BLOCK, LANES), prv0),
            pl.BlockSpec((1, qwid), const),
            pl.BlockSpec((1, kvw // 2), const),
            pl.BlockSpec((qwid, qwid), const),
            pl.BlockSpec((kvw // 2, kvw // 2), const),
        ],
        out_specs=pl.BlockSpec((tq, qwid), cur),
        out_shape=jax.ShapeDtypeStruct((t, qwid), BF16),
        compiler_params=_params("arbitrary", "arbitrary"),
        name="swa_attention",
    )(sinks_l, a_in, a_in, cos_t, sin_t, cos_t, sin_t, qw, kw, bdq, bdk)


def _mlstm_kernel(min_ref, gr_ref, cw_ref, cb_ref, bcol_ref, nw_ref,
                  hm_ref, ext_ref, q_ref, kt_ref, st_ref, mx_ref, ab_ref, bc_ref):
    tt = min_ref.shape[0]
    mw = M_HEADS * M_DIM
    nchunks = tt // CHUNK

    @pl.when(pl.program_id(1) == 0)
    def _():
        ext_ref[0:SUBLANES, :] = jnp.zeros((SUBLANES, 2 * mw), F32)
        st_ref[...] = jnp.zeros(st_ref.shape, F32)
        mx_ref[...] = jnp.zeros(mx_ref.shape, F32)

    def conv_block(cols):
        u = min_ref[:, cols].astype(F32)
        ext_ref[SUBLANES:SUBLANES + tt, cols] = u
        acc = cb_ref[:, cols] + cw_ref[CONV_K - 1:CONV_K, cols] * u
        for jj in range(CONV_K - 1):
            off = SUBLANES - (CONV_K - 1) + jj
            acc = acc + cw_ref[jj:jj + 1, cols] * ext_ref[off:off + tt, cols]
        ext_ref[0:SUBLANES, cols] = u[tt - SUBLANES:tt, :]
        return acc * _sigmoid(acc)

    def q_body(h, carry):
        cols = pl.ds(pl.multiple_of(h * M_DIM, M_DIM), M_DIM)
        q_ref[:, cols] = conv_block(cols).astype(BF16)
        return carry

    def k_body(h, carry):
        off = pl.multiple_of(h * M_DIM, M_DIM)
        act = conv_block(pl.ds(mw + off, M_DIM)) * (M_DIM ** -0.5)
        for j in range(nchunks):
            kt_ref[pl.ds(off, M_DIM), j * CHUNK:(j + 1) * CHUNK] = act[j * CHUNK:(j + 1) * CHUNK, :].T
        return carry

    lax.fori_loop(0, M_HEADS, q_body, 0)
    lax.fori_loop(0, M_HEADS, k_body, 0)

    ri = lax.broadcasted_iota(I32, (CHUNK, CHUNK), 0)
    ci = lax.broadcasted_iota(I32, (CHUNK, CHUNK), 1)
    causal = ci <= ri
    triu = jnp.where(ri <= ci, 1.0, 0.0).astype(BF16)
    ones_half = jnp.ones((CHUNK, M_DIM), BF16)
    mean_mat = jnp.full((M_DIM, M_DIM), 1.0 / M_DIM, BF16)
    sub = lax.broadcasted_iota(I32, (SUBLANES, CHUNK), 0)
    heads = range(M_HEADS)

    pad_rows = jnp.zeros((CHUNK - SUBLANES, CHUNK), F32)
    zero_rows = jnp.zeros((SUBLANES, CHUNK), F32)

    def gate_body(jg, carry):
        for u_ in range(GATE_CHUNKS_PER_STEP):
            rs = pl.ds(pl.multiple_of((jg * GATE_CHUNKS_PER_STEP + u_) * CHUNK, CHUNK), CHUNK)
            gr = gr_ref[:, rs] + bcol_ref[...]
            ls = _log_sigmoid(gr)
            ls1 = ls.astype(BF16).astype(F32)
            ls2 = (ls - ls1).astype(BF16).astype(F32)
            pieces = jnp.concatenate([ls1, ls2, ls - ls1 - ls2, zero_rows], axis=0).astype(BF16)
            sums = _dot(pieces, triu)
            br = sums[0:SUBLANES] + sums[SUBLANES:2 * SUBLANES] + sums[2 * SUBLANES:3 * SUBLANES]
            ab = jnp.where(sub < M_HEADS, gr - pltpu.roll(br, M_HEADS, axis=0), br)
            ab_ref[:, rs] = ab
            bc_ref[rs, :] = jnp.concatenate([ab, pad_rows], axis=0).T
        return carry

    lax.fori_loop(0, nchunks // GATE_CHUNKS_PER_STEP, gate_body, 0)

    def group_body(cg, carry):
        rows, ab = [], []
        for u_ in range(CHUNKS_PER_STEP):
            r0 = pl.multiple_of((cg * CHUNKS_PER_STEP + u_) * CHUNK, CHUNK)
            rows.append(pl.ds(r0, CHUNK))
            ab.append(ab_ref[:, rows[u_]])
        lanes = [(u_, h) for u_ in range(CHUNKS_PER_STEP) for h in heads]
        a_r = {(u_, h): ab[u_][h:h + 1, :] for u_, h in lanes}
        b_last = {(u_, h): ab[u_][M_HEADS + h:M_HEADS + h + 1, CHUNK - 1:CHUNK] for u_, h in lanes}

        m_prev, a_max, a_dec, s_in = {}, {}, {}, {}
        m_run = [mx_ref[h][0:1, 0:1] for h in heads]
        for k in lanes:
            u_, h = k
            m_prev[k] = m_run[h]
            a_max[k] = jnp.max(a_r[k], axis=-1, keepdims=True)
            m_loc = b_last[k] + a_max[k]
            m_new = jnp.maximum(b_last[k] + m_prev[k], m_loc)
            a_dec[k] = jnp.exp(b_last[k] + m_prev[k] - m_new)
            s_in[k] = jnp.exp(m_loc - m_new)
            m_run[h] = m_new
        for h in heads:
            mx_ref[h] = jnp.broadcast_to(m_run[h], (SUBLANES, LANES))

        q, v_ext, s_qk, kv = {}, {}, {}, {}
        for k in lanes:
            u_, h = k
            rs = rows[u_]
            q[k] = q_ref[rs, h * M_DIM:(h + 1) * M_DIM]
            kt = kt_ref[h * M_DIM:(h + 1) * M_DIM, rs]
            v = min_ref[rs, 2 * mw + h * M_DIM:2 * mw + (h + 1) * M_DIM]
            v_ext[k] = jnp.concatenate([v, ones_half], axis=1)
            s_qk[k] = _dot(q[k], kt.astype(BF16))
            e_r = jnp.exp(a_r[k] - a_max[k])
            kv[k] = _dot((kt * e_r).astype(BF16), v_ext[k])

        thr, qk, inter = {}, {}, {}
        for k in lanes:
            u_, h = k
            a_mat = jnp.where(causal, a_r[k], -jnp.inf)
            mu = jnp.maximum(jnp.max(a_mat, axis=-1, keepdims=True), m_prev[k])
            b_c = bc_ref[rows[u_], M_HEADS + h:M_HEADS + h + 1]
            thr[k] = jnp.broadcast_to(jnp.exp(-(b_c + mu)), (CHUNK, M_DIM))
            mu_b = jnp.broadcast_to(mu, (CHUNK, CHUNK))
            inter[k] = jnp.exp(m_prev[k] - mu_b)
            qk[k] = (s_qk[k] * jnp.exp(a_mat - mu_b)).astype(BF16)

        q_state = {}
        state = [st_ref[h] for h in heads]
        for k in lanes:
            u_, h = k
            q_state[k] = _dot(q[k], state[h].astype(BF16))
            state[h] = a_dec[k] * state[h] + s_in[k] * kv[k]
        for h in heads:
            st_ref[h] = state[h]

        for k in lanes:
            u_, h = k
            hs = slice(h * M_DIM, (h + 1) * M_DIM)
            num = _dot(qk[k], v_ext[k])
            den = jnp.maximum(jnp.abs(num[:, M_DIM:] + inter[k] * q_state[k][:, M_DIM:]), thr[k])
            hh = (num[:, :M_DIM] + inter[k] * q_state[k][:, :M_DIM]) / den
            msq = _dot((hh * hh).astype(BF16), mean_mat)
            hn = hh * lax.rsqrt(msq + EPS) * nw_ref[:, hs]
            og = min_ref[rows[u_], 3 * mw + h * M_DIM:3 * mw + (h + 1) * M_DIM].astype(F32)
            hm_ref[rows[u_], hs] = (_sigmoid(og) * hn).astype(BF16)
        return carry

    lax.fori_loop(0, nchunks // CHUNKS_PER_STEP, group_body, 0)


def _mlstm(m_in, grow, conv_w, conv_b, bcol, nw, batch, seq):
    t = m_in.shape[0]
    tt = min(TILE_MLSTM, seq)
    tpb = seq // tt
    mw = M_HEADS * M_DIM
    cur = lambda b, i: (b * tpb + i, 0)
    const = lambda b, i: (0, 0)
    return pl.pallas_call(
        _mlstm_kernel,
        grid=(batch, tpb),
        in_specs=[
            pl.BlockSpec((tt, 4 * mw), cur),
            pl.BlockSpec((SUBLANES, tt), lambda b, i: (0, b * tpb + i)),
            pl.BlockSpec((CONV_K, 2 * mw), const),
            pl.BlockSpec((1, 2 * mw), const),
            pl.BlockSpec((SUBLANES, LANES), const),
            pl.BlockSpec((1, mw), const),
        ],
        out_specs=pl.BlockSpec((tt, mw), cur),
        out_shape=jax.ShapeDtypeStruct((t, mw), BF16),
        scratch_shapes=[
            pltpu.VMEM((tt + SUBLANES, 2 * mw), F32),
            pltpu.VMEM((tt, mw), BF16),
            pltpu.VMEM((mw, tt), F32),
            pltpu.VMEM((M_HEADS, M_DIM, 2 * M_DIM), F32),
            pltpu.VMEM((M_HEADS, SUBLANES, LANES), F32),
            pltpu.VMEM((SUBLANES, tt), F32),
            pltpu.VMEM((tt, LANES), F32),
        ],
        compiler_params=_params("arbitrary", "arbitrary"),
        name="mlstm",
    )(m_in, grow, conv_w, conv_b, bcol, nw)


def _merge_kernel(o_ref, hm_ref, mg_ref, x_ref, g1_ref, sc_ref, sh_ref, nw_ref,
                  wa_ref, wm_ref, wo_ref, rwh_ref, rwl_ref, rb_ref, tri_ref,
                  x1_ref, pay_ref, route_ref, cnt_ref, carry_ref):
    tm, d = x_ref.shape

    @pl.when(pl.program_id(0) == 0)
    def _():
        carry_ref[...] = jnp.zeros(carry_ref.shape, F32)

    ya = _dot(o_ref[...], wa_ref[...])
    yb = _dot(hm_ref[...], wm_ref[...])
    mg = mg_ref[...]
    merged = mg[:, :d].astype(F32) * ya + mg[:, d:].astype(F32) * yb
    x1 = x_ref[...] + g1_ref[...] * _dot(merged.astype(BF16), wo_ref[...])
    x1_ref[...] = x1

    ms = jnp.mean(x1 * x1, axis=-1, keepdims=True)
    h2 = x1 * lax.rsqrt(ms + EPS) * (nw_ref[...] * (1.0 + sc_ref[...])) + sh_ref[...]
    hi = h2.astype(BF16)
    hif = hi.astype(F32)
    lo = (h2 - hif).astype(BF16)
    r_hi = _dot_nt(rwh_ref[...], hi)
    r_lo = _dot_nt(rwl_ref[...], lo)
    sc_t = _sigmoid(r_hi[:N_EXPERTS] + r_hi[N_EXPERTS:] + r_lo)
    sel_t = sc_t + rb_ref[:, 0:1]

    def row(a, e):
        return a[e:e + 1, :]

    best = None
    gi = jnp.zeros((1, tm), I32)
    for g in range(N_GROUPS):
        r = [row(sel_t, EPG * g + i) for i in range(EPG)]
        gs = None
        for i in range(EPG):
            for j in range(i + 1, EPG):
                pr = r[i] + r[j]
                gs = pr if gs is None else jnp.maximum(gs, pr)
        if best is None:
            best = gs
        else:
            upd = gs > best
            gi = jnp.where(upd, g, gi)
            best = jnp.maximum(best, gs)

    def pick(a, i):
        out = row(a, i)
        for g in range(1, N_GROUPS):
            out = jnp.where(gi == g, row(a, EPG * g + i), out)
        return out

    v = [pick(sel_t, i) for i in range(EPG)]
    s = [pick(sc_t, i) for i in range(EPG)]

    def argmax4(vals):
        bv, bi = vals[0], jnp.zeros((1, tm), I32)
        for i in range(1, EPG):
            upd = vals[i] > bv
            bi = jnp.where(upd, i, bi)
            bv = jnp.maximum(bv, vals[i])
        return bi

    i1 = argmax4(v)
    i2 = argmax4([jnp.where(i1 == i, -jnp.inf, v[i]) for i in range(EPG)])
    ia = jnp.minimum(i1, i2)
    ib = jnp.maximum(i1, i2)
    pidx = jnp.where(ia == 0, ib - 1, jnp.where(ia == 1, jnp.where(ib == 3, 3, 5), 4))
    bucket = gi * N_PAIRS + pidx

    def by_index(vals, idx):
        out = vals[0]
        for i in range(1, EPG):
            out = jnp.where(idx == i, vals[i], out)
        return out

    swap = pidx == N_PAIRS - 1
    s_lo, s_hi = by_index(s, ia), by_index(s, ib)
    s_a, s_b = jnp.where(swap, s_hi, s_lo), jnp.where(swap, s_lo, s_hi)
    gate_a = s_a / (s_a + s_b)
    gate_b = s_b / (s_a + s_b)

    brow = lax.broadcasted_iota(I32, (BUCKET_ROWS, tm), 0)
    onehot = brow == bucket
    cums = _dot(jnp.where(onehot, 1.0, 0.0).astype(BF16), tri_ref[...])
    carry = carry_ref[...]
    rank = jnp.sum(jnp.where(onehot, carry[:, 0:1] + cums, 0.0), axis=0, keepdims=True) - 1.0
    new_carry = carry + cums[:, tm - 1:tm]
    carry_ref[...] = new_carry
    cnt_ref[...] = new_carry

    route_ref[...] = jnp.concatenate(
        [bucket.astype(F32), gate_a, gate_b, rank, jnp.zeros((SUBLANES - 4, tm), F32)], axis=0)

    half = d // 2
    packed = _pack_bf16_pairs(hif, rounded=True)
    for cpart in range(half // LANES):
        pay_ref[cpart] = packed[:, cpart * LANES:(cpart + 1) * LANES]
    gates_t = jnp.concatenate([gate_a, gate_b, jnp.zeros((LANES - 2, tm), F32)], axis=0)
    pay_ref[half // LANES] = lax.bitcast_convert_type(gates_t.T, U32)


def _merge(o_attn, hm, mg, x, g1, sc2, sh2, nw, wa, wm, wo, rwh, rwl, rb, tri, l, seq):
    t, d = x.shape
    tm = tri.shape[0]
    tpb = seq // tm
    row = lambda i: (i, 0)
    bsel = lambda i: (i // tpb, 0, 0)
    wsel = lambda i: (l, 0, 0)
    const = lambda i: (0, 0)
    hw = o_attn.shape[1]
    return pl.pallas_call(
        _merge_kernel,
        grid=(t // tm,),
        in_specs=[
            pl.BlockSpec((tm, hw), row),
            pl.BlockSpec((tm, hw), row),
            pl.BlockSpec((tm, 2 * d), row),
            pl.BlockSpec((tm, d), row),
            pl.BlockSpec((None, 1, d), bsel),
            pl.BlockSpec((None, 1, d), bsel),
            pl.BlockSpec((None, 1, d), bsel),
            pl.BlockSpec((None, 1, d), wsel),
            pl.BlockSpec((None, hw, d), wsel),
            pl.BlockSpec((None, hw, d), wsel),
            pl.BlockSpec((None, d, d), wsel),
            pl.BlockSpec((2 * N_EXPERTS, d), const),
            pl.BlockSpec((N_EXPERTS, d), const),
            pl.BlockSpec((N_EXPERTS, LANES), const),
            pl.BlockSpec((tm, tm), const),
        ],
        out_specs=[
            pl.BlockSpec((tm, d), row),
            pl.BlockSpec((PAY_PARTS, tm, LANES), lambda i: (0, i, 0)),
            pl.BlockSpec((SUBLANES, tm), lambda i: (0, i)),
            pl.BlockSpec((BUCKET_ROWS, LANES), const),
        ],
        out_shape=[
            jax.ShapeDtypeStruct((t, d), F32),
            jax.ShapeDtypeStruct((PAY_PARTS, t, LANES), U32),
            jax.ShapeDtypeStruct((SUBLANES, t), F32),
            jax.ShapeDtypeStruct((BUCKET_ROWS, LANES), F32),
        ],
        scratch_shapes=[pltpu.VMEM((BUCKET_ROWS, LANES), F32)],
        compiler_params=_params("arbitrary"),
        name="merge_router",
    )(o_attn, hm, mg, x, g1, sc2, sh2, nw, wa, wm, wo, rwh, rwl, rb, tri)


def _sc_mesh():
    return plsc.VectorSubcoreMesh(core_axis_name="core", subcore_axis_name="subcore")


def _sc_scatter_rows(rows, dest, n_out):
    n, w = rows.shape

    @pl.kernel(out_type=jax.ShapeDtypeStruct((n_out, w), rows.dtype), mesh=_sc_mesh(), scratch_types=[])
    def scatter(x_hbm, i_hbm, o_hbm):
        def body(x_vmem, i_vmem):
            pltpu.sync_copy(x_vmem, o_hbm.at[i_vmem.at[0]])

        pltpu.emit_pipeline(
            body,
            grid=(n // SC_WINDOW,),
            in_specs=[pl.BlockSpec((SC_WINDOW, w), lambda i: (i, 0)),
                      pl.BlockSpec((1, SC_WINDOW), lambda i: (0, i))],
            out_specs=[],
            core_axis_name=("core", "subcore"),
            dimension_semantics=(pltpu.PARALLEL,),
        )(x_hbm, i_hbm)

    return scatter(rows, dest.reshape(1, n))


def _sc_gather_rows(src, idx):
    n = idx.shape[0]
    w = src.shape[1]

    @pl.kernel(out_type=jax.ShapeDtypeStruct((n, w), src.dtype), mesh=_sc_mesh(), scratch_types=[])
    def gather(x_hbm, i_hbm, o_hbm):
        def body(i_vmem, o_vmem):
            pltpu.sync_copy(x_hbm.at[i_vmem.at[0]], o_vmem)

        pltpu.emit_pipeline(
            body,
            grid=(n // SC_WINDOW,),
            in_specs=[pl.BlockSpec((1, SC_WINDOW), lambda i: (0, i))],
            out_specs=[pl.BlockSpec((SC_WINDOW, w), lambda i: (i, 0))],
            core_axis_name=("core", "subcore"),
            dimension_semantics=(pltpu.PARALLEL,),
        )(i_hbm, o_hbm)

    return gather(src, idx.reshape(1, n))


def _row_index_kernel(ps_ref, route_ref, o_ref, *, n_rows):
    bucket = route_ref[0:1, :].astype(I32)
    start = jnp.zeros(bucket.shape, I32)
    for b in range(N_BUCKETS):
        start = jnp.where(bucket == b, ps_ref[b], start)
    dest = start + route_ref[3:4, :].astype(I32)
    part = lax.broadcasted_iota(I32, o_ref.shape, 0)
    o_ref[...] = part * n_rows + dest


def _row_index(pad_starts, route, n_rows):
    t = route.shape[1]
    tm = min(TILE_ROW_INDEX, t)
    return pl.pallas_call(
        functools.partial(_row_index_kernel, n_rows=n_rows),
        grid=(t // tm,),
        in_specs=[pl.BlockSpec(memory_space=pltpu.SMEM), pl.BlockSpec((SUBLANES, tm), lambda i: (0, i))],
        out_specs=pl.BlockSpec((SUBLANES, tm), lambda i: (0, i)),
        out_shape=jax.ShapeDtypeStruct((SUBLANES, t), I32),
        compiler_params=_params("arbitrary"),
        name="row_index",
    )(pad_starts, route)


def _residual_kernel(x_ref, y_ref, g2_ref, o_ref):
    y = _unpack_bf16_pairs(jnp.concatenate([y_ref[c] for c in range(OUT_PARTS)], axis=1))
    o_ref[...] = x_ref[...] + g2_ref[...] * y


def _residual(x1, ytok, g2, seq):
    t, d = x1.shape
    tm = min(TILE_RESIDUAL, seq)
    tpb = seq // tm
    return pl.pallas_call(
        _residual_kernel,
        grid=(t // tm,),
        in_specs=[
            pl.BlockSpec((tm, d), lambda i: (i, 0)),
            pl.BlockSpec((OUT_PARTS, tm, LANES), lambda i: (0, i, 0)),
            pl.BlockSpec((None, 1, d), lambda i: (i // tpb, 0, 0)),
        ],
        out_specs=pl.BlockSpec((tm, d), lambda i: (i, 0)),
        out_shape=jax.ShapeDtypeStruct((t, d), F32),
        compiler_params=_params("arbitrary"),
        name="residual",
    )(x1, ytok, g2)


def _expert_kernel(ea_ref, eb_ref, nr_ref, xs_ref, wga_ref, wua_ref, wda_ref, wgb_ref, wub_ref, wdb_ref, ys_ref,
                   ga_ref, ua_ref, da_ref, gb_ref, ub_ref, db_ref):
    j = pl.program_id(0)
    nr = nr_ref[0]
    prev = jnp.maximum(j - 1, 0)

    @pl.when((j == 0) | (ea_ref[j] != ea_ref[prev]))
    def _():
        ga_ref[...] = wga_ref[...].astype(BF16)
        ua_ref[...] = wua_ref[...].astype(BF16)
        da_ref[...] = wda_ref[...].astype(BF16)

    @pl.when((j == 0) | (eb_ref[j] != eb_ref[prev]))
    def _():
        gb_ref[...] = wgb_ref[...].astype(BF16)
        ub_ref[...] = wub_ref[...].astype(BF16)
        db_ref[...] = wdb_ref[...].astype(BF16)

    @pl.when(j < nr)
    def _():
        x = _unpack_bf16_pairs(jnp.concatenate([xs_ref[c] for c in range(PAY_PARTS - 1)], axis=1)).astype(BF16)
        gl = lax.bitcast_convert_type(xs_ref[PAY_PARTS - 1], F32)

        def ffn(wg_ref, wu_ref, wd_ref):
            gte = _dot(x, wg_ref[...])
            act = gte * _sigmoid(gte) * _dot(x, wu_ref[...])
            return _dot(act.astype(BF16), wd_ref[...])

        y = _pack_bf16_pairs(gl[:, 0:1] * ffn(ga_ref, ua_ref, da_ref) + gl[:, 1:2] * ffn(gb_ref, ub_ref, db_ref))
        for c in range(OUT_PARTS):
            ys_ref[c] = y[:, c * LANES:(c + 1) * LANES]

    @pl.when(j >= nr)
    def _():
        ys_ref[...] = jnp.zeros(ys_ref.shape, U32)


def _experts(blk_ea, blk_eb, n_real, xs, wg, wu, wd, d):
    n_rows = xs.shape[1]
    nblk = n_rows // EXPERT_BLOCK
    f = wg.shape[2]
    grid_spec = pltpu.PrefetchScalarGridSpec(
        num_scalar_prefetch=3,
        grid=(nblk,),
        in_specs=[
            pl.BlockSpec((PAY_PARTS, EXPERT_BLOCK, LANES), lambda j, ea, eb, nr: (0, j, 0)),
            pl.BlockSpec((None, d, f), lambda j, ea, eb, nr: (ea[j], 0, 0)),
            pl.BlockSpec((None, d, f), lambda j, ea, eb, nr: (ea[j], 0, 0)),
            pl.BlockSpec((None, f, d), lambda j, ea, eb, nr: (ea[j], 0, 0)),
            pl.BlockSpec((None, d, f), lambda j, ea, eb, nr: (eb[j], 0, 0)),
            pl.BlockSpec((None, d, f), lambda j, ea, eb, nr: (eb[j], 0, 0)),
            pl.BlockSpec((None, f, d), lambda j, ea, eb, nr: (eb[j], 0, 0)),
        ],
        out_specs=pl.BlockSpec((OUT_PARTS, EXPERT_BLOCK, LANES), lambda j, ea, eb, nr: (0, j, 0)),
        scratch_shapes=[pltpu.VMEM((d, f), BF16), pltpu.VMEM((d, f), BF16), pltpu.VMEM((f, d), BF16)] * 2,
    )
    return pl.pallas_call(
        _expert_kernel,
        grid_spec=grid_spec,
        out_shape=jax.ShapeDtypeStruct((OUT_PARTS, n_rows, LANES), U32),
        compiler_params=_params("arbitrary"),
        name="experts",
    )(blk_ea, blk_eb, n_real, xs, wg, wu, wd, wg, wu, wd)


_PAIR_A = (0, 0, 0, 1, 2, 2)
_PAIR_B = (1, 2, 3, 3, 3, 1)


def kernel(x, c, positions, ada_w, ada_b, norm_mix_w, norm_ffn_w, w_in, b_igate, b_fgate, q_norm_w, k_norm_w,
           sinks, conv_w, conv_b, mlstm_norm_w, w_attn_up, w_mlstm_up, w_out, router_w, router_bias,
           w_gate, w_up, w_down):
    batch, seq, d = x.shape
    depth = w_in.shape[0]
    t = batch * seq
    qw = N_HEADS * HEAD_DIM
    kvw = N_KV * HEAD_DIM
    mw = M_HEADS * M_DIM

    o = 0
    cols = {}
    for name, wdt in (("q", qw), ("k", kvw), ("v", kvw), ("mqk", 2 * mw), ("mv", mw), ("mi", M_HEADS),
                      ("mf", M_HEADS), ("mo", mw), ("ga", d), ("gb", d)):
        cols[name] = (o, o + wdt)
        o += wdt

    def wc(name, lo=0, hi=None):
        s, e = cols[name]
        return w_in[:, :, s + lo:(s + hi if hi is not None else e)]

    w_a = jnp.concatenate([wc("q"), wc("k"), wc("v")], axis=2).astype(BF16)
    w_m = jnp.concatenate([wc("mqk"), wc("mv"), wc("mo")], axis=2).astype(BF16)
    w_g = jnp.concatenate([wc("mi"), wc("mf"), jnp.zeros((depth, d, LANES - 2 * M_HEADS), F32)], axis=2).astype(BF16)
    w_mg = jnp.concatenate([wc("ga"), wc("gb")], axis=2).astype(BF16)
    w_au = w_attn_up.astype(BF16)
    w_mu = w_mlstm_up.astype(BF16)
    w_o = w_out.astype(BF16)
    n_e = w_gate.shape[1]
    w_g8 = w_gate.reshape(depth * n_e, d, -1)
    w_u8 = w_up.reshape(depth * n_e, d, -1)
    w_d = w_down.reshape(depth * n_e, -1, d)

    rw_t = router_w.astype(F32).T
    rw_top = rw_t.astype(BF16)
    rw_hi = jnp.concatenate([rw_top, (rw_t - rw_top.astype(F32)).astype(BF16)], axis=0)
    rw_lo = rw_top
    rb = jnp.broadcast_to(router_bias.astype(F32)[:, None], (n_e, LANES))

    qn_w = jnp.tile(q_norm_w * (HEAD_DIM ** -0.5), (1, N_HEADS)).reshape(depth, 1, qw)
    kn_w = jnp.tile(k_norm_w, (1, N_KV)).reshape(depth, 1, kvw)
    seg = jnp.arange(qw) // HEAD_DIM
    bdq = jnp.where(seg[:, None] == seg[None, :], 1.0 / HEAD_DIM, 0.0).astype(BF16)
    bdk = bdq[:kvw, :kvw]

    inv_freq = ROPE_THETA ** (-(jnp.arange(0, ROPE_DIM, 2, dtype=F32) / ROPE_DIM))
    ang = positions.astype(F32).reshape(1, t) * inv_freq[:, None]
    cos8, sin8 = jnp.cos(ang).T, jnp.sin(ang).T
    pad1 = jnp.ones((t, HEAD_DIM - ROPE_DIM), F32)
    pad0 = jnp.zeros((t, HEAD_DIM - ROPE_DIM), F32)
    cos_t = jnp.tile(jnp.concatenate([cos8, cos8, pad1], axis=1), (1, LANES // HEAD_DIM))
    sin_t = jnp.tile(jnp.concatenate([-sin8, sin8, pad0], axis=1), (1, LANES // HEAD_DIM))

    gate_bias = jnp.concatenate([b_igate, b_fgate], axis=1).astype(F32)
    bcol = jnp.broadcast_to(gate_bias[:, :, None], (depth, 2 * M_HEADS, LANES))

    tm_merge = min(TILE_MERGE, seq)
    ii = jnp.arange(tm_merge)
    tri = (ii[:, None] <= ii[None, :]).astype(BF16)

    n_blk = (t + N_BUCKETS * (EXPERT_BLOCK - 1)) // EXPERT_BLOCK + 1
    n_rows = n_blk * EXPERT_BLOCK
    pair_a = jnp.asarray(_PAIR_A, I32)
    pair_b = jnp.asarray(_PAIR_B, I32)

    c_pad = jnp.zeros((SUBLANES, d), F32).at[:batch].set(c)
    mod = _ada_mod(c_pad, ada_w, ada_b)[:, :batch]

    xf = x.reshape(t, d)
    moe = None
    for l in range(depth):
        sh1, sc1, g1, sh2, sc2, g2 = [m.reshape(batch, 1, d) for m in jnp.split(mod[l], 6, axis=-1)]

        outs = _inproj(xf, moe, sc1, sh1, norm_mix_w.reshape(depth, 1, d), w_a, w_m, w_g, w_mg, l, seq)
        a_in, m_in, mg, grow = outs[:4]
        if moe is not None:
            xf = outs[4]
        o_attn = _attention(a_in, cos_t, sin_t, sinks[l], qn_w[l], kn_w[l], bdq, bdk, batch, seq)
        hm = _mlstm(m_in, grow, conv_w[l], conv_b[l].reshape(1, -1), bcol[l], mlstm_norm_w[l].reshape(1, mw),
                    batch, seq)
        x1, pay, route, cnt = _merge(o_attn, hm, mg, xf, g1, sc2, sh2, norm_ffn_w.reshape(depth, 1, d),
                                     w_au, w_mu, w_o, rw_hi, rw_lo, rb, tri, l, seq)

        counts = cnt[:N_BUCKETS, 0].astype(I32)
        padded = (counts + EXPERT_BLOCK - 1) // EXPERT_BLOCK * EXPERT_BLOCK
        pad_ends = jnp.cumsum(padded)
        pad_starts = pad_ends - padded
        row_idx = _row_index(jnp.concatenate([pad_starts, jnp.zeros((BUCKET_ROWS - N_BUCKETS,), I32)]), route, n_rows)
        blk_start = jnp.arange(n_blk, dtype=I32) * EXPERT_BLOCK
        blk_bucket = jnp.minimum(jnp.sum((pad_ends[None, :] <= blk_start[:, None]).astype(I32), axis=1), N_BUCKETS - 1)
        grp = blk_bucket // N_PAIRS
        blk_ea = (l * n_e + grp * EPG + pair_a[blk_bucket % N_PAIRS]).astype(I32)
        blk_eb = (l * n_e + grp * EPG + pair_b[blk_bucket % N_PAIRS]).astype(I32)
        n_real = (pad_ends[-1:] // EXPERT_BLOCK).astype(I32)

        xs = _sc_scatter_rows(pay.reshape(PAY_PARTS * t, LANES), row_idx[:PAY_PARTS].reshape(-1),
                              PAY_PARTS * n_rows).reshape(PAY_PARTS, n_rows, LANES)
        ys = _experts(blk_ea, blk_eb, n_real, xs, w_g8, w_u8, w_d, d)
        ytok = _sc_gather_rows(ys.reshape(OUT_PARTS * n_rows, LANES),
                               row_idx[:OUT_PARTS].reshape(-1)).reshape(OUT_PARTS, t, LANES)
        xf, moe = x1, (ytok, g2)
    return _residual(xf, moe[0], moe[1], seq).reshape(batch, seq, d)
```

```python
import functools

import jax
import jax.numpy as jnp
from jax import lax
from jax.experimental import pallas as pl
from jax.experimental.pallas import tpu as pltpu
from jax.experimental.pallas import tpu_sc as plsc

F32 = jnp.float32
BF16 = jnp.bfloat16
U32 = jnp.uint32
I32 = jnp.int32
HIGHEST = lax.Precision.HIGHEST

HEAD_DIM = 64
N_HEADS = 8
N_KV = 2
ROPE_DIM = 16
ROPE_THETA = 500000.0
ATTN_BLOCK = 128
M_HEADS = 4
M_DIM = 128
CONV_K = 4
N_EXPERTS = 16
N_GROUPS = 4
EPG = 4
EPS = 1e-6

LANES = 128
SUBLANES = 8

TILE_INPROJ = 1024
TILE_ATTN = 1024
TILE_MLSTM = 2048
TILE_MERGE = 1024
TILE_RESIDUAL = 1024
TILE_ROW_INDEX = 8192
ADA_COLS = 3072
CHUNK = 128
CHUNKS_PER_STEP = 4
GATE_CHUNKS_PER_STEP = 4
N_PAIRS = 6
N_BUCKETS = N_GROUPS * N_PAIRS
BUCKET_ROWS = 32
EXPERT_BLOCK = 512
PAY_PARTS = 5
OUT_PARTS = 4
SC_WINDOW = 128
VMEM_LIMIT = 56 * 1024 * 1024


def _dot(a, b, precision=None):
    return jnp.dot(a, b, preferred_element_type=F32, precision=precision)


def _dot_nt(a, b):
    return lax.dot_general(a, b, (((1,), (1,)), ((), ())), preferred_element_type=F32)


def _sigmoid(x):
    return 1.0 / (1.0 + jnp.exp(-x))


def _log_sigmoid(x):
    return jnp.minimum(x, 0.0) - jnp.log1p(jnp.exp(-jnp.abs(x)))


def _pack_bf16_pairs(v, rounded=False):
    n = v.shape[1] // 2
    bits = lax.bitcast_convert_type(v if rounded else v.astype(BF16).astype(F32), U32)
    return bits[:, :n] | (bits[:, n:] >> 16)


def _unpack_bf16_pairs(w):
    hi = lax.bitcast_convert_type(w & jnp.uint32(0xFFFF0000), F32)
    lo = lax.bitcast_convert_type(w << 16, F32)
    return jnp.concatenate([hi, lo], axis=1)


def _params(*sem):
    return pltpu.CompilerParams(dimension_semantics=sem, vmem_limit_bytes=VMEM_LIMIT)


def _ada_kernel(c_ref, w_ref, b_ref, o_ref):
    c = c_ref[...]
    ca = c * _sigmoid(c)
    o_ref[0] = _dot(ca, w_ref[0], HIGHEST) + b_ref[0]


def _ada_mod(c_pad, ada_w, ada_b):
    depth, d, n = ada_w.shape
    tn = ADA_COLS
    return pl.pallas_call(
        _ada_kernel,
        grid=(depth, n // tn),
        in_specs=[
            pl.BlockSpec((SUBLANES, d), lambda l, j: (0, 0)),
            pl.BlockSpec((1, d, tn), lambda l, j: (l, 0, j)),
            pl.BlockSpec((1, 1, tn), lambda l, j: (l, 0, j)),
        ],
        out_specs=pl.BlockSpec((1, SUBLANES, tn), lambda l, j: (l, 0, j)),
        out_shape=jax.ShapeDtypeStruct((depth, SUBLANES, n), F32),
        compiler_params=_params("arbitrary", "arbitrary"),
        name="ada_mod",
    )(c_pad, ada_w, ada_b.reshape(depth, 1, n))


def _inproj_kernel(*refs, fuse_residual):
    if fuse_residual:
        (x_ref, y_ref, g2_ref, sc_ref, sh_ref, nw_ref, wa_ref, wm_ref, wg_ref, wmg_ref,
         a_ref, m_ref, mg_ref, gr_ref, xo_ref, g_ref) = refs
        y = _unpack_bf16_pairs(jnp.concatenate([y_ref[c] for c in range(OUT_PARTS)], axis=1))
        x = x_ref[...] + g2_ref[...] * y
        xo_ref[...] = x
    else:
        (x_ref, sc_ref, sh_ref, nw_ref, wa_ref, wm_ref, wg_ref, wmg_ref,
         a_ref, m_ref, mg_ref, gr_ref, g_ref) = refs
        x = x_ref[...]
    ms = jnp.mean(x * x, axis=-1, keepdims=True)
    h = x * lax.rsqrt(ms + EPS) * (nw_ref[...] * (1.0 + sc_ref[...])) + sh_ref[...]
    hb = h.astype(BF16)
    a_ref[...] = _dot(hb, wa_ref[...]).astype(BF16)
    m_ref[...] = _dot(hb, wm_ref[...]).astype(BF16)
    mg_ref[...] = _sigmoid(_dot(hb, wmg_ref[...])).astype(BF16)
    g_ref[...] = _dot(hb, wg_ref[...])
    gr_ref[...] = g_ref[...].T[:SUBLANES, :]


def _inproj(x, moe, sc, sh, nw, wa, wm, wg, wmg, l, seq):
    t, d = x.shape
    tm = min(TILE_INPROJ, seq)
    tpb = seq // tm
    row = lambda i: (i, 0)
    bsel = lambda i: (i // tpb, 0, 0)
    wsel = lambda i: (l, 0, 0)
    once = pl.Buffered(1)
    na, nm, ng, nmg = wa.shape[2], wm.shape[2], wg.shape[2], wmg.shape[2]
    fuse = moe is not None
    moe_specs = [pl.BlockSpec((OUT_PARTS, tm, LANES), lambda i: (0, i, 0)), pl.BlockSpec((None, 1, d), bsel)]
    return pl.pallas_call(
        functools.partial(_inproj_kernel, fuse_residual=fuse),
        grid=(t // tm,),
        in_specs=[pl.BlockSpec((tm, d), row)] + (moe_specs if fuse else []) + [
            pl.BlockSpec((None, 1, d), bsel),
            pl.BlockSpec((None, 1, d), bsel),
            pl.BlockSpec((None, 1, d), wsel),
            pl.BlockSpec((None, d, na), wsel, pipeline_mode=once),
            pl.BlockSpec((None, d, nm), wsel, pipeline_mode=once),
            pl.BlockSpec((None, d, ng), wsel, pipeline_mode=once),
            pl.BlockSpec((None, d, nmg), wsel, pipeline_mode=once),
        ],
        out_specs=[
            pl.BlockSpec((tm, na), row),
            pl.BlockSpec((tm, nm), row),
            pl.BlockSpec((tm, nmg), row),
            pl.BlockSpec((SUBLANES, tm), lambda i: (0, i)),
        ] + ([pl.BlockSpec((tm, d), row)] if fuse else []),
        out_shape=[
            jax.ShapeDtypeStruct((t, na), BF16),
            jax.ShapeDtypeStruct((t, nm), BF16),
            jax.ShapeDtypeStruct((t, nmg), BF16),
            jax.ShapeDtypeStruct((SUBLANES, t), F32),
        ] + ([jax.ShapeDtypeStruct((t, d), F32)] if fuse else []),
        scratch_shapes=[pltpu.VMEM((tm, ng), F32)],
        compiler_params=_params("arbitrary"),
        name="inproj",
    )(x, *(moe if fuse else ()), sc, sh, nw, wa, wm, wg, wmg)


def _rope(t, cos, sin):
    w = t.shape[1]
    reps = w // LANES
    cosw = jnp.concatenate([cos] * reps, axis=1) if reps > 1 else cos
    sinw = jnp.concatenate([sin] * reps, axis=1) if reps > 1 else sin
    lane = lax.broadcasted_iota(I32, t.shape, 1)
    half = ROPE_DIM // 2
    up = pltpu.roll(t, w - half, axis=1)
    dn = pltpu.roll(t, half, axis=1)
    partner = jnp.where((lane % ROPE_DIM) < half, up, dn)
    return t * cosw + partner * sinw


def _head_norm(t, bd, w):
    ms = _dot((t * t).astype(BF16), bd)
    return t * lax.rsqrt(ms + EPS) * w


def _attn_kernel(sink_ref, cur_ref, prev_ref, cos_ref, sin_ref, cosp_ref, sinp_ref,
                 qw_ref, kw_ref, bdq_ref, bdk_ref, o_ref):
    tq = cur_ref.shape[0]
    nj = tq // ATTN_BLOCK
    qw = N_HEADS * HEAD_DIM
    kw = N_KV * HEAD_DIM
    blk0 = pl.program_id(1) * nj

    cur = cur_ref[...]
    q = cur[:, :qw].astype(F32)
    kc = cur[:, qw:qw + kw].astype(F32)
    vc = cur[:, qw + kw:].astype(F32)
    prev = prev_ref[...]
    kp = prev[:, :kw].astype(F32)
    vp = prev[:, kw:].astype(F32)

    cos, sin = cos_ref[...], sin_ref[...]
    q = _rope(_head_norm(q, bdq_ref[...], qw_ref[...]), cos, sin)
    kc = _rope(_head_norm(kc, bdk_ref[...], kw_ref[...]), cos, sin)
    kp = _rope(_head_norm(kp, bdk_ref[...], kw_ref[...]), cosp_ref[...], sinp_ref[...])
    qb = q.astype(BF16)

    def both_halves(x2):
        swapped = pltpu.roll(x2, HEAD_DIM, axis=1)
        first = lax.broadcasted_iota(I32, x2.shape, 1) < HEAD_DIM
        return jnp.concatenate([jnp.where(first, x2, swapped), jnp.where(first, swapped, x2)], axis=1).astype(BF16)

    k_all = both_halves(jnp.concatenate([kp, kc], axis=0))
    v_all = both_halves(jnp.concatenate([vp, vc], axis=0))

    lane = lax.broadcasted_iota(I32, (ATTN_BLOCK, LANES), 1)
    lo = lane < HEAD_DIM
    zero = jnp.zeros((ATTN_BLOCK, LANES), BF16)
    g_heads = N_HEADS // N_KV
    ri = lax.broadcasted_iota(I32, (g_heads * ATTN_BLOCK, ATTN_BLOCK), 0) % ATTN_BLOCK
    ci = lax.broadcasted_iota(I32, (g_heads * ATTN_BLOCK, ATTN_BLOCK), 1)
    from_prev = ci > ri
    head_row = lax.broadcasted_iota(I32, (g_heads * ATTN_BLOCK, 1), 0) // ATTN_BLOCK
    ones_v = jnp.ones((2 * ATTN_BLOCK, LANES), BF16)

    tiles = [(j, g) for j in range(nj) for g in range(N_KV)]
    scores = {}
    for j, g in tiles:
        rows = slice(j * ATTN_BLOCK, (j + 1) * ATTN_BLOCK)
        band = slice(j * ATTN_BLOCK, (j + 2) * ATTN_BLOCK)
        qp0 = qb[rows, (2 * g) * LANES:(2 * g + 1) * LANES]
        qp1 = qb[rows, (2 * g + 1) * LANES:(2 * g + 2) * LANES]
        q4 = jnp.concatenate([jnp.where(lo, qp0, zero), jnp.where(lo, zero, qp0),
                              jnp.where(lo, qp1, zero), jnp.where(lo, zero, qp1)], axis=0)
        scores[j, g] = _dot_nt(q4, k_all[band, g * LANES:(g + 1) * LANES])

    probs, sink_term = {}, {}
    for j, g in tiles:
        s2 = scores[j, g]
        if j == 0:
            prev_ok = ci > ri + (1 - jnp.minimum(blk0, 1)) * ATTN_BLOCK
            s = jnp.where(prev_ok, s2[:, :ATTN_BLOCK], jnp.where(from_prev, -jnp.inf, s2[:, ATTN_BLOCK:]))
        else:
            s = jnp.where(from_prev, s2[:, :ATTN_BLOCK], s2[:, ATTN_BLOCK:])
        sink = jnp.full((g_heads * ATTN_BLOCK, 1), sink_ref[g_heads * g], F32)
        for r in range(1, g_heads):
            sink = jnp.where(head_row == r, sink_ref[g_heads * g + r], sink)
        m = jnp.maximum(jnp.max(s, axis=-1, keepdims=True), sink)
        p = jnp.exp(s - m)
        probs[j, g] = jnp.concatenate([jnp.where(from_prev, p, 0.0), jnp.where(from_prev, 0.0, p)],
                                      axis=1).astype(BF16)
        sink_term[j, g] = jnp.exp(sink - m)

    for j, g in tiles:
        rows = slice(j * ATTN_BLOCK, (j + 1) * ATTN_BLOCK)
        band = slice(j * ATTN_BLOCK, (j + 2) * ATTN_BLOCK)
        o8 = _dot(probs[j, g], jnp.concatenate([v_all[band, g * LANES:(g + 1) * LANES], ones_v], axis=1))
        o4 = o8[:, :LANES] / (o8[:, LANES:] + sink_term[j, g])
        b = ATTN_BLOCK
        o_ref[rows, (2 * g) * LANES:(2 * g + 1) * LANES] = jnp.where(lo, o4[0:b], o4[b:2 * b]).astype(BF16)
        o_ref[rows, (2 * g + 1) * LANES:(2 * g + 2) * LANES] = jnp.where(
            lo, o4[2 * b:3 * b], o4[3 * b:4 * b]).astype(BF16)


def _attention(a_in, cos_t, sin_t, sinks_l, qw, kw, bdq, bdk, batch, seq):
    t = a_in.shape[0]
    tq = min(TILE_ATTN, seq)
    nj = tq // ATTN_BLOCK
    tpb = seq // tq
    bpb = seq // ATTN_BLOCK
    qwid = N_HEADS * HEAD_DIM
    kvw = 2 * N_KV * HEAD_DIM
    cur = lambda b, i: (b * tpb + i, 0)
    prv = lambda b, i: (b * bpb + jnp.maximum(i * nj - 1, 0), qwid // kvw)
    prv0 = lambda b, i: (b * bpb + jnp.maximum(i * nj - 1, 0), 0)
    const = lambda b, i: (0, 0)
    return pl.pallas_call(
        _attn_kernel,
        grid=(batch, tpb),
        in_specs=[
            pl.BlockSpec(memory_space=pltpu.SMEM),
            pl.BlockSpec((tq, qwid + kvw), cur),
            pl.BlockSpec((ATTN_BLOCK, kvw), prv),
            pl.BlockSpec((tq, LANES), cur),
            pl.BlockSpec((tq, LANES), cur),
            pl.BlockSpec((ATTN_BLOCK, LANES), prv0),
            pl.BlockSpec((ATTN_BLOCK, LANES), prv0),
            pl.BlockSpec((1, qwid), const),
            pl.BlockSpec((1, kvw // 2), const),
            pl.BlockSpec((qwid, qwid), const),
            pl.BlockSpec((kvw // 2, kvw // 2), const),
        ],
        out_specs=pl.BlockSpec((tq, qwid), cur),
        out_shape=jax.ShapeDtypeStruct((t, qwid), BF16),
        compiler_params=_params("arbitrary", "arbitrary"),
        name="swa_attention",
    )(sinks_l, a_in, a_in, cos_t, sin_t, cos_t, sin_t, qw, kw, bdq, bdk)


def _mlstm_kernel(min_ref, gr_ref, cw_ref, cb_ref, bcol_ref, nw_ref,
                  hm_ref, ext_ref, q_ref, kt_ref, st_ref, mx_ref, ab_ref, bc_ref):
    tt = min_ref.shape[0]
    mw = M_HEADS * M_DIM
    nchunks = tt // CHUNK

    @pl.when(pl.program_id(1) == 0)
    def _():
        ext_ref[0:SUBLANES, :] = jnp.zeros((SUBLANES, 2 * mw), F32)
        st_ref[...] = jnp.zeros(st_ref.shape, F32)
        mx_ref[...] = jnp.zeros(mx_ref.shape, F32)

    def conv_block(cols):
        u = min_ref[:, cols].astype(F32)
        ext_ref[SUBLANES:SUBLANES + tt, cols] = u
        acc = cb_ref[:, cols] + cw_ref[CONV_K - 1:CONV_K, cols] * u
        for jj in range(CONV_K - 1):
            off = SUBLANES - (CONV_K - 1) + jj
            acc = acc + cw_ref[jj:jj + 1, cols] * ext_ref[off:off + tt, cols]
        ext_ref[0:SUBLANES, cols] = u[tt - SUBLANES:tt, :]
        return acc * _sigmoid(acc)

    def q_body(h, carry):
        cols = pl.ds(pl.multiple_of(h * M_DIM, M_DIM), M_DIM)
        q_ref[:, cols] = conv_block(cols).astype(BF16)
        return carry

    def k_body(h, carry):
        off = pl.multiple_of(h * M_DIM, M_DIM)
        act = conv_block(pl.ds(mw + off, M_DIM)) * (M_DIM ** -0.5)
        for j in range(nchunks):
            kt_ref[pl.ds(off, M_DIM), j * CHUNK:(j + 1) * CHUNK] = act[j * CHUNK:(j + 1) * CHUNK, :].T
        return carry

    lax.fori_loop(0, M_HEADS, q_body, 0)
    lax.fori_loop(0, M_HEADS, k_body, 0)

    ri = lax.broadcasted_iota(I32, (CHUNK, CHUNK), 0)
    ci = lax.broadcasted_iota(I32, (CHUNK, CHUNK), 1)
    causal = ci <= ri
    triu = jnp.where(ri <= ci, 1.0, 0.0).astype(BF16)
    ones_half = jnp.ones((CHUNK, M_DIM), BF16)
    mean_mat = jnp.full((M_DIM, M_DIM), 1.0 / M_DIM, BF16)
    sub = lax.broadcasted_iota(I32, (SUBLANES, CHUNK), 0)
    heads = range(M_HEADS)

    pad_rows = jnp.zeros((CHUNK - SUBLANES, CHUNK), F32)
    zero_rows = jnp.zeros((SUBLANES, CHUNK), F32)

    def gate_body(jg, carry):
        for u_ in range(GATE_CHUNKS_PER_STEP):
            rs = pl.ds(pl.multiple_of((jg * GATE_CHUNKS_PER_STEP + u_) * CHUNK, CHUNK), CHUNK)
            gr = gr_ref[:, rs] + bcol_ref[...]
            ls = _log_sigmoid(gr)
            ls1 = ls.astype(BF16).astype(F32)
            ls2 = (ls - ls1).astype(BF16).astype(F32)
            pieces = jnp.concatenate([ls1, ls2, ls - ls1 - ls2, zero_rows], axis=0).astype(BF16)
            sums = _dot(pieces, triu)
            br = sums[0:SUBLANES] + sums[SUBLANES:2 * SUBLANES] + sums[2 * SUBLANES:3 * SUBLANES]
            ab = jnp.where(sub < M_HEADS, gr - pltpu.roll(br, M_HEADS, axis=0), br)
            ab_ref[:, rs] = ab
            bc_ref[rs, :] = jnp.concatenate([ab, pad_rows], axis=0).T
        return carry

    lax.fori_loop(0, nchunks // GATE_CHUNKS_PER_STEP, gate_body, 0)

    def group_body(cg, carry):
        rows, ab = [], []
        for u_ in range(CHUNKS_PER_STEP):
            r0 = pl.multiple_of((cg * CHUNKS_PER_STEP + u_) * CHUNK, CHUNK)
            rows.append(pl.ds(r0, CHUNK))
            ab.append(ab_ref[:, rows[u_]])
        lanes = [(u_, h) for u_ in range(CHUNKS_PER_STEP) for h in heads]
        a_r = {(u_, h): ab[u_][h:h + 1, :] for u_, h in lanes}
        b_last = {(u_, h): ab[u_][M_HEADS + h:M_HEADS + h + 1, CHUNK - 1:CHUNK] for u_, h in lanes}

        m_prev, a_max, a_dec, s_in = {}, {}, {}, {}
        m_run = [mx_ref[h][0:1, 0:1] for h in heads]
        for k in lanes:
            u_, h = k
            m_prev[k] = m_run[h]
            a_max[k] = jnp.max(a_r[k], axis=-1, keepdims=True)
            m_loc = b_last[k] + a_max[k]
            m_new = jnp.maximum(b_last[k] + m_prev[k], m_loc)
            a_dec[k] = jnp.exp(b_last[k] + m_prev[k] - m_new)
            s_in[k] = jnp.exp(m_loc - m_new)
            m_run[h] = m_new
        for h in heads:
            mx_ref[h] = jnp.broadcast_to(m_run[h], (SUBLANES, LANES))

        q, v_ext, s_qk, kv = {}, {}, {}, {}
        for k in lanes:
            u_, h = k
            rs = rows[u_]
            q[k] = q_ref[rs, h * M_DIM:(h + 1) * M_DIM]
            kt = kt_ref[h * M_DIM:(h + 1) * M_DIM, rs]
            v = min_ref[rs, 2 * mw + h * M_DIM:2 * mw + (h + 1) * M_DIM]
            v_ext[k] = jnp.concatenate([v, ones_half], axis=1)
            s_qk[k] = _dot(q[k], kt.astype(BF16))
            e_r = jnp.exp(a_r[k] - a_max[k])
            kv[k] = _dot((kt * e_r).astype(BF16), v_ext[k])

        thr, qk = {}, {}
        for k in lanes:
            u_, h = k
            a_mat = jnp.where(causal, a_r[k], -jnp.inf)
            mu = jnp.maximum(jnp.max(a_mat, axis=-1, keepdims=True), m_prev[k])
            b_c = bc_ref[rows[u_], M_HEADS + h:M_HEADS + h + 1]
            thr[k] = jnp.broadcast_to(jnp.exp(-(b_c + mu)), (CHUNK, M_DIM))
            mu_b = jnp.broadcast_to(mu, (CHUNK, CHUNK))
            inter = jnp.exp(m_prev[k] - mu_b)
            qk[k] = jnp.concatenate([(s_qk[k] * jnp.exp(a_mat - mu_b)).astype(BF16),
                                     (q[k].astype(F32) * inter).astype(BF16)], axis=1)

        state = [st_ref[h] for h in heads]
        for k in lanes:
            u_, h = k
            hs = slice(h * M_DIM, (h + 1) * M_DIM)
            num = _dot(qk[k], jnp.concatenate([v_ext[k], state[h].astype(BF16)], axis=0))
            state[h] = a_dec[k] * state[h] + s_in[k] * kv[k]
            den = jnp.maximum(jnp.abs(num[:, M_DIM:]), thr[k])
            hh = num[:, :M_DIM] / den
            msq = _dot((hh * hh).astype(BF16), mean_mat)
            hn = hh * lax.rsqrt(msq + EPS) * nw_ref[:, hs]
            og = min_ref[rows[u_], 3 * mw + h * M_DIM:3 * mw + (h + 1) * M_DIM].astype(F32)
            hm_ref[rows[u_], hs] = (_sigmoid(og) * hn).astype(BF16)
        for h in heads:
            st_ref[h] = state[h]
        return carry

    lax.fori_loop(0, nchunks // CHUNKS_PER_STEP, group_body, 0)


def _mlstm(m_in, grow, conv_w, conv_b, bcol, nw, batch, seq):
    t = m_in.shape[0]
    tt = min(TILE_MLSTM, seq)
    tpb = seq // tt
    mw = M_HEADS * M_DIM
    cur = lambda b, i: (b * tpb + i, 0)
    const = lambda b, i: (0, 0)
    return pl.pallas_call(
        _mlstm_kernel,
        grid=(batch, tpb),
        in_specs=[
            pl.BlockSpec((tt, 4 * mw), cur),
            pl.BlockSpec((SUBLANES, tt), lambda b, i: (0, b * tpb + i)),
            pl.BlockSpec((CONV_K, 2 * mw), const),
            pl.BlockSpec((1, 2 * mw), const),
            pl.BlockSpec((SUBLANES, LANES), const),
            pl.BlockSpec((1, mw), const),
        ],
        out_specs=pl.BlockSpec((tt, mw), cur),
        out_shape=jax.ShapeDtypeStruct((t, mw), BF16),
        scratch_shapes=[
            pltpu.VMEM((tt + SUBLANES, 2 * mw), F32),
            pltpu.VMEM((tt, mw), BF16),
            pltpu.VMEM((mw, tt), F32),
            pltpu.VMEM((M_HEADS, M_DIM, 2 * M_DIM), F32),
            pltpu.VMEM((M_HEADS, SUBLANES, LANES), F32),
            pltpu.VMEM((SUBLANES, tt), F32),
            pltpu.VMEM((tt, LANES), F32),
        ],
        compiler_params=_params("arbitrary", "arbitrary"),
        name="mlstm",
    )(m_in, grow, conv_w, conv_b, bcol, nw)


def _merge_kernel(o_ref, hm_ref, mg_ref, x_ref, g1_ref, sc_ref, sh_ref, nw_ref,
                  wa_ref, wm_ref, wo_ref, rwh_ref, rwl_ref, rb_ref, tri_ref,
                  x1_ref, pay_ref, route_ref, cnt_ref, carry_ref):
    tm, d = x_ref.shape

    @pl.when(pl.program_id(0) == 0)
    def _():
        carry_ref[...] = jnp.zeros(carry_ref.shape, F32)

    ya = _dot(o_ref[...], wa_ref[...])
    yb = _dot(hm_ref[...], wm_ref[...])
    mg = mg_ref[...]
    merged = mg[:, :d].astype(F32) * ya + mg[:, d:].astype(F32) * yb
    x1 = x_ref[...] + g1_ref[...] * _dot(merged.astype(BF16), wo_ref[...])
    x1_ref[...] = x1

    ms = jnp.mean(x1 * x1, axis=-1, keepdims=True)
    h2 = x1 * lax.rsqrt(ms + EPS) * (nw_ref[...] * (1.0 + sc_ref[...])) + sh_ref[...]
    hi = h2.astype(BF16)
    hif = hi.astype(F32)
    lo = (h2 - hif).astype(BF16)
    r_hi = _dot_nt(rwh_ref[...], hi)
    r_lo = _dot_nt(rwl_ref[...], lo)
    sc_t = _sigmoid(r_hi[:N_EXPERTS] + r_hi[N_EXPERTS:] + r_lo)
    sel_t = sc_t + rb_ref[:, 0:1]

    def row(a, e):
        return a[e:e + 1, :]

    best = None
    gi = jnp.zeros((1, tm), I32)
    for g in range(N_GROUPS):
        r = [row(sel_t, EPG * g + i) for i in range(EPG)]
        gs = None
        for i in range(EPG):
            for j in range(i + 1, EPG):
                pr = r[i] + r[j]
                gs = pr if gs is None else jnp.maximum(gs, pr)
        if best is None:
            best = gs
        else:
            upd = gs > best
            gi = jnp.where(upd, g, gi)
            best = jnp.maximum(best, gs)

    def pick(a, i):
        out = row(a, i)
        for g in range(1, N_GROUPS):
            out = jnp.where(gi == g, row(a, EPG * g + i), out)
        return out

    v = [pick(sel_t, i) for i in range(EPG)]
    s = [pick(sc_t, i) for i in range(EPG)]

    def argmax4(vals):
        bv, bi = vals[0], jnp.zeros((1, tm), I32)
        for i in range(1, EPG):
            upd = vals[i] > bv
            bi = jnp.where(upd, i, bi)
            bv = jnp.maximum(bv, vals[i])
        return bi

    i1 = argmax4(v)
    i2 = argmax4([jnp.where(i1 == i, -jnp.inf, v[i]) for i in range(EPG)])
    ia = jnp.minimum(i1, i2)
    ib = jnp.maximum(i1, i2)
    pidx = jnp.where(ia == 0, ib - 1, jnp.where(ia == 1, jnp.where(ib == 3, 3, 5), 4))
    bucket = gi * N_PAIRS + pidx

    def by_index(vals, idx):
        out = vals[0]
        for i in range(1, EPG):
            out = jnp.where(idx == i, vals[i], out)
        return out

    swap = pidx == N_PAIRS - 1
    s_lo, s_hi = by_index(s, ia), by_index(s, ib)
    s_a, s_b = jnp.where(swap, s_hi, s_lo), jnp.where(swap, s_lo, s_hi)
    gate_a = s_a / (s_a + s_b)
    gate_b = s_b / (s_a + s_b)

    brow = lax.broadcasted_iota(I32, (BUCKET_ROWS, tm), 0)
    onehot = brow == bucket
    cums = _dot(jnp.where(onehot, 1.0, 0.0).astype(BF16), tri_ref[...])
    carry = carry_ref[...]
    rank = jnp.sum(jnp.where(onehot, carry[:, 0:1] + cums, 0.0), axis=0, keepdims=True) - 1.0
    new_carry = carry + cums[:, tm - 1:tm]
    carry_ref[...] = new_carry
    cnt_ref[...] = new_carry

    route_ref[...] = jnp.concatenate(
        [bucket.astype(F32), gate_a, gate_b, rank, jnp.zeros((SUBLANES - 4, tm), F32)], axis=0)

    half = d // 2
    packed = _pack_bf16_pairs(hif, rounded=True)
    for cpart in range(half // LANES):
        pay_ref[cpart] = packed[:, cpart * LANES:(cpart + 1) * LANES]
    gates_t = jnp.concatenate([gate_a, gate_b, jnp.zeros((LANES - 2, tm), F32)], axis=0)
    pay_ref[half // LANES] = lax.bitcast_convert_type(gates_t.T, U32)


def _merge(o_attn, hm, mg, x, g1, sc2, sh2, nw, wa, wm, wo, rwh, rwl, rb, tri, l, seq):
    t, d = x.shape
    tm = tri.shape[0]
    tpb = seq // tm
    row = lambda i: (i, 0)
    bsel = lambda i: (i // tpb, 0, 0)
    wsel = lambda i: (l, 0, 0)
    const = lambda i: (0, 0)
    hw = o_attn.shape[1]
    return pl.pallas_call(
        _merge_kernel,
        grid=(t // tm,),
        in_specs=[
            pl.BlockSpec((tm, hw), row),
            pl.BlockSpec((tm, hw), row),
            pl.BlockSpec((tm, 2 * d), row),
            pl.BlockSpec((tm, d), row),
            pl.BlockSpec((None, 1, d), bsel),
            pl.BlockSpec((None, 1, d), bsel),
            pl.BlockSpec((None, 1, d), bsel),
            pl.BlockSpec((None, 1, d), wsel),
            pl.BlockSpec((None, hw, d), wsel),
            pl.BlockSpec((None, hw, d), wsel),
            pl.BlockSpec((None, d, d), wsel),
            pl.BlockSpec((2 * N_EXPERTS, d), const),
            pl.BlockSpec((N_EXPERTS, d), const),
            pl.BlockSpec((N_EXPERTS, LANES), const),
            pl.BlockSpec((tm, tm), const),
        ],
        out_specs=[
            pl.BlockSpec((tm, d), row),
            pl.BlockSpec((PAY_PARTS, tm, LANES), lambda i: (0, i, 0)),
            pl.BlockSpec((SUBLANES, tm), lambda i: (0, i)),
            pl.BlockSpec((BUCKET_ROWS, LANES), const),
        ],
        out_shape=[
            jax.ShapeDtypeStruct((t, d), F32),
            jax.ShapeDtypeStruct((PAY_PARTS, t, LANES), U32),
            jax.ShapeDtypeStruct((SUBLANES, t), F32),
            jax.ShapeDtypeStruct((BUCKET_ROWS, LANES), F32),
        ],
        scratch_shapes=[pltpu.VMEM((BUCKET_ROWS, LANES), F32)],
        compiler_params=_params("arbitrary"),
        name="merge_router",
    )(o_attn, hm, mg, x, g1, sc2, sh2, nw, wa, wm, wo, rwh, rwl, rb, tri)


def _sc_mesh():
    return plsc.VectorSubcoreMesh(core_axis_name="core", subcore_axis_name="subcore")


def _sc_scatter_rows(rows, dest, n_out):
    n, w = rows.shape

    @pl.kernel(out_type=jax.ShapeDtypeStruct((n_out, w), rows.dtype), mesh=_sc_mesh(), scratch_types=[])
    def scatter(x_hbm, i_hbm, o_hbm):
        def body(x_vmem, i_vmem):
            pltpu.sync_copy(x_vmem, o_hbm.at[i_vmem.at[0]])

        pltpu.emit_pipeline(
            body,
            grid=(n // SC_WINDOW,),
            in_specs=[pl.BlockSpec((SC_WINDOW, w), lambda i: (i, 0)),
                      pl.BlockSpec((1, SC_WINDOW), lambda i: (0, i))],
            out_specs=[],
            core_axis_name=("core", "subcore"),
            dimension_semantics=(pltpu.PARALLEL,),
        )(x_hbm, i_hbm)

    return scatter(rows, dest.reshape(1, n))


def _sc_gather_rows(src, idx):
    n = idx.shape[0]
    w = src.shape[1]

    @pl.kernel(out_type=jax.ShapeDtypeStruct((n, w), src.dtype), mesh=_sc_mesh(), scratch_types=[])
    def gather(x_hbm, i_hbm, o_hbm):
        def body(i_vmem, o_vmem):
            pltpu.sync_copy(x_hbm.at[i_vmem.at[0]], o_vmem)

        pltpu.emit_pipeline(
            body,
            grid=(n // SC_WINDOW,),
            in_specs=[pl.BlockSpec((1, SC_WINDOW), lambda i: (0, i))],
            out_specs=[pl.BlockSpec((SC_WINDOW, w), lambda i: (i, 0))],
            core_axis_name=("core", "subcore"),
            dimension_semantics=(pltpu.PARALLEL,),
        )(i_hbm, o_hbm)

    return gather(src, idx.reshape(1, n))


def _row_index_kernel(ps_ref, route_ref, o_ref, *, n_rows):
    bucket = route_ref[0:1, :].astype(I32)
    start = jnp.zeros(bucket.shape, I32)
    for b in range(N_BUCKETS):
        start = jnp.where(bucket == b, ps_ref[b], start)
    dest = start + route_ref[3:4, :].astype(I32)
    part = lax.broadcasted_iota(I32, o_ref.shape, 0)
    o_ref[...] = part * n_rows + dest


def _row_index(pad_starts, route, n_rows):
    t = route.shape[1]
    tm = min(TILE_ROW_INDEX, t)
    return pl.pallas_call(
        functools.partial(_row_index_kernel, n_rows=n_rows),
        grid=(t // tm,),
        in_specs=[pl.BlockSpec(memory_space=pltpu.SMEM), pl.BlockSpec((SUBLANES, tm), lambda i: (0, i))],
        out_specs=pl.BlockSpec((SUBLANES, tm), lambda i: (0, i)),
        out_shape=jax.ShapeDtypeStruct((SUBLANES, t), I32),
        compiler_params=_params("arbitrary"),
        name="row_index",
    )(pad_starts, route)


def _residual_kernel(x_ref, y_ref, g2_ref, o_ref):
    y = _unpack_bf16_pairs(jnp.concatenate([y_ref[c] for c in range(OUT_PARTS)], axis=1))
    o_ref[...] = x_ref[...] + g2_ref[...] * y


def _residual(x1, ytok, g2, seq):
    t, d = x1.shape
    tm = min(TILE_RESIDUAL, seq)
    tpb = seq // tm
    return pl.pallas_call(
        _residual_kernel,
        grid=(t // tm,),
        in_specs=[
            pl.BlockSpec((tm, d), lambda i: (i, 0)),
            pl.BlockSpec((OUT_PARTS, tm, LANES), lambda i: (0, i, 0)),
            pl.BlockSpec((None, 1, d), lambda i: (i // tpb, 0, 0)),
        ],
        out_specs=pl.BlockSpec((tm, d), lambda i: (i, 0)),
        out_shape=jax.ShapeDtypeStruct((t, d), F32),
        compiler_params=_params("arbitrary"),
        name="residual",
    )(x1, ytok, g2)


def _expert_kernel(ea_ref, eb_ref, nr_ref, xs_ref, wga_ref, wua_ref, wda_ref, wgb_ref, wub_ref, wdb_ref, ys_ref,
                   ga_ref, ua_ref, da_ref, gb_ref, ub_ref, db_ref):
    j = pl.program_id(0)
    nr = nr_ref[0]
    prev = jnp.maximum(j - 1, 0)

    @pl.when((j == 0) | (ea_ref[j] != ea_ref[prev]))
    def _():
        ga_ref[...] = wga_ref[...].astype(BF16)
        ua_ref[...] = wua_ref[...].astype(BF16)
        da_ref[...] = wda_ref[...].astype(BF16)

    @pl.when((j == 0) | (eb_ref[j] != eb_ref[prev]))
    def _():
        gb_ref[...] = wgb_ref[...].astype(BF16)
        ub_ref[...] = wub_ref[...].astype(BF16)
        db_ref[...] = wdb_ref[...].astype(BF16)

    @pl.when(j < nr)
    def _():
        x = _unpack_bf16_pairs(jnp.concatenate([xs_ref[c] for c in range(PAY_PARTS - 1)], axis=1)).astype(BF16)
        gl = lax.bitcast_convert_type(xs_ref[PAY_PARTS - 1], F32)

        def ffn(wg_ref, wu_ref, wd_ref):
            gte = _dot(x, wg_ref[...])
            act = gte * _sigmoid(gte) * _dot(x, wu_ref[...])
            return _dot(act.astype(BF16), wd_ref[...])

        y = _pack_bf16_pairs(gl[:, 0:1] * ffn(ga_ref, ua_ref, da_ref) + gl[:, 1:2] * ffn(gb_ref, ub_ref, db_ref))
        for c in range(OUT_PARTS):
            ys_ref[c] = y[:, c * LANES:(c + 1) * LANES]

    @pl.when(j >= nr)
    def _():
        ys_ref[...] = jnp.zeros(ys_ref.shape, U32)


def _experts(blk_ea, blk_eb, n_real, xs, wg, wu, wd, d):
    n_rows = xs.shape[1]
    nblk = n_rows // EXPERT_BLOCK
    f = wg.shape[2]
    grid_spec = pltpu.PrefetchScalarGridSpec(
        num_scalar_prefetch=3,
        grid=(nblk,),
        in_specs=[
            pl.BlockSpec((PAY_PARTS, EXPERT_BLOCK, LANES), lambda j, ea, eb, nr: (0, j, 0)),
            pl.BlockSpec((None, d, f), lambda j, ea, eb, nr: (ea[j], 0, 0)),
            pl.BlockSpec((None, d, f), lambda j, ea, eb, nr: (ea[j], 0, 0)),
            pl.BlockSpec((None, f, d), lambda j, ea, eb, nr: (ea[j], 0, 0)),
            pl.BlockSpec((None, d, f), lambda j, ea, eb, nr: (eb[j], 0, 0)),
            pl.BlockSpec((None, d, f), lambda j, ea, eb, nr: (eb[j], 0, 0)),
            pl.BlockSpec((None, f, d), lambda j, ea, eb, nr: (eb[j], 0, 0)),
        ],
        out_specs=pl.BlockSpec((OUT_PARTS, EXPERT_BLOCK, LANES), lambda j, ea, eb, nr: (0, j, 0)),
        scratch_shapes=[pltpu.VMEM((d, f), BF16), pltpu.VMEM((d, f), BF16), pltpu.VMEM((f, d), BF16)] * 2,
    )
    return pl.pallas_call(
        _expert_kernel,
        grid_spec=grid_spec,
        out_shape=jax.ShapeDtypeStruct((OUT_PARTS, n_rows, LANES), U32),
        compiler_params=_params("arbitrary"),
        name="experts",
    )(blk_ea, blk_eb, n_real, xs, wg, wu, wd, wg, wu, wd)


_PAIR_A = (0, 0, 0, 1, 2, 2)
_PAIR_B = (1, 2, 3, 3, 3, 1)


def kernel(x, c, positions, ada_w, ada_b, norm_mix_w, norm_ffn_w, w_in, b_igate, b_fgate, q_norm_w, k_norm_w,
           sinks, conv_w, conv_b, mlstm_norm_w, w_attn_up, w_mlstm_up, w_out, router_w, router_bias,
           w_gate, w_up, w_down):
    batch, seq, d = x.shape
    depth = w_in.shape[0]
    t = batch * seq
    qw = N_HEADS * HEAD_DIM
    kvw = N_KV * HEAD_DIM
    mw = M_HEADS * M_DIM

    o = 0
    cols = {}
    for name, wdt in (("q", qw), ("k", kvw), ("v", kvw), ("mqk", 2 * mw), ("mv", mw), ("mi", M_HEADS),
                      ("mf", M_HEADS), ("mo", mw), ("ga", d), ("gb", d)):
        cols[name] = (o, o + wdt)
        o += wdt

    def wc(name, lo=0, hi=None):
        s, e = cols[name]
        return w_in[:, :, s + lo:(s + hi if hi is not None else e)]

    w_a = jnp.concatenate([wc("q"), wc("k"), wc("v")], axis=2).astype(BF16)
    w_m = jnp.concatenate([wc("mqk"), wc("mv"), wc("mo")], axis=2).astype(BF16)
    w_g = jnp.concatenate([wc("mi"), wc("mf"), jnp.zeros((depth, d, LANES - 2 * M_HEADS), F32)], axis=2).astype(BF16)
    w_mg = jnp.concatenate([wc("ga"), wc("gb")], axis=2).astype(BF16)
    w_au = w_attn_up.astype(BF16)
    w_mu = w_mlstm_up.astype(BF16)
    w_o = w_out.astype(BF16)
    n_e = w_gate.shape[1]
    w_g8 = w_gate.reshape(depth * n_e, d, -1)
    w_u8 = w_up.reshape(depth * n_e, d, -1)
    w_d = w_down.reshape(depth * n_e, -1, d)

    rw_t = router_w.astype(F32).T
    rw_top = rw_t.astype(BF16)
    rw_hi = jnp.concatenate([rw_top, (rw_t - rw_top.astype(F32)).astype(BF16)], axis=0)
    rw_lo = rw_top
    rb = jnp.broadcast_to(router_bias.astype(F32)[:, None], (n_e, LANES))

    qn_w = jnp.tile(q_norm_w * (HEAD_DIM ** -0.5), (1, N_HEADS)).reshape(depth, 1, qw)
    kn_w = jnp.tile(k_norm_w, (1, N_KV)).reshape(depth, 1, kvw)
    seg = jnp.arange(qw) // HEAD_DIM
    bdq = jnp.where(seg[:, None] == seg[None, :], 1.0 / HEAD_DIM, 0.0).astype(BF16)
    bdk = bdq[:kvw, :kvw]

    inv_freq = ROPE_THETA ** (-(jnp.arange(0, ROPE_DIM, 2, dtype=F32) / ROPE_DIM))
    ang = positions.astype(F32).reshape(1, t) * inv_freq[:, None]
    cos8, sin8 = jnp.cos(ang).T, jnp.sin(ang).T
    pad1 = jnp.ones((t, HEAD_DIM - ROPE_DIM), F32)
    pad0 = jnp.zeros((t, HEAD_DIM - ROPE_DIM), F32)
    cos_t = jnp.tile(jnp.concatenate([cos8, cos8, pad1], axis=1), (1, LANES // HEAD_DIM))
    sin_t = jnp.tile(jnp.concatenate([-sin8, sin8, pad0], axis=1), (1, LANES // HEAD_DIM))

    gate_bias = jnp.concatenate([b_igate, b_fgate], axis=1).astype(F32)
    bcol = jnp.broadcast_to(gate_bias[:, :, None], (depth, 2 * M_HEADS, LANES))

    tm_merge = min(TILE_MERGE, seq)
    ii = jnp.arange(tm_merge)
    tri = (ii[:, None] <= ii[None, :]).astype(BF16)

    n_blk = (t + N_BUCKETS * (EXPERT_BLOCK - 1)) // EXPERT_BLOCK + 1
    n_rows = n_blk * EXPERT_BLOCK
    pair_a = jnp.asarray(_PAIR_A, I32)
    pair_b = jnp.asarray(_PAIR_B, I32)

    c_pad = jnp.zeros((SUBLANES, d), F32).at[:batch].set(c)
    mod = _ada_mod(c_pad, ada_w, ada_b)[:, :batch]

    xf = x.reshape(t, d)
    moe = None
    for l in range(depth):
        sh1, sc1, g1, sh2, sc2, g2 = [m.reshape(batch, 1, d) for m in jnp.split(mod[l], 6, axis=-1)]

        outs = _inproj(xf, moe, sc1, sh1, norm_mix_w.reshape(depth, 1, d), w_a, w_m, w_g, w_mg, l, seq)
        a_in, m_in, mg, grow = outs[:4]
        if moe is not None:
            xf = outs[4]
        o_attn = _attention(a_in, cos_t, sin_t, sinks[l], qn_w[l], kn_w[l], bdq, bdk, batch, seq)
        hm = _mlstm(m_in, grow, conv_w[l], conv_b[l].reshape(1, -1), bcol[l], mlstm_norm_w[l].reshape(1, mw),
                    batch, seq)
        x1, pay, route, cnt = _merge(o_attn, hm, mg, xf, g1, sc2, sh2, norm_ffn_w.reshape(depth, 1, d),
                                     w_au, w_mu, w_o, rw_hi, rw_lo, rb, tri, l, seq)

        counts = cnt[:N_BUCKETS, 0].astype(I32)
        padded = (counts + EXPERT_BLOCK - 1) // EXPERT_BLOCK * EXPERT_BLOCK
        pad_ends = jnp.cumsum(padded)
        pad_starts = pad_ends - padded
        row_idx = _row_index(jnp.concatenate([pad_starts, jnp.zeros((BUCKET_ROWS - N_BUCKETS,), I32)]), route, n_rows)
        blk_start = jnp.arange(n_blk, dtype=I32) * EXPERT_BLOCK
        blk_bucket = jnp.minimum(jnp.sum((pad_ends[None, :] <= blk_start[:, None]).astype(I32), axis=1), N_BUCKETS - 1)
        grp = blk_bucket // N_PAIRS
        blk_ea = (l * n_e + grp * EPG + pair_a[blk_bucket % N_PAIRS]).astype(I32)
        blk_eb = (l * n_e + grp * EPG + pair_b[blk_bucket % N_PAIRS]).astype(I32)
        n_real = (pad_ends[-1:] // EXPERT_BLOCK).astype(I32)

        xs = _sc_scatter_rows(pay.reshape(PAY_PARTS * t, LANES), row_idx[:PAY_PARTS].reshape(-1),
                              PAY_PARTS * n_rows).reshape(PAY_PARTS, n_rows, LANES)
        ys = _experts(blk_ea, blk_eb, n_real, xs, w_g8, w_u8, w_d, d)
        ytok = _sc_gather_rows(ys.reshape(OUT_PARTS * n_rows, LANES),
                               row_idx[:OUT_PARTS].reshape(-1)).reshape(OUT_PARTS, t, LANES)
        xf, moe = x1, (ytok, g2)
    return _residual(xf, moe[0], moe[1], seq).reshape(batch, seq, d)
```

```python
import functools

import jax
import jax.numpy as jnp
from jax import lax
from jax.experimental import pallas as pl
from jax.experimental.pallas import tpu as pltpu
from jax.experimental.pallas import tpu_sc as plsc

F32 = jnp.float32
BF16 = jnp.bfloat16
U32 = jnp.uint32
I32 = jnp.int32
HIGHEST = lax.Precision.HIGHEST

HEAD_DIM = 64
N_HEADS = 8
N_KV = 2
ROPE_DIM = 16
ROPE_THETA = 500000.0
ATTN_BLOCK = 128
M_HEADS = 4
M_DIM = 128
CONV_K = 4
N_EXPERTS = 16
N_GROUPS = 4
EPG = 4
EPS = 1e-6

LANES = 128
SUBLANES = 8

TILE_INPROJ = 1024
TILE_ATTN = 1024
TILE_MLSTM = 2048
TILE_MERGE = 1024
TILE_RESIDUAL = 1024
TILE_ROW_INDEX = 8192
ADA_COLS = 3072
CHUNK = 128
CHUNKS_PER_STEP = 4
GATE_CHUNKS_PER_STEP = 4
N_PAIRS = 6
N_BUCKETS = N_GROUPS * N_PAIRS
BUCKET_ROWS = 32
EXPERT_BLOCK = 512
PAY_PARTS = 5
OUT_PARTS = 4
SC_WINDOW = 128
VMEM_LIMIT = 56 * 1024 * 1024


def _dot(a, b, precision=None):
    return jnp.dot(a, b, preferred_element_type=F32, precision=precision)


def _dot_nt(a, b):
    return lax.dot_general(a, b, (((1,), (1,)), ((), ())), preferred_element_type=F32)


def _sigmoid(x):
    return 1.0 / (1.0 + jnp.exp(-x))


def _log_sigmoid(x):
    return jnp.minimum(x, 0.0) - jnp.log1p(jnp.exp(-jnp.abs(x)))


def _pack_bf16_pairs(v, rounded=False):
    n = v.shape[1] // 2
    bits = lax.bitcast_convert_type(v if rounded else v.astype(BF16).astype(F32), U32)
    return bits[:, :n] | (bits[:, n:] >> 16)


def _unpack_bf16_pairs(w):
    hi = lax.bitcast_convert_type(w & jnp.uint32(0xFFFF0000), F32)
    lo = lax.bitcast_convert_type(w << 16, F32)
    return jnp.concatenate([hi, lo], axis=1)


def _params(*sem):
    return pltpu.CompilerParams(dimension_semantics=sem, vmem_limit_bytes=VMEM_LIMIT)


def _ada_kernel(c_ref, w_ref, b_ref, o_ref):
    c = c_ref[...]
    ca = c * _sigmoid(c)
    o_ref[0] = _dot(ca, w_ref[0], HIGHEST) + b_ref[0]


def _ada_mod(c_pad, ada_w, ada_b):
    depth, d, n = ada_w.shape
    tn = ADA_COLS
    return pl.pallas_call(
        _ada_kernel,
        grid=(depth, n // tn),
        in_specs=[
            pl.BlockSpec((SUBLANES, d), lambda l, j: (0, 0)),
            pl.BlockSpec((1, d, tn), lambda l, j: (l, 0, j)),
            pl.BlockSpec((1, 1, tn), lambda l, j: (l, 0, j)),
        ],
        out_specs=pl.BlockSpec((1, SUBLANES, tn), lambda l, j: (l, 0, j)),
        out_shape=jax.ShapeDtypeStruct((depth, SUBLANES, n), F32),
        compiler_params=_params("arbitrary", "arbitrary"),
        name="ada_mod",
    )(c_pad, ada_w, ada_b.reshape(depth, 1, n))


def _inproj_kernel(*refs, fuse_residual):
    if fuse_residual:
        (x_ref, y_ref, g2_ref, sc_ref, sh_ref, nw_ref, wa_ref, wm_ref, wg_ref, wmg_ref,
         a_ref, m_ref, mg_ref, gr_ref, xo_ref, g_ref) = refs
        y = _unpack_bf16_pairs(jnp.concatenate([y_ref[c] for c in range(OUT_PARTS)], axis=1))
        x = x_ref[...] + g2_ref[...] * y
        xo_ref[...] = x
    else:
        (x_ref, sc_ref, sh_ref, nw_ref, wa_ref, wm_ref, wg_ref, wmg_ref,
         a_ref, m_ref, mg_ref, gr_ref, g_ref) = refs
        x = x_ref[...]
    ms = jnp.mean(x * x, axis=-1, keepdims=True)
    h = x * lax.rsqrt(ms + EPS) * (nw_ref[...] * (1.0 + sc_ref[...])) + sh_ref[...]
    hb = h.astype(BF16)
    a_ref[...] = _dot(hb, wa_ref[...]).astype(BF16)
    m_ref[...] = _dot(hb, wm_ref[...]).astype(BF16)
    mg_ref[...] = _sigmoid(_dot(hb, wmg_ref[...])).astype(BF16)
    g_ref[...] = _dot(hb, wg_ref[...])
    gr_ref[...] = g_ref[...].T[:SUBLANES, :]


def _inproj(x, moe, sc, sh, nw, wa, wm, wg, wmg, l, seq):
    t, d = x.shape
    tm = min(TILE_INPROJ, seq)
    tpb = seq // tm
    row = lambda i: (i, 0)
    bsel = lambda i: (i // tpb, 0, 0)
    wsel = lambda i: (l, 0, 0)
    once = pl.Buffered(1)
    na, nm, ng, nmg = wa.shape[2], wm.shape[2], wg.shape[2], wmg.shape[2]
    fuse = moe is not None
    moe_specs = [pl.BlockSpec((OUT_PARTS, tm, LANES), lambda i: (0, i, 0)), pl.BlockSpec((None, 1, d), bsel)]
    return pl.pallas_call(
        functools.partial(_inproj_kernel, fuse_residual=fuse),
        grid=(t // tm,),
        in_specs=[pl.BlockSpec((tm, d), row)] + (moe_specs if fuse else []) + [
            pl.BlockSpec((None, 1, d), bsel),
            pl.BlockSpec((None, 1, d), bsel),
            pl.BlockSpec((None, 1, d), wsel),
            pl.BlockSpec((None, d, na), wsel, pipeline_mode=once),
            pl.BlockSpec((None, d, nm), wsel, pipeline_mode=once),
            pl.BlockSpec((None, d, ng), wsel, pipeline_mode=once),
            pl.BlockSpec((None, d, nmg), wsel, pipeline_mode=once),
        ],
        out_specs=[
            pl.BlockSpec((tm, na), row),
            pl.BlockSpec((tm, nm), row),
            pl.BlockSpec((tm, nmg), row),
            pl.BlockSpec((SUBLANES, tm), lambda i: (0, i)),
        ] + ([pl.BlockSpec((tm, d), row)] if fuse else []),
        out_shape=[
            jax.ShapeDtypeStruct((t, na), BF16),
            jax.ShapeDtypeStruct((t, nm), BF16),
            jax.ShapeDtypeStruct((t, nmg), BF16),
            jax.ShapeDtypeStruct((SUBLANES, t), F32),
        ] + ([jax.ShapeDtypeStruct((t, d), F32)] if fuse else []),
        scratch_shapes=[pltpu.VMEM((tm, ng), F32)],
        compiler_params=_params("arbitrary"),
        name="inproj",
    )(x, *(moe if fuse else ()), sc, sh, nw, wa, wm, wg, wmg)


def _rope(t, cos, sin):
    w = t.shape[1]
    reps = w // LANES
    cosw = jnp.concatenate([cos] * reps, axis=1) if reps > 1 else cos
    sinw = jnp.concatenate([sin] * reps, axis=1) if reps > 1 else sin
    lane = lax.broadcasted_iota(I32, t.shape, 1)
    half = ROPE_DIM // 2
    up = pltpu.roll(t, w - half, axis=1)
    dn = pltpu.roll(t, half, axis=1)
    partner = jnp.where((lane % ROPE_DIM) < half, up, dn)
    return t * cosw + partner * sinw


def _head_norm(t, bd, w):
    ms = _dot((t * t).astype(BF16), bd)
    return t * lax.rsqrt(ms + EPS) * w


def _attn_kernel(sink_ref, cur_ref, prev_ref, cos_ref, sin_ref, cosp_ref, sinp_ref,
                 qw_ref, kw_ref, bdq_ref, bdk_ref, o_ref):
    tq = cur_ref.shape[0]
    nj = tq // ATTN_BLOCK
    qw = N_HEADS * HEAD_DIM
    kw = N_KV * HEAD_DIM
    blk0 = pl.program_id(1) * nj

    cur = cur_ref[...]
    q = cur[:, :qw].astype(F32)
    kc = cur[:, qw:qw + kw].astype(F32)
    vc = cur[:, qw + kw:].astype(F32)
    prev = prev_ref[...]
    kp = prev[:, :kw].astype(F32)
    vp = prev[:, kw:].astype(F32)

    cos, sin = cos_ref[...], sin_ref[...]
    q = _rope(_head_norm(q, bdq_ref[...], qw_ref[...]), cos, sin)
    kc = _rope(_head_norm(kc, bdk_ref[...], kw_ref[...]), cos, sin)
    kp = _rope(_head_norm(kp, bdk_ref[...], kw_ref[...]), cosp_ref[...], sinp_ref[...])
    qb = q.astype(BF16)

    def both_halves(x2):
        swapped = pltpu.roll(x2, HEAD_DIM, axis=1)
        first = lax.broadcasted_iota(I32, x2.shape, 1) < HEAD_DIM
        return jnp.concatenate([jnp.where(first, x2, swapped), jnp.where(first, swapped, x2)], axis=1).astype(BF16)

    k_all = both_halves(jnp.concatenate([kp, kc], axis=0))
    v_all = both_halves(jnp.concatenate([vp, vc], axis=0))

    lane = lax.broadcasted_iota(I32, (ATTN_BLOCK, LANES), 1)
    lo = lane < HEAD_DIM
    zero = jnp.zeros((ATTN_BLOCK, LANES), BF16)
    g_heads = N_HEADS // N_KV
    ri = lax.broadcasted_iota(I32, (g_heads * ATTN_BLOCK, ATTN_BLOCK), 0) % ATTN_BLOCK
    ci = lax.broadcasted_iota(I32, (g_heads * ATTN_BLOCK, ATTN_BLOCK), 1)
    from_prev = ci > ri
    head_row = lax.broadcasted_iota(I32, (g_heads * ATTN_BLOCK, 1), 0) // ATTN_BLOCK
    ones_v = jnp.ones((2 * ATTN_BLOCK, LANES), BF16)

    tiles = [(j, g) for j in range(nj) for g in range(N_KV)]
    scores = {}
    for j, g in tiles:
        rows = slice(j * ATTN_BLOCK, (j + 1) * ATTN_BLOCK)
        band = slice(j * ATTN_BLOCK, (j + 2) * ATTN_BLOCK)
        qp0 = qb[rows, (2 * g) * LANES:(2 * g + 1) * LANES]
        qp1 = qb[rows, (2 * g + 1) * LANES:(2 * g + 2) * LANES]
        q4 = jnp.concatenate([jnp.where(lo, qp0, zero), jnp.where(lo, zero, qp0),
                              jnp.where(lo, qp1, zero), jnp.where(lo, zero, qp1)], axis=0)
        scores[j, g] = _dot_nt(q4, k_all[band, g * LANES:(g + 1) * LANES])

    probs, sink_term = {}, {}
    for j, g in tiles:
        s2 = scores[j, g]
        if j == 0:
            prev_ok = ci > ri + (1 - jnp.minimum(blk0, 1)) * ATTN_BLOCK
            s = jnp.where(prev_ok, s2[:, :ATTN_BLOCK], jnp.where(from_prev, -jnp.inf, s2[:, ATTN_BLOCK:]))
        else:
            s = jnp.where(from_prev, s2[:, :ATTN_BLOCK], s2[:, ATTN_BLOCK:])
        sink = jnp.full((g_heads * ATTN_BLOCK, 1), sink_ref[g_heads * g], F32)
        for r in range(1, g_heads):
            sink = jnp.where(head_row == r, sink_ref[g_heads * g + r], sink)
        m = jnp.maximum(jnp.max(s, axis=-1, keepdims=True), sink)
        p = jnp.exp(s - m)
        probs[j, g] = jnp.concatenate([jnp.where(from_prev, p, 0.0), jnp.where(from_prev, 0.0, p)],
                                      axis=1).astype(BF16)
        sink_term[j, g] = jnp.exp(sink - m)

    for j, g in tiles:
        rows = slice(j * ATTN_BLOCK, (j + 1) * ATTN_BLOCK)
        band = slice(j * ATTN_BLOCK, (j + 2) * ATTN_BLOCK)
        o8 = _dot(probs[j, g], jnp.concatenate([v_all[band, g * LANES:(g + 1) * LANES], ones_v], axis=1))
        o4 = o8[:, :LANES] / (o8[:, LANES:] + sink_term[j, g])
        b = ATTN_BLOCK
        o_ref[rows, (2 * g) * LANES:(2 * g + 1) * LANES] = jnp.where(lo, o4[0:b], o4[b:2 * b]).astype(BF16)
        o_ref[rows, (2 * g + 1) * LANES:(2 * g + 2) * LANES] = jnp.where(
            lo, o4[2 * b:3 * b], o4[3 * b:4 * b]).astype(BF16)


def _attention(a_in, cos_t, sin_t, sinks_l, qw, kw, bdq, bdk, batch, seq):
    t = a_in.shape[0]
    tq = min(TILE_ATTN, seq)
    nj = tq // ATTN_BLOCK
    tpb = seq // tq
    bpb = seq // ATTN_BLOCK
    qwid = N_HEADS * HEAD_DIM
    kvw = 2 * N_KV * HEAD_DIM
    cur = lambda b, i: (b * tpb + i, 0)
    prv = lambda b, i: (b * bpb + jnp.maximum(i * nj - 1, 0), qwid // kvw)
    prv0 = lambda b, i: (b * bpb + jnp.maximum(i * nj - 1, 0), 0)
    const = lambda b, i: (0, 0)
    return pl.pallas_call(
        _attn_kernel,
        grid=(batch, tpb),
        in_specs=[
            pl.BlockSpec(memory_space=pltpu.SMEM),
            pl.BlockSpec((tq, qwid + kvw), cur),
            pl.BlockSpec((ATTN_BLOCK, kvw), prv),
            pl.BlockSpec((tq, LANES), cur),
            pl.BlockSpec((tq, LANES), cur),
            pl.BlockSpec((ATTN_BLOCK, LANES), prv0),
            pl.BlockSpec((ATTN_BLOCK, LANES), prv0),
            pl.BlockSpec((1, qwid), const),
            pl.BlockSpec((1, kvw // 2), const),
            pl.BlockSpec((qwid, qwid), const),
            pl.BlockSpec((kvw // 2, kvw // 2), const),
        ],
        out_specs=pl.BlockSpec((tq, qwid), cur),
        out_shape=jax.ShapeDtypeStruct((t, qwid), BF16),
        compiler_params=_params("arbitrary", "arbitrary"),
        name="swa_attention",
    )(sinks_l, a_in, a_in, cos_t, sin_t, cos_t, sin_t, qw, kw, bdq, bdk)


def _mlstm_kernel(min_ref, gr_ref, cw_ref, cb_ref, bcol_ref, nw_ref,
                  hm_ref, ext_ref, q_ref, kt_ref, st_ref, mx_ref, ab_ref, bc_ref):
    tt = min_ref.shape[0]
    mw = M_HEADS * M_DIM
    nchunks = tt // CHUNK

    @pl.when(pl.program_id(1) == 0)
    def _():
        ext_ref[0:SUBLANES, :] = jnp.zeros((SUBLANES, 2 * mw), F32)
        st_ref[...] = jnp.zeros(st_ref.shape, F32)
        mx_ref[...] = jnp.zeros(mx_ref.shape, F32)

    def conv_block(cols):
        u = min_ref[:, cols].astype(F32)
        ext_ref[SUBLANES:SUBLANES + tt, cols] = u
        acc = cb_ref[:, cols] + cw_ref[CONV_K - 1:CONV_K, cols] * u
        for jj in range(CONV_K - 1):
            off = SUBLANES - (CONV_K - 1) + jj
            acc = acc + cw_ref[jj:jj + 1, cols] * ext_ref[off:off + tt, cols]
        ext_ref[0:SUBLANES, cols] = u[tt - SUBLANES:tt, :]
        return acc * _sigmoid(acc)

    def q_body(h, carry):
        cols = pl.ds(pl.multiple_of(h * M_DIM, M_DIM), M_DIM)
        q_ref[:, cols] = conv_block(cols).astype(BF16)
        return carry

    def k_body(h, carry):
        off = pl.multiple_of(h * M_DIM, M_DIM)
        act = conv_block(pl.ds(mw + off, M_DIM)) * (M_DIM ** -0.5)
        for j in range(nchunks):
            kt_ref[pl.ds(off, M_DIM), j * CHUNK:(j + 1) * CHUNK] = act[j * CHUNK:(j + 1) * CHUNK, :].T
        return carry

    lax.fori_loop(0, M_HEADS, q_body, 0)
    lax.fori_loop(0, M_HEADS, k_body, 0)

    ri = lax.broadcasted_iota(I32, (CHUNK, CHUNK), 0)
    ci = lax.broadcasted_iota(I32, (CHUNK, CHUNK), 1)
    causal = ci <= ri
    triu = jnp.where(ri <= ci, 1.0, 0.0).astype(BF16)
    ones_half = jnp.ones((CHUNK, M_DIM), BF16)
    mean_mat = jnp.full((M_DIM, M_DIM), 1.0 / M_DIM, BF16)
    sub = lax.broadcasted_iota(I32, (SUBLANES, CHUNK), 0)
    heads = range(M_HEADS)

    pad_rows = jnp.zeros((CHUNK - SUBLANES, CHUNK), F32)
    zero_rows = jnp.zeros((SUBLANES, CHUNK), F32)

    def gate_body(jg, carry):
        for u_ in range(GATE_CHUNKS_PER_STEP):
            rs = pl.ds(pl.multiple_of((jg * GATE_CHUNKS_PER_STEP + u_) * CHUNK, CHUNK), CHUNK)
            gr = gr_ref[:, rs] + bcol_ref[...]
            ls = _log_sigmoid(gr)
            ls1 = ls.astype(BF16).astype(F32)
            ls2 = (ls - ls1).astype(BF16).astype(F32)
            pieces = jnp.concatenate([ls1, ls2, ls - ls1 - ls2, zero_rows], axis=0).astype(BF16)
            sums = _dot(pieces, triu)
            br = sums[0:SUBLANES] + sums[SUBLANES:2 * SUBLANES] + sums[2 * SUBLANES:3 * SUBLANES]
            ab = jnp.where(sub < M_HEADS, gr - pltpu.roll(br, M_HEADS, axis=0), br)
            ab_ref[:, rs] = ab
            bc_ref[rs, :] = jnp.concatenate([ab, pad_rows], axis=0).T
        return carry

    lax.fori_loop(0, nchunks // GATE_CHUNKS_PER_STEP, gate_body, 0)

    def group_body(cg, carry):
        rows, ab = [], []
        for u_ in range(CHUNKS_PER_STEP):
            r0 = pl.multiple_of((cg * CHUNKS_PER_STEP + u_) * CHUNK, CHUNK)
            rows.append(pl.ds(r0, CHUNK))
            ab.append(ab_ref[:, rows[u_]])
        lanes = [(u_, h) for u_ in range(CHUNKS_PER_STEP) for h in heads]
        a_r = {(u_, h): ab[u_][h:h + 1, :] for u_, h in lanes}
        b_last = {(u_, h): ab[u_][M_HEADS + h:M_HEADS + h + 1, CHUNK - 1:CHUNK] for u_, h in lanes}

        m_prev, a_max, a_dec, s_in = {}, {}, {}, {}
        m_run = [mx_ref[h][0:1, 0:1] for h in heads]
        for k in lanes:
            u_, h = k
            m_prev[k] = m_run[h]
            a_max[k] = jnp.max(a_r[k], axis=-1, keepdims=True)
            m_loc = b_last[k] + a_max[k]
            m_new = jnp.maximum(b_last[k] + m_prev[k], m_loc)
            a_dec[k] = jnp.exp(b_last[k] + m_prev[k] - m_new)
            s_in[k] = jnp.exp(m_loc - m_new)
            m_run[h] = m_new
        for h in heads:
            mx_ref[h] = jnp.broadcast_to(m_run[h], (SUBLANES, LANES))

        q, v_ext, s_qk, kv = {}, {}, {}, {}
        for k in lanes:
            u_, h = k
            rs = rows[u_]
            q[k] = q_ref[rs, h * M_DIM:(h + 1) * M_DIM]
            kt = kt_ref[h * M_DIM:(h + 1) * M_DIM, rs]
            v = min_ref[rs, 2 * mw + h * M_DIM:2 * mw + (h + 1) * M_DIM]
            v_ext[k] = jnp.concatenate([v, ones_half], axis=1)
            s_qk[k] = _dot(q[k], kt.astype(BF16))
            e_r = jnp.exp(a_r[k] - a_max[k])
            kv[k] = _dot((kt * e_r).astype(BF16), v_ext[k])

        thr, qk = {}, {}
        for k in lanes:
            u_, h = k
            a_mat = jnp.where(causal, a_r[k], -jnp.inf)
            mu = jnp.maximum(jnp.max(a_mat, axis=-1, keepdims=True), m_prev[k])
            b_c = bc_ref[rows[u_], M_HEADS + h:M_HEADS + h + 1]
            thr[k] = jnp.broadcast_to(jnp.exp(-(b_c + mu)), (CHUNK, M_DIM))
            mu_b = jnp.broadcast_to(mu, (CHUNK, CHUNK))
            inter = jnp.exp(m_prev[k] - mu_b)
            qk[k] = jnp.concatenate([(s_qk[k] * jnp.exp(a_mat - mu_b)).astype(BF16),
                                     (q[k].astype(F32) * inter).astype(BF16)], axis=1)

        state = [st_ref[h] for h in heads]
        for k in lanes:
            u_, h = k
            hs = slice(h * M_DIM, (h + 1) * M_DIM)
            num = _dot(qk[k], jnp.concatenate([v_ext[k], state[h].astype(BF16)], axis=0))
            state[h] = a_dec[k] * state[h] + s_in[k] * kv[k]
            den = jnp.maximum(jnp.abs(num[:, M_DIM:]), thr[k])
            hh = num[:, :M_DIM] / den
            msq = _dot((hh * hh).astype(BF16), mean_mat)
            hn = hh * lax.rsqrt(msq + EPS) * nw_ref[:, hs]
            og = min_ref[rows[u_], 3 * mw + h * M_DIM:3 * mw + (h + 1) * M_DIM].astype(F32)
            hm_ref[rows[u_], hs] = (_sigmoid(og) * hn).astype(BF16)
        for h in heads:
            st_ref[h] = state[h]
        return carry

    lax.fori_loop(0, nchunks // CHUNKS_PER_STEP, group_body, 0)


def _mlstm(m_in, grow, conv_w, conv_b, bcol, nw, batch, seq):
    t = m_in.shape[0]
    tt = min(TILE_MLSTM, seq)
    tpb = seq // tt
    mw = M_HEADS * M_DIM
    cur = lambda b, i: (b * tpb + i, 0)
    const = lambda b, i: (0, 0)
    return pl.pallas_call(
        _mlstm_kernel,
        grid=(batch, tpb),
        in_specs=[
            pl.BlockSpec((tt, 4 * mw), cur),
            pl.BlockSpec((SUBLANES, tt), lambda b, i: (0, b * tpb + i)),
            pl.BlockSpec((CONV_K, 2 * mw), const),
            pl.BlockSpec((1, 2 * mw), const),
            pl.BlockSpec((SUBLANES, LANES), const),
            pl.BlockSpec((1, mw), const),
        ],
        out_specs=pl.BlockSpec((tt, mw), cur),
        out_shape=jax.ShapeDtypeStruct((t, mw), BF16),
        scratch_shapes=[
            pltpu.VMEM((tt + SUBLANES, 2 * mw), F32),
            pltpu.VMEM((tt, mw), BF16),
            pltpu.VMEM((mw, tt), F32),
            pltpu.VMEM((M_HEADS, M_DIM, 2 * M_DIM), F32),
            pltpu.VMEM((M_HEADS, SUBLANES, LANES), F32),
            pltpu.VMEM((SUBLANES, tt), F32),
            pltpu.VMEM((tt, LANES), F32),
        ],
        compiler_params=_params("arbitrary", "arbitrary"),
        name="mlstm",
    )(m_in, grow, conv_w, conv_b, bcol, nw)


def _merge_kernel(o_ref, hm_ref, mg_ref, x_ref, g1_ref, sc_ref, sh_ref, nw_ref,
                  wa_ref, wm_ref, wo_ref, rwh_ref, rwl_ref, rb_ref, tri_ref,
                  x1_ref, pay_ref, route_ref, cnt_ref, carry_ref):
    tm, d = x_ref.shape

    @pl.when(pl.program_id(0) == 0)
    def _():
        carry_ref[...] = jnp.zeros(carry_ref.shape, F32)

    ya = _dot(o_ref[...], wa_ref[...])
    yb = _dot(hm_ref[...], wm_ref[...])
    mg = mg_ref[...]
    merged = mg[:, :d].astype(F32) * ya + mg[:, d:].astype(F32) * yb
    x1 = x_ref[...] + g1_ref[...] * _dot(merged.astype(BF16), wo_ref[...])
    x1_ref[...] = x1

    ms = jnp.mean(x1 * x1, axis=-1, keepdims=True)
    h2 = x1 * lax.rsqrt(ms + EPS) * (nw_ref[...] * (1.0 + sc_ref[...])) + sh_ref[...]
    hi = h2.astype(BF16)
    hif = hi.astype(F32)
    lo = (h2 - hif).astype(BF16)
    r_hi = _dot_nt(rwh_ref[...], hi)
    r_lo = _dot_nt(rwl_ref[...], lo)
    sc_t = _sigmoid(r_hi[:N_EXPERTS] + r_hi[N_EXPERTS:] + r_lo)
    sel_t = sc_t + rb_ref[:, 0:1]

    def row(a, e):
        return a[e:e + 1, :]

    best = None
    gi = jnp.zeros((1, tm), I32)
    for g in range(N_GROUPS):
        r = [row(sel_t, EPG * g + i) for i in range(EPG)]
        gs = None
        for i in range(EPG):
            for j in range(i + 1, EPG):
                pr = r[i] + r[j]
                gs = pr if gs is None else jnp.maximum(gs, pr)
        if best is None:
            best = gs
        else:
            upd = gs > best
            gi = jnp.where(upd, g, gi)
            best = jnp.maximum(best, gs)

    def pick(a, i):
        out = row(a, i)
        for g in range(1, N_GROUPS):
            out = jnp.where(gi == g, row(a, EPG * g + i), out)
        return out

    v = [pick(sel_t, i) for i in range(EPG)]
    s = [pick(sc_t, i) for i in range(EPG)]

    def argmax4(vals):
        bv, bi = vals[0], jnp.zeros((1, tm), I32)
        for i in range(1, EPG):
            upd = vals[i] > bv
            bi = jnp.where(upd, i, bi)
            bv = jnp.maximum(bv, vals[i])
        return bi

    i1 = argmax4(v)
    i2 = argmax4([jnp.where(i1 == i, -jnp.inf, v[i]) for i in range(EPG)])
    ia = jnp.minimum(i1, i2)
    ib = jnp.maximum(i1, i2)
    pidx = jnp.where(ia == 0, ib - 1, jnp.where(ia == 1, jnp.where(ib == 3, 3, 5), 4))
    bucket = gi * N_PAIRS + pidx

    def by_index(vals, idx):
        out = vals[0]
        for i in range(1, EPG):
            out = jnp.where(idx == i, vals[i], out)
        return out

    swap = pidx == N_PAIRS - 1
    s_lo, s_hi = by_index(s, ia), by_index(s, ib)
    s_a, s_b = jnp.where(swap, s_hi, s_lo), jnp.where(swap, s_lo, s_hi)
    gate_a = s_a / (s_a + s_b)
    gate_b = s_b / (s_a + s_b)

    brow = lax.broadcasted_iota(I32, (BUCKET_ROWS, tm), 0)
    onehot = brow == bucket
    cums = _dot(jnp.where(onehot, 1.0, 0.0).astype(BF16), tri_ref[...])
    carry = carry_ref[...]
    rank = jnp.sum(jnp.where(onehot, carry[:, 0:1] + cums, 0.0), axis=0, keepdims=True) - 1.0
    new_carry = carry + cums[:, tm - 1:tm]
    carry_ref[...] = new_carry
    cnt_ref[...] = new_carry

    route_ref[...] = jnp.concatenate(
        [bucket.astype(F32), gate_a, gate_b, rank, jnp.zeros((SUBLANES - 4, tm), F32)], axis=0)

    half = d // 2
    packed = _pack_bf16_pairs(hif, rounded=True)
    for cpart in range(half // LANES):
        pay_ref[cpart] = packed[:, cpart * LANES:(cpart + 1) * LANES]
    gates_t = jnp.concatenate([gate_a, gate_b, jnp.zeros((LANES - 2, tm), F32)], axis=0)
    pay_ref[half // LANES] = lax.bitcast_convert_type(gates_t.T, U32)


def _merge(o_attn, hm, mg, x, g1, sc2, sh2, nw, wa, wm, wo, rwh, rwl, rb, tri, l, seq):
    t, d = x.shape
    tm = tri.shape[0]
    tpb = seq // tm
    row = lambda i: (i, 0)
    bsel = lambda i: (i // tpb, 0, 0)
    wsel = lambda i: (l, 0, 0)
    const = lambda i: (0, 0)
    hw = o_attn.shape[1]
    return pl.pallas_call(
        _merge_kernel,
        grid=(t // tm,),
        in_specs=[
            pl.BlockSpec((tm, hw), row),
            pl.BlockSpec((tm, hw), row),
            pl.BlockSpec((tm, 2 * d), row),
            pl.BlockSpec((tm, d), row),
            pl.BlockSpec((None, 1, d), bsel),
            pl.BlockSpec((None, 1, d), bsel),
            pl.BlockSpec((None, 1, d), bsel),
            pl.BlockSpec((None, 1, d), wsel),
            pl.BlockSpec((None, hw, d), wsel),
            pl.BlockSpec((None, hw, d), wsel),
            pl.BlockSpec((None, d, d), wsel),
            pl.BlockSpec((2 * N_EXPERTS, d), const),
            pl.BlockSpec((N_EXPERTS, d), const),
            pl.BlockSpec((N_EXPERTS, LANES), const),
            pl.BlockSpec((tm, tm), const),
        ],
        out_specs=[
            pl.BlockSpec((tm, d), row),
            pl.BlockSpec((PAY_PARTS, tm, LANES), lambda i: (0, i, 0)),
            pl.BlockSpec((SUBLANES, tm), lambda i: (0, i)),
            pl.BlockSpec((BUCKET_ROWS, LANES), const),
        ],
        out_shape=[
            jax.ShapeDtypeStruct((t, d), F32),
            jax.ShapeDtypeStruct((PAY_PARTS, t, LANES), U32),
            jax.ShapeDtypeStruct((SUBLANES, t), F32),
            jax.ShapeDtypeStruct((BUCKET_ROWS, LANES), F32),
        ],
        scratch_shapes=[pltpu.VMEM((BUCKET_ROWS, LANES), F32)],
        compiler_params=_params("arbitrary"),
        name="merge_router",
    )(o_attn, hm, mg, x, g1, sc2, sh2, nw, wa, wm, wo, rwh, rwl, rb, tri)


def _sc_mesh():
    return plsc.VectorSubcoreMesh(core_axis_name="core", subcore_axis_name="subcore")


def _sc_scatter_rows(rows, dest, n_out):
    n, w = rows.shape

    @pl.kernel(out_type=jax.ShapeDtypeStruct((n_out, w), rows.dtype), mesh=_sc_mesh(), scratch_types=[])
    def scatter(x_hbm, i_hbm, o_hbm):
        def body(x_vmem, i_vmem):
            pltpu.sync_copy(x_vmem, o_hbm.at[i_vmem.at[0]])

        pltpu.emit_pipeline(
            body,
            grid=(n // SC_WINDOW,),
            in_specs=[pl.BlockSpec((SC_WINDOW, w), lambda i: (i, 0)),
                      pl.BlockSpec((1, SC_WINDOW), lambda i: (0, i))],
            out_specs=[],
            core_axis_name=("core", "subcore"),
            dimension_semantics=(pltpu.PARALLEL,),
        )(x_hbm, i_hbm)

    return scatter(rows, dest.reshape(1, n))


def _sc_gather_rows(src, idx):
    n = idx.shape[0]
    w = src.shape[1]

    @pl.kernel(out_type=jax.ShapeDtypeStruct((n, w), src.dtype), mesh=_sc_mesh(), scratch_types=[])
    def gather(x_hbm, i_hbm, o_hbm):
        def body(i_vmem, o_vmem):
            pltpu.sync_copy(x_hbm.at[i_vmem.at[0]], o_vmem)

        pltpu.emit_pipeline(
            body,
            grid=(n // SC_WINDOW,),
            in_specs=[pl.BlockSpec((1, SC_WINDOW), lambda i: (0, i))],
            out_specs=[pl.BlockSpec((SC_WINDOW, w), lambda i: (i, 0))],
            core_axis_name=("core", "subcore"),
            dimension_semantics=(pltpu.PARALLEL,),
        )(i_hbm, o_hbm)

    return gather(src, idx.reshape(1, n))


def _row_index_kernel(ps_ref, route_ref, o_ref, *, n_rows):
    bucket = route_ref[0:1, :].astype(I32)
    start = jnp.zeros(bucket.shape, I32)
    for b in range(N_BUCKETS):
        start = jnp.where(bucket == b, ps_ref[b], start)
    dest = start + route_ref[3:4, :].astype(I32)
    part = lax.broadcasted_iota(I32, o_ref.shape, 0)
    o_ref[...] = part * n_rows + dest


def _row_index(pad_starts, route, n_rows):
    t = route.shape[1]
    tm = min(TILE_ROW_INDEX, t)
    return pl.pallas_call(
        functools.partial(_row_index_kernel, n_rows=n_rows),
        grid=(t // tm,),
        in_specs=[pl.BlockSpec(memory_space=pltpu.SMEM), pl.BlockSpec((SUBLANES, tm), lambda i: (0, i))],
        out_specs=pl.BlockSpec((SUBLANES, tm), lambda i: (0, i)),
        out_shape=jax.ShapeDtypeStruct((SUBLANES, t), I32),
        compiler_params=_params("arbitrary"),
        name="row_index",
    )(pad_starts, route)


def _residual_kernel(x_ref, y_ref, g2_ref, o_ref):
    y = _unpack_bf16_pairs(jnp.concatenate([y_ref[c] for c in range(OUT_PARTS)], axis=1))
    o_ref[...] = x_ref[...] + g2_ref[...] * y


def _residual(x1, ytok, g2, seq):
    t, d = x1.shape
    tm = min(TILE_RESIDUAL, seq)
    tpb = seq // tm
    return pl.pallas_call(
        _residual_kernel,
        grid=(t // tm,),
        in_specs=[
            pl.BlockSpec((tm, d), lambda i: (i, 0)),
            pl.BlockSpec((OUT_PARTS, tm, LANES), lambda i: (0, i, 0)),
            pl.BlockSpec((None, 1, d), lambda i: (i // tpb, 0, 0)),
        ],
        out_specs=pl.BlockSpec((tm, d), lambda i: (i, 0)),
        out_shape=jax.ShapeDtypeStruct((t, d), F32),
        compiler_params=_params("arbitrary"),
        name="residual",
    )(x1, ytok, g2)


def _expert_kernel(ea_ref, eb_ref, nr_ref, xs_ref, wga_ref, wua_ref, wda_ref, wgb_ref, wub_ref, wdb_ref, ys_ref,
                   ga_ref, ua_ref, gb_ref, ub_ref, dab_ref):
    j = pl.program_id(0)
    nr = nr_ref[0]
    prev = jnp.maximum(j - 1, 0)
    f = wda_ref.shape[0]

    @pl.when((j == 0) | (ea_ref[j] != ea_ref[prev]))
    def _():
        ga_ref[...] = wga_ref[...].astype(BF16)
        ua_ref[...] = wua_ref[...].astype(BF16)
        dab_ref[0:f, :] = wda_ref[...].astype(BF16)

    @pl.when((j == 0) | (eb_ref[j] != eb_ref[prev]))
    def _():
        gb_ref[...] = wgb_ref[...].astype(BF16)
        ub_ref[...] = wub_ref[...].astype(BF16)
        dab_ref[f:2 * f, :] = wdb_ref[...].astype(BF16)

    @pl.when(j < nr)
    def _():
        x = _unpack_bf16_pairs(jnp.concatenate([xs_ref[c] for c in range(PAY_PARTS - 1)], axis=1)).astype(BF16)
        gl = lax.bitcast_convert_type(xs_ref[PAY_PARTS - 1], F32)

        def gated_act(wg_ref, wu_ref, gate):
            gte = _dot(x, wg_ref[...])
            return (gte * _sigmoid(gte) * _dot(x, wu_ref[...]) * gate).astype(BF16)

        acts = jnp.concatenate([gated_act(ga_ref, ua_ref, gl[:, 0:1]), gated_act(gb_ref, ub_ref, gl[:, 1:2])], axis=1)
        y = _pack_bf16_pairs(_dot(acts, dab_ref[...]))
        for c in range(OUT_PARTS):
            ys_ref[c] = y[:, c * LANES:(c + 1) * LANES]

    @pl.when(j >= nr)
    def _():
        ys_ref[...] = jnp.zeros(ys_ref.shape, U32)


def _experts(blk_ea, blk_eb, n_real, xs, wg, wu, wd, d):
    n_rows = xs.shape[1]
    nblk = n_rows // EXPERT_BLOCK
    f = wg.shape[2]
    grid_spec = pltpu.PrefetchScalarGridSpec(
        num_scalar_prefetch=3,
        grid=(nblk,),
        in_specs=[
            pl.BlockSpec((PAY_PARTS, EXPERT_BLOCK, LANES), lambda j, ea, eb, nr: (0, j, 0)),
            pl.BlockSpec((None, d, f), lambda j, ea, eb, nr: (ea[j], 0, 0)),
            pl.BlockSpec((None, d, f), lambda j, ea, eb, nr: (ea[j], 0, 0)),
            pl.BlockSpec((None, f, d), lambda j, ea, eb, nr: (ea[j], 0, 0)),
            pl.BlockSpec((None, d, f), lambda j, ea, eb, nr: (eb[j], 0, 0)),
            pl.BlockSpec((None, d, f), lambda j, ea, eb, nr: (eb[j], 0, 0)),
            pl.BlockSpec((None, f, d), lambda j, ea, eb, nr: (eb[j], 0, 0)),
        ],
        out_specs=pl.BlockSpec((OUT_PARTS, EXPERT_BLOCK, LANES), lambda j, ea, eb, nr: (0, j, 0)),
        scratch_shapes=[pltpu.VMEM((d, f), BF16)] * 4 + [pltpu.VMEM((2 * f, d), BF16)],
    )
    return pl.pallas_call(
        _expert_kernel,
        grid_spec=grid_spec,
        out_shape=jax.ShapeDtypeStruct((OUT_PARTS, n_rows, LANES), U32),
        compiler_params=_params("arbitrary"),
        name="experts",
    )(blk_ea, blk_eb, n_real, xs, wg, wu, wd, wg, wu, wd)


_PAIR_A = (0, 0, 0, 1, 2, 2)
_PAIR_B = (1, 2, 3, 3, 3, 1)


def kernel(x, c, positions, ada_w, ada_b, norm_mix_w, norm_ffn_w, w_in, b_igate, b_fgate, q_norm_w, k_norm_w,
           sinks, conv_w, conv_b, mlstm_norm_w, w_attn_up, w_mlstm_up, w_out, router_w, router_bias,
           w_gate, w_up, w_down):
    batch, seq, d = x.shape
    depth = w_in.shape[0]
    t = batch * seq
    qw = N_HEADS * HEAD_DIM
    kvw = N_KV * HEAD_DIM
    mw = M_HEADS * M_DIM

    o = 0
    cols = {}
    for name, wdt in (("q", qw), ("k", kvw), ("v", kvw), ("mqk", 2 * mw), ("mv", mw), ("mi", M_HEADS),
                      ("mf", M_HEADS), ("mo", mw), ("ga", d), ("gb", d)):
        cols[name] = (o, o + wdt)
        o += wdt

    def wc(name, lo=0, hi=None):
        s, e = cols[name]
        return w_in[:, :, s + lo:(s + hi if hi is not None else e)]

    w_a = jnp.concatenate([wc("q"), wc("k"), wc("v")], axis=2).astype(BF16)
    w_m = jnp.concatenate([wc("mqk"), wc("mv"), wc("mo")], axis=2).astype(BF16)
    w_g = jnp.concatenate([wc("mi"), wc("mf"), jnp.zeros((depth, d, LANES - 2 * M_HEADS), F32)], axis=2).astype(BF16)
    w_mg = jnp.concatenate([wc("ga"), wc("gb")], axis=2).astype(BF16)
    w_au = w_attn_up.astype(BF16)
    w_mu = w_mlstm_up.astype(BF16)
    w_o = w_out.astype(BF16)
    n_e = w_gate.shape[1]
    w_g8 = w_gate.reshape(depth * n_e, d, -1)
    w_u8 = w_up.reshape(depth * n_e, d, -1)
    w_d = w_down.reshape(depth * n_e, -1, d)

    rw_t = router_w.astype(F32).T
    rw_top = rw_t.astype(BF16)
    rw_hi = jnp.concatenate([rw_top, (rw_t - rw_top.astype(F32)).astype(BF16)], axis=0)
    rw_lo = rw_top
    rb = jnp.broadcast_to(router_bias.astype(F32)[:, None], (n_e, LANES))

    qn_w = jnp.tile(q_norm_w * (HEAD_DIM ** -0.5), (1, N_HEADS)).reshape(depth, 1, qw)
    kn_w = jnp.tile(k_norm_w, (1, N_KV)).reshape(depth, 1, kvw)
    seg = jnp.arange(qw) // HEAD_DIM
    bdq = jnp.where(seg[:, None] == seg[None, :], 1.0 / HEAD_DIM, 0.0).astype(BF16)
    bdk = bdq[:kvw, :kvw]

    inv_freq = ROPE_THETA ** (-(jnp.arange(0, ROPE_DIM, 2, dtype=F32) / ROPE_DIM))
    ang = positions.astype(F32).reshape(1, t) * inv_freq[:, None]
    cos8, sin8 = jnp.cos(ang).T, jnp.sin(ang).T
    pad1 = jnp.ones((t, HEAD_DIM - ROPE_DIM), F32)
    pad0 = jnp.zeros((t, HEAD_DIM - ROPE_DIM), F32)
    cos_t = jnp.tile(jnp.concatenate([cos8, cos8, pad1], axis=1), (1, LANES // HEAD_DIM))
    sin_t = jnp.tile(jnp.concatenate([-sin8, sin8, pad0], axis=1), (1, LANES // HEAD_DIM))

    gate_bias = jnp.concatenate([b_igate, b_fgate], axis=1).astype(F32)
    bcol = jnp.broadcast_to(gate_bias[:, :, None], (depth, 2 * M_HEADS, LANES))

    tm_merge = min(TILE_MERGE, seq)
    ii = jnp.arange(tm_merge)
    tri = (ii[:, None] <= ii[None, :]).astype(BF16)

    n_blk = (t + N_BUCKETS * (EXPERT_BLOCK - 1)) // EXPERT_BLOCK + 1
    n_rows = n_blk * EXPERT_BLOCK
    pair_a = jnp.asarray(_PAIR_A, I32)
    pair_b = jnp.asarray(_PAIR_B, I32)

    c_pad = jnp.zeros((SUBLANES, d), F32).at[:batch].set(c)
    mod = _ada_mod(c_pad, ada_w, ada_b)[:, :batch]

    xf = x.reshape(t, d)
    moe = None
    for l in range(depth):
        sh1, sc1, g1, sh2, sc2, g2 = [m.reshape(batch, 1, d) for m in jnp.split(mod[l], 6, axis=-1)]

        outs = _inproj(xf, moe, sc1, sh1, norm_mix_w.reshape(depth, 1, d), w_a, w_m, w_g, w_mg, l, seq)
        a_in, m_in, mg, grow = outs[:4]
        if moe is not None:
            xf = outs[4]
        o_attn = _attention(a_in, cos_t, sin_t, sinks[l], qn_w[l], kn_w[l], bdq, bdk, batch, seq)
        hm = _mlstm(m_in, grow, conv_w[l], conv_b[l].reshape(1, -1), bcol[l], mlstm_norm_w[l].reshape(1, mw),
                    batch, seq)
        x1, pay, route, cnt = _merge(o_attn, hm, mg, xf, g1, sc2, sh2, norm_ffn_w.reshape(depth, 1, d),
                                     w_au, w_mu, w_o, rw_hi, rw_lo, rb, tri, l, seq)

        counts = cnt[:N_BUCKETS, 0].astype(I32)
        padded = (counts + EXPERT_BLOCK - 1) // EXPERT_BLOCK * EXPERT_BLOCK
        pad_ends = jnp.cumsum(padded)
        pad_starts = pad_ends - padded
        row_idx = _row_index(jnp.concatenate([pad_starts, jnp.zeros((BUCKET_ROWS - N_BUCKETS,), I32)]), route, n_rows)
        blk_start = jnp.arange(n_blk, dtype=I32) * EXPERT_BLOCK
        blk_bucket = jnp.minimum(jnp.sum((pad_ends[None, :] <= blk_start[:, None]).astype(I32), axis=1), N_BUCKETS - 1)
        grp = blk_bucket // N_PAIRS
        blk_ea = (l * n_e + grp * EPG + pair_a[blk_bucket % N_PAIRS]).astype(I32)
        blk_eb = (l * n_e + grp * EPG + pair_b[blk_bucket % N_PAIRS]).astype(I32)
        n_real = (pad_ends[-1:] // EXPERT_BLOCK).astype(I32)

        xs = _sc_scatter_rows(pay.reshape(PAY_PARTS * t, LANES), row_idx[:PAY_PARTS].reshape(-1),
                              PAY_PARTS * n_rows).reshape(PAY_PARTS, n_rows, LANES)
        ys = _experts(blk_ea, blk_eb, n_real, xs, w_g8, w_u8, w_d, d)
        ytok = _sc_gather_rows(ys.reshape(OUT_PARTS * n_rows, LANES),
                               row_idx[:OUT_PARTS].reshape(-1)).reshape(OUT_PARTS, t, LANES)
        xf, moe = x1, (ytok, g2)
    return _residual(xf, moe[0], moe[1], seq).reshape(batch, seq, d)
```

```python
import functools

import jax
import jax.numpy as jnp
from jax import lax
from jax.experimental import pallas as pl
from jax.experimental.pallas import tpu as pltpu
from jax.experimental.pallas import tpu_sc as plsc

F32 = jnp.float32
BF16 = jnp.bfloat16
U32 = jnp.uint32
I32 = jnp.int32
HIGHEST = lax.Precision.HIGHEST

HEAD_DIM = 64
N_HEADS = 8
N_KV = 2
ROPE_DIM = 16
ROPE_THETA = 500000.0
ATTN_BLOCK = 128
M_HEADS = 4
M_DIM = 128
CONV_K = 4
N_EXPERTS = 16
N_GROUPS = 4
EPG = 4
EPS = 1e-6

LANES = 128
SUBLANES = 8

TILE_INPROJ = 1024
TILE_ATTN = 1024
TILE_MLSTM = 2048
TILE_MERGE = 1024
TILE_RESIDUAL = 1024
TILE_ROW_INDEX = 8192
ADA_COLS = 3072
CHUNK = 128
CHUNKS_PER_STEP = 4
GATE_CHUNKS_PER_STEP = 4
N_PAIRS = 6
N_BUCKETS = N_GROUPS * N_PAIRS
BUCKET_ROWS = 32
EXPERT_BLOCK = 512
PAY_PARTS = 5
OUT_PARTS = 4
SC_WINDOW = 128
VMEM_LIMIT = 56 * 1024 * 1024


def _dot(a, b, precision=None):
    return jnp.dot(a, b, preferred_element_type=F32, precision=precision)


def _dot_nt(a, b):
    return lax.dot_general(a, b, (((1,), (1,)), ((), ())), preferred_element_type=F32)


def _sigmoid(x):
    return 1.0 / (1.0 + jnp.exp(-x))


def _log_sigmoid(x):
    return jnp.minimum(x, 0.0) - jnp.log1p(jnp.exp(-jnp.abs(x)))


def _pack_bf16_pairs(v, rounded=False):
    n = v.shape[1] // 2
    bits = lax.bitcast_convert_type(v if rounded else v.astype(BF16).astype(F32), U32)
    return bits[:, :n] | (bits[:, n:] >> 16)


def _unpack_bf16_pairs(w):
    hi = lax.bitcast_convert_type(w & jnp.uint32(0xFFFF0000), F32)
    lo = lax.bitcast_convert_type(w << 16, F32)
    return jnp.concatenate([hi, lo], axis=1)


def _params(*sem):
    return pltpu.CompilerParams(dimension_semantics=sem, vmem_limit_bytes=VMEM_LIMIT)


def _ada_kernel(c_ref, w_ref, b_ref, o_ref):
    c = c_ref[...]
    ca = c * _sigmoid(c)
    o_ref[0] = _dot(ca, w_ref[0], HIGHEST) + b_ref[0]


def _ada_mod(c_pad, ada_w, ada_b):
    depth, d, n = ada_w.shape
    tn = ADA_COLS
    return pl.pallas_call(
        _ada_kernel,
        grid=(depth, n // tn),
        in_specs=[
            pl.BlockSpec((SUBLANES, d), lambda l, j: (0, 0)),
            pl.BlockSpec((1, d, tn), lambda l, j: (l, 0, j)),
            pl.BlockSpec((1, 1, tn), lambda l, j: (l, 0, j)),
        ],
        out_specs=pl.BlockSpec((1, SUBLANES, tn), lambda l, j: (l, 0, j)),
        out_shape=jax.ShapeDtypeStruct((depth, SUBLANES, n), F32),
        compiler_params=_params("arbitrary", "arbitrary"),
        name="ada_mod",
    )(c_pad, ada_w, ada_b.reshape(depth, 1, n))


def _inproj_kernel(*refs, fuse_residual):
    if fuse_residual:
        (x_ref, y_ref, g2_ref, sc_ref, sh_ref, nw_ref, wa_ref, wm_ref, wg_ref, wmg_ref,
         a_ref, m_ref, mg_ref, gr_ref, xo_ref, g_ref) = refs
        y = _unpack_bf16_pairs(jnp.concatenate([y_ref[c] for c in range(OUT_PARTS)], axis=1))
        x = x_ref[...] + g2_ref[...] * y
        xo_ref[...] = x
    else:
        (x_ref, sc_ref, sh_ref, nw_ref, wa_ref, wm_ref, wg_ref, wmg_ref,
         a_ref, m_ref, mg_ref, gr_ref, g_ref) = refs
        x = x_ref[...]
    ms = jnp.mean(x * x, axis=-1, keepdims=True)
    h = x * lax.rsqrt(ms + EPS) * (nw_ref[...] * (1.0 + sc_ref[...])) + sh_ref[...]
    hb = h.astype(BF16)
    a_ref[...] = _dot(hb, wa_ref[...]).astype(BF16)
    m_ref[...] = _dot(hb, wm_ref[...]).astype(BF16)
    mg_ref[...] = _sigmoid(_dot(hb, wmg_ref[...])).astype(BF16)
    g_ref[...] = _dot(hb, wg_ref[...])
    gr_ref[...] = g_ref[...].T[:SUBLANES, :]


def _inproj(x, moe, sc, sh, nw, wa, wm, wg, wmg, l, seq):
    t, d = x.shape
    tm = min(TILE_INPROJ, seq)
    tpb = seq // tm
    row = lambda i: (i, 0)
    bsel = lambda i: (i // tpb, 0, 0)
    wsel = lambda i: (l, 0, 0)
    once = pl.Buffered(1)
    na, nm, ng, nmg = wa.shape[2], wm.shape[2], wg.shape[2], wmg.shape[2]
    fuse = moe is not None
    moe_specs = [pl.BlockSpec((OUT_PARTS, tm, LANES), lambda i: (0, i, 0)), pl.BlockSpec((None, 1, d), bsel)]
    return pl.pallas_call(
        functools.partial(_inproj_kernel, fuse_residual=fuse),
        grid=(t // tm,),
        in_specs=[pl.BlockSpec((tm, d), row)] + (moe_specs if fuse else []) + [
            pl.BlockSpec((None, 1, d), bsel),
            pl.BlockSpec((None, 1, d), bsel),
            pl.BlockSpec((None, 1, d), wsel),
            pl.BlockSpec((None, d, na), wsel, pipeline_mode=once),
            pl.BlockSpec((None, d, nm), wsel, pipeline_mode=once),
            pl.BlockSpec((None, d, ng), wsel, pipeline_mode=once),
            pl.BlockSpec((None, d, nmg), wsel, pipeline_mode=once),
        ],
        out_specs=[
            pl.BlockSpec((tm, na), row),
            pl.BlockSpec((tm, nm), row),
            pl.BlockSpec((tm, nmg), row),
            pl.BlockSpec((SUBLANES, tm), lambda i: (0, i)),
        ] + ([pl.BlockSpec((tm, d), row)] if fuse else []),
        out_shape=[
            jax.ShapeDtypeStruct((t, na), BF16),
            jax.ShapeDtypeStruct((t, nm), BF16),
            jax.ShapeDtypeStruct((t, nmg), BF16),
            jax.ShapeDtypeStruct((SUBLANES, t), F32),
        ] + ([jax.ShapeDtypeStruct((t, d), F32)] if fuse else []),
        scratch_shapes=[pltpu.VMEM((tm, ng), F32)],
        compiler_params=_params("arbitrary"),
        name="inproj",
    )(x, *(moe if fuse else ()), sc, sh, nw, wa, wm, wg, wmg)


def _rope(t, cos, sin):
    w = t.shape[1]
    reps = w // LANES
    cosw = jnp.concatenate([cos] * reps, axis=1) if reps > 1 else cos
    sinw = jnp.concatenate([sin] * reps, axis=1) if reps > 1 else sin
    lane = lax.broadcasted_iota(I32, t.shape, 1)
    half = ROPE_DIM // 2
    up = pltpu.roll(t, w - half, axis=1)
    dn = pltpu.roll(t, half, axis=1)
    partner = jnp.where((lane % ROPE_DIM) < half, up, dn)
    return t * cosw + partner * sinw


def _head_norm(t, bd, w):
    ms = _dot((t * t).astype(BF16), bd)
    return t * lax.rsqrt(ms + EPS) * w


def _attn_kernel(sink_ref, cur_ref, prev_ref, cos_ref, sin_ref, cosp_ref, sinp_ref,
                 qw_ref, kw_ref, bdq_ref, bdk_ref, o_ref):
    tq = cur_ref.shape[0]
    nj = tq // ATTN_BLOCK
    qw = N_HEADS * HEAD_DIM
    kw = N_KV * HEAD_DIM
    blk0 = pl.program_id(1) * nj

    cur = cur_ref[...]
    q = cur[:, :qw].astype(F32)
    kc = cur[:, qw:qw + kw].astype(F32)
    vc = cur[:, qw + kw:].astype(F32)
    prev = prev_ref[...]
    kp = prev[:, :kw].astype(F32)
    vp = prev[:, kw:].astype(F32)

    cos, sin = cos_ref[...], sin_ref[...]
    q = _rope(_head_norm(q, bdq_ref[...], qw_ref[...]), cos, sin)
    kc = _rope(_head_norm(kc, bdk_ref[...], kw_ref[...]), cos, sin)
    kp = _rope(_head_norm(kp, bdk_ref[...], kw_ref[...]), cosp_ref[...], sinp_ref[...])
    qb = q.astype(BF16)

    def both_halves(x2):
        swapped = pltpu.roll(x2, HEAD_DIM, axis=1)
        first = lax.broadcasted_iota(I32, x2.shape, 1) < HEAD_DIM
        return jnp.concatenate([jnp.where(first, x2, swapped), jnp.where(first, swapped, x2)], axis=1).astype(BF16)

    k_all = both_halves(jnp.concatenate([kp, kc], axis=0))
    v_all = both_halves(jnp.concatenate([vp, vc], axis=0))

    lane = lax.broadcasted_iota(I32, (ATTN_BLOCK, LANES), 1)
    lo = lane < HEAD_DIM
    zero = jnp.zeros((ATTN_BLOCK, LANES), BF16)
    g_heads = N_HEADS // N_KV
    ri = lax.broadcasted_iota(I32, (g_heads * ATTN_BLOCK, ATTN_BLOCK), 0) % ATTN_BLOCK
    ci = lax.broadcasted_iota(I32, (g_heads * ATTN_BLOCK, ATTN_BLOCK), 1)
    from_prev = ci > ri
    head_row = lax.broadcasted_iota(I32, (g_heads * ATTN_BLOCK, 1), 0) // ATTN_BLOCK
    ones_v = jnp.ones((2 * ATTN_BLOCK, LANES), BF16)

    tiles = [(j, g) for j in range(nj) for g in range(N_KV)]
    scores = {}
    for j, g in tiles:
        rows = slice(j * ATTN_BLOCK, (j + 1) * ATTN_BLOCK)
        band = slice(j * ATTN_BLOCK, (j + 2) * ATTN_BLOCK)
        qp0 = qb[rows, (2 * g) * LANES:(2 * g + 1) * LANES]
        qp1 = qb[rows, (2 * g + 1) * LANES:(2 * g + 2) * LANES]
        q4 = jnp.concatenate([jnp.where(lo, qp0, zero), jnp.where(lo, zero, qp0),
                              jnp.where(lo, qp1, zero), jnp.where(lo, zero, qp1)], axis=0)
        scores[j, g] = _dot_nt(q4, k_all[band, g * LANES:(g + 1) * LANES])

    probs, sink_term = {}, {}
    for j, g in tiles:
        s2 = scores[j, g]
        if j == 0:
            prev_ok = ci > ri + (1 - jnp.minimum(blk0, 1)) * ATTN_BLOCK
            s = jnp.where(prev_ok, s2[:, :ATTN_BLOCK], jnp.where(from_prev, -jnp.inf, s2[:, ATTN_BLOCK:]))
        else:
            s = jnp.where(from_prev, s2[:, :ATTN_BLOCK], s2[:, ATTN_BLOCK:])
        sink = jnp.full((g_heads * ATTN_BLOCK, 1), sink_ref[g_heads * g], F32)
        for r in range(1, g_heads):
            sink = jnp.where(head_row == r, sink_ref[g_heads * g + r], sink)
        m = jnp.maximum(jnp.max(s, axis=-1, keepdims=True), sink)
        p = jnp.exp(s - m)
        probs[j, g] = jnp.concatenate([jnp.where(from_prev, p, 0.0), jnp.where(from_prev, 0.0, p)],
                                      axis=1).astype(BF16)
        sink_term[j, g] = jnp.exp(sink - m)

    for j, g in tiles:
        rows = slice(j * ATTN_BLOCK, (j + 1) * ATTN_BLOCK)
        band = slice(j * ATTN_BLOCK, (j + 2) * ATTN_BLOCK)
        o8 = _dot(probs[j, g], jnp.concatenate([v_all[band, g * LANES:(g + 1) * LANES], ones_v], axis=1))
        o4 = o8[:, :LANES] / (o8[:, LANES:] + sink_term[j, g])
        b = ATTN_BLOCK
        o_ref[rows, (2 * g) * LANES:(2 * g + 1) * LANES] = jnp.where(lo, o4[0:b], o4[b:2 * b]).astype(BF16)
        o_ref[rows, (2 * g + 1) * LANES:(2 * g + 2) * LANES] = jnp.where(
            lo, o4[2 * b:3 * b], o4[3 * b:4 * b]).astype(BF16)


def _attention(a_in, cos_t, sin_t, sinks_l, qw, kw, bdq, bdk, batch, seq):
    t = a_in.shape[0]
    tq = min(TILE_ATTN, seq)
    nj = tq // ATTN_BLOCK
    tpb = seq // tq
    bpb = seq // ATTN_BLOCK
    qwid = N_HEADS * HEAD_DIM
    kvw = 2 * N_KV * HEAD_DIM
    cur = lambda b, i: (b * tpb + i, 0)
    prv = lambda b, i: (b * bpb + jnp.maximum(i * nj - 1, 0), qwid // kvw)
    prv0 = lambda b, i: (b * bpb + jnp.maximum(i * nj - 1, 0), 0)
    const = lambda b, i: (0, 0)
    return pl.pallas_call(
        _attn_kernel,
        grid=(batch, tpb),
        in_specs=[
            pl.BlockSpec(memory_space=pltpu.SMEM),
            pl.BlockSpec((tq, qwid + kvw), cur),
            pl.BlockSpec((ATTN_BLOCK, kvw), prv),
            pl.BlockSpec((tq, LANES), cur),
            pl.BlockSpec((tq, LANES), cur),
            pl.BlockSpec((ATTN_BLOCK, LANES), prv0),
            pl.BlockSpec((ATTN_BLOCK, LANES), prv0),
            pl.BlockSpec((1, qwid), const),
            pl.BlockSpec((1, kvw // 2), const),
            pl.BlockSpec((qwid, qwid), const),
            pl.BlockSpec((kvw // 2, kvw // 2), const),
        ],
        out_specs=pl.BlockSpec((tq, qwid), cur),
        out_shape=jax.ShapeDtypeStruct((t, qwid), BF16),
        compiler_params=_params("arbitrary", "arbitrary"),
        name="swa_attention",
    )(sinks_l, a_in, a_in, cos_t, sin_t, cos_t, sin_t, qw, kw, bdq, bdk)


def _mlstm_kernel(min_ref, gr_ref, cw_ref, cb_ref, bcol_ref, nw_ref,
                  hm_ref, ext_ref, q_ref, kt_ref, st_ref, mx_ref, ab_ref, bc_ref):
    tt = min_ref.shape[0]
    mw = M_HEADS * M_DIM
    nchunks = tt // CHUNK

    @pl.when(pl.program_id(1) == 0)
    def _():
        ext_ref[0:SUBLANES, :] = jnp.zeros((SUBLANES, 2 * mw), F32)
        st_ref[...] = jnp.zeros(st_ref.shape, F32)
        mx_ref[...] = jnp.zeros(mx_ref.shape, F32)

    def conv_block(cols):
        u = min_ref[:, cols].astype(F32)
        ext_ref[SUBLANES:SUBLANES + tt, cols] = u
        acc = cb_ref[:, cols] + cw_ref[CONV_K - 1:CONV_K, cols] * u
        for jj in range(CONV_K - 1):
            off = SUBLANES - (CONV_K - 1) + jj
            acc = acc + cw_ref[jj:jj + 1, cols] * ext_ref[off:off + tt, cols]
        ext_ref[0:SUBLANES, cols] = u[tt - SUBLANES:tt, :]
        return acc * _sigmoid(acc)

    def q_body(h, carry):
        cols = pl.ds(pl.multiple_of(h * M_DIM, M_DIM), M_DIM)
        q_ref[:, cols] = conv_block(cols).astype(BF16)
        return carry

    def k_body(h, carry):
        off = pl.multiple_of(h * M_DIM, M_DIM)
        act = conv_block(pl.ds(mw + off, M_DIM)) * (M_DIM ** -0.5)
        for j in range(nchunks):
            kt_ref[pl.ds(off, M_DIM), j * CHUNK:(j + 1) * CHUNK] = act[j * CHUNK:(j + 1) * CHUNK, :].T
        return carry

    lax.fori_loop(0, M_HEADS, q_body, 0)
    lax.fori_loop(0, M_HEADS, k_body, 0)

    ri = lax.broadcasted_iota(I32, (CHUNK, CHUNK), 0)
    ci = lax.broadcasted_iota(I32, (CHUNK, CHUNK), 1)
    causal = ci <= ri
    triu = jnp.where(ri <= ci, 1.0, 0.0).astype(BF16)
    ones_half = jnp.ones((CHUNK, M_DIM), BF16)
    mean_mat = jnp.full((M_DIM, M_DIM), 1.0 / M_DIM, BF16)
    sub = lax.broadcasted_iota(I32, (SUBLANES, CHUNK), 0)
    heads = range(M_HEADS)

    pad_rows = jnp.zeros((CHUNK - SUBLANES, CHUNK), F32)
    zero_rows = jnp.zeros((SUBLANES, CHUNK), F32)

    def gate_body(jg, carry):
        for u_ in range(GATE_CHUNKS_PER_STEP):
            rs = pl.ds(pl.multiple_of((jg * GATE_CHUNKS_PER_STEP + u_) * CHUNK, CHUNK), CHUNK)
            gr = gr_ref[:, rs] + bcol_ref[...]
            ls = _log_sigmoid(gr)
            ls1 = ls.astype(BF16).astype(F32)
            ls2 = (ls - ls1).astype(BF16).astype(F32)
            pieces = jnp.concatenate([ls1, ls2, ls - ls1 - ls2, zero_rows], axis=0).astype(BF16)
            sums = _dot(pieces, triu)
            br = sums[0:SUBLANES] + sums[SUBLANES:2 * SUBLANES] + sums[2 * SUBLANES:3 * SUBLANES]
            ab = jnp.where(sub < M_HEADS, gr - pltpu.roll(br, M_HEADS, axis=0), br)
            ab_ref[:, rs] = ab
            bc_ref[rs, :] = jnp.concatenate([ab, pad_rows], axis=0).T
        return carry

    lax.fori_loop(0, nchunks // GATE_CHUNKS_PER_STEP, gate_body, 0)

    def group_body(cg, carry):
        rows, ab = [], []
        for u_ in range(CHUNKS_PER_STEP):
            r0 = pl.multiple_of((cg * CHUNKS_PER_STEP + u_) * CHUNK, CHUNK)
            rows.append(pl.ds(r0, CHUNK))
            ab.append(ab_ref[:, rows[u_]])
        lanes = [(u_, h) for u_ in range(CHUNKS_PER_STEP) for h in heads]
        a_r = {(u_, h): ab[u_][h:h + 1, :] for u_, h in lanes}
        b_last = {(u_, h): ab[u_][M_HEADS + h:M_HEADS + h + 1, CHUNK - 1:CHUNK] for u_, h in lanes}

        m_prev, a_max, a_dec, s_in = {}, {}, {}, {}
        m_run = [mx_ref[h][0:1, 0:1] for h in heads]
        for k in lanes:
            u_, h = k
            m_prev[k] = m_run[h]
            a_max[k] = jnp.max(a_r[k], axis=-1, keepdims=True)
            m_loc = b_last[k] + a_max[k]
            m_new = jnp.maximum(b_last[k] + m_prev[k], m_loc)
            a_dec[k] = jnp.exp(b_last[k] + m_prev[k] - m_new)
            s_in[k] = jnp.exp(m_loc - m_new)
            m_run[h] = m_new
        for h in heads:
            mx_ref[h] = jnp.broadcast_to(m_run[h], (SUBLANES, LANES))

        q, v_ext, s_qk, kv = {}, {}, {}, {}
        for k in lanes:
            u_, h = k
            rs = rows[u_]
            q[k] = q_ref[rs, h * M_DIM:(h + 1) * M_DIM]
            kt = kt_ref[h * M_DIM:(h + 1) * M_DIM, rs]
            v = min_ref[rs, 2 * mw + h * M_DIM:2 * mw + (h + 1) * M_DIM]
            v_ext[k] = jnp.concatenate([v, ones_half], axis=1)
            s_qk[k] = _dot(q[k], kt.astype(BF16))
            e_r = jnp.exp(a_r[k] - a_max[k])
            kv[k] = _dot((kt * e_r).astype(BF16), v_ext[k])

        thr, qk = {}, {}
        for k in lanes:
            u_, h = k
            a_mat = jnp.where(causal, a_r[k], -jnp.inf)
            mu = jnp.maximum(jnp.max(a_mat, axis=-1, keepdims=True), m_prev[k])
            b_c = bc_ref[rows[u_], M_HEADS + h:M_HEADS + h + 1]
            thr[k] = jnp.broadcast_to(jnp.exp(-(b_c + mu)), (CHUNK, M_DIM))
            mu_b = jnp.broadcast_to(mu, (CHUNK, CHUNK))
            inter = jnp.exp(m_prev[k] - mu_b)
            qk[k] = jnp.concatenate([(s_qk[k] * jnp.exp(a_mat - mu_b)).astype(BF16),
                                     (q[k].astype(F32) * inter).astype(BF16)], axis=1)

        state = [st_ref[h] for h in heads]
        for k in lanes:
            u_, h = k
            hs = slice(h * M_DIM, (h + 1) * M_DIM)
            num = _dot(qk[k], jnp.concatenate([v_ext[k], state[h].astype(BF16)], axis=0))
            state[h] = a_dec[k] * state[h] + s_in[k] * kv[k]
            den = jnp.maximum(jnp.abs(num[:, M_DIM:]), thr[k])
            hh = num[:, :M_DIM] / den
            msq = _dot((hh * hh).astype(BF16), mean_mat)
            hn = hh * lax.rsqrt(msq + EPS) * nw_ref[:, hs]
            og = min_ref[rows[u_], 3 * mw + h * M_DIM:3 * mw + (h + 1) * M_DIM].astype(F32)
            hm_ref[rows[u_], hs] = (_sigmoid(og) * hn).astype(BF16)
        for h in heads:
            st_ref[h] = state[h]
        return carry

    lax.fori_loop(0, nchunks // CHUNKS_PER_STEP, group_body, 0)


def _mlstm(m_in, grow, conv_w, conv_b, bcol, nw, batch, seq):
    t = m_in.shape[0]
    tt = min(TILE_MLSTM, seq)
    tpb = seq // tt
    mw = M_HEADS * M_DIM
    cur = lambda b, i: (b * tpb + i, 0)
    const = lambda b, i: (0, 0)
    return pl.pallas_call(
        _mlstm_kernel,
        grid=(batch, tpb),
        in_specs=[
            pl.BlockSpec((tt, 4 * mw), cur),
            pl.BlockSpec((SUBLANES, tt), lambda b, i: (0, b * tpb + i)),
            pl.BlockSpec((CONV_K, 2 * mw), const),
            pl.BlockSpec((1, 2 * mw), const),
            pl.BlockSpec((SUBLANES, LANES), const),
            pl.BlockSpec((1, mw), const),
        ],
        out_specs=pl.BlockSpec((tt, mw), cur),
        out_shape=jax.ShapeDtypeStruct((t, mw), BF16),
        scratch_shapes=[
            pltpu.VMEM((tt + SUBLANES, 2 * mw), F32),
            pltpu.VMEM((tt, mw), BF16),
            pltpu.VMEM((mw, tt), F32),
            pltpu.VMEM((M_HEADS, M_DIM, 2 * M_DIM), F32),
            pltpu.VMEM((M_HEADS, SUBLANES, LANES), F32),
            pltpu.VMEM((SUBLANES, tt), F32),
            pltpu.VMEM((tt, LANES), F32),
        ],
        compiler_params=_params("arbitrary", "arbitrary"),
        name="mlstm",
    )(m_in, grow, conv_w, conv_b, bcol, nw)


def _merge_kernel(o_ref, hm_ref, mg_ref, x_ref, g1_ref, sc_ref, sh_ref, nw_ref,
                  wa_ref, wm_ref, wo_ref, rwh_ref, rwl_ref, rb_ref, tri_ref,
                  x1_ref, pay_ref, route_ref, cnt_ref, carry_ref):
    tm, d = x_ref.shape

    @pl.when(pl.program_id(0) == 0)
    def _():
        carry_ref[...] = jnp.zeros(carry_ref.shape, F32)

    ya = _dot(o_ref[...], wa_ref[...])
    yb = _dot(hm_ref[...], wm_ref[...])
    mg = mg_ref[...]
    merged = mg[:, :d].astype(F32) * ya + mg[:, d:].astype(F32) * yb
    x1 = x_ref[...] + g1_ref[...] * _dot(merged.astype(BF16), wo_ref[...])
    x1_ref[...] = x1

    ms = jnp.mean(x1 * x1, axis=-1, keepdims=True)
    h2 = x1 * lax.rsqrt(ms + EPS) * (nw_ref[...] * (1.0 + sc_ref[...])) + sh_ref[...]
    hi = h2.astype(BF16)
    hif = hi.astype(F32)
    lo = (h2 - hif).astype(BF16)
    r_hi = _dot_nt(rwh_ref[...], hi)
    r_lo = _dot_nt(rwl_ref[...], lo)
    sc_t = _sigmoid(r_hi[:N_EXPERTS] + r_hi[N_EXPERTS:] + r_lo)
    sel_t = sc_t + rb_ref[:, 0:1]

    def row(a, e):
        return a[e:e + 1, :]

    best = None
    gi = jnp.zeros((1, tm), I32)
    for g in range(N_GROUPS):
        r = [row(sel_t, EPG * g + i) for i in range(EPG)]
        gs = None
        for i in range(EPG):
            for j in range(i + 1, EPG):
                pr = r[i] + r[j]
                gs = pr if gs is None else jnp.maximum(gs, pr)
        if best is None:
            best = gs
        else:
            upd = gs > best
            gi = jnp.where(upd, g, gi)
            best = jnp.maximum(best, gs)

    def pick(a, i):
        out = row(a, i)
        for g in range(1, N_GROUPS):
            out = jnp.where(gi == g, row(a, EPG * g + i), out)
        return out

    v = [pick(sel_t, i) for i in range(EPG)]
    s = [pick(sc_t, i) for i in range(EPG)]

    def argmax4(vals):
        bv, bi = vals[0], jnp.zeros((1, tm), I32)
        for i in range(1, EPG):
            upd = vals[i] > bv
            bi = jnp.where(upd, i, bi)
            bv = jnp.maximum(bv, vals[i])
        return bi

    i1 = argmax4(v)
    i2 = argmax4([jnp.where(i1 == i, -jnp.inf, v[i]) for i in range(EPG)])
    ia = jnp.minimum(i1, i2)
    ib = jnp.maximum(i1, i2)
    pidx = jnp.where(ia == 0, ib - 1, jnp.where(ia == 1, jnp.where(ib == 3, 3, 5), 4))
    bucket = gi * N_PAIRS + pidx

    def by_index(vals, idx):
        out = vals[0]
        for i in range(1, EPG):
            out = jnp.where(idx == i, vals[i], out)
        return out

    swap = pidx == N_PAIRS - 1
    s_lo, s_hi = by_index(s, ia), by_index(s, ib)
    s_a, s_b = jnp.where(swap, s_hi, s_lo), jnp.where(swap, s_lo, s_hi)
    gate_a = s_a / (s_a + s_b)
    gate_b = s_b / (s_a + s_b)

    brow = lax.broadcasted_iota(I32, (BUCKET_ROWS, tm), 0)
    onehot = brow == bucket
    cums = _dot(jnp.where(onehot, 1.0, 0.0).astype(BF16), tri_ref[...])
    carry = carry_ref[...]
    rank = jnp.sum(jnp.where(onehot, carry[:, 0:1] + cums, 0.0), axis=0, keepdims=True) - 1.0
    new_carry = carry + cums[:, tm - 1:tm]
    carry_ref[...] = new_carry
    cnt_ref[...] = new_carry

    route_ref[...] = jnp.concatenate(
        [bucket.astype(F32), gate_a, gate_b, rank, jnp.zeros((SUBLANES - 4, tm), F32)], axis=0)

    half = d // 2
    packed = _pack_bf16_pairs(hif, rounded=True)
    for cpart in range(half // LANES):
        pay_ref[cpart] = packed[:, cpart * LANES:(cpart + 1) * LANES]
    gates_t = jnp.concatenate([gate_a, gate_b, jnp.zeros((LANES - 2, tm), F32)], axis=0)
    pay_ref[half // LANES] = lax.bitcast_convert_type(gates_t.T, U32)


def _merge(o_attn, hm, mg, x, g1, sc2, sh2, nw, wa, wm, wo, rwh, rwl, rb, tri, l, seq):
    t, d = x.shape
    tm = tri.shape[0]
    tpb = seq // tm
    row = lambda i: (i, 0)
    bsel = lambda i: (i // tpb, 0, 0)
    wsel = lambda i: (l, 0, 0)
    const = lambda i: (0, 0)
    hw = o_attn.shape[1]
    return pl.pallas_call(
        _merge_kernel,
        grid=(t // tm,),
        in_specs=[
            pl.BlockSpec((tm, hw), row),
            pl.BlockSpec((tm, hw), row),
            pl.BlockSpec((tm, 2 * d), row),
            pl.BlockSpec((tm, d), row),
            pl.BlockSpec((None, 1, d), bsel),
            pl.BlockSpec((None, 1, d), bsel),
            pl.BlockSpec((None, 1, d), bsel),
            pl.BlockSpec((None, 1, d), wsel),
            pl.BlockSpec((None, hw, d), wsel),
            pl.BlockSpec((None, hw, d), wsel),
            pl.BlockSpec((None, d, d), wsel),
            pl.BlockSpec((2 * N_EXPERTS, d), const),
            pl.BlockSpec((N_EXPERTS, d), const),
            pl.BlockSpec((N_EXPERTS, LANES), const),
            pl.BlockSpec((tm, tm), const),
        ],
        out_specs=[
            pl.BlockSpec((tm, d), row),
            pl.BlockSpec((PAY_PARTS, tm, LANES), lambda i: (0, i, 0)),
            pl.BlockSpec((SUBLANES, tm), lambda i: (0, i)),
            pl.BlockSpec((BUCKET_ROWS, LANES), const),
        ],
        out_shape=[
            jax.ShapeDtypeStruct((t, d), F32),
            jax.ShapeDtypeStruct((PAY_PARTS, t, LANES), U32),
            jax.ShapeDtypeStruct((SUBLANES, t), F32),
            jax.ShapeDtypeStruct((BUCKET_ROWS, LANES), F32),
        ],
        scratch_shapes=[pltpu.VMEM((BUCKET_ROWS, LANES), F32)],
        compiler_params=_params("arbitrary"),
        name="merge_router",
    )(o_attn, hm, mg, x, g1, sc2, sh2, nw, wa, wm, wo, rwh, rwl, rb, tri)


def _sc_mesh():
    return plsc.VectorSubcoreMesh(core_axis_name="core", subcore_axis_name="subcore")


def _sc_scatter_rows(rows, dest, n_out):
    n, w = rows.shape

    @pl.kernel(out_type=jax.ShapeDtypeStruct((n_out, w), rows.dtype), mesh=_sc_mesh(), scratch_types=[])
    def scatter(x_hbm, i_hbm, o_hbm):
        def body(x_vmem, i_vmem):
            pltpu.sync_copy(x_vmem, o_hbm.at[i_vmem.at[0]])

        pltpu.emit_pipeline(
            body,
            grid=(n // SC_WINDOW,),
            in_specs=[pl.BlockSpec((SC_WINDOW, w), lambda i: (i, 0)),
                      pl.BlockSpec((1, SC_WINDOW), lambda i: (0, i))],
            out_specs=[],
            core_axis_name=("core", "subcore"),
            dimension_semantics=(pltpu.PARALLEL,),
        )(x_hbm, i_hbm)

    return scatter(rows, dest.reshape(1, n))


def _sc_gather_rows(src, idx):
    n = idx.shape[0]
    w = src.shape[1]

    @pl.kernel(out_type=jax.ShapeDtypeStruct((n, w), src.dtype), mesh=_sc_mesh(), scratch_types=[])
    def gather(x_hbm, i_hbm, o_hbm):
        def body(i_vmem, o_vmem):
            pltpu.sync_copy(x_hbm.at[i_vmem.at[0]], o_vmem)

        pltpu.emit_pipeline(
            body,
            grid=(n // SC_WINDOW,),
            in_specs=[pl.BlockSpec((1, SC_WINDOW), lambda i: (0, i))],
            out_specs=[pl.BlockSpec((SC_WINDOW, w), lambda i: (i, 0))],
            core_axis_name=("core", "subcore"),
            dimension_semantics=(pltpu.PARALLEL,),
        )(i_hbm, o_hbm)

    return gather(src, idx.reshape(1, n))


def _row_index_kernel(ps_ref, route_ref, o_ref, *, n_rows):
    bucket = route_ref[0:1, :].astype(I32)
    start = jnp.zeros(bucket.shape, I32)
    for b in range(N_BUCKETS):
        start = jnp.where(bucket == b, ps_ref[b], start)
    dest = start + route_ref[3:4, :].astype(I32)
    part = lax.broadcasted_iota(I32, o_ref.shape, 0)
    o_ref[...] = part * n_rows + dest


def _row_index(pad_starts, route, n_rows):
    t = route.shape[1]
    tm = min(TILE_ROW_INDEX, t)
    return pl.pallas_call(
        functools.partial(_row_index_kernel, n_rows=n_rows),
        grid=(t // tm,),
        in_specs=[pl.BlockSpec(memory_space=pltpu.SMEM), pl.BlockSpec((SUBLANES, tm), lambda i: (0, i))],
        out_specs=pl.BlockSpec((SUBLANES, tm), lambda i: (0, i)),
        out_shape=jax.ShapeDtypeStruct((SUBLANES, t), I32),
        compiler_params=_params("arbitrary"),
        name="row_index",
    )(pad_starts, route)


def _residual_kernel(x_ref, y_ref, g2_ref, o_ref):
    y = _unpack_bf16_pairs(jnp.concatenate([y_ref[c] for c in range(OUT_PARTS)], axis=1))
    o_ref[...] = x_ref[...] + g2_ref[...] * y


def _residual(x1, ytok, g2, seq):
    t, d = x1.shape
    tm = min(TILE_RESIDUAL, seq)
    tpb = seq // tm
    return pl.pallas_call(
        _residual_kernel,
        grid=(t // tm,),
        in_specs=[
            pl.BlockSpec((tm, d), lambda i: (i, 0)),
            pl.BlockSpec((OUT_PARTS, tm, LANES), lambda i: (0, i, 0)),
            pl.BlockSpec((None, 1, d), lambda i: (i // tpb, 0, 0)),
        ],
        out_specs=pl.BlockSpec((tm, d), lambda i: (i, 0)),
        out_shape=jax.ShapeDtypeStruct((t, d), F32),
        compiler_params=_params("arbitrary"),
        name="residual",
    )(x1, ytok, g2)


def _expert_kernel(ea_ref, eb_ref, nr_ref, xs_ref, wga_ref, wua_ref, wda_ref, wgb_ref, wub_ref, wdb_ref, ys_ref,
                   ga_ref, ua_ref, gb_ref, ub_ref, dab_ref):
    j = pl.program_id(0)
    nr = nr_ref[0]
    prev = jnp.maximum(j - 1, 0)
    f = wda_ref.shape[0]

    @pl.when((j == 0) | (ea_ref[j] != ea_ref[prev]))
    def _():
        ga_ref[...] = wga_ref[...].astype(BF16)
        ua_ref[...] = wua_ref[...].astype(BF16)
        dab_ref[0:f, :] = wda_ref[...].astype(BF16)

    @pl.when((j == 0) | (eb_ref[j] != eb_ref[prev]))
    def _():
        gb_ref[...] = wgb_ref[...].astype(BF16)
        ub_ref[...] = wub_ref[...].astype(BF16)
        dab_ref[f:2 * f, :] = wdb_ref[...].astype(BF16)

    @pl.when(j < nr)
    def _():
        x = _unpack_bf16_pairs(jnp.concatenate([xs_ref[c] for c in range(PAY_PARTS - 1)], axis=1)).astype(BF16)
        gl = lax.bitcast_convert_type(xs_ref[PAY_PARTS - 1], F32)

        def gated_act(wg_ref, wu_ref, gate):
            gte = _dot(x, wg_ref[...])
            return (gte * _sigmoid(gte) * _dot(x, wu_ref[...]) * gate).astype(BF16)

        acts = jnp.concatenate([gated_act(ga_ref, ua_ref, gl[:, 0:1]), gated_act(gb_ref, ub_ref, gl[:, 1:2])], axis=1)
        y = _pack_bf16_pairs(_dot(acts, dab_ref[...]))
        for c in range(OUT_PARTS):
            ys_ref[c] = y[:, c * LANES:(c + 1) * LANES]

    @pl.when(j >= nr)
    def _():
        ys_ref[...] = jnp.zeros(ys_ref.shape, U32)


def _experts(blk_ea, blk_eb, n_real, xs, wg, wu, wd, d):
    n_rows = xs.shape[1]
    nblk = n_rows // EXPERT_BLOCK
    f = wg.shape[2]
    grid_spec = pltpu.PrefetchScalarGridSpec(
        num_scalar_prefetch=3,
        grid=(nblk,),
        in_specs=[
            pl.BlockSpec((PAY_PARTS, EXPERT_BLOCK, LANES), lambda j, ea, eb, nr: (0, j, 0)),
            pl.BlockSpec((None, d, f), lambda j, ea, eb, nr: (ea[j], 0, 0)),
            pl.BlockSpec((None, d, f), lambda j, ea, eb, nr: (ea[j], 0, 0)),
            pl.BlockSpec((None, f, d), lambda j, ea, eb, nr: (ea[j], 0, 0)),
            pl.BlockSpec((None, d, f), lambda j, ea, eb, nr: (eb[j], 0, 0)),
            pl.BlockSpec((None, d, f), lambda j, ea, eb, nr: (eb[j], 0, 0)),
            pl.BlockSpec((None, f, d), lambda j, ea, eb, nr: (eb[j], 0, 0)),
        ],
        out_specs=pl.BlockSpec((OUT_PARTS, EXPERT_BLOCK, LANES), lambda j, ea, eb, nr: (0, j, 0)),
        scratch_shapes=[pltpu.VMEM((d, f), BF16)] * 4 + [pltpu.VMEM((2 * f, d), BF16)],
    )
    return pl.pallas_call(
        _expert_kernel,
        grid_spec=grid_spec,
        out_shape=jax.ShapeDtypeStruct((OUT_PARTS, n_rows, LANES), U32),
        compiler_params=_params("arbitrary"),
        name="experts",
    )(blk_ea, blk_eb, n_real, xs, wg, wu, wd, wg, wu, wd)


_PAIR_A = (0, 0, 0, 1, 2, 2)
_PAIR_B = (1, 2, 3, 3, 3, 1)


def kernel(x, c, positions, ada_w, ada_b, norm_mix_w, norm_ffn_w, w_in, b_igate, b_fgate, q_norm_w, k_norm_w,
           sinks, conv_w, conv_b, mlstm_norm_w, w_attn_up, w_mlstm_up, w_out, router_w, router_bias,
           w_gate, w_up, w_down):
    batch, seq, d = x.shape
    depth = w_in.shape[0]
    t = batch * seq
    qw = N_HEADS * HEAD_DIM
    kvw = N_KV * HEAD_DIM
    mw = M_HEADS * M_DIM

    o = 0
    cols = {}
    for name, wdt in (("q", qw), ("k", kvw), ("v", kvw), ("mqk", 2 * mw), ("mv", mw), ("mi", M_HEADS),
                      ("mf", M_HEADS), ("mo", mw), ("ga", d), ("gb", d)):
        cols[name] = (o, o + wdt)
        o += wdt

    w_in_b = w_in.astype(BF16)

    def wspan(first, last):
        return w_in_b[:, :, cols[first][0]:cols[last][1]]

    w_a = wspan("q", "v")
    w_m = jnp.concatenate([wspan("mqk", "mv"), wspan("mo", "mo")], axis=2)
    w_g = jnp.concatenate([wspan("mi", "mf"), jnp.zeros((depth, d, LANES - 2 * M_HEADS), BF16)], axis=2)
    w_mg = wspan("ga", "gb")
    w_au = w_attn_up.astype(BF16)
    w_mu = w_mlstm_up.astype(BF16)
    w_o = w_out.astype(BF16)
    n_e = w_gate.shape[1]
    w_g8 = w_gate.reshape(depth * n_e, d, -1)
    w_u8 = w_up.reshape(depth * n_e, d, -1)
    w_d = w_down.reshape(depth * n_e, -1, d)

    rw_t = router_w.astype(F32).T
    rw_top = rw_t.astype(BF16)
    rw_hi = jnp.concatenate([rw_top, (rw_t - rw_top.astype(F32)).astype(BF16)], axis=0)
    rw_lo = rw_top
    rb = jnp.broadcast_to(router_bias.astype(F32)[:, None], (n_e, LANES))

    qn_w = jnp.tile(q_norm_w * (HEAD_DIM ** -0.5), (1, N_HEADS)).reshape(depth, 1, qw)
    kn_w = jnp.tile(k_norm_w, (1, N_KV)).reshape(depth, 1, kvw)
    seg = jnp.arange(qw) // HEAD_DIM
    bdq = jnp.where(seg[:, None] == seg[None, :], 1.0 / HEAD_DIM, 0.0).astype(BF16)
    bdk = bdq[:kvw, :kvw]

    inv_freq = ROPE_THETA ** (-(jnp.arange(0, ROPE_DIM, 2, dtype=F32) / ROPE_DIM))
    ang = positions.astype(F32).reshape(1, t) * inv_freq[:, None]
    cos8, sin8 = jnp.cos(ang).T, jnp.sin(ang).T
    pad1 = jnp.ones((t, HEAD_DIM - ROPE_DIM), F32)
    pad0 = jnp.zeros((t, HEAD_DIM - ROPE_DIM), F32)
    cos_t = jnp.tile(jnp.concatenate([cos8, cos8, pad1], axis=1), (1, LANES // HEAD_DIM))
    sin_t = jnp.tile(jnp.concatenate([-sin8, sin8, pad0], axis=1), (1, LANES // HEAD_DIM))

    gate_bias = jnp.concatenate([b_igate, b_fgate], axis=1).astype(F32)
    bcol = jnp.broadcast_to(gate_bias[:, :, None], (depth, 2 * M_HEADS, LANES))

    tm_merge = min(TILE_MERGE, seq)
    ii = jnp.arange(tm_merge)
    tri = (ii[:, None] <= ii[None, :]).astype(BF16)

    n_blk = (t + N_BUCKETS * (EXPERT_BLOCK - 1)) // EXPERT_BLOCK + 1
    n_rows = n_blk * EXPERT_BLOCK
    pair_a = jnp.asarray(_PAIR_A, I32)
    pair_b = jnp.asarray(_PAIR_B, I32)

    c_pad = jnp.zeros((SUBLANES, d), F32).at[:batch].set(c)
    mod = _ada_mod(c_pad, ada_w, ada_b)[:, :batch]

    xf = x.reshape(t, d)
    moe = None
    for l in range(depth):
        sh1, sc1, g1, sh2, sc2, g2 = [m.reshape(batch, 1, d) for m in jnp.split(mod[l], 6, axis=-1)]

        outs = _inproj(xf, moe, sc1, sh1, norm_mix_w.reshape(depth, 1, d), w_a, w_m, w_g, w_mg, l, seq)
        a_in, m_in, mg, grow = outs[:4]
        if moe is not None:
            xf = outs[4]
        o_attn = _attention(a_in, cos_t, sin_t, sinks[l], qn_w[l], kn_w[l], bdq, bdk, batch, seq)
        hm = _mlstm(m_in, grow, conv_w[l], conv_b[l].reshape(1, -1), bcol[l], mlstm_norm_w[l].reshape(1, mw),
                    batch, seq)
        x1, pay, route, cnt = _merge(o_attn, hm, mg, xf, g1, sc2, sh2, norm_ffn_w.reshape(depth, 1, d),
                                     w_au, w_mu, w_o, rw_hi, rw_lo, rb, tri, l, seq)

        counts = cnt[:N_BUCKETS, 0].astype(I32)
        padded = (counts + EXPERT_BLOCK - 1) // EXPERT_BLOCK * EXPERT_BLOCK
        pad_ends = jnp.cumsum(padded)
        pad_starts = pad_ends - padded
        row_idx = _row_index(jnp.concatenate([pad_starts, jnp.zeros((BUCKET_ROWS - N_BUCKETS,), I32)]), route, n_rows)
        blk_start = jnp.arange(n_blk, dtype=I32) * EXPERT_BLOCK
        blk_bucket = jnp.minimum(jnp.sum((pad_ends[None, :] <= blk_start[:, None]).astype(I32), axis=1), N_BUCKETS - 1)
        grp = blk_bucket // N_PAIRS
        blk_ea = (l * n_e + grp * EPG + pair_a[blk_bucket % N_PAIRS]).astype(I32)
        blk_eb = (l * n_e + grp * EPG + pair_b[blk_bucket % N_PAIRS]).astype(I32)
        n_real = (pad_ends[-1:] // EXPERT_BLOCK).astype(I32)

        xs = _sc_scatter_rows(pay.reshape(PAY_PARTS * t, LANES), row_idx[:PAY_PARTS].reshape(-1),
                              PAY_PARTS * n_rows).reshape(PAY_PARTS, n_rows, LANES)
        ys = _experts(blk_ea, blk_eb, n_real, xs, w_g8, w_u8, w_d, d)
        ytok = _sc_gather_rows(ys.reshape(OUT_PARTS * n_rows, LANES),
                               row_idx[:OUT_PARTS].reshape(-1)).reshape(OUT_PARTS, t, LANES)
        xf, moe = x1, (ytok, g2)
    return _residual(xf, moe[0], moe[1], seq).reshape(batch, seq, d)
```

```python
import functools

import jax
import jax.numpy as jnp
from jax import lax
from jax.experimental import pallas as pl
from jax.experimental.pallas import tpu as pltpu
from jax.experimental.pallas import tpu_sc as plsc

F32 = jnp.float32
BF16 = jnp.bfloat16
U32 = jnp.uint32
I32 = jnp.int32
HIGHEST = lax.Precision.HIGHEST

HEAD_DIM = 64
N_HEADS = 8
N_KV = 2
ROPE_DIM = 16
ROPE_THETA = 500000.0
ATTN_BLOCK = 128
M_HEADS = 4
M_DIM = 128
CONV_K = 4
N_EXPERTS = 16
N_GROUPS = 4
EPG = 4
EPS = 1e-6

LANES = 128
SUBLANES = 8

TILE_INPROJ = 1024
TILE_ATTN = 1024
TILE_MLSTM = 2048
TILE_MERGE = 1024
TILE_RESIDUAL = 1024
TILE_ROW_INDEX = 8192
ADA_COLS = 3072
CHUNK = 128
CHUNKS_PER_STEP = 4
GATE_CHUNKS_PER_STEP = 4
N_PAIRS = 6
N_BUCKETS = N_GROUPS * N_PAIRS
BUCKET_ROWS = 32
EXPERT_BLOCK = 512
PAY_PARTS = 5
OUT_PARTS = 4
SC_WINDOW = 128
GATHER_PARTS = 2
VMEM_LIMIT = 56 * 1024 * 1024


def _dot(a, b, precision=None):
    return jnp.dot(a, b, preferred_element_type=F32, precision=precision)


def _dot_nt(a, b):
    return lax.dot_general(a, b, (((1,), (1,)), ((), ())), preferred_element_type=F32)


def _sigmoid(x):
    return 1.0 / (1.0 + jnp.exp(-x))


def _log_sigmoid(x):
    return jnp.minimum(x, 0.0) - jnp.log1p(jnp.exp(-jnp.abs(x)))


def _pack_bf16_pairs(v, rounded=False):
    n = v.shape[1] // 2
    bits = lax.bitcast_convert_type(v if rounded else v.astype(BF16).astype(F32), U32)
    return bits[:, :n] | (bits[:, n:] >> 16)


def _unpack_bf16_pairs(w):
    hi = lax.bitcast_convert_type(w & jnp.uint32(0xFFFF0000), F32)
    lo = lax.bitcast_convert_type(w << 16, F32)
    return jnp.concatenate([hi, lo], axis=1)


def _params(*sem):
    return pltpu.CompilerParams(dimension_semantics=sem, vmem_limit_bytes=VMEM_LIMIT)


def _ada_kernel(c_ref, w_ref, b_ref, o_ref):
    c = c_ref[...]
    ca = c * _sigmoid(c)
    o_ref[0] = _dot(ca, w_ref[0], HIGHEST) + b_ref[0]


def _ada_mod(c_pad, ada_w, ada_b):
    depth, d, n = ada_w.shape
    tn = ADA_COLS
    return pl.pallas_call(
        _ada_kernel,
        grid=(depth, n // tn),
        in_specs=[
            pl.BlockSpec((SUBLANES, d), lambda l, j: (0, 0)),
            pl.BlockSpec((1, d, tn), lambda l, j: (l, 0, j)),
            pl.BlockSpec((1, 1, tn), lambda l, j: (l, 0, j)),
        ],
        out_specs=pl.BlockSpec((1, SUBLANES, tn), lambda l, j: (l, 0, j)),
        out_shape=jax.ShapeDtypeStruct((depth, SUBLANES, n), F32),
        compiler_params=_params("arbitrary", "arbitrary"),
        name="ada_mod",
    )(c_pad, ada_w, ada_b.reshape(depth, 1, n))


def _inproj_kernel(*refs, fuse_residual, n_carry=0):
    if n_carry:
        refs = refs[:10] + refs[10 + n_carry:]
    if fuse_residual:
        (x_ref, y_ref, g2_ref, sc_ref, sh_ref, nw_ref, wa_ref, wm_ref, wg_ref, wmg_ref,
         a_ref, m_ref, mg_ref, gr_ref, xo_ref, g_ref) = refs
        y = _unpack_bf16_pairs(jnp.concatenate([y_ref[c] for c in range(OUT_PARTS)], axis=1))
        x = x_ref[...] + g2_ref[...] * y
        xo_ref[...] = x
    else:
        (x_ref, sc_ref, sh_ref, nw_ref, wa_ref, wm_ref, wg_ref, wmg_ref,
         a_ref, m_ref, mg_ref, gr_ref, g_ref) = refs
        x = x_ref[...]
    ms = jnp.mean(x * x, axis=-1, keepdims=True)
    h = x * lax.rsqrt(ms + EPS) * (nw_ref[...] * (1.0 + sc_ref[...])) + sh_ref[...]
    hb = h.astype(BF16)
    a_ref[...] = _dot(hb, wa_ref[...]).astype(BF16)
    m_ref[...] = _dot(hb, wm_ref[...]).astype(BF16)
    mg_ref[...] = _sigmoid(_dot(hb, wmg_ref[...])).astype(BF16)
    g_ref[...] = _dot(hb, wg_ref[...])
    gr_ref[...] = g_ref[...].T[:SUBLANES, :]


def _inproj(x, moe, sc, sh, nw, wa, wm, wg, wmg, l, seq, part=None, carry=None):
    t, d = x.shape
    tm = min(TILE_INPROJ, seq)
    tpb = seq // tm
    p, n_part = part if part is not None else (0, 1)
    steps = t // tm // n_part
    off = p * steps
    row = lambda i: (i + off, 0)
    bsel = lambda i: ((i + off) // tpb, 0, 0)
    wsel = lambda i: (l, 0, 0)
    once = pl.Buffered(1)
    na, nm, ng, nmg = wa.shape[2], wm.shape[2], wg.shape[2], wmg.shape[2]
    fuse = moe is not None
    moe_specs = [pl.BlockSpec((OUT_PARTS, tm, LANES), lambda i: (0, i, 0)), pl.BlockSpec((None, 1, d), bsel)]
    carry = list(carry) if carry is not None else []
    aliases = {10 + k: k for k in range(len(carry))}
    if part is not None:
        aliases[0] = 4
    return pl.pallas_call(
        functools.partial(_inproj_kernel, fuse_residual=fuse, n_carry=len(carry)),
        grid=(steps,),
        in_specs=[pl.BlockSpec((tm, d), row)] + (moe_specs if fuse else []) + [
            pl.BlockSpec((None, 1, d), bsel),
            pl.BlockSpec((None, 1, d), bsel),
            pl.BlockSpec((None, 1, d), wsel),
            pl.BlockSpec((None, d, na), wsel, pipeline_mode=once),
            pl.BlockSpec((None, d, nm), wsel, pipeline_mode=once),
            pl.BlockSpec((None, d, ng), wsel, pipeline_mode=once),
            pl.BlockSpec((None, d, nmg), wsel, pipeline_mode=once),
        ] + [pl.BlockSpec(memory_space=pl.ANY)] * len(carry),
        out_specs=[
            pl.BlockSpec((tm, na), row),
            pl.BlockSpec((tm, nm), row),
            pl.BlockSpec((tm, nmg), row),
            pl.BlockSpec((SUBLANES, tm), lambda i: (0, i + off)),
        ] + ([pl.BlockSpec((tm, d), row)] if fuse else []),
        out_shape=[
            jax.ShapeDtypeStruct((t, na), BF16),
            jax.ShapeDtypeStruct((t, nm), BF16),
            jax.ShapeDtypeStruct((t, nmg), BF16),
            jax.ShapeDtypeStruct((SUBLANES, t), F32),
        ] + ([jax.ShapeDtypeStruct((t, d), F32)] if fuse else []),
        scratch_shapes=[pltpu.VMEM((tm, ng), F32)],
        input_output_aliases=aliases,
        compiler_params=_params("arbitrary"),
        name="inproj",
    )(x, *(moe if fuse else ()), sc, sh, nw, wa, wm, wg, wmg, *carry)


def _rope(t, cos, sin):
    w = t.shape[1]
    reps = w // LANES
    cosw = jnp.concatenate([cos] * reps, axis=1) if reps > 1 else cos
    sinw = jnp.concatenate([sin] * reps, axis=1) if reps > 1 else sin
    lane = lax.broadcasted_iota(I32, t.shape, 1)
    half = ROPE_DIM // 2
    up = pltpu.roll(t, w - half, axis=1)
    dn = pltpu.roll(t, half, axis=1)
    partner = jnp.where((lane % ROPE_DIM) < half, up, dn)
    return t * cosw + partner * sinw


def _head_norm(t, bd, w):
    ms = _dot((t * t).astype(BF16), bd)
    return t * lax.rsqrt(ms + EPS) * w


def _attn_kernel(sink_ref, cur_ref, prev_ref, cos_ref, sin_ref, cosp_ref, sinp_ref,
                 qw_ref, kw_ref, bdq_ref, bdk_ref, o_ref):
    tq = cur_ref.shape[0]
    nj = tq // ATTN_BLOCK
    qw = N_HEADS * HEAD_DIM
    kw = N_KV * HEAD_DIM
    blk0 = pl.program_id(1) * nj

    cur = cur_ref[...]
    q = cur[:, :qw].astype(F32)
    kc = cur[:, qw:qw + kw].astype(F32)
    vc = cur[:, qw + kw:].astype(F32)
    prev = prev_ref[...]
    kp = prev[:, :kw].astype(F32)
    vp = prev[:, kw:].astype(F32)

    cos, sin = cos_ref[...], sin_ref[...]
    q = _rope(_head_norm(q, bdq_ref[...], qw_ref[...]), cos, sin)
    kc = _rope(_head_norm(kc, bdk_ref[...], kw_ref[...]), cos, sin)
    kp = _rope(_head_norm(kp, bdk_ref[...], kw_ref[...]), cosp_ref[...], sinp_ref[...])
    qb = q.astype(BF16)

    def both_halves(x2):
        swapped = pltpu.roll(x2, HEAD_DIM, axis=1)
        first = lax.broadcasted_iota(I32, x2.shape, 1) < HEAD_DIM
        return jnp.concatenate([jnp.where(first, x2, swapped), jnp.where(first, swapped, x2)], axis=1).astype(BF16)

    k_all = both_halves(jnp.concatenate([kp, kc], axis=0))
    v_all = both_halves(jnp.concatenate([vp, vc], axis=0))

    lane = lax.broadcasted_iota(I32, (ATTN_BLOCK, LANES), 1)
    lo = lane < HEAD_DIM
    zero = jnp.zeros((ATTN_BLOCK, LANES), BF16)
    g_heads = N_HEADS // N_KV
    ri = lax.broadcasted_iota(I32, (g_heads * ATTN_BLOCK, ATTN_BLOCK), 0) % ATTN_BLOCK
    ci = lax.broadcasted_iota(I32, (g_heads * ATTN_BLOCK, ATTN_BLOCK), 1)
    from_prev = ci > ri
    head_row = lax.broadcasted_iota(I32, (g_heads * ATTN_BLOCK, 1), 0) // ATTN_BLOCK
    ones_v = jnp.ones((2 * ATTN_BLOCK, LANES), BF16)

    tiles = [(j, g) for j in range(nj) for g in range(N_KV)]
    scores = {}
    for j, g in tiles:
        rows = slice(j * ATTN_BLOCK, (j + 1) * ATTN_BLOCK)
        band = slice(j * ATTN_BLOCK, (j + 2) * ATTN_BLOCK)
        qp0 = qb[rows, (2 * g) * LANES:(2 * g + 1) * LANES]
        qp1 = qb[rows, (2 * g + 1) * LANES:(2 * g + 2) * LANES]
        q4 = jnp.concatenate([jnp.where(lo, qp0, zero), jnp.where(lo, zero, qp0),
                              jnp.where(lo, qp1, zero), jnp.where(lo, zero, qp1)], axis=0)
        scores[j, g] = _dot_nt(q4, k_all[band, g * LANES:(g + 1) * LANES])

    probs, sink_term = {}, {}
    for j, g in tiles:
        s2 = scores[j, g]
        if j == 0:
            prev_ok = ci > ri + (1 - jnp.minimum(blk0, 1)) * ATTN_BLOCK
            s = jnp.where(prev_ok, s2[:, :ATTN_BLOCK], jnp.where(from_prev, -jnp.inf, s2[:, ATTN_BLOCK:]))
        else:
            s = jnp.where(from_prev, s2[:, :ATTN_BLOCK], s2[:, ATTN_BLOCK:])
        sink = jnp.full((g_heads * ATTN_BLOCK, 1), sink_ref[g_heads * g], F32)
        for r in range(1, g_heads):
            sink = jnp.where(head_row == r, sink_ref[g_heads * g + r], sink)
        m = jnp.maximum(jnp.max(s, axis=-1, keepdims=True), sink)
        p = jnp.exp(s - m)
        probs[j, g] = jnp.concatenate([jnp.where(from_prev, p, 0.0), jnp.where(from_prev, 0.0, p)],
                                      axis=1).astype(BF16)
        sink_term[j, g] = jnp.exp(sink - m)

    for j, g in tiles:
        rows = slice(j * ATTN_BLOCK, (j + 1) * ATTN_BLOCK)
        band = slice(j * ATTN_BLOCK, (j + 2) * ATTN_BLOCK)
        o8 = _dot(probs[j, g], jnp.concatenate([v_all[band, g * LANES:(g + 1) * LANES], ones_v], axis=1))
        o4 = o8[:, :LANES] / (o8[:, LANES:] + sink_term[j, g])
        b = ATTN_BLOCK
        o_ref[rows, (2 * g) * LANES:(2 * g + 1) * LANES] = jnp.where(lo, o4[0:b], o4[b:2 * b]).astype(BF16)
        o_ref[rows, (2 * g + 1) * LANES:(2 * g + 2) * LANES] = jnp.where(
            lo, o4[2 * b:3 * b], o4[3 * b:4 * b]).astype(BF16)


def _attention(a_in, cos_t, sin_t, sinks_l, qw, kw, bdq, bdk, batch, seq):
    t = a_in.shape[0]
    tq = min(TILE_ATTN, seq)
    nj = tq // ATTN_BLOCK
    tpb = seq // tq
    bpb = seq // ATTN_BLOCK
    qwid = N_HEADS * HEAD_DIM
    kvw = 2 * N_KV * HEAD_DIM
    cur = lambda b, i: (b * tpb + i, 0)
    prv = lambda b, i: (b * bpb + jnp.maximum(i * nj - 1, 0), qwid // kvw)
    prv0 = lambda b, i: (b * bpb + jnp.maximum(i * nj - 1, 0), 0)
    const = lambda b, i: (0, 0)
    return pl.pallas_call(
        _attn_kernel,
        grid=(batch, tpb),
        in_specs=[
            pl.BlockSpec(memory_space=pltpu.SMEM),
            pl.BlockSpec((tq, qwid + kvw), cur),
            pl.BlockSpec((ATTN_BLOCK, kvw), prv),
            pl.BlockSpec((tq, LANES), cur),
            pl.BlockSpec((tq, LANES), cur),
            pl.BlockSpec((ATTN_BLOCK, LANES), prv0),
            pl.BlockSpec((ATTN_BLOCK, LANES), prv0),
            pl.BlockSpec((1, qwid), const),
            pl.BlockSpec((1, kvw // 2), const),
            pl.BlockSpec((qwid, qwid), const),
            pl.BlockSpec((kvw // 2, kvw // 2), const),
        ],
        out_specs=pl.BlockSpec((tq, qwid), cur),
        out_shape=jax.ShapeDtypeStruct((t, qwid), BF16),
        compiler_params=_params("arbitrary", "arbitrary"),
        name="swa_attention",
    )(sinks_l, a_in, a_in, cos_t, sin_t, cos_t, sin_t, qw, kw, bdq, bdk)


def _mlstm_kernel(min_ref, gr_ref, cw_ref, cb_ref, bcol_ref, nw_ref,
                  hm_ref, ext_ref, q_ref, kt_ref, st_ref, mx_ref, ab_ref, bc_ref):
    tt = min_ref.shape[0]
    mw = M_HEADS * M_DIM
    nchunks = tt // CHUNK

    @pl.when(pl.program_id(1) == 0)
    def _():
        ext_ref[0:SUBLANES, :] = jnp.zeros((SUBLANES, 2 * mw), F32)
        st_ref[...] = jnp.zeros(st_ref.shape, F32)
        mx_ref[...] = jnp.zeros(mx_ref.shape, F32)

    def conv_block(cols):
        u = min_ref[:, cols].astype(F32)
        ext_ref[SUBLANES:SUBLANES + tt, cols] = u
        acc = cb_ref[:, cols] + cw_ref[CONV_K - 1:CONV_K, cols] * u
        for jj in range(CONV_K - 1):
            off = SUBLANES - (CONV_K - 1) + jj
            acc = acc + cw_ref[jj:jj + 1, cols] * ext_ref[off:off + tt, cols]
        ext_ref[0:SUBLANES, cols] = u[tt - SUBLANES:tt, :]
        return acc * _sigmoid(acc)

    def q_body(h, carry):
        cols = pl.ds(pl.multiple_of(h * M_DIM, M_DIM), M_DIM)
        q_ref[:, cols] = conv_block(cols).astype(BF16)
        return carry

    def k_body(h, carry):
        off = pl.multiple_of(h * M_DIM, M_DIM)
        act = conv_block(pl.ds(mw + off, M_DIM)) * (M_DIM ** -0.5)
        for j in range(nchunks):
            kt_ref[pl.ds(off, M_DIM), j * CHUNK:(j + 1) * CHUNK] = act[j * CHUNK:(j + 1) * CHUNK, :].T
        return carry

    lax.fori_loop(0, M_HEADS, q_body, 0)
    lax.fori_loop(0, M_HEADS, k_body, 0)

    ri = lax.broadcasted_iota(I32, (CHUNK, CHUNK), 0)
    ci = lax.broadcasted_iota(I32, (CHUNK, CHUNK), 1)
    causal = ci <= ri
    triu = jnp.where(ri <= ci, 1.0, 0.0).astype(BF16)
    ones_half = jnp.ones((CHUNK, M_DIM), BF16)
    mean_mat = jnp.full((M_DIM, M_DIM), 1.0 / M_DIM, BF16)
    sub = lax.broadcasted_iota(I32, (SUBLANES, CHUNK), 0)
    heads = range(M_HEADS)

    pad_rows = jnp.zeros((CHUNK - SUBLANES, CHUNK), F32)
    zero_rows = jnp.zeros((SUBLANES, CHUNK), F32)

    def gate_body(jg, carry):
        for u_ in range(GATE_CHUNKS_PER_STEP):
            rs = pl.ds(pl.multiple_of((jg * GATE_CHUNKS_PER_STEP + u_) * CHUNK, CHUNK), CHUNK)
            gr = gr_ref[:, rs] + bcol_ref[...]
            ls = _log_sigmoid(gr)
            ls1 = ls.astype(BF16).astype(F32)
            ls2 = (ls - ls1).astype(BF16).astype(F32)
            pieces = jnp.concatenate([ls1, ls2, ls - ls1 - ls2, zero_rows], axis=0).astype(BF16)
            sums = _dot(pieces, triu)
            br = sums[0:SUBLANES] + sums[SUBLANES:2 * SUBLANES] + sums[2 * SUBLANES:3 * SUBLANES]
            ab = jnp.where(sub < M_HEADS, gr - pltpu.roll(br, M_HEADS, axis=0), br)
            ab_ref[:, rs] = ab
            bc_ref[rs, :] = jnp.concatenate([ab, pad_rows], axis=0).T
        return carry

    lax.fori_loop(0, nchunks // GATE_CHUNKS_PER_STEP, gate_body, 0)

    def group_body(cg, carry):
        rows, ab = [], []
        for u_ in range(CHUNKS_PER_STEP):
            r0 = pl.multiple_of((cg * CHUNKS_PER_STEP + u_) * CHUNK, CHUNK)
            rows.append(pl.ds(r0, CHUNK))
            ab.append(ab_ref[:, rows[u_]])
        lanes = [(u_, h) for u_ in range(CHUNKS_PER_STEP) for h in heads]
        a_r = {(u_, h): ab[u_][h:h + 1, :] for u_, h in lanes}
        b_last = {(u_, h): ab[u_][M_HEADS + h:M_HEADS + h + 1, CHUNK - 1:CHUNK] for u_, h in lanes}

        m_prev, a_max, a_dec, s_in = {}, {}, {}, {}
        m_run = [mx_ref[h][0:1, 0:1] for h in heads]
        for k in lanes:
            u_, h = k
            m_prev[k] = m_run[h]
            a_max[k] = jnp.max(a_r[k], axis=-1, keepdims=True)
            m_loc = b_last[k] + a_max[k]
            m_new = jnp.maximum(b_last[k] + m_prev[k], m_loc)
            a_dec[k] = jnp.exp(b_last[k] + m_prev[k] - m_new)
            s_in[k] = jnp.exp(m_loc - m_new)
            m_run[h] = m_new
        for h in heads:
            mx_ref[h] = jnp.broadcast_to(m_run[h], (SUBLANES, LANES))

        q, v_ext, s_qk, kv = {}, {}, {}, {}
        for k in lanes:
            u_, h = k
            rs = rows[u_]
            q[k] = q_ref[rs, h * M_DIM:(h + 1) * M_DIM]
            kt = kt_ref[h * M_DIM:(h + 1) * M_DIM, rs]
            v = min_ref[rs, 2 * mw + h * M_DIM:2 * mw + (h + 1) * M_DIM]
            v_ext[k] = jnp.concatenate([v, ones_half], axis=1)
            s_qk[k] = _dot(q[k], kt.astype(BF16))
            e_r = jnp.exp(a_r[k] - a_max[k])
            kv[k] = _dot((kt * e_r).astype(BF16), v_ext[k])

        thr, qk = {}, {}
        for k in lanes:
            u_, h = k
            a_mat = jnp.where(causal, a_r[k], -jnp.inf)
            mu = jnp.maximum(jnp.max(a_mat, axis=-1, keepdims=True), m_prev[k])
            b_c = bc_ref[rows[u_], M_HEADS + h:M_HEADS + h + 1]
            thr[k] = jnp.broadcast_to(jnp.exp(-(b_c + mu)), (CHUNK, M_DIM))
            mu_b = jnp.broadcast_to(mu, (CHUNK, CHUNK))
            inter = jnp.exp(m_prev[k] - mu_b)
            qk[k] = jnp.concatenate([(s_qk[k] * jnp.exp(a_mat - mu_b)).astype(BF16),
                                     (q[k].astype(F32) * inter).astype(BF16)], axis=1)

        state = [st_ref[h] for h in heads]
        for k in lanes:
            u_, h = k
            hs = slice(h * M_DIM, (h + 1) * M_DIM)
            num = _dot(qk[k], jnp.concatenate([v_ext[k], state[h].astype(BF16)], axis=0))
            state[h] = a_dec[k] * state[h] + s_in[k] * kv[k]
            den = jnp.maximum(jnp.abs(num[:, M_DIM:]), thr[k])
            hh = num[:, :M_DIM] / den
            msq = _dot((hh * hh).astype(BF16), mean_mat)
            hn = hh * lax.rsqrt(msq + EPS) * nw_ref[:, hs]
            og = min_ref[rows[u_], 3 * mw + h * M_DIM:3 * mw + (h + 1) * M_DIM].astype(F32)
            hm_ref[rows[u_], hs] = (_sigmoid(og) * hn).astype(BF16)
        for h in heads:
            st_ref[h] = state[h]
        return carry

    lax.fori_loop(0, nchunks // CHUNKS_PER_STEP, group_body, 0)


def _mlstm(m_in, grow, conv_w, conv_b, bcol, nw, batch, seq):
    t = m_in.shape[0]
    tt = min(TILE_MLSTM, seq)
    tpb = seq // tt
    mw = M_HEADS * M_DIM
    cur = lambda b, i: (b * tpb + i, 0)
    const = lambda b, i: (0, 0)
    return pl.pallas_call(
        _mlstm_kernel,
        grid=(batch, tpb),
        in_specs=[
            pl.BlockSpec((tt, 4 * mw), cur),
            pl.BlockSpec((SUBLANES, tt), lambda b, i: (0, b * tpb + i)),
            pl.BlockSpec((CONV_K, 2 * mw), const),
            pl.BlockSpec((1, 2 * mw), const),
            pl.BlockSpec((SUBLANES, LANES), const),
            pl.BlockSpec((1, mw), const),
        ],
        out_specs=pl.BlockSpec((tt, mw), cur),
        out_shape=jax.ShapeDtypeStruct((t, mw), BF16),
        scratch_shapes=[
            pltpu.VMEM((tt + SUBLANES, 2 * mw), F32),
            pltpu.VMEM((tt, mw), BF16),
            pltpu.VMEM((mw, tt), F32),
            pltpu.VMEM((M_HEADS, M_DIM, 2 * M_DIM), F32),
            pltpu.VMEM((M_HEADS, SUBLANES, LANES), F32),
            pltpu.VMEM((SUBLANES, tt), F32),
            pltpu.VMEM((tt, LANES), F32),
        ],
        compiler_params=_params("arbitrary", "arbitrary"),
        name="mlstm",
    )(m_in, grow, conv_w, conv_b, bcol, nw)


def _merge_kernel(o_ref, hm_ref, mg_ref, x_ref, g1_ref, sc_ref, sh_ref, nw_ref,
                  wa_ref, wm_ref, wo_ref, rwh_ref, rwl_ref, rb_ref, tri_ref,
                  x1_ref, pay_ref, route_ref, cnt_ref, carry_ref):
    tm, d = x_ref.shape

    @pl.when(pl.program_id(0) == 0)
    def _():
        carry_ref[...] = jnp.zeros(carry_ref.shape, F32)

    ya = _dot(o_ref[...], wa_ref[...])
    yb = _dot(hm_ref[...], wm_ref[...])
    mg = mg_ref[...]
    merged = mg[:, :d].astype(F32) * ya + mg[:, d:].astype(F32) * yb
    x1 = x_ref[...] + g1_ref[...] * _dot(merged.astype(BF16), wo_ref[...])
    x1_ref[...] = x1

    ms = jnp.mean(x1 * x1, axis=-1, keepdims=True)
    h2 = x1 * lax.rsqrt(ms + EPS) * (nw_ref[...] * (1.0 + sc_ref[...])) + sh_ref[...]
    hi = h2.astype(BF16)
    hif = hi.astype(F32)
    lo = (h2 - hif).astype(BF16)
    r_hi = _dot_nt(rwh_ref[...], hi)
    r_lo = _dot_nt(rwl_ref[...], lo)
    sc_t = _sigmoid(r_hi[:N_EXPERTS] + r_hi[N_EXPERTS:] + r_lo)
    sel_t = sc_t + rb_ref[:, 0:1]

    def row(a, e):
        return a[e:e + 1, :]

    best = None
    gi = jnp.zeros((1, tm), I32)
    for g in range(N_GROUPS):
        r = [row(sel_t, EPG * g + i) for i in range(EPG)]
        gs = None
        for i in range(EPG):
            for j in range(i + 1, EPG):
                pr = r[i] + r[j]
                gs = pr if gs is None else jnp.maximum(gs, pr)
        if best is None:
            best = gs
        else:
            upd = gs > best
            gi = jnp.where(upd, g, gi)
            best = jnp.maximum(best, gs)

    def pick(a, i):
        out = row(a, i)
        for g in range(1, N_GROUPS):
            out = jnp.where(gi == g, row(a, EPG * g + i), out)
        return out

    v = [pick(sel_t, i) for i in range(EPG)]
    s = [pick(sc_t, i) for i in range(EPG)]

    def argmax4(vals):
        bv, bi = vals[0], jnp.zeros((1, tm), I32)
        for i in range(1, EPG):
            upd = vals[i] > bv
            bi = jnp.where(upd, i, bi)
            bv = jnp.maximum(bv, vals[i])
        return bi

    i1 = argmax4(v)
    i2 = argmax4([jnp.where(i1 == i, -jnp.inf, v[i]) for i in range(EPG)])
    ia = jnp.minimum(i1, i2)
    ib = jnp.maximum(i1, i2)
    pidx = jnp.where(ia == 0, ib - 1, jnp.where(ia == 1, jnp.where(ib == 3, 3, 5), 4))
    bucket = gi * N_PAIRS + pidx

    def by_index(vals, idx):
        out = vals[0]
        for i in range(1, EPG):
            out = jnp.where(idx == i, vals[i], out)
        return out

    swap = pidx == N_PAIRS - 1
    s_lo, s_hi = by_index(s, ia), by_index(s, ib)
    s_a, s_b = jnp.where(swap, s_hi, s_lo), jnp.where(swap, s_lo, s_hi)
    gate_a = s_a / (s_a + s_b)
    gate_b = s_b / (s_a + s_b)

    brow = lax.broadcasted_iota(I32, (BUCKET_ROWS, tm), 0)
    onehot = brow == bucket
    cums = _dot(jnp.where(onehot, 1.0, 0.0).astype(BF16), tri_ref[...])
    carry = carry_ref[...]
    rank = jnp.sum(jnp.where(onehot, carry[:, 0:1] + cums, 0.0), axis=0, keepdims=True) - 1.0
    new_carry = carry + cums[:, tm - 1:tm]
    carry_ref[...] = new_carry
    cnt_ref[...] = new_carry

    route_ref[...] = jnp.concatenate(
        [bucket.astype(F32), gate_a, gate_b, rank, jnp.zeros((SUBLANES - 4, tm), F32)], axis=0)

    half = d // 2
    packed = _pack_bf16_pairs(hif, rounded=True)
    for cpart in range(half // LANES):
        pay_ref[cpart] = packed[:, cpart * LANES:(cpart + 1) * LANES]
    gates_t = jnp.concatenate([gate_a, gate_b, jnp.zeros((LANES - 2, tm), F32)], axis=0)
    pay_ref[half // LANES] = lax.bitcast_convert_type(gates_t.T, U32)


def _merge(o_attn, hm, mg, x, g1, sc2, sh2, nw, wa, wm, wo, rwh, rwl, rb, tri, l, seq):
    t, d = x.shape
    tm = tri.shape[0]
    tpb = seq // tm
    row = lambda i: (i, 0)
    bsel = lambda i: (i // tpb, 0, 0)
    wsel = lambda i: (l, 0, 0)
    const = lambda i: (0, 0)
    hw = o_attn.shape[1]
    return pl.pallas_call(
        _merge_kernel,
        grid=(t // tm,),
        in_specs=[
            pl.BlockSpec((tm, hw), row),
            pl.BlockSpec((tm, hw), row),
            pl.BlockSpec((tm, 2 * d), row),
            pl.BlockSpec((tm, d), row),
            pl.BlockSpec((None, 1, d), bsel),
            pl.BlockSpec((None, 1, d), bsel),
            pl.BlockSpec((None, 1, d), bsel),
            pl.BlockSpec((None, 1, d), wsel),
            pl.BlockSpec((None, hw, d), wsel),
            pl.BlockSpec((None, hw, d), wsel),
            pl.BlockSpec((None, d, d), wsel),
            pl.BlockSpec((2 * N_EXPERTS, d), const),
            pl.BlockSpec((N_EXPERTS, d), const),
            pl.BlockSpec((N_EXPERTS, LANES), const),
            pl.BlockSpec((tm, tm), const),
        ],
        out_specs=[
            pl.BlockSpec((tm, d), row),
            pl.BlockSpec((PAY_PARTS, tm, LANES), lambda i: (0, i, 0)),
            pl.BlockSpec((SUBLANES, tm), lambda i: (0, i)),
            pl.BlockSpec((BUCKET_ROWS, LANES), const),
        ],
        out_shape=[
            jax.ShapeDtypeStruct((t, d), F32),
            jax.ShapeDtypeStruct((PAY_PARTS, t, LANES), U32),
            jax.ShapeDtypeStruct((SUBLANES, t), F32),
            jax.ShapeDtypeStruct((BUCKET_ROWS, LANES), F32),
        ],
        scratch_shapes=[pltpu.VMEM((BUCKET_ROWS, LANES), F32)],
        compiler_params=_params("arbitrary"),
        name="merge_router",
    )(o_attn, hm, mg, x, g1, sc2, sh2, nw, wa, wm, wo, rwh, rwl, rb, tri)


def _sc_mesh():
    return plsc.VectorSubcoreMesh(core_axis_name="core", subcore_axis_name="subcore")


def _sc_scatter_rows(rows, dest, n_out):
    n, w = rows.shape

    @pl.kernel(out_type=jax.ShapeDtypeStruct((n_out, w), rows.dtype), mesh=_sc_mesh(), scratch_types=[])
    def scatter(x_hbm, i_hbm, o_hbm):
        def body(x_vmem, i_vmem):
            pltpu.sync_copy(x_vmem, o_hbm.at[i_vmem.at[0]])

        pltpu.emit_pipeline(
            body,
            grid=(n // SC_WINDOW,),
            in_specs=[pl.BlockSpec((SC_WINDOW, w), lambda i: (i, 0)),
                      pl.BlockSpec((1, SC_WINDOW), lambda i: (0, i))],
            out_specs=[],
            core_axis_name=("core", "subcore"),
            dimension_semantics=(pltpu.PARALLEL,),
        )(x_hbm, i_hbm)

    return scatter(rows, dest.reshape(1, n))


def _sc_gather_rows(src, idx):
    n = idx.shape[0]
    w = src.shape[1]

    @pl.kernel(out_type=jax.ShapeDtypeStruct((n, w), src.dtype), mesh=_sc_mesh(), scratch_types=[])
    def gather(x_hbm, i_hbm, o_hbm):
        def body(i_vmem, o_vmem):
            pltpu.sync_copy(x_hbm.at[i_vmem.at[0]], o_vmem)

        pltpu.emit_pipeline(
            body,
            grid=(n // SC_WINDOW,),
            in_specs=[pl.BlockSpec((1, SC_WINDOW), lambda i: (0, i))],
            out_specs=[pl.BlockSpec((SC_WINDOW, w), lambda i: (i, 0))],
            core_axis_name=("core", "subcore"),
            dimension_semantics=(pltpu.PARALLEL,),
        )(i_hbm, o_hbm)

    return gather(src, idx.reshape(1, n))


def _row_index_kernel(ps_ref, route_ref, o_ref, *, n_rows):
    bucket = route_ref[0:1, :].astype(I32)
    start = jnp.zeros(bucket.shape, I32)
    for b in range(N_BUCKETS):
        start = jnp.where(bucket == b, ps_ref[b], start)
    dest = start + route_ref[3:4, :].astype(I32)
    part = lax.broadcasted_iota(I32, o_ref.shape, 0)
    o_ref[...] = part * n_rows + dest


def _row_index(pad_starts, route, n_rows):
    t = route.shape[1]
    tm = min(TILE_ROW_INDEX, t)
    return pl.pallas_call(
        functools.partial(_row_index_kernel, n_rows=n_rows),
        grid=(t // tm,),
        in_specs=[pl.BlockSpec(memory_space=pltpu.SMEM), pl.BlockSpec((SUBLANES, tm), lambda i: (0, i))],
        out_specs=pl.BlockSpec((SUBLANES, tm), lambda i: (0, i)),
        out_shape=jax.ShapeDtypeStruct((SUBLANES, t), I32),
        compiler_params=_params("arbitrary"),
        name="row_index",
    )(pad_starts, route)


def _residual_kernel(x_ref, y_ref, g2_ref, o_ref):
    y = _unpack_bf16_pairs(jnp.concatenate([y_ref[c] for c in range(OUT_PARTS)], axis=1))
    o_ref[...] = x_ref[...] + g2_ref[...] * y


def _residual(x1, ytok, g2, seq, part):
    t, d = x1.shape
    tm = min(TILE_RESIDUAL, seq)
    tpb = seq // tm
    p, n_part = part
    steps = t // tm // n_part
    off = p * steps
    return pl.pallas_call(
        _residual_kernel,
        grid=(steps,),
        in_specs=[
            pl.BlockSpec((tm, d), lambda i: (i + off, 0)),
            pl.BlockSpec((OUT_PARTS, tm, LANES), lambda i: (0, i, 0)),
            pl.BlockSpec((None, 1, d), lambda i: ((i + off) // tpb, 0, 0)),
        ],
        out_specs=pl.BlockSpec((tm, d), lambda i: (i + off, 0)),
        out_shape=jax.ShapeDtypeStruct((t, d), F32),
        input_output_aliases={0: 0},
        compiler_params=_params("arbitrary"),
        name="residual",
    )(x1, ytok, g2)


def _expert_kernel(ea_ref, eb_ref, nr_ref, xs_ref, wga_ref, wua_ref, wda_ref, wgb_ref, wub_ref, wdb_ref, ys_ref,
                   ga_ref, ua_ref, gb_ref, ub_ref, dab_ref):
    j = pl.program_id(0)
    nr = nr_ref[0]
    prev = jnp.maximum(j - 1, 0)
    f = wda_ref.shape[0]

    @pl.when((j == 0) | (ea_ref[j] != ea_ref[prev]))
    def _():
        ga_ref[...] = wga_ref[...].astype(BF16)
        ua_ref[...] = wua_ref[...].astype(BF16)
        dab_ref[0:f, :] = wda_ref[...].astype(BF16)

    @pl.when((j == 0) | (eb_ref[j] != eb_ref[prev]))
    def _():
        gb_ref[...] = wgb_ref[...].astype(BF16)
        ub_ref[...] = wub_ref[...].astype(BF16)
        dab_ref[f:2 * f, :] = wdb_ref[...].astype(BF16)

    @pl.when(j < nr)
    def _():
        x = _unpack_bf16_pairs(jnp.concatenate([xs_ref[c] for c in range(PAY_PARTS - 1)], axis=1)).astype(BF16)
        gl = lax.bitcast_convert_type(xs_ref[PAY_PARTS - 1], F32)

        def gated_act(wg_ref, wu_ref, gate):
            gte = _dot(x, wg_ref[...])
            return (gte * _sigmoid(gte) * _dot(x, wu_ref[...]) * gate).astype(BF16)

        acts = jnp.concatenate([gated_act(ga_ref, ua_ref, gl[:, 0:1]), gated_act(gb_ref, ub_ref, gl[:, 1:2])], axis=1)
        y = _pack_bf16_pairs(_dot(acts, dab_ref[...]))
        for c in range(OUT_PARTS):
            ys_ref[c] = y[:, c * LANES:(c + 1) * LANES]

    @pl.when(j >= nr)
    def _():
        ys_ref[...] = jnp.zeros(ys_ref.shape, U32)


def _experts(blk_ea, blk_eb, n_real, xs, wg, wu, wd, d):
    n_rows = xs.shape[1]
    nblk = n_rows // EXPERT_BLOCK
    f = wg.shape[2]
    grid_spec = pltpu.PrefetchScalarGridSpec(
        num_scalar_prefetch=3,
        grid=(nblk,),
        in_specs=[
            pl.BlockSpec((PAY_PARTS, EXPERT_BLOCK, LANES), lambda j, ea, eb, nr: (0, j, 0)),
            pl.BlockSpec((None, d, f), lambda j, ea, eb, nr: (ea[j], 0, 0)),
            pl.BlockSpec((None, d, f), lambda j, ea, eb, nr: (ea[j], 0, 0)),
            pl.BlockSpec((None, f, d), lambda j, ea, eb, nr: (ea[j], 0, 0)),
            pl.BlockSpec((None, d, f), lambda j, ea, eb, nr: (eb[j], 0, 0)),
            pl.BlockSpec((None, d, f), lambda j, ea, eb, nr: (eb[j], 0, 0)),
            pl.BlockSpec((None, f, d), lambda j, ea, eb, nr: (eb[j], 0, 0)),
        ],
        out_specs=pl.BlockSpec((OUT_PARTS, EXPERT_BLOCK, LANES), lambda j, ea, eb, nr: (0, j, 0)),
        scratch_shapes=[pltpu.VMEM((d, f), BF16)] * 4 + [pltpu.VMEM((2 * f, d), BF16)],
    )
    return pl.pallas_call(
        _expert_kernel,
        grid_spec=grid_spec,
        out_shape=jax.ShapeDtypeStruct((OUT_PARTS, n_rows, LANES), U32),
        compiler_params=_params("arbitrary"),
        name="experts",
    )(blk_ea, blk_eb, n_real, xs, wg, wu, wd, wg, wu, wd)


_PAIR_A = (0, 0, 0, 1, 2, 2)
_PAIR_B = (1, 2, 3, 3, 3, 1)


def kernel(x, c, positions, ada_w, ada_b, norm_mix_w, norm_ffn_w, w_in, b_igate, b_fgate, q_norm_w, k_norm_w,
           sinks, conv_w, conv_b, mlstm_norm_w, w_attn_up, w_mlstm_up, w_out, router_w, router_bias,
           w_gate, w_up, w_down):
    batch, seq, d = x.shape
    depth = w_in.shape[0]
    t = batch * seq
    qw = N_HEADS * HEAD_DIM
    kvw = N_KV * HEAD_DIM
    mw = M_HEADS * M_DIM

    o = 0
    cols = {}
    for name, wdt in (("q", qw), ("k", kvw), ("v", kvw), ("mqk", 2 * mw), ("mv", mw), ("mi", M_HEADS),
                      ("mf", M_HEADS), ("mo", mw), ("ga", d), ("gb", d)):
        cols[name] = (o, o + wdt)
        o += wdt

    def wc(name, lo=0, hi=None):
        s, e = cols[name]
        return w_in[:, :, s + lo:(s + hi if hi is not None else e)]

    w_a = jnp.concatenate([wc("q"), wc("k"), wc("v")], axis=2).astype(BF16)
    w_m = jnp.concatenate([wc("mqk"), wc("mv"), wc("mo")], axis=2).astype(BF16)
    w_g = jnp.concatenate([wc("mi"), wc("mf"), jnp.zeros((depth, d, LANES - 2 * M_HEADS), F32)], axis=2).astype(BF16)
    w_mg = jnp.concatenate([wc("ga"), wc("gb")], axis=2).astype(BF16)
    w_au = w_attn_up.astype(BF16)
    w_mu = w_mlstm_up.astype(BF16)
    w_o = w_out.astype(BF16)
    n_e = w_gate.shape[1]
    w_g8 = w_gate.reshape(depth * n_e, d, -1)
    w_u8 = w_up.reshape(depth * n_e, d, -1)
    w_d = w_down.reshape(depth * n_e, -1, d)

    rw_t = router_w.astype(F32).T
    rw_top = rw_t.astype(BF16)
    rw_hi = jnp.concatenate([rw_top, (rw_t - rw_top.astype(F32)).astype(BF16)], axis=0)
    rw_lo = rw_top
    rb = jnp.broadcast_to(router_bias.astype(F32)[:, None], (n_e, LANES))

    qn_w = jnp.tile(q_norm_w * (HEAD_DIM ** -0.5), (1, N_HEADS)).reshape(depth, 1, qw)
    kn_w = jnp.tile(k_norm_w, (1, N_KV)).reshape(depth, 1, kvw)
    seg = jnp.arange(qw) // HEAD_DIM
    bdq = jnp.where(seg[:, None] == seg[None, :], 1.0 / HEAD_DIM, 0.0).astype(BF16)
    bdk = bdq[:kvw, :kvw]

    inv_freq = ROPE_THETA ** (-(jnp.arange(0, ROPE_DIM, 2, dtype=F32) / ROPE_DIM))
    ang = positions.astype(F32).reshape(1, t) * inv_freq[:, None]
    cos8, sin8 = jnp.cos(ang).T, jnp.sin(ang).T
    pad1 = jnp.ones((t, HEAD_DIM - ROPE_DIM), F32)
    pad0 = jnp.zeros((t, HEAD_DIM - ROPE_DIM), F32)
    cos_t = jnp.tile(jnp.concatenate([cos8, cos8, pad1], axis=1), (1, LANES // HEAD_DIM))
    sin_t = jnp.tile(jnp.concatenate([-sin8, sin8, pad0], axis=1), (1, LANES // HEAD_DIM))

    gate_bias = jnp.concatenate([b_igate, b_fgate], axis=1).astype(F32)
    bcol = jnp.broadcast_to(gate_bias[:, :, None], (depth, 2 * M_HEADS, LANES))

    tm_merge = min(TILE_MERGE, seq)
    ii = jnp.arange(tm_merge)
    tri = (ii[:, None] <= ii[None, :]).astype(BF16)

    n_blk = (t + N_BUCKETS * (EXPERT_BLOCK - 1)) // EXPERT_BLOCK + 1
    n_rows = n_blk * EXPERT_BLOCK
    pair_a = jnp.asarray(_PAIR_A, I32)
    pair_b = jnp.asarray(_PAIR_B, I32)

    c_pad = jnp.zeros((SUBLANES, d), F32).at[:batch].set(c)
    mod = _ada_mod(c_pad, ada_w, ada_b)[:, :batch]

    xf = x.reshape(t, d)
    moe = None
    for l in range(depth):
        sh1, sc1, g1, sh2, sc2, g2 = [m.reshape(batch, 1, d) for m in jnp.split(mod[l], 6, axis=-1)]

        nmw = norm_mix_w.reshape(depth, 1, d)
        if moe is None:
            a_in, m_in, mg, grow = _inproj(xf, None, sc1, sh1, nmw, w_a, w_m, w_g, w_mg, l, seq)
        else:
            outs = None
            for p in range(GATHER_PARTS):
                outs = _inproj(xf, (moe[0][p], moe[1]), sc1, sh1, nmw, w_a, w_m, w_g, w_mg, l, seq,
                               part=(p, GATHER_PARTS), carry=None if outs is None else outs[:4])
                xf = outs[4]
            a_in, m_in, mg, grow = outs[:4]
        o_attn = _attention(a_in, cos_t, sin_t, sinks[l], qn_w[l], kn_w[l], bdq, bdk, batch, seq)
        hm = _mlstm(m_in, grow, conv_w[l], conv_b[l].reshape(1, -1), bcol[l], mlstm_norm_w[l].reshape(1, mw),
                    batch, seq)
        x1, pay, route, cnt = _merge(o_attn, hm, mg, xf, g1, sc2, sh2, norm_ffn_w.reshape(depth, 1, d),
                                     w_au, w_mu, w_o, rw_hi, rw_lo, rb, tri, l, seq)

        counts = cnt[:N_BUCKETS, 0].astype(I32)
        padded = (counts + EXPERT_BLOCK - 1) // EXPERT_BLOCK * EXPERT_BLOCK
        pad_ends = jnp.cumsum(padded)
        pad_starts = pad_ends - padded
        row_idx = _row_index(jnp.concatenate([pad_starts, jnp.zeros((BUCKET_ROWS - N_BUCKETS,), I32)]), route, n_rows)
        blk_start = jnp.arange(n_blk, dtype=I32) * EXPERT_BLOCK
        blk_bucket = jnp.minimum(jnp.sum((pad_ends[None, :] <= blk_start[:, None]).astype(I32), axis=1), N_BUCKETS - 1)
        grp = blk_bucket // N_PAIRS
        blk_ea = (l * n_e + grp * EPG + pair_a[blk_bucket % N_PAIRS]).astype(I32)
        blk_eb = (l * n_e + grp * EPG + pair_b[blk_bucket % N_PAIRS]).astype(I32)
        n_real = (pad_ends[-1:] // EXPERT_BLOCK).astype(I32)

        xs = _sc_scatter_rows(pay.reshape(PAY_PARTS * t, LANES), row_idx[:PAY_PARTS].reshape(-1),
                              PAY_PARTS * n_rows).reshape(PAY_PARTS, n_rows, LANES)
        ys = _experts(blk_ea, blk_eb, n_real, xs, w_g8, w_u8, w_d, d)
        tp = t // GATHER_PARTS
        ys_rows = ys.reshape(OUT_PARTS * n_rows, LANES)
        ytok = [_sc_gather_rows(ys_rows, row_idx[:OUT_PARTS, p * tp:(p + 1) * tp].reshape(-1)).reshape(OUT_PARTS, tp, LANES)
                for p in range(GATHER_PARTS)]
        xf, moe = x1, (ytok, g2)
    for p in range(GATHER_PARTS):
        xf = _residual(xf, moe[0][p], moe[1], seq, (p, GATHER_PARTS))
    return xf.reshape(batch, seq, d)
```

```python
import functools

import jax
import jax.numpy as jnp
from jax import lax
from jax.experimental import pallas as pl
from jax.experimental.pallas import tpu as pltpu
from jax.experimental.pallas import tpu_sc as plsc

F32 = jnp.float32
BF16 = jnp.bfloat16
U32 = jnp.uint32
I32 = jnp.int32
HIGHEST = lax.Precision.HIGHEST

HEAD_DIM = 64
N_HEADS = 8
N_KV = 2
ROPE_DIM = 16
ROPE_THETA = 500000.0
ATTN_BLOCK = 128
M_HEADS = 4
M_DIM = 128
CONV_K = 4
N_EXPERTS = 16
N_GROUPS = 4
EPG = 4
EPS = 1e-6

LANES = 128
SUBLANES = 8

TILE_INPROJ = 1024
TILE_ATTN = 1024
TILE_MLSTM = 2048
TILE_MERGE = 1024
TILE_RESIDUAL = 1024
TILE_ROW_INDEX = 8192
ADA_COLS = 3072
CHUNK = 128
CHUNKS_PER_STEP = 4
GATE_CHUNKS_PER_STEP = 4
N_PAIRS = 6
N_BUCKETS = N_GROUPS * N_PAIRS
BUCKET_ROWS = 32
EXPERT_BLOCK = 512
PAY_PARTS = 5
OUT_PARTS = 4
SC_WINDOW = 128
GATHER_PARTS = 2
VMEM_LIMIT = 56 * 1024 * 1024


def _dot(a, b, precision=None):
    return jnp.dot(a, b, preferred_element_type=F32, precision=precision)


def _dot_nt(a, b):
    return lax.dot_general(a, b, (((1,), (1,)), ((), ())), preferred_element_type=F32)


def _sigmoid(x):
    return 1.0 / (1.0 + jnp.exp(-x))


def _log_sigmoid(x):
    return jnp.minimum(x, 0.0) - jnp.log1p(jnp.exp(-jnp.abs(x)))


def _pack_bf16_pairs(v, rounded=False):
    n = v.shape[1] // 2
    bits = lax.bitcast_convert_type(v if rounded else v.astype(BF16).astype(F32), U32)
    return bits[:, :n] | (bits[:, n:] >> 16)


def _unpack_bf16_pairs(w):
    hi = lax.bitcast_convert_type(w & jnp.uint32(0xFFFF0000), F32)
    lo = lax.bitcast_convert_type(w << 16, F32)
    return jnp.concatenate([hi, lo], axis=1)


def _params(*sem):
    return pltpu.CompilerParams(dimension_semantics=sem, vmem_limit_bytes=VMEM_LIMIT)


def _ada_kernel(c_ref, w_ref, b_ref, o_ref):
    c = c_ref[...]
    ca = c * _sigmoid(c)
    o_ref[0] = _dot(ca, w_ref[0], HIGHEST) + b_ref[0]


def _ada_mod(c_pad, ada_w, ada_b):
    depth, d, n = ada_w.shape
    tn = ADA_COLS
    return pl.pallas_call(
        _ada_kernel,
        grid=(depth, n // tn),
        in_specs=[
            pl.BlockSpec((SUBLANES, d), lambda l, j: (0, 0)),
            pl.BlockSpec((1, d, tn), lambda l, j: (l, 0, j)),
            pl.BlockSpec((1, 1, tn), lambda l, j: (l, 0, j)),
        ],
        out_specs=pl.BlockSpec((1, SUBLANES, tn), lambda l, j: (l, 0, j)),
        out_shape=jax.ShapeDtypeStruct((depth, SUBLANES, n), F32),
        compiler_params=_params("arbitrary", "arbitrary"),
        name="ada_mod",
    )(c_pad, ada_w, ada_b.reshape(depth, 1, n))


def _inproj_kernel(*refs, fuse_residual, n_carry=0):
    if n_carry:
        refs = refs[:10] + refs[10 + n_carry:]
    if fuse_residual:
        (x_ref, y_ref, g2_ref, sc_ref, sh_ref, nw_ref, wa_ref, wm_ref, wg_ref, wmg_ref,
         a_ref, m_ref, mg_ref, gr_ref, xo_ref, g_ref) = refs
        y = _unpack_bf16_pairs(jnp.concatenate([y_ref[c] for c in range(OUT_PARTS)], axis=1))
        x = x_ref[...] + g2_ref[...] * y
        xo_ref[...] = x
    else:
        (x_ref, sc_ref, sh_ref, nw_ref, wa_ref, wm_ref, wg_ref, wmg_ref,
         a_ref, m_ref, mg_ref, gr_ref, g_ref) = refs
        x = x_ref[...]
    ms = jnp.mean(x * x, axis=-1, keepdims=True)
    h = x * lax.rsqrt(ms + EPS) * (nw_ref[...] * (1.0 + sc_ref[...])) + sh_ref[...]
    hb = h.astype(BF16)
    a_ref[...] = _dot(hb, wa_ref[...]).astype(BF16)
    m_ref[...] = _dot(hb, wm_ref[...]).astype(BF16)
    mg_ref[...] = _sigmoid(_dot(hb, wmg_ref[...])).astype(BF16)
    g_ref[...] = _dot(hb, wg_ref[...])
    gr_ref[...] = g_ref[...].T[:SUBLANES, :]


def _inproj(x, moe, sc, sh, nw, wa, wm, wg, wmg, l, seq, part=None, carry=None):
    t, d = x.shape
    tm = min(TILE_INPROJ, seq)
    tpb = seq // tm
    p, n_part = part if part is not None else (0, 1)
    steps = t // tm // n_part
    off = p * steps
    row = lambda i: (i + off, 0)
    bsel = lambda i: ((i + off) // tpb, 0, 0)
    wsel = lambda i: (l, 0, 0)
    once = pl.Buffered(1)
    na, nm, ng, nmg = wa.shape[2], wm.shape[2], wg.shape[2], wmg.shape[2]
    fuse = moe is not None
    moe_specs = [pl.BlockSpec((OUT_PARTS, tm, LANES), lambda i: (0, i, 0)), pl.BlockSpec((None, 1, d), bsel)]
    carry = list(carry) if carry is not None else []
    aliases = {10 + k: k for k in range(len(carry))}
    if part is not None:
        aliases[0] = 4
    return pl.pallas_call(
        functools.partial(_inproj_kernel, fuse_residual=fuse, n_carry=len(carry)),
        grid=(steps,),
        in_specs=[pl.BlockSpec((tm, d), row)] + (moe_specs if fuse else []) + [
            pl.BlockSpec((None, 1, d), bsel),
            pl.BlockSpec((None, 1, d), bsel),
            pl.BlockSpec((None, 1, d), wsel),
            pl.BlockSpec((None, d, na), wsel, pipeline_mode=once),
            pl.BlockSpec((None, d, nm), wsel, pipeline_mode=once),
            pl.BlockSpec((None, d, ng), wsel, pipeline_mode=once),
            pl.BlockSpec((None, d, nmg), wsel, pipeline_mode=once),
        ] + [pl.BlockSpec(memory_space=pl.ANY)] * len(carry),
        out_specs=[
            pl.BlockSpec((tm, na), row),
            pl.BlockSpec((tm, nm), row),
            pl.BlockSpec((tm, nmg), row),
            pl.BlockSpec((SUBLANES, tm), lambda i: (0, i + off)),
        ] + ([pl.BlockSpec((tm, d), row)] if fuse else []),
        out_shape=[
            jax.ShapeDtypeStruct((t, na), BF16),
            jax.ShapeDtypeStruct((t, nm), BF16),
            jax.ShapeDtypeStruct((t, nmg), BF16),
            jax.ShapeDtypeStruct((SUBLANES, t), F32),
        ] + ([jax.ShapeDtypeStruct((t, d), F32)] if fuse else []),
        scratch_shapes=[pltpu.VMEM((tm, ng), F32)],
        input_output_aliases=aliases,
        compiler_params=_params("arbitrary"),
        name="inproj",
    )(x, *(moe if fuse else ()), sc, sh, nw, wa, wm, wg, wmg, *carry)


def _rope(t, cos, sin):
    w = t.shape[1]
    reps = w // LANES
    cosw = jnp.concatenate([cos] * reps, axis=1) if reps > 1 else cos
    sinw = jnp.concatenate([sin] * reps, axis=1) if reps > 1 else sin
    lane = lax.broadcasted_iota(I32, t.shape, 1)
    half = ROPE_DIM // 2
    up = pltpu.roll(t, w - half, axis=1)
    dn = pltpu.roll(t, half, axis=1)
    partner = jnp.where((lane % ROPE_DIM) < half, up, dn)
    return t * cosw + partner * sinw


def _head_norm(t, bd, w):
    ms = _dot((t * t).astype(BF16), bd)
    return t * lax.rsqrt(ms + EPS) * w


def _attn_kernel(sink_ref, cur_ref, prev_ref, cos_ref, sin_ref, cosp_ref, sinp_ref,
                 qw_ref, kw_ref, bdq_ref, bdk_ref, o_ref):
    tq = cur_ref.shape[0]
    nj = tq // ATTN_BLOCK
    qw = N_HEADS * HEAD_DIM
    kw = N_KV * HEAD_DIM
    blk0 = pl.program_id(1) * nj

    cur = cur_ref[...]
    q = cur[:, :qw].astype(F32)
    kc = cur[:, qw:qw + kw].astype(F32)
    vc = cur[:, qw + kw:].astype(F32)
    prev = prev_ref[...]
    kp = prev[:, :kw].astype(F32)
    vp = prev[:, kw:].astype(F32)

    cos, sin = cos_ref[...], sin_ref[...]
    q = _rope(_head_norm(q, bdq_ref[...], qw_ref[...]), cos, sin)
    kc = _rope(_head_norm(kc, bdk_ref[...], kw_ref[...]), cos, sin)
    kp = _rope(_head_norm(kp, bdk_ref[...], kw_ref[...]), cosp_ref[...], sinp_ref[...])
    qb = q.astype(BF16)

    def both_halves(x2):
        swapped = pltpu.roll(x2, HEAD_DIM, axis=1)
        first = lax.broadcasted_iota(I32, x2.shape, 1) < HEAD_DIM
        return jnp.concatenate([jnp.where(first, x2, swapped), jnp.where(first, swapped, x2)], axis=1).astype(BF16)

    k_all = both_halves(jnp.concatenate([kp, kc], axis=0))
    v_all = both_halves(jnp.concatenate([vp, vc], axis=0))

    lane = lax.broadcasted_iota(I32, (ATTN_BLOCK, LANES), 1)
    lo = lane < HEAD_DIM
    zero = jnp.zeros((ATTN_BLOCK, LANES), BF16)
    g_heads = N_HEADS // N_KV
    ri = lax.broadcasted_iota(I32, (g_heads * ATTN_BLOCK, ATTN_BLOCK), 0) % ATTN_BLOCK
    ci = lax.broadcasted_iota(I32, (g_heads * ATTN_BLOCK, ATTN_BLOCK), 1)
    from_prev = ci > ri
    head_row = lax.broadcasted_iota(I32, (g_heads * ATTN_BLOCK, 1), 0) // ATTN_BLOCK
    ones_v = jnp.ones((2 * ATTN_BLOCK, LANES), BF16)

    tiles = [(j, g) for j in range(nj) for g in range(N_KV)]
    scores = {}
    for j, g in tiles:
        rows = slice(j * ATTN_BLOCK, (j + 1) * ATTN_BLOCK)
        band = slice(j * ATTN_BLOCK, (j + 2) * ATTN_BLOCK)
        qp0 = qb[rows, (2 * g) * LANES:(2 * g + 1) * LANES]
        qp1 = qb[rows, (2 * g + 1) * LANES:(2 * g + 2) * LANES]
        q4 = jnp.concatenate([jnp.where(lo, qp0, zero), jnp.where(lo, zero, qp0),
                              jnp.where(lo, qp1, zero), jnp.where(lo, zero, qp1)], axis=0)
        scores[j, g] = _dot_nt(q4, k_all[band, g * LANES:(g + 1) * LANES])

    probs, sink_term = {}, {}
    for j, g in tiles:
        s2 = scores[j, g]
        if j == 0:
            prev_ok = ci > ri + (1 - jnp.minimum(blk0, 1)) * ATTN_BLOCK
            s = jnp.where(prev_ok, s2[:, :ATTN_BLOCK], jnp.where(from_prev, -jnp.inf, s2[:, ATTN_BLOCK:]))
        else:
            s = jnp.where(from_prev, s2[:, :ATTN_BLOCK], s2[:, ATTN_BLOCK:])
        sink = jnp.full((g_heads * ATTN_BLOCK, 1), sink_ref[g_heads * g], F32)
        for r in range(1, g_heads):
            sink = jnp.where(head_row == r, sink_ref[g_heads * g + r], sink)
        m = jnp.maximum(jnp.max(s, axis=-1, keepdims=True), sink)
        p = jnp.exp(s - m)
        probs[j, g] = jnp.concatenate([jnp.where(from_prev, p, 0.0), jnp.where(from_prev, 0.0, p)],
                                      axis=1).astype(BF16)
        sink_term[j, g] = jnp.exp(sink - m)

    for j, g in tiles:
        rows = slice(j * ATTN_BLOCK, (j + 1) * ATTN_BLOCK)
        band = slice(j * ATTN_BLOCK, (j + 2) * ATTN_BLOCK)
        o8 = _dot(probs[j, g], jnp.concatenate([v_all[band, g * LANES:(g + 1) * LANES], ones_v], axis=1))
        o4 = o8[:, :LANES] / (o8[:, LANES:] + sink_term[j, g])
        b = ATTN_BLOCK
        o_ref[rows, (2 * g) * LANES:(2 * g + 1) * LANES] = jnp.where(lo, o4[0:b], o4[b:2 * b]).astype(BF16)
        o_ref[rows, (2 * g + 1) * LANES:(2 * g + 2) * LANES] = jnp.where(
            lo, o4[2 * b:3 * b], o4[3 * b:4 * b]).astype(BF16)


def _attention(a_in, cos_t, sin_t, sinks_l, qw, kw, bdq, bdk, batch, seq):
    t = a_in.shape[0]
    tq = min(TILE_ATTN, seq)
    nj = tq // ATTN_BLOCK
    tpb = seq // tq
    bpb = seq // ATTN_BLOCK
    qwid = N_HEADS * HEAD_DIM
    kvw = 2 * N_KV * HEAD_DIM
    cur = lambda b, i: (b * tpb + i, 0)
    prv = lambda b, i: (b * bpb + jnp.maximum(i * nj - 1, 0), qwid // kvw)
    prv0 = lambda b, i: (b * bpb + jnp.maximum(i * nj - 1, 0), 0)
    const = lambda b, i: (0, 0)
    return pl.pallas_call(
        _attn_kernel,
        grid=(batch, tpb),
        in_specs=[
            pl.BlockSpec(memory_space=pltpu.SMEM),
            pl.BlockSpec((tq, qwid + kvw), cur),
            pl.BlockSpec((ATTN_BLOCK, kvw), prv),
            pl.BlockSpec((tq, LANES), cur),
            pl.BlockSpec((tq, LANES), cur),
            pl.BlockSpec((ATTN_BLOCK, LANES), prv0),
            pl.BlockSpec((ATTN_BLOCK, LANES), prv0),
            pl.BlockSpec((1, qwid), const),
            pl.BlockSpec((1, kvw // 2), const),
            pl.BlockSpec((qwid, qwid), const),
            pl.BlockSpec((kvw // 2, kvw // 2), const),
        ],
        out_specs=pl.BlockSpec((tq, qwid), cur),
        out_shape=jax.ShapeDtypeStruct((t, qwid), BF16),
        compiler_params=_params("arbitrary", "arbitrary"),
        name="swa_attention",
    )(sinks_l, a_in, a_in, cos_t, sin_t, cos_t, sin_t, qw, kw, bdq, bdk)


def _mlstm_kernel(min_ref, gr_ref, cw_ref, cb_ref, bcol_ref, nw_ref,
                  hm_ref, ext_ref, q_ref, kt_ref, st_ref, mx_ref, ab_ref, bc_ref):
    tt = min_ref.shape[0]
    mw = M_HEADS * M_DIM
    nchunks = tt // CHUNK

    @pl.when(pl.program_id(1) == 0)
    def _():
        ext_ref[0:SUBLANES, :] = jnp.zeros((SUBLANES, 2 * mw), F32)
        st_ref[...] = jnp.zeros(st_ref.shape, F32)
        mx_ref[...] = jnp.zeros(mx_ref.shape, F32)

    def conv_block(cols):
        u = min_ref[:, cols].astype(F32)
        ext_ref[SUBLANES:SUBLANES + tt, cols] = u
        acc = cb_ref[:, cols] + cw_ref[CONV_K - 1:CONV_K, cols] * u
        for jj in range(CONV_K - 1):
            off = SUBLANES - (CONV_K - 1) + jj
            acc = acc + cw_ref[jj:jj + 1, cols] * ext_ref[off:off + tt, cols]
        ext_ref[0:SUBLANES, cols] = u[tt - SUBLANES:tt, :]
        return acc * _sigmoid(acc)

    def q_body(h, carry):
        cols = pl.ds(pl.multiple_of(h * M_DIM, M_DIM), M_DIM)
        q_ref[:, cols] = conv_block(cols).astype(BF16)
        return carry

    def k_body(h, carry):
        off = pl.multiple_of(h * M_DIM, M_DIM)
        act = conv_block(pl.ds(mw + off, M_DIM)) * (M_DIM ** -0.5)
        for j in range(nchunks):
            kt_ref[pl.ds(off, M_DIM), j * CHUNK:(j + 1) * CHUNK] = act[j * CHUNK:(j + 1) * CHUNK, :].T
        return carry

    lax.fori_loop(0, M_HEADS, q_body, 0)
    lax.fori_loop(0, M_HEADS, k_body, 0)

    ri = lax.broadcasted_iota(I32, (CHUNK, CHUNK), 0)
    ci = lax.broadcasted_iota(I32, (CHUNK, CHUNK), 1)
    causal = ci <= ri
    triu = jnp.where(ri <= ci, 1.0, 0.0).astype(BF16)
    ones_half = jnp.ones((CHUNK, M_DIM), BF16)
    mean_mat = jnp.full((M_DIM, M_DIM), 1.0 / M_DIM, BF16)
    sub = lax.broadcasted_iota(I32, (SUBLANES, CHUNK), 0)
    heads = range(M_HEADS)

    pad_rows = jnp.zeros((CHUNK - SUBLANES, CHUNK), F32)
    zero_rows = jnp.zeros((SUBLANES, CHUNK), F32)

    def gate_body(jg, carry):
        for u_ in range(GATE_CHUNKS_PER_STEP):
            rs = pl.ds(pl.multiple_of((jg * GATE_CHUNKS_PER_STEP + u_) * CHUNK, CHUNK), CHUNK)
            gr = gr_ref[:, rs] + bcol_ref[...]
            ls = _log_sigmoid(gr)
            ls1 = ls.astype(BF16).astype(F32)
            ls2 = (ls - ls1).astype(BF16).astype(F32)
            pieces = jnp.concatenate([ls1, ls2, ls - ls1 - ls2, zero_rows], axis=0).astype(BF16)
            sums = _dot(pieces, triu)
            br = sums[0:SUBLANES] + sums[SUBLANES:2 * SUBLANES] + sums[2 * SUBLANES:3 * SUBLANES]
            ab = jnp.where(sub < M_HEADS, gr - pltpu.roll(br, M_HEADS, axis=0), br)
            ab_ref[:, rs] = ab
            bc_ref[rs, :] = jnp.concatenate([ab, pad_rows], axis=0).T
        return carry

    lax.fori_loop(0, nchunks // GATE_CHUNKS_PER_STEP, gate_body, 0)

    def group_body(cg, carry):
        rows, ab = [], []
        for u_ in range(CHUNKS_PER_STEP):
            r0 = pl.multiple_of((cg * CHUNKS_PER_STEP + u_) * CHUNK, CHUNK)
            rows.append(pl.ds(r0, CHUNK))
            ab.append(ab_ref[:, rows[u_]])
        lanes = [(u_, h) for u_ in range(CHUNKS_PER_STEP) for h in heads]
        a_r = {(u_, h): ab[u_][h:h + 1, :] for u_, h in lanes}
        b_last = {(u_, h): ab[u_][M_HEADS + h:M_HEADS + h + 1, CHUNK - 1:CHUNK] for u_, h in lanes}

        m_prev, a_max, a_dec, s_in = {}, {}, {}, {}
        m_run = [mx_ref[h][0:1, 0:1] for h in heads]
        for k in lanes:
            u_, h = k
            m_prev[k] = m_run[h]
            a_max[k] = jnp.max(a_r[k], axis=-1, keepdims=True)
            m_loc = b_last[k] + a_max[k]
            m_new = jnp.maximum(b_last[k] + m_prev[k], m_loc)
            a_dec[k] = jnp.exp(b_last[k] + m_prev[k] - m_new)
            s_in[k] = jnp.exp(m_loc - m_new)
            m_run[h] = m_new
        for h in heads:
            mx_ref[h] = jnp.broadcast_to(m_run[h], (SUBLANES, LANES))

        q, v_ext, s_qk, kv = {}, {}, {}, {}
        for k in lanes:
            u_, h = k
            rs = rows[u_]
            q[k] = q_ref[rs, h * M_DIM:(h + 1) * M_DIM]
            kt = kt_ref[h * M_DIM:(h + 1) * M_DIM, rs]
            v = min_ref[rs, 2 * mw + h * M_DIM:2 * mw + (h + 1) * M_DIM]
            v_ext[k] = jnp.concatenate([v, ones_half], axis=1)
            s_qk[k] = _dot(q[k], kt.astype(BF16))
            e_r = jnp.exp(a_r[k] - a_max[k])
            kv[k] = _dot((kt * e_r).astype(BF16), v_ext[k])

        thr, qk = {}, {}
        for k in lanes:
            u_, h = k
            a_mat = jnp.where(causal, a_r[k], -jnp.inf)
            mu = jnp.maximum(jnp.max(a_mat, axis=-1, keepdims=True), m_prev[k])
            b_c = bc_ref[rows[u_], M_HEADS + h:M_HEADS + h + 1]
            thr[k] = jnp.broadcast_to(jnp.exp(-(b_c + mu)), (CHUNK, M_DIM))
            mu_b = jnp.broadcast_to(mu, (CHUNK, CHUNK))
            inter = jnp.exp(m_prev[k] - mu_b)
            qk[k] = jnp.concatenate([(s_qk[k] * jnp.exp(a_mat - mu_b)).astype(BF16),
                                     (q[k].astype(F32) * inter).astype(BF16)], axis=1)

        state = [st_ref[h] for h in heads]
        for k in lanes:
            u_, h = k
            hs = slice(h * M_DIM, (h + 1) * M_DIM)
            num = _dot(qk[k], jnp.concatenate([v_ext[k], state[h].astype(BF16)], axis=0))
            state[h] = a_dec[k] * state[h] + s_in[k] * kv[k]
            den = jnp.maximum(jnp.abs(num[:, M_DIM:]), thr[k])
            hh = num[:, :M_DIM] / den
            msq = _dot((hh * hh).astype(BF16), mean_mat)
            hn = hh * lax.rsqrt(msq + EPS) * nw_ref[:, hs]
            og = min_ref[rows[u_], 3 * mw + h * M_DIM:3 * mw + (h + 1) * M_DIM].astype(F32)
            hm_ref[rows[u_], hs] = (_sigmoid(og) * hn).astype(BF16)
        for h in heads:
            st_ref[h] = state[h]
        return carry

    lax.fori_loop(0, nchunks // CHUNKS_PER_STEP, group_body, 0)


def _mlstm(m_in, grow, conv_w, conv_b, bcol, nw, batch, seq):
    t = m_in.shape[0]
    tt = min(TILE_MLSTM, seq)
    tpb = seq // tt
    mw = M_HEADS * M_DIM
    cur = lambda b, i: (b * tpb + i, 0)
    const = lambda b, i: (0, 0)
    return pl.pallas_call(
        _mlstm_kernel,
        grid=(batch, tpb),
        in_specs=[
            pl.BlockSpec((tt, 4 * mw), cur),
            pl.BlockSpec((SUBLANES, tt), lambda b, i: (0, b * tpb + i)),
            pl.BlockSpec((CONV_K, 2 * mw), const),
            pl.BlockSpec((1, 2 * mw), const),
            pl.BlockSpec((SUBLANES, LANES), const),
            pl.BlockSpec((1, mw), const),
        ],
        out_specs=pl.BlockSpec((tt, mw), cur),
        out_shape=jax.ShapeDtypeStruct((t, mw), BF16),
        scratch_shapes=[
            pltpu.VMEM((tt + SUBLANES, 2 * mw), F32),
            pltpu.VMEM((tt, mw), BF16),
            pltpu.VMEM((mw, tt), F32),
            pltpu.VMEM((M_HEADS, M_DIM, 2 * M_DIM), F32),
            pltpu.VMEM((M_HEADS, SUBLANES, LANES), F32),
            pltpu.VMEM((SUBLANES, tt), F32),
            pltpu.VMEM((tt, LANES), F32),
        ],
        compiler_params=_params("arbitrary", "arbitrary"),
        name="mlstm",
    )(m_in, grow, conv_w, conv_b, bcol, nw)


def _merge_kernel(o_ref, hm_ref, mg_ref, x_ref, g1_ref, sc_ref, sh_ref, nw_ref,
                  wa_ref, wm_ref, wo_ref, rwh_ref, rwl_ref, rb_ref, tri_ref,
                  x1_ref, pay_ref, route_ref, cnt_ref, carry_ref):
    tm, d = x_ref.shape

    @pl.when(pl.program_id(0) == 0)
    def _():
        carry_ref[...] = jnp.zeros(carry_ref.shape, F32)

    ya = _dot(o_ref[...], wa_ref[...])
    yb = _dot(hm_ref[...], wm_ref[...])
    mg = mg_ref[...]
    merged = mg[:, :d].astype(F32) * ya + mg[:, d:].astype(F32) * yb
    x1 = x_ref[...] + g1_ref[...] * _dot(merged.astype(BF16), wo_ref[...])
    x1_ref[...] = x1

    ms = jnp.mean(x1 * x1, axis=-1, keepdims=True)
    h2 = x1 * lax.rsqrt(ms + EPS) * (nw_ref[...] * (1.0 + sc_ref[...])) + sh_ref[...]
    hi = h2.astype(BF16)
    hif = hi.astype(F32)
    lo = (h2 - hif).astype(BF16)
    r_hi = _dot_nt(rwh_ref[...], hi)
    r_lo = _dot_nt(rwl_ref[...], lo)
    sc_t = _sigmoid(r_hi[:N_EXPERTS] + r_hi[N_EXPERTS:] + r_lo)
    sel_t = sc_t + rb_ref[:, 0:1]

    def row(a, e):
        return a[e:e + 1, :]

    best = None
    gi = jnp.zeros((1, tm), I32)
    for g in range(N_GROUPS):
        r = [row(sel_t, EPG * g + i) for i in range(EPG)]
        gs = None
        for i in range(EPG):
            for j in range(i + 1, EPG):
                pr = r[i] + r[j]
                gs = pr if gs is None else jnp.maximum(gs, pr)
        if best is None:
            best = gs
        else:
            upd = gs > best
            gi = jnp.where(upd, g, gi)
            best = jnp.maximum(best, gs)

    def pick(a, i):
        out = row(a, i)
        for g in range(1, N_GROUPS):
            out = jnp.where(gi == g, row(a, EPG * g + i), out)
        return out

    v = [pick(sel_t, i) for i in range(EPG)]
    s = [pick(sc_t, i) for i in range(EPG)]

    def argmax4(vals):
        bv, bi = vals[0], jnp.zeros((1, tm), I32)
        for i in range(1, EPG):
            upd = vals[i] > bv
            bi = jnp.where(upd, i, bi)
            bv = jnp.maximum(bv, vals[i])
        return bi

    i1 = argmax4(v)
    i2 = argmax4([jnp.where(i1 == i, -jnp.inf, v[i]) for i in range(EPG)])
    ia = jnp.minimum(i1, i2)
    ib = jnp.maximum(i1, i2)
    pidx = jnp.where(ia == 0, ib - 1, jnp.where(ia == 1, jnp.where(ib == 3, 3, 5), 4))
    bucket = gi * N_PAIRS + pidx

    def by_index(vals, idx):
        out = vals[0]
        for i in range(1, EPG):
            out = jnp.where(idx == i, vals[i], out)
        return out

    swap = pidx == N_PAIRS - 1
    s_lo, s_hi = by_index(s, ia), by_index(s, ib)
    s_a, s_b = jnp.where(swap, s_hi, s_lo), jnp.where(swap, s_lo, s_hi)
    gate_a = s_a / (s_a + s_b)
    gate_b = s_b / (s_a + s_b)

    brow = lax.broadcasted_iota(I32, (BUCKET_ROWS, tm), 0)
    onehot = brow == bucket
    cums = _dot(jnp.where(onehot, 1.0, 0.0).astype(BF16), tri_ref[...])
    carry = carry_ref[...]
    rank = jnp.sum(jnp.where(onehot, carry[:, 0:1] + cums, 0.0), axis=0, keepdims=True) - 1.0
    new_carry = carry + cums[:, tm - 1:tm]
    carry_ref[...] = new_carry
    cnt_ref[...] = new_carry

    route_ref[...] = jnp.concatenate(
        [bucket.astype(F32), gate_a, gate_b, rank, jnp.zeros((SUBLANES - 4, tm), F32)], axis=0)

    half = d // 2
    packed = _pack_bf16_pairs(hif, rounded=True)
    for cpart in range(half // LANES):
        pay_ref[cpart] = packed[:, cpart * LANES:(cpart + 1) * LANES]
    gates_t = jnp.concatenate([gate_a, gate_b, jnp.zeros((LANES - 2, tm), F32)], axis=0)
    pay_ref[half // LANES] = lax.bitcast_convert_type(gates_t.T, U32)


def _merge(o_attn, hm, mg, x, g1, sc2, sh2, nw, wa, wm, wo, rwh, rwl, rb, tri, l, seq):
    t, d = x.shape
    tm = tri.shape[0]
    tpb = seq // tm
    row = lambda i: (i, 0)
    bsel = lambda i: (i // tpb, 0, 0)
    wsel = lambda i: (l, 0, 0)
    const = lambda i: (0, 0)
    hw = o_attn.shape[1]
    return pl.pallas_call(
        _merge_kernel,
        grid=(t // tm,),
        in_specs=[
            pl.BlockSpec((tm, hw), row),
            pl.BlockSpec((tm, hw), row),
            pl.BlockSpec((tm, 2 * d), row),
            pl.BlockSpec((tm, d), row),
            pl.BlockSpec((None, 1, d), bsel),
            pl.BlockSpec((None, 1, d), bsel),
            pl.BlockSpec((None, 1, d), bsel),
            pl.BlockSpec((None, 1, d), wsel),
            pl.BlockSpec((None, hw, d), wsel),
            pl.BlockSpec((None, hw, d), wsel),
            pl.BlockSpec((None, d, d), wsel),
            pl.BlockSpec((2 * N_EXPERTS, d), const),
            pl.BlockSpec((N_EXPERTS, d), const),
            pl.BlockSpec((N_EXPERTS, LANES), const),
            pl.BlockSpec((tm, tm), const),
        ],
        out_specs=[
            pl.BlockSpec((tm, d), row),
            pl.BlockSpec((PAY_PARTS, tm, LANES), lambda i: (0, i, 0)),
            pl.BlockSpec((SUBLANES, tm), lambda i: (0, i)),
            pl.BlockSpec((BUCKET_ROWS, LANES), const),
        ],
        out_shape=[
            jax.ShapeDtypeStruct((t, d), F32),
            jax.ShapeDtypeStruct((PAY_PARTS, t, LANES), U32),
            jax.ShapeDtypeStruct((SUBLANES, t), F32),
            jax.ShapeDtypeStruct((BUCKET_ROWS, LANES), F32),
        ],
        scratch_shapes=[pltpu.VMEM((BUCKET_ROWS, LANES), F32)],
        compiler_params=_params("arbitrary"),
        name="merge_router",
    )(o_attn, hm, mg, x, g1, sc2, sh2, nw, wa, wm, wo, rwh, rwl, rb, tri)


def _sc_mesh():
    return plsc.VectorSubcoreMesh(core_axis_name="core", subcore_axis_name="subcore")


def _sc_scatter_rows(rows, dest, n_out):
    n, w = rows.shape

    @pl.kernel(out_type=jax.ShapeDtypeStruct((n_out, w), rows.dtype), mesh=_sc_mesh(), scratch_types=[])
    def scatter(x_hbm, i_hbm, o_hbm):
        def body(x_vmem, i_vmem):
            pltpu.sync_copy(x_vmem, o_hbm.at[i_vmem.at[0]])

        pltpu.emit_pipeline(
            body,
            grid=(n // SC_WINDOW,),
            in_specs=[pl.BlockSpec((SC_WINDOW, w), lambda i: (i, 0)),
                      pl.BlockSpec((1, SC_WINDOW), lambda i: (0, i))],
            out_specs=[],
            core_axis_name=("core", "subcore"),
            dimension_semantics=(pltpu.PARALLEL,),
        )(x_hbm, i_hbm)

    return scatter(rows, dest.reshape(1, n))


def _sc_gather_rows(src, idx):
    n = idx.shape[0]
    w = src.shape[1]

    @pl.kernel(out_type=jax.ShapeDtypeStruct((n, w), src.dtype), mesh=_sc_mesh(), scratch_types=[])
    def gather(x_hbm, i_hbm, o_hbm):
        def body(i_vmem, o_vmem):
            pltpu.sync_copy(x_hbm.at[i_vmem.at[0]], o_vmem)

        pltpu.emit_pipeline(
            body,
            grid=(n // SC_WINDOW,),
            in_specs=[pl.BlockSpec((1, SC_WINDOW), lambda i: (0, i))],
            out_specs=[pl.BlockSpec((SC_WINDOW, w), lambda i: (i, 0))],
            core_axis_name=("core", "subcore"),
            dimension_semantics=(pltpu.PARALLEL,),
        )(i_hbm, o_hbm)

    return gather(src, idx.reshape(1, n))


def _row_index_kernel(ps_ref, route_ref, o_ref, *, n_rows):
    bucket = route_ref[0:1, :].astype(I32)
    start = jnp.zeros(bucket.shape, I32)
    for b in range(N_BUCKETS):
        start = jnp.where(bucket == b, ps_ref[b], start)
    dest = start + route_ref[3:4, :].astype(I32)
    part = lax.broadcasted_iota(I32, o_ref.shape, 0)
    o_ref[...] = part * n_rows + dest


def _row_index(pad_starts, route, n_rows):
    t = route.shape[1]
    tm = min(TILE_ROW_INDEX, t)
    return pl.pallas_call(
        functools.partial(_row_index_kernel, n_rows=n_rows),
        grid=(t // tm,),
        in_specs=[pl.BlockSpec(memory_space=pltpu.SMEM), pl.BlockSpec((SUBLANES, tm), lambda i: (0, i))],
        out_specs=pl.BlockSpec((SUBLANES, tm), lambda i: (0, i)),
        out_shape=jax.ShapeDtypeStruct((SUBLANES, t), I32),
        compiler_params=_params("arbitrary"),
        name="row_index",
    )(pad_starts, route)


def _residual_kernel(x_ref, y_ref, g2_ref, o_ref):
    y = _unpack_bf16_pairs(jnp.concatenate([y_ref[c] for c in range(OUT_PARTS)], axis=1))
    o_ref[...] = x_ref[...] + g2_ref[...] * y


def _residual(x1, ytok, g2, seq, part):
    t, d = x1.shape
    tm = min(TILE_RESIDUAL, seq)
    tpb = seq // tm
    p, n_part = part
    steps = t // tm // n_part
    off = p * steps
    return pl.pallas_call(
        _residual_kernel,
        grid=(steps,),
        in_specs=[
            pl.BlockSpec((tm, d), lambda i: (i + off, 0)),
            pl.BlockSpec((OUT_PARTS, tm, LANES), lambda i: (0, i, 0)),
            pl.BlockSpec((None, 1, d), lambda i: ((i + off) // tpb, 0, 0)),
        ],
        out_specs=pl.BlockSpec((tm, d), lambda i: (i + off, 0)),
        out_shape=jax.ShapeDtypeStruct((t, d), F32),
        input_output_aliases={0: 0},
        compiler_params=_params("arbitrary"),
        name="residual",
    )(x1, ytok, g2)


def _expert_kernel(ea_ref, eb_ref, nr_ref, xs_ref, wga_ref, wua_ref, wda_ref, wgb_ref, wub_ref, wdb_ref, ys_ref,
                   ga_ref, ua_ref, gb_ref, ub_ref, dab_ref):
    j = pl.program_id(0)
    nr = nr_ref[0]
    prev = jnp.maximum(j - 1, 0)
    f = wda_ref.shape[0]

    @pl.when((j == 0) | (ea_ref[j] != ea_ref[prev]))
    def _():
        ga_ref[...] = wga_ref[...].astype(BF16)
        ua_ref[...] = wua_ref[...].astype(BF16)
        dab_ref[0:f, :] = wda_ref[...].astype(BF16)

    @pl.when((j == 0) | (eb_ref[j] != eb_ref[prev]))
    def _():
        gb_ref[...] = wgb_ref[...].astype(BF16)
        ub_ref[...] = wub_ref[...].astype(BF16)
        dab_ref[f:2 * f, :] = wdb_ref[...].astype(BF16)

    @pl.when(j < nr)
    def _():
        x = _unpack_bf16_pairs(jnp.concatenate([xs_ref[c] for c in range(PAY_PARTS - 1)], axis=1)).astype(BF16)
        gl = lax.bitcast_convert_type(xs_ref[PAY_PARTS - 1], F32)

        def gated_act(wg_ref, wu_ref, gate):
            gte = _dot(x, wg_ref[...])
            return (gte * _sigmoid(gte) * _dot(x, wu_ref[...]) * gate).astype(BF16)

        acts = jnp.concatenate([gated_act(ga_ref, ua_ref, gl[:, 0:1]), gated_act(gb_ref, ub_ref, gl[:, 1:2])], axis=1)
        y = _pack_bf16_pairs(_dot(acts, dab_ref[...]))
        for c in range(OUT_PARTS):
            ys_ref[c] = y[:, c * LANES:(c + 1) * LANES]

    @pl.when(j >= nr)
    def _():
        ys_ref[...] = jnp.zeros(ys_ref.shape, U32)


def _experts(blk_ea, blk_eb, n_real, xs, wg, wu, wd, d):
    n_rows = xs.shape[1]
    nblk = n_rows // EXPERT_BLOCK
    f = wg.shape[2]
    grid_spec = pltpu.PrefetchScalarGridSpec(
        num_scalar_prefetch=3,
        grid=(nblk,),
        in_specs=[
            pl.BlockSpec((PAY_PARTS, EXPERT_BLOCK, LANES), lambda j, ea, eb, nr: (0, j, 0)),
            pl.BlockSpec((None, d, f), lambda j, ea, eb, nr: (ea[j], 0, 0)),
            pl.BlockSpec((None, d, f), lambda j, ea, eb, nr: (ea[j], 0, 0)),
            pl.BlockSpec((None, f, d), lambda j, ea, eb, nr: (ea[j], 0, 0)),
            pl.BlockSpec((None, d, f), lambda j, ea, eb, nr: (eb[j], 0, 0)),
            pl.BlockSpec((None, d, f), lambda j, ea, eb, nr: (eb[j], 0, 0)),
            pl.BlockSpec((None, f, d), lambda j, ea, eb, nr: (eb[j], 0, 0)),
        ],
        out_specs=pl.BlockSpec((OUT_PARTS, EXPERT_BLOCK, LANES), lambda j, ea, eb, nr: (0, j, 0)),
        scratch_shapes=[pltpu.VMEM((d, f), BF16)] * 4 + [pltpu.VMEM((2 * f, d), BF16)],
    )
    return pl.pallas_call(
        _expert_kernel,
        grid_spec=grid_spec,
        out_shape=jax.ShapeDtypeStruct((OUT_PARTS, n_rows, LANES), U32),
        compiler_params=_params("arbitrary"),
        name="experts",
    )(blk_ea, blk_eb, n_real, xs, wg, wu, wd, wg, wu, wd)


_PAIR_A = (0, 0, 0, 1, 2, 2)
_PAIR_B = (1, 2, 3, 3, 3, 1)


def kernel(x, c, positions, ada_w, ada_b, norm_mix_w, norm_ffn_w, w_in, b_igate, b_fgate, q_norm_w, k_norm_w,
           sinks, conv_w, conv_b, mlstm_norm_w, w_attn_up, w_mlstm_up, w_out, router_w, router_bias,
           w_gate, w_up, w_down):
    batch, seq, d = x.shape
    depth = w_in.shape[0]
    t = batch * seq
    qw = N_HEADS * HEAD_DIM
    kvw = N_KV * HEAD_DIM
    mw = M_HEADS * M_DIM

    o = 0
    cols = {}
    for name, wdt in (("q", qw), ("k", kvw), ("v", kvw), ("mqk", 2 * mw), ("mv", mw), ("mi", M_HEADS),
                      ("mf", M_HEADS), ("mo", mw), ("ga", d), ("gb", d)):
        cols[name] = (o, o + wdt)
        o += wdt

    def in_weights(w_src, l):
        def wc(name):
            s, e = cols[name]
            return w_src[l:l + 1, :, s:e]
        return (jnp.concatenate([wc("q"), wc("k"), wc("v")], axis=2).astype(BF16),
                jnp.concatenate([wc("mqk"), wc("mv"), wc("mo")], axis=2).astype(BF16),
                jnp.concatenate([wc("mi"), wc("mf"), jnp.zeros((1, d, LANES - 2 * M_HEADS), F32)], axis=2).astype(BF16),
                jnp.concatenate([wc("ga"), wc("gb")], axis=2).astype(BF16))

    w_in_l = in_weights(w_in, 0)
    w_au = w_attn_up.astype(BF16)
    w_mu = w_mlstm_up.astype(BF16)
    w_o = w_out.astype(BF16)
    n_e = w_gate.shape[1]
    w_g8 = w_gate.reshape(depth * n_e, d, -1)
    w_u8 = w_up.reshape(depth * n_e, d, -1)
    w_d = w_down.reshape(depth * n_e, -1, d)

    rw_t = router_w.astype(F32).T
    rw_top = rw_t.astype(BF16)
    rw_hi = jnp.concatenate([rw_top, (rw_t - rw_top.astype(F32)).astype(BF16)], axis=0)
    rw_lo = rw_top
    rb = jnp.broadcast_to(router_bias.astype(F32)[:, None], (n_e, LANES))

    qn_w = jnp.tile(q_norm_w * (HEAD_DIM ** -0.5), (1, N_HEADS)).reshape(depth, 1, qw)
    kn_w = jnp.tile(k_norm_w, (1, N_KV)).reshape(depth, 1, kvw)
    seg = jnp.arange(qw) // HEAD_DIM
    bdq = jnp.where(seg[:, None] == seg[None, :], 1.0 / HEAD_DIM, 0.0).astype(BF16)
    bdk = bdq[:kvw, :kvw]

    inv_freq = ROPE_THETA ** (-(jnp.arange(0, ROPE_DIM, 2, dtype=F32) / ROPE_DIM))
    ang = positions.astype(F32).reshape(1, t) * inv_freq[:, None]
    cos8, sin8 = jnp.cos(ang).T, jnp.sin(ang).T
    pad1 = jnp.ones((t, HEAD_DIM - ROPE_DIM), F32)
    pad0 = jnp.zeros((t, HEAD_DIM - ROPE_DIM), F32)
    cos_t = jnp.tile(jnp.concatenate([cos8, cos8, pad1], axis=1), (1, LANES // HEAD_DIM))
    sin_t = jnp.tile(jnp.concatenate([-sin8, sin8, pad0], axis=1), (1, LANES // HEAD_DIM))

    gate_bias = jnp.concatenate([b_igate, b_fgate], axis=1).astype(F32)
    bcol = jnp.broadcast_to(gate_bias[:, :, None], (depth, 2 * M_HEADS, LANES))

    tm_merge = min(TILE_MERGE, seq)
    ii = jnp.arange(tm_merge)
    tri = (ii[:, None] <= ii[None, :]).astype(BF16)

    n_blk = (t + N_BUCKETS * (EXPERT_BLOCK - 1)) // EXPERT_BLOCK + 1
    n_rows = n_blk * EXPERT_BLOCK
    pair_a = jnp.asarray(_PAIR_A, I32)
    pair_b = jnp.asarray(_PAIR_B, I32)

    c_pad = jnp.zeros((SUBLANES, d), F32).at[:batch].set(c)
    mod = _ada_mod(c_pad, ada_w, ada_b)[:, :batch]

    xf = x.reshape(t, d)
    moe = None
    for l in range(depth):
        sh1, sc1, g1, sh2, sc2, g2 = [m.reshape(batch, 1, d) for m in jnp.split(mod[l], 6, axis=-1)]

        nmw = norm_mix_w.reshape(depth, 1, d)
        if moe is None:
            a_in, m_in, mg, grow = _inproj(xf, None, sc1, sh1, nmw[l:l + 1], *w_in_l, 0, seq)
        else:
            outs = None
            for p in range(GATHER_PARTS):
                outs = _inproj(xf, (moe[0][p], moe[1]), sc1, sh1, nmw[l:l + 1], *w_in_l, 0, seq,
                               part=(p, GATHER_PARTS), carry=None if outs is None else outs[:4])
                xf = outs[4]
            a_in, m_in, mg, grow = outs[:4]
        o_attn = _attention(a_in, cos_t, sin_t, sinks[l], qn_w[l], kn_w[l], bdq, bdk, batch, seq)
        hm = _mlstm(m_in, grow, conv_w[l], conv_b[l].reshape(1, -1), bcol[l], mlstm_norm_w[l].reshape(1, mw),
                    batch, seq)
        x1, pay, route, cnt = _merge(o_attn, hm, mg, xf, g1, sc2, sh2, norm_ffn_w.reshape(depth, 1, d),
                                     w_au, w_mu, w_o, rw_hi, rw_lo, rb, tri, l, seq)

        counts = cnt[:N_BUCKETS, 0].astype(I32)
        padded = (counts + EXPERT_BLOCK - 1) // EXPERT_BLOCK * EXPERT_BLOCK
        pad_ends = jnp.cumsum(padded)
        pad_starts = pad_ends - padded
        row_idx = _row_index(jnp.concatenate([pad_starts, jnp.zeros((BUCKET_ROWS - N_BUCKETS,), I32)]), route, n_rows)
        if l + 1 < depth:
            w_in, row_idx = lax.optimization_barrier((w_in, row_idx))
            w_in_l = in_weights(w_in, l + 1)
        blk_start = jnp.arange(n_blk, dtype=I32) * EXPERT_BLOCK
        blk_bucket = jnp.minimum(jnp.sum((pad_ends[None, :] <= blk_start[:, None]).astype(I32), axis=1), N_BUCKETS - 1)
        grp = blk_bucket // N_PAIRS
        blk_ea = (l * n_e + grp * EPG + pair_a[blk_bucket % N_PAIRS]).astype(I32)
        blk_eb = (l * n_e + grp * EPG + pair_b[blk_bucket % N_PAIRS]).astype(I32)
        n_real = (pad_ends[-1:] // EXPERT_BLOCK).astype(I32)

        xs = _sc_scatter_rows(pay.reshape(PAY_PARTS * t, LANES), row_idx[:PAY_PARTS].reshape(-1),
                              PAY_PARTS * n_rows).reshape(PAY_PARTS, n_rows, LANES)
        ys = _experts(blk_ea, blk_eb, n_real, xs, w_g8, w_u8, w_d, d)
        tp = t // GATHER_PARTS
        ys_rows = ys.reshape(OUT_PARTS * n_rows, LANES)
        ytok = [_sc_gather_rows(ys_rows, row_idx[:OUT_PARTS, p * tp:(p + 1) * tp].reshape(-1)).reshape(OUT_PARTS, tp, LANES)
                for p in range(GATHER_PARTS)]
        xf, moe = x1, (ytok, g2)
    for p in range(GATHER_PARTS):
        xf = _residual(xf, moe[0][p], moe[1], seq, (p, GATHER_PARTS))
    return xf.reshape(batch, seq, d)
```

```python
import functools

import jax
import jax.numpy as jnp
from jax import lax
from jax.experimental import pallas as pl
from jax.experimental.pallas import tpu as pltpu
from jax.experimental.pallas import tpu_sc as plsc

F32 = jnp.float32
BF16 = jnp.bfloat16
U32 = jnp.uint32
I32 = jnp.int32
HIGHEST = lax.Precision.HIGHEST

HEAD_DIM = 64
N_HEADS = 8
N_KV = 2
ROPE_DIM = 16
ROPE_THETA = 500000.0
ATTN_BLOCK = 128
M_HEADS = 4
M_DIM = 128
CONV_K = 4
N_EXPERTS = 16
N_GROUPS = 4
EPG = 4
EPS = 1e-6

LANES = 128
SUBLANES = 8

TILE_INPROJ = 1024
TILE_ATTN = 1024
TILE_MLSTM = 2048
TILE_MERGE = 1024
TILE_RESIDUAL = 1024
TILE_ROW_INDEX = 8192
ADA_COLS = 3072
CHUNK = 128
CHUNKS_PER_STEP = 4
GATE_CHUNKS_PER_STEP = 4
N_PAIRS = 6
N_BUCKETS = N_GROUPS * N_PAIRS
BUCKET_ROWS = 32
EXPERT_BLOCK = 512
PAY_PARTS = 5
OUT_PARTS = 4
SC_WINDOW = 128
GATHER_PARTS = 2
VMEM_LIMIT = 56 * 1024 * 1024


def _dot(a, b, precision=None):
    return jnp.dot(a, b, preferred_element_type=F32, precision=precision)


def _dot_nt(a, b):
    return lax.dot_general(a, b, (((1,), (1,)), ((), ())), preferred_element_type=F32)


def _sigmoid(x):
    return 1.0 / (1.0 + jnp.exp(-x))


def _log_sigmoid(x):
    return jnp.minimum(x, 0.0) - jnp.log1p(jnp.exp(-jnp.abs(x)))


def _pack_bf16_pairs(v, rounded=False):
    n = v.shape[1] // 2
    bits = lax.bitcast_convert_type(v if rounded else v.astype(BF16).astype(F32), U32)
    return bits[:, :n] | (bits[:, n:] >> 16)


def _unpack_bf16_pairs(w):
    hi = lax.bitcast_convert_type(w & jnp.uint32(0xFFFF0000), F32)
    lo = lax.bitcast_convert_type(w << 16, F32)
    return jnp.concatenate([hi, lo], axis=1)


def _params(*sem):
    return pltpu.CompilerParams(dimension_semantics=sem, vmem_limit_bytes=VMEM_LIMIT)


def _ada_kernel(c_ref, w_ref, b_ref, o_ref):
    c = c_ref[...]
    ca = c * _sigmoid(c)
    o_ref[0] = _dot(ca, w_ref[0], HIGHEST) + b_ref[0]


def _ada_mod(c_pad, ada_w, ada_b):
    depth, d, n = ada_w.shape
    tn = ADA_COLS
    return pl.pallas_call(
        _ada_kernel,
        grid=(depth, n // tn),
        in_specs=[
            pl.BlockSpec((SUBLANES, d), lambda l, j: (0, 0)),
            pl.BlockSpec((1, d, tn), lambda l, j: (l, 0, j)),
            pl.BlockSpec((1, 1, tn), lambda l, j: (l, 0, j)),
        ],
        out_specs=pl.BlockSpec((1, SUBLANES, tn), lambda l, j: (l, 0, j)),
        out_shape=jax.ShapeDtypeStruct((depth, SUBLANES, n), F32),
        compiler_params=_params("arbitrary", "arbitrary"),
        name="ada_mod",
    )(c_pad, ada_w, ada_b.reshape(depth, 1, n))


def _inproj_kernel(*refs, fuse_residual, n_carry=0):
    if n_carry:
        refs = refs[:10] + refs[10 + n_carry:]
    if fuse_residual:
        (x_ref, y_ref, g2_ref, sc_ref, sh_ref, nw_ref, wa_ref, wm_ref, wg_ref, wmg_ref,
         a_ref, m_ref, mg_ref, gr_ref, xo_ref, g_ref) = refs
        y = _unpack_bf16_pairs(jnp.concatenate([y_ref[c] for c in range(OUT_PARTS)], axis=1))
        x = x_ref[...] + g2_ref[...] * y
        xo_ref[...] = x
    else:
        (x_ref, sc_ref, sh_ref, nw_ref, wa_ref, wm_ref, wg_ref, wmg_ref,
         a_ref, m_ref, mg_ref, gr_ref, g_ref) = refs
        x = x_ref[...]
    ms = jnp.mean(x * x, axis=-1, keepdims=True)
    h = x * lax.rsqrt(ms + EPS) * (nw_ref[...] * (1.0 + sc_ref[...])) + sh_ref[...]
    hb = h.astype(BF16)
    a_ref[...] = _dot(hb, wa_ref[...]).astype(BF16)
    m_ref[...] = _dot(hb, wm_ref[...]).astype(BF16)
    mg_ref[...] = _sigmoid(_dot(hb, wmg_ref[...])).astype(BF16)
    g_ref[...] = _dot(hb, wg_ref[...])
    gr_ref[...] = g_ref[...].T[:SUBLANES, :]


def _inproj(x, moe, sc, sh, nw, wa, wm, wg, wmg, l, seq, part=None, carry=None):
    t, d = x.shape
    tm = min(TILE_INPROJ, seq)
    tpb = seq // tm
    p, n_part = part if part is not None else (0, 1)
    steps = t // tm // n_part
    off = p * steps
    row = lambda i: (i + off, 0)
    bsel = lambda i: ((i + off) // tpb, 0, 0)
    wsel = lambda i: (l, 0, 0)
    once = pl.Buffered(1)
    na, nm, ng, nmg = wa.shape[2], wm.shape[2], wg.shape[2], wmg.shape[2]
    fuse = moe is not None
    moe_specs = [pl.BlockSpec((OUT_PARTS, tm, LANES), lambda i: (0, i, 0)), pl.BlockSpec((None, 1, d), bsel)]
    carry = list(carry) if carry is not None else []
    aliases = {10 + k: k for k in range(len(carry))}
    if part is not None:
        aliases[0] = 4
    return pl.pallas_call(
        functools.partial(_inproj_kernel, fuse_residual=fuse, n_carry=len(carry)),
        grid=(steps,),
        in_specs=[pl.BlockSpec((tm, d), row)] + (moe_specs if fuse else []) + [
            pl.BlockSpec((None, 1, d), bsel),
            pl.BlockSpec((None, 1, d), bsel),
            pl.BlockSpec((None, 1, d), wsel),
            pl.BlockSpec((None, d, na), wsel, pipeline_mode=once),
            pl.BlockSpec((None, d, nm), wsel, pipeline_mode=once),
            pl.BlockSpec((None, d, ng), wsel, pipeline_mode=once),
            pl.BlockSpec((None, d, nmg), wsel, pipeline_mode=once),
        ] + [pl.BlockSpec(memory_space=pl.ANY)] * len(carry),
        out_specs=[
            pl.BlockSpec((tm, na), row),
            pl.BlockSpec((tm, nm), row),
            pl.BlockSpec((tm, nmg), row),
            pl.BlockSpec((SUBLANES, tm), lambda i: (0, i + off)),
        ] + ([pl.BlockSpec((tm, d), row)] if fuse else []),
        out_shape=[
            jax.ShapeDtypeStruct((t, na), BF16),
            jax.ShapeDtypeStruct((t, nm), BF16),
            jax.ShapeDtypeStruct((t, nmg), BF16),
            jax.ShapeDtypeStruct((SUBLANES, t), F32),
        ] + ([jax.ShapeDtypeStruct((t, d), F32)] if fuse else []),
        scratch_shapes=[pltpu.VMEM((tm, ng), F32)],
        input_output_aliases=aliases,
        compiler_params=_params("arbitrary"),
        name="inproj",
    )(x, *(moe if fuse else ()), sc, sh, nw, wa, wm, wg, wmg, *carry)


def _rope(t, cos, sin):
    w = t.shape[1]
    reps = w // LANES
    cosw = jnp.concatenate([cos] * reps, axis=1) if reps > 1 else cos
    sinw = jnp.concatenate([sin] * reps, axis=1) if reps > 1 else sin
    lane = lax.broadcasted_iota(I32, t.shape, 1)
    half = ROPE_DIM // 2
    up = pltpu.roll(t, w - half, axis=1)
    dn = pltpu.roll(t, half, axis=1)
    partner = jnp.where((lane % ROPE_DIM) < half, up, dn)
    return t * cosw + partner * sinw


def _head_norm(t, bd, w):
    ms = _dot((t * t).astype(BF16), bd)
    return t * lax.rsqrt(ms + EPS) * w


def _attn_kernel(sink_ref, cur_ref, prev_ref, cos_ref, sin_ref, cosp_ref, sinp_ref,
                 qw_ref, kw_ref, bdq_ref, bdk_ref, o_ref):
    tq = cur_ref.shape[0]
    nj = tq // ATTN_BLOCK
    qw = N_HEADS * HEAD_DIM
    kw = N_KV * HEAD_DIM
    blk0 = pl.program_id(1) * nj

    cur = cur_ref[...]
    q = cur[:, :qw].astype(F32)
    kc = cur[:, qw:qw + kw].astype(F32)
    vc = cur[:, qw + kw:].astype(F32)
    prev = prev_ref[...]
    kp = prev[:, :kw].astype(F32)
    vp = prev[:, kw:].astype(F32)

    cos, sin = cos_ref[...], sin_ref[...]
    q = _rope(_head_norm(q, bdq_ref[...], qw_ref[...]), cos, sin)
    kc = _rope(_head_norm(kc, bdk_ref[...], kw_ref[...]), cos, sin)
    kp = _rope(_head_norm(kp, bdk_ref[...], kw_ref[...]), cosp_ref[...], sinp_ref[...])
    qb = q.astype(BF16)

    def both_halves(x2):
        swapped = pltpu.roll(x2, HEAD_DIM, axis=1)
        first = lax.broadcasted_iota(I32, x2.shape, 1) < HEAD_DIM
        return jnp.concatenate([jnp.where(first, x2, swapped), jnp.where(first, swapped, x2)], axis=1).astype(BF16)

    k_all = both_halves(jnp.concatenate([kp, kc], axis=0))
    v_all = both_halves(jnp.concatenate([vp, vc], axis=0))

    lane = lax.broadcasted_iota(I32, (ATTN_BLOCK, LANES), 1)
    lo = lane < HEAD_DIM
    zero = jnp.zeros((ATTN_BLOCK, LANES), BF16)
    g_heads = N_HEADS // N_KV
    ri = lax.broadcasted_iota(I32, (g_heads * ATTN_BLOCK, ATTN_BLOCK), 0) % ATTN_BLOCK
    ci = lax.broadcasted_iota(I32, (g_heads * ATTN_BLOCK, ATTN_BLOCK), 1)
    from_prev = ci > ri
    head_row = lax.broadcasted_iota(I32, (g_heads * ATTN_BLOCK, 1), 0) // ATTN_BLOCK
    ones_v = jnp.ones((2 * ATTN_BLOCK, LANES), BF16)

    tiles = [(j, g) for j in range(nj) for g in range(N_KV)]
    scores = {}
    for j, g in tiles:
        rows = slice(j * ATTN_BLOCK, (j + 1) * ATTN_BLOCK)
        band = slice(j * ATTN_BLOCK, (j + 2) * ATTN_BLOCK)
        qp0 = qb[rows, (2 * g) * LANES:(2 * g + 1) * LANES]
        qp1 = qb[rows, (2 * g + 1) * LANES:(2 * g + 2) * LANES]
        q4 = jnp.concatenate([jnp.where(lo, qp0, zero), jnp.where(lo, zero, qp0),
                              jnp.where(lo, qp1, zero), jnp.where(lo, zero, qp1)], axis=0)
        scores[j, g] = _dot_nt(q4, k_all[band, g * LANES:(g + 1) * LANES])

    probs, sink_term = {}, {}
    for j, g in tiles:
        s2 = scores[j, g]
        if j == 0:
            prev_ok = ci > ri + (1 - jnp.minimum(blk0, 1)) * ATTN_BLOCK
            s = jnp.where(prev_ok, s2[:, :ATTN_BLOCK], jnp.where(from_prev, -jnp.inf, s2[:, ATTN_BLOCK:]))
        else:
            s = jnp.where(from_prev, s2[:, :ATTN_BLOCK], s2[:, ATTN_BLOCK:])
        sink = jnp.full((g_heads * ATTN_BLOCK, 1), sink_ref[g_heads * g], F32)
        for r in range(1, g_heads):
            sink = jnp.where(head_row == r, sink_ref[g_heads * g + r], sink)
        m = jnp.maximum(jnp.max(s, axis=-1, keepdims=True), sink)
        p = jnp.exp(s - m)
        probs[j, g] = jnp.concatenate([jnp.where(from_prev, p, 0.0), jnp.where(from_prev, 0.0, p)],
                                      axis=1).astype(BF16)
        sink_term[j, g] = jnp.exp(sink - m)

    for j, g in tiles:
        rows = slice(j * ATTN_BLOCK, (j + 1) * ATTN_BLOCK)
        band = slice(j * ATTN_BLOCK, (j + 2) * ATTN_BLOCK)
        o8 = _dot(probs[j, g], jnp.concatenate([v_all[band, g * LANES:(g + 1) * LANES], ones_v], axis=1))
        o4 = o8[:, :LANES] / (o8[:, LANES:] + sink_term[j, g])
        b = ATTN_BLOCK
        o_ref[rows, (2 * g) * LANES:(2 * g + 1) * LANES] = jnp.where(lo, o4[0:b], o4[b:2 * b]).astype(BF16)
        o_ref[rows, (2 * g + 1) * LANES:(2 * g + 2) * LANES] = jnp.where(
            lo, o4[2 * b:3 * b], o4[3 * b:4 * b]).astype(BF16)


def _attention(a_in, cos_t, sin_t, sinks_l, qw, kw, bdq, bdk, batch, seq):
    t = a_in.shape[0]
    tq = min(TILE_ATTN, seq)
    nj = tq // ATTN_BLOCK
    tpb = seq // tq
    bpb = seq // ATTN_BLOCK
    qwid = N_HEADS * HEAD_DIM
    kvw = 2 * N_KV * HEAD_DIM
    cur = lambda b, i: (b * tpb + i, 0)
    prv = lambda b, i: (b * bpb + jnp.maximum(i * nj - 1, 0), qwid // kvw)
    prv0 = lambda b, i: (b * bpb + jnp.maximum(i * nj - 1, 0), 0)
    const = lambda b, i: (0, 0)
    return pl.pallas_call(
        _attn_kernel,
        grid=(batch, tpb),
        in_specs=[
            pl.BlockSpec(memory_space=pltpu.SMEM),
            pl.BlockSpec((tq, qwid + kvw), cur),
            pl.BlockSpec((ATTN_BLOCK, kvw), prv),
            pl.BlockSpec((tq, LANES), cur),
            pl.BlockSpec((tq, LANES), cur),
            pl.BlockSpec((ATTN_BLOCK, LANES), prv0),
            pl.BlockSpec((ATTN_BLOCK, LANES), prv0),
            pl.BlockSpec((1, qwid), const),
            pl.BlockSpec((1, kvw // 2), const),
            pl.BlockSpec((qwid, qwid), const),
            pl.BlockSpec((kvw // 2, kvw // 2), const),
        ],
        out_specs=pl.BlockSpec((tq, qwid), cur),
        out_shape=jax.ShapeDtypeStruct((t, qwid), BF16),
        compiler_params=_params("arbitrary", "arbitrary"),
        name="swa_attention",
    )(sinks_l, a_in, a_in, cos_t, sin_t, cos_t, sin_t, qw, kw, bdq, bdk)


def _mlstm_kernel(min_ref, gr_ref, cw_ref, cb_ref, bcol_ref, nw_ref,
                  hm_ref, ext_ref, q_ref, kt_ref, st_ref, mx_ref, ab_ref, bc_ref):
    tt = min_ref.shape[0]
    mw = M_HEADS * M_DIM
    nchunks = tt // CHUNK

    @pl.when(pl.program_id(1) == 0)
    def _():
        ext_ref[0:SUBLANES, :] = jnp.zeros((SUBLANES, 2 * mw), F32)
        st_ref[...] = jnp.zeros(st_ref.shape, F32)
        mx_ref[...] = jnp.zeros(mx_ref.shape, F32)

    def conv_block(cols):
        u = min_ref[:, cols].astype(F32)
        ext_ref[SUBLANES:SUBLANES + tt, cols] = u
        acc = cb_ref[:, cols] + cw_ref[CONV_K - 1:CONV_K, cols] * u
        for jj in range(CONV_K - 1):
            off = SUBLANES - (CONV_K - 1) + jj
            acc = acc + cw_ref[jj:jj + 1, cols] * ext_ref[off:off + tt, cols]
        ext_ref[0:SUBLANES, cols] = u[tt - SUBLANES:tt, :]
        return acc * _sigmoid(acc)

    def q_body(h, carry):
        cols = pl.ds(pl.multiple_of(h * M_DIM, M_DIM), M_DIM)
        q_ref[:, cols] = conv_block(cols).astype(BF16)
        return carry

    def k_body(h, carry):
        off = pl.multiple_of(h * M_DIM, M_DIM)
        act = conv_block(pl.ds(mw + off, M_DIM)) * (M_DIM ** -0.5)
        for j in range(nchunks):
            kt_ref[pl.ds(off, M_DIM), j * CHUNK:(j + 1) * CHUNK] = act[j * CHUNK:(j + 1) * CHUNK, :].T
        return carry

    lax.fori_loop(0, M_HEADS, q_body, 0)
    lax.fori_loop(0, M_HEADS, k_body, 0)

    ri = lax.broadcasted_iota(I32, (CHUNK, CHUNK), 0)
    ci = lax.broadcasted_iota(I32, (CHUNK, CHUNK), 1)
    causal = ci <= ri
    triu = jnp.where(ri <= ci, 1.0, 0.0).astype(BF16)
    ones_half = jnp.ones((CHUNK, M_DIM), BF16)
    mean_mat = jnp.full((M_DIM, M_DIM), 1.0 / M_DIM, BF16)
    sub = lax.broadcasted_iota(I32, (SUBLANES, CHUNK), 0)
    heads = range(M_HEADS)

    pad_rows = jnp.zeros((CHUNK - SUBLANES, CHUNK), F32)
    zero_rows = jnp.zeros((SUBLANES, CHUNK), F32)

    def gate_body(jg, carry):
        for u_ in range(GATE_CHUNKS_PER_STEP):
            rs = pl.ds(pl.multiple_of((jg * GATE_CHUNKS_PER_STEP + u_) * CHUNK, CHUNK), CHUNK)
            gr = gr_ref[:, rs] + bcol_ref[...]
            ls = _log_sigmoid(gr)
            ls1 = ls.astype(BF16).astype(F32)
            ls2 = (ls - ls1).astype(BF16).astype(F32)
            pieces = jnp.concatenate([ls1, ls2, ls - ls1 - ls2, zero_rows], axis=0).astype(BF16)
            sums = _dot(pieces, triu)
            br = sums[0:SUBLANES] + sums[SUBLANES:2 * SUBLANES] + sums[2 * SUBLANES:3 * SUBLANES]
            ab = jnp.where(sub < M_HEADS, gr - pltpu.roll(br, M_HEADS, axis=0), br)
            ab_ref[:, rs] = ab
            bc_ref[rs, :] = jnp.concatenate([ab, pad_rows], axis=0).T
        return carry

    lax.fori_loop(0, nchunks // GATE_CHUNKS_PER_STEP, gate_body, 0)

    def group_body(cg, carry):
        rows, ab = [], []
        for u_ in range(CHUNKS_PER_STEP):
            r0 = pl.multiple_of((cg * CHUNKS_PER_STEP + u_) * CHUNK, CHUNK)
            rows.append(pl.ds(r0, CHUNK))
            ab.append(ab_ref[:, rows[u_]])
        lanes = [(u_, h) for u_ in range(CHUNKS_PER_STEP) for h in heads]
        a_r = {(u_, h): ab[u_][h:h + 1, :] for u_, h in lanes}
        b_last = {(u_, h): ab[u_][M_HEADS + h:M_HEADS + h + 1, CHUNK - 1:CHUNK] for u_, h in lanes}

        m_prev, a_max, a_dec, s_in = {}, {}, {}, {}
        m_run = [mx_ref[h][0:1, 0:1] for h in heads]
        for k in lanes:
            u_, h = k
            m_prev[k] = m_run[h]
            a_max[k] = jnp.max(a_r[k], axis=-1, keepdims=True)
            m_loc = b_last[k] + a_max[k]
            m_new = jnp.maximum(b_last[k] + m_prev[k], m_loc)
            a_dec[k] = jnp.exp(b_last[k] + m_prev[k] - m_new)
            s_in[k] = jnp.exp(m_loc - m_new)
            m_run[h] = m_new
        for h in heads:
            mx_ref[h] = jnp.broadcast_to(m_run[h], (SUBLANES, LANES))

        q, v_ext, s_qk, kv = {}, {}, {}, {}
        for k in lanes:
            u_, h = k
            rs = rows[u_]
            q[k] = q_ref[rs, h * M_DIM:(h + 1) * M_DIM]
            kt = kt_ref[h * M_DIM:(h + 1) * M_DIM, rs]
            v = min_ref[rs, 2 * mw + h * M_DIM:2 * mw + (h + 1) * M_DIM]
            v_ext[k] = jnp.concatenate([v, ones_half], axis=1)
            s_qk[k] = _dot(q[k], kt.astype(BF16))
            e_r = jnp.exp(a_r[k] - a_max[k])
            kv[k] = _dot((kt * e_r).astype(BF16), v_ext[k])

        thr, qk = {}, {}
        for k in lanes:
            u_, h = k
            a_mat = jnp.where(causal, a_r[k], -jnp.inf)
            mu = jnp.maximum(jnp.max(a_mat, axis=-1, keepdims=True), m_prev[k])
            b_c = bc_ref[rows[u_], M_HEADS + h:M_HEADS + h + 1]
            thr[k] = jnp.broadcast_to(jnp.exp(-(b_c + mu)), (CHUNK, M_DIM))
            mu_b = jnp.broadcast_to(mu, (CHUNK, CHUNK))
            inter = jnp.exp(m_prev[k] - mu_b)
            qk[k] = jnp.concatenate([(s_qk[k] * jnp.exp(a_mat - mu_b)).astype(BF16),
                                     (q[k].astype(F32) * inter).astype(BF16)], axis=1)

        state = [st_ref[h] for h in heads]
        for k in lanes:
            u_, h = k
            hs = slice(h * M_DIM, (h + 1) * M_DIM)
            num = _dot(qk[k], jnp.concatenate([v_ext[k], state[h].astype(BF16)], axis=0))
            state[h] = a_dec[k] * state[h] + s_in[k] * kv[k]
            den = jnp.maximum(jnp.abs(num[:, M_DIM:]), thr[k])
            hh = num[:, :M_DIM] / den
            msq = _dot((hh * hh).astype(BF16), mean_mat)
            hn = hh * lax.rsqrt(msq + EPS) * nw_ref[:, hs]
            og = min_ref[rows[u_], 3 * mw + h * M_DIM:3 * mw + (h + 1) * M_DIM].astype(F32)
            hm_ref[rows[u_], hs] = (_sigmoid(og) * hn).astype(BF16)
        for h in heads:
            st_ref[h] = state[h]
        return carry

    lax.fori_loop(0, nchunks // CHUNKS_PER_STEP, group_body, 0)


def _mlstm(m_in, grow, conv_w, conv_b, bcol, nw, batch, seq):
    t = m_in.shape[0]
    tt = min(TILE_MLSTM, seq)
    tpb = seq // tt
    mw = M_HEADS * M_DIM
    cur = lambda b, i: (b * tpb + i, 0)
    const = lambda b, i: (0, 0)
    return pl.pallas_call(
        _mlstm_kernel,
        grid=(batch, tpb),
        in_specs=[
            pl.BlockSpec((tt, 4 * mw), cur),
            pl.BlockSpec((SUBLANES, tt), lambda b, i: (0, b * tpb + i)),
            pl.BlockSpec((CONV_K, 2 * mw), const),
            pl.BlockSpec((1, 2 * mw), const),
            pl.BlockSpec((SUBLANES, LANES), const),
            pl.BlockSpec((1, mw), const),
        ],
        out_specs=pl.BlockSpec((tt, mw), cur),
        out_shape=jax.ShapeDtypeStruct((t, mw), BF16),
        scratch_shapes=[
            pltpu.VMEM((tt + SUBLANES, 2 * mw), F32),
            pltpu.VMEM((tt, mw), BF16),
            pltpu.VMEM((mw, tt), F32),
            pltpu.VMEM((M_HEADS, M_DIM, 2 * M_DIM), F32),
            pltpu.VMEM((M_HEADS, SUBLANES, LANES), F32),
            pltpu.VMEM((SUBLANES, tt), F32),
            pltpu.VMEM((tt, LANES), F32),
        ],
        compiler_params=_params("arbitrary", "arbitrary"),
        name="mlstm",
    )(m_in, grow, conv_w, conv_b, bcol, nw)


def _merge_kernel(o_ref, hm_ref, mg_ref, x_ref, g1_ref, sc_ref, sh_ref, nw_ref,
                  wa_ref, wm_ref, wo_ref, rwh_ref, rwl_ref, rb_ref, tri_ref,
                  x1_ref, pay_ref, route_ref, cnt_ref, carry_ref):
    tm, d = x_ref.shape

    @pl.when(pl.program_id(0) == 0)
    def _():
        carry_ref[...] = jnp.zeros(carry_ref.shape, F32)

    ya = _dot(o_ref[...], wa_ref[...])
    yb = _dot(hm_ref[...], wm_ref[...])
    mg = mg_ref[...]
    merged = mg[:, :d].astype(F32) * ya + mg[:, d:].astype(F32) * yb
    x1 = x_ref[...] + g1_ref[...] * _dot(merged.astype(BF16), wo_ref[...])
    x1_ref[...] = x1

    ms = jnp.mean(x1 * x1, axis=-1, keepdims=True)
    h2 = x1 * lax.rsqrt(ms + EPS) * (nw_ref[...] * (1.0 + sc_ref[...])) + sh_ref[...]
    hi = h2.astype(BF16)
    hif = hi.astype(F32)
    lo = (h2 - hif).astype(BF16)
    r_hi = _dot_nt(rwh_ref[...], hi)
    r_lo = _dot_nt(rwl_ref[...], lo)
    sc_t = _sigmoid(r_hi[:N_EXPERTS] + r_hi[N_EXPERTS:] + r_lo)
    sel_t = sc_t + rb_ref[:, 0:1]

    def row(a, e):
        return a[e:e + 1, :]

    best = None
    gi = jnp.zeros((1, tm), I32)
    for g in range(N_GROUPS):
        r = [row(sel_t, EPG * g + i) for i in range(EPG)]
        gs = None
        for i in range(EPG):
            for j in range(i + 1, EPG):
                pr = r[i] + r[j]
                gs = pr if gs is None else jnp.maximum(gs, pr)
        if best is None:
            best = gs
        else:
            upd = gs > best
            gi = jnp.where(upd, g, gi)
            best = jnp.maximum(best, gs)

    def pick(a, i):
        out = row(a, i)
        for g in range(1, N_GROUPS):
            out = jnp.where(gi == g, row(a, EPG * g + i), out)
        return out

    v = [pick(sel_t, i) for i in range(EPG)]
    s = [pick(sc_t, i) for i in range(EPG)]

    def argmax4(vals):
        bv, bi = vals[0], jnp.zeros((1, tm), I32)
        for i in range(1, EPG):
            upd = vals[i] > bv
            bi = jnp.where(upd, i, bi)
            bv = jnp.maximum(bv, vals[i])
        return bi

    i1 = argmax4(v)
    i2 = argmax4([jnp.where(i1 == i, -jnp.inf, v[i]) for i in range(EPG)])
    ia = jnp.minimum(i1, i2)
    ib = jnp.maximum(i1, i2)
    pidx = jnp.where(ia == 0, ib - 1, jnp.where(ia == 1, jnp.where(ib == 3, 3, 5), 4))
    bucket = gi * N_PAIRS + pidx

    def by_index(vals, idx):
        out = vals[0]
        for i in range(1, EPG):
            out = jnp.where(idx == i, vals[i], out)
        return out

    swap = pidx == N_PAIRS - 1
    s_lo, s_hi = by_index(s, ia), by_index(s, ib)
    s_a, s_b = jnp.where(swap, s_hi, s_lo), jnp.where(swap, s_lo, s_hi)
    gate_a = s_a / (s_a + s_b)
    gate_b = s_b / (s_a + s_b)

    brow = lax.broadcasted_iota(I32, (BUCKET_ROWS, tm), 0)
    onehot = brow == bucket
    cums = _dot(jnp.where(onehot, 1.0, 0.0).astype(BF16), tri_ref[...])
    carry = carry_ref[...]
    rank = jnp.sum(jnp.where(onehot, carry[:, 0:1] + cums, 0.0), axis=0, keepdims=True) - 1.0
    new_carry = carry + cums[:, tm - 1:tm]
    carry_ref[...] = new_carry
    cnt_ref[...] = new_carry

    route_ref[...] = jnp.concatenate(
        [bucket.astype(F32), gate_a, gate_b, rank, jnp.zeros((SUBLANES - 4, tm), F32)], axis=0)

    half = d // 2
    packed = _pack_bf16_pairs(hif, rounded=True)
    for cpart in range(half // LANES):
        pay_ref[cpart] = packed[:, cpart * LANES:(cpart + 1) * LANES]
    gates_t = jnp.concatenate([gate_a, gate_b, jnp.zeros((LANES - 2, tm), F32)], axis=0)
    pay_ref[half // LANES] = lax.bitcast_convert_type(gates_t.T, U32)


def _merge(o_attn, hm, mg, x, g1, sc2, sh2, nw, wa, wm, wo, rwh, rwl, rb, tri, l, seq):
    t, d = x.shape
    tm = tri.shape[0]
    tpb = seq // tm
    row = lambda i: (i, 0)
    bsel = lambda i: (i // tpb, 0, 0)
    wsel = lambda i: (l, 0, 0)
    const = lambda i: (0, 0)
    hw = o_attn.shape[1]
    return pl.pallas_call(
        _merge_kernel,
        grid=(t // tm,),
        in_specs=[
            pl.BlockSpec((tm, hw), row),
            pl.BlockSpec((tm, hw), row),
            pl.BlockSpec((tm, 2 * d), row),
            pl.BlockSpec((tm, d), row),
            pl.BlockSpec((None, 1, d), bsel),
            pl.BlockSpec((None, 1, d), bsel),
            pl.BlockSpec((None, 1, d), bsel),
            pl.BlockSpec((None, 1, d), wsel),
            pl.BlockSpec((None, hw, d), wsel),
            pl.BlockSpec((None, hw, d), wsel),
            pl.BlockSpec((None, d, d), wsel),
            pl.BlockSpec((2 * N_EXPERTS, d), const),
            pl.BlockSpec((N_EXPERTS, d), const),
            pl.BlockSpec((N_EXPERTS, LANES), const),
            pl.BlockSpec((tm, tm), const),
        ],
        out_specs=[
            pl.BlockSpec((tm, d), row),
            pl.BlockSpec((PAY_PARTS, tm, LANES), lambda i: (0, i, 0)),
            pl.BlockSpec((SUBLANES, tm), lambda i: (0, i)),
            pl.BlockSpec((BUCKET_ROWS, LANES), const),
        ],
        out_shape=[
            jax.ShapeDtypeStruct((t, d), F32),
            jax.ShapeDtypeStruct((PAY_PARTS, t, LANES), U32),
            jax.ShapeDtypeStruct((SUBLANES, t), F32),
            jax.ShapeDtypeStruct((BUCKET_ROWS, LANES), F32),
        ],
        scratch_shapes=[pltpu.VMEM((BUCKET_ROWS, LANES), F32)],
        compiler_params=_params("arbitrary"),
        name="merge_router",
    )(o_attn, hm, mg, x, g1, sc2, sh2, nw, wa, wm, wo, rwh, rwl, rb, tri)


def _sc_mesh():
    return plsc.VectorSubcoreMesh(core_axis_name="core", subcore_axis_name="subcore")


def _sc_scatter_rows(rows, dest, n_out):
    n, w = rows.shape

    @pl.kernel(out_type=jax.ShapeDtypeStruct((n_out, w), rows.dtype), mesh=_sc_mesh(), scratch_types=[])
    def scatter(x_hbm, i_hbm, o_hbm):
        def body(x_vmem, i_vmem):
            pltpu.sync_copy(x_vmem, o_hbm.at[i_vmem.at[0]])

        pltpu.emit_pipeline(
            body,
            grid=(n // SC_WINDOW,),
            in_specs=[pl.BlockSpec((SC_WINDOW, w), lambda i: (i, 0)),
                      pl.BlockSpec((1, SC_WINDOW), lambda i: (0, i))],
            out_specs=[],
            core_axis_name=("core", "subcore"),
            dimension_semantics=(pltpu.PARALLEL,),
        )(x_hbm, i_hbm)

    return scatter(rows, dest.reshape(1, n))


def _sc_gather_rows(src, idx):
    n = idx.shape[0]
    w = src.shape[1]

    @pl.kernel(out_type=jax.ShapeDtypeStruct((n, w), src.dtype), mesh=_sc_mesh(), scratch_types=[])
    def gather(x_hbm, i_hbm, o_hbm):
        def body(i_vmem, o_vmem):
            pltpu.sync_copy(x_hbm.at[i_vmem.at[0]], o_vmem)

        pltpu.emit_pipeline(
            body,
            grid=(n // SC_WINDOW,),
            in_specs=[pl.BlockSpec((1, SC_WINDOW), lambda i: (0, i))],
            out_specs=[pl.BlockSpec((SC_WINDOW, w), lambda i: (i, 0))],
            core_axis_name=("core", "subcore"),
            dimension_semantics=(pltpu.PARALLEL,),
        )(i_hbm, o_hbm)

    return gather(src, idx.reshape(1, n))


def _row_index_kernel(ps_ref, route_ref, o_ref, *, n_rows):
    bucket = route_ref[0:1, :].astype(I32)
    start = jnp.zeros(bucket.shape, I32)
    for b in range(N_BUCKETS):
        start = jnp.where(bucket == b, ps_ref[b], start)
    dest = start + route_ref[3:4, :].astype(I32)
    part = lax.broadcasted_iota(I32, o_ref.shape, 0)
    o_ref[...] = part * n_rows + dest


def _row_index(pad_starts, route, n_rows):
    t = route.shape[1]
    tm = min(TILE_ROW_INDEX, t)
    return pl.pallas_call(
        functools.partial(_row_index_kernel, n_rows=n_rows),
        grid=(t // tm,),
        in_specs=[pl.BlockSpec(memory_space=pltpu.SMEM), pl.BlockSpec((SUBLANES, tm), lambda i: (0, i))],
        out_specs=pl.BlockSpec((SUBLANES, tm), lambda i: (0, i)),
        out_shape=jax.ShapeDtypeStruct((SUBLANES, t), I32),
        compiler_params=_params("arbitrary"),
        name="row_index",
    )(pad_starts, route)


def _residual_kernel(x_ref, y_ref, g2_ref, o_ref):
    y = _unpack_bf16_pairs(jnp.concatenate([y_ref[c] for c in range(OUT_PARTS)], axis=1))
    o_ref[...] = x_ref[...] + g2_ref[...] * y


def _residual(x1, ytok, g2, seq, part):
    t, d = x1.shape
    tm = min(TILE_RESIDUAL, seq)
    tpb = seq // tm
    p, n_part = part
    steps = t // tm // n_part
    off = p * steps
    return pl.pallas_call(
        _residual_kernel,
        grid=(steps,),
        in_specs=[
            pl.BlockSpec((tm, d), lambda i: (i + off, 0)),
            pl.BlockSpec((OUT_PARTS, tm, LANES), lambda i: (0, i, 0)),
            pl.BlockSpec((None, 1, d), lambda i: ((i + off) // tpb, 0, 0)),
        ],
        out_specs=pl.BlockSpec((tm, d), lambda i: (i + off, 0)),
        out_shape=jax.ShapeDtypeStruct((t, d), F32),
        input_output_aliases={0: 0},
        compiler_params=_params("arbitrary"),
        name="residual",
    )(x1, ytok, g2)


def _expert_kernel(ea_ref, eb_ref, nr_ref, xs_ref, wga_ref, wua_ref, wda_ref, wgb_ref, wub_ref, wdb_ref, ys_ref,
                   ga_ref, ua_ref, gb_ref, ub_ref, dab_ref):
    j = pl.program_id(0)
    nr = nr_ref[0]
    prev = jnp.maximum(j - 1, 0)
    f = wda_ref.shape[0]

    @pl.when((j == 0) | (ea_ref[j] != ea_ref[prev]))
    def _():
        ga_ref[...] = wga_ref[...].astype(BF16)
        ua_ref[...] = wua_ref[...].astype(BF16)
        dab_ref[0:f, :] = wda_ref[...].astype(BF16)

    @pl.when((j == 0) | (eb_ref[j] != eb_ref[prev]))
    def _():
        gb_ref[...] = wgb_ref[...].astype(BF16)
        ub_ref[...] = wub_ref[...].astype(BF16)
        dab_ref[f:2 * f, :] = wdb_ref[...].astype(BF16)

    @pl.when(j < nr)
    def _():
        x = _unpack_bf16_pairs(jnp.concatenate([xs_ref[c] for c in range(PAY_PARTS - 1)], axis=1)).astype(BF16)
        gl = lax.bitcast_convert_type(xs_ref[PAY_PARTS - 1], F32)

        def gated_act(wg_ref, wu_ref, gate):
            gte = _dot(x, wg_ref[...])
            return (gte * _sigmoid(gte) * _dot(x, wu_ref[...]) * gate).astype(BF16)

        acts = jnp.concatenate([gated_act(ga_ref, ua_ref, gl[:, 0:1]), gated_act(gb_ref, ub_ref, gl[:, 1:2])], axis=1)
        y = _pack_bf16_pairs(_dot(acts, dab_ref[...]))
        for c in range(OUT_PARTS):
            ys_ref[c] = y[:, c * LANES:(c + 1) * LANES]

    @pl.when(j >= nr)
    def _():
        ys_ref[...] = jnp.zeros(ys_ref.shape, U32)


def _experts(blk_ea, blk_eb, n_real, xs, wg, wu, wd, d):
    n_rows = xs.shape[1]
    nblk = n_rows // EXPERT_BLOCK
    f = wg.shape[2]
    grid_spec = pltpu.PrefetchScalarGridSpec(
        num_scalar_prefetch=3,
        grid=(nblk,),
        in_specs=[
            pl.BlockSpec((PAY_PARTS, EXPERT_BLOCK, LANES), lambda j, ea, eb, nr: (0, j, 0)),
            pl.BlockSpec((None, d, f), lambda j, ea, eb, nr: (ea[j], 0, 0)),
            pl.BlockSpec((None, d, f), lambda j, ea, eb, nr: (ea[j], 0, 0)),
            pl.BlockSpec((None, f, d), lambda j, ea, eb, nr: (ea[j], 0, 0)),
            pl.BlockSpec((None, d, f), lambda j, ea, eb, nr: (eb[j], 0, 0)),
            pl.BlockSpec((None, d, f), lambda j, ea, eb, nr: (eb[j], 0, 0)),
            pl.BlockSpec((None, f, d), lambda j, ea, eb, nr: (eb[j], 0, 0)),
        ],
        out_specs=pl.BlockSpec((OUT_PARTS, EXPERT_BLOCK, LANES), lambda j, ea, eb, nr: (0, j, 0)),
        scratch_shapes=[pltpu.VMEM((d, f), BF16)] * 4 + [pltpu.VMEM((2 * f, d), BF16)],
    )
    return pl.pallas_call(
        _expert_kernel,
        grid_spec=grid_spec,
        out_shape=jax.ShapeDtypeStruct((OUT_PARTS, n_rows, LANES), U32),
        compiler_params=_params("arbitrary"),
        name="experts",
    )(blk_ea, blk_eb, n_real, xs, wg, wu, wd, wg, wu, wd)


_PAIR_A = (0, 0, 0, 1, 2, 2)
_PAIR_B = (1, 2, 3, 3, 3, 1)


def kernel(x, c, positions, ada_w, ada_b, norm_mix_w, norm_ffn_w, w_in, b_igate, b_fgate, q_norm_w, k_norm_w,
           sinks, conv_w, conv_b, mlstm_norm_w, w_attn_up, w_mlstm_up, w_out, router_w, router_bias,
           w_gate, w_up, w_down):
    batch, seq, d = x.shape
    depth = w_in.shape[0]
    t = batch * seq
    qw = N_HEADS * HEAD_DIM
    kvw = N_KV * HEAD_DIM
    mw = M_HEADS * M_DIM

    o = 0
    cols = {}
    for name, wdt in (("q", qw), ("k", kvw), ("v", kvw), ("mqk", 2 * mw), ("mv", mw), ("mi", M_HEADS),
                      ("mf", M_HEADS), ("mo", mw), ("ga", d), ("gb", d)):
        cols[name] = (o, o + wdt)
        o += wdt

    def in_weights(w_src, l):
        def wc(name):
            s, e = cols[name]
            return w_src[l:l + 1, :, s:e]
        return (jnp.concatenate([wc("q"), wc("k"), wc("v")], axis=2).astype(BF16),
                jnp.concatenate([wc("mqk"), wc("mv"), wc("mo")], axis=2).astype(BF16),
                jnp.concatenate([wc("mi"), wc("mf"), jnp.zeros((1, d, LANES - 2 * M_HEADS), F32)], axis=2).astype(BF16),
                jnp.concatenate([wc("ga"), wc("gb")], axis=2).astype(BF16))

    w_in_l = in_weights(w_in, 0)
    w_au = w_attn_up.astype(BF16)
    w_mu = w_mlstm_up.astype(BF16)
    w_o = w_out.astype(BF16)
    n_e = w_gate.shape[1]
    w_g8 = w_gate.reshape(depth * n_e, d, -1)
    w_u8 = w_up.reshape(depth * n_e, d, -1)
    w_d = w_down.reshape(depth * n_e, -1, d)

    rw_t = router_w.astype(F32).T
    rw_top = rw_t.astype(BF16)
    rw_hi = jnp.concatenate([rw_top, (rw_t - rw_top.astype(F32)).astype(BF16)], axis=0)
    rw_lo = rw_top
    rb = jnp.broadcast_to(router_bias.astype(F32)[:, None], (n_e, LANES))

    qn_w = jnp.tile(q_norm_w * (HEAD_DIM ** -0.5), (1, N_HEADS)).reshape(depth, 1, qw)
    kn_w = jnp.tile(k_norm_w, (1, N_KV)).reshape(depth, 1, kvw)
    seg = jnp.arange(qw) // HEAD_DIM
    bdq = jnp.where(seg[:, None] == seg[None, :], 1.0 / HEAD_DIM, 0.0).astype(BF16)
    bdk = bdq[:kvw, :kvw]

    inv_freq = ROPE_THETA ** (-(jnp.arange(0, ROPE_DIM, 2, dtype=F32) / ROPE_DIM))
    ang = positions.astype(F32).reshape(1, t) * inv_freq[:, None]
    cos8, sin8 = jnp.cos(ang).T, jnp.sin(ang).T
    pad1 = jnp.ones((t, HEAD_DIM - ROPE_DIM), F32)
    pad0 = jnp.zeros((t, HEAD_DIM - ROPE_DIM), F32)
    cos_t = jnp.tile(jnp.concatenate([cos8, cos8, pad1], axis=1), (1, LANES // HEAD_DIM))
    sin_t = jnp.tile(jnp.concatenate([-sin8, sin8, pad0], axis=1), (1, LANES // HEAD_DIM))

    gate_bias = jnp.concatenate([b_igate, b_fgate], axis=1).astype(F32)
    bcol = jnp.broadcast_to(gate_bias[:, :, None], (depth, 2 * M_HEADS, LANES))

    tm_merge = min(TILE_MERGE, seq)
    ii = jnp.arange(tm_merge)
    tri = (ii[:, None] <= ii[None, :]).astype(BF16)

    n_blk = (t + N_BUCKETS * (EXPERT_BLOCK - 1)) // EXPERT_BLOCK + 1
    n_rows = n_blk * EXPERT_BLOCK
    pair_a = jnp.asarray(_PAIR_A, I32)
    pair_b = jnp.asarray(_PAIR_B, I32)

    c_pad = jnp.zeros((SUBLANES, d), F32).at[:batch].set(c)
    mod = _ada_mod(c_pad, ada_w, ada_b)[:, :batch]

    xf = x.reshape(t, d)
    moe = None
    for l in range(depth):
        sh1, sc1, g1, sh2, sc2, g2 = [m.reshape(batch, 1, d) for m in jnp.split(mod[l], 6, axis=-1)]

        nmw = norm_mix_w.reshape(depth, 1, d)
        if moe is None:
            a_in, m_in, mg, grow = _inproj(xf, None, sc1, sh1, nmw[l:l + 1], *w_in_l, 0, seq)
        else:
            outs = None
            for p in range(GATHER_PARTS):
                outs = _inproj(xf, (moe[0][p], moe[1]), sc1, sh1, nmw[l:l + 1], *w_in_l, 0, seq,
                               part=(p, GATHER_PARTS), carry=None if outs is None else outs[:4])
                xf = outs[4]
            a_in, m_in, mg, grow = outs[:4]
        o_attn = _attention(a_in, cos_t, sin_t, sinks[l], qn_w[l], kn_w[l], bdq, bdk, batch, seq)
        hm = _mlstm(m_in, grow, conv_w[l], conv_b[l].reshape(1, -1), bcol[l], mlstm_norm_w[l].reshape(1, mw),
                    batch, seq)
        x1, pay, route, cnt = _merge(o_attn, hm, mg, xf, g1, sc2, sh2, norm_ffn_w.reshape(depth, 1, d),
                                     w_au, w_mu, w_o, rw_hi, rw_lo, rb, tri, l, seq)

        counts = cnt[:N_BUCKETS, 0].astype(I32)
        padded = (counts + EXPERT_BLOCK - 1) // EXPERT_BLOCK * EXPERT_BLOCK
        pad_ends = jnp.cumsum(padded)
        pad_starts = pad_ends - padded
        row_idx = _row_index(jnp.concatenate([pad_starts, jnp.zeros((BUCKET_ROWS - N_BUCKETS,), I32)]), route, n_rows)
        if l + 1 < depth:
            w_next, row_idx = lax.optimization_barrier((w_in[l + 1:l + 2], row_idx))
            w_in_l = in_weights(w_next, 0)
        blk_start = jnp.arange(n_blk, dtype=I32) * EXPERT_BLOCK
        blk_bucket = jnp.minimum(jnp.sum((pad_ends[None, :] <= blk_start[:, None]).astype(I32), axis=1), N_BUCKETS - 1)
        grp = blk_bucket // N_PAIRS
        blk_ea = (l * n_e + grp * EPG + pair_a[blk_bucket % N_PAIRS]).astype(I32)
        blk_eb = (l * n_e + grp * EPG + pair_b[blk_bucket % N_PAIRS]).astype(I32)
        n_real = (pad_ends[-1:] // EXPERT_BLOCK).astype(I32)

        xs = _sc_scatter_rows(pay.reshape(PAY_PARTS * t, LANES), row_idx[:PAY_PARTS].reshape(-1),
                              PAY_PARTS * n_rows).reshape(PAY_PARTS, n_rows, LANES)
        ys = _experts(blk_ea, blk_eb, n_real, xs, w_g8, w_u8, w_d, d)
        tp = t // GATHER_PARTS
        ys_rows = ys.reshape(OUT_PARTS * n_rows, LANES)
        ytok = [_sc_gather_rows(ys_rows, row_idx[:OUT_PARTS, p * tp:(p + 1) * tp].reshape(-1)).reshape(OUT_PARTS, tp, LANES)
                for p in range(GATHER_PARTS)]
        xf, moe = x1, (ytok, g2)
    for p in range(GATHER_PARTS):
        xf = _residual(xf, moe[0][p], moe[1], seq, (p, GATHER_PARTS))
    return xf.reshape(batch, seq, d)
```

```python
import functools

import jax
import jax.numpy as jnp
from jax import lax
from jax.experimental import pallas as pl
from jax.experimental.pallas import tpu as pltpu
from jax.experimental.pallas import tpu_sc as plsc

F32 = jnp.float32
BF16 = jnp.bfloat16
U32 = jnp.uint32
I32 = jnp.int32
HIGHEST = lax.Precision.HIGHEST

HEAD_DIM = 64
N_HEADS = 8
N_KV = 2
ROPE_DIM = 16
ROPE_THETA = 500000.0
ATTN_BLOCK = 128
M_HEADS = 4
M_DIM = 128
CONV_K = 4
N_EXPERTS = 16
N_GROUPS = 4
EPG = 4
EPS = 1e-6

LANES = 128
SUBLANES = 8

TILE_INPROJ = 1024
TILE_ATTN = 1024
TILE_MLSTM = 2048
TILE_MERGE = 1024
TILE_RESIDUAL = 1024
TILE_ROW_INDEX = 8192
ADA_COLS = 3072
CHUNK = 128
CHUNKS_PER_STEP = 4
GATE_CHUNKS_PER_STEP = 4
N_PAIRS = 6
N_BUCKETS = N_GROUPS * N_PAIRS
BUCKET_ROWS = 32
EXPERT_BLOCK = 512
PAY_PARTS = 5
OUT_PARTS = 4
SC_WINDOW = 128
GATHER_PARTS = 4
VMEM_LIMIT = 56 * 1024 * 1024


def _dot(a, b, precision=None):
    return jnp.dot(a, b, preferred_element_type=F32, precision=precision)


def _dot_nt(a, b):
    return lax.dot_general(a, b, (((1,), (1,)), ((), ())), preferred_element_type=F32)


def _sigmoid(x):
    return 1.0 / (1.0 + jnp.exp(-x))


def _log_sigmoid(x):
    return jnp.minimum(x, 0.0) - jnp.log1p(jnp.exp(-jnp.abs(x)))


def _pack_bf16_pairs(v, rounded=False):
    n = v.shape[1] // 2
    bits = lax.bitcast_convert_type(v if rounded else v.astype(BF16).astype(F32), U32)
    return bits[:, :n] | (bits[:, n:] >> 16)


def _unpack_bf16_pairs(w):
    hi = lax.bitcast_convert_type(w & jnp.uint32(0xFFFF0000), F32)
    lo = lax.bitcast_convert_type(w << 16, F32)
    return jnp.concatenate([hi, lo], axis=1)


def _params(*sem):
    return pltpu.CompilerParams(dimension_semantics=sem, vmem_limit_bytes=VMEM_LIMIT)


def _ada_kernel(c_ref, w_ref, b_ref, o_ref):
    c = c_ref[...]
    ca = c * _sigmoid(c)
    o_ref[0] = _dot(ca, w_ref[0], HIGHEST) + b_ref[0]


def _ada_mod(c_pad, ada_w, ada_b):
    depth, d, n = ada_w.shape
    tn = ADA_COLS
    return pl.pallas_call(
        _ada_kernel,
        grid=(depth, n // tn),
        in_specs=[
            pl.BlockSpec((SUBLANES, d), lambda l, j: (0, 0)),
            pl.BlockSpec((1, d, tn), lambda l, j: (l, 0, j)),
            pl.BlockSpec((1, 1, tn), lambda l, j: (l, 0, j)),
        ],
        out_specs=pl.BlockSpec((1, SUBLANES, tn), lambda l, j: (l, 0, j)),
        out_shape=jax.ShapeDtypeStruct((depth, SUBLANES, n), F32),
        compiler_params=_params("arbitrary", "arbitrary"),
        name="ada_mod",
    )(c_pad, ada_w, ada_b.reshape(depth, 1, n))


def _inproj_kernel(*refs, fuse_residual, n_carry=0):
    if n_carry:
        refs = refs[:10] + refs[10 + n_carry:]
    if fuse_residual:
        (x_ref, y_ref, g2_ref, sc_ref, sh_ref, nw_ref, wa_ref, wm_ref, wg_ref, wmg_ref,
         a_ref, m_ref, mg_ref, gr_ref, xo_ref, g_ref) = refs
        y = _unpack_bf16_pairs(jnp.concatenate([y_ref[c] for c in range(OUT_PARTS)], axis=1))
        x = x_ref[...] + g2_ref[...] * y
        xo_ref[...] = x
    else:
        (x_ref, sc_ref, sh_ref, nw_ref, wa_ref, wm_ref, wg_ref, wmg_ref,
         a_ref, m_ref, mg_ref, gr_ref, g_ref) = refs
        x = x_ref[...]
    ms = jnp.mean(x * x, axis=-1, keepdims=True)
    h = x * lax.rsqrt(ms + EPS) * (nw_ref[...] * (1.0 + sc_ref[...])) + sh_ref[...]
    hb = h.astype(BF16)
    a_ref[...] = _dot(hb, wa_ref[...]).astype(BF16)
    m_ref[...] = _dot(hb, wm_ref[...]).astype(BF16)
    mg_ref[...] = _sigmoid(_dot(hb, wmg_ref[...])).astype(BF16)
    g_ref[...] = _dot(hb, wg_ref[...])
    gr_ref[...] = g_ref[...].T[:SUBLANES, :]


def _inproj(x, moe, sc, sh, nw, wa, wm, wg, wmg, l, seq, part=None, carry=None):
    t, d = x.shape
    tm = min(TILE_INPROJ, seq)
    tpb = seq // tm
    p, n_part = part if part is not None else (0, 1)
    steps = t // tm // n_part
    off = p * steps
    row = lambda i: (i + off, 0)
    bsel = lambda i: ((i + off) // tpb, 0, 0)
    wsel = lambda i: (l, 0, 0)
    once = pl.Buffered(1)
    na, nm, ng, nmg = wa.shape[2], wm.shape[2], wg.shape[2], wmg.shape[2]
    fuse = moe is not None
    moe_specs = [pl.BlockSpec((OUT_PARTS, tm, LANES), lambda i: (0, i, 0)), pl.BlockSpec((None, 1, d), bsel)]
    carry = list(carry) if carry is not None else []
    aliases = {10 + k: k for k in range(len(carry))}
    if part is not None:
        aliases[0] = 4
    return pl.pallas_call(
        functools.partial(_inproj_kernel, fuse_residual=fuse, n_carry=len(carry)),
        grid=(steps,),
        in_specs=[pl.BlockSpec((tm, d), row)] + (moe_specs if fuse else []) + [
            pl.BlockSpec((None, 1, d), bsel),
            pl.BlockSpec((None, 1, d), bsel),
            pl.BlockSpec((None, 1, d), wsel),
            pl.BlockSpec((None, d, na), wsel, pipeline_mode=once),
            pl.BlockSpec((None, d, nm), wsel, pipeline_mode=once),
            pl.BlockSpec((None, d, ng), wsel, pipeline_mode=once),
            pl.BlockSpec((None, d, nmg), wsel, pipeline_mode=once),
        ] + [pl.BlockSpec(memory_space=pl.ANY)] * len(carry),
        out_specs=[
            pl.BlockSpec((tm, na), row),
            pl.BlockSpec((tm, nm), row),
            pl.BlockSpec((tm, nmg), row),
            pl.BlockSpec((SUBLANES, tm), lambda i: (0, i + off)),
        ] + ([pl.BlockSpec((tm, d), row)] if fuse else []),
        out_shape=[
            jax.ShapeDtypeStruct((t, na), BF16),
            jax.ShapeDtypeStruct((t, nm), BF16),
            jax.ShapeDtypeStruct((t, nmg), BF16),
            jax.ShapeDtypeStruct((SUBLANES, t), F32),
        ] + ([jax.ShapeDtypeStruct((t, d), F32)] if fuse else []),
        scratch_shapes=[pltpu.VMEM((tm, ng), F32)],
        input_output_aliases=aliases,
        compiler_params=_params("arbitrary"),
        name="inproj",
    )(x, *(moe if fuse else ()), sc, sh, nw, wa, wm, wg, wmg, *carry)


def _rope(t, cos, sin):
    w = t.shape[1]
    reps = w // LANES
    cosw = jnp.concatenate([cos] * reps, axis=1) if reps > 1 else cos
    sinw = jnp.concatenate([sin] * reps, axis=1) if reps > 1 else sin
    lane = lax.broadcasted_iota(I32, t.shape, 1)
    half = ROPE_DIM // 2
    up = pltpu.roll(t, w - half, axis=1)
    dn = pltpu.roll(t, half, axis=1)
    partner = jnp.where((lane % ROPE_DIM) < half, up, dn)
    return t * cosw + partner * sinw


def _head_norm(t, bd, w):
    ms = _dot((t * t).astype(BF16), bd)
    return t * lax.rsqrt(ms + EPS) * w


def _attn_kernel(sink_ref, cur_ref, prev_ref, cos_ref, sin_ref, cosp_ref, sinp_ref,
                 qw_ref, kw_ref, bdq_ref, bdk_ref, o_ref):
    tq = cur_ref.shape[0]
    nj = tq // ATTN_BLOCK
    qw = N_HEADS * HEAD_DIM
    kw = N_KV * HEAD_DIM
    blk0 = pl.program_id(1) * nj

    cur = cur_ref[...]
    q = cur[:, :qw].astype(F32)
    kc = cur[:, qw:qw + kw].astype(F32)
    vc = cur[:, qw + kw:].astype(F32)
    prev = prev_ref[...]
    kp = prev[:, :kw].astype(F32)
    vp = prev[:, kw:].astype(F32)

    cos, sin = cos_ref[...], sin_ref[...]
    q = _rope(_head_norm(q, bdq_ref[...], qw_ref[...]), cos, sin)
    kc = _rope(_head_norm(kc, bdk_ref[...], kw_ref[...]), cos, sin)
    kp = _rope(_head_norm(kp, bdk_ref[...], kw_ref[...]), cosp_ref[...], sinp_ref[...])
    qb = q.astype(BF16)

    def both_halves(x2):
        swapped = pltpu.roll(x2, HEAD_DIM, axis=1)
        first = lax.broadcasted_iota(I32, x2.shape, 1) < HEAD_DIM
        return jnp.concatenate([jnp.where(first, x2, swapped), jnp.where(first, swapped, x2)], axis=1).astype(BF16)

    k_all = both_halves(jnp.concatenate([kp, kc], axis=0))
    v_all = both_halves(jnp.concatenate([vp, vc], axis=0))

    lane = lax.broadcasted_iota(I32, (ATTN_BLOCK, LANES), 1)
    lo = lane < HEAD_DIM
    zero = jnp.zeros((ATTN_BLOCK, LANES), BF16)
    g_heads = N_HEADS // N_KV
    ri = lax.broadcasted_iota(I32, (g_heads * ATTN_BLOCK, ATTN_BLOCK), 0) % ATTN_BLOCK
    ci = lax.broadcasted_iota(I32, (g_heads * ATTN_BLOCK, ATTN_BLOCK), 1)
    from_prev = ci > ri
    head_row = lax.broadcasted_iota(I32, (g_heads * ATTN_BLOCK, 1), 0) // ATTN_BLOCK
    ones_v = jnp.ones((2 * ATTN_BLOCK, LANES), BF16)

    tiles = [(j, g) for j in range(nj) for g in range(N_KV)]
    scores = {}
    for j, g in tiles:
        rows = slice(j * ATTN_BLOCK, (j + 1) * ATTN_BLOCK)
        band = slice(j * ATTN_BLOCK, (j + 2) * ATTN_BLOCK)
        qp0 = qb[rows, (2 * g) * LANES:(2 * g + 1) * LANES]
        qp1 = qb[rows, (2 * g + 1) * LANES:(2 * g + 2) * LANES]
        q4 = jnp.concatenate([jnp.where(lo, qp0, zero), jnp.where(lo, zero, qp0),
                              jnp.where(lo, qp1, zero), jnp.where(lo, zero, qp1)], axis=0)
        scores[j, g] = _dot_nt(q4, k_all[band, g * LANES:(g + 1) * LANES])

    probs, sink_term = {}, {}
    for j, g in tiles:
        s2 = scores[j, g]
        if j == 0:
            prev_ok = ci > ri + (1 - jnp.minimum(blk0, 1)) * ATTN_BLOCK
            s = jnp.where(prev_ok, s2[:, :ATTN_BLOCK], jnp.where(from_prev, -jnp.inf, s2[:, ATTN_BLOCK:]))
        else:
            s = jnp.where(from_prev, s2[:, :ATTN_BLOCK], s2[:, ATTN_BLOCK:])
        sink = jnp.full((g_heads * ATTN_BLOCK, 1), sink_ref[g_heads * g], F32)
        for r in range(1, g_heads):
            sink = jnp.where(head_row == r, sink_ref[g_heads * g + r], sink)
        m = jnp.maximum(jnp.max(s, axis=-1, keepdims=True), sink)
        p = jnp.exp(s - m)
        probs[j, g] = jnp.concatenate([jnp.where(from_prev, p, 0.0), jnp.where(from_prev, 0.0, p)],
                                      axis=1).astype(BF16)
        sink_term[j, g] = jnp.exp(sink - m)

    for j, g in tiles:
        rows = slice(j * ATTN_BLOCK, (j + 1) * ATTN_BLOCK)
        band = slice(j * ATTN_BLOCK, (j + 2) * ATTN_BLOCK)
        o8 = _dot(probs[j, g], jnp.concatenate([v_all[band, g * LANES:(g + 1) * LANES], ones_v], axis=1))
        o4 = o8[:, :LANES] / (o8[:, LANES:] + sink_term[j, g])
        b = ATTN_BLOCK
        o_ref[rows, (2 * g) * LANES:(2 * g + 1) * LANES] = jnp.where(lo, o4[0:b], o4[b:2 * b]).astype(BF16)
        o_ref[rows, (2 * g + 1) * LANES:(2 * g + 2) * LANES] = jnp.where(
            lo, o4[2 * b:3 * b], o4[3 * b:4 * b]).astype(BF16)


def _attention(a_in, cos_t, sin_t, sinks_l, qw, kw, bdq, bdk, batch, seq):
    t = a_in.shape[0]
    tq = min(TILE_ATTN, seq)
    nj = tq // ATTN_BLOCK
    tpb = seq // tq
    bpb = seq // ATTN_BLOCK
    qwid = N_HEADS * HEAD_DIM
    kvw = 2 * N_KV * HEAD_DIM
    cur = lambda b, i: (b * tpb + i, 0)
    prv = lambda b, i: (b * bpb + jnp.maximum(i * nj - 1, 0), qwid // kvw)
    prv0 = lambda b, i: (b * bpb + jnp.maximum(i * nj - 1, 0), 0)
    const = lambda b, i: (0, 0)
    return pl.pallas_call(
        _attn_kernel,
        grid=(batch, tpb),
        in_specs=[
            pl.BlockSpec(memory_space=pltpu.SMEM),
            pl.BlockSpec((tq, qwid + kvw), cur),
            pl.BlockSpec((ATTN_BLOCK, kvw), prv),
            pl.BlockSpec((tq, LANES), cur),
            pl.BlockSpec((tq, LANES), cur),
            pl.BlockSpec((ATTN_BLOCK, LANES), prv0),
            pl.BlockSpec((ATTN_BLOCK, LANES), prv0),
            pl.BlockSpec((1, qwid), const),
            pl.BlockSpec((1, kvw // 2), const),
            pl.BlockSpec((qwid, qwid), const),
            pl.BlockSpec((kvw // 2, kvw // 2), const),
        ],
        out_specs=pl.BlockSpec((tq, qwid), cur),
        out_shape=jax.ShapeDtypeStruct((t, qwid), BF16),
        compiler_params=_params("arbitrary", "arbitrary"),
        name="swa_attention",
    )(sinks_l, a_in, a_in, cos_t, sin_t, cos_t, sin_t, qw, kw, bdq, bdk)


def _mlstm_kernel(min_ref, gr_ref, cw_ref, cb_ref, bcol_ref, nw_ref,
                  hm_ref, ext_ref, q_ref, kt_ref, st_ref, mx_ref, ab_ref, bc_ref):
    tt = min_ref.shape[0]
    mw = M_HEADS * M_DIM
    nchunks = tt // CHUNK

    @pl.when(pl.program_id(1) == 0)
    def _():
        ext_ref[0:SUBLANES, :] = jnp.zeros((SUBLANES, 2 * mw), F32)
        st_ref[...] = jnp.zeros(st_ref.shape, F32)
        mx_ref[...] = jnp.zeros(mx_ref.shape, F32)

    def conv_block(cols):
        u = min_ref[:, cols].astype(F32)
        ext_ref[SUBLANES:SUBLANES + tt, cols] = u
        acc = cb_ref[:, cols] + cw_ref[CONV_K - 1:CONV_K, cols] * u
        for jj in range(CONV_K - 1):
            off = SUBLANES - (CONV_K - 1) + jj
            acc = acc + cw_ref[jj:jj + 1, cols] * ext_ref[off:off + tt, cols]
        ext_ref[0:SUBLANES, cols] = u[tt - SUBLANES:tt, :]
        return acc * _sigmoid(acc)

    def q_body(h, carry):
        cols = pl.ds(pl.multiple_of(h * M_DIM, M_DIM), M_DIM)
        q_ref[:, cols] = conv_block(cols).astype(BF16)
        return carry

    def k_body(h, carry):
        off = pl.multiple_of(h * M_DIM, M_DIM)
        act = conv_block(pl.ds(mw + off, M_DIM)) * (M_DIM ** -0.5)
        for j in range(nchunks):
            kt_ref[pl.ds(off, M_DIM), j * CHUNK:(j + 1) * CHUNK] = act[j * CHUNK:(j + 1) * CHUNK, :].T
        return carry

    lax.fori_loop(0, M_HEADS, q_body, 0)
    lax.fori_loop(0, M_HEADS, k_body, 0)

    ri = lax.broadcasted_iota(I32, (CHUNK, CHUNK), 0)
    ci = lax.broadcasted_iota(I32, (CHUNK, CHUNK), 1)
    causal = ci <= ri
    triu = jnp.where(ri <= ci, 1.0, 0.0).astype(BF16)
    ones_half = jnp.ones((CHUNK, M_DIM), BF16)
    mean_mat = jnp.full((M_DIM, M_DIM), 1.0 / M_DIM, BF16)
    sub = lax.broadcasted_iota(I32, (SUBLANES, CHUNK), 0)
    heads = range(M_HEADS)

    pad_rows = jnp.zeros((CHUNK - SUBLANES, CHUNK), F32)
    zero_rows = jnp.zeros((SUBLANES, CHUNK), F32)

    def gate_body(jg, carry):
        for u_ in range(GATE_CHUNKS_PER_STEP):
            rs = pl.ds(pl.multiple_of((jg * GATE_CHUNKS_PER_STEP + u_) * CHUNK, CHUNK), CHUNK)
            gr = gr_ref[:, rs] + bcol_ref[...]
            ls = _log_sigmoid(gr)
            ls1 = ls.astype(BF16).astype(F32)
            ls2 = (ls - ls1).astype(BF16).astype(F32)
            pieces = jnp.concatenate([ls1, ls2, ls - ls1 - ls2, zero_rows], axis=0).astype(BF16)
            sums = _dot(pieces, triu)
            br = sums[0:SUBLANES] + sums[SUBLANES:2 * SUBLANES] + sums[2 * SUBLANES:3 * SUBLANES]
            ab = jnp.where(sub < M_HEADS, gr - pltpu.roll(br, M_HEADS, axis=0), br)
            ab_ref[:, rs] = ab
            bc_ref[rs, :] = jnp.concatenate([ab, pad_rows], axis=0).T
        return carry

    lax.fori_loop(0, nchunks // GATE_CHUNKS_PER_STEP, gate_body, 0)

    def group_body(cg, carry):
        rows, ab = [], []
        for u_ in range(CHUNKS_PER_STEP):
            r0 = pl.multiple_of((cg * CHUNKS_PER_STEP + u_) * CHUNK, CHUNK)
            rows.append(pl.ds(r0, CHUNK))
            ab.append(ab_ref[:, rows[u_]])
        lanes = [(u_, h) for u_ in range(CHUNKS_PER_STEP) for h in heads]
        a_r = {(u_, h): ab[u_][h:h + 1, :] for u_, h in lanes}
        b_last = {(u_, h): ab[u_][M_HEADS + h:M_HEADS + h + 1, CHUNK - 1:CHUNK] for u_, h in lanes}

        m_prev, a_max, a_dec, s_in = {}, {}, {}, {}
        m_run = [mx_ref[h][0:1, 0:1] for h in heads]
        for k in lanes:
            u_, h = k
            m_prev[k] = m_run[h]
            a_max[k] = jnp.max(a_r[k], axis=-1, keepdims=True)
            m_loc = b_last[k] + a_max[k]
            m_new = jnp.maximum(b_last[k] + m_prev[k], m_loc)
            a_dec[k] = jnp.exp(b_last[k] + m_prev[k] - m_new)
            s_in[k] = jnp.exp(m_loc - m_new)
            m_run[h] = m_new
        for h in heads:
            mx_ref[h] = jnp.broadcast_to(m_run[h], (SUBLANES, LANES))

        q, v_ext, s_qk, kv = {}, {}, {}, {}
        for k in lanes:
            u_, h = k
            rs = rows[u_]
            q[k] = q_ref[rs, h * M_DIM:(h + 1) * M_DIM]
            kt = kt_ref[h * M_DIM:(h + 1) * M_DIM, rs]
            v = min_ref[rs, 2 * mw + h * M_DIM:2 * mw + (h + 1) * M_DIM]
            v_ext[k] = jnp.concatenate([v, ones_half], axis=1)
            s_qk[k] = _dot(q[k], kt.astype(BF16))
            e_r = jnp.exp(a_r[k] - a_max[k])
            kv[k] = _dot((kt * e_r).astype(BF16), v_ext[k])

        thr, qk = {}, {}
        for k in lanes:
            u_, h = k
            a_mat = jnp.where(causal, a_r[k], -jnp.inf)
            mu = jnp.maximum(jnp.max(a_mat, axis=-1, keepdims=True), m_prev[k])
            b_c = bc_ref[rows[u_], M_HEADS + h:M_HEADS + h + 1]
            thr[k] = jnp.broadcast_to(jnp.exp(-(b_c + mu)), (CHUNK, M_DIM))
            mu_b = jnp.broadcast_to(mu, (CHUNK, CHUNK))
            inter = jnp.exp(m_prev[k] - mu_b)
            qk[k] = jnp.concatenate([(s_qk[k] * jnp.exp(a_mat - mu_b)).astype(BF16),
                                     (q[k].astype(F32) * inter).astype(BF16)], axis=1)

        state = [st_ref[h] for h in heads]
        for k in lanes:
            u_, h = k
            hs = slice(h * M_DIM, (h + 1) * M_DIM)
            num = _dot(qk[k], jnp.concatenate([v_ext[k], state[h].astype(BF16)], axis=0))
            state[h] = a_dec[k] * state[h] + s_in[k] * kv[k]
            den = jnp.maximum(jnp.abs(num[:, M_DIM:]), thr[k])
            hh = num[:, :M_DIM] / den
            msq = _dot((hh * hh).astype(BF16), mean_mat)
            hn = hh * lax.rsqrt(msq + EPS) * nw_ref[:, hs]
            og = min_ref[rows[u_], 3 * mw + h * M_DIM:3 * mw + (h + 1) * M_DIM].astype(F32)
            hm_ref[rows[u_], hs] = (_sigmoid(og) * hn).astype(BF16)
        for h in heads:
            st_ref[h] = state[h]
        return carry

    lax.fori_loop(0, nchunks // CHUNKS_PER_STEP, group_body, 0)


def _mlstm(m_in, grow, conv_w, conv_b, bcol, nw, batch, seq):
    t = m_in.shape[0]
    tt = min(TILE_MLSTM, seq)
    tpb = seq // tt
    mw = M_HEADS * M_DIM
    cur = lambda b, i: (b * tpb + i, 0)
    const = lambda b, i: (0, 0)
    return pl.pallas_call(
        _mlstm_kernel,
        grid=(batch, tpb),
        in_specs=[
            pl.BlockSpec((tt, 4 * mw), cur),
            pl.BlockSpec((SUBLANES, tt), lambda b, i: (0, b * tpb + i)),
            pl.BlockSpec((CONV_K, 2 * mw), const),
            pl.BlockSpec((1, 2 * mw), const),
            pl.BlockSpec((SUBLANES, LANES), const),
            pl.BlockSpec((1, mw), const),
        ],
        out_specs=pl.BlockSpec((tt, mw), cur),
        out_shape=jax.ShapeDtypeStruct((t, mw), BF16),
        scratch_shapes=[
            pltpu.VMEM((tt + SUBLANES, 2 * mw), F32),
            pltpu.VMEM((tt, mw), BF16),
            pltpu.VMEM((mw, tt), F32),
            pltpu.VMEM((M_HEADS, M_DIM, 2 * M_DIM), F32),
            pltpu.VMEM((M_HEADS, SUBLANES, LANES), F32),
            pltpu.VMEM((SUBLANES, tt), F32),
            pltpu.VMEM((tt, LANES), F32),
        ],
        compiler_params=_params("arbitrary", "arbitrary"),
        name="mlstm",
    )(m_in, grow, conv_w, conv_b, bcol, nw)


def _merge_kernel(o_ref, hm_ref, mg_ref, x_ref, g1_ref, sc_ref, sh_ref, nw_ref,
                  wa_ref, wm_ref, wo_ref, rwh_ref, rwl_ref, rb_ref, tri_ref,
                  x1_ref, pay_ref, route_ref, cnt_ref, carry_ref):
    tm, d = x_ref.shape

    @pl.when(pl.program_id(0) == 0)
    def _():
        carry_ref[...] = jnp.zeros(carry_ref.shape, F32)

    ya = _dot(o_ref[...], wa_ref[...])
    yb = _dot(hm_ref[...], wm_ref[...])
    mg = mg_ref[...]
    merged = mg[:, :d].astype(F32) * ya + mg[:, d:].astype(F32) * yb
    x1 = x_ref[...] + g1_ref[...] * _dot(merged.astype(BF16), wo_ref[...])
    x1_ref[...] = x1

    ms = jnp.mean(x1 * x1, axis=-1, keepdims=True)
    h2 = x1 * lax.rsqrt(ms + EPS) * (nw_ref[...] * (1.0 + sc_ref[...])) + sh_ref[...]
    hi = h2.astype(BF16)
    hif = hi.astype(F32)
    lo = (h2 - hif).astype(BF16)
    r_hi = _dot_nt(rwh_ref[...], hi)
    r_lo = _dot_nt(rwl_ref[...], lo)
    sc_t = _sigmoid(r_hi[:N_EXPERTS] + r_hi[N_EXPERTS:] + r_lo)
    sel_t = sc_t + rb_ref[:, 0:1]

    def row(a, e):
        return a[e:e + 1, :]

    best = None
    gi = jnp.zeros((1, tm), I32)
    for g in range(N_GROUPS):
        r = [row(sel_t, EPG * g + i) for i in range(EPG)]
        gs = None
        for i in range(EPG):
            for j in range(i + 1, EPG):
                pr = r[i] + r[j]
                gs = pr if gs is None else jnp.maximum(gs, pr)
        if best is None:
            best = gs
        else:
            upd = gs > best
            gi = jnp.where(upd, g, gi)
            best = jnp.maximum(best, gs)

    def pick(a, i):
        out = row(a, i)
        for g in range(1, N_GROUPS):
            out = jnp.where(gi == g, row(a, EPG * g + i), out)
        return out

    v = [pick(sel_t, i) for i in range(EPG)]
    s = [pick(sc_t, i) for i in range(EPG)]

    def argmax4(vals):
        bv, bi = vals[0], jnp.zeros((1, tm), I32)
        for i in range(1, EPG):
            upd = vals[i] > bv
            bi = jnp.where(upd, i, bi)
            bv = jnp.maximum(bv, vals[i])
        return bi

    i1 = argmax4(v)
    i2 = argmax4([jnp.where(i1 == i, -jnp.inf, v[i]) for i in range(EPG)])
    ia = jnp.minimum(i1, i2)
    ib = jnp.maximum(i1, i2)
    pidx = jnp.where(ia == 0, ib - 1, jnp.where(ia == 1, jnp.where(ib == 3, 3, 5), 4))
    bucket = gi * N_PAIRS + pidx

    def by_index(vals, idx):
        out = vals[0]
        for i in range(1, EPG):
            out = jnp.where(idx == i, vals[i], out)
        return out

    swap = pidx == N_PAIRS - 1
    s_lo, s_hi = by_index(s, ia), by_index(s, ib)
    s_a, s_b = jnp.where(swap, s_hi, s_lo), jnp.where(swap, s_lo, s_hi)
    gate_a = s_a / (s_a + s_b)
    gate_b = s_b / (s_a + s_b)

    brow = lax.broadcasted_iota(I32, (BUCKET_ROWS, tm), 0)
    onehot = brow == bucket
    cums = _dot(jnp.where(onehot, 1.0, 0.0).astype(BF16), tri_ref[...])
    carry = carry_ref[...]
    rank = jnp.sum(jnp.where(onehot, carry[:, 0:1] + cums, 0.0), axis=0, keepdims=True) - 1.0
    new_carry = carry + cums[:, tm - 1:tm]
    carry_ref[...] = new_carry
    cnt_ref[...] = new_carry

    route_ref[...] = jnp.concatenate(
        [bucket.astype(F32), gate_a, gate_b, rank, jnp.zeros((SUBLANES - 4, tm), F32)], axis=0)

    half = d // 2
    packed = _pack_bf16_pairs(hif, rounded=True)
    for cpart in range(half // LANES):
        pay_ref[cpart] = packed[:, cpart * LANES:(cpart + 1) * LANES]
    gates_t = jnp.concatenate([gate_a, gate_b, jnp.zeros((LANES - 2, tm), F32)], axis=0)
    pay_ref[half // LANES] = lax.bitcast_convert_type(gates_t.T, U32)


def _merge(o_attn, hm, mg, x, g1, sc2, sh2, nw, wa, wm, wo, rwh, rwl, rb, tri, l, seq):
    t, d = x.shape
    tm = tri.shape[0]
    tpb = seq // tm
    row = lambda i: (i, 0)
    bsel = lambda i: (i // tpb, 0, 0)
    wsel = lambda i: (l, 0, 0)
    const = lambda i: (0, 0)
    hw = o_attn.shape[1]
    return pl.pallas_call(
        _merge_kernel,
        grid=(t // tm,),
        in_specs=[
            pl.BlockSpec((tm, hw), row),
            pl.BlockSpec((tm, hw), row),
            pl.BlockSpec((tm, 2 * d), row),
            pl.BlockSpec((tm, d), row),
            pl.BlockSpec((None, 1, d), bsel),
            pl.BlockSpec((None, 1, d), bsel),
            pl.BlockSpec((None, 1, d), bsel),
            pl.BlockSpec((None, 1, d), wsel),
            pl.BlockSpec((None, hw, d), wsel),
            pl.BlockSpec((None, hw, d), wsel),
            pl.BlockSpec((None, d, d), wsel),
            pl.BlockSpec((2 * N_EXPERTS, d), const),
            pl.BlockSpec((N_EXPERTS, d), const),
            pl.BlockSpec((N_EXPERTS, LANES), const),
            pl.BlockSpec((tm, tm), const),
        ],
        out_specs=[
            pl.BlockSpec((tm, d), row),
            pl.BlockSpec((PAY_PARTS, tm, LANES), lambda i: (0, i, 0)),
            pl.BlockSpec((SUBLANES, tm), lambda i: (0, i)),
            pl.BlockSpec((BUCKET_ROWS, LANES), const),
        ],
        out_shape=[
            jax.ShapeDtypeStruct((t, d), F32),
            jax.ShapeDtypeStruct((PAY_PARTS, t, LANES), U32),
            jax.ShapeDtypeStruct((SUBLANES, t), F32),
            jax.ShapeDtypeStruct((BUCKET_ROWS, LANES), F32),
        ],
        scratch_shapes=[pltpu.VMEM((BUCKET_ROWS, LANES), F32)],
        compiler_params=_params("arbitrary"),
        name="merge_router",
    )(o_attn, hm, mg, x, g1, sc2, sh2, nw, wa, wm, wo, rwh, rwl, rb, tri)


def _sc_mesh():
    return plsc.VectorSubcoreMesh(core_axis_name="core", subcore_axis_name="subcore")


def _sc_scatter_rows(rows, dest, n_out):
    n, w = rows.shape

    @pl.kernel(out_type=jax.ShapeDtypeStruct((n_out, w), rows.dtype), mesh=_sc_mesh(), scratch_types=[])
    def scatter(x_hbm, i_hbm, o_hbm):
        def body(x_vmem, i_vmem):
            pltpu.sync_copy(x_vmem, o_hbm.at[i_vmem.at[0]])

        pltpu.emit_pipeline(
            body,
            grid=(n // SC_WINDOW,),
            in_specs=[pl.BlockSpec((SC_WINDOW, w), lambda i: (i, 0)),
                      pl.BlockSpec((1, SC_WINDOW), lambda i: (0, i))],
            out_specs=[],
            core_axis_name=("core", "subcore"),
            dimension_semantics=(pltpu.PARALLEL,),
        )(x_hbm, i_hbm)

    return scatter(rows, dest.reshape(1, n))


def _sc_gather_rows(src, idx):
    n = idx.shape[0]
    w = src.shape[1]

    @pl.kernel(out_type=jax.ShapeDtypeStruct((n, w), src.dtype), mesh=_sc_mesh(), scratch_types=[])
    def gather(x_hbm, i_hbm, o_hbm):
        def body(i_vmem, o_vmem):
            pltpu.sync_copy(x_hbm.at[i_vmem.at[0]], o_vmem)

        pltpu.emit_pipeline(
            body,
            grid=(n // SC_WINDOW,),
            in_specs=[pl.BlockSpec((1, SC_WINDOW), lambda i: (0, i))],
            out_specs=[pl.BlockSpec((SC_WINDOW, w), lambda i: (i, 0))],
            core_axis_name=("core", "subcore"),
            dimension_semantics=(pltpu.PARALLEL,),
        )(i_hbm, o_hbm)

    return gather(src, idx.reshape(1, n))


def _row_index_kernel(ps_ref, route_ref, o_ref, *, n_rows):
    bucket = route_ref[0:1, :].astype(I32)
    start = jnp.zeros(bucket.shape, I32)
    for b in range(N_BUCKETS):
        start = jnp.where(bucket == b, ps_ref[b], start)
    dest = start + route_ref[3:4, :].astype(I32)
    part = lax.broadcasted_iota(I32, o_ref.shape, 0)
    o_ref[...] = part * n_rows + dest


def _row_index(pad_starts, route, n_rows):
    t = route.shape[1]
    tm = min(TILE_ROW_INDEX, t)
    return pl.pallas_call(
        functools.partial(_row_index_kernel, n_rows=n_rows),
        grid=(t // tm,),
        in_specs=[pl.BlockSpec(memory_space=pltpu.SMEM), pl.BlockSpec((SUBLANES, tm), lambda i: (0, i))],
        out_specs=pl.BlockSpec((SUBLANES, tm), lambda i: (0, i)),
        out_shape=jax.ShapeDtypeStruct((SUBLANES, t), I32),
        compiler_params=_params("arbitrary"),
        name="row_index",
    )(pad_starts, route)


def _residual_kernel(x_ref, y_ref, g2_ref, o_ref):
    y = _unpack_bf16_pairs(jnp.concatenate([y_ref[c] for c in range(OUT_PARTS)], axis=1))
    o_ref[...] = x_ref[...] + g2_ref[...] * y


def _residual(x1, ytok, g2, seq, part):
    t, d = x1.shape
    tm = min(TILE_RESIDUAL, seq)
    tpb = seq // tm
    p, n_part = part
    steps = t // tm // n_part
    off = p * steps
    return pl.pallas_call(
        _residual_kernel,
        grid=(steps,),
        in_specs=[
            pl.BlockSpec((tm, d), lambda i: (i + off, 0)),
            pl.BlockSpec((OUT_PARTS, tm, LANES), lambda i: (0, i, 0)),
            pl.BlockSpec((None, 1, d), lambda i: ((i + off) // tpb, 0, 0)),
        ],
        out_specs=pl.BlockSpec((tm, d), lambda i: (i + off, 0)),
        out_shape=jax.ShapeDtypeStruct((t, d), F32),
        input_output_aliases={0: 0},
        compiler_params=_params("arbitrary"),
        name="residual",
    )(x1, ytok, g2)


def _expert_kernel(ea_ref, eb_ref, nr_ref, xs_ref, wga_ref, wua_ref, wda_ref, wgb_ref, wub_ref, wdb_ref, ys_ref,
                   ga_ref, ua_ref, gb_ref, ub_ref, dab_ref):
    j = pl.program_id(0)
    nr = nr_ref[0]
    prev = jnp.maximum(j - 1, 0)
    f = wda_ref.shape[0]

    @pl.when((j == 0) | (ea_ref[j] != ea_ref[prev]))
    def _():
        ga_ref[...] = wga_ref[...].astype(BF16)
        ua_ref[...] = wua_ref[...].astype(BF16)
        dab_ref[0:f, :] = wda_ref[...].astype(BF16)

    @pl.when((j == 0) | (eb_ref[j] != eb_ref[prev]))
    def _():
        gb_ref[...] = wgb_ref[...].astype(BF16)
        ub_ref[...] = wub_ref[...].astype(BF16)
        dab_ref[f:2 * f, :] = wdb_ref[...].astype(BF16)

    @pl.when(j < nr)
    def _():
        x = _unpack_bf16_pairs(jnp.concatenate([xs_ref[c] for c in range(PAY_PARTS - 1)], axis=1)).astype(BF16)
        gl = lax.bitcast_convert_type(xs_ref[PAY_PARTS - 1], F32)

        def gated_act(wg_ref, wu_ref, gate):
            gte = _dot(x, wg_ref[...])
            return (gte * _sigmoid(gte) * _dot(x, wu_ref[...]) * gate).astype(BF16)

        acts = jnp.concatenate([gated_act(ga_ref, ua_ref, gl[:, 0:1]), gated_act(gb_ref, ub_ref, gl[:, 1:2])], axis=1)
        y = _pack_bf16_pairs(_dot(acts, dab_ref[...]))
        for c in range(OUT_PARTS):
            ys_ref[c] = y[:, c * LANES:(c + 1) * LANES]

    @pl.when(j >= nr)
    def _():
        ys_ref[...] = jnp.zeros(ys_ref.shape, U32)


def _experts(blk_ea, blk_eb, n_real, xs, wg, wu, wd, d):
    n_rows = xs.shape[1]
    nblk = n_rows // EXPERT_BLOCK
    f = wg.shape[2]
    grid_spec = pltpu.PrefetchScalarGridSpec(
        num_scalar_prefetch=3,
        grid=(nblk,),
        in_specs=[
            pl.BlockSpec((PAY_PARTS, EXPERT_BLOCK, LANES), lambda j, ea, eb, nr: (0, j, 0)),
            pl.BlockSpec((None, d, f), lambda j, ea, eb, nr: (ea[j], 0, 0)),
            pl.BlockSpec((None, d, f), lambda j, ea, eb, nr: (ea[j], 0, 0)),
            pl.BlockSpec((None, f, d), lambda j, ea, eb, nr: (ea[j], 0, 0)),
            pl.BlockSpec((None, d, f), lambda j, ea, eb, nr: (eb[j], 0, 0)),
            pl.BlockSpec((None, d, f), lambda j, ea, eb, nr: (eb[j], 0, 0)),
            pl.BlockSpec((None, f, d), lambda j, ea, eb, nr: (eb[j], 0, 0)),
        ],
        out_specs=pl.BlockSpec((OUT_PARTS, EXPERT_BLOCK, LANES), lambda j, ea, eb, nr: (0, j, 0)),
        scratch_shapes=[pltpu.VMEM((d, f), BF16)] * 4 + [pltpu.VMEM((2 * f, d), BF16)],
    )
    return pl.pallas_call(
        _expert_kernel,
        grid_spec=grid_spec,
        out_shape=jax.ShapeDtypeStruct((OUT_PARTS, n_rows, LANES), U32),
        compiler_params=_params("arbitrary"),
        name="experts",
    )(blk_ea, blk_eb, n_real, xs, wg, wu, wd, wg, wu, wd)


_PAIR_A = (0, 0, 0, 1, 2, 2)
_PAIR_B = (1, 2, 3, 3, 3, 1)


def kernel(x, c, positions, ada_w, ada_b, norm_mix_w, norm_ffn_w, w_in, b_igate, b_fgate, q_norm_w, k_norm_w,
           sinks, conv_w, conv_b, mlstm_norm_w, w_attn_up, w_mlstm_up, w_out, router_w, router_bias,
           w_gate, w_up, w_down):
    batch, seq, d = x.shape
    depth = w_in.shape[0]
    t = batch * seq
    qw = N_HEADS * HEAD_DIM
    kvw = N_KV * HEAD_DIM
    mw = M_HEADS * M_DIM

    o = 0
    cols = {}
    for name, wdt in (("q", qw), ("k", kvw), ("v", kvw), ("mqk", 2 * mw), ("mv", mw), ("mi", M_HEADS),
                      ("mf", M_HEADS), ("mo", mw), ("ga", d), ("gb", d)):
        cols[name] = (o, o + wdt)
        o += wdt

    def wc(name, lo=0, hi=None):
        s, e = cols[name]
        return w_in[:, :, s + lo:(s + hi if hi is not None else e)]

    w_a = jnp.concatenate([wc("q"), wc("k"), wc("v")], axis=2).astype(BF16)
    w_m = jnp.concatenate([wc("mqk"), wc("mv"), wc("mo")], axis=2).astype(BF16)
    w_g = jnp.concatenate([wc("mi"), wc("mf"), jnp.zeros((depth, d, LANES - 2 * M_HEADS), F32)], axis=2).astype(BF16)
    w_mg = jnp.concatenate([wc("ga"), wc("gb")], axis=2).astype(BF16)
    w_au = w_attn_up.astype(BF16)
    w_mu = w_mlstm_up.astype(BF16)
    w_o = w_out.astype(BF16)
    n_e = w_gate.shape[1]
    w_g8 = w_gate.reshape(depth * n_e, d, -1)
    w_u8 = w_up.reshape(depth * n_e, d, -1)
    w_d = w_down.reshape(depth * n_e, -1, d)

    rw_t = router_w.astype(F32).T
    rw_top = rw_t.astype(BF16)
    rw_hi = jnp.concatenate([rw_top, (rw_t - rw_top.astype(F32)).astype(BF16)], axis=0)
    rw_lo = rw_top
    rb = jnp.broadcast_to(router_bias.astype(F32)[:, None], (n_e, LANES))

    qn_w = jnp.tile(q_norm_w * (HEAD_DIM ** -0.5), (1, N_HEADS)).reshape(depth, 1, qw)
    kn_w = jnp.tile(k_norm_w, (1, N_KV)).reshape(depth, 1, kvw)
    seg = jnp.arange(qw) // HEAD_DIM
    bdq = jnp.where(seg[:, None] == seg[None, :], 1.0 / HEAD_DIM, 0.0).astype(BF16)
    bdk = bdq[:kvw, :kvw]

    inv_freq = ROPE_THETA ** (-(jnp.arange(0, ROPE_DIM, 2, dtype=F32) / ROPE_DIM))
    ang = positions.astype(F32).reshape(1, t) * inv_freq[:, None]
    cos8, sin8 = jnp.cos(ang).T, jnp.sin(ang).T
    pad1 = jnp.ones((t, HEAD_DIM - ROPE_DIM), F32)
    pad0 = jnp.zeros((t, HEAD_DIM - ROPE_DIM), F32)
    cos_t = jnp.tile(jnp.concatenate([cos8, cos8, pad1], axis=1), (1, LANES // HEAD_DIM))
    sin_t = jnp.tile(jnp.concatenate([-sin8, sin8, pad0], axis=1), (1, LANES // HEAD_DIM))

    gate_bias = jnp.concatenate([b_igate, b_fgate], axis=1).astype(F32)
    bcol = jnp.broadcast_to(gate_bias[:, :, None], (depth, 2 * M_HEADS, LANES))

    tm_merge = min(TILE_MERGE, seq)
    ii = jnp.arange(tm_merge)
    tri = (ii[:, None] <= ii[None, :]).astype(BF16)

    n_blk = (t + N_BUCKETS * (EXPERT_BLOCK - 1)) // EXPERT_BLOCK + 1
    n_rows = n_blk * EXPERT_BLOCK
    pair_a = jnp.asarray(_PAIR_A, I32)
    pair_b = jnp.asarray(_PAIR_B, I32)

    c_pad = jnp.zeros((SUBLANES, d), F32).at[:batch].set(c)
    mod = _ada_mod(c_pad, ada_w, ada_b)[:, :batch]

    xf = x.reshape(t, d)
    moe = None
    for l in range(depth):
        sh1, sc1, g1, sh2, sc2, g2 = [m.reshape(batch, 1, d) for m in jnp.split(mod[l], 6, axis=-1)]

        nmw = norm_mix_w.reshape(depth, 1, d)
        if moe is None:
            a_in, m_in, mg, grow = _inproj(xf, None, sc1, sh1, nmw, w_a, w_m, w_g, w_mg, l, seq)
        else:
            outs = None
            for p in range(GATHER_PARTS):
                outs = _inproj(xf, (moe[0][p], moe[1]), sc1, sh1, nmw, w_a, w_m, w_g, w_mg, l, seq,
                               part=(p, GATHER_PARTS), carry=None if outs is None else outs[:4])
                xf = outs[4]
            a_in, m_in, mg, grow = outs[:4]
        o_attn = _attention(a_in, cos_t, sin_t, sinks[l], qn_w[l], kn_w[l], bdq, bdk, batch, seq)
        hm = _mlstm(m_in, grow, conv_w[l], conv_b[l].reshape(1, -1), bcol[l], mlstm_norm_w[l].reshape(1, mw),
                    batch, seq)
        x1, pay, route, cnt = _merge(o_attn, hm, mg, xf, g1, sc2, sh2, norm_ffn_w.reshape(depth, 1, d),
                                     w_au, w_mu, w_o, rw_hi, rw_lo, rb, tri, l, seq)

        counts = cnt[:N_BUCKETS, 0].astype(I32)
        padded = (counts + EXPERT_BLOCK - 1) // EXPERT_BLOCK * EXPERT_BLOCK
        pad_ends = jnp.cumsum(padded)
        pad_starts = pad_ends - padded
        row_idx = _row_index(jnp.concatenate([pad_starts, jnp.zeros((BUCKET_ROWS - N_BUCKETS,), I32)]), route, n_rows)
        blk_start = jnp.arange(n_blk, dtype=I32) * EXPERT_BLOCK
        blk_bucket = jnp.minimum(jnp.sum((pad_ends[None, :] <= blk_start[:, None]).astype(I32), axis=1), N_BUCKETS - 1)
        grp = blk_bucket // N_PAIRS
        blk_ea = (l * n_e + grp * EPG + pair_a[blk_bucket % N_PAIRS]).astype(I32)
        blk_eb = (l * n_e + grp * EPG + pair_b[blk_bucket % N_PAIRS]).astype(I32)
        n_real = (pad_ends[-1:] // EXPERT_BLOCK).astype(I32)

        xs = _sc_scatter_rows(pay.reshape(PAY_PARTS * t, LANES), row_idx[:PAY_PARTS].reshape(-1),
                              PAY_PARTS * n_rows).reshape(PAY_PARTS, n_rows, LANES)
        ys = _experts(blk_ea, blk_eb, n_real, xs, w_g8, w_u8, w_d, d)
        tp = t // GATHER_PARTS
        ys_rows = ys.reshape(OUT_PARTS * n_rows, LANES)
        ytok = [_sc_gather_rows(ys_rows, row_idx[:OUT_PARTS, p * tp:(p + 1) * tp].reshape(-1)).reshape(OUT_PARTS, tp, LANES)
                for p in range(GATHER_PARTS)]
        xf, moe = x1, (ytok, g2)
    for p in range(GATHER_PARTS):
        xf = _residual(xf, moe[0][p], moe[1], seq, (p, GATHER_PARTS))
    return xf.reshape(batch, seq, d)
```

```python
import functools

import jax
import jax.numpy as jnp
from jax import lax
from jax.experimental import pallas as pl
from jax.experimental.pallas import tpu as pltpu
from jax.experimental.pallas import tpu_sc as plsc

F32 = jnp.float32
BF16 = jnp.bfloat16
U32 = jnp.uint32
I32 = jnp.int32
HIGHEST = lax.Precision.HIGHEST

HEAD_DIM = 64
N_HEADS = 8
N_KV = 2
ROPE_DIM = 16
ROPE_THETA = 500000.0
ATTN_BLOCK = 128
M_HEADS = 4
M_DIM = 128
CONV_K = 4
N_EXPERTS = 16
N_GROUPS = 4
EPG = 4
EPS = 1e-6

LANES = 128
SUBLANES = 8

TILE_INPROJ = 1024
TILE_ATTN = 1024
TILE_MLSTM = 2048
TILE_MERGE = 1024
TILE_RESIDUAL = 1024
TILE_ROW_INDEX = 8192
ADA_COLS = 3072
CHUNK = 128
CHUNKS_PER_STEP = 4
GATE_CHUNKS_PER_STEP = 4
N_PAIRS = 6
N_BUCKETS = N_GROUPS * N_PAIRS
BUCKET_ROWS = 32
EXPERT_BLOCK = 512
PAY_PARTS = 5
OUT_PARTS = 4
SC_WINDOW = 128
GATHER_SPLIT = (1, 3)
VMEM_LIMIT = 56 * 1024 * 1024


def _dot(a, b, precision=None):
    return jnp.dot(a, b, preferred_element_type=F32, precision=precision)


def _dot_nt(a, b):
    return lax.dot_general(a, b, (((1,), (1,)), ((), ())), preferred_element_type=F32)


def _sigmoid(x):
    return 1.0 / (1.0 + jnp.exp(-x))


def _log_sigmoid(x):
    return jnp.minimum(x, 0.0) - jnp.log1p(jnp.exp(-jnp.abs(x)))


def _pack_bf16_pairs(v, rounded=False):
    n = v.shape[1] // 2
    bits = lax.bitcast_convert_type(v if rounded else v.astype(BF16).astype(F32), U32)
    return bits[:, :n] | (bits[:, n:] >> 16)


def _unpack_bf16_pairs(w):
    hi = lax.bitcast_convert_type(w & jnp.uint32(0xFFFF0000), F32)
    lo = lax.bitcast_convert_type(w << 16, F32)
    return jnp.concatenate([hi, lo], axis=1)


def _params(*sem):
    return pltpu.CompilerParams(dimension_semantics=sem, vmem_limit_bytes=VMEM_LIMIT)


def _ada_kernel(c_ref, w_ref, b_ref, o_ref):
    c = c_ref[...]
    ca = c * _sigmoid(c)
    o_ref[0] = _dot(ca, w_ref[0], HIGHEST) + b_ref[0]


def _ada_mod(c_pad, ada_w, ada_b):
    depth, d, n = ada_w.shape
    tn = ADA_COLS
    return pl.pallas_call(
        _ada_kernel,
        grid=(depth, n // tn),
        in_specs=[
            pl.BlockSpec((SUBLANES, d), lambda l, j: (0, 0)),
            pl.BlockSpec((1, d, tn), lambda l, j: (l, 0, j)),
            pl.BlockSpec((1, 1, tn), lambda l, j: (l, 0, j)),
        ],
        out_specs=pl.BlockSpec((1, SUBLANES, tn), lambda l, j: (l, 0, j)),
        out_shape=jax.ShapeDtypeStruct((depth, SUBLANES, n), F32),
        compiler_params=_params("arbitrary", "arbitrary"),
        name="ada_mod",
    )(c_pad, ada_w, ada_b.reshape(depth, 1, n))


def _inproj_kernel(*refs, fuse_residual, n_carry=0):
    if n_carry:
        refs = refs[:10] + refs[10 + n_carry:]
    if fuse_residual:
        (x_ref, y_ref, g2_ref, sc_ref, sh_ref, nw_ref, wa_ref, wm_ref, wg_ref, wmg_ref,
         a_ref, m_ref, mg_ref, gr_ref, xo_ref, g_ref) = refs
        y = _unpack_bf16_pairs(jnp.concatenate([y_ref[c] for c in range(OUT_PARTS)], axis=1))
        x = x_ref[...] + g2_ref[...] * y
        xo_ref[...] = x
    else:
        (x_ref, sc_ref, sh_ref, nw_ref, wa_ref, wm_ref, wg_ref, wmg_ref,
         a_ref, m_ref, mg_ref, gr_ref, g_ref) = refs
        x = x_ref[...]
    ms = jnp.mean(x * x, axis=-1, keepdims=True)
    h = x * lax.rsqrt(ms + EPS) * (nw_ref[...] * (1.0 + sc_ref[...])) + sh_ref[...]
    hb = h.astype(BF16)
    a_ref[...] = _dot(hb, wa_ref[...]).astype(BF16)
    m_ref[...] = _dot(hb, wm_ref[...]).astype(BF16)
    mg_ref[...] = _sigmoid(_dot(hb, wmg_ref[...])).astype(BF16)
    g_ref[...] = _dot(hb, wg_ref[...])
    gr_ref[...] = g_ref[...].T[:SUBLANES, :]


def _inproj(x, moe, sc, sh, nw, wa, wm, wg, wmg, l, seq, part=None, carry=None):
    t, d = x.shape
    tm = min(TILE_INPROJ, seq)
    tpb = seq // tm
    lo, hi = part if part is not None else (0, t)
    off, steps = lo // tm, (hi - lo) // tm
    row = lambda i: (i + off, 0)
    bsel = lambda i: ((i + off) // tpb, 0, 0)
    wsel = lambda i: (l, 0, 0)
    once = pl.Buffered(1)
    na, nm, ng, nmg = wa.shape[2], wm.shape[2], wg.shape[2], wmg.shape[2]
    fuse = moe is not None
    moe_specs = [pl.BlockSpec((OUT_PARTS, tm, LANES), lambda i: (0, i, 0)), pl.BlockSpec((None, 1, d), bsel)]
    carry = list(carry) if carry is not None else []
    aliases = {10 + k: k for k in range(len(carry))}
    if part is not None:
        aliases[0] = 4
    return pl.pallas_call(
        functools.partial(_inproj_kernel, fuse_residual=fuse, n_carry=len(carry)),
        grid=(steps,),
        in_specs=[pl.BlockSpec((tm, d), row)] + (moe_specs if fuse else []) + [
            pl.BlockSpec((None, 1, d), bsel),
            pl.BlockSpec((None, 1, d), bsel),
            pl.BlockSpec((None, 1, d), wsel),
            pl.BlockSpec((None, d, na), wsel, pipeline_mode=once),
            pl.BlockSpec((None, d, nm), wsel, pipeline_mode=once),
            pl.BlockSpec((None, d, ng), wsel, pipeline_mode=once),
            pl.BlockSpec((None, d, nmg), wsel, pipeline_mode=once),
        ] + [pl.BlockSpec(memory_space=pl.ANY)] * len(carry),
        out_specs=[
            pl.BlockSpec((tm, na), row),
            pl.BlockSpec((tm, nm), row),
            pl.BlockSpec((tm, nmg), row),
            pl.BlockSpec((SUBLANES, tm), lambda i: (0, i + off)),
        ] + ([pl.BlockSpec((tm, d), row)] if fuse else []),
        out_shape=[
            jax.ShapeDtypeStruct((t, na), BF16),
            jax.ShapeDtypeStruct((t, nm), BF16),
            jax.ShapeDtypeStruct((t, nmg), BF16),
            jax.ShapeDtypeStruct((SUBLANES, t), F32),
        ] + ([jax.ShapeDtypeStruct((t, d), F32)] if fuse else []),
        scratch_shapes=[pltpu.VMEM((tm, ng), F32)],
        input_output_aliases=aliases,
        compiler_params=_params("arbitrary"),
        name="inproj",
    )(x, *(moe if fuse else ()), sc, sh, nw, wa, wm, wg, wmg, *carry)


def _rope(t, cos, sin):
    w = t.shape[1]
    reps = w // LANES
    cosw = jnp.concatenate([cos] * reps, axis=1) if reps > 1 else cos
    sinw = jnp.concatenate([sin] * reps, axis=1) if reps > 1 else sin
    lane = lax.broadcasted_iota(I32, t.shape, 1)
    half = ROPE_DIM // 2
    up = pltpu.roll(t, w - half, axis=1)
    dn = pltpu.roll(t, half, axis=1)
    partner = jnp.where((lane % ROPE_DIM) < half, up, dn)
    return t * cosw + partner * sinw


def _head_norm(t, bd, w):
    ms = _dot((t * t).astype(BF16), bd)
    return t * lax.rsqrt(ms + EPS) * w


def _attn_kernel(sink_ref, cur_ref, prev_ref, cos_ref, sin_ref, cosp_ref, sinp_ref,
                 qw_ref, kw_ref, bdq_ref, bdk_ref, o_ref):
    tq = cur_ref.shape[0]
    nj = tq // ATTN_BLOCK
    qw = N_HEADS * HEAD_DIM
    kw = N_KV * HEAD_DIM
    blk0 = pl.program_id(1) * nj

    cur = cur_ref[...]
    q = cur[:, :qw].astype(F32)
    kc = cur[:, qw:qw + kw].astype(F32)
    vc = cur[:, qw + kw:].astype(F32)
    prev = prev_ref[...]
    kp = prev[:, :kw].astype(F32)
    vp = prev[:, kw:].astype(F32)

    cos, sin = cos_ref[...], sin_ref[...]
    q = _rope(_head_norm(q, bdq_ref[...], qw_ref[...]), cos, sin)
    kc = _rope(_head_norm(kc, bdk_ref[...], kw_ref[...]), cos, sin)
    kp = _rope(_head_norm(kp, bdk_ref[...], kw_ref[...]), cosp_ref[...], sinp_ref[...])
    qb = q.astype(BF16)

    def both_halves(x2):
        swapped = pltpu.roll(x2, HEAD_DIM, axis=1)
        first = lax.broadcasted_iota(I32, x2.shape, 1) < HEAD_DIM
        return jnp.concatenate([jnp.where(first, x2, swapped), jnp.where(first, swapped, x2)], axis=1).astype(BF16)

    k_all = both_halves(jnp.concatenate([kp, kc], axis=0))
    v_all = both_halves(jnp.concatenate([vp, vc], axis=0))

    lane = lax.broadcasted_iota(I32, (ATTN_BLOCK, LANES), 1)
    lo = lane < HEAD_DIM
    zero = jnp.zeros((ATTN_BLOCK, LANES), BF16)
    g_heads = N_HEADS // N_KV
    ri = lax.broadcasted_iota(I32, (g_heads * ATTN_BLOCK, ATTN_BLOCK), 0) % ATTN_BLOCK
    ci = lax.broadcasted_iota(I32, (g_heads * ATTN_BLOCK, ATTN_BLOCK), 1)
    from_prev = ci > ri
    head_row = lax.broadcasted_iota(I32, (g_heads * ATTN_BLOCK, 1), 0) // ATTN_BLOCK
    ones_v = jnp.ones((2 * ATTN_BLOCK, LANES), BF16)

    tiles = [(j, g) for j in range(nj) for g in range(N_KV)]
    scores = {}
    for j, g in tiles:
        rows = slice(j * ATTN_BLOCK, (j + 1) * ATTN_BLOCK)
        band = slice(j * ATTN_BLOCK, (j + 2) * ATTN_BLOCK)
        qp0 = qb[rows, (2 * g) * LANES:(2 * g + 1) * LANES]
        qp1 = qb[rows, (2 * g + 1) * LANES:(2 * g + 2) * LANES]
        q4 = jnp.concatenate([jnp.where(lo, qp0, zero), jnp.where(lo, zero, qp0),
                              jnp.where(lo, qp1, zero), jnp.where(lo, zero, qp1)], axis=0)
        scores[j, g] = _dot_nt(q4, k_all[band, g * LANES:(g + 1) * LANES])

    probs, sink_term = {}, {}
    for j, g in tiles:
        s2 = scores[j, g]
        if j == 0:
            prev_ok = ci > ri + (1 - jnp.minimum(blk0, 1)) * ATTN_BLOCK
            s = jnp.where(prev_ok, s2[:, :ATTN_BLOCK], jnp.where(from_prev, -jnp.inf, s2[:, ATTN_BLOCK:]))
        else:
            s = jnp.where(from_prev, s2[:, :ATTN_BLOCK], s2[:, ATTN_BLOCK:])
        sink = jnp.full((g_heads * ATTN_BLOCK, 1), sink_ref[g_heads * g], F32)
        for r in range(1, g_heads):
            sink = jnp.where(head_row == r, sink_ref[g_heads * g + r], sink)
        m = jnp.maximum(jnp.max(s, axis=-1, keepdims=True), sink)
        p = jnp.exp(s - m)
        probs[j, g] = jnp.concatenate([jnp.where(from_prev, p, 0.0), jnp.where(from_prev, 0.0, p)],
                                      axis=1).astype(BF16)
        sink_term[j, g] = jnp.exp(sink - m)

    for j, g in tiles:
        rows = slice(j * ATTN_BLOCK, (j + 1) * ATTN_BLOCK)
        band = slice(j * ATTN_BLOCK, (j + 2) * ATTN_BLOCK)
        o8 = _dot(probs[j, g], jnp.concatenate([v_all[band, g * LANES:(g + 1) * LANES], ones_v], axis=1))
        o4 = o8[:, :LANES] / (o8[:, LANES:] + sink_term[j, g])
        b = ATTN_BLOCK
        o_ref[rows, (2 * g) * LANES:(2 * g + 1) * LANES] = jnp.where(lo, o4[0:b], o4[b:2 * b]).astype(BF16)
        o_ref[rows, (2 * g + 1) * LANES:(2 * g + 2) * LANES] = jnp.where(
            lo, o4[2 * b:3 * b], o4[3 * b:4 * b]).astype(BF16)


def _attention(a_in, cos_t, sin_t, sinks_l, qw, kw, bdq, bdk, batch, seq):
    t = a_in.shape[0]
    tq = min(TILE_ATTN, seq)
    nj = tq // ATTN_BLOCK
    tpb = seq // tq
    bpb = seq // ATTN_BLOCK
    qwid = N_HEADS * HEAD_DIM
    kvw = 2 * N_KV * HEAD_DIM
    cur = lambda b, i: (b * tpb + i, 0)
    prv = lambda b, i: (b * bpb + jnp.maximum(i * nj - 1, 0), qwid // kvw)
    prv0 = lambda b, i: (b * bpb + jnp.maximum(i * nj - 1, 0), 0)
    const = lambda b, i: (0, 0)
    return pl.pallas_call(
        _attn_kernel,
        grid=(batch, tpb),
        in_specs=[
            pl.BlockSpec(memory_space=pltpu.SMEM),
            pl.BlockSpec((tq, qwid + kvw), cur),
            pl.BlockSpec((ATTN_BLOCK, kvw), prv),
            pl.BlockSpec((tq, LANES), cur),
            pl.BlockSpec((tq, LANES), cur),
            pl.BlockSpec((ATTN_BLOCK, LANES), prv0),
            pl.BlockSpec((ATTN_BLOCK, LANES), prv0),
            pl.BlockSpec((1, qwid), const),
            pl.BlockSpec((1, kvw // 2), const),
            pl.BlockSpec((qwid, qwid), const),
            pl.BlockSpec((kvw // 2, kvw // 2), const),
        ],
        out_specs=pl.BlockSpec((tq, qwid), cur),
        out_shape=jax.ShapeDtypeStruct((t, qwid), BF16),
        compiler_params=_params("arbitrary", "arbitrary"),
        name="swa_attention",
    )(sinks_l, a_in, a_in, cos_t, sin_t, cos_t, sin_t, qw, kw, bdq, bdk)


def _mlstm_kernel(min_ref, gr_ref, cw_ref, cb_ref, bcol_ref, nw_ref,
                  hm_ref, ext_ref, q_ref, kt_ref, st_ref, mx_ref, ab_ref, bc_ref):
    tt = min_ref.shape[0]
    mw = M_HEADS * M_DIM
    nchunks = tt // CHUNK

    @pl.when(pl.program_id(1) == 0)
    def _():
        ext_ref[0:SUBLANES, :] = jnp.zeros((SUBLANES, 2 * mw), F32)
        st_ref[...] = jnp.zeros(st_ref.shape, F32)
        mx_ref[...] = jnp.zeros(mx_ref.shape, F32)

    def conv_block(cols):
        u = min_ref[:, cols].astype(F32)
        ext_ref[SUBLANES:SUBLANES + tt, cols] = u
        acc = cb_ref[:, cols] + cw_ref[CONV_K - 1:CONV_K, cols] * u
        for jj in range(CONV_K - 1):
            off = SUBLANES - (CONV_K - 1) + jj
            acc = acc + cw_ref[jj:jj + 1, cols] * ext_ref[off:off + tt, cols]
        ext_ref[0:SUBLANES, cols] = u[tt - SUBLANES:tt, :]
        return acc * _sigmoid(acc)

    def q_body(h, carry):
        cols = pl.ds(pl.multiple_of(h * M_DIM, M_DIM), M_DIM)
        q_ref[:, cols] = conv_block(cols).astype(BF16)
        return carry

    def k_body(h, carry):
        off = pl.multiple_of(h * M_DIM, M_DIM)
        act = conv_block(pl.ds(mw + off, M_DIM)) * (M_DIM ** -0.5)
        for j in range(nchunks):
            kt_ref[pl.ds(off, M_DIM), j * CHUNK:(j + 1) * CHUNK] = act[j * CHUNK:(j + 1) * CHUNK, :].T
        return carry

    lax.fori_loop(0, M_HEADS, q_body, 0)
    lax.fori_loop(0, M_HEADS, k_body, 0)

    ri = lax.broadcasted_iota(I32, (CHUNK, CHUNK), 0)
    ci = lax.broadcasted_iota(I32, (CHUNK, CHUNK), 1)
    causal = ci <= ri
    triu = jnp.where(ri <= ci, 1.0, 0.0).astype(BF16)
    ones_half = jnp.ones((CHUNK, M_DIM), BF16)
    mean_mat = jnp.full((M_DIM, M_DIM), 1.0 / M_DIM, BF16)
    sub = lax.broadcasted_iota(I32, (SUBLANES, CHUNK), 0)
    heads = range(M_HEADS)

    pad_rows = jnp.zeros((CHUNK - SUBLANES, CHUNK), F32)
    zero_rows = jnp.zeros((SUBLANES, CHUNK), F32)

    def gate_body(jg, carry):
        for u_ in range(GATE_CHUNKS_PER_STEP):
            rs = pl.ds(pl.multiple_of((jg * GATE_CHUNKS_PER_STEP + u_) * CHUNK, CHUNK), CHUNK)
            gr = gr_ref[:, rs] + bcol_ref[...]
            ls = _log_sigmoid(gr)
            ls1 = ls.astype(BF16).astype(F32)
            ls2 = (ls - ls1).astype(BF16).astype(F32)
            pieces = jnp.concatenate([ls1, ls2, ls - ls1 - ls2, zero_rows], axis=0).astype(BF16)
            sums = _dot(pieces, triu)
            br = sums[0:SUBLANES] + sums[SUBLANES:2 * SUBLANES] + sums[2 * SUBLANES:3 * SUBLANES]
            ab = jnp.where(sub < M_HEADS, gr - pltpu.roll(br, M_HEADS, axis=0), br)
            ab_ref[:, rs] = ab
            bc_ref[rs, :] = jnp.concatenate([ab, pad_rows], axis=0).T
        return carry

    lax.fori_loop(0, nchunks // GATE_CHUNKS_PER_STEP, gate_body, 0)

    def group_body(cg, carry):
        rows, ab = [], []
        for u_ in range(CHUNKS_PER_STEP):
            r0 = pl.multiple_of((cg * CHUNKS_PER_STEP + u_) * CHUNK, CHUNK)
            rows.append(pl.ds(r0, CHUNK))
            ab.append(ab_ref[:, rows[u_]])
        lanes = [(u_, h) for u_ in range(CHUNKS_PER_STEP) for h in heads]
        a_r = {(u_, h): ab[u_][h:h + 1, :] for u_, h in lanes}
        b_last = {(u_, h): ab[u_][M_HEADS + h:M_HEADS + h + 1, CHUNK - 1:CHUNK] for u_, h in lanes}

        m_prev, a_max, a_dec, s_in = {}, {}, {}, {}
        m_run = [mx_ref[h][0:1, 0:1] for h in heads]
        for k in lanes:
            u_, h = k
            m_prev[k] = m_run[h]
            a_max[k] = jnp.max(a_r[k], axis=-1, keepdims=True)
            m_loc = b_last[k] + a_max[k]
            m_new = jnp.maximum(b_last[k] + m_prev[k], m_loc)
            a_dec[k] = jnp.exp(b_last[k] + m_prev[k] - m_new)
            s_in[k] = jnp.exp(m_loc - m_new)
            m_run[h] = m_new
        for h in heads:
            mx_ref[h] = jnp.broadcast_to(m_run[h], (SUBLANES, LANES))

        q, v_ext, s_qk, kv = {}, {}, {}, {}
        for k in lanes:
            u_, h = k
            rs = rows[u_]
            q[k] = q_ref[rs, h * M_DIM:(h + 1) * M_DIM]
            kt = kt_ref[h * M_DIM:(h + 1) * M_DIM, rs]
            v = min_ref[rs, 2 * mw + h * M_DIM:2 * mw + (h + 1) * M_DIM]
            v_ext[k] = jnp.concatenate([v, ones_half], axis=1)
            s_qk[k] = _dot(q[k], kt.astype(BF16))
            e_r = jnp.exp(a_r[k] - a_max[k])
            kv[k] = _dot((kt * e_r).astype(BF16), v_ext[k])

        thr, qk = {}, {}
        for k in lanes:
            u_, h = k
            a_mat = jnp.where(causal, a_r[k], -jnp.inf)
            mu = jnp.maximum(jnp.max(a_mat, axis=-1, keepdims=True), m_prev[k])
            b_c = bc_ref[rows[u_], M_HEADS + h:M_HEADS + h + 1]
            thr[k] = jnp.broadcast_to(jnp.exp(-(b_c + mu)), (CHUNK, M_DIM))
            mu_b = jnp.broadcast_to(mu, (CHUNK, CHUNK))
            inter = jnp.exp(m_prev[k] - mu_b)
            qk[k] = jnp.concatenate([(s_qk[k] * jnp.exp(a_mat - mu_b)).astype(BF16),
                                     (q[k].astype(F32) * inter).astype(BF16)], axis=1)

        state = [st_ref[h] for h in heads]
        for k in lanes:
            u_, h = k
            hs = slice(h * M_DIM, (h + 1) * M_DIM)
            num = _dot(qk[k], jnp.concatenate([v_ext[k], state[h].astype(BF16)], axis=0))
            state[h] = a_dec[k] * state[h] + s_in[k] * kv[k]
            den = jnp.maximum(jnp.abs(num[:, M_DIM:]), thr[k])
            hh = num[:, :M_DIM] / den
            msq = _dot((hh * hh).astype(BF16), mean_mat)
            hn = hh * lax.rsqrt(msq + EPS) * nw_ref[:, hs]
            og = min_ref[rows[u_], 3 * mw + h * M_DIM:3 * mw + (h + 1) * M_DIM].astype(F32)
            hm_ref[rows[u_], hs] = (_sigmoid(og) * hn).astype(BF16)
        for h in heads:
            st_ref[h] = state[h]
        return carry

    lax.fori_loop(0, nchunks // CHUNKS_PER_STEP, group_body, 0)


def _mlstm(m_in, grow, conv_w, conv_b, bcol, nw, batch, seq):
    t = m_in.shape[0]
    tt = min(TILE_MLSTM, seq)
    tpb = seq // tt
    mw = M_HEADS * M_DIM
    cur = lambda b, i: (b * tpb + i, 0)
    const = lambda b, i: (0, 0)
    return pl.pallas_call(
        _mlstm_kernel,
        grid=(batch, tpb),
        in_specs=[
            pl.BlockSpec((tt, 4 * mw), cur),
            pl.BlockSpec((SUBLANES, tt), lambda b, i: (0, b * tpb + i)),
            pl.BlockSpec((CONV_K, 2 * mw), const),
            pl.BlockSpec((1, 2 * mw), const),
            pl.BlockSpec((SUBLANES, LANES), const),
            pl.BlockSpec((1, mw), const),
        ],
        out_specs=pl.BlockSpec((tt, mw), cur),
        out_shape=jax.ShapeDtypeStruct((t, mw), BF16),
        scratch_shapes=[
            pltpu.VMEM((tt + SUBLANES, 2 * mw), F32),
            pltpu.VMEM((tt, mw), BF16),
            pltpu.VMEM((mw, tt), F32),
            pltpu.VMEM((M_HEADS, M_DIM, 2 * M_DIM), F32),
            pltpu.VMEM((M_HEADS, SUBLANES, LANES), F32),
            pltpu.VMEM((SUBLANES, tt), F32),
            pltpu.VMEM((tt, LANES), F32),
        ],
        compiler_params=_params("arbitrary", "arbitrary"),
        name="mlstm",
    )(m_in, grow, conv_w, conv_b, bcol, nw)


def _merge_kernel(o_ref, hm_ref, mg_ref, x_ref, g1_ref, sc_ref, sh_ref, nw_ref,
                  wa_ref, wm_ref, wo_ref, rwh_ref, rwl_ref, rb_ref, tri_ref,
                  x1_ref, pay_ref, route_ref, cnt_ref, carry_ref):
    tm, d = x_ref.shape

    @pl.when(pl.program_id(0) == 0)
    def _():
        carry_ref[...] = jnp.zeros(carry_ref.shape, F32)

    ya = _dot(o_ref[...], wa_ref[...])
    yb = _dot(hm_ref[...], wm_ref[...])
    mg = mg_ref[...]
    merged = mg[:, :d].astype(F32) * ya + mg[:, d:].astype(F32) * yb
    x1 = x_ref[...] + g1_ref[...] * _dot(merged.astype(BF16), wo_ref[...])
    x1_ref[...] = x1

    ms = jnp.mean(x1 * x1, axis=-1, keepdims=True)
    h2 = x1 * lax.rsqrt(ms + EPS) * (nw_ref[...] * (1.0 + sc_ref[...])) + sh_ref[...]
    hi = h2.astype(BF16)
    hif = hi.astype(F32)
    lo = (h2 - hif).astype(BF16)
    r_hi = _dot_nt(rwh_ref[...], hi)
    r_lo = _dot_nt(rwl_ref[...], lo)
    sc_t = _sigmoid(r_hi[:N_EXPERTS] + r_hi[N_EXPERTS:] + r_lo)
    sel_t = sc_t + rb_ref[:, 0:1]

    def row(a, e):
        return a[e:e + 1, :]

    best = None
    gi = jnp.zeros((1, tm), I32)
    for g in range(N_GROUPS):
        r = [row(sel_t, EPG * g + i) for i in range(EPG)]
        gs = None
        for i in range(EPG):
            for j in range(i + 1, EPG):
                pr = r[i] + r[j]
                gs = pr if gs is None else jnp.maximum(gs, pr)
        if best is None:
            best = gs
        else:
            upd = gs > best
            gi = jnp.where(upd, g, gi)
            best = jnp.maximum(best, gs)

    def pick(a, i):
        out = row(a, i)
        for g in range(1, N_GROUPS):
            out = jnp.where(gi == g, row(a, EPG * g + i), out)
        return out

    v = [pick(sel_t, i) for i in range(EPG)]
    s = [pick(sc_t, i) for i in range(EPG)]

    def argmax4(vals):
        bv, bi = vals[0], jnp.zeros((1, tm), I32)
        for i in range(1, EPG):
            upd = vals[i] > bv
            bi = jnp.where(upd, i, bi)
            bv = jnp.maximum(bv, vals[i])
        return bi

    i1 = argmax4(v)
    i2 = argmax4([jnp.where(i1 == i, -jnp.inf, v[i]) for i in range(EPG)])
    ia = jnp.minimum(i1, i2)
    ib = jnp.maximum(i1, i2)
    pidx = jnp.where(ia == 0, ib - 1, jnp.where(ia == 1, jnp.where(ib == 3, 3, 5), 4))
    bucket = gi * N_PAIRS + pidx

    def by_index(vals, idx):
        out = vals[0]
        for i in range(1, EPG):
            out = jnp.where(idx == i, vals[i], out)
        return out

    swap = pidx == N_PAIRS - 1
    s_lo, s_hi = by_index(s, ia), by_index(s, ib)
    s_a, s_b = jnp.where(swap, s_hi, s_lo), jnp.where(swap, s_lo, s_hi)
    gate_a = s_a / (s_a + s_b)
    gate_b = s_b / (s_a + s_b)

    brow = lax.broadcasted_iota(I32, (BUCKET_ROWS, tm), 0)
    onehot = brow == bucket
    cums = _dot(jnp.where(onehot, 1.0, 0.0).astype(BF16), tri_ref[...])
    carry = carry_ref[...]
    rank = jnp.sum(jnp.where(onehot, carry[:, 0:1] + cums, 0.0), axis=0, keepdims=True) - 1.0
    new_carry = carry + cums[:, tm - 1:tm]
    carry_ref[...] = new_carry
    cnt_ref[...] = new_carry

    route_ref[...] = jnp.concatenate(
        [bucket.astype(F32), gate_a, gate_b, rank, jnp.zeros((SUBLANES - 4, tm), F32)], axis=0)

    half = d // 2
    packed = _pack_bf16_pairs(hif, rounded=True)
    for cpart in range(half // LANES):
        pay_ref[cpart] = packed[:, cpart * LANES:(cpart + 1) * LANES]
    gates_t = jnp.concatenate([gate_a, gate_b, jnp.zeros((LANES - 2, tm), F32)], axis=0)
    pay_ref[half // LANES] = lax.bitcast_convert_type(gates_t.T, U32)


def _merge(o_attn, hm, mg, x, g1, sc2, sh2, nw, wa, wm, wo, rwh, rwl, rb, tri, l, seq):
    t, d = x.shape
    tm = tri.shape[0]
    tpb = seq // tm
    row = lambda i: (i, 0)
    bsel = lambda i: (i // tpb, 0, 0)
    wsel = lambda i: (l, 0, 0)
    const = lambda i: (0, 0)
    hw = o_attn.shape[1]
    return pl.pallas_call(
        _merge_kernel,
        grid=(t // tm,),
        in_specs=[
            pl.BlockSpec((tm, hw), row),
            pl.BlockSpec((tm, hw), row),
            pl.BlockSpec((tm, 2 * d), row),
            pl.BlockSpec((tm, d), row),
            pl.BlockSpec((None, 1, d), bsel),
            pl.BlockSpec((None, 1, d), bsel),
            pl.BlockSpec((None, 1, d), bsel),
            pl.BlockSpec((None, 1, d), wsel),
            pl.BlockSpec((None, hw, d), wsel),
            pl.BlockSpec((None, hw, d), wsel),
            pl.BlockSpec((None, d, d), wsel),
            pl.BlockSpec((2 * N_EXPERTS, d), const),
            pl.BlockSpec((N_EXPERTS, d), const),
            pl.BlockSpec((N_EXPERTS, LANES), const),
            pl.BlockSpec((tm, tm), const),
        ],
        out_specs=[
            pl.BlockSpec((tm, d), row),
            pl.BlockSpec((PAY_PARTS, tm, LANES), lambda i: (0, i, 0)),
            pl.BlockSpec((SUBLANES, tm), lambda i: (0, i)),
            pl.BlockSpec((BUCKET_ROWS, LANES), const),
        ],
        out_shape=[
            jax.ShapeDtypeStruct((t, d), F32),
            jax.ShapeDtypeStruct((PAY_PARTS, t, LANES), U32),
            jax.ShapeDtypeStruct((SUBLANES, t), F32),
            jax.ShapeDtypeStruct((BUCKET_ROWS, LANES), F32),
        ],
        scratch_shapes=[pltpu.VMEM((BUCKET_ROWS, LANES), F32)],
        compiler_params=_params("arbitrary"),
        name="merge_router",
    )(o_attn, hm, mg, x, g1, sc2, sh2, nw, wa, wm, wo, rwh, rwl, rb, tri)


def _sc_mesh():
    return plsc.VectorSubcoreMesh(core_axis_name="core", subcore_axis_name="subcore")


def _sc_scatter_rows(rows, dest, n_out):
    n, w = rows.shape

    @pl.kernel(out_type=jax.ShapeDtypeStruct((n_out, w), rows.dtype), mesh=_sc_mesh(), scratch_types=[])
    def scatter(x_hbm, i_hbm, o_hbm):
        def body(x_vmem, i_vmem):
            pltpu.sync_copy(x_vmem, o_hbm.at[i_vmem.at[0]])

        pltpu.emit_pipeline(
            body,
            grid=(n // SC_WINDOW,),
            in_specs=[pl.BlockSpec((SC_WINDOW, w), lambda i: (i, 0)),
                      pl.BlockSpec((1, SC_WINDOW), lambda i: (0, i))],
            out_specs=[],
            core_axis_name=("core", "subcore"),
            dimension_semantics=(pltpu.PARALLEL,),
        )(x_hbm, i_hbm)

    return scatter(rows, dest.reshape(1, n))


def _sc_gather_rows(src, idx):
    n = idx.shape[0]
    w = src.shape[1]

    @pl.kernel(out_type=jax.ShapeDtypeStruct((n, w), src.dtype), mesh=_sc_mesh(), scratch_types=[])
    def gather(x_hbm, i_hbm, o_hbm):
        def body(i_vmem, o_vmem):
            pltpu.sync_copy(x_hbm.at[i_vmem.at[0]], o_vmem)

        pltpu.emit_pipeline(
            body,
            grid=(n // SC_WINDOW,),
            in_specs=[pl.BlockSpec((1, SC_WINDOW), lambda i: (0, i))],
            out_specs=[pl.BlockSpec((SC_WINDOW, w), lambda i: (i, 0))],
            core_axis_name=("core", "subcore"),
            dimension_semantics=(pltpu.PARALLEL,),
        )(i_hbm, o_hbm)

    return gather(src, idx.reshape(1, n))


def _row_index_kernel(ps_ref, route_ref, o_ref, *, n_rows):
    bucket = route_ref[0:1, :].astype(I32)
    start = jnp.zeros(bucket.shape, I32)
    for b in range(N_BUCKETS):
        start = jnp.where(bucket == b, ps_ref[b], start)
    dest = start + route_ref[3:4, :].astype(I32)
    part = lax.broadcasted_iota(I32, o_ref.shape, 0)
    o_ref[...] = part * n_rows + dest


def _row_index(pad_starts, route, n_rows):
    t = route.shape[1]
    tm = min(TILE_ROW_INDEX, t)
    return pl.pallas_call(
        functools.partial(_row_index_kernel, n_rows=n_rows),
        grid=(t // tm,),
        in_specs=[pl.BlockSpec(memory_space=pltpu.SMEM), pl.BlockSpec((SUBLANES, tm), lambda i: (0, i))],
        out_specs=pl.BlockSpec((SUBLANES, tm), lambda i: (0, i)),
        out_shape=jax.ShapeDtypeStruct((SUBLANES, t), I32),
        compiler_params=_params("arbitrary"),
        name="row_index",
    )(pad_starts, route)


def _residual_kernel(x_ref, y_ref, g2_ref, o_ref):
    y = _unpack_bf16_pairs(jnp.concatenate([y_ref[c] for c in range(OUT_PARTS)], axis=1))
    o_ref[...] = x_ref[...] + g2_ref[...] * y


def _residual(x1, ytok, g2, seq, part):
    t, d = x1.shape
    tm = min(TILE_RESIDUAL, seq)
    tpb = seq // tm
    lo, hi = part
    off, steps = lo // tm, (hi - lo) // tm
    return pl.pallas_call(
        _residual_kernel,
        grid=(steps,),
        in_specs=[
            pl.BlockSpec((tm, d), lambda i: (i + off, 0)),
            pl.BlockSpec((OUT_PARTS, tm, LANES), lambda i: (0, i, 0)),
            pl.BlockSpec((None, 1, d), lambda i: ((i + off) // tpb, 0, 0)),
        ],
        out_specs=pl.BlockSpec((tm, d), lambda i: (i + off, 0)),
        out_shape=jax.ShapeDtypeStruct((t, d), F32),
        input_output_aliases={0: 0},
        compiler_params=_params("arbitrary"),
        name="residual",
    )(x1, ytok, g2)


def _expert_kernel(ea_ref, eb_ref, nr_ref, xs_ref, wga_ref, wua_ref, wda_ref, wgb_ref, wub_ref, wdb_ref, ys_ref,
                   ga_ref, ua_ref, gb_ref, ub_ref, dab_ref):
    j = pl.program_id(0)
    nr = nr_ref[0]
    prev = jnp.maximum(j - 1, 0)
    f = wda_ref.shape[0]

    @pl.when((j == 0) | (ea_ref[j] != ea_ref[prev]))
    def _():
        ga_ref[...] = wga_ref[...].astype(BF16)
        ua_ref[...] = wua_ref[...].astype(BF16)
        dab_ref[0:f, :] = wda_ref[...].astype(BF16)

    @pl.when((j == 0) | (eb_ref[j] != eb_ref[prev]))
    def _():
        gb_ref[...] = wgb_ref[...].astype(BF16)
        ub_ref[...] = wub_ref[...].astype(BF16)
        dab_ref[f:2 * f, :] = wdb_ref[...].astype(BF16)

    @pl.when(j < nr)
    def _():
        x = _unpack_bf16_pairs(jnp.concatenate([xs_ref[c] for c in range(PAY_PARTS - 1)], axis=1)).astype(BF16)
        gl = lax.bitcast_convert_type(xs_ref[PAY_PARTS - 1], F32)

        def gated_act(wg_ref, wu_ref, gate):
            gte = _dot(x, wg_ref[...])
            return (gte * _sigmoid(gte) * _dot(x, wu_ref[...]) * gate).astype(BF16)

        acts = jnp.concatenate([gated_act(ga_ref, ua_ref, gl[:, 0:1]), gated_act(gb_ref, ub_ref, gl[:, 1:2])], axis=1)
        y = _pack_bf16_pairs(_dot(acts, dab_ref[...]))
        for c in range(OUT_PARTS):
            ys_ref[c] = y[:, c * LANES:(c + 1) * LANES]

    @pl.when(j >= nr)
    def _():
        ys_ref[...] = jnp.zeros(ys_ref.shape, U32)


def _experts(blk_ea, blk_eb, n_real, xs, wg, wu, wd, d):
    n_rows = xs.shape[1]
    nblk = n_rows // EXPERT_BLOCK
    f = wg.shape[2]
    grid_spec = pltpu.PrefetchScalarGridSpec(
        num_scalar_prefetch=3,
        grid=(nblk,),
        in_specs=[
            pl.BlockSpec((PAY_PARTS, EXPERT_BLOCK, LANES), lambda j, ea, eb, nr: (0, j, 0)),
            pl.BlockSpec((None, d, f), lambda j, ea, eb, nr: (ea[j], 0, 0)),
            pl.BlockSpec((None, d, f), lambda j, ea, eb, nr: (ea[j], 0, 0)),
            pl.BlockSpec((None, f, d), lambda j, ea, eb, nr: (ea[j], 0, 0)),
            pl.BlockSpec((None, d, f), lambda j, ea, eb, nr: (eb[j], 0, 0)),
            pl.BlockSpec((None, d, f), lambda j, ea, eb, nr: (eb[j], 0, 0)),
            pl.BlockSpec((None, f, d), lambda j, ea, eb, nr: (eb[j], 0, 0)),
        ],
        out_specs=pl.BlockSpec((OUT_PARTS, EXPERT_BLOCK, LANES), lambda j, ea, eb, nr: (0, j, 0)),
        scratch_shapes=[pltpu.VMEM((d, f), BF16)] * 4 + [pltpu.VMEM((2 * f, d), BF16)],
    )
    return pl.pallas_call(
        _expert_kernel,
        grid_spec=grid_spec,
        out_shape=jax.ShapeDtypeStruct((OUT_PARTS, n_rows, LANES), U32),
        compiler_params=_params("arbitrary"),
        name="experts",
    )(blk_ea, blk_eb, n_real, xs, wg, wu, wd, wg, wu, wd)


_PAIR_A = (0, 0, 0, 1, 2, 2)
_PAIR_B = (1, 2, 3, 3, 3, 1)


def kernel(x, c, positions, ada_w, ada_b, norm_mix_w, norm_ffn_w, w_in, b_igate, b_fgate, q_norm_w, k_norm_w,
           sinks, conv_w, conv_b, mlstm_norm_w, w_attn_up, w_mlstm_up, w_out, router_w, router_bias,
           w_gate, w_up, w_down):
    batch, seq, d = x.shape
    depth = w_in.shape[0]
    t = batch * seq
    qw = N_HEADS * HEAD_DIM
    kvw = N_KV * HEAD_DIM
    mw = M_HEADS * M_DIM

    o = 0
    cols = {}
    for name, wdt in (("q", qw), ("k", kvw), ("v", kvw), ("mqk", 2 * mw), ("mv", mw), ("mi", M_HEADS),
                      ("mf", M_HEADS), ("mo", mw), ("ga", d), ("gb", d)):
        cols[name] = (o, o + wdt)
        o += wdt

    def wc(name, lo=0, hi=None):
        s, e = cols[name]
        return w_in[:, :, s + lo:(s + hi if hi is not None else e)]

    w_a = jnp.concatenate([wc("q"), wc("k"), wc("v")], axis=2).astype(BF16)
    w_m = jnp.concatenate([wc("mqk"), wc("mv"), wc("mo")], axis=2).astype(BF16)
    w_g = jnp.concatenate([wc("mi"), wc("mf"), jnp.zeros((depth, d, LANES - 2 * M_HEADS), F32)], axis=2).astype(BF16)
    w_mg = jnp.concatenate([wc("ga"), wc("gb")], axis=2).astype(BF16)
    w_au = w_attn_up.astype(BF16)
    w_mu = w_mlstm_up.astype(BF16)
    w_o = w_out.astype(BF16)
    n_e = w_gate.shape[1]
    w_g8 = w_gate.reshape(depth * n_e, d, -1)
    w_u8 = w_up.reshape(depth * n_e, d, -1)
    w_d = w_down.reshape(depth * n_e, -1, d)

    rw_t = router_w.astype(F32).T
    rw_top = rw_t.astype(BF16)
    rw_hi = jnp.concatenate([rw_top, (rw_t - rw_top.astype(F32)).astype(BF16)], axis=0)
    rw_lo = rw_top
    rb = jnp.broadcast_to(router_bias.astype(F32)[:, None], (n_e, LANES))

    qn_w = jnp.tile(q_norm_w * (HEAD_DIM ** -0.5), (1, N_HEADS)).reshape(depth, 1, qw)
    kn_w = jnp.tile(k_norm_w, (1, N_KV)).reshape(depth, 1, kvw)
    seg = jnp.arange(qw) // HEAD_DIM
    bdq = jnp.where(seg[:, None] == seg[None, :], 1.0 / HEAD_DIM, 0.0).astype(BF16)
    bdk = bdq[:kvw, :kvw]

    inv_freq = ROPE_THETA ** (-(jnp.arange(0, ROPE_DIM, 2, dtype=F32) / ROPE_DIM))
    ang = positions.astype(F32).reshape(1, t) * inv_freq[:, None]
    cos8, sin8 = jnp.cos(ang).T, jnp.sin(ang).T
    pad1 = jnp.ones((t, HEAD_DIM - ROPE_DIM), F32)
    pad0 = jnp.zeros((t, HEAD_DIM - ROPE_DIM), F32)
    cos_t = jnp.tile(jnp.concatenate([cos8, cos8, pad1], axis=1), (1, LANES // HEAD_DIM))
    sin_t = jnp.tile(jnp.concatenate([-sin8, sin8, pad0], axis=1), (1, LANES // HEAD_DIM))

    gate_bias = jnp.concatenate([b_igate, b_fgate], axis=1).astype(F32)
    bcol = jnp.broadcast_to(gate_bias[:, :, None], (depth, 2 * M_HEADS, LANES))

    tm_merge = min(TILE_MERGE, seq)
    ii = jnp.arange(tm_merge)
    tri = (ii[:, None] <= ii[None, :]).astype(BF16)

    n_blk = (t + N_BUCKETS * (EXPERT_BLOCK - 1)) // EXPERT_BLOCK + 1
    n_rows = n_blk * EXPERT_BLOCK
    pair_a = jnp.asarray(_PAIR_A, I32)
    pair_b = jnp.asarray(_PAIR_B, I32)

    c_pad = jnp.zeros((SUBLANES, d), F32).at[:batch].set(c)
    mod = _ada_mod(c_pad, ada_w, ada_b)[:, :batch]

    xf = x.reshape(t, d)
    moe = None
    cuts = [t * sum(GATHER_SPLIT[:k]) // sum(GATHER_SPLIT) for k in range(len(GATHER_SPLIT) + 1)]
    ranges = list(zip(cuts[:-1], cuts[1:]))
    for l in range(depth):
        sh1, sc1, g1, sh2, sc2, g2 = [m.reshape(batch, 1, d) for m in jnp.split(mod[l], 6, axis=-1)]

        nmw = norm_mix_w.reshape(depth, 1, d)
        if moe is None:
            a_in, m_in, mg, grow = _inproj(xf, None, sc1, sh1, nmw, w_a, w_m, w_g, w_mg, l, seq)
        else:
            outs = None
            for p, rng in enumerate(ranges):
                outs = _inproj(xf, (moe[0][p], moe[1]), sc1, sh1, nmw, w_a, w_m, w_g, w_mg, l, seq,
                               part=rng, carry=None if outs is None else outs[:4])
                xf = outs[4]
            a_in, m_in, mg, grow = outs[:4]
        o_attn = _attention(a_in, cos_t, sin_t, sinks[l], qn_w[l], kn_w[l], bdq, bdk, batch, seq)
        hm = _mlstm(m_in, grow, conv_w[l], conv_b[l].reshape(1, -1), bcol[l], mlstm_norm_w[l].reshape(1, mw),
                    batch, seq)
        x1, pay, route, cnt = _merge(o_attn, hm, mg, xf, g1, sc2, sh2, norm_ffn_w.reshape(depth, 1, d),
                                     w_au, w_mu, w_o, rw_hi, rw_lo, rb, tri, l, seq)

        counts = cnt[:N_BUCKETS, 0].astype(I32)
        padded = (counts + EXPERT_BLOCK - 1) // EXPERT_BLOCK * EXPERT_BLOCK
        pad_ends = jnp.cumsum(padded)
        pad_starts = pad_ends - padded
        row_idx = _row_index(jnp.concatenate([pad_starts, jnp.zeros((BUCKET_ROWS - N_BUCKETS,), I32)]), route, n_rows)
        blk_start = jnp.arange(n_blk, dtype=I32) * EXPERT_BLOCK
        blk_bucket = jnp.minimum(jnp.sum((pad_ends[None, :] <= blk_start[:, None]).astype(I32), axis=1), N_BUCKETS - 1)
        grp = blk_bucket // N_PAIRS
        blk_ea = (l * n_e + grp * EPG + pair_a[blk_bucket % N_PAIRS]).astype(I32)
        blk_eb = (l * n_e + grp * EPG + pair_b[blk_bucket % N_PAIRS]).astype(I32)
        n_real = (pad_ends[-1:] // EXPERT_BLOCK).astype(I32)

        xs = _sc_scatter_rows(pay.reshape(PAY_PARTS * t, LANES), row_idx[:PAY_PARTS].reshape(-1),
                              PAY_PARTS * n_rows).reshape(PAY_PARTS, n_rows, LANES)
        ys = _experts(blk_ea, blk_eb, n_real, xs, w_g8, w_u8, w_d, d)
        ys_rows = ys.reshape(OUT_PARTS * n_rows, LANES)
        ytok = [_sc_gather_rows(ys_rows, row_idx[:OUT_PARTS, lo:hi].reshape(-1)).reshape(OUT_PARTS, hi - lo, LANES)
                for lo, hi in ranges]
        xf, moe = x1, (ytok, g2)
    for p, rng in enumerate(ranges):
        xf = _residual(xf, moe[0][p], moe[1], seq, rng)
    return xf.reshape(batch, seq, d)
```
